```python
import math
import functools
import jax
import jax.numpy as jnp
from jax import lax
import numpy as np

D_MODEL = 4096
BATCH = 8
SEQ = 4096
DEPTH = 2

CTX_LEN = 256
GRID_W = 64
EPS = 1e-6

DN_HEADS = 16
DN_HEAD_DIM = 128
DN_WIDTH = DN_HEADS * DN_HEAD_DIM
DN_CONV = 4
DN_CHUNK = 64

LRU_WIDTH = 2048
LRU_BLOCKS = 16
LRU_BLOCK_DIM = LRU_WIDTH // LRU_BLOCKS
LRU_CONV = 4
LRU_C = 8.0

SC_WIDTH = D_MODEL
SC_CONV = 3

OFF_LRU = 3 * DN_WIDTH
OFF_BETA = OFF_LRU + LRU_WIDTH
OFF_ALPHA = OFF_BETA + 2 * DN_HEADS
AB_STATE = OFF_ALPHA + 2 * DN_HEADS
OFF_LRU_GATE = AB_STATE + DN_WIDTH
AB_IN = OFF_LRU_GATE + LRU_WIDTH
AB_OUT = DN_WIDTH + LRU_WIDTH

kernel_name = 'hybrid_deltanet_rglru_shortconv_dit'

F32 = jnp.float32


def _rms_norm(x, w):
    xf = x.astype(F32)
    y = xf * lax.rsqrt(jnp.mean(xf * xf, axis=-1, keepdims=True) + EPS)
    return (y * w.astype(F32)).astype(x.dtype)


def _l2norm(x):
    return x * lax.rsqrt(jnp.sum(x * x, axis=-1, keepdims=True) + EPS)


def _rev(t, axis, on):
    return jnp.flip(t, axis=axis) if on else t


def _dw_conv(x, w, b=None):
    k = w.shape[0]
    left = k // 2
    y = lax.conv_general_dilated(
        x, w[:, None, :].astype(x.dtype), window_strides=(1,), padding=[(left, k - 1 - left)],
        dimension_numbers=('NWC', 'WIO', 'NWC'), feature_group_count=x.shape[-1])
    if b is not None:
        y = y + b.astype(x.dtype)
    return y


def _to_col_major(t, rows):
    bsz, length, ch = t.shape
    return t.reshape(bsz, rows, GRID_W, ch).swapaxes(1, 2).reshape(bsz, length, ch)


def _to_raster(t, rows):
    bsz, length, ch = t.shape
    return t.reshape(bsz, GRID_W, rows, ch).swapaxes(1, 2).reshape(bsz, length, ch)


def _short_conv_heads(p, conv_w):
    bsz, length, width = p.shape
    y = jax.nn.silu(_dw_conv(p, conv_w)).astype(F32)
    return y.reshape(bsz, length, width // DN_WIDTH, DN_HEADS, DN_HEAD_DIM).transpose(2, 0, 3, 1, 4)


def _decay_gates(p_beta, p_alpha, a_log, dt_bias):
    bsz, length, _ = p_beta.shape
    beta = jax.nn.sigmoid(p_beta.astype(F32)).reshape(bsz, length, 2, DN_HEADS)
    alpha = p_alpha.astype(F32).reshape(bsz, length, 2, DN_HEADS)
    g = -jnp.exp(a_log.astype(F32)) * jax.nn.softplus(alpha + dt_bias.astype(F32))
    return beta.transpose(2, 0, 3, 1), g.transpose(2, 0, 3, 1)


def _delta_chunks(k, v, beta, g):
    bsz, nh, length, dk = k.shape
    n = length // DN_CHUNK
    kc = k.reshape(bsz, nh, n, DN_CHUNK, dk)
    vc = v.reshape(bsz, nh, n, DN_CHUNK, v.shape[-1])
    bc = beta.reshape(bsz, nh, n, DN_CHUNK)
    g_cum = jnp.cumsum(g.reshape(bsz, nh, n, DN_CHUNK), axis=-1)
    idx = jnp.arange(DN_CHUNK)
    diff = g_cum[..., :, None] - g_cum[..., None, :]
    decay = jnp.exp(jnp.where(idx[:, None] >= idx[None, :], diff, -jnp.inf))
    kk = jnp.einsum('bhntd,bhnsd->bhnts', kc, kc)
    lower = jnp.where(idx[:, None] > idx[None, :], bc[..., :, None] * kk * decay, 0.0)
    t_mat = lower + jnp.eye(DN_CHUNK, dtype=lower.dtype)
    solve = functools.partial(lax.linalg.triangular_solve, left_side=True, lower=True, unit_diagonal=True)
    w = solve(t_mat, (bc * jnp.exp(g_cum))[..., None] * kc)
    u = solve(t_mat, bc[..., None] * vc)
    k_end = kc * jnp.exp(g_cum[..., -1:] - g_cum)[..., None]
    g_end = jnp.exp(g_cum[..., -1])
    return g_cum, decay, kc, w, u, k_end, g_end


def _delta_step(s, w_c, u_c, ke_c, ge_c):
    u_c = u_c - jnp.einsum('bhtd,bhdv->bhtv', w_c, s)
    s_new = ge_c[..., None, None] * s + jnp.einsum('bhtd,bhtv->bhdv', ke_c, u_c)
    return u_c, s_new


def _delta_final_state(k, v, beta, g, s0):
    _, _, _, w, u, k_end, g_end = _delta_chunks(k, v, beta, g)

    def step(s, xs):
        _, s = _delta_step(s, *xs)
        return s, None

    s, _ = lax.scan(step, s0, tuple(jnp.moveaxis(t, 2, 0) for t in (w, u, k_end, g_end)))
    return s


def _delta_outputs(q, k, v, beta, g, s0):
    g_cum, decay, kc, w, u, k_end, g_end = _delta_chunks(k, v, beta, g)
    bsz, nh, length, dk = q.shape
    qc = q.reshape(bsz, nh, -1, DN_CHUNK, dk)
    a_qk = jnp.einsum('bhntd,bhnsd->bhnts', qc, kc) * decay
    q_g = qc * jnp.exp(g_cum)[..., None]

    def step(s, xs):
        w_c, u_c, ke_c, ge_c, aqk_c, qg_c = xs
        u_c, s_new = _delta_step(s, w_c, u_c, ke_c, ge_c)
        o = jnp.einsum('bhtd,bhdv->bhtv', qg_c, s) + jnp.einsum('bhts,bhsv->bhtv', aqk_c, u_c)
        return s_new, o

    xs = tuple(jnp.moveaxis(t, 2, 0) for t in (w, u, k_end, g_end, a_qk, q_g))
    _, o = lax.scan(step, s0, xs)
    return jnp.moveaxis(o, 0, 2).reshape(bsz, nh, length, -1)


def _rglru_gates(xc, w_r, b_r, w_i, b_i, lam):
    blocks = xc.reshape(xc.shape[0], xc.shape[1], LRU_BLOCKS, LRU_BLOCK_DIM)
    r = jax.nn.sigmoid(jnp.einsum('blnd,nde->blne', blocks, w_r).reshape(xc.shape) + b_r)
    i = jax.nn.sigmoid(jnp.einsum('blnd,nde->blne', blocks, w_i).reshape(xc.shape) + b_i)
    log_a = -LRU_C * r * jax.nn.softplus(-lam)
    b = jnp.sqrt(-jnp.expm1(2.0 * log_a)) * (i * xc)
    return log_a, b


def _linear_scan(a, b, h0):
    b = b.at[:, 0].add(a[:, 0] * h0)

    def combine(prev, nxt):
        return prev[0] * nxt[0], nxt[0] * prev[1] + nxt[1]

    _, h = lax.associative_scan(combine, (a, b), axis=1)
    return h


def _linear_final_state(log_a, b):
    suffix = lax.cumsum(log_a, axis=1, reverse=True) - log_a
    return jnp.sum(jnp.exp(suffix) * b, axis=1)


def _ab_mixer(h, hc, w_in, qkv_conv, a_log, dt_bias, dn_norm, lru_conv_w, lru_conv_b,
              lru_w_r, lru_b_r, lru_w_i, lru_b_i, lru_lambda, w_out):
    bsz, length, _ = h.shape
    rows = length // GRID_W
    sh = DN_WIDTH
    proj = h @ w_in
    proj_c = hc @ w_in[:, sh:AB_STATE]

    qkv = _short_conv_heads(proj[..., :OFF_LRU], qkv_conv)
    q = _l2norm(qkv[0]) * DN_HEAD_DIM ** -0.5
    k = _l2norm(qkv[1])
    v = qkv[2]
    beta, g = _decay_gates(proj[..., OFF_BETA:OFF_ALPHA], proj[..., OFF_ALPHA:AB_STATE], a_log, dt_bias)
    kv_c = _short_conv_heads(proj_c[..., :OFF_LRU - sh], qkv_conv[:, sh:])
    k_c = _l2norm(kv_c[0])
    v_c = kv_c[1]
    beta_c, g_c = _decay_gates(proj_c[..., OFF_BETA - sh:OFF_ALPHA - sh], proj_c[..., OFF_ALPHA - sh:],
                               a_log, dt_bias)
    s0 = jnp.zeros((bsz, DN_HEADS, DN_HEAD_DIM, DN_HEAD_DIM), F32)
    o_dn = jnp.zeros_like(v)
    for d in range(2):
        s_ctx = _delta_final_state(_rev(k_c, 2, d), _rev(v_c, 2, d), _rev(beta_c[d], 2, d),
                                   _rev(g_c[d], 2, d), s0)
        o = _delta_outputs(_rev(q, 2, d), _rev(k, 2, d), _rev(v, 2, d), _rev(beta[d], 2, d),
                           _rev(g[d], 2, d), s_ctx)
        o_dn = o_dn + _rev(o, 2, d)
    o_dn = _rms_norm(o_dn, dn_norm).transpose(0, 2, 1, 3).reshape(bsz, length, DN_WIDTH)

    xc = _dw_conv(_to_col_major(proj[..., OFF_LRU:OFF_BETA], rows), lru_conv_w, lru_conv_b).astype(F32)
    xc_c = _dw_conv(proj_c[..., OFF_LRU - sh:OFF_BETA - sh], lru_conv_w, lru_conv_b).astype(F32)
    h_lru = jnp.zeros_like(xc)
    for d in range(2):
        la_c, b_c = _rglru_gates(_rev(xc_c, 1, d), lru_w_r[d], lru_b_r[d], lru_w_i[d], lru_b_i[d], lru_lambda[d])
        h0 = _linear_final_state(la_c, b_c)
        la, bb = _rglru_gates(_rev(xc, 1, d), lru_w_r[d], lru_b_r[d], lru_w_i[d], lru_b_i[d], lru_lambda[d])
        h_lru = h_lru + _rev(_linear_scan(jnp.exp(la), bb, h0), 1, d)
    h_lru = _to_raster(h_lru, rows)

    y = jnp.concatenate([o_dn * jax.nn.silu(proj[..., AB_STATE:OFF_LRU_GATE]),
                         h_lru * jax.nn.silu(proj[..., OFF_LRU_GATE:])], axis=-1)
    return y @ w_out


def _sc_mixer(h, w_in, conv_w, w_out):
    bsz, length, _ = h.shape
    rows = length // GRID_W
    b_g, c_g, x_in, gate = jnp.split(h @ w_in, 4, axis=-1)
    z = (c_g * x_in).reshape(bsz * rows, GRID_W, SC_WIDTH)
    z = _dw_conv(z, conv_w).reshape(bsz, length, SC_WIDTH)
    return (b_g * z * jax.nn.silu(gate)) @ w_out


def _fwd_setup_inputs(seed: int = 0) -> dict:
    key = jax.random.key(seed)
    ks = iter(jax.random.split(key, 32))
    ne, no = (DEPTH + 1) // 2, DEPTH // 2
    dm = D_MODEL

    def nrm(shape, scale):
        return jax.random.normal(next(ks), shape, F32) * scale

    a_pow = jax.random.uniform(next(ks), (ne, 2, LRU_WIDTH), F32, 0.9, 0.999)
    a_base = a_pow ** (1.0 / LRU_C)
    lru_lambda = jnp.log(a_base) - jnp.log1p(-a_base)
    a_mag = jax.random.uniform(next(ks), (ne, 2, DN_HEADS), F32, 1.0, 16.0)
    dt = jnp.exp(jax.random.uniform(next(ks), (ne, 2, DN_HEADS), F32, math.log(1e-3), math.log(1e-1)))
    dt_bias = dt + jnp.log(-jnp.expm1(-dt))
    return {
        'x': nrm((BATCH, SEQ, dm), 1.0),
        'c': nrm((BATCH, dm), 1.0),
        'ctx': nrm((BATCH, CTX_LEN, dm), 1.0),
        'c_ctx': nrm((dm,), 1.0),
        'mod_w': nrm((DEPTH, dm, 3 * dm), dm ** -0.5),
        'mod_b': nrm((DEPTH, 3 * dm), 0.02),
        'norm_w': 1.0 + nrm((DEPTH, dm), 0.02),
        'ab_w_in': nrm((ne, dm, AB_IN), dm ** -0.5),
        'ab_qkv_conv': nrm((ne, DN_CONV, 3 * DN_WIDTH), DN_CONV ** -0.5),
        'ab_a_log': jnp.log(a_mag),
        'ab_dt_bias': dt_bias,
        'ab_dn_norm': 1.0 + nrm((ne, DN_HEAD_DIM), 0.02),
        'ab_lru_conv_w': nrm((ne, LRU_CONV, LRU_WIDTH), LRU_CONV ** -0.5),
        'ab_lru_conv_b': nrm((ne, LRU_WIDTH), 0.02),
        'ab_lru_w_r': nrm((ne, 2, LRU_BLOCKS, LRU_BLOCK_DIM, LRU_BLOCK_DIM), LRU_BLOCK_DIM ** -0.5),
        'ab_lru_b_r': nrm((ne, 2, LRU_WIDTH), 0.02),
        'ab_lru_w_i': nrm((ne, 2, LRU_BLOCKS, LRU_BLOCK_DIM, LRU_BLOCK_DIM), LRU_BLOCK_DIM ** -0.5),
        'ab_lru_b_i': nrm((ne, 2, LRU_WIDTH), 0.02),
        'ab_lru_lambda': lru_lambda,
        'ab_w_out': nrm((ne, AB_OUT, dm), AB_OUT ** -0.5),
        'sc_w_in': nrm((no, dm, 4 * SC_WIDTH), dm ** -0.5),
        'sc_conv': nrm((no, SC_CONV, SC_WIDTH), SC_CONV ** -0.5),
        'sc_w_out': nrm((no, SC_WIDTH, dm), SC_WIDTH ** -0.5),
        'final_norm_w': 1.0 + nrm((dm,), 0.02),
    }


def _fwd_reference(x, c, ctx, c_ctx, mod_w, mod_b, norm_w, ab_w_in, ab_qkv_conv, ab_a_log, ab_dt_bias,
              ab_dn_norm, ab_lru_conv_w, ab_lru_conv_b, ab_lru_w_r, ab_lru_b_r, ab_lru_w_i, ab_lru_b_i,
              ab_lru_lambda, ab_w_out, sc_w_in, sc_conv, sc_w_out, final_norm_w):
    dm = D_MODEL
    silu_c = jax.nn.silu(c)
    silu_cc = jax.nn.silu(c_ctx)
    for layer in range(DEPTH):
        j = layer // 2
        shift, scale, gate = jnp.split(silu_c @ mod_w[layer] + mod_b[layer], 3, axis=-1)
        hn = _rms_norm(x, norm_w[layer]) * (1.0 + scale[:, None, :]) + shift[:, None, :]
        if layer % 2 == 0:
            shift_c, scale_c = jnp.split(silu_cc @ mod_w[layer][:, :2 * dm] + mod_b[layer][:2 * dm], 2)
            hc = _rms_norm(ctx, norm_w[layer]) * (1.0 + scale_c) + shift_c
            y = _ab_mixer(hn, hc, ab_w_in[j], ab_qkv_conv[j], ab_a_log[j], ab_dt_bias[j], ab_dn_norm[j],
                          ab_lru_conv_w[j], ab_lru_conv_b[j], ab_lru_w_r[j], ab_lru_b_r[j], ab_lru_w_i[j],
                          ab_lru_b_i[j], ab_lru_lambda[j], ab_w_out[j])
        else:
            y = _sc_mixer(hn, sc_w_in[j], sc_conv[j], sc_w_out[j])
        x = x + (gate[:, None, :] * y).astype(x.dtype)
    return _rms_norm(x, final_norm_w)


import jax as _jax
import jax.numpy as _jnp

TWIN_FORMAT = 'train_step'
FWD_PARAMS = ['x', 'c', 'ctx', 'c_ctx', 'mod_w', 'mod_b', 'norm_w', 'ab_w_in', 'ab_qkv_conv', 'ab_a_log', 'ab_dt_bias', 'ab_dn_norm', 'ab_lru_conv_w', 'ab_lru_conv_b', 'ab_lru_w_r', 'ab_lru_b_r', 'ab_lru_w_i', 'ab_lru_b_i', 'ab_lru_lambda', 'ab_w_out', 'sc_w_in', 'sc_conv', 'sc_w_out', 'final_norm_w']
TWIN_WEIGHTS = ['c_ctx', 'mod_w', 'mod_b', 'norm_w', 'ab_w_in', 'ab_qkv_conv', 'ab_a_log', 'ab_dt_bias', 'ab_dn_norm', 'ab_lru_conv_w', 'ab_lru_conv_b', 'ab_lru_w_r', 'ab_lru_b_r', 'ab_lru_w_i', 'ab_lru_b_i', 'ab_lru_lambda', 'ab_w_out', 'sc_w_in', 'sc_conv', 'sc_w_out', 'final_norm_w']
TWIN_DIFF_INPUT = 'x'
TWIN_INPUTS = ['x', 'c', 'ctx', 'c_ctx', 'mod_w', 'mod_b', 'norm_w', 'ab_w_in', 'ab_qkv_conv', 'ab_a_log', 'ab_dt_bias', 'ab_dn_norm', 'ab_lru_conv_w', 'ab_lru_conv_b', 'ab_lru_w_r', 'ab_lru_b_r', 'ab_lru_w_i', 'ab_lru_b_i', 'ab_lru_lambda', 'ab_w_out', 'sc_w_in', 'sc_conv', 'sc_w_out', 'final_norm_w', 'loss_target', 'm_c_ctx', 'm_mod_w', 'm_mod_b', 'm_norm_w', 'm_ab_w_in', 'm_ab_qkv_conv', 'm_ab_a_log', 'm_ab_dt_bias', 'm_ab_dn_norm', 'm_ab_lru_conv_w', 'm_ab_lru_conv_b', 'm_ab_lru_w_r', 'm_ab_lru_b_r', 'm_ab_lru_w_i', 'm_ab_lru_b_i', 'm_ab_lru_lambda', 'm_ab_w_out', 'm_sc_w_in', 'm_sc_conv', 'm_sc_w_out', 'm_final_norm_w', 'v_c_ctx', 'v_mod_w', 'v_mod_b', 'v_norm_w', 'v_ab_w_in', 'v_ab_qkv_conv', 'v_ab_a_log', 'v_ab_dt_bias', 'v_ab_dn_norm', 'v_ab_lru_conv_w', 'v_ab_lru_conv_b', 'v_ab_lru_w_r', 'v_ab_lru_b_r', 'v_ab_lru_w_i', 'v_ab_lru_b_i', 'v_ab_lru_lambda', 'v_ab_w_out', 'v_sc_w_in', 'v_sc_conv', 'v_sc_w_out', 'v_final_norm_w']
TWIN_OUTPUTS = ['loss', 'grad_x', 'grad_c_ctx', 'grad_mod_w', 'grad_mod_b', 'grad_norm_w', 'grad_ab_w_in', 'grad_ab_qkv_conv', 'grad_ab_a_log', 'grad_ab_dt_bias', 'grad_ab_dn_norm', 'grad_ab_lru_conv_w', 'grad_ab_lru_conv_b', 'grad_ab_lru_w_r', 'grad_ab_lru_b_r', 'grad_ab_lru_w_i', 'grad_ab_lru_b_i', 'grad_ab_lru_lambda', 'grad_ab_w_out', 'grad_sc_w_in', 'grad_sc_conv', 'grad_sc_w_out', 'grad_final_norm_w', 'delta_c_ctx', 'delta_mod_w', 'delta_mod_b', 'delta_norm_w', 'delta_ab_w_in', 'delta_ab_qkv_conv', 'delta_ab_a_log', 'delta_ab_dt_bias', 'delta_ab_dn_norm', 'delta_ab_lru_conv_w', 'delta_ab_lru_conv_b', 'delta_ab_lru_w_r', 'delta_ab_lru_b_r', 'delta_ab_lru_w_i', 'delta_ab_lru_b_i', 'delta_ab_lru_lambda', 'delta_ab_w_out', 'delta_sc_w_in', 'delta_sc_conv', 'delta_sc_w_out', 'delta_final_norm_w', 'new_m_c_ctx', 'new_m_mod_w', 'new_m_mod_b', 'new_m_norm_w', 'new_m_ab_w_in', 'new_m_ab_qkv_conv', 'new_m_ab_a_log', 'new_m_ab_dt_bias', 'new_m_ab_dn_norm', 'new_m_ab_lru_conv_w', 'new_m_ab_lru_conv_b', 'new_m_ab_lru_w_r', 'new_m_ab_lru_b_r', 'new_m_ab_lru_w_i', 'new_m_ab_lru_b_i', 'new_m_ab_lru_lambda', 'new_m_ab_w_out', 'new_m_sc_w_in', 'new_m_sc_conv', 'new_m_sc_w_out', 'new_m_final_norm_w', 'new_v_c_ctx', 'new_v_mod_w', 'new_v_mod_b', 'new_v_norm_w', 'new_v_ab_w_in', 'new_v_ab_qkv_conv', 'new_v_ab_a_log', 'new_v_ab_dt_bias', 'new_v_ab_dn_norm', 'new_v_ab_lru_conv_w', 'new_v_ab_lru_conv_b', 'new_v_ab_lru_w_r', 'new_v_ab_lru_b_r', 'new_v_ab_lru_w_i', 'new_v_ab_lru_b_i', 'new_v_ab_lru_lambda', 'new_v_ab_w_out', 'new_v_sc_w_in', 'new_v_sc_conv', 'new_v_sc_w_out', 'new_v_final_norm_w']
TWIN_LEAF_KINDS = {'loss': 'loss', 'grad_x': 'grad_x', 'grad_c_ctx': 'grad_w', 'grad_mod_w': 'grad_w', 'grad_mod_b': 'grad_w', 'grad_norm_w': 'grad_w', 'grad_ab_w_in': 'grad_w', 'grad_ab_qkv_conv': 'grad_w', 'grad_ab_a_log': 'grad_w', 'grad_ab_dt_bias': 'grad_w', 'grad_ab_dn_norm': 'grad_w', 'grad_ab_lru_conv_w': 'grad_w', 'grad_ab_lru_conv_b': 'grad_w', 'grad_ab_lru_w_r': 'grad_w', 'grad_ab_lru_b_r': 'grad_w', 'grad_ab_lru_w_i': 'grad_w', 'grad_ab_lru_b_i': 'grad_w', 'grad_ab_lru_lambda': 'grad_w', 'grad_ab_w_out': 'grad_w', 'grad_sc_w_in': 'grad_w', 'grad_sc_conv': 'grad_w', 'grad_sc_w_out': 'grad_w', 'grad_final_norm_w': 'grad_w', 'delta_c_ctx': 'delta_w', 'delta_mod_w': 'delta_w', 'delta_mod_b': 'delta_w', 'delta_norm_w': 'delta_w', 'delta_ab_w_in': 'delta_w', 'delta_ab_qkv_conv': 'delta_w', 'delta_ab_a_log': 'delta_w', 'delta_ab_dt_bias': 'delta_w', 'delta_ab_dn_norm': 'delta_w', 'delta_ab_lru_conv_w': 'delta_w', 'delta_ab_lru_conv_b': 'delta_w', 'delta_ab_lru_w_r': 'delta_w', 'delta_ab_lru_b_r': 'delta_w', 'delta_ab_lru_w_i': 'delta_w', 'delta_ab_lru_b_i': 'delta_w', 'delta_ab_lru_lambda': 'delta_w', 'delta_ab_w_out': 'delta_w', 'delta_sc_w_in': 'delta_w', 'delta_sc_conv': 'delta_w', 'delta_sc_w_out': 'delta_w', 'delta_final_norm_w': 'delta_w', 'new_m_c_ctx': 'new_m', 'new_m_mod_w': 'new_m', 'new_m_mod_b': 'new_m', 'new_m_norm_w': 'new_m', 'new_m_ab_w_in': 'new_m', 'new_m_ab_qkv_conv': 'new_m', 'new_m_ab_a_log': 'new_m', 'new_m_ab_dt_bias': 'new_m', 'new_m_ab_dn_norm': 'new_m', 'new_m_ab_lru_conv_w': 'new_m', 'new_m_ab_lru_conv_b': 'new_m', 'new_m_ab_lru_w_r': 'new_m', 'new_m_ab_lru_b_r': 'new_m', 'new_m_ab_lru_w_i': 'new_m', 'new_m_ab_lru_b_i': 'new_m', 'new_m_ab_lru_lambda': 'new_m', 'new_m_ab_w_out': 'new_m', 'new_m_sc_w_in': 'new_m', 'new_m_sc_conv': 'new_m', 'new_m_sc_w_out': 'new_m', 'new_m_final_norm_w': 'new_m', 'new_v_c_ctx': 'new_v', 'new_v_mod_w': 'new_v', 'new_v_mod_b': 'new_v', 'new_v_norm_w': 'new_v', 'new_v_ab_w_in': 'new_v', 'new_v_ab_qkv_conv': 'new_v', 'new_v_ab_a_log': 'new_v', 'new_v_ab_dt_bias': 'new_v', 'new_v_ab_dn_norm': 'new_v', 'new_v_ab_lru_conv_w': 'new_v', 'new_v_ab_lru_conv_b': 'new_v', 'new_v_ab_lru_w_r': 'new_v', 'new_v_ab_lru_b_r': 'new_v', 'new_v_ab_lru_w_i': 'new_v', 'new_v_ab_lru_b_i': 'new_v', 'new_v_ab_lru_lambda': 'new_v', 'new_v_ab_w_out': 'new_v', 'new_v_sc_w_in': 'new_v', 'new_v_sc_conv': 'new_v', 'new_v_sc_w_out': 'new_v', 'new_v_final_norm_w': 'new_v'}


def _forward(args):
    return _fwd_reference(*[args[k] for k in FWD_PARAMS])


def _output_shape():
    out = _jax.eval_shape(lambda: _forward(_fwd_setup_inputs(0)))
    return out.shape, out.dtype

N_MICROBATCH = 1
ADAM_LR = 0.001
ADAM_B1 = 0.9
ADAM_B2 = 0.999
ADAM_EPS = 1e-08
ADAM_WD = 0.01
ADAM_STEP = 10
PER_EXAMPLE_BATCH_AXIS = {'x': 0, 'c': 0, 'ctx': 0, 'loss_target': 0}
SHARED_INPUTS = []
_WEIGHT_DTYPES = {'c_ctx': _jnp.float32, 'mod_w': _jnp.float32, 'mod_b': _jnp.float32, 'norm_w': _jnp.float32, 'ab_w_in': _jnp.float32, 'ab_qkv_conv': _jnp.float32, 'ab_a_log': _jnp.float32, 'ab_dt_bias': _jnp.float32, 'ab_dn_norm': _jnp.float32, 'ab_lru_conv_w': _jnp.float32, 'ab_lru_conv_b': _jnp.float32, 'ab_lru_w_r': _jnp.float32, 'ab_lru_b_r': _jnp.float32, 'ab_lru_w_i': _jnp.float32, 'ab_lru_b_i': _jnp.float32, 'ab_lru_lambda': _jnp.float32, 'ab_w_out': _jnp.float32, 'sc_w_in': _jnp.float32, 'sc_conv': _jnp.float32, 'sc_w_out': _jnp.float32, 'final_norm_w': _jnp.float32}
MOMENT_SCALE = {'c_ctx': 2.128056e-02, 'mod_w': 4.702388e-02, 'mod_b': 8.084272e-02, 'norm_w': 4.102604e-02, 'ab_w_in': 4.403351e-02, 'ab_qkv_conv': 9.379001e-03, 'ab_a_log': 2.975703e-02, 'ab_dt_bias': 2.926313e-02, 'ab_dn_norm': 4.772559e-02, 'ab_lru_conv_w': 8.589727e-02, 'ab_lru_conv_b': 1.338726e-01, 'ab_lru_w_r': 4.999041e-03, 'ab_lru_b_r': 6.329987e-03, 'ab_lru_w_i': 1.140870e-02, 'ab_lru_b_i': 1.733481e-02, 'ab_lru_lambda': 1.755257e-02, 'ab_w_out': 5.907321e-02, 'sc_w_in': 2.092714e-02, 'sc_conv': 2.065784e-02, 'sc_w_out': 2.063308e-02, 'final_norm_w': 8.997592e+00}


def _to_microbatches(a, axis):
    t = _jnp.moveaxis(a, axis, 0)
    t = t.reshape((N_MICROBATCH, t.shape[0] // N_MICROBATCH) + t.shape[1:])
    return _jnp.moveaxis(t, 1, axis + 1)


def setup_inputs(seed: int = 0) -> dict:
    inp = _fwd_setup_inputs(seed)
    key = _jax.random.fold_in(_jax.random.key(seed), 7919)
    shape, _ = _output_shape()
    out = dict(inp)
    out["loss_target"] = _jax.random.normal(_jax.random.fold_in(key, 0), shape, _jnp.float32)
    for i, name in enumerate(TWIN_WEIGHTS):
        w = inp[name].astype(_jnp.float32)
        if MOMENT_SCALE is None:
            s = _jnp.sqrt(_jnp.mean(_jnp.square(w)) + 1e-30)
        else:
            s = MOMENT_SCALE[name]
        km, kv = _jax.random.split(_jax.random.fold_in(key, i + 1))
        out[name] = w
        out["m_" + name] = s * _jax.random.normal(km, w.shape, _jnp.float32)
        out["v_" + name] = (s * s) * _jax.random.uniform(kv, w.shape, _jnp.float32, 0.5, 1.5)
    if N_MICROBATCH > 1:
        for name, axis in PER_EXAMPLE_BATCH_AXIS.items():
            out[name] = _to_microbatches(out[name], axis)
    return {'x': out['x'], 'c': out['c'], 'ctx': out['ctx'], 'c_ctx': out['c_ctx'], 'mod_w': out['mod_w'], 'mod_b': out['mod_b'], 'norm_w': out['norm_w'], 'ab_w_in': out['ab_w_in'], 'ab_qkv_conv': out['ab_qkv_conv'], 'ab_a_log': out['ab_a_log'], 'ab_dt_bias': out['ab_dt_bias'], 'ab_dn_norm': out['ab_dn_norm'], 'ab_lru_conv_w': out['ab_lru_conv_w'], 'ab_lru_conv_b': out['ab_lru_conv_b'], 'ab_lru_w_r': out['ab_lru_w_r'], 'ab_lru_b_r': out['ab_lru_b_r'], 'ab_lru_w_i': out['ab_lru_w_i'], 'ab_lru_b_i': out['ab_lru_b_i'], 'ab_lru_lambda': out['ab_lru_lambda'], 'ab_w_out': out['ab_w_out'], 'sc_w_in': out['sc_w_in'], 'sc_conv': out['sc_conv'], 'sc_w_out': out['sc_w_out'], 'final_norm_w': out['final_norm_w'], 'loss_target': out['loss_target'], 'm_c_ctx': out['m_c_ctx'], 'm_mod_w': out['m_mod_w'], 'm_mod_b': out['m_mod_b'], 'm_norm_w': out['m_norm_w'], 'm_ab_w_in': out['m_ab_w_in'], 'm_ab_qkv_conv': out['m_ab_qkv_conv'], 'm_ab_a_log': out['m_ab_a_log'], 'm_ab_dt_bias': out['m_ab_dt_bias'], 'm_ab_dn_norm': out['m_ab_dn_norm'], 'm_ab_lru_conv_w': out['m_ab_lru_conv_w'], 'm_ab_lru_conv_b': out['m_ab_lru_conv_b'], 'm_ab_lru_w_r': out['m_ab_lru_w_r'], 'm_ab_lru_b_r': out['m_ab_lru_b_r'], 'm_ab_lru_w_i': out['m_ab_lru_w_i'], 'm_ab_lru_b_i': out['m_ab_lru_b_i'], 'm_ab_lru_lambda': out['m_ab_lru_lambda'], 'm_ab_w_out': out['m_ab_w_out'], 'm_sc_w_in': out['m_sc_w_in'], 'm_sc_conv': out['m_sc_conv'], 'm_sc_w_out': out['m_sc_w_out'], 'm_final_norm_w': out['m_final_norm_w'], 'v_c_ctx': out['v_c_ctx'], 'v_mod_w': out['v_mod_w'], 'v_mod_b': out['v_mod_b'], 'v_norm_w': out['v_norm_w'], 'v_ab_w_in': out['v_ab_w_in'], 'v_ab_qkv_conv': out['v_ab_qkv_conv'], 'v_ab_a_log': out['v_ab_a_log'], 'v_ab_dt_bias': out['v_ab_dt_bias'], 'v_ab_dn_norm': out['v_ab_dn_norm'], 'v_ab_lru_conv_w': out['v_ab_lru_conv_w'], 'v_ab_lru_conv_b': out['v_ab_lru_conv_b'], 'v_ab_lru_w_r': out['v_ab_lru_w_r'], 'v_ab_lru_b_r': out['v_ab_lru_b_r'], 'v_ab_lru_w_i': out['v_ab_lru_w_i'], 'v_ab_lru_b_i': out['v_ab_lru_b_i'], 'v_ab_lru_lambda': out['v_ab_lru_lambda'], 'v_ab_w_out': out['v_ab_w_out'], 'v_sc_w_in': out['v_sc_w_in'], 'v_sc_conv': out['v_sc_conv'], 'v_sc_w_out': out['v_sc_w_out'], 'v_final_norm_w': out['v_final_norm_w']}


def _loss(weights, diff, rest, loss_target):
    with _jax.named_scope("forward"):
        args = {**rest, TWIN_DIFF_INPUT: diff, **{k: w.astype(_WEIGHT_DTYPES[k]) for k, w in weights.items()}}
        y = _forward(args)
    with _jax.named_scope("loss_head"):
        err = _jnp.square(y.astype(_jnp.float32) - loss_target)
        return 0.5 * _jnp.sum(_jnp.mean(err, axis=-1)) if err.ndim else 0.5 * err


def _adamw(w, g, m, v):
    m = ADAM_B1 * m + (1.0 - ADAM_B1) * g
    v = ADAM_B2 * v + (1.0 - ADAM_B2) * _jnp.square(g)
    m_hat = m / (1.0 - ADAM_B1 ** ADAM_STEP)
    v_hat = v / (1.0 - ADAM_B2 ** ADAM_STEP)
    delta = -ADAM_LR * (m_hat / (_jnp.sqrt(v_hat) + ADAM_EPS) + ADAM_WD * w)
    return delta, m, v


def reference(x, c, ctx, c_ctx, mod_w, mod_b, norm_w, ab_w_in, ab_qkv_conv, ab_a_log, ab_dt_bias, ab_dn_norm, ab_lru_conv_w, ab_lru_conv_b, ab_lru_w_r, ab_lru_b_r, ab_lru_w_i, ab_lru_b_i, ab_lru_lambda, ab_w_out, sc_w_in, sc_conv, sc_w_out, final_norm_w, loss_target, m_c_ctx, m_mod_w, m_mod_b, m_norm_w, m_ab_w_in, m_ab_qkv_conv, m_ab_a_log, m_ab_dt_bias, m_ab_dn_norm, m_ab_lru_conv_w, m_ab_lru_conv_b, m_ab_lru_w_r, m_ab_lru_b_r, m_ab_lru_w_i, m_ab_lru_b_i, m_ab_lru_lambda, m_ab_w_out, m_sc_w_in, m_sc_conv, m_sc_w_out, m_final_norm_w, v_c_ctx, v_mod_w, v_mod_b, v_norm_w, v_ab_w_in, v_ab_qkv_conv, v_ab_a_log, v_ab_dt_bias, v_ab_dn_norm, v_ab_lru_conv_w, v_ab_lru_conv_b, v_ab_lru_w_r, v_ab_lru_b_r, v_ab_lru_w_i, v_ab_lru_b_i, v_ab_lru_lambda, v_ab_w_out, v_sc_w_in, v_sc_conv, v_sc_w_out, v_final_norm_w):
    given = dict(x=x, c=c, ctx=ctx, c_ctx=c_ctx, mod_w=mod_w, mod_b=mod_b, norm_w=norm_w, ab_w_in=ab_w_in, ab_qkv_conv=ab_qkv_conv, ab_a_log=ab_a_log, ab_dt_bias=ab_dt_bias, ab_dn_norm=ab_dn_norm, ab_lru_conv_w=ab_lru_conv_w, ab_lru_conv_b=ab_lru_conv_b, ab_lru_w_r=ab_lru_w_r, ab_lru_b_r=ab_lru_b_r, ab_lru_w_i=ab_lru_w_i, ab_lru_b_i=ab_lru_b_i, ab_lru_lambda=ab_lru_lambda, ab_w_out=ab_w_out, sc_w_in=sc_w_in, sc_conv=sc_conv, sc_w_out=sc_w_out, final_norm_w=final_norm_w, loss_target=loss_target, m_c_ctx=m_c_ctx, m_mod_w=m_mod_w, m_mod_b=m_mod_b, m_norm_w=m_norm_w, m_ab_w_in=m_ab_w_in, m_ab_qkv_conv=m_ab_qkv_conv, m_ab_a_log=m_ab_a_log, m_ab_dt_bias=m_ab_dt_bias, m_ab_dn_norm=m_ab_dn_norm, m_ab_lru_conv_w=m_ab_lru_conv_w, m_ab_lru_conv_b=m_ab_lru_conv_b, m_ab_lru_w_r=m_ab_lru_w_r, m_ab_lru_b_r=m_ab_lru_b_r, m_ab_lru_w_i=m_ab_lru_w_i, m_ab_lru_b_i=m_ab_lru_b_i, m_ab_lru_lambda=m_ab_lru_lambda, m_ab_w_out=m_ab_w_out, m_sc_w_in=m_sc_w_in, m_sc_conv=m_sc_conv, m_sc_w_out=m_sc_w_out, m_final_norm_w=m_final_norm_w, v_c_ctx=v_c_ctx, v_mod_w=v_mod_w, v_mod_b=v_mod_b, v_norm_w=v_norm_w, v_ab_w_in=v_ab_w_in, v_ab_qkv_conv=v_ab_qkv_conv, v_ab_a_log=v_ab_a_log, v_ab_dt_bias=v_ab_dt_bias, v_ab_dn_norm=v_ab_dn_norm, v_ab_lru_conv_w=v_ab_lru_conv_w, v_ab_lru_conv_b=v_ab_lru_conv_b, v_ab_lru_w_r=v_ab_lru_w_r, v_ab_lru_b_r=v_ab_lru_b_r, v_ab_lru_w_i=v_ab_lru_w_i, v_ab_lru_b_i=v_ab_lru_b_i, v_ab_lru_lambda=v_ab_lru_lambda, v_ab_w_out=v_ab_w_out, v_sc_w_in=v_sc_w_in, v_sc_conv=v_sc_conv, v_sc_w_out=v_sc_w_out, v_final_norm_w=v_final_norm_w)
    weights = {n: given[n] for n in TWIN_WEIGHTS}
    shared = {n: given[n] for n in SHARED_INPUTS}
    per_example = {n: given[n] for n in ['x', 'c', 'ctx']}
    grad_fn = _jax.value_and_grad(_loss, argnums=(0, 1))

    def one_microbatch(ex, loss_target):
        ex = dict(ex)
        diff = ex.pop(TWIN_DIFF_INPUT)
        return grad_fn(weights, diff, {**shared, **ex}, loss_target)

    if N_MICROBATCH == 1:
        loss, (grad_w, grad_x) = one_microbatch(per_example, given["loss_target"])
    else:
        def body(carry, xs):
            loss_sum, grad_sum = carry
            l_k, (gw_k, gx_k) = one_microbatch(xs[0], xs[1])
            with _jax.named_scope("update"):
                return (loss_sum + l_k, _jax.tree.map(_jnp.add, grad_sum, gw_k)), gx_k

        init = (_jnp.zeros((), _jnp.float32), _jax.tree.map(_jnp.zeros_like, weights))
        (loss, grad_w), grad_x = _jax.lax.scan(body, init, (per_example, given["loss_target"]))
    with _jax.named_scope("update"):
        delta_w, new_m, new_v = {}, {}, {}
        for n in TWIN_WEIGHTS:
            delta_w[n], new_m[n], new_v[n] = _adamw(weights[n], grad_w[n], given["m_" + n], given["v_" + n])
    return (loss, grad_x, *[grad_w[n] for n in TWIN_WEIGHTS], *[delta_w[n] for n in TWIN_WEIGHTS],
            *[new_m[n] for n in TWIN_WEIGHTS], *[new_v[n] for n in TWIN_WEIGHTS])
```

```python
import functools

import jax
import jax.numpy as jnp
from jax import lax
from jax.experimental import pallas as pl
from jax.experimental.pallas import tpu as pltpu

F32 = jnp.float32
BF16 = jnp.bfloat16

NDEV = 8
D_MODEL = 4096
SEQ = 4096
CTX_LEN = 256
GRID_W = 64
EPS = 1e-6
DN_HEADS = 16
DN_HEAD_DIM = 128
DN_CONV = 4
DN_CHUNK = 64
LRU_WIDTH = 2048
LRU_BLOCKS = 16
LRU_CONV = 4
LRU_C = 8.0
SC_CONV = 3
ADAM_LR = 0.001
ADAM_B1 = 0.9
ADAM_B2 = 0.999
ADAM_EPS = 1e-08
ADAM_WD = 0.01
ADAM_STEP = 10

ROW_TILE = 256
HEADS_PER_STEP = 4
LANE = 128
VMEM_LIMIT = 56 * 1024 * 1024


def _dims():
    d = D_MODEL
    dnw = DN_HEADS * DN_HEAD_DIM
    lw = LRU_WIDTH
    off_lru = 3 * dnw
    off_beta = off_lru + lw
    ab_state = off_beta + 4 * DN_HEADS
    ab_in = ab_state + dnw + lw
    return dict(d=d, dnw=dnw, lw=lw, off_lru=off_lru, off_beta=off_beta, ab_state=ab_state, ab_in=ab_in,
                nmain=3 * dnw + lw + dnw + lw, r=CTX_LEN + SEQ, lbd=lw // LRU_BLOCKS)


def _div(n, pref, mult):
    best = None
    for t in range(mult, min(n, pref) + 1, mult):
        if n % t == 0:
            best = t
    return best if best is not None else n


def _cp(sem, vmem=VMEM_LIMIT):
    return pltpu.CompilerParams(dimension_semantics=sem, vmem_limit_bytes=vmem)


def _silu(z):
    return z * jax.nn.sigmoid(z)


def _dsilu(z):
    s = jax.nn.sigmoid(z)
    return s * (1.0 + z * (1.0 - s))


def _softplus(z):
    return jnp.maximum(z, 0.0) + jnp.log1p(jnp.exp(-jnp.abs(z)))


def _neg_expm1(z):
    series = -z * (1.0 + z * (0.5 + z * (1.0 / 6.0 + z * (1.0 / 24.0))))
    return jnp.where(z > -0.05, series, 1.0 - jnp.exp(z))


def _me():
    return 4 * lax.axis_index("x") + 2 * lax.axis_index("y") + lax.axis_index("c")


def _exchange(srcs, name, scatter):
    n = len(srcs)

    def body(*refs):
        src_refs, out_refs = refs[:n], refs[n:2 * n]
        send_sems, recv_sems, local_sems = refs[2 * n:]
        x, y, c = lax.axis_index("x"), lax.axis_index("y"), lax.axis_index("c")
        me = 4 * x + 2 * y + c
        copies = []
        for i in range(n):
            mine = src_refs[i].at[me] if scatter else src_refs[i]
            lc = pltpu.make_async_copy(mine, out_refs[i].at[me], local_sems.at[i])
            lc.start()
            copies.append(lc)
            for k in range(1, NDEV):
                px = 1 - x if (k >> 2) & 1 else x
                py = 1 - y if (k >> 1) & 1 else y
                pc = 1 - c if k & 1 else c
                peer = 4 * px + 2 * py + pc
                src = src_refs[i].at[peer] if scatter else src_refs[i]
                cp = pltpu.make_async_remote_copy(
                    src_ref=src, dst_ref=out_refs[i].at[me], send_sem=send_sems.at[i, k - 1],
                    recv_sem=recv_sems.at[i, k - 1], device_id=(px, py, pc), device_id_type=pl.DeviceIdType.MESH)
                cp.start()
                copies.append(cp)
        for cp in copies:
            cp.wait()

    any_spec = pl.BlockSpec(memory_space=pl.ANY)
    out_shape = tuple(jax.ShapeDtypeStruct(s.shape if scatter else (NDEV,) + s.shape, s.dtype) for s in srcs)
    outs = pl.pallas_call(
        body, name=name, out_shape=out_shape, in_specs=[any_spec] * n, out_specs=tuple([any_spec] * n),
        scratch_shapes=[pltpu.SemaphoreType.DMA((n, NDEV - 1)), pltpu.SemaphoreType.DMA((n, NDEV - 1)),
                        pltpu.SemaphoreType.DMA((n,))],
    )(*srcs)
    return list(outs)


def _sum_slots(g, name):
    _, rows, cols = g.shape
    tm = _div(rows, 1024, 8)

    def body(g_ref, o_ref):
        acc = g_ref[0]
        for j in range(1, NDEV):
            acc = acc + g_ref[j]
        o_ref[...] = acc

    return pl.pallas_call(
        body, name=name, grid=(rows // tm,), out_shape=jax.ShapeDtypeStruct((rows, cols), F32),
        in_specs=[pl.BlockSpec((NDEV, tm, cols), lambda i: (0, i, 0))],
        out_specs=pl.BlockSpec((tm, cols), lambda i: (i, 0)), compiler_params=_cp(("arbitrary",)))(g)


def _mm_nn(a, b3, out_dtype, name, tm=256, tn=1024):
    m_, k_ = a.shape
    j_, _, nj = b3.shape
    tm = _div(m_, tm, 16)
    tn = _div(nj, tn, LANE)
    per = nj // tn

    def body(a_ref, b_ref, o_ref):
        o_ref[...] = jnp.dot(a_ref[...], b_ref[0], preferred_element_type=F32).astype(o_ref.dtype)

    return pl.pallas_call(
        body, name=name, grid=(j_ * per, m_ // tm), out_shape=jax.ShapeDtypeStruct((m_, j_ * nj), out_dtype),
        in_specs=[pl.BlockSpec((tm, k_), lambda n, m: (m, 0)),
                  pl.BlockSpec((1, k_, tn), lambda n, m: (n // per, 0, n % per))],
        out_specs=pl.BlockSpec((tm, tn), lambda n, m: (m, n)),
        compiler_params=_cp(("arbitrary", "arbitrary")))(a, b3)


def _mm_nt(a, b3, out_dtype, name, tm=256, tn=1024, tk=2048):
    m_, k_ = a.shape
    j_, n_, kj = b3.shape
    tm = _div(m_, tm, 16)
    tn = _div(n_, tn, LANE)
    tk = _div(kj, tk, LANE)
    perk = kj // tk
    nk = k_ // tk

    def body(a_ref, b_ref, o_ref, acc_ref):
        k = pl.program_id(2)

        @pl.when(k == 0)
        def _():
            acc_ref[...] = jnp.zeros_like(acc_ref)

        acc_ref[...] += lax.dot_general(a_ref[...], b_ref[0], (((1,), (1,)), ((), ())), preferred_element_type=F32)

        @pl.when(k == nk - 1)
        def _():
            o_ref[...] = acc_ref[...].astype(o_ref.dtype)

    return pl.pallas_call(
        body, name=name, grid=(n_ // tn, m_ // tm, nk), out_shape=jax.ShapeDtypeStruct((m_, n_), out_dtype),
        in_specs=[pl.BlockSpec((tm, tk), lambda n, m, k: (m, k)),
                  pl.BlockSpec((1, tn, tk), lambda n, m, k: (k // perk, n, k % perk))],
        out_specs=pl.BlockSpec((tm, tn), lambda n, m, k: (m, n)),
        scratch_shapes=[pltpu.VMEM((tm, tn), F32)],
        compiler_params=_cp(("arbitrary", "arbitrary", "arbitrary")))(a, b3)


def _mm_tn(a, b, out_dtype, name, oj=1, tm=1024, tn=512):
    k_, m_ = a.shape
    _, n_ = b.shape
    nj = n_ // oj
    tm = _div(m_, tm, LANE)
    tn = _div(nj, tn, LANE)
    per = nj // tn

    def body(a_ref, b_ref, o_ref):
        o_ref[0] = lax.dot_general(a_ref[...], b_ref[...], (((0,), (0,)), ((), ())),
                                   preferred_element_type=F32).astype(o_ref.dtype)

    return pl.pallas_call(
        body, name=name, grid=(m_ // tm, n_ // tn), out_shape=jax.ShapeDtypeStruct((oj, m_, nj), out_dtype),
        in_specs=[pl.BlockSpec((k_, tm), lambda m, n: (0, m)), pl.BlockSpec((k_, tn), lambda m, n: (0, n))],
        out_specs=pl.BlockSpec((1, tm, tn), lambda m, n: (n // per, m, n % per)),
        compiler_params=_cp(("arbitrary", "arbitrary")))(a, b)


def _mod_fwd(c16, w, bias):
    _, d, cols = w.shape
    tn = _div(cols, 512, LANE)

    def body(c_ref, w_ref, b_ref, o_ref):
        sc = _silu(c_ref[...])
        o_ref[0] = jnp.dot(sc, w_ref[0], preferred_element_type=F32) + b_ref[0]

    return pl.pallas_call(
        body, name="mod_fwd", grid=(2, cols // tn), out_shape=jax.ShapeDtypeStruct((2, 16, cols), F32),
        in_specs=[pl.BlockSpec((16, d), lambda l, n: (0, 0)), pl.BlockSpec((1, d, tn), lambda l, n: (l, 0, n)),
                  pl.BlockSpec((1, 1, tn), lambda l, n: (l, 0, n))],
        out_specs=pl.BlockSpec((1, 16, tn), lambda l, n: (l, 0, n)),
        compiler_params=_cp(("arbitrary", "arbitrary")))(c16, w, bias)


def _adam(w, g, m, v):
    m2 = ADAM_B1 * m + (1.0 - ADAM_B1) * g
    v2 = ADAM_B2 * v + (1.0 - ADAM_B2) * (g * g)
    m_hat = m2 / (1.0 - ADAM_B1 ** ADAM_STEP)
    v_hat = v2 / (1.0 - ADAM_B2 ** ADAM_STEP)
    delta = -ADAM_LR * (m_hat / (jnp.sqrt(v_hat) + ADAM_EPS) + ADAM_WD * w)
    return delta, m2, v2


def _mod_bwd(c16, dm, w, m, v):
    _, d, cols = w.shape
    tm = _div(d, 256, 8)

    def body(c_ref, dm_ref, w_ref, m_ref, v_ref, g_ref, dl_ref, nm_ref, nv_ref, gc_ref):
        sc = _silu(c_ref[...])
        g = lax.dot_general(sc, dm_ref[0], (((0,), (0,)), ((), ())), preferred_element_type=F32,
                            precision=lax.Precision.HIGHEST)
        wv = w_ref[0]
        dl, m2, v2 = _adam(wv, g, m_ref[0], v_ref[0])
        g_ref[0] = g
        dl_ref[0] = dl
        nm_ref[0] = m2
        nv_ref[0] = v2
        gc_ref[0] = jnp.sum(wv * dm_ref[0, 8:9, :], axis=1, keepdims=True)

    big = pl.BlockSpec((1, tm, cols), lambda l, i: (l, i, 0))
    shp = jax.ShapeDtypeStruct(w.shape, F32)
    return pl.pallas_call(
        body, name="mod_bwd", grid=(2, d // tm),
        out_shape=(shp, shp, shp, shp, jax.ShapeDtypeStruct((2, d, 1), F32)),
        in_specs=[pl.BlockSpec((16, tm), lambda l, i: (0, i)), pl.BlockSpec((1, 16, cols), lambda l, i: (l, 0, 0)),
                  big, big, big],
        out_specs=(big, big, big, big, pl.BlockSpec((1, tm, 1), lambda l, i: (l, i, 0))),
        compiler_params=_cp(("arbitrary", "arbitrary")))(c16, dm, w, m, v)


def _row_spec(tm, d, off=0):
    return pl.BlockSpec((tm, d), lambda i: (i + off, 0))


def _vec_spec(d):
    return pl.BlockSpec((1, d), lambda i: (0, 0))


def _norm_fwd(x, nw, shift, scale, name):
    n, d = x.shape
    tm = _div(n, ROW_TILE, 16)

    def body(x_ref, nw_ref, sh_ref, sc_ref, hn_ref, r_ref):
        xv = x_ref[...]
        r = lax.rsqrt(jnp.mean(xv * xv, axis=1, keepdims=True) + EPS)
        hn_ref[...] = (xv * r * nw_ref[...] * (1.0 + sc_ref[...]) + sh_ref[...]).astype(BF16)
        r_ref[...] = r

    return pl.pallas_call(
        body, name=name, grid=(n // tm,),
        out_shape=(jax.ShapeDtypeStruct((n, d), BF16), jax.ShapeDtypeStruct((n, 1), F32)),
        in_specs=[_row_spec(tm, d), _vec_spec(d), _vec_spec(d), _vec_spec(d)],
        out_specs=(_row_spec(tm, d), _row_spec(tm, 1)), compiler_params=_cp(("arbitrary",)))(x, nw, shift, scale)


def _resid_norm_fwd(x, out0, gate, nw, shift, scale):
    n, d = x.shape
    tm = _div(n, ROW_TILE, 16)

    def body(x_ref, o_ref, g_ref, nw_ref, sh_ref, sc_ref, x1_ref, hn_ref, r_ref):
        xv = x_ref[...] + g_ref[...] * o_ref[...]
        r = lax.rsqrt(jnp.mean(xv * xv, axis=1, keepdims=True) + EPS)
        x1_ref[...] = xv
        hn_ref[...] = (xv * r * nw_ref[...] * (1.0 + sc_ref[...]) + sh_ref[...]).astype(BF16)
        r_ref[...] = r

    return pl.pallas_call(
        body, name="resid_norm1", grid=(n // tm,),
        out_shape=(jax.ShapeDtypeStruct((n, d), F32), jax.ShapeDtypeStruct((n, d), BF16),
                   jax.ShapeDtypeStruct((n, 1), F32)),
        in_specs=[_row_spec(tm, d), _row_spec(tm, d), _vec_spec(d), _vec_spec(d), _vec_spec(d), _vec_spec(d)],
        out_specs=(_row_spec(tm, d), _row_spec(tm, d), _row_spec(tm, 1)),
        compiler_params=_cp(("arbitrary",)))(x, out0, gate, nw, shift, scale)


def _loss_head(x1, out1, gate1, fnw, tgt):
    n, d = x1.shape
    tm = _div(n, ROW_TILE, 16)
    nsteps = n // tm

    def body(x_ref, o_ref, g_ref, w_ref, t_ref, dx_ref, do_ref, loss_ref, dw_ref, dg_ref, lacc_ref):
        i = pl.program_id(0)

        @pl.when(i == 0)
        def _():
            lacc_ref[...] = jnp.zeros_like(lacc_ref)
            dw_ref[...] = jnp.zeros_like(dw_ref)
            dg_ref[...] = jnp.zeros_like(dg_ref)

        ov = o_ref[...]
        gv = g_ref[...]
        wv = w_ref[...]
        xv = x_ref[...] + gv * ov
        r = lax.rsqrt(jnp.mean(xv * xv, axis=1, keepdims=True) + EPS)
        xh = xv * r
        diff = xh * wv - t_ref[...]
        lacc_ref[...] += jnp.sum(diff * diff, axis=0, keepdims=True)
        dy = diff * (1.0 / d)
        dw_ref[...] += jnp.sum(dy * xh, axis=0, keepdims=True)
        dxh = dy * wv
        dx = r * (dxh - xh * jnp.mean(xh * dxh, axis=1, keepdims=True))
        dx_ref[...] = dx
        dg_ref[...] += jnp.sum(dx * ov, axis=0, keepdims=True)
        do_ref[...] = (gv * dx).astype(BF16)

        @pl.when(i == nsteps - 1)
        def _():
            loss_ref[...] = (0.5 / d) * jnp.sum(lacc_ref[...], axis=1, keepdims=True)

    vec = jax.ShapeDtypeStruct((1, d), F32)
    return pl.pallas_call(
        body, name="loss_head", grid=(nsteps,),
        out_shape=(jax.ShapeDtypeStruct((n, d), F32), jax.ShapeDtypeStruct((n, d), BF16),
                   jax.ShapeDtypeStruct((1, 1), F32), vec, vec),
        in_specs=[_row_spec(tm, d), _row_spec(tm, d), _vec_spec(d), _vec_spec(d), _row_spec(tm, d)],
        out_specs=(_row_spec(tm, d), _row_spec(tm, d), pl.BlockSpec((1, 1), lambda i: (0, 0)), _vec_spec(d),
                   _vec_spec(d)),
        scratch_shapes=[pltpu.VMEM((1, d), F32)], compiler_params=_cp(("arbitrary",)))(x1, out1, gate1, fnw, tgt)


def _norm_bwd(x, rinv, dhns, nw, scale, name, row_off=0, dx_in=None, resid=None):
    n, d = x.shape
    tm = _div(n, ROW_TILE // 2, 16)
    nd = len(dhns)
    has_in = dx_in is not None
    has_res = resid is not None

    def body(*refs):
        it = iter(refs)
        x_ref, r_ref = next(it), next(it)
        dh_refs = [next(it) for _ in range(nd)]
        nw_ref, sc_ref = next(it), next(it)
        dxin_ref = next(it) if has_in else None
        o_ref, g_ref = (next(it), next(it)) if has_res else (None, None)
        dx_ref = next(it)
        do_ref = next(it) if has_res else None
        dsh_ref, dsc_ref, dnw_ref = next(it), next(it), next(it)
        dg_ref = next(it) if has_res else None
        i = pl.program_id(0)

        @pl.when(i == 0)
        def _():
            dsh_ref[...] = jnp.zeros_like(dsh_ref)
            dsc_ref[...] = jnp.zeros_like(dsc_ref)
            dnw_ref[...] = jnp.zeros_like(dnw_ref)
            if has_res:
                dg_ref[...] = jnp.zeros_like(dg_ref)

        dh = dh_refs[0][...]
        for rr in dh_refs[1:]:
            dh = dh + rr[...]
        r = r_ref[...]
        xh = x_ref[...] * r
        nwv = nw_ref[...]
        mod = 1.0 + sc_ref[...]
        dsh_ref[...] += jnp.sum(dh, axis=0, keepdims=True)
        dhx = dh * xh
        dsc_ref[...] += jnp.sum(dhx * nwv, axis=0, keepdims=True)
        dnw_ref[...] += jnp.sum(dhx * mod, axis=0, keepdims=True)
        dxh = dh * (nwv * mod)
        dx = r * (dxh - xh * jnp.mean(xh * dxh, axis=1, keepdims=True))
        if has_in:
            dx = dx + dxin_ref[...]
        dx_ref[...] = dx
        if has_res:
            dg_ref[...] += jnp.sum(dx * o_ref[...], axis=0, keepdims=True)
            do_ref[...] = (g_ref[...] * dx).astype(BF16)

    ins = [x, rinv] + list(dhns) + [nw, scale]
    in_specs = [_row_spec(tm, d), _row_spec(tm, 1)] + [_row_spec(tm, d, row_off)] * nd + [_vec_spec(d), _vec_spec(d)]
    if has_in:
        ins.append(dx_in)
        in_specs.append(_row_spec(tm, d))
    if has_res:
        ins += list(resid)
        in_specs += [_row_spec(tm, d), _vec_spec(d)]
    vec = jax.ShapeDtypeStruct((1, d), F32)
    out_shape = [jax.ShapeDtypeStruct((n, d), F32)]
    out_specs = [_row_spec(tm, d)]
    if has_res:
        out_shape.append(jax.ShapeDtypeStruct((n, d), BF16))
        out_specs.append(_row_spec(tm, d))
    out_shape += [vec, vec, vec]
    out_specs += [_vec_spec(d)] * 3
    if has_res:
        out_shape.append(vec)
        out_specs.append(_vec_spec(d))
    return pl.pallas_call(
        body, name=name, grid=(n // tm,), out_shape=tuple(out_shape), in_specs=in_specs, out_specs=tuple(out_specs),
        compiler_params=_cp(("arbitrary",)))(*ins)


def _seg_masks(rows, width, seg_bounds, offs):
    t = lax.broadcasted_iota(jnp.int32, (rows, width), 0)
    s0 = jnp.zeros_like(t)
    s1 = jnp.full_like(t, rows)
    for (a, b) in seg_bounds:
        inside = (t >= a) & (t < b)
        s0 = jnp.where(inside, a, s0)
        s1 = jnp.where(inside, b, s1)
    return {off: ((t + off >= s0) & (t + off < s1)) for off in offs}


def _shift_rows(v, off, valid):
    rows = v.shape[0]
    return jnp.where(valid, pltpu.roll(v, (-off) % rows, axis=0), 0.0)


def _conv_fwd_val(xv, wv, taps, masks):
    acc = None
    for tap, off in taps:
        xs = xv if off == 0 else _shift_rows(xv, off, masks[off])
        term = wv[tap:tap + 1, :] * xs
        acc = term if acc is None else acc + term
    return acc


def _conv_bwd_val(xv, dacc, wv, taps, masks):
    dx = None
    dws = []
    for tap, off in taps:
        xs = xv if off == 0 else _shift_rows(xv, off, masks[off])
        dws.append(jnp.sum(dacc * xs, axis=0, keepdims=True))
        ds = dacc if off == 0 else _shift_rows(dacc, -off, masks[-off])
        term = wv[tap:tap + 1, :] * ds
        dx = term if dx is None else dx + term
    return dx, dws


def _taps(k):
    left = k // 2
    return [(j, j - left) for j in range(k)]


def _seq_bounds():
    return [(0, CTX_LEN), (CTX_LEN, CTX_LEN + SEQ)]


def _dn_pre_fwd(proj, convw):
    dd = _dims()
    r = dd["r"]
    nstrip = 3 * DN_HEADS
    taps = _taps(DN_CONV)
    offs = [o for _, o in taps if o != 0]

    def body(x_ref, w_ref, o_ref):
        kind = pl.program_id(0) // DN_HEADS
        masks = _seg_masks(r, DN_HEAD_DIM, _seq_bounds(), offs)
        y = _silu(_conv_fwd_val(x_ref[...], w_ref[...], taps, masks))
        inv = lax.rsqrt(jnp.sum(y * y, axis=1, keepdims=True) + EPS)
        scale = jnp.where(kind == 0, inv * (DN_HEAD_DIM ** -0.5), jnp.where(kind == 1, inv, 1.0))
        o_ref[...] = y * scale

    return pl.pallas_call(
        body, name="dn_pre_fwd", grid=(nstrip,), out_shape=jax.ShapeDtypeStruct((r, 3 * dd["dnw"]), F32),
        in_specs=[pl.BlockSpec((r, DN_HEAD_DIM), lambda j: (0, j)), pl.BlockSpec((8, DN_HEAD_DIM), lambda j: (0, j))],
        out_specs=pl.BlockSpec((r, DN_HEAD_DIM), lambda j: (0, j)), compiler_params=_cp(("arbitrary",)))(proj, convw)


def _dn_pre_bwd(proj, convw, dqkv, kind):
    dd = _dims()
    r = dd["r"]
    taps = _taps(DN_CONV)
    offs = sorted({o for _, o in taps if o != 0} | {-o for _, o in taps if o != 0})
    cscale = DN_HEAD_DIM ** -0.5 if kind == 0 else 1.0

    def body(x_ref, w_ref, d_ref, dx_ref, dw_ref):
        masks = _seg_masks(r, DN_HEAD_DIM, _seq_bounds(), offs)
        xv = x_ref[...]
        wv = w_ref[...]
        acc = _conv_fwd_val(xv, wv, taps, masks)
        dout = d_ref[0] + d_ref[1]
        if kind == 2:
            dy = dout
        else:
            y = _silu(acc)
            inv = lax.rsqrt(jnp.sum(y * y, axis=1, keepdims=True) + EPS)
            dy = cscale * inv * (dout - y * (inv * inv) * jnp.sum(dout * y, axis=1, keepdims=True))
        dacc = dy * _dsilu(acc)
        dx, dws = _conv_bwd_val(xv, dacc, wv, taps, masks)
        dx_ref[...] = dx.astype(BF16)
        dw_ref[...] = jnp.concatenate(dws + [jnp.zeros((8 - len(dws), DN_HEAD_DIM), F32)], axis=0)

    base = kind * DN_HEADS
    return pl.pallas_call(
        body, name=f"dn_pre_bwd{kind}", grid=(DN_HEADS,),
        out_shape=(jax.ShapeDtypeStruct((r, dd["dnw"]), BF16), jax.ShapeDtypeStruct((8, dd["dnw"]), F32)),
        in_specs=[pl.BlockSpec((r, DN_HEAD_DIM), lambda j: (0, base + j)),
                  pl.BlockSpec((8, DN_HEAD_DIM), lambda j: (0, base + j)),
                  pl.BlockSpec((2, r, DN_HEAD_DIM), lambda j: (0, 0, j))],
        out_specs=(pl.BlockSpec((r, DN_HEAD_DIM), lambda j: (0, j)), pl.BlockSpec((8, DN_HEAD_DIM), lambda j: (0, j))),
        compiler_params=_cp(("arbitrary",)))(proj, convw, dqkv)


def _lru_conv_fwd(xl, convw):
    r, lw = xl.shape
    taps = _taps(LRU_CONV)
    offs = [o for _, o in taps if o != 0]

    def body(x_ref, w_ref, o_ref):
        masks = _seg_masks(r, LANE, _seq_bounds(), offs)
        wv = w_ref[...]
        o_ref[...] = _conv_fwd_val(x_ref[...], wv, taps, masks) + wv[4:5, :]

    return pl.pallas_call(
        body, name="lru_conv_fwd", grid=(lw // LANE,), out_shape=jax.ShapeDtypeStruct((r, lw), F32),
        in_specs=[pl.BlockSpec((r, LANE), lambda j: (0, j)), pl.BlockSpec((8, LANE), lambda j: (0, j))],
        out_specs=pl.BlockSpec((r, LANE), lambda j: (0, j)), compiler_params=_cp(("arbitrary",)))(xl, convw)


def _lru_conv_bwd(xl, convw, dxc):
    r, lw = xl.shape
    taps = _taps(LRU_CONV)
    offs = sorted({o for _, o in taps if o != 0} | {-o for _, o in taps if o != 0})

    def body(x_ref, w_ref, d_ref, dx_ref, dw_ref):
        masks = _seg_masks(r, LANE, _seq_bounds(), offs)
        dacc = d_ref[0] + d_ref[1]
        dx, dws = _conv_bwd_val(x_ref[...], dacc, w_ref[...], taps, masks)
        dx_ref[...] = dx
        db = jnp.sum(dacc, axis=0, keepdims=True)
        dw_ref[...] = jnp.concatenate(dws + [db, jnp.zeros((3, LANE), F32)], axis=0)

    return pl.pallas_call(
        body, name="lru_conv_bwd", grid=(lw // LANE,),
        out_shape=(jax.ShapeDtypeStruct((r, lw), F32), jax.ShapeDtypeStruct((8, lw), F32)),
        in_specs=[pl.BlockSpec((r, LANE), lambda j: (0, j)), pl.BlockSpec((8, LANE), lambda j: (0, j)),
                  pl.BlockSpec((2, r, LANE), lambda j: (0, 0, j))],
        out_specs=(pl.BlockSpec((r, LANE), lambda j: (0, j)), pl.BlockSpec((8, LANE), lambda j: (0, j))),
        compiler_params=_cp(("arbitrary",)))(xl, convw, dxc)


def _tri(rev_lane, n, m):
    r = lax.broadcasted_iota(jnp.int32, (n, n), 0)
    c = lax.broadcasted_iota(jnp.int32, (n, n), 1)
    return (r >= c).astype(F32), (r <= c).astype(F32)


def _hdot(a, b):
    return jnp.dot(a, b, preferred_element_type=F32, precision=lax.Precision.HIGHEST)


def _gate_lanes():
    lane = lax.broadcasted_iota(jnp.int32, (DN_CHUNK, LANE), 1)
    group = lane // 32
    rev = (lane % 32) >= DN_HEADS if DN_HEADS * 2 <= 32 else None
    return group, rev


def _dn_gates_fwd(pba, par):
    r = pba.shape[0]
    nch = r // DN_CHUNK
    hh = 2 * DN_HEADS

    def body(x_ref, p_ref, o_ref):
        lo, up = _tri(None, DN_CHUNK, DN_CHUNK)
        ones = jnp.ones((DN_CHUNK, DN_CHUNK), F32)
        lane = lax.broadcasted_iota(jnp.int32, (DN_CHUNK, LANE), 1)
        group = lane // 32
        rev = (lane % 32) >= DN_HEADS
        dtb = p_ref[0:1, :]
        nea = p_ref[1:2, :]

        def step(i, carry):
            rows = pl.ds(pl.multiple_of(i * DN_CHUNK, DN_CHUNK), DN_CHUNK)
            xv = x_ref[rows, :]
            beta = jax.nn.sigmoid(xv)
            g = jnp.where(group == 1, nea * _softplus(xv + dtb), 0.0)
            gc = jnp.where(rev, _hdot(up, g), _hdot(lo, g))
            gt = _hdot(ones, g)
            out = jnp.where(group == 0, beta, jnp.where(group == 1, g, 0.0))
            out = out + jnp.where(group == 2, pltpu.roll(gc, 32, axis=1), 0.0)
            out = out + jnp.where(group == 3, pltpu.roll(gt, 64, axis=1), 0.0)
            o_ref[rows, :] = out
            return carry

        lax.fori_loop(0, nch, step, 0)

    return pl.pallas_call(body, name="dn_gates_fwd", out_shape=jax.ShapeDtypeStruct((r, LANE), F32),
                          compiler_params=_cp(None))(pba, par)


def _dn_gates_bwd(pba, par, dsc):
    r = pba.shape[0]
    nch = r // DN_CHUNK

    def body(x_ref, p_ref, d_ref, dx_ref, dp_ref, acc_ref):
        lo, up = _tri(None, DN_CHUNK, DN_CHUNK)
        ones = jnp.ones((DN_CHUNK, DN_CHUNK), F32)
        lane = lax.broadcasted_iota(jnp.int32, (DN_CHUNK, LANE), 1)
        group = lane // 32
        rev = (lane % 32) >= DN_HEADS
        dtb = p_ref[0:1, :]
        nea = p_ref[1:2, :]
        acc_ref[...] = jnp.zeros_like(acc_ref)

        def step(i, carry):
            rows = pl.ds(pl.multiple_of(i * DN_CHUNK, DN_CHUNK), DN_CHUNK)
            xv = x_ref[rows, :]
            dv = d_ref[rows, :]
            beta = jax.nn.sigmoid(xv)
            z = xv + dtb
            sp = _softplus(z)
            dgc = jnp.where(group == 1, pltpu.roll(dv, LANE - 32, axis=1), 0.0)
            dgt = jnp.where(group == 1, pltpu.roll(dv, LANE - 64, axis=1), 0.0)
            dg = jnp.where(rev, _hdot(lo, dgc), _hdot(up, dgc)) + _hdot(ones, dgt)
            dg = dg + jnp.where(group == 1, dv, 0.0)
            dal = dg * nea * jax.nn.sigmoid(z)
            dpb = dv * beta * (1.0 - beta)
            dx_ref[rows, :] = jnp.where(group == 0, dpb, jnp.where(group == 1, dal, 0.0)).astype(BF16)
            acc_ref[0:1, :] += jnp.sum(jnp.where(group == 1, dal, 0.0), axis=0, keepdims=True)
            acc_ref[1:2, :] += jnp.sum(jnp.where(group == 1, dg * nea * sp, 0.0), axis=0, keepdims=True)
            return carry

        lax.fori_loop(0, nch, step, 0)
        dp_ref[...] = acc_ref[...]

    return pl.pallas_call(
        body, name="dn_gates_bwd",
        out_shape=(jax.ShapeDtypeStruct((r, LANE), BF16), jax.ShapeDtypeStruct((8, LANE), F32)),
        scratch_shapes=[pltpu.VMEM((8, LANE), F32)], compiler_params=_cp(None))(pba, par, dsc)


@jax.custom_vjp
def _inv_unit_tri(l):
    n = l.shape[0]
    eye = (lax.broadcasted_iota(jnp.int32, (n, n), 0) == lax.broadcasted_iota(jnp.int32, (n, n), 1)).astype(F32)
    m = -l
    inv = eye + m
    p = m
    k = 2
    while k < n:
        p = _hdot(p, p)
        inv = inv + _hdot(inv, p)
        k *= 2
    return inv


def _inv_fwd(l):
    a = _inv_unit_tri(l)
    return a, a


def _inv_bwd(a, da):
    hi = lax.Precision.HIGHEST
    t = lax.dot_general(a, da, (((0,), (0,)), ((), ())), preferred_element_type=F32, precision=hi)
    return (-lax.dot_general(t, a, (((1,), (1,)), ((), ())), preferred_element_type=F32, precision=hi),)


_inv_unit_tri.defvjp(_inv_fwd, _inv_bwd)


def _delta_chunk(q4, k4, v4, sc, s4, rev):
    hps = HEADS_PER_STEP
    c = DN_CHUNK
    hd = DN_HEAD_DIM
    ri = lax.broadcasted_iota(jnp.int32, (c, c), 0)
    ci = lax.broadcasted_iota(jnp.int32, (c, c), 1)
    ahead = (ri - ci) * jnp.where(rev, -1, 1)
    incl = ahead >= 0
    strict = ahead > 0
    sct = sc.T
    outs, news = [], []
    for j in range(hps):
        q = q4[:, j * hd:(j + 1) * hd]
        k = k4[:, j * hd:(j + 1) * hd]
        v = v4[:, j * hd:(j + 1) * hd]
        s = s4[j]
        beta_c = sc[:, j:j + 1]
        gc_c = sc[:, hps + j:hps + j + 1]
        gt_c = sc[:, 2 * hps + j:2 * hps + j + 1]
        beta_r = sct[j:j + 1, :]
        gc_r = sct[hps + j:hps + j + 1, :]
        dmat = jnp.where(incl, jnp.exp(jnp.where(incl, gc_c - gc_r, 0.0)), 0.0)
        kk = lax.dot_general(k, k, (((1,), (1,)), ((), ())), preferred_element_type=F32,
                             precision=lax.Precision.HIGHEST)
        qk = lax.dot_general(q, k, (((1,), (1,)), ((), ())), preferred_element_type=F32)
        low = jnp.where(strict, beta_c * kk * dmat, 0.0)
        a = _inv_unit_tri(low)
        w = _hdot(a * (beta_r * jnp.exp(gc_r)), k)
        u = _hdot(a * beta_r, v)
        u2 = u - jnp.dot(w, s, preferred_element_type=F32)
        o = jnp.exp(gc_c) * jnp.dot(q, s, preferred_element_type=F32) + jnp.dot(qk * dmat, u2,
                                                                                 preferred_element_type=F32)
        ke = k * jnp.exp(gt_c - gc_c)
        s_new = jnp.exp(gt_c[0:1, :]) * s + lax.dot_general(ke, u2, (((0,), (0,)), ((), ())),
                                                            preferred_element_type=F32)
        outs.append(o)
        news.append(s_new)
    return jnp.concatenate(outs, axis=1), jnp.stack(news, axis=0)


def _chunk_index(d, n, nctx, ntot):
    rev_idx = jnp.where(n < nctx, nctx - 1 - n, ntot - 1 - (n - nctx))
    return jnp.where(d == 0, n, rev_idx)


def _delta_fwd(qkv, scg):
    dd = _dims()
    r, dnw = dd["r"], dd["dnw"]
    hps = HEADS_PER_STEP
    nhg = DN_HEADS // hps
    bw = hps * DN_HEAD_DIM
    nctx, ntot = CTX_LEN // DN_CHUNK, r // DN_CHUNK

    def body(q_ref, k_ref, v_ref, sc_ref, o_ref, ss_ref, s_ref):
        d, n = pl.program_id(0), pl.program_id(2)

        @pl.when(n == 0)
        def _():
            s_ref[...] = jnp.zeros_like(s_ref)

        s4 = s_ref[...]
        ss_ref[0, 0] = s4
        o4, s4n = _delta_chunk(q_ref[...], k_ref[...], v_ref[...], sc_ref[0], s4, d == 1)
        o_ref[0] = o4
        s_ref[...] = s4n

    def rows(d, g, n):
        return _chunk_index(d, n, nctx, ntot)

    return pl.pallas_call(
        body, name="delta_fwd", grid=(2, nhg, ntot),
        out_shape=(jax.ShapeDtypeStruct((2, r, dnw), F32),
                   jax.ShapeDtypeStruct((2, ntot, DN_HEADS, DN_HEAD_DIM, DN_HEAD_DIM), F32)),
        in_specs=[pl.BlockSpec((DN_CHUNK, bw), lambda d, g, n: (rows(d, g, n), g)),
                  pl.BlockSpec((DN_CHUNK, bw), lambda d, g, n: (rows(d, g, n), nhg + g)),
                  pl.BlockSpec((DN_CHUNK, bw), lambda d, g, n: (rows(d, g, n), 2 * nhg + g)),
                  pl.BlockSpec((1, DN_CHUNK, LANE), lambda d, g, n: (d * nhg + g, rows(d, g, n), 0))],
        out_specs=(pl.BlockSpec((1, DN_CHUNK, bw), lambda d, g, n: (d, rows(d, g, n), g)),
                   pl.BlockSpec((1, 1, hps, DN_HEAD_DIM, DN_HEAD_DIM), lambda d, g, n: (d, n, g, 0, 0))),
        scratch_shapes=[pltpu.VMEM((hps, DN_HEAD_DIM, DN_HEAD_DIM), F32)],
        compiler_params=_cp(("arbitrary", "arbitrary", "arbitrary")))(qkv, qkv, qkv, scg)


def _delta_bwd(qkv, scg, states, do):
    dd = _dims()
    r, dnw = dd["r"], dd["dnw"]
    hps = HEADS_PER_STEP
    nhg = DN_HEADS // hps
    bw = hps * DN_HEAD_DIM
    nctx, ntot = CTX_LEN // DN_CHUNK, r // DN_CHUNK

    def body(q_ref, k_ref, v_ref, sc_ref, ss_ref, do_ref, dq_ref, dk_ref, dv_ref, dsc_ref, ds_ref):
        d, n2 = pl.program_id(0), pl.program_id(2)

        @pl.when(n2 == 0)
        def _():
            ds_ref[...] = jnp.zeros_like(ds_ref)

        fn = functools.partial(_delta_chunk, rev=(d == 1))
        _, vjp = jax.vjp(fn, q_ref[...], k_ref[...], v_ref[...], sc_ref[0], ss_ref[0, 0])
        dq, dk, dv, dsc, ds = vjp((do_ref[...], ds_ref[...]))
        dq_ref[0] = dq
        dk_ref[0] = dk
        dv_ref[0] = dv
        dsc_ref[0] = dsc
        ds_ref[...] = ds

    def rows(d, g, n2):
        return _chunk_index(d, ntot - 1 - n2, nctx, ntot)

    blk = lambda col: pl.BlockSpec((DN_CHUNK, bw), lambda d, g, n2: (rows(d, g, n2), col(g)))
    oblk = pl.BlockSpec((1, DN_CHUNK, bw), lambda d, g, n2: (d, rows(d, g, n2), g))
    scblk = pl.BlockSpec((1, DN_CHUNK, LANE), lambda d, g, n2: (d * nhg + g, rows(d, g, n2), 0))
    big = jax.ShapeDtypeStruct((2, r, dnw), F32)
    return pl.pallas_call(
        body, name="delta_bwd", grid=(2, nhg, ntot),
        out_shape=(big, big, big, jax.ShapeDtypeStruct(scg.shape, F32)),
        in_specs=[blk(lambda g: g), blk(lambda g: nhg + g), blk(lambda g: 2 * nhg + g), scblk,
                  pl.BlockSpec((1, 1, hps, DN_HEAD_DIM, DN_HEAD_DIM), lambda d, g, n2: (d, ntot - 1 - n2, g, 0, 0)),
                  blk(lambda g: g)],
        out_specs=(oblk, oblk, oblk, scblk),
        scratch_shapes=[pltpu.VMEM((hps, DN_HEAD_DIM, DN_HEAD_DIM), F32)],
        compiler_params=_cp(("arbitrary", "arbitrary", "arbitrary")))(qkv, qkv, qkv, scg, states, do)


def _lru_gate_vals(xn, wr, wi, br, bi, lam):
    r = jax.nn.sigmoid(jnp.dot(xn, wr, preferred_element_type=F32) + br)
    i = jax.nn.sigmoid(jnp.dot(xn, wi, preferred_element_type=F32) + bi)
    sp = _softplus(-lam)
    la = -LRU_C * r * sp
    a = jnp.exp(la)
    mm = jnp.sqrt(_neg_expm1(2.0 * la))
    return r, i, sp, la, a, mm


def _lru_gates_fwd(xc, w_r, w_i, vecs):
    r, lw = xc.shape
    bd = lw // LRU_BLOCKS
    tm = _div(r, ROW_TILE, 8)

    def body(x_ref, wr_ref, wi_ref, v_ref, a_ref, b_ref):
        for n in range(LRU_BLOCKS):
            sl = slice(n * bd, (n + 1) * bd)
            xn = x_ref[:, sl]
            _, i, _, _, a, mm = _lru_gate_vals(xn, wr_ref[0, n], wi_ref[0, n], v_ref[0, 0:1, sl], v_ref[0, 1:2, sl],
                                               v_ref[0, 2:3, sl])
            a_ref[0, :, sl] = a
            b_ref[0, :, sl] = mm * (i * xn)

    wspec = pl.BlockSpec((1, LRU_BLOCKS, bd, bd), lambda d, i: (d, 0, 0, 0))
    out = jax.ShapeDtypeStruct((2, r, lw), F32)
    ospec = pl.BlockSpec((1, tm, lw), lambda d, i: (d, i, 0))
    return pl.pallas_call(
        body, name="lru_gates_fwd", grid=(2, r // tm), out_shape=(out, out),
        in_specs=[pl.BlockSpec((tm, lw), lambda d, i: (i, 0)), wspec, wspec,
                  pl.BlockSpec((1, 8, lw), lambda d, i: (d, 0, 0))],
        out_specs=(ospec, ospec), compiler_params=_cp(("arbitrary", "arbitrary")))(xc, w_r, w_i, vecs)


def _lru_gates_bwd(xc, w_r, w_i, vecs, da, db):
    r, lw = xc.shape
    bd = lw // LRU_BLOCKS
    tm = _div(r, ROW_TILE, 8)

    def body(x_ref, wr_ref, wi_ref, v_ref, da_ref, db_ref, dx_ref, dwr_ref, dwi_ref, dv_ref):
        @pl.when(pl.program_id(1) == 0)
        def _():
            dwr_ref[...] = jnp.zeros_like(dwr_ref)
            dwi_ref[...] = jnp.zeros_like(dwi_ref)
            dv_ref[...] = jnp.zeros_like(dv_ref)

        for n in range(LRU_BLOCKS):
            sl = slice(n * bd, (n + 1) * bd)
            xn = x_ref[:, sl]
            wr, wi = wr_ref[0, n], wi_ref[0, n]
            lam = v_ref[0, 2:3, sl]
            rr, ii, sp, la, a, mm = _lru_gate_vals(xn, wr, wi, v_ref[0, 0:1, sl], v_ref[0, 1:2, sl], lam)
            dav, dbv = da_ref[0, :, sl], db_ref[0, :, sl]
            dmm = dbv * (ii * xn)
            dla = dav * a - dmm * (a * a) / jnp.maximum(mm, 1e-30)
            di = dbv * mm * xn
            dxn = dbv * mm * ii
            dr = dla * (-LRU_C * sp)
            dsp = jnp.sum(dla * (-LRU_C * rr), axis=0, keepdims=True)
            dpr = dr * rr * (1.0 - rr)
            dpi = di * ii * (1.0 - ii)
            dxn = dxn + lax.dot_general(dpr, wr, (((1,), (1,)), ((), ())), preferred_element_type=F32)
            dxn = dxn + lax.dot_general(dpi, wi, (((1,), (1,)), ((), ())), preferred_element_type=F32)
            dx_ref[0, :, sl] = dxn
            dwr_ref[0, n] += lax.dot_general(xn, dpr, (((0,), (0,)), ((), ())), preferred_element_type=F32)
            dwi_ref[0, n] += lax.dot_general(xn, dpi, (((0,), (0,)), ((), ())), preferred_element_type=F32)
            dv_ref[0, 0:1, sl] += jnp.sum(dpr, axis=0, keepdims=True)
            dv_ref[0, 1:2, sl] += jnp.sum(dpi, axis=0, keepdims=True)
            dv_ref[0, 2:3, sl] += dsp * (-jax.nn.sigmoid(-lam))

    wspec = pl.BlockSpec((1, LRU_BLOCKS, bd, bd), lambda d, i: (d, 0, 0, 0))
    vspec = pl.BlockSpec((1, 8, lw), lambda d, i: (d, 0, 0))
    big = pl.BlockSpec((1, tm, lw), lambda d, i: (d, i, 0))
    return pl.pallas_call(
        body, name="lru_gates_bwd", grid=(2, r // tm),
        out_shape=(jax.ShapeDtypeStruct((2, r, lw), F32), jax.ShapeDtypeStruct(w_r.shape, F32),
                   jax.ShapeDtypeStruct(w_i.shape, F32), jax.ShapeDtypeStruct((2, 8, lw), F32)),
        in_specs=[pl.BlockSpec((tm, lw), lambda d, i: (i, 0)), wspec, wspec, vspec, big, big],
        out_specs=(big, wspec, wspec, vspec), compiler_params=_cp(("arbitrary", "arbitrary")))(xc, w_r, w_i, vecs, da, db)


def _scan_tiles(rev):
    nctx, nlat = CTX_LEN // 8, SEQ // 8
    return nctx, nlat


def _tile_of(step, rev, nctx, ntot):
    if not rev:
        return step
    return jnp.where(step < nctx, nctx - 1 - step, ntot - 1 - (step - nctx))


def _lru_scan_fwd(a, b, d):
    _, r, lw = a.shape
    cw = _div(lw, 256, LANE)
    nctx, ntot = CTX_LEN // 8, r // 8
    rev = d == 1
    order = list(range(7, -1, -1)) if rev else list(range(8))

    def body(a_ref, b_ref, h_ref):
        def step(s, carry):
            t0 = pl.multiple_of(_tile_of(s, rev, nctx, ntot) * 8, 8)
            at = a_ref[0, pl.ds(t0, 8), :]
            bt = b_ref[0, pl.ds(t0, 8), :]
            rows = [None] * 8
            h = carry
            for i in order:
                h = at[i:i + 1, :] * h + bt[i:i + 1, :]
                rows[i] = h
            h_ref[pl.ds(t0, 8), :] = jnp.concatenate(rows, axis=0)
            return h

        lax.fori_loop(0, ntot, step, jnp.zeros((1, cw), F32))

    return pl.pallas_call(
        body, name=f"lru_scan_fwd{d}", grid=(lw // cw,), out_shape=jax.ShapeDtypeStruct((r, lw), F32),
        in_specs=[pl.BlockSpec((1, r, cw), lambda j: (d, 0, j)), pl.BlockSpec((1, r, cw), lambda j: (d, 0, j))],
        out_specs=pl.BlockSpec((r, cw), lambda j: (0, j)), compiler_params=_cp(("arbitrary",)))(a, b)


def _lru_scan_bwd(a, h, dh, d):
    _, r, lw = a.shape
    cw = _div(lw, 256, LANE)
    nctx, ntot = CTX_LEN // 8, r // 8
    rev = d == 1
    order = list(range(8)) if rev else list(range(7, -1, -1))

    def body(a_ref, h_ref, dh_ref, da_ref, db_ref):
        def step(s2, carry):
            s = ntot - 1 - s2
            t0 = pl.multiple_of(_tile_of(s, rev, nctx, ntot) * 8, 8)
            sp = jnp.maximum(s - 1, 0)
            tp = pl.multiple_of(_tile_of(sp, rev, nctx, ntot) * 8, 8)
            at = a_ref[0, pl.ds(t0, 8), :]
            ht = h_ref[pl.ds(t0, 8), :]
            hp = h_ref[pl.ds(tp, 8), :]
            dht = dh_ref[pl.ds(t0, 8), :]
            last = hp[0:1, :] if rev else hp[7:8, :]
            hprev_tile = jnp.where(s > 0, last, 0.0)
            das, dbs = [None] * 8, [None] * 8
            c = carry
            for i in order:
                lam = dht[i:i + 1, :] + c
                before = i + 1 if rev else i - 1
                hb = ht[before:before + 1, :] if 0 <= before < 8 else hprev_tile
                das[i] = lam * hb
                dbs[i] = lam
                c = at[i:i + 1, :] * lam
            da_ref[pl.ds(t0, 8), :] = jnp.concatenate(das, axis=0)
            db_ref[pl.ds(t0, 8), :] = jnp.concatenate(dbs, axis=0)
            return c

        lax.fori_loop(0, ntot, step, jnp.zeros((1, cw), F32))

    strip = pl.BlockSpec((r, cw), lambda j: (0, j))
    out = jax.ShapeDtypeStruct((r, lw), F32)
    return pl.pallas_call(
        body, name=f"lru_scan_bwd{d}", grid=(lw // cw,), out_shape=(out, out),
        in_specs=[pl.BlockSpec((1, r, cw), lambda j: (d, 0, j)), strip, strip],
        out_specs=(strip, strip), compiler_params=_cp(("arbitrary",)))(a, h, dh)


def _mix_post_fwd(o, hsum, proj, dnw_vec):
    dd = _dims()
    dnw, lw, nmain = dd["dnw"], dd["lw"], dd["nmain"]
    tm = _div(SEQ, ROW_TILE, 16)
    coff = CTX_LEN // tm
    gcol = (3 * dnw + lw) // dnw
    lcol = (3 * dnw + lw + dnw) // lw
    hd = DN_HEAD_DIM

    def body(o_ref, h_ref, g_ref, lg_ref, w_ref, y_ref):
        wv = w_ref[...]
        for hh in range(DN_HEADS):
            sl = slice(hh * hd, (hh + 1) * hd)
            os_ = o_ref[0, :, sl] + o_ref[1, :, sl]
            rr = lax.rsqrt(jnp.mean(os_ * os_, axis=1, keepdims=True) + EPS)
            y_ref[:, sl] = (os_ * rr * wv * _silu(g_ref[:, sl])).astype(BF16)
        y_ref[:, dnw:] = (h_ref[...] * _silu(lg_ref[...])).astype(BF16)

    return pl.pallas_call(
        body, name="mix_post_fwd", grid=(SEQ // tm,), out_shape=jax.ShapeDtypeStruct((SEQ, dnw + lw), BF16),
        in_specs=[pl.BlockSpec((2, tm, dnw), lambda i: (0, i + coff, 0)), pl.BlockSpec((tm, lw), lambda i: (i, 0)),
                  pl.BlockSpec((tm, dnw), lambda i: (i + coff, gcol)), pl.BlockSpec((tm, lw), lambda i: (i + coff, lcol)),
                  pl.BlockSpec((1, hd), lambda i: (0, 0))],
        out_specs=pl.BlockSpec((tm, dnw + lw), lambda i: (i, 0)), compiler_params=_cp(("arbitrary",)))(
            o, hsum, proj, proj, dnw_vec)


def _mix_post_bwd(o, hsum, proj, dnw_vec, dy):
    dd = _dims()
    dnw, lw, r = dd["dnw"], dd["lw"], dd["r"]
    tm = _div(SEQ, ROW_TILE, 16)
    coff = CTX_LEN // tm
    gcol = (3 * dnw + lw) // dnw
    lcol = (3 * dnw + lw + dnw) // lw
    hd = DN_HEAD_DIM

    def body(o_ref, h_ref, g_ref, lg_ref, w_ref, dy_ref, do_ref, dh_ref, dg_ref, dw_ref):
        i = pl.program_id(0)

        @pl.when(i == 0)
        def _():
            dw_ref[...] = jnp.zeros_like(dw_ref)

        @pl.when(i < coff)
        def _():
            do_ref[...] = jnp.zeros_like(do_ref)
            dg_ref[...] = jnp.zeros_like(dg_ref)

        @pl.when(i >= coff)
        def _():
            wv = w_ref[...]
            dwacc = jnp.zeros((1, hd), F32)
            for hh in range(DN_HEADS):
                sl = slice(hh * hd, (hh + 1) * hd)
                os_ = o_ref[0, :, sl] + o_ref[1, :, sl]
                rr = lax.rsqrt(jnp.mean(os_ * os_, axis=1, keepdims=True) + EPS)
                nn = os_ * rr
                gv = g_ref[:, sl]
                dyv = dy_ref[:, sl]
                don = dyv * _silu(gv)
                dg_ref[:, sl] = (dyv * nn * wv * _dsilu(gv)).astype(BF16)
                dwacc = dwacc + jnp.sum(don * nn, axis=0, keepdims=True)
                dn_ = don * wv
                do_ref[:, sl] = rr * (dn_ - nn * jnp.mean(nn * dn_, axis=1, keepdims=True))
            dw_ref[...] += dwacc
            lgv = lg_ref[...]
            dyl = dy_ref[:, dnw:]
            dh_ref[...] = dyl * _silu(lgv)
            dg_ref[:, dnw:] = (dyl * h_ref[...] * _dsilu(lgv)).astype(BF16)

    lat = lambda i: jnp.maximum(i - coff, 0)
    return pl.pallas_call(
        body, name="mix_post_bwd", grid=(r // tm,),
        out_shape=(jax.ShapeDtypeStruct((r, dnw), F32), jax.ShapeDtypeStruct((SEQ, lw), F32),
                   jax.ShapeDtypeStruct((r, dnw + lw), BF16), jax.ShapeDtypeStruct((1, hd), F32)),
        in_specs=[pl.BlockSpec((2, tm, dnw), lambda i: (0, i, 0)), pl.BlockSpec((tm, lw), lambda i: (lat(i), 0)),
                  pl.BlockSpec((tm, dnw), lambda i: (i, gcol)), pl.BlockSpec((tm, lw), lambda i: (i, lcol)),
                  pl.BlockSpec((1, hd), lambda i: (0, 0)), pl.BlockSpec((tm, dnw + lw), lambda i: (lat(i), 0))],
        out_specs=(pl.BlockSpec((tm, dnw), lambda i: (i, 0)), pl.BlockSpec((tm, lw), lambda i: (lat(i), 0)),
                   pl.BlockSpec((tm, dnw + lw), lambda i: (i, 0)), pl.BlockSpec((1, hd), lambda i: (0, 0))),
        compiler_params=_cp(("arbitrary",)))(o, hsum, proj, proj, dnw_vec, dy)


def _sc_masks(tm, cw):
    t = lax.broadcasted_iota(jnp.int32, (tm, cw), 0) % GRID_W
    return {-1: t != 0, 1: t != GRID_W - 1}


def _sc_mix_fwd(p, convw):
    s, w4 = p.shape
    w = w4 // 4
    tm = _div(s, 512, GRID_W if GRID_W % 16 == 0 else 16 * GRID_W)
    cw = _div(w, 512, LANE)
    nc = w // cw
    taps = _taps(SC_CONV)

    def body(b_ref, c_ref, x_ref, g_ref, w_ref, y_ref):
        masks = _sc_masks(tm, cw)
        u = c_ref[...] * x_ref[...]
        z = _conv_fwd_val(u, w_ref[...], taps, masks)
        y_ref[...] = (b_ref[...] * z * _silu(g_ref[...])).astype(BF16)

    blk = lambda k: pl.BlockSpec((tm, cw), lambda c, i: (i, k * nc + c))
    return pl.pallas_call(
        body, name="sc_mix_fwd", grid=(nc, s // tm), out_shape=jax.ShapeDtypeStruct((s, w), BF16),
        in_specs=[blk(0), blk(1), blk(2), blk(3), pl.BlockSpec((8, cw), lambda c, i: (0, c))],
        out_specs=pl.BlockSpec((tm, cw), lambda c, i: (i, c)),
        compiler_params=_cp(("arbitrary", "arbitrary")))(p, p, p, p, convw)


def _sc_mix_bwd(p, convw, dy):
    s, w4 = p.shape
    w = w4 // 4
    tm = _div(s, 512, GRID_W if GRID_W % 16 == 0 else 16 * GRID_W)
    cw = _div(w, 512, LANE)
    nc = w // cw
    taps = _taps(SC_CONV)

    def body(b_ref, c_ref, x_ref, g_ref, w_ref, dy_ref, db_ref, dc_ref, dx_ref, dg_ref, dw_ref):
        @pl.when(pl.program_id(1) == 0)
        def _():
            dw_ref[...] = jnp.zeros_like(dw_ref)

        masks = _sc_masks(tm, cw)
        wv = w_ref[...]
        bv, cv, xv, gv, dyv = b_ref[...], c_ref[...], x_ref[...], g_ref[...], dy_ref[...]
        u = cv * xv
        z = _conv_fwd_val(u, wv, taps, masks)
        sg = _silu(gv)
        db_ref[...] = (dyv * z * sg).astype(BF16)
        dg_ref[...] = (dyv * bv * z * _dsilu(gv)).astype(BF16)
        dz = dyv * bv * sg
        du, dws = _conv_bwd_val(u, dz, wv, taps, masks)
        dc_ref[...] = (du * xv).astype(BF16)
        dx_ref[...] = (du * cv).astype(BF16)
        dw_ref[...] += jnp.concatenate(dws + [jnp.zeros((8 - len(dws), cw), F32)], axis=0)

    blk = lambda k: pl.BlockSpec((tm, cw), lambda c, i: (i, k * nc + c))
    oblk = pl.BlockSpec((tm, cw), lambda c, i: (i, c))
    wblk = pl.BlockSpec((8, cw), lambda c, i: (0, c))
    out = jax.ShapeDtypeStruct((s, w), BF16)
    return pl.pallas_call(
        body, name="sc_mix_bwd", grid=(nc, s // tm),
        out_shape=(out, out, out, out, jax.ShapeDtypeStruct((8, w), F32)),
        in_specs=[blk(0), blk(1), blk(2), blk(3), wblk, oblk],
        out_specs=(oblk, oblk, oblk, oblk, wblk),
        compiler_params=_cp(("arbitrary", "arbitrary")))(p, p, p, p, convw, dy)


def _adamw_slots(w, slots, m, v, name):
    rows, cols = w.shape
    tm = _div(rows, max(16, (64 * 4096) // cols), 16)

    def body(w_ref, s_ref, m_ref, v_ref, g_ref, dl_ref, nm_ref, nv_ref):
        g = s_ref[0].astype(F32)
        for j in range(1, NDEV):
            g = g + s_ref[j].astype(F32)
        dl, m2, v2 = _adam(w_ref[...], g, m_ref[...], v_ref[...])
        g_ref[...] = g
        dl_ref[...] = dl
        nm_ref[...] = m2
        nv_ref[...] = v2

    blk = pl.BlockSpec((tm, cols), lambda i: (i, 0))
    out = jax.ShapeDtypeStruct((rows, cols), F32)
    return pl.pallas_call(
        body, name=name, grid=(rows // tm,), out_shape=(out, out, out, out),
        in_specs=[blk, pl.BlockSpec((NDEV, tm, cols), lambda i: (0, i, 0)), blk, blk],
        out_specs=(blk, blk, blk, blk), compiler_params=_cp(("arbitrary",)))(w, slots, m, v)


def _adamw_flat(w, g, m, v, name):
    rows, cols = w.shape
    tm = _div(rows, 1024, 8)

    def body(w_ref, g_ref, m_ref, v_ref, dl_ref, nm_ref, nv_ref):
        dl, m2, v2 = _adam(w_ref[...], g_ref[...], m_ref[...], v_ref[...])
        dl_ref[...] = dl
        nm_ref[...] = m2
        nv_ref[...] = v2

    blk = pl.BlockSpec((tm, cols), lambda i: (i, 0))
    out = jax.ShapeDtypeStruct((rows, cols), F32)
    return pl.pallas_call(
        body, name=name, grid=(rows // tm,), out_shape=(out, out, out), in_specs=[blk] * 4, out_specs=(blk, blk, blk),
        compiler_params=_cp(("arbitrary",)))(w, g, m, v)


def _pack(arrs):
    flat = [a.reshape(-1).astype(F32) for a in arrs]
    sizes = [f.shape[0] for f in flat]
    total = sum(sizes)
    padded = -(-total // 1024) * 1024
    if padded > total:
        flat.append(jnp.zeros((padded - total,), F32))
    offs, o = [], 0
    for s_ in sizes:
        offs.append(o)
        o += s_
    return jnp.concatenate(flat).reshape(padded // LANE, LANE), offs


def _unpack(flat2d, offs, shapes):
    flat = flat2d.reshape(-1)
    out = []
    for o, shp in zip(offs, shapes):
        n = 1
        for s_ in shp:
            n *= s_
        out.append(flat[o:o + n].reshape(shp))
    return out


def _pad_rows(a, rows):
    return jnp.concatenate([a, jnp.zeros((rows - a.shape[0],) + a.shape[1:], a.dtype)], axis=0)


def _to_col_major(t):
    n, ch = t.shape
    return t.reshape(n // GRID_W, GRID_W, ch).swapaxes(0, 1).reshape(n, ch)


def _to_raster(t):
    n, ch = t.shape
    return t.reshape(GRID_W, n // GRID_W, ch).swapaxes(0, 1).reshape(n, ch)


def _scg_layout(sc):
    r = sc.shape[0]
    hps = HEADS_PER_STEP
    nhg = DN_HEADS // hps
    t = sc.reshape(r, 4, 32)[:, :, :2 * DN_HEADS].reshape(r, 4, 2, nhg, hps)
    t = jnp.stack([t[:, 0], t[:, 2], t[:, 3]], axis=1)
    t = t.transpose(2, 3, 0, 1, 4).reshape(2 * nhg, r, 3 * hps)
    return jnp.concatenate([t, jnp.zeros((2 * nhg, r, LANE - 3 * hps), F32)], axis=2)


def _scg_layout_bwd(dscg):
    r = dscg.shape[1]
    hps = HEADS_PER_STEP
    nhg = DN_HEADS // hps
    t = dscg[:, :, :3 * hps].reshape(2, nhg, r, 3, hps).transpose(2, 3, 0, 1, 4).reshape(r, 3, 2 * DN_HEADS)
    t = jnp.concatenate([t, jnp.zeros((r, 3, 32 - 2 * DN_HEADS), F32)], axis=2) if 2 * DN_HEADS < 32 else t
    z = jnp.zeros((r, 1, 32), F32)
    return jnp.concatenate([t[:, 0:1], z, t[:, 1:2], t[:, 2:3]], axis=1).reshape(r, LANE)


def _gather_cols(full8, width):
    return full8.transpose(1, 0, 2).reshape(full8.shape[1], NDEV * width)


def kernel(x, c, ctx, c_ctx, mod_w, mod_b, norm_w, ab_w_in, ab_qkv_conv, ab_a_log, ab_dt_bias, ab_dn_norm, ab_lru_conv_w, ab_lru_conv_b, ab_lru_w_r, ab_lru_b_r, ab_lru_w_i, ab_lru_b_i, ab_lru_lambda, ab_w_out, sc_w_in, sc_conv, sc_w_out, final_norm_w, loss_target, m_c_ctx, m_mod_w, m_mod_b, m_norm_w, m_ab_w_in, m_ab_qkv_conv, m_ab_a_log, m_ab_dt_bias, m_ab_dn_norm, m_ab_lru_conv_w, m_ab_lru_conv_b, m_ab_lru_w_r, m_ab_lru_b_r, m_ab_lru_w_i, m_ab_lru_b_i, m_ab_lru_lambda, m_ab_w_out, m_sc_w_in, m_sc_conv, m_sc_w_out, m_final_norm_w, v_c_ctx, v_mod_w, v_mod_b, v_norm_w, v_ab_w_in, v_ab_qkv_conv, v_ab_a_log, v_ab_dt_bias, v_ab_dn_norm, v_ab_lru_conv_w, v_ab_lru_conv_b, v_ab_lru_w_r, v_ab_lru_b_r, v_ab_lru_w_i, v_ab_lru_b_i, v_ab_lru_lambda, v_ab_w_out, v_sc_w_in, v_sc_conv, v_sc_w_out, v_final_norm_w):
    dd = _dims()
    d, dnw, lw, r = dd["d"], dd["dnw"], dd["lw"], dd["r"]
    off_lru, off_beta, ab_state, ab_in, nmain = dd["off_lru"], dd["off_beta"], dd["ab_state"], dd["ab_in"], dd["nmain"]
    hh2 = 2 * DN_HEADS
    me = _me()
    x = x[0]
    ctx = ctx[0]
    tgt = loss_target[0]

    small_sharded = [ab_qkv_conv[0], ab_lru_conv_w[0], ab_lru_b_r[0], ab_lru_b_i[0], ab_lru_lambda[0], sc_conv[0]]
    pk, offs0 = _pack([c] + small_sharded)
    (g0,) = _exchange([pk], "gather_small_in", scatter=False)
    parts = [_unpack(g0[j], offs0, [c.shape] + [a.shape for a in small_sharded]) for j in range(NDEV)]
    c_all = jnp.concatenate([p[0] for p in parts], axis=0)
    qkv_conv, lru_conv_w, lru_b_r, lru_b_i, lru_lam, sc_conv_f = [
        jnp.concatenate([p[i] for p in parts], axis=-1) for i in range(1, 7)]

    mcols = mod_w.shape[2]
    c16 = jnp.concatenate([c_all, c_ctx[None, :], jnp.zeros((7, d), F32)], axis=0)
    bias = lax.dynamic_slice_in_dim(mod_b, me * mcols, mcols, axis=1)[:, None, :]
    mod_loc = _mod_fwd(c16, mod_w, bias)
    (mod_g,) = _exchange([mod_loc], "gather_mod", scatter=False)
    mod_all = mod_g.transpose(1, 2, 0, 3).reshape(2, 16, NDEV * mcols)
    mine = lax.dynamic_slice_in_dim(mod_all, me, 1, axis=1)[:, 0]
    shift = [mine[l, None, 0:d] for l in range(2)]
    scale = [mine[l, None, d:2 * d] for l in range(2)]
    gate = [mine[l, None, 2 * d:3 * d] for l in range(2)]
    shift_c, scale_c = mod_all[0, 8, None, 0:d], mod_all[0, 8, None, d:2 * d]
    nw0, nw1 = norm_w[0][None, :], norm_w[1][None, :]

    wsrc = [ab_w_in[0].astype(BF16), ab_w_out[0].astype(BF16), sc_w_in[0].astype(BF16), sc_w_out[0].astype(BF16)]
    g_abin, g_about, g_scin, g_scout = _exchange(wsrc, "gather_weights", scatter=False)
    w_full = _gather_cols(g_abin, ab_in // NDEV)
    w_main = jnp.concatenate([w_full[:, :off_beta], w_full[:, ab_state:]], axis=1)
    w_ba = jnp.concatenate([w_full[:, off_beta:ab_state], jnp.zeros((d, LANE - 2 * hh2), BF16)], axis=1)
    w_about = g_about.reshape(dnw + lw, d)
    w_scout = g_scout.reshape(d, d)

    hn_c, rinv_c = _norm_fwd(ctx, nw0, shift_c, scale_c, "norm0_ctx")
    hn_x, rinv_x = _norm_fwd(x, nw0, shift[0], scale[0], "norm0_x")
    hn0 = jnp.concatenate([hn_c, hn_x], axis=0)
    proj = _mm_nn(hn0, w_main[None], F32, "proj_main")
    pba = _mm_nn(hn0, w_ba[None], F32, "proj_ba")

    convw8 = _pad_rows(qkv_conv, 8)
    qkv = _dn_pre_fwd(proj, convw8)
    zpad = jnp.zeros((32 - hh2,), F32)
    par = jnp.stack([jnp.concatenate([jnp.zeros((32,), F32), ab_dt_bias[0].reshape(-1), zpad, jnp.zeros((64,), F32)]),
                     jnp.concatenate([jnp.zeros((32,), F32), -jnp.exp(ab_a_log[0].reshape(-1)), zpad,
                                      jnp.zeros((64,), F32)])] + [jnp.zeros((LANE,), F32)] * 6)
    if hh2 < 32:
        pba_l = jnp.concatenate([pba[:, :hh2], jnp.zeros((r, 32 - hh2), F32), pba[:, hh2:2 * hh2],
                                 jnp.zeros((r, LANE - 32 - hh2), F32)], axis=1)
    else:
        pba_l = pba
    sc = _dn_gates_fwd(pba_l, par)
    scg = _scg_layout(sc)
    o_dn, states = _delta_fwd(qkv, scg)

    xl = proj[:, off_lru:off_beta]
    xl_cat = jnp.concatenate([xl[:CTX_LEN], _to_col_major(xl[CTX_LEN:])], axis=0)
    lconv8 = jnp.concatenate([lru_conv_w, ab_lru_conv_b[0][None, :], jnp.zeros((3, lw), F32)], axis=0)
    xc = _lru_conv_fwd(xl_cat, lconv8)
    lvecs = jnp.stack([lru_b_r, lru_b_i, lru_lam] + [jnp.zeros_like(lru_lam)] * 5, axis=1)
    w_r, w_i = ab_lru_w_r[0], ab_lru_w_i[0]
    la_a, la_b = _lru_gates_fwd(xc, w_r, w_i, lvecs)
    h_dir = [_lru_scan_fwd(la_a, la_b, 0), _lru_scan_fwd(la_a, la_b, 1)]
    hsum = _to_raster(h_dir[0][CTX_LEN:] + h_dir[1][CTX_LEN:])

    dnn = ab_dn_norm[0][None, :]
    y0 = _mix_post_fwd(o_dn, hsum, proj, dnn)
    out0 = _mm_nn(y0, w_about[None], F32, "out0")

    x1, hn1, rinv1 = _resid_norm_fwd(x, out0, gate[0], nw1, shift[1], scale[1])
    p1 = _mm_nn(hn1, g_scin, F32, "proj_sc")
    scw8 = _pad_rows(sc_conv_f, 8)
    y1 = _sc_mix_fwd(p1, scw8)
    out1 = _mm_nn(y1, w_scout[None], F32, "out1")

    fnw = final_norm_w[None, :]
    dx2, dout1, loss11, dfnw, dgate1 = _loss_head(x1, out1, gate[1], fnw, tgt)
    dy1 = _mm_nt(dout1, w_scout[None], F32, "dy1")
    gw_scout = _mm_tn(y1, dout1, BF16, "gw_scout")[0].reshape(NDEV, d // NDEV, d)
    dpb, dpc, dpx, dpg, dscw8 = _sc_mix_bwd(p1, scw8, dy1)
    dp1 = jnp.concatenate([dpb, dpc, dpx, dpg], axis=1)
    dhn1 = _mm_nt(dp1, g_scin, F32, "dhn1", tk=2048)
    gw_scin = _mm_tn(hn1, dp1, BF16, "gw_scin", oj=NDEV)
    dx1, dout0, dshift1, dscale1, dnw1, dgate0 = _norm_bwd(
        x1, rinv1, [dhn1], nw1, scale[1], "norm1_bwd", dx_in=dx2, resid=(out0, gate[0]))

    dy0 = _mm_nt(dout0, w_about[None], F32, "dy0")
    gw_about = _mm_tn(y0, dout0, BF16, "gw_about")[0].reshape(NDEV, (dnw + lw) // NDEV, d)
    do, dh_r, dgates, ddnn = _mix_post_bwd(o_dn, hsum, proj, dnn, dy0)

    dq, dk, dv, dscg = _delta_bwd(qkv, scg, states, do)
    dsc = _scg_layout_bwd(dscg)
    dpba, dpar = _dn_gates_bwd(pba_l, par, dsc)
    dpq, dcw_q = _dn_pre_bwd(proj, convw8, dq, 0)
    dpk, dcw_k = _dn_pre_bwd(proj, convw8, dk, 1)
    dpv, dcw_v = _dn_pre_bwd(proj, convw8, dv, 2)

    dh_cat = jnp.concatenate([jnp.zeros((CTX_LEN, lw), F32), _to_col_major(dh_r)], axis=0)
    da0, db0 = _lru_scan_bwd(la_a, h_dir[0], dh_cat, 0)
    da1, db1 = _lru_scan_bwd(la_a, h_dir[1], dh_cat, 1)
    dxc, dw_r, dw_i, dlvecs = _lru_gates_bwd(xc, w_r, w_i, lvecs, jnp.stack([da0, da1]), jnp.stack([db0, db1]))
    dxl_cat, dlconv8 = _lru_conv_bwd(xl_cat, lconv8, dxc)
    dxl = jnp.concatenate([dxl_cat[:CTX_LEN], _to_raster(dxl_cat[CTX_LEN:])], axis=0).astype(BF16)

    dproj = jnp.concatenate([dpq, dpk, dpv, dxl, dgates], axis=1)
    if hh2 < 32:
        dpba_w = jnp.concatenate([dpba[:, :hh2], dpba[:, 32:32 + hh2], jnp.zeros((r, LANE - 2 * hh2), BF16)], axis=1)
    else:
        dpba_w = dpba
    dhn0_a = _mm_nt(dproj, w_main[None], F32, "dhn0_main")
    dhn0_b = _mm_nt(dpba_w, w_ba[None], F32, "dhn0_ba")
    gw_main = _mm_tn(hn0, dproj, BF16, "gw_main")[0]
    gw_ba = _mm_tn(hn0, dpba_w, BF16, "gw_ba")[0]
    gw_full = jnp.concatenate([gw_main[:, :off_beta], gw_ba[:, :2 * hh2], gw_main[:, off_beta:]], axis=1)
    gw_abin = gw_full.reshape(d, NDEV, ab_in // NDEV).transpose(1, 0, 2)

    nt = _div(SEQ, ROW_TILE // 2, 16)
    grad_x, dshift0, dscale0, dnw0_x = _norm_bwd(x, rinv_x, [dhn0_a, dhn0_b], nw0, scale[0], "norm0_bwd_x",
                                                 row_off=CTX_LEN // nt, dx_in=dx1)
    _, dshift_c, dscale_c, dnw0_c = _norm_bwd(ctx, rinv_c, [dhn0_a, dhn0_b], nw0, scale_c, "norm0_bwd_ctx")

    s_abin, s_about, s_scin, s_scout = _exchange([gw_abin, gw_about, gw_scin, gw_scout], "scatter_grads", scatter=True)
    big = {}
    big["ab_w_in"] = _adamw_slots(ab_w_in[0], s_abin, m_ab_w_in[0], v_ab_w_in[0], "adamw_ab_w_in")
    big["ab_w_out"] = _adamw_slots(ab_w_out[0], s_about, m_ab_w_out[0], v_ab_w_out[0], "adamw_ab_w_out")
    big["sc_w_in"] = _adamw_slots(sc_w_in[0], s_scin, m_sc_w_in[0], v_sc_w_in[0], "adamw_sc_w_in")
    big["sc_w_out"] = _adamw_slots(sc_w_out[0], s_scout, m_sc_w_out[0], v_sc_w_out[0], "adamw_sc_w_out")

    dmod_own = jnp.stack([jnp.concatenate([dshift0, dscale0, dgate0], axis=1)[0],
                          jnp.concatenate([dshift1, dscale1, dgate1], axis=1)[0]])
    dmod_c = jnp.concatenate([dshift_c, dscale_c], axis=1)[0]
    g_qkv_conv = jnp.concatenate([dcw_q[:DN_CONV], dcw_k[:DN_CONV], dcw_v[:DN_CONV]], axis=1)
    small = [
        ("loss", loss11.reshape(1)),
        ("norm_w", jnp.concatenate([dnw0_x + dnw0_c, dnw1], axis=0)),
        ("dmod_own", dmod_own),
        ("dmod_c", dmod_c),
        ("qkv_conv", g_qkv_conv),
        ("a_log", dpar[1, 32:32 + hh2].reshape(2, DN_HEADS)),
        ("dt_bias", dpar[0, 32:32 + hh2].reshape(2, DN_HEADS)),
        ("dn_norm", ddnn[0]),
        ("lru_conv_w", dlconv8[:LRU_CONV]),
        ("lru_conv_b", dlconv8[LRU_CONV]),
        ("lru_w_r", dw_r),
        ("lru_w_i", dw_i),
        ("lru_b_r", dlvecs[:, 0]),
        ("lru_b_i", dlvecs[:, 1]),
        ("lru_lambda", dlvecs[:, 2]),
        ("sc_conv", dscw8[:SC_CONV]),
        ("final_norm_w", dfnw[0]),
    ]
    names = [n for n, _ in small]
    shapes = [a.shape for _, a in small]
    spk, soffs = _pack([a for _, a in small])
    (sg,) = _exchange([spk], "gather_small_grads", scatter=False)
    tot = dict(zip(names, _unpack(_sum_slots(sg, "sum_small_grads"), soffs, shapes)))
    i_own = names.index("dmod_own")
    dmod_all = jnp.stack([_unpack(sg[j], soffs[i_own:i_own + 1], shapes[i_own:i_own + 1])[0] for j in range(NDEV)])

    dmc_full = jnp.concatenate([tot["dmod_c"], jnp.zeros((d,), F32)])
    rows9 = jnp.concatenate([dmod_all, jnp.stack([dmc_full, jnp.zeros((3 * d,), F32)])[None],
                             jnp.zeros((7, 2, 3 * d), F32)], axis=0)
    dm = lax.dynamic_slice_in_dim(rows9, me * mcols, mcols, axis=2).transpose(1, 0, 2)
    g_modw, dl_modw, nm_modw, nv_modw, gcc = _mod_bwd(c16, dm, mod_w, m_mod_w, v_mod_w)
    gcc_pk, _ = _pack([gcc[0, :, 0]])
    (gcc_g,) = _exchange([gcc_pk], "gather_c_ctx", scatter=False)
    gcc_sum = _sum_slots(gcc_g, "sum_c_ctx").reshape(-1)[:d]
    g_c_ctx_pre = gcc_sum

    grad_mod_b = tot["dmod_own"] + jnp.stack([dmc_full, jnp.zeros((3 * d,), F32)])

    def shard(a):
        wd = a.shape[-1] // NDEV
        return lax.dynamic_slice_in_dim(a, me * wd, wd, axis=a.ndim - 1)

    small_w = [
        ("c_ctx", c_ctx, None, m_c_ctx, v_c_ctx),
        ("mod_b", mod_b, grad_mod_b, m_mod_b, v_mod_b),
        ("norm_w", norm_w, tot["norm_w"], m_norm_w, v_norm_w),
        ("ab_qkv_conv", ab_qkv_conv, shard(tot["qkv_conv"])[None], m_ab_qkv_conv, v_ab_qkv_conv),
        ("ab_a_log", ab_a_log, tot["a_log"][None], m_ab_a_log, v_ab_a_log),
        ("ab_dt_bias", ab_dt_bias, tot["dt_bias"][None], m_ab_dt_bias, v_ab_dt_bias),
        ("ab_dn_norm", ab_dn_norm, tot["dn_norm"][None], m_ab_dn_norm, v_ab_dn_norm),
        ("ab_lru_conv_w", ab_lru_conv_w, shard(tot["lru_conv_w"])[None], m_ab_lru_conv_w, v_ab_lru_conv_w),
        ("ab_lru_conv_b", ab_lru_conv_b, tot["lru_conv_b"][None], m_ab_lru_conv_b, v_ab_lru_conv_b),
        ("ab_lru_w_r", ab_lru_w_r, tot["lru_w_r"][None], m_ab_lru_w_r, v_ab_lru_w_r),
        ("ab_lru_b_r", ab_lru_b_r, shard(tot["lru_b_r"])[None], m_ab_lru_b_r, v_ab_lru_b_r),
        ("ab_lru_w_i", ab_lru_w_i, tot["lru_w_i"][None], m_ab_lru_w_i, v_ab_lru_w_i),
        ("ab_lru_b_i", ab_lru_b_i, shard(tot["lru_b_i"])[None], m_ab_lru_b_i, v_ab_lru_b_i),
        ("ab_lru_lambda", ab_lru_lambda, shard(tot["lru_lambda"])[None], m_ab_lru_lambda, v_ab_lru_lambda),
        ("sc_conv", sc_conv, shard(tot["sc_conv"])[None], m_sc_conv, v_sc_conv),
        ("final_norm_w", final_norm_w, tot["final_norm_w"], m_final_norm_w, v_final_norm_w),
    ]
    sg_cc = jax.nn.sigmoid(c_ctx)
    g_c_ctx = g_c_ctx_pre * (sg_cc * (1.0 + c_ctx * (1.0 - sg_cc)))
    small_w[0] = ("c_ctx", c_ctx, g_c_ctx, m_c_ctx, v_c_ctx)
    wshapes = [w.shape for _, w, _, _, _ in small_w]
    wpk, woffs = _pack([w for _, w, _, _, _ in small_w])
    gpk, _ = _pack([g for _, _, g, _, _ in small_w])
    mpk, _ = _pack([m for _, _, _, m, _ in small_w])
    vpk, _ = _pack([v for _, _, _, _, v in small_w])
    dl_pk, nm_pk, nv_pk = _adamw_flat(wpk, gpk, mpk, vpk, "adamw_small")
    sm_g = {n: g for n, _, g, _, _ in small_w}
    sm_dl = dict(zip([n for n, *_ in small_w], _unpack(dl_pk, woffs, wshapes)))
    sm_nm = dict(zip([n for n, *_ in small_w], _unpack(nm_pk, woffs, wshapes)))
    sm_nv = dict(zip([n for n, *_ in small_w], _unpack(nv_pk, woffs, wshapes)))

    grads, deltas, new_m, new_v = {}, {}, {}, {}
    for n in sm_g:
        grads[n], deltas[n], new_m[n], new_v[n] = sm_g[n], sm_dl[n], sm_nm[n], sm_nv[n]
    grads["mod_w"], deltas["mod_w"], new_m["mod_w"], new_v["mod_w"] = g_modw, dl_modw, nm_modw, nv_modw
    for n in ("ab_w_in", "ab_w_out", "sc_w_in", "sc_w_out"):
        g, dl, m2, v2 = big[n]
        grads[n], deltas[n], new_m[n], new_v[n] = g[None], dl[None], m2[None], v2[None]

    order = ["c_ctx", "mod_w", "mod_b", "norm_w", "ab_w_in", "ab_qkv_conv", "ab_a_log", "ab_dt_bias", "ab_dn_norm",
             "ab_lru_conv_w", "ab_lru_conv_b", "ab_lru_w_r", "ab_lru_b_r", "ab_lru_w_i", "ab_lru_b_i", "ab_lru_lambda",
             "ab_w_out", "sc_w_in", "sc_conv", "sc_w_out", "final_norm_w"]
    loss = tot["loss"][0]
    return (loss, grad_x[None], *[grads[n] for n in order], *[deltas[n] for n in order],
            *[new_m[n] for n in order], *[new_v[n] for n in order])
```

```python
import functools

import jax
import jax.numpy as jnp
from jax import lax
from jax.experimental import pallas as pl
from jax.experimental.pallas import tpu as pltpu

F32 = jnp.float32
BF16 = jnp.bfloat16

NDEV = 8
D_MODEL = 4096
SEQ = 4096
CTX_LEN = 256
GRID_W = 64
EPS = 1e-6
DN_HEADS = 16
DN_HEAD_DIM = 128
DN_CONV = 4
DN_CHUNK = 64
LRU_WIDTH = 2048
LRU_BLOCKS = 16
LRU_CONV = 4
LRU_C = 8.0
SC_CONV = 3
ADAM_LR = 0.001
ADAM_B1 = 0.9
ADAM_B2 = 0.999
ADAM_EPS = 1e-08
ADAM_WD = 0.01
ADAM_STEP = 10

ROW_TILE = 256
HEADS_PER_STEP = 16
LANE = 128
VMEM_LIMIT = 56 * 1024 * 1024


def _dims():
    d = D_MODEL
    dnw = DN_HEADS * DN_HEAD_DIM
    lw = LRU_WIDTH
    off_lru = 3 * dnw
    off_beta = off_lru + lw
    ab_state = off_beta + 4 * DN_HEADS
    ab_in = ab_state + dnw + lw
    return dict(d=d, dnw=dnw, lw=lw, off_lru=off_lru, off_beta=off_beta, ab_state=ab_state, ab_in=ab_in,
                nmain=3 * dnw + lw + dnw + lw, r=CTX_LEN + SEQ, lbd=lw // LRU_BLOCKS)


def _div(n, pref, mult):
    best = None
    for t in range(mult, min(n, pref) + 1, mult):
        if n % t == 0:
            best = t
    return best if best is not None else n


def _cp(sem, vmem=VMEM_LIMIT):
    return pltpu.CompilerParams(dimension_semantics=sem, vmem_limit_bytes=vmem)


def _silu(z):
    return z * jax.nn.sigmoid(z)


def _dsilu(z):
    s = jax.nn.sigmoid(z)
    return s * (1.0 + z * (1.0 - s))


def _softplus(z):
    return jnp.maximum(z, 0.0) + jnp.log1p(jnp.exp(-jnp.abs(z)))


def _neg_expm1(z):
    series = -z * (1.0 + z * (0.5 + z * (1.0 / 6.0 + z * (1.0 / 24.0))))
    return jnp.where(z > -0.05, series, 1.0 - jnp.exp(z))


def _me():
    return 4 * lax.axis_index("x") + 2 * lax.axis_index("y") + lax.axis_index("c")


def _exchange(srcs, name, scatter):
    n = len(srcs)

    def body(*refs):
        src_refs, out_refs = refs[:n], refs[n:2 * n]
        send_sems, recv_sems, local_sems = refs[2 * n:]
        x, y, c = lax.axis_index("x"), lax.axis_index("y"), lax.axis_index("c")
        me = 4 * x + 2 * y + c
        copies = []
        for i in range(n):
            mine = src_refs[i].at[me] if scatter else src_refs[i]
            lc = pltpu.make_async_copy(mine, out_refs[i].at[me], local_sems.at[i])
            lc.start()
            copies.append(lc)
            for k in range(1, NDEV):
                px = 1 - x if (k >> 2) & 1 else x
                py = 1 - y if (k >> 1) & 1 else y
                pc = 1 - c if k & 1 else c
                peer = 4 * px + 2 * py + pc
                src = src_refs[i].at[peer] if scatter else src_refs[i]
                cp = pltpu.make_async_remote_copy(
                    src_ref=src, dst_ref=out_refs[i].at[me], send_sem=send_sems.at[i, k - 1],
                    recv_sem=recv_sems.at[i, k - 1], device_id=(px, py, pc), device_id_type=pl.DeviceIdType.MESH)
                cp.start()
                copies.append(cp)
        for cp in copies:
            cp.wait()

    any_spec = pl.BlockSpec(memory_space=pl.ANY)
    out_shape = tuple(jax.ShapeDtypeStruct(s.shape if scatter else (NDEV,) + s.shape, s.dtype) for s in srcs)
    outs = pl.pallas_call(
        body, name=name, out_shape=out_shape, in_specs=[any_spec] * n, out_specs=tuple([any_spec] * n),
        scratch_shapes=[pltpu.SemaphoreType.DMA((n, NDEV - 1)), pltpu.SemaphoreType.DMA((n, NDEV - 1)),
                        pltpu.SemaphoreType.DMA((n,))],
    )(*srcs)
    return list(outs)


def _sum_slots(g, name):
    _, rows, cols = g.shape
    tm = _div(rows, 1024, 8)

    def body(g_ref, o_ref):
        acc = g_ref[0]
        for j in range(1, NDEV):
            acc = acc + g_ref[j]
        o_ref[...] = acc

    return pl.pallas_call(
        body, name=name, grid=(rows // tm,), out_shape=jax.ShapeDtypeStruct((rows, cols), F32),
        in_specs=[pl.BlockSpec((NDEV, tm, cols), lambda i: (0, i, 0))],
        out_specs=pl.BlockSpec((tm, cols), lambda i: (i, 0)), compiler_params=_cp(("arbitrary",)))(g)


def _mm_nn(a, b3, out_dtype, name, tm=256, tn=1024):
    m_, k_ = a.shape
    j_, _, nj = b3.shape
    tm = _div(m_, tm, 16)
    tn = _div(nj, tn, LANE)
    per = nj // tn

    def body(a_ref, b_ref, o_ref):
        o_ref[...] = jnp.dot(a_ref[...], b_ref[0], preferred_element_type=F32).astype(o_ref.dtype)

    return pl.pallas_call(
        body, name=name, grid=(j_ * per, m_ // tm), out_shape=jax.ShapeDtypeStruct((m_, j_ * nj), out_dtype),
        in_specs=[pl.BlockSpec((tm, k_), lambda n, m: (m, 0)),
                  pl.BlockSpec((1, k_, tn), lambda n, m: (n // per, 0, n % per))],
        out_specs=pl.BlockSpec((tm, tn), lambda n, m: (m, n)),
        compiler_params=_cp(("arbitrary", "arbitrary")))(a, b3)


def _mm_nt(a, b3, out_dtype, name, tm=256, tn=1024, tk=2048):
    m_, k_ = a.shape
    j_, n_, kj = b3.shape
    tm = _div(m_, tm, 16)
    tn = _div(n_, tn, LANE)
    tk = _div(kj, tk, LANE)
    perk = kj // tk
    nk = k_ // tk

    def body(a_ref, b_ref, o_ref, acc_ref):
        k = pl.program_id(2)

        @pl.when(k == 0)
        def _():
            acc_ref[...] = jnp.zeros_like(acc_ref)

        acc_ref[...] += lax.dot_general(a_ref[...], b_ref[0], (((1,), (1,)), ((), ())), preferred_element_type=F32)

        @pl.when(k == nk - 1)
        def _():
            o_ref[...] = acc_ref[...].astype(o_ref.dtype)

    return pl.pallas_call(
        body, name=name, grid=(n_ // tn, m_ // tm, nk), out_shape=jax.ShapeDtypeStruct((m_, n_), out_dtype),
        in_specs=[pl.BlockSpec((tm, tk), lambda n, m, k: (m, k)),
                  pl.BlockSpec((1, tn, tk), lambda n, m, k: (k // perk, n, k % perk))],
        out_specs=pl.BlockSpec((tm, tn), lambda n, m, k: (m, n)),
        scratch_shapes=[pltpu.VMEM((tm, tn), F32)],
        compiler_params=_cp(("arbitrary", "arbitrary", "arbitrary")))(a, b3)


def _mm_tn(a, b, out_dtype, name, oj=1, tm=1024, tn=512):
    k_, m_ = a.shape
    _, n_ = b.shape
    nj = n_ // oj
    tm = _div(m_, tm, LANE)
    tn = _div(nj, tn, LANE)
    per = nj // tn

    def body(a_ref, b_ref, o_ref):
        o_ref[0] = lax.dot_general(a_ref[...], b_ref[...], (((0,), (0,)), ((), ())),
                                   preferred_element_type=F32).astype(o_ref.dtype)

    return pl.pallas_call(
        body, name=name, grid=(m_ // tm, n_ // tn), out_shape=jax.ShapeDtypeStruct((oj, m_, nj), out_dtype),
        in_specs=[pl.BlockSpec((k_, tm), lambda m, n: (0, m)), pl.BlockSpec((k_, tn), lambda m, n: (0, n))],
        out_specs=pl.BlockSpec((1, tm, tn), lambda m, n: (n // per, m, n % per)),
        compiler_params=_cp(("arbitrary", "arbitrary")))(a, b)


def _mod_fwd(c16, w, bias):
    _, d, cols = w.shape
    tn = _div(cols, 512, LANE)

    def body(c_ref, w_ref, b_ref, o_ref):
        sc = _silu(c_ref[...])
        o_ref[0] = jnp.dot(sc, w_ref[0], preferred_element_type=F32) + b_ref[0]

    return pl.pallas_call(
        body, name="mod_fwd", grid=(2, cols // tn), out_shape=jax.ShapeDtypeStruct((2, 16, cols), F32),
        in_specs=[pl.BlockSpec((16, d), lambda l, n: (0, 0)), pl.BlockSpec((1, d, tn), lambda l, n: (l, 0, n)),
                  pl.BlockSpec((1, 1, tn), lambda l, n: (l, 0, n))],
        out_specs=pl.BlockSpec((1, 16, tn), lambda l, n: (l, 0, n)),
        compiler_params=_cp(("arbitrary", "arbitrary")))(c16, w, bias)


def _adam(w, g, m, v):
    m2 = ADAM_B1 * m + (1.0 - ADAM_B1) * g
    v2 = ADAM_B2 * v + (1.0 - ADAM_B2) * (g * g)
    m_hat = m2 / (1.0 - ADAM_B1 ** ADAM_STEP)
    v_hat = v2 / (1.0 - ADAM_B2 ** ADAM_STEP)
    delta = -ADAM_LR * (m_hat / (jnp.sqrt(v_hat) + ADAM_EPS) + ADAM_WD * w)
    return delta, m2, v2


def _mod_bwd(c16, dm, w, m, v):
    _, d, cols = w.shape
    tm = _div(d, 256, 8)

    def body(c_ref, dm_ref, w_ref, m_ref, v_ref, g_ref, dl_ref, nm_ref, nv_ref, gc_ref):
        sc = _silu(c_ref[...])
        g = lax.dot_general(sc, dm_ref[0], (((0,), (0,)), ((), ())), preferred_element_type=F32,
                            precision=lax.Precision.HIGHEST)
        wv = w_ref[0]
        dl, m2, v2 = _adam(wv, g, m_ref[0], v_ref[0])
        g_ref[0] = g
        dl_ref[0] = dl
        nm_ref[0] = m2
        nv_ref[0] = v2
        gc_ref[0] = jnp.sum(wv * dm_ref[0, 8:9, :], axis=1, keepdims=True)

    big = pl.BlockSpec((1, tm, cols), lambda l, i: (l, i, 0))
    shp = jax.ShapeDtypeStruct(w.shape, F32)
    return pl.pallas_call(
        body, name="mod_bwd", grid=(2, d // tm),
        out_shape=(shp, shp, shp, shp, jax.ShapeDtypeStruct((2, d, 1), F32)),
        in_specs=[pl.BlockSpec((16, tm), lambda l, i: (0, i)), pl.BlockSpec((1, 16, cols), lambda l, i: (l, 0, 0)),
                  big, big, big],
        out_specs=(big, big, big, big, pl.BlockSpec((1, tm, 1), lambda l, i: (l, i, 0))),
        compiler_params=_cp(("arbitrary", "arbitrary")))(c16, dm, w, m, v)


def _row_spec(tm, d, off=0):
    return pl.BlockSpec((tm, d), lambda i: (i + off, 0))


def _vec_spec(d):
    return pl.BlockSpec((1, d), lambda i: (0, 0))


def _norm_fwd(x, nw, shift, scale, name):
    n, d = x.shape
    tm = _div(n, ROW_TILE, 16)

    def body(x_ref, nw_ref, sh_ref, sc_ref, hn_ref, r_ref):
        xv = x_ref[...]
        r = lax.rsqrt(jnp.mean(xv * xv, axis=1, keepdims=True) + EPS)
        hn_ref[...] = (xv * r * nw_ref[...] * (1.0 + sc_ref[...]) + sh_ref[...]).astype(BF16)
        r_ref[...] = r

    return pl.pallas_call(
        body, name=name, grid=(n // tm,),
        out_shape=(jax.ShapeDtypeStruct((n, d), BF16), jax.ShapeDtypeStruct((n, 1), F32)),
        in_specs=[_row_spec(tm, d), _vec_spec(d), _vec_spec(d), _vec_spec(d)],
        out_specs=(_row_spec(tm, d), _row_spec(tm, 1)), compiler_params=_cp(("arbitrary",)))(x, nw, shift, scale)


def _resid_norm_fwd(x, out0, gate, nw, shift, scale):
    n, d = x.shape
    tm = _div(n, ROW_TILE, 16)

    def body(x_ref, o_ref, g_ref, nw_ref, sh_ref, sc_ref, x1_ref, hn_ref, r_ref):
        xv = x_ref[...] + g_ref[...] * o_ref[...]
        r = lax.rsqrt(jnp.mean(xv * xv, axis=1, keepdims=True) + EPS)
        x1_ref[...] = xv
        hn_ref[...] = (xv * r * nw_ref[...] * (1.0 + sc_ref[...]) + sh_ref[...]).astype(BF16)
        r_ref[...] = r

    return pl.pallas_call(
        body, name="resid_norm1", grid=(n // tm,),
        out_shape=(jax.ShapeDtypeStruct((n, d), F32), jax.ShapeDtypeStruct((n, d), BF16),
                   jax.ShapeDtypeStruct((n, 1), F32)),
        in_specs=[_row_spec(tm, d), _row_spec(tm, d), _vec_spec(d), _vec_spec(d), _vec_spec(d), _vec_spec(d)],
        out_specs=(_row_spec(tm, d), _row_spec(tm, d), _row_spec(tm, 1)),
        compiler_params=_cp(("arbitrary",)))(x, out0, gate, nw, shift, scale)


def _loss_head(x1, out1, gate1, fnw, tgt):
    n, d = x1.shape
    tm = _div(n, ROW_TILE, 16)
    nsteps = n // tm

    def body(x_ref, o_ref, g_ref, w_ref, t_ref, dx_ref, do_ref, loss_ref, dw_ref, dg_ref, lacc_ref):
        i = pl.program_id(0)

        @pl.when(i == 0)
        def _():
            lacc_ref[...] = jnp.zeros_like(lacc_ref)
            dw_ref[...] = jnp.zeros_like(dw_ref)
            dg_ref[...] = jnp.zeros_like(dg_ref)

        ov = o_ref[...]
        gv = g_ref[...]
        wv = w_ref[...]
        xv = x_ref[...] + gv * ov
        r = lax.rsqrt(jnp.mean(xv * xv, axis=1, keepdims=True) + EPS)
        xh = xv * r
        diff = xh * wv - t_ref[...]
        lacc_ref[...] += jnp.sum(diff * diff, axis=0, keepdims=True)
        dy = diff * (1.0 / d)
        dw_ref[...] += jnp.sum(dy * xh, axis=0, keepdims=True)
        dxh = dy * wv
        dx = r * (dxh - xh * jnp.mean(xh * dxh, axis=1, keepdims=True))
        dx_ref[...] = dx
        dg_ref[...] += jnp.sum(dx * ov, axis=0, keepdims=True)
        do_ref[...] = (gv * dx).astype(BF16)

        @pl.when(i == nsteps - 1)
        def _():
            loss_ref[...] = (0.5 / d) * jnp.sum(lacc_ref[...], axis=1, keepdims=True)

    vec = jax.ShapeDtypeStruct((1, d), F32)
    return pl.pallas_call(
        body, name="loss_head", grid=(nsteps,),
        out_shape=(jax.ShapeDtypeStruct((n, d), F32), jax.ShapeDtypeStruct((n, d), BF16),
                   jax.ShapeDtypeStruct((1, 1), F32), vec, vec),
        in_specs=[_row_spec(tm, d), _row_spec(tm, d), _vec_spec(d), _vec_spec(d), _row_spec(tm, d)],
        out_specs=(_row_spec(tm, d), _row_spec(tm, d), pl.BlockSpec((1, 1), lambda i: (0, 0)), _vec_spec(d),
                   _vec_spec(d)),
        scratch_shapes=[pltpu.VMEM((1, d), F32)], compiler_params=_cp(("arbitrary",)))(x1, out1, gate1, fnw, tgt)


def _norm_bwd(x, rinv, dhns, nw, scale, name, row_off=0, dx_in=None, resid=None):
    n, d = x.shape
    tm = _div(n, ROW_TILE // 2, 16)
    nd = len(dhns)
    has_in = dx_in is not None
    has_res = resid is not None

    def body(*refs):
        it = iter(refs)
        x_ref, r_ref = next(it), next(it)
        dh_refs = [next(it) for _ in range(nd)]
        nw_ref, sc_ref = next(it), next(it)
        dxin_ref = next(it) if has_in else None
        o_ref, g_ref = (next(it), next(it)) if has_res else (None, None)
        dx_ref = next(it)
        do_ref = next(it) if has_res else None
        dsh_ref, dsc_ref, dnw_ref = next(it), next(it), next(it)
        dg_ref = next(it) if has_res else None
        i = pl.program_id(0)

        @pl.when(i == 0)
        def _():
            dsh_ref[...] = jnp.zeros_like(dsh_ref)
            dsc_ref[...] = jnp.zeros_like(dsc_ref)
            dnw_ref[...] = jnp.zeros_like(dnw_ref)
            if has_res:
                dg_ref[...] = jnp.zeros_like(dg_ref)

        dh = dh_refs[0][...]
        for rr in dh_refs[1:]:
            dh = dh + rr[...]
        r = r_ref[...]
        xh = x_ref[...] * r
        nwv = nw_ref[...]
        mod = 1.0 + sc_ref[...]
        dsh_ref[...] += jnp.sum(dh, axis=0, keepdims=True)
        dhx = dh * xh
        dsc_ref[...] += jnp.sum(dhx * nwv, axis=0, keepdims=True)
        dnw_ref[...] += jnp.sum(dhx * mod, axis=0, keepdims=True)
        dxh = dh * (nwv * mod)
        dx = r * (dxh - xh * jnp.mean(xh * dxh, axis=1, keepdims=True))
        if has_in:
            dx = dx + dxin_ref[...]
        dx_ref[...] = dx
        if has_res:
            dg_ref[...] += jnp.sum(dx * o_ref[...], axis=0, keepdims=True)
            do_ref[...] = (g_ref[...] * dx).astype(BF16)

    ins = [x, rinv] + list(dhns) + [nw, scale]
    in_specs = [_row_spec(tm, d), _row_spec(tm, 1)] + [_row_spec(tm, d, row_off)] * nd + [_vec_spec(d), _vec_spec(d)]
    if has_in:
        ins.append(dx_in)
        in_specs.append(_row_spec(tm, d))
    if has_res:
        ins += list(resid)
        in_specs += [_row_spec(tm, d), _vec_spec(d)]
    vec = jax.ShapeDtypeStruct((1, d), F32)
    out_shape = [jax.ShapeDtypeStruct((n, d), F32)]
    out_specs = [_row_spec(tm, d)]
    if has_res:
        out_shape.append(jax.ShapeDtypeStruct((n, d), BF16))
        out_specs.append(_row_spec(tm, d))
    out_shape += [vec, vec, vec]
    out_specs += [_vec_spec(d)] * 3
    if has_res:
        out_shape.append(vec)
        out_specs.append(_vec_spec(d))
    return pl.pallas_call(
        body, name=name, grid=(n // tm,), out_shape=tuple(out_shape), in_specs=in_specs, out_specs=tuple(out_specs),
        compiler_params=_cp(("arbitrary",)))(*ins)


def _seg_masks(rows, width, seg_bounds, offs):
    t = lax.broadcasted_iota(jnp.int32, (rows, width), 0)
    s0 = jnp.zeros_like(t)
    s1 = jnp.full_like(t, rows)
    for (a, b) in seg_bounds:
        inside = (t >= a) & (t < b)
        s0 = jnp.where(inside, a, s0)
        s1 = jnp.where(inside, b, s1)
    return {off: ((t + off >= s0) & (t + off < s1)) for off in offs}


def _shift_rows(v, off, valid):
    rows = v.shape[0]
    return jnp.where(valid, pltpu.roll(v, (-off) % rows, axis=0), 0.0)


def _conv_fwd_val(xv, wv, taps, masks):
    acc = None
    for tap, off in taps:
        xs = xv if off == 0 else _shift_rows(xv, off, masks[off])
        term = wv[tap:tap + 1, :] * xs
        acc = term if acc is None else acc + term
    return acc


def _conv_bwd_val(xv, dacc, wv, taps, masks):
    dx = None
    dws = []
    for tap, off in taps:
        xs = xv if off == 0 else _shift_rows(xv, off, masks[off])
        dws.append(jnp.sum(dacc * xs, axis=0, keepdims=True))
        ds = dacc if off == 0 else _shift_rows(dacc, -off, masks[-off])
        term = wv[tap:tap + 1, :] * ds
        dx = term if dx is None else dx + term
    return dx, dws


def _taps(k):
    left = k // 2
    return [(j, j - left) for j in range(k)]


def _seq_bounds():
    return [(0, CTX_LEN), (CTX_LEN, CTX_LEN + SEQ)]


def _dn_pre_fwd(proj, convw):
    dd = _dims()
    r = dd["r"]
    nstrip = 3 * DN_HEADS
    taps = _taps(DN_CONV)
    offs = [o for _, o in taps if o != 0]

    def body(x_ref, w_ref, o_ref):
        kind = pl.program_id(0) // DN_HEADS
        masks = _seg_masks(r, DN_HEAD_DIM, _seq_bounds(), offs)
        y = _silu(_conv_fwd_val(x_ref[...], w_ref[...], taps, masks))
        inv = lax.rsqrt(jnp.sum(y * y, axis=1, keepdims=True) + EPS)
        scale = jnp.where(kind == 0, inv * (DN_HEAD_DIM ** -0.5), jnp.where(kind == 1, inv, 1.0))
        o_ref[...] = y * scale

    return pl.pallas_call(
        body, name="dn_pre_fwd", grid=(nstrip,), out_shape=jax.ShapeDtypeStruct((r, 3 * dd["dnw"]), F32),
        in_specs=[pl.BlockSpec((r, DN_HEAD_DIM), lambda j: (0, j)), pl.BlockSpec((8, DN_HEAD_DIM), lambda j: (0, j))],
        out_specs=pl.BlockSpec((r, DN_HEAD_DIM), lambda j: (0, j)), compiler_params=_cp(("arbitrary",)))(proj, convw)


def _dn_pre_bwd(proj, convw, dqkv, kind):
    dd = _dims()
    r = dd["r"]
    taps = _taps(DN_CONV)
    offs = sorted({o for _, o in taps if o != 0} | {-o for _, o in taps if o != 0})
    cscale = DN_HEAD_DIM ** -0.5 if kind == 0 else 1.0

    def body(x_ref, w_ref, d_ref, dx_ref, dw_ref):
        masks = _seg_masks(r, DN_HEAD_DIM, _seq_bounds(), offs)
        xv = x_ref[...]
        wv = w_ref[...]
        acc = _conv_fwd_val(xv, wv, taps, masks)
        dout = d_ref[0] + d_ref[1]
        if kind == 2:
            dy = dout
        else:
            y = _silu(acc)
            inv = lax.rsqrt(jnp.sum(y * y, axis=1, keepdims=True) + EPS)
            dy = cscale * inv * (dout - y * (inv * inv) * jnp.sum(dout * y, axis=1, keepdims=True))
        dacc = dy * _dsilu(acc)
        dx, dws = _conv_bwd_val(xv, dacc, wv, taps, masks)
        dx_ref[...] = dx.astype(BF16)
        dw_ref[...] = jnp.concatenate(dws + [jnp.zeros((8 - len(dws), DN_HEAD_DIM), F32)], axis=0)

    base = kind * DN_HEADS
    return pl.pallas_call(
        body, name=f"dn_pre_bwd{kind}", grid=(DN_HEADS,),
        out_shape=(jax.ShapeDtypeStruct((r, dd["dnw"]), BF16), jax.ShapeDtypeStruct((8, dd["dnw"]), F32)),
        in_specs=[pl.BlockSpec((r, DN_HEAD_DIM), lambda j: (0, base + j)),
                  pl.BlockSpec((8, DN_HEAD_DIM), lambda j: (0, base + j)),
                  pl.BlockSpec((2, r, DN_HEAD_DIM), lambda j: (0, 0, j))],
        out_specs=(pl.BlockSpec((r, DN_HEAD_DIM), lambda j: (0, j)), pl.BlockSpec((8, DN_HEAD_DIM), lambda j: (0, j))),
        compiler_params=_cp(("arbitrary",)))(proj, convw, dqkv)


def _lru_conv_fwd(xl, convw):
    r, lw = xl.shape
    taps = _taps(LRU_CONV)
    offs = [o for _, o in taps if o != 0]

    def body(x_ref, w_ref, o_ref):
        masks = _seg_masks(r, LANE, _seq_bounds(), offs)
        wv = w_ref[...]
        o_ref[...] = _conv_fwd_val(x_ref[...], wv, taps, masks) + wv[4:5, :]

    return pl.pallas_call(
        body, name="lru_conv_fwd", grid=(lw // LANE,), out_shape=jax.ShapeDtypeStruct((r, lw), F32),
        in_specs=[pl.BlockSpec((r, LANE), lambda j: (0, j)), pl.BlockSpec((8, LANE), lambda j: (0, j))],
        out_specs=pl.BlockSpec((r, LANE), lambda j: (0, j)), compiler_params=_cp(("arbitrary",)))(xl, convw)


def _lru_conv_bwd(xl, convw, dxc):
    r, lw = xl.shape
    taps = _taps(LRU_CONV)
    offs = sorted({o for _, o in taps if o != 0} | {-o for _, o in taps if o != 0})

    def body(x_ref, w_ref, d_ref, dx_ref, dw_ref):
        masks = _seg_masks(r, LANE, _seq_bounds(), offs)
        dacc = d_ref[0] + d_ref[1]
        dx, dws = _conv_bwd_val(x_ref[...], dacc, w_ref[...], taps, masks)
        dx_ref[...] = dx
        db = jnp.sum(dacc, axis=0, keepdims=True)
        dw_ref[...] = jnp.concatenate(dws + [db, jnp.zeros((3, LANE), F32)], axis=0)

    return pl.pallas_call(
        body, name="lru_conv_bwd", grid=(lw // LANE,),
        out_shape=(jax.ShapeDtypeStruct((r, lw), F32), jax.ShapeDtypeStruct((8, lw), F32)),
        in_specs=[pl.BlockSpec((r, LANE), lambda j: (0, j)), pl.BlockSpec((8, LANE), lambda j: (0, j)),
                  pl.BlockSpec((2, r, LANE), lambda j: (0, 0, j))],
        out_specs=(pl.BlockSpec((r, LANE), lambda j: (0, j)), pl.BlockSpec((8, LANE), lambda j: (0, j))),
        compiler_params=_cp(("arbitrary",)))(xl, convw, dxc)


def _tri(rev_lane, n, m):
    r = lax.broadcasted_iota(jnp.int32, (n, n), 0)
    c = lax.broadcasted_iota(jnp.int32, (n, n), 1)
    return (r >= c).astype(F32), (r <= c).astype(F32)


def _hdot(a, b):
    return jnp.dot(a, b, preferred_element_type=F32, precision=lax.Precision.HIGHEST)


def _gate_lanes():
    lane = lax.broadcasted_iota(jnp.int32, (DN_CHUNK, LANE), 1)
    group = lane // 32
    rev = (lane % 32) >= DN_HEADS if DN_HEADS * 2 <= 32 else None
    return group, rev


def _dn_gates_fwd(pba, par):
    r = pba.shape[0]
    nch = r // DN_CHUNK
    hh = 2 * DN_HEADS

    def body(x_ref, p_ref, o_ref):
        lo, up = _tri(None, DN_CHUNK, DN_CHUNK)
        ones = jnp.ones((DN_CHUNK, DN_CHUNK), F32)
        lane = lax.broadcasted_iota(jnp.int32, (DN_CHUNK, LANE), 1)
        group = lane // 32
        rev = (lane % 32) >= DN_HEADS
        dtb = p_ref[0:1, :]
        nea = p_ref[1:2, :]

        def step(i, carry):
            rows = pl.ds(pl.multiple_of(i * DN_CHUNK, DN_CHUNK), DN_CHUNK)
            xv = x_ref[rows, :]
            beta = jax.nn.sigmoid(xv)
            g = jnp.where(group == 1, nea * _softplus(xv + dtb), 0.0)
            gc = jnp.where(rev, _hdot(up, g), _hdot(lo, g))
            gt = _hdot(ones, g)
            out = jnp.where(group == 0, beta, jnp.where(group == 1, g, 0.0))
            out = out + jnp.where(group == 2, pltpu.roll(gc, 32, axis=1), 0.0)
            out = out + jnp.where(group == 3, pltpu.roll(gt, 64, axis=1), 0.0)
            o_ref[rows, :] = out
            return carry

        lax.fori_loop(0, nch, step, 0)

    return pl.pallas_call(body, name="dn_gates_fwd", out_shape=jax.ShapeDtypeStruct((r, LANE), F32),
                          compiler_params=_cp(None))(pba, par)


def _dn_gates_bwd(pba, par, dsc):
    r = pba.shape[0]
    nch = r // DN_CHUNK

    def body(x_ref, p_ref, d_ref, dx_ref, dp_ref, acc_ref):
        lo, up = _tri(None, DN_CHUNK, DN_CHUNK)
        ones = jnp.ones((DN_CHUNK, DN_CHUNK), F32)
        lane = lax.broadcasted_iota(jnp.int32, (DN_CHUNK, LANE), 1)
        group = lane // 32
        rev = (lane % 32) >= DN_HEADS
        dtb = p_ref[0:1, :]
        nea = p_ref[1:2, :]
        acc_ref[...] = jnp.zeros_like(acc_ref)

        def step(i, carry):
            rows = pl.ds(pl.multiple_of(i * DN_CHUNK, DN_CHUNK), DN_CHUNK)
            xv = x_ref[rows, :]
            dv = d_ref[rows, :]
            beta = jax.nn.sigmoid(xv)
            z = xv + dtb
            sp = _softplus(z)
            dgc = jnp.where(group == 1, pltpu.roll(dv, LANE - 32, axis=1), 0.0)
            dgt = jnp.where(group == 1, pltpu.roll(dv, LANE - 64, axis=1), 0.0)
            dg = jnp.where(rev, _hdot(lo, dgc), _hdot(up, dgc)) + _hdot(ones, dgt)
            dg = dg + jnp.where(group == 1, dv, 0.0)
            dal = dg * nea * jax.nn.sigmoid(z)
            dpb = dv * beta * (1.0 - beta)
            dx_ref[rows, :] = jnp.where(group == 0, dpb, jnp.where(group == 1, dal, 0.0)).astype(BF16)
            acc_ref[0:1, :] += jnp.sum(jnp.where(group == 1, dal, 0.0), axis=0, keepdims=True)
            acc_ref[1:2, :] += jnp.sum(jnp.where(group == 1, dg * nea * sp, 0.0), axis=0, keepdims=True)
            return carry

        lax.fori_loop(0, nch, step, 0)
        dp_ref[...] = acc_ref[...]

    return pl.pallas_call(
        body, name="dn_gates_bwd",
        out_shape=(jax.ShapeDtypeStruct((r, LANE), BF16), jax.ShapeDtypeStruct((8, LANE), F32)),
        scratch_shapes=[pltpu.VMEM((8, LANE), F32)], compiler_params=_cp(None))(pba, par, dsc)


_B_NN = (((2,), (1,)), ((0,), (0,)))
_B_NT = (((2,), (2,)), ((0,), (0,)))
_B_TN = (((1,), (1,)), ((0,), (0,)))


def _bdot(a, b, dims, exact=False):
    return lax.dot_general(a, b, dims, preferred_element_type=F32,
                           precision=lax.Precision.HIGH if exact else None)


@jax.custom_vjp
def _inv_unit_tri(l):
    n = l.shape[-1]
    ri = lax.broadcasted_iota(jnp.int32, (n, n), 0)
    ci = lax.broadcasted_iota(jnp.int32, (n, n), 1)
    x = jnp.broadcast_to((ri == ci).astype(F32), l.shape)
    inner = 1
    for size in (4, 16, n):
        so, si = size.bit_length() - 1, inner.bit_length() - 1
        same_outer = (ri >> so) == (ci >> so)
        same_inner = (ri >> si) == (ci >> si)
        part = jnp.where(same_outer, jnp.where(same_inner, 0.0, l), 0.0)
        m = part if inner == 1 else _bdot(x, part, _B_NN, exact=True)
        y = x
        for _ in range(size // inner - 1):
            y = x - _bdot(m, y, _B_NN, exact=True)
        x = y
        inner = size
    return x


def _inv_fwd(l):
    a = _inv_unit_tri(l)
    return a, a


def _inv_bwd(a, da):
    t = _bdot(a, da, _B_TN, exact=True)
    return (-_bdot(t, a, _B_NT, exact=True),)


_inv_unit_tri.defvjp(_inv_fwd, _inv_bwd)


def _delta_chunk(q4, k4, v4, sc, s4, rev):
    hps = HEADS_PER_STEP
    c = DN_CHUNK
    hd = DN_HEAD_DIM
    ri = lax.broadcasted_iota(jnp.int32, (c, c), 0)
    ci = lax.broadcasted_iota(jnp.int32, (c, c), 1)
    ahead = (ri - ci) * jnp.where(rev, -1, 1)
    incl = ahead >= 0
    strict = ahead > 0
    sct = sc.T

    def heads(x):
        return jnp.stack([x[:, j * hd:(j + 1) * hd] for j in range(hps)], axis=0)

    def col(off):
        return jnp.stack([sc[:, off + j:off + j + 1] for j in range(hps)], axis=0)

    def row(off):
        return jnp.stack([sct[off + j:off + j + 1, :] for j in range(hps)], axis=0)

    q, k, v = heads(q4), heads(k4), heads(v4)
    beta_c, gc_c, gt_c = col(0), col(hps), col(2 * hps)
    beta_r, gc_r = row(0), row(hps)
    dmat = jnp.where(incl, jnp.exp(jnp.where(incl, gc_c - gc_r, 0.0)), 0.0)
    kk = _bdot(k, k, _B_NT, exact=True)
    qk = _bdot(q, k, _B_NT)
    a = _inv_unit_tri(jnp.where(strict, beta_c * kk * dmat, 0.0))
    w = _bdot(a * (beta_r * jnp.exp(gc_r)), k, _B_NN, exact=True)
    u = _bdot(a * beta_r, v, _B_NN, exact=True)
    u2 = u - _bdot(w, s4, _B_NN)
    o = jnp.exp(gc_c) * _bdot(q, s4, _B_NN) + _bdot(qk * dmat, u2, _B_NN)
    ke = k * jnp.exp(gt_c - gc_c)
    s_new = jnp.exp(gt_c[:, 0:1, :]) * s4 + _bdot(ke, u2, _B_TN)
    return jnp.concatenate([o[j] for j in range(hps)], axis=1), s_new


def _chunk_index(d, n, nctx, ntot):
    rev_idx = jnp.where(n < nctx, nctx - 1 - n, ntot - 1 - (n - nctx))
    return jnp.where(d == 0, n, rev_idx)


def _delta_fwd(qkv, scg):
    dd = _dims()
    r, dnw = dd["r"], dd["dnw"]
    hps = HEADS_PER_STEP
    nhg = DN_HEADS // hps
    bw = hps * DN_HEAD_DIM
    nctx, ntot = CTX_LEN // DN_CHUNK, r // DN_CHUNK

    def body(q_ref, k_ref, v_ref, sc_ref, o_ref, ss_ref, s_ref):
        d, n = pl.program_id(0), pl.program_id(2)

        @pl.when(n == 0)
        def _():
            s_ref[...] = jnp.zeros_like(s_ref)

        s4 = s_ref[...]
        ss_ref[0, 0] = s4
        o4, s4n = _delta_chunk(q_ref[...], k_ref[...], v_ref[...], sc_ref[0], s4, d == 1)
        o_ref[0] = o4
        s_ref[...] = s4n

    def rows(d, g, n):
        return _chunk_index(d, n, nctx, ntot)

    return pl.pallas_call(
        body, name="delta_fwd", grid=(2, nhg, ntot),
        out_shape=(jax.ShapeDtypeStruct((2, r, dnw), F32),
                   jax.ShapeDtypeStruct((2, ntot, DN_HEADS, DN_HEAD_DIM, DN_HEAD_DIM), F32)),
        in_specs=[pl.BlockSpec((DN_CHUNK, bw), lambda d, g, n: (rows(d, g, n), g)),
                  pl.BlockSpec((DN_CHUNK, bw), lambda d, g, n: (rows(d, g, n), nhg + g)),
                  pl.BlockSpec((DN_CHUNK, bw), lambda d, g, n: (rows(d, g, n), 2 * nhg + g)),
                  pl.BlockSpec((1, DN_CHUNK, LANE), lambda d, g, n: (d * nhg + g, rows(d, g, n), 0))],
        out_specs=(pl.BlockSpec((1, DN_CHUNK, bw), lambda d, g, n: (d, rows(d, g, n), g)),
                   pl.BlockSpec((1, 1, hps, DN_HEAD_DIM, DN_HEAD_DIM), lambda d, g, n: (d, n, g, 0, 0))),
        scratch_shapes=[pltpu.VMEM((hps, DN_HEAD_DIM, DN_HEAD_DIM), F32)],
        compiler_params=_cp(("arbitrary", "arbitrary", "arbitrary")))(qkv, qkv, qkv, scg)


def _delta_bwd(qkv, scg, states, do):
    dd = _dims()
    r, dnw = dd["r"], dd["dnw"]
    hps = HEADS_PER_STEP
    nhg = DN_HEADS // hps
    bw = hps * DN_HEAD_DIM
    nctx, ntot = CTX_LEN // DN_CHUNK, r // DN_CHUNK

    def body(q_ref, k_ref, v_ref, sc_ref, ss_ref, do_ref, dq_ref, dk_ref, dv_ref, dsc_ref, ds_ref):
        d, n2 = pl.program_id(0), pl.program_id(2)

        @pl.when(n2 == 0)
        def _():
            ds_ref[...] = jnp.zeros_like(ds_ref)

        fn = functools.partial(_delta_chunk, rev=(d == 1))
        _, vjp = jax.vjp(fn, q_ref[...], k_ref[...], v_ref[...], sc_ref[0], ss_ref[0, 0])
        dq, dk, dv, dsc, ds = vjp((do_ref[...], ds_ref[...]))
        dq_ref[0] = dq
        dk_ref[0] = dk
        dv_ref[0] = dv
        dsc_ref[0] = dsc
        ds_ref[...] = ds

    def rows(d, g, n2):
        return _chunk_index(d, ntot - 1 - n2, nctx, ntot)

    blk = lambda col: pl.BlockSpec((DN_CHUNK, bw), lambda d, g, n2: (rows(d, g, n2), col(g)))
    oblk = pl.BlockSpec((1, DN_CHUNK, bw), lambda d, g, n2: (d, rows(d, g, n2), g))
    scblk = pl.BlockSpec((1, DN_CHUNK, LANE), lambda d, g, n2: (d * nhg + g, rows(d, g, n2), 0))
    big = jax.ShapeDtypeStruct((2, r, dnw), F32)
    return pl.pallas_call(
        body, name="delta_bwd", grid=(2, nhg, ntot),
        out_shape=(big, big, big, jax.ShapeDtypeStruct(scg.shape, F32)),
        in_specs=[blk(lambda g: g), blk(lambda g: nhg + g), blk(lambda g: 2 * nhg + g), scblk,
                  pl.BlockSpec((1, 1, hps, DN_HEAD_DIM, DN_HEAD_DIM), lambda d, g, n2: (d, ntot - 1 - n2, g, 0, 0)),
                  blk(lambda g: g)],
        out_specs=(oblk, oblk, oblk, scblk),
        scratch_shapes=[pltpu.VMEM((hps, DN_HEAD_DIM, DN_HEAD_DIM), F32)],
        compiler_params=_cp(("arbitrary", "arbitrary", "arbitrary")))(qkv, qkv, qkv, scg, states, do)


def _lru_gate_vals(xn, wr, wi, br, bi, lam):
    r = jax.nn.sigmoid(jnp.dot(xn, wr, preferred_element_type=F32) + br)
    i = jax.nn.sigmoid(jnp.dot(xn, wi, preferred_element_type=F32) + bi)
    sp = _softplus(-lam)
    la = -LRU_C * r * sp
    a = jnp.exp(la)
    mm = jnp.sqrt(_neg_expm1(2.0 * la))
    return r, i, sp, la, a, mm


def _lru_gates_fwd(xc, w_r, w_i, vecs):
    r, lw = xc.shape
    bd = lw // LRU_BLOCKS
    tm = _div(r, ROW_TILE, 8)

    def body(x_ref, wr_ref, wi_ref, v_ref, a_ref, b_ref):
        for n in range(LRU_BLOCKS):
            sl = slice(n * bd, (n + 1) * bd)
            xn = x_ref[:, sl]
            _, i, _, _, a, mm = _lru_gate_vals(xn, wr_ref[0, n], wi_ref[0, n], v_ref[0, 0:1, sl], v_ref[0, 1:2, sl],
                                               v_ref[0, 2:3, sl])
            a_ref[0, :, sl] = a
            b_ref[0, :, sl] = mm * (i * xn)

    wspec = pl.BlockSpec((1, LRU_BLOCKS, bd, bd), lambda d, i: (d, 0, 0, 0))
    out = jax.ShapeDtypeStruct((2, r, lw), F32)
    ospec = pl.BlockSpec((1, tm, lw), lambda d, i: (d, i, 0))
    return pl.pallas_call(
        body, name="lru_gates_fwd", grid=(2, r // tm), out_shape=(out, out),
        in_specs=[pl.BlockSpec((tm, lw), lambda d, i: (i, 0)), wspec, wspec,
                  pl.BlockSpec((1, 8, lw), lambda d, i: (d, 0, 0))],
        out_specs=(ospec, ospec), compiler_params=_cp(("arbitrary", "arbitrary")))(xc, w_r, w_i, vecs)


def _lru_gates_bwd(xc, w_r, w_i, vecs, da, db):
    r, lw = xc.shape
    bd = lw // LRU_BLOCKS
    tm = _div(r, ROW_TILE, 8)

    def body(x_ref, wr_ref, wi_ref, v_ref, da_ref, db_ref, dx_ref, dwr_ref, dwi_ref, dv_ref):
        @pl.when(pl.program_id(1) == 0)
        def _():
            dwr_ref[...] = jnp.zeros_like(dwr_ref)
            dwi_ref[...] = jnp.zeros_like(dwi_ref)
            dv_ref[...] = jnp.zeros_like(dv_ref)

        for n in range(LRU_BLOCKS):
            sl = slice(n * bd, (n + 1) * bd)
            xn = x_ref[:, sl]
            wr, wi = wr_ref[0, n], wi_ref[0, n]
            lam = v_ref[0, 2:3, sl]
            rr, ii, sp, la, a, mm = _lru_gate_vals(xn, wr, wi, v_ref[0, 0:1, sl], v_ref[0, 1:2, sl], lam)
            dav, dbv = da_ref[0, :, sl], db_ref[0, :, sl]
            dmm = dbv * (ii * xn)
            dla = dav * a - dmm * (a * a) / jnp.maximum(mm, 1e-30)
            di = dbv * mm * xn
            dxn = dbv * mm * ii
            dr = dla * (-LRU_C * sp)
            dsp = jnp.sum(dla * (-LRU_C * rr), axis=0, keepdims=True)
            dpr = dr * rr * (1.0 - rr)
            dpi = di * ii * (1.0 - ii)
            dxn = dxn + lax.dot_general(dpr, wr, (((1,), (1,)), ((), ())), preferred_element_type=F32)
            dxn = dxn + lax.dot_general(dpi, wi, (((1,), (1,)), ((), ())), preferred_element_type=F32)
            dx_ref[0, :, sl] = dxn
            dwr_ref[0, n] += lax.dot_general(xn, dpr, (((0,), (0,)), ((), ())), preferred_element_type=F32)
            dwi_ref[0, n] += lax.dot_general(xn, dpi, (((0,), (0,)), ((), ())), preferred_element_type=F32)
            dv_ref[0, 0:1, sl] += jnp.sum(dpr, axis=0, keepdims=True)
            dv_ref[0, 1:2, sl] += jnp.sum(dpi, axis=0, keepdims=True)
            dv_ref[0, 2:3, sl] += dsp * (-jax.nn.sigmoid(-lam))

    wspec = pl.BlockSpec((1, LRU_BLOCKS, bd, bd), lambda d, i: (d, 0, 0, 0))
    vspec = pl.BlockSpec((1, 8, lw), lambda d, i: (d, 0, 0))
    big = pl.BlockSpec((1, tm, lw), lambda d, i: (d, i, 0))
    return pl.pallas_call(
        body, name="lru_gates_bwd", grid=(2, r // tm),
        out_shape=(jax.ShapeDtypeStruct((2, r, lw), F32), jax.ShapeDtypeStruct(w_r.shape, F32),
                   jax.ShapeDtypeStruct(w_i.shape, F32), jax.ShapeDtypeStruct((2, 8, lw), F32)),
        in_specs=[pl.BlockSpec((tm, lw), lambda d, i: (i, 0)), wspec, wspec, vspec, big, big],
        out_specs=(big, wspec, wspec, vspec), compiler_params=_cp(("arbitrary", "arbitrary")))(xc, w_r, w_i, vecs, da, db)


def _scan_tiles(rev):
    nctx, nlat = CTX_LEN // 8, SEQ // 8
    return nctx, nlat


def _tile_of(step, rev, nctx, ntot):
    if not rev:
        return step
    return jnp.where(step < nctx, nctx - 1 - step, ntot - 1 - (step - nctx))


def _lru_scan_fwd(a, b, d):
    _, r, lw = a.shape
    cw = _div(lw, 256, LANE)
    nctx, ntot = CTX_LEN // 8, r // 8
    rev = d == 1
    order = list(range(7, -1, -1)) if rev else list(range(8))

    def body(a_ref, b_ref, h_ref):
        def step(s, carry):
            t0 = pl.multiple_of(_tile_of(s, rev, nctx, ntot) * 8, 8)
            at = a_ref[0, pl.ds(t0, 8), :]
            bt = b_ref[0, pl.ds(t0, 8), :]
            rows = [None] * 8
            h = carry
            for i in order:
                h = at[i:i + 1, :] * h + bt[i:i + 1, :]
                rows[i] = h
            h_ref[pl.ds(t0, 8), :] = jnp.concatenate(rows, axis=0)
            return h

        lax.fori_loop(0, ntot, step, jnp.zeros((1, cw), F32))

    return pl.pallas_call(
        body, name=f"lru_scan_fwd{d}", grid=(lw // cw,), out_shape=jax.ShapeDtypeStruct((r, lw), F32),
        in_specs=[pl.BlockSpec((1, r, cw), lambda j: (d, 0, j)), pl.BlockSpec((1, r, cw), lambda j: (d, 0, j))],
        out_specs=pl.BlockSpec((r, cw), lambda j: (0, j)), compiler_params=_cp(("arbitrary",)))(a, b)


def _lru_scan_bwd(a, h, dh, d):
    _, r, lw = a.shape
    cw = _div(lw, 256, LANE)
    nctx, ntot = CTX_LEN // 8, r // 8
    rev = d == 1
    order = list(range(8)) if rev else list(range(7, -1, -1))

    def body(a_ref, h_ref, dh_ref, da_ref, db_ref):
        def step(s2, carry):
            s = ntot - 1 - s2
            t0 = pl.multiple_of(_tile_of(s, rev, nctx, ntot) * 8, 8)
            sp = jnp.maximum(s - 1, 0)
            tp = pl.multiple_of(_tile_of(sp, rev, nctx, ntot) * 8, 8)
            at = a_ref[0, pl.ds(t0, 8), :]
            ht = h_ref[pl.ds(t0, 8), :]
            hp = h_ref[pl.ds(tp, 8), :]
            dht = dh_ref[pl.ds(t0, 8), :]
            last = hp[0:1, :] if rev else hp[7:8, :]
            hprev_tile = jnp.where(s > 0, last, 0.0)
            das, dbs = [None] * 8, [None] * 8
            c = carry
            for i in order:
                lam = dht[i:i + 1, :] + c
                before = i + 1 if rev else i - 1
                hb = ht[before:before + 1, :] if 0 <= before < 8 else hprev_tile
                das[i] = lam * hb
                dbs[i] = lam
                c = at[i:i + 1, :] * lam
            da_ref[pl.ds(t0, 8), :] = jnp.concatenate(das, axis=0)
            db_ref[pl.ds(t0, 8), :] = jnp.concatenate(dbs, axis=0)
            return c

        lax.fori_loop(0, ntot, step, jnp.zeros((1, cw), F32))

    strip = pl.BlockSpec((r, cw), lambda j: (0, j))
    out = jax.ShapeDtypeStruct((r, lw), F32)
    return pl.pallas_call(
        body, name=f"lru_scan_bwd{d}", grid=(lw // cw,), out_shape=(out, out),
        in_specs=[pl.BlockSpec((1, r, cw), lambda j: (d, 0, j)), strip, strip],
        out_specs=(strip, strip), compiler_params=_cp(("arbitrary",)))(a, h, dh)


def _mix_post_fwd(o, hsum, proj, dnw_vec):
    dd = _dims()
    dnw, lw, nmain = dd["dnw"], dd["lw"], dd["nmain"]
    tm = _div(SEQ, ROW_TILE, 16)
    coff = CTX_LEN // tm
    gcol = (3 * dnw + lw) // dnw
    lcol = (3 * dnw + lw + dnw) // lw
    hd = DN_HEAD_DIM

    def body(o_ref, h_ref, g_ref, lg_ref, w_ref, y_ref):
        wv = w_ref[...]
        for hh in range(DN_HEADS):
            sl = slice(hh * hd, (hh + 1) * hd)
            os_ = o_ref[0, :, sl] + o_ref[1, :, sl]
            rr = lax.rsqrt(jnp.mean(os_ * os_, axis=1, keepdims=True) + EPS)
            y_ref[:, sl] = (os_ * rr * wv * _silu(g_ref[:, sl])).astype(BF16)
        y_ref[:, dnw:] = (h_ref[...] * _silu(lg_ref[...])).astype(BF16)

    return pl.pallas_call(
        body, name="mix_post_fwd", grid=(SEQ // tm,), out_shape=jax.ShapeDtypeStruct((SEQ, dnw + lw), BF16),
        in_specs=[pl.BlockSpec((2, tm, dnw), lambda i: (0, i + coff, 0)), pl.BlockSpec((tm, lw), lambda i: (i, 0)),
                  pl.BlockSpec((tm, dnw), lambda i: (i + coff, gcol)), pl.BlockSpec((tm, lw), lambda i: (i + coff, lcol)),
                  pl.BlockSpec((1, hd), lambda i: (0, 0))],
        out_specs=pl.BlockSpec((tm, dnw + lw), lambda i: (i, 0)), compiler_params=_cp(("arbitrary",)))(
            o, hsum, proj, proj, dnw_vec)


def _mix_post_bwd(o, hsum, proj, dnw_vec, dy):
    dd = _dims()
    dnw, lw, r = dd["dnw"], dd["lw"], dd["r"]
    tm = _div(SEQ, ROW_TILE, 16)
    coff = CTX_LEN // tm
    gcol = (3 * dnw + lw) // dnw
    lcol = (3 * dnw + lw + dnw) // lw
    hd = DN_HEAD_DIM

    def body(o_ref, h_ref, g_ref, lg_ref, w_ref, dy_ref, do_ref, dh_ref, dg_ref, dw_ref):
        i = pl.program_id(0)

        @pl.when(i == 0)
        def _():
            dw_ref[...] = jnp.zeros_like(dw_ref)

        @pl.when(i < coff)
        def _():
            do_ref[...] = jnp.zeros_like(do_ref)
            dg_ref[...] = jnp.zeros_like(dg_ref)

        @pl.when(i >= coff)
        def _():
            wv = w_ref[...]
            dwacc = jnp.zeros((1, hd), F32)
            for hh in range(DN_HEADS):
                sl = slice(hh * hd, (hh + 1) * hd)
                os_ = o_ref[0, :, sl] + o_ref[1, :, sl]
                rr = lax.rsqrt(jnp.mean(os_ * os_, axis=1, keepdims=True) + EPS)
                nn = os_ * rr
                gv = g_ref[:, sl]
                dyv = dy_ref[:, sl]
                don = dyv * _silu(gv)
                dg_ref[:, sl] = (dyv * nn * wv * _dsilu(gv)).astype(BF16)
                dwacc = dwacc + jnp.sum(don * nn, axis=0, keepdims=True)
                dn_ = don * wv
                do_ref[:, sl] = rr * (dn_ - nn * jnp.mean(nn * dn_, axis=1, keepdims=True))
            dw_ref[...] += dwacc
            lgv = lg_ref[...]
            dyl = dy_ref[:, dnw:]
            dh_ref[...] = dyl * _silu(lgv)
            dg_ref[:, dnw:] = (dyl * h_ref[...] * _dsilu(lgv)).astype(BF16)

    lat = lambda i: jnp.maximum(i - coff, 0)
    return pl.pallas_call(
        body, name="mix_post_bwd", grid=(r // tm,),
        out_shape=(jax.ShapeDtypeStruct((r, dnw), F32), jax.ShapeDtypeStruct((SEQ, lw), F32),
                   jax.ShapeDtypeStruct((r, dnw + lw), BF16), jax.ShapeDtypeStruct((1, hd), F32)),
        in_specs=[pl.BlockSpec((2, tm, dnw), lambda i: (0, i, 0)), pl.BlockSpec((tm, lw), lambda i: (lat(i), 0)),
                  pl.BlockSpec((tm, dnw), lambda i: (i, gcol)), pl.BlockSpec((tm, lw), lambda i: (i, lcol)),
                  pl.BlockSpec((1, hd), lambda i: (0, 0)), pl.BlockSpec((tm, dnw + lw), lambda i: (lat(i), 0))],
        out_specs=(pl.BlockSpec((tm, dnw), lambda i: (i, 0)), pl.BlockSpec((tm, lw), lambda i: (lat(i), 0)),
                   pl.BlockSpec((tm, dnw + lw), lambda i: (i, 0)), pl.BlockSpec((1, hd), lambda i: (0, 0))),
        compiler_params=_cp(("arbitrary",)))(o, hsum, proj, proj, dnw_vec, dy)


def _sc_masks(tm, cw):
    t = lax.broadcasted_iota(jnp.int32, (tm, cw), 0) % GRID_W
    return {-1: t != 0, 1: t != GRID_W - 1}


def _sc_mix_fwd(p, convw):
    s, w4 = p.shape
    w = w4 // 4
    tm = _div(s, 512, GRID_W if GRID_W % 16 == 0 else 16 * GRID_W)
    cw = _div(w, 512, LANE)
    nc = w // cw
    taps = _taps(SC_CONV)

    def body(b_ref, c_ref, x_ref, g_ref, w_ref, y_ref):
        masks = _sc_masks(tm, cw)
        u = c_ref[...] * x_ref[...]
        z = _conv_fwd_val(u, w_ref[...], taps, masks)
        y_ref[...] = (b_ref[...] * z * _silu(g_ref[...])).astype(BF16)

    blk = lambda k: pl.BlockSpec((tm, cw), lambda c, i: (i, k * nc + c))
    return pl.pallas_call(
        body, name="sc_mix_fwd", grid=(nc, s // tm), out_shape=jax.ShapeDtypeStruct((s, w), BF16),
        in_specs=[blk(0), blk(1), blk(2), blk(3), pl.BlockSpec((8, cw), lambda c, i: (0, c))],
        out_specs=pl.BlockSpec((tm, cw), lambda c, i: (i, c)),
        compiler_params=_cp(("arbitrary", "arbitrary")))(p, p, p, p, convw)


def _sc_mix_bwd(p, convw, dy):
    s, w4 = p.shape
    w = w4 // 4
    tm = _div(s, 512, GRID_W if GRID_W % 16 == 0 else 16 * GRID_W)
    cw = _div(w, 512, LANE)
    nc = w // cw
    taps = _taps(SC_CONV)

    def body(b_ref, c_ref, x_ref, g_ref, w_ref, dy_ref, db_ref, dc_ref, dx_ref, dg_ref, dw_ref):
        @pl.when(pl.program_id(1) == 0)
        def _():
            dw_ref[...] = jnp.zeros_like(dw_ref)

        masks = _sc_masks(tm, cw)
        wv = w_ref[...]
        bv, cv, xv, gv, dyv = b_ref[...], c_ref[...], x_ref[...], g_ref[...], dy_ref[...]
        u = cv * xv
        z = _conv_fwd_val(u, wv, taps, masks)
        sg = _silu(gv)
        db_ref[...] = (dyv * z * sg).astype(BF16)
        dg_ref[...] = (dyv * bv * z * _dsilu(gv)).astype(BF16)
        dz = dyv * bv * sg
        du, dws = _conv_bwd_val(u, dz, wv, taps, masks)
        dc_ref[...] = (du * xv).astype(BF16)
        dx_ref[...] = (du * cv).astype(BF16)
        dw_ref[...] += jnp.concatenate(dws + [jnp.zeros((8 - len(dws), cw), F32)], axis=0)

    blk = lambda k: pl.BlockSpec((tm, cw), lambda c, i: (i, k * nc + c))
    oblk = pl.BlockSpec((tm, cw), lambda c, i: (i, c))
    wblk = pl.BlockSpec((8, cw), lambda c, i: (0, c))
    out = jax.ShapeDtypeStruct((s, w), BF16)
    return pl.pallas_call(
        body, name="sc_mix_bwd", grid=(nc, s // tm),
        out_shape=(out, out, out, out, jax.ShapeDtypeStruct((8, w), F32)),
        in_specs=[blk(0), blk(1), blk(2), blk(3), wblk, oblk],
        out_specs=(oblk, oblk, oblk, oblk, wblk),
        compiler_params=_cp(("arbitrary", "arbitrary")))(p, p, p, p, convw, dy)


def _adamw_slots(w, slots, m, v, name):
    rows, cols = w.shape
    tm = _div(rows, max(16, (64 * 4096) // cols), 16)

    def body(w_ref, s_ref, m_ref, v_ref, g_ref, dl_ref, nm_ref, nv_ref):
        g = s_ref[0].astype(F32)
        for j in range(1, NDEV):
            g = g + s_ref[j].astype(F32)
        dl, m2, v2 = _adam(w_ref[...], g, m_ref[...], v_ref[...])
        g_ref[...] = g
        dl_ref[...] = dl
        nm_ref[...] = m2
        nv_ref[...] = v2

    blk = pl.BlockSpec((tm, cols), lambda i: (i, 0))
    out = jax.ShapeDtypeStruct((rows, cols), F32)
    return pl.pallas_call(
        body, name=name, grid=(rows // tm,), out_shape=(out, out, out, out),
        in_specs=[blk, pl.BlockSpec((NDEV, tm, cols), lambda i: (0, i, 0)), blk, blk],
        out_specs=(blk, blk, blk, blk), compiler_params=_cp(("arbitrary",)))(w, slots, m, v)


def _adamw_flat(w, g, m, v, name):
    rows, cols = w.shape
    tm = _div(rows, 1024, 8)

    def body(w_ref, g_ref, m_ref, v_ref, dl_ref, nm_ref, nv_ref):
        dl, m2, v2 = _adam(w_ref[...], g_ref[...], m_ref[...], v_ref[...])
        dl_ref[...] = dl
        nm_ref[...] = m2
        nv_ref[...] = v2

    blk = pl.BlockSpec((tm, cols), lambda i: (i, 0))
    out = jax.ShapeDtypeStruct((rows, cols), F32)
    return pl.pallas_call(
        body, name=name, grid=(rows // tm,), out_shape=(out, out, out), in_specs=[blk] * 4, out_specs=(blk, blk, blk),
        compiler_params=_cp(("arbitrary",)))(w, g, m, v)


def _pack(arrs):
    flat = [a.reshape(-1).astype(F32) for a in arrs]
    sizes = [f.shape[0] for f in flat]
    total = sum(sizes)
    padded = -(-total // 1024) * 1024
    if padded > total:
        flat.append(jnp.zeros((padded - total,), F32))
    offs, o = [], 0
    for s_ in sizes:
        offs.append(o)
        o += s_
    return jnp.concatenate(flat).reshape(padded // LANE, LANE), offs


def _unpack(flat2d, offs, shapes):
    flat = flat2d.reshape(-1)
    out = []
    for o, shp in zip(offs, shapes):
        n = 1
        for s_ in shp:
            n *= s_
        out.append(flat[o:o + n].reshape(shp))
    return out


def _pad_rows(a, rows):
    return jnp.concatenate([a, jnp.zeros((rows - a.shape[0],) + a.shape[1:], a.dtype)], axis=0)


def _to_col_major(t):
    n, ch = t.shape
    return t.reshape(n // GRID_W, GRID_W, ch).swapaxes(0, 1).reshape(n, ch)


def _to_raster(t):
    n, ch = t.shape
    return t.reshape(GRID_W, n // GRID_W, ch).swapaxes(0, 1).reshape(n, ch)


def _scg_layout(sc):
    r = sc.shape[0]
    hps = HEADS_PER_STEP
    nhg = DN_HEADS // hps
    t = sc.reshape(r, 4, 32)[:, :, :2 * DN_HEADS].reshape(r, 4, 2, nhg, hps)
    t = jnp.stack([t[:, 0], t[:, 2], t[:, 3]], axis=1)
    t = t.transpose(2, 3, 0, 1, 4).reshape(2 * nhg, r, 3 * hps)
    return jnp.concatenate([t, jnp.zeros((2 * nhg, r, LANE - 3 * hps), F32)], axis=2)


def _scg_layout_bwd(dscg):
    r = dscg.shape[1]
    hps = HEADS_PER_STEP
    nhg = DN_HEADS // hps
    t = dscg[:, :, :3 * hps].reshape(2, nhg, r, 3, hps).transpose(2, 3, 0, 1, 4).reshape(r, 3, 2 * DN_HEADS)
    t = jnp.concatenate([t, jnp.zeros((r, 3, 32 - 2 * DN_HEADS), F32)], axis=2) if 2 * DN_HEADS < 32 else t
    z = jnp.zeros((r, 1, 32), F32)
    return jnp.concatenate([t[:, 0:1], z, t[:, 1:2], t[:, 2:3]], axis=1).reshape(r, LANE)


def _gather_cols(full8, width):
    return full8.transpose(1, 0, 2).reshape(full8.shape[1], NDEV * width)


def kernel(x, c, ctx, c_ctx, mod_w, mod_b, norm_w, ab_w_in, ab_qkv_conv, ab_a_log, ab_dt_bias, ab_dn_norm, ab_lru_conv_w, ab_lru_conv_b, ab_lru_w_r, ab_lru_b_r, ab_lru_w_i, ab_lru_b_i, ab_lru_lambda, ab_w_out, sc_w_in, sc_conv, sc_w_out, final_norm_w, loss_target, m_c_ctx, m_mod_w, m_mod_b, m_norm_w, m_ab_w_in, m_ab_qkv_conv, m_ab_a_log, m_ab_dt_bias, m_ab_dn_norm, m_ab_lru_conv_w, m_ab_lru_conv_b, m_ab_lru_w_r, m_ab_lru_b_r, m_ab_lru_w_i, m_ab_lru_b_i, m_ab_lru_lambda, m_ab_w_out, m_sc_w_in, m_sc_conv, m_sc_w_out, m_final_norm_w, v_c_ctx, v_mod_w, v_mod_b, v_norm_w, v_ab_w_in, v_ab_qkv_conv, v_ab_a_log, v_ab_dt_bias, v_ab_dn_norm, v_ab_lru_conv_w, v_ab_lru_conv_b, v_ab_lru_w_r, v_ab_lru_b_r, v_ab_lru_w_i, v_ab_lru_b_i, v_ab_lru_lambda, v_ab_w_out, v_sc_w_in, v_sc_conv, v_sc_w_out, v_final_norm_w):
    dd = _dims()
    d, dnw, lw, r = dd["d"], dd["dnw"], dd["lw"], dd["r"]
    off_lru, off_beta, ab_state, ab_in, nmain = dd["off_lru"], dd["off_beta"], dd["ab_state"], dd["ab_in"], dd["nmain"]
    hh2 = 2 * DN_HEADS
    me = _me()
    x = x[0]
    ctx = ctx[0]
    tgt = loss_target[0]

    small_sharded = [ab_qkv_conv[0], ab_lru_conv_w[0], ab_lru_b_r[0], ab_lru_b_i[0], ab_lru_lambda[0], sc_conv[0]]
    pk, offs0 = _pack([c] + small_sharded)
    (g0,) = _exchange([pk], "gather_small_in", scatter=False)
    parts = [_unpack(g0[j], offs0, [c.shape] + [a.shape for a in small_sharded]) for j in range(NDEV)]
    c_all = jnp.concatenate([p[0] for p in parts], axis=0)
    qkv_conv, lru_conv_w, lru_b_r, lru_b_i, lru_lam, sc_conv_f = [
        jnp.concatenate([p[i] for p in parts], axis=-1) for i in range(1, 7)]

    mcols = mod_w.shape[2]
    c16 = jnp.concatenate([c_all, c_ctx[None, :], jnp.zeros((7, d), F32)], axis=0)
    bias = lax.dynamic_slice_in_dim(mod_b, me * mcols, mcols, axis=1)[:, None, :]
    mod_loc = _mod_fwd(c16, mod_w, bias)
    (mod_g,) = _exchange([mod_loc], "gather_mod", scatter=False)
    mod_all = mod_g.transpose(1, 2, 0, 3).reshape(2, 16, NDEV * mcols)
    mine = lax.dynamic_slice_in_dim(mod_all, me, 1, axis=1)[:, 0]
    shift = [mine[l, None, 0:d] for l in range(2)]
    scale = [mine[l, None, d:2 * d] for l in range(2)]
    gate = [mine[l, None, 2 * d:3 * d] for l in range(2)]
    shift_c, scale_c = mod_all[0, 8, None, 0:d], mod_all[0, 8, None, d:2 * d]
    nw0, nw1 = norm_w[0][None, :], norm_w[1][None, :]

    wsrc = [ab_w_in[0].astype(BF16), ab_w_out[0].astype(BF16), sc_w_in[0].astype(BF16), sc_w_out[0].astype(BF16)]
    g_abin, g_about, g_scin, g_scout = _exchange(wsrc, "gather_weights", scatter=False)
    w_full = _gather_cols(g_abin, ab_in // NDEV)
    w_main = jnp.concatenate([w_full[:, :off_beta], w_full[:, ab_state:]], axis=1)
    w_ba = jnp.concatenate([w_full[:, off_beta:ab_state], jnp.zeros((d, LANE - 2 * hh2), BF16)], axis=1)
    w_about = g_about.reshape(dnw + lw, d)
    w_scout = g_scout.reshape(d, d)

    hn_c, rinv_c = _norm_fwd(ctx, nw0, shift_c, scale_c, "norm0_ctx")
    hn_x, rinv_x = _norm_fwd(x, nw0, shift[0], scale[0], "norm0_x")
    hn0 = jnp.concatenate([hn_c, hn_x], axis=0)
    proj = _mm_nn(hn0, w_main[None], F32, "proj_main")
    pba = _mm_nn(hn0, w_ba[None], F32, "proj_ba")

    convw8 = _pad_rows(qkv_conv, 8)
    qkv = _dn_pre_fwd(proj, convw8)
    zpad = jnp.zeros((32 - hh2,), F32)
    par = jnp.stack([jnp.concatenate([jnp.zeros((32,), F32), ab_dt_bias[0].reshape(-1), zpad, jnp.zeros((64,), F32)]),
                     jnp.concatenate([jnp.zeros((32,), F32), -jnp.exp(ab_a_log[0].reshape(-1)), zpad,
                                      jnp.zeros((64,), F32)])] + [jnp.zeros((LANE,), F32)] * 6)
    if hh2 < 32:
        pba_l = jnp.concatenate([pba[:, :hh2], jnp.zeros((r, 32 - hh2), F32), pba[:, hh2:2 * hh2],
                                 jnp.zeros((r, LANE - 32 - hh2), F32)], axis=1)
    else:
        pba_l = pba
    sc = _dn_gates_fwd(pba_l, par)
    scg = _scg_layout(sc)
    o_dn, states = _delta_fwd(qkv, scg)

    xl = proj[:, off_lru:off_beta]
    xl_cat = jnp.concatenate([xl[:CTX_LEN], _to_col_major(xl[CTX_LEN:])], axis=0)
    lconv8 = jnp.concatenate([lru_conv_w, ab_lru_conv_b[0][None, :], jnp.zeros((3, lw), F32)], axis=0)
    xc = _lru_conv_fwd(xl_cat, lconv8)
    lvecs = jnp.stack([lru_b_r, lru_b_i, lru_lam] + [jnp.zeros_like(lru_lam)] * 5, axis=1)
    w_r, w_i = ab_lru_w_r[0], ab_lru_w_i[0]
    la_a, la_b = _lru_gates_fwd(xc, w_r, w_i, lvecs)
    h_dir = [_lru_scan_fwd(la_a, la_b, 0), _lru_scan_fwd(la_a, la_b, 1)]
    hsum = _to_raster(h_dir[0][CTX_LEN:] + h_dir[1][CTX_LEN:])

    dnn = ab_dn_norm[0][None, :]
    y0 = _mix_post_fwd(o_dn, hsum, proj, dnn)
    out0 = _mm_nn(y0, w_about[None], F32, "out0")

    x1, hn1, rinv1 = _resid_norm_fwd(x, out0, gate[0], nw1, shift[1], scale[1])
    p1 = _mm_nn(hn1, g_scin, F32, "proj_sc")
    scw8 = _pad_rows(sc_conv_f, 8)
    y1 = _sc_mix_fwd(p1, scw8)
    out1 = _mm_nn(y1, w_scout[None], F32, "out1")

    fnw = final_norm_w[None, :]
    dx2, dout1, loss11, dfnw, dgate1 = _loss_head(x1, out1, gate[1], fnw, tgt)
    dy1 = _mm_nt(dout1, w_scout[None], F32, "dy1")
    gw_scout = _mm_tn(y1, dout1, BF16, "gw_scout")[0].reshape(NDEV, d // NDEV, d)
    dpb, dpc, dpx, dpg, dscw8 = _sc_mix_bwd(p1, scw8, dy1)
    dp1 = jnp.concatenate([dpb, dpc, dpx, dpg], axis=1)
    dhn1 = _mm_nt(dp1, g_scin, F32, "dhn1", tk=2048)
    gw_scin = _mm_tn(hn1, dp1, BF16, "gw_scin", oj=NDEV)
    dx1, dout0, dshift1, dscale1, dnw1, dgate0 = _norm_bwd(
        x1, rinv1, [dhn1], nw1, scale[1], "norm1_bwd", dx_in=dx2, resid=(out0, gate[0]))

    dy0 = _mm_nt(dout0, w_about[None], F32, "dy0")
    gw_about = _mm_tn(y0, dout0, BF16, "gw_about")[0].reshape(NDEV, (dnw + lw) // NDEV, d)
    do, dh_r, dgates, ddnn = _mix_post_bwd(o_dn, hsum, proj, dnn, dy0)

    dq, dk, dv, dscg = _delta_bwd(qkv, scg, states, do)
    dsc = _scg_layout_bwd(dscg)
    dpba, dpar = _dn_gates_bwd(pba_l, par, dsc)
    dpq, dcw_q = _dn_pre_bwd(proj, convw8, dq, 0)
    dpk, dcw_k = _dn_pre_bwd(proj, convw8, dk, 1)
    dpv, dcw_v = _dn_pre_bwd(proj, convw8, dv, 2)

    dh_cat = jnp.concatenate([jnp.zeros((CTX_LEN, lw), F32), _to_col_major(dh_r)], axis=0)
    da0, db0 = _lru_scan_bwd(la_a, h_dir[0], dh_cat, 0)
    da1, db1 = _lru_scan_bwd(la_a, h_dir[1], dh_cat, 1)
    dxc, dw_r, dw_i, dlvecs = _lru_gates_bwd(xc, w_r, w_i, lvecs, jnp.stack([da0, da1]), jnp.stack([db0, db1]))
    dxl_cat, dlconv8 = _lru_conv_bwd(xl_cat, lconv8, dxc)
    dxl = jnp.concatenate([dxl_cat[:CTX_LEN], _to_raster(dxl_cat[CTX_LEN:])], axis=0).astype(BF16)

    dproj = jnp.concatenate([dpq, dpk, dpv, dxl, dgates], axis=1)
    if hh2 < 32:
        dpba_w = jnp.concatenate([dpba[:, :hh2], dpba[:, 32:32 + hh2], jnp.zeros((r, LANE - 2 * hh2), BF16)], axis=1)
    else:
        dpba_w = dpba
    dhn0_a = _mm_nt(dproj, w_main[None], F32, "dhn0_main")
    dhn0_b = _mm_nt(dpba_w, w_ba[None], F32, "dhn0_ba")
    gw_main = _mm_tn(hn0, dproj, BF16, "gw_main")[0]
    gw_ba = _mm_tn(hn0, dpba_w, BF16, "gw_ba")[0]
    gw_full = jnp.concatenate([gw_main[:, :off_beta], gw_ba[:, :2 * hh2], gw_main[:, off_beta:]], axis=1)
    gw_abin = gw_full.reshape(d, NDEV, ab_in // NDEV).transpose(1, 0, 2)

    nt = _div(SEQ, ROW_TILE // 2, 16)
    grad_x, dshift0, dscale0, dnw0_x = _norm_bwd(x, rinv_x, [dhn0_a, dhn0_b], nw0, scale[0], "norm0_bwd_x",
                                                 row_off=CTX_LEN // nt, dx_in=dx1)
    _, dshift_c, dscale_c, dnw0_c = _norm_bwd(ctx, rinv_c, [dhn0_a, dhn0_b], nw0, scale_c, "norm0_bwd_ctx")

    s_abin, s_about, s_scin, s_scout = _exchange([gw_abin, gw_about, gw_scin, gw_scout], "scatter_grads", scatter=True)
    big = {}
    big["ab_w_in"] = _adamw_slots(ab_w_in[0], s_abin, m_ab_w_in[0], v_ab_w_in[0], "adamw_ab_w_in")
    big["ab_w_out"] = _adamw_slots(ab_w_out[0], s_about, m_ab_w_out[0], v_ab_w_out[0], "adamw_ab_w_out")
    big["sc_w_in"] = _adamw_slots(sc_w_in[0], s_scin, m_sc_w_in[0], v_sc_w_in[0], "adamw_sc_w_in")
    big["sc_w_out"] = _adamw_slots(sc_w_out[0], s_scout, m_sc_w_out[0], v_sc_w_out[0], "adamw_sc_w_out")

    dmod_own = jnp.stack([jnp.concatenate([dshift0, dscale0, dgate0], axis=1)[0],
                          jnp.concatenate([dshift1, dscale1, dgate1], axis=1)[0]])
    dmod_c = jnp.concatenate([dshift_c, dscale_c], axis=1)[0]
    g_qkv_conv = jnp.concatenate([dcw_q[:DN_CONV], dcw_k[:DN_CONV], dcw_v[:DN_CONV]], axis=1)
    small = [
        ("loss", loss11.reshape(1)),
        ("norm_w", jnp.concatenate([dnw0_x + dnw0_c, dnw1], axis=0)),
        ("dmod_own", dmod_own),
        ("dmod_c", dmod_c),
        ("qkv_conv", g_qkv_conv),
        ("a_log", dpar[1, 32:32 + hh2].reshape(2, DN_HEADS)),
        ("dt_bias", dpar[0, 32:32 + hh2].reshape(2, DN_HEADS)),
        ("dn_norm", ddnn[0]),
        ("lru_conv_w", dlconv8[:LRU_CONV]),
        ("lru_conv_b", dlconv8[LRU_CONV]),
        ("lru_w_r", dw_r),
        ("lru_w_i", dw_i),
        ("lru_b_r", dlvecs[:, 0]),
        ("lru_b_i", dlvecs[:, 1]),
        ("lru_lambda", dlvecs[:, 2]),
        ("sc_conv", dscw8[:SC_CONV]),
        ("final_norm_w", dfnw[0]),
    ]
    names = [n for n, _ in small]
    shapes = [a.shape for _, a in small]
    spk, soffs = _pack([a for _, a in small])
    (sg,) = _exchange([spk], "gather_small_grads", scatter=False)
    tot = dict(zip(names, _unpack(_sum_slots(sg, "sum_small_grads"), soffs, shapes)))
    i_own = names.index("dmod_own")
    dmod_all = jnp.stack([_unpack(sg[j], soffs[i_own:i_own + 1], shapes[i_own:i_own + 1])[0] for j in range(NDEV)])

    dmc_full = jnp.concatenate([tot["dmod_c"], jnp.zeros((d,), F32)])
    rows9 = jnp.concatenate([dmod_all, jnp.stack([dmc_full, jnp.zeros((3 * d,), F32)])[None],
                             jnp.zeros((7, 2, 3 * d), F32)], axis=0)
    dm = lax.dynamic_slice_in_dim(rows9, me * mcols, mcols, axis=2).transpose(1, 0, 2)
    g_modw, dl_modw, nm_modw, nv_modw, gcc = _mod_bwd(c16, dm, mod_w, m_mod_w, v_mod_w)
    gcc_pk, _ = _pack([gcc[0, :, 0]])
    (gcc_g,) = _exchange([gcc_pk], "gather_c_ctx", scatter=False)
    gcc_sum = _sum_slots(gcc_g, "sum_c_ctx").reshape(-1)[:d]
    g_c_ctx_pre = gcc_sum

    grad_mod_b = tot["dmod_own"] + jnp.stack([dmc_full, jnp.zeros((3 * d,), F32)])

    def shard(a):
        wd = a.shape[-1] // NDEV
        return lax.dynamic_slice_in_dim(a, me * wd, wd, axis=a.ndim - 1)

    small_w = [
        ("c_ctx", c_ctx, None, m_c_ctx, v_c_ctx),
        ("mod_b", mod_b, grad_mod_b, m_mod_b, v_mod_b),
        ("norm_w", norm_w, tot["norm_w"], m_norm_w, v_norm_w),
        ("ab_qkv_conv", ab_qkv_conv, shard(tot["qkv_conv"])[None], m_ab_qkv_conv, v_ab_qkv_conv),
        ("ab_a_log", ab_a_log, tot["a_log"][None], m_ab_a_log, v_ab_a_log),
        ("ab_dt_bias", ab_dt_bias, tot["dt_bias"][None], m_ab_dt_bias, v_ab_dt_bias),
        ("ab_dn_norm", ab_dn_norm, tot["dn_norm"][None], m_ab_dn_norm, v_ab_dn_norm),
        ("ab_lru_conv_w", ab_lru_conv_w, shard(tot["lru_conv_w"])[None], m_ab_lru_conv_w, v_ab_lru_conv_w),
        ("ab_lru_conv_b", ab_lru_conv_b, tot["lru_conv_b"][None], m_ab_lru_conv_b, v_ab_lru_conv_b),
        ("ab_lru_w_r", ab_lru_w_r, tot["lru_w_r"][None], m_ab_lru_w_r, v_ab_lru_w_r),
        ("ab_lru_b_r", ab_lru_b_r, shard(tot["lru_b_r"])[None], m_ab_lru_b_r, v_ab_lru_b_r),
        ("ab_lru_w_i", ab_lru_w_i, tot["lru_w_i"][None], m_ab_lru_w_i, v_ab_lru_w_i),
        ("ab_lru_b_i", ab_lru_b_i, shard(tot["lru_b_i"])[None], m_ab_lru_b_i, v_ab_lru_b_i),
        ("ab_lru_lambda", ab_lru_lambda, shard(tot["lru_lambda"])[None], m_ab_lru_lambda, v_ab_lru_lambda),
        ("sc_conv", sc_conv, shard(tot["sc_conv"])[None], m_sc_conv, v_sc_conv),
        ("final_norm_w", final_norm_w, tot["final_norm_w"], m_final_norm_w, v_final_norm_w),
    ]
    sg_cc = jax.nn.sigmoid(c_ctx)
    g_c_ctx = g_c_ctx_pre * (sg_cc * (1.0 + c_ctx * (1.0 - sg_cc)))
    small_w[0] = ("c_ctx", c_ctx, g_c_ctx, m_c_ctx, v_c_ctx)
    wshapes = [w.shape for _, w, _, _, _ in small_w]
    wpk, woffs = _pack([w for _, w, _, _, _ in small_w])
    gpk, _ = _pack([g for _, _, g, _, _ in small_w])
    mpk, _ = _pack([m for _, _, _, m, _ in small_w])
    vpk, _ = _pack([v for _, _, _, _, v in small_w])
    dl_pk, nm_pk, nv_pk = _adamw_flat(wpk, gpk, mpk, vpk, "adamw_small")
    sm_g = {n: g for n, _, g, _, _ in small_w}
    sm_dl = dict(zip([n for n, *_ in small_w], _unpack(dl_pk, woffs, wshapes)))
    sm_nm = dict(zip([n for n, *_ in small_w], _unpack(nm_pk, woffs, wshapes)))
    sm_nv = dict(zip([n for n, *_ in small_w], _unpack(nv_pk, woffs, wshapes)))

    grads, deltas, new_m, new_v = {}, {}, {}, {}
    for n in sm_g:
        grads[n], deltas[n], new_m[n], new_v[n] = sm_g[n], sm_dl[n], sm_nm[n], sm_nv[n]
    grads["mod_w"], deltas["mod_w"], new_m["mod_w"], new_v["mod_w"] = g_modw, dl_modw, nm_modw, nv_modw
    for n in ("ab_w_in", "ab_w_out", "sc_w_in", "sc_w_out"):
        g, dl, m2, v2 = big[n]
        grads[n], deltas[n], new_m[n], new_v[n] = g[None], dl[None], m2[None], v2[None]

    order = ["c_ctx", "mod_w", "mod_b", "norm_w", "ab_w_in", "ab_qkv_conv", "ab_a_log", "ab_dt_bias", "ab_dn_norm",
             "ab_lru_conv_w", "ab_lru_conv_b", "ab_lru_w_r", "ab_lru_b_r", "ab_lru_w_i", "ab_lru_b_i", "ab_lru_lambda",
             "ab_w_out", "sc_w_in", "sc_conv", "sc_w_out", "final_norm_w"]
    loss = tot["loss"][0]
    return (loss, grad_x[None], *[grads[n] for n in order], *[deltas[n] for n in order],
            *[new_m[n] for n in order], *[new_v[n] for n in order])
```

```python
import functools

import jax
import jax.numpy as jnp
from jax import lax
from jax.experimental import pallas as pl
from jax.experimental.pallas import tpu as pltpu

F32 = jnp.float32
BF16 = jnp.bfloat16

NDEV = 8
D_MODEL = 4096
SEQ = 4096
CTX_LEN = 256
GRID_W = 64
EPS = 1e-6
DN_HEADS = 16
DN_HEAD_DIM = 128
DN_CONV = 4
DN_CHUNK = 64
LRU_WIDTH = 2048
LRU_BLOCKS = 16
LRU_CONV = 4
LRU_C = 8.0
SC_CONV = 3
ADAM_LR = 0.001
ADAM_B1 = 0.9
ADAM_B2 = 0.999
ADAM_EPS = 1e-08
ADAM_WD = 0.01
ADAM_STEP = 10

ROW_TILE = 256
HEADS_PER_STEP = 16
LANE = 128
VMEM_LIMIT = 56 * 1024 * 1024


def _dims():
    d = D_MODEL
    dnw = DN_HEADS * DN_HEAD_DIM
    lw = LRU_WIDTH
    off_lru = 3 * dnw
    off_beta = off_lru + lw
    ab_state = off_beta + 4 * DN_HEADS
    ab_in = ab_state + dnw + lw
    return dict(d=d, dnw=dnw, lw=lw, off_lru=off_lru, off_beta=off_beta, ab_state=ab_state, ab_in=ab_in,
                nmain=3 * dnw + lw + dnw + lw, r=CTX_LEN + SEQ, lbd=lw // LRU_BLOCKS)


def _div(n, pref, mult):
    best = None
    for t in range(mult, min(n, pref) + 1, mult):
        if n % t == 0:
            best = t
    return best if best is not None else n


def _cp(sem, vmem=VMEM_LIMIT):
    return pltpu.CompilerParams(dimension_semantics=sem, vmem_limit_bytes=vmem)


def _silu(z):
    return z * jax.nn.sigmoid(z)


def _dsilu(z):
    s = jax.nn.sigmoid(z)
    return s * (1.0 + z * (1.0 - s))


def _softplus(z):
    return jnp.maximum(z, 0.0) + jnp.log1p(jnp.exp(-jnp.abs(z)))


def _neg_expm1(z):
    series = -z * (1.0 + z * (0.5 + z * (1.0 / 6.0 + z * (1.0 / 24.0))))
    return jnp.where(z > -0.05, series, 1.0 - jnp.exp(z))


def _me():
    return 4 * lax.axis_index("x") + 2 * lax.axis_index("y") + lax.axis_index("c")


def _exchange(srcs, name, scatter):
    n = len(srcs)

    def body(*refs):
        src_refs, out_refs = refs[:n], refs[n:2 * n]
        send_sems, recv_sems, local_sems = refs[2 * n:]
        x, y, c = lax.axis_index("x"), lax.axis_index("y"), lax.axis_index("c")
        me = 4 * x + 2 * y + c
        copies = []
        for i in range(n):
            mine = src_refs[i].at[me] if scatter else src_refs[i]
            lc = pltpu.make_async_copy(mine, out_refs[i].at[me], local_sems.at[i])
            lc.start()
            copies.append(lc)
            for k in range(1, NDEV):
                px = 1 - x if (k >> 2) & 1 else x
                py = 1 - y if (k >> 1) & 1 else y
                pc = 1 - c if k & 1 else c
                peer = 4 * px + 2 * py + pc
                src = src_refs[i].at[peer] if scatter else src_refs[i]
                cp = pltpu.make_async_remote_copy(
                    src_ref=src, dst_ref=out_refs[i].at[me], send_sem=send_sems.at[i, k - 1],
                    recv_sem=recv_sems.at[i, k - 1], device_id=(px, py, pc), device_id_type=pl.DeviceIdType.MESH)
                cp.start()
                copies.append(cp)
        for cp in copies:
            cp.wait()

    any_spec = pl.BlockSpec(memory_space=pl.ANY)
    out_shape = tuple(jax.ShapeDtypeStruct(s.shape if scatter else (NDEV,) + s.shape, s.dtype) for s in srcs)
    outs = pl.pallas_call(
        body, name=name, out_shape=out_shape, in_specs=[any_spec] * n, out_specs=tuple([any_spec] * n),
        scratch_shapes=[pltpu.SemaphoreType.DMA((n, NDEV - 1)), pltpu.SemaphoreType.DMA((n, NDEV - 1)),
                        pltpu.SemaphoreType.DMA((n,))],
    )(*srcs)
    return list(outs)


def _flip(v, bit):
    return 1 - v if bit else v


def _gather_two_level(srcs, name):
    n = len(srcs)

    def body(*refs):
        src_refs, out_refs = refs[:n], refs[n:2 * n]
        send_sems, recv_sems, local_sems = refs[2 * n:]
        x, y, c = lax.axis_index("x"), lax.axis_index("y"), lax.axis_index("c")
        me = 4 * x + 2 * y + c
        sibling = (x, y, 1 - c)
        chips = [(_flip(x, k >> 1), _flip(y, k & 1)) for k in (1, 2, 3)]

        def copy(i, sem, slot, to, src=None):
            return pltpu.make_async_remote_copy(
                src_ref=out_refs[i].at[slot] if src is None else src, dst_ref=out_refs[i].at[slot],
                send_sem=send_sems.at[i, sem], recv_sem=recv_sems.at[i, sem], device_id=to,
                device_id_type=pl.DeviceIdType.MESH)

        own_copies, sent = [], []
        for i in range(n):
            lc = pltpu.make_async_copy(src_refs[i], out_refs[i].at[me], local_sems.at[i])
            lc.start()
            own_copies.append(lc)
            first = [copy(i, 0, me, sibling, src=src_refs[i])]
            first += [copy(i, 1 + j, me, (px, py, c), src=src_refs[i]) for j, (px, py) in enumerate(chips)]
            for cp in first:
                cp.start()
            sent += first
        for i in range(n):
            for j, (px, py) in enumerate(chips):
                theirs = 4 * px + 2 * py + c
                copy(i, 1 + j, theirs, (x, y, c)).wait_recv()
                fwd = copy(i, 4 + j, theirs, sibling)
                fwd.start()
                sent.append(fwd)
        for i in range(n):
            copy(i, 0, 4 * x + 2 * y + (1 - c), (x, y, c)).wait_recv()
            for j, (px, py) in enumerate(chips):
                copy(i, 4 + j, 4 * px + 2 * py + (1 - c), (x, y, c)).wait_recv()
        for cp in sent:
            cp.wait_send()
        for lc in own_copies:
            lc.wait()

    any_spec = pl.BlockSpec(memory_space=pl.ANY)
    out_shape = tuple(jax.ShapeDtypeStruct((NDEV,) + s.shape, s.dtype) for s in srcs)
    outs = pl.pallas_call(
        body, name=name, out_shape=out_shape, in_specs=[any_spec] * n, out_specs=tuple([any_spec] * n),
        scratch_shapes=[pltpu.SemaphoreType.DMA((n, NDEV - 1)), pltpu.SemaphoreType.DMA((n, NDEV - 1)),
                        pltpu.SemaphoreType.DMA((n,))],
    )(*srcs)
    return list(outs)


def _sibling_swap(srcs, name):
    n = len(srcs)
    nchip = NDEV // 2

    def body(*refs):
        src_refs, out_refs = refs[:n], refs[n:2 * n]
        send_sems, recv_sems = refs[2 * n:]
        x, y, c = lax.axis_index("x"), lax.axis_index("y"), lax.axis_index("c")
        copies = []
        for i in range(n):
            for q in range(nchip):
                cp = pltpu.make_async_remote_copy(
                    src_ref=src_refs[i].at[2 * q + (1 - c)], dst_ref=out_refs[i].at[q], send_sem=send_sems.at[i, q],
                    recv_sem=recv_sems.at[i, q], device_id=(x, y, 1 - c), device_id_type=pl.DeviceIdType.MESH)
                cp.start()
                copies.append(cp)
        for cp in copies:
            cp.wait()

    any_spec = pl.BlockSpec(memory_space=pl.ANY)
    out_shape = tuple(jax.ShapeDtypeStruct((nchip,) + s.shape[1:], s.dtype) for s in srcs)
    outs = pl.pallas_call(
        body, name=name, out_shape=out_shape, in_specs=[any_spec] * n, out_specs=tuple([any_spec] * n),
        scratch_shapes=[pltpu.SemaphoreType.DMA((n, nchip)), pltpu.SemaphoreType.DMA((n, nchip))],
    )(*srcs)
    return list(outs)


def _chip_partial(own, sib, name):
    _, rows, cols = own.shape
    tm = _div(rows, max(16, (128 * 4096) // cols), 16)

    def body(c_ref, o_ref, s_ref, p_ref):
        p_ref[0] = (o_ref[0].astype(F32) + s_ref[0].astype(F32)).astype(BF16)

    core = lax.axis_index("c").astype(jnp.int32).reshape(1)
    grid_spec = pltpu.PrefetchScalarGridSpec(
        num_scalar_prefetch=1, grid=(NDEV // 2, rows // tm),
        in_specs=[pl.BlockSpec((1, tm, cols), lambda q, i, c_ref: (2 * q + c_ref[0], i, 0)),
                  pl.BlockSpec((1, tm, cols), lambda q, i, c_ref: (q, i, 0))],
        out_specs=pl.BlockSpec((1, tm, cols), lambda q, i, c_ref: (q, i, 0)))
    return pl.pallas_call(
        body, name=name, grid_spec=grid_spec, out_shape=jax.ShapeDtypeStruct(sib.shape, BF16),
        compiler_params=_cp(("arbitrary", "arbitrary")))(core, own, sib)


def _chip_exchange(parts, name):
    n = len(parts)
    nchip = NDEV // 2

    def body(*refs):
        src_refs, out_refs = refs[:n], refs[n:2 * n]
        send_sems, recv_sems, local_sems = refs[2 * n:]
        x, y, c = lax.axis_index("x"), lax.axis_index("y"), lax.axis_index("c")
        mine = 2 * x + y
        copies = []
        for i in range(n):
            lc = pltpu.make_async_copy(src_refs[i].at[mine], out_refs[i].at[mine], local_sems.at[i])
            lc.start()
            copies.append(lc)
            for k in (1, 2, 3):
                px, py = _flip(x, k >> 1), _flip(y, k & 1)
                cp = pltpu.make_async_remote_copy(
                    src_ref=src_refs[i].at[2 * px + py], dst_ref=out_refs[i].at[mine], send_sem=send_sems.at[i, k - 1],
                    recv_sem=recv_sems.at[i, k - 1], device_id=(px, py, c), device_id_type=pl.DeviceIdType.MESH)
                cp.start()
                copies.append(cp)
        for cp in copies:
            cp.wait()

    any_spec = pl.BlockSpec(memory_space=pl.ANY)
    out_shape = tuple(jax.ShapeDtypeStruct(s.shape, s.dtype) for s in parts)
    outs = pl.pallas_call(
        body, name=name, out_shape=out_shape, in_specs=[any_spec] * n, out_specs=tuple([any_spec] * n),
        scratch_shapes=[pltpu.SemaphoreType.DMA((n, nchip - 1)), pltpu.SemaphoreType.DMA((n, nchip - 1)),
                        pltpu.SemaphoreType.DMA((n,))],
    )(*parts)
    return list(outs)


def _sum_slots(g, name):
    _, rows, cols = g.shape
    tm = _div(rows, 1024, 8)

    def body(g_ref, o_ref):
        acc = g_ref[0]
        for j in range(1, NDEV):
            acc = acc + g_ref[j]
        o_ref[...] = acc

    return pl.pallas_call(
        body, name=name, grid=(rows // tm,), out_shape=jax.ShapeDtypeStruct((rows, cols), F32),
        in_specs=[pl.BlockSpec((NDEV, tm, cols), lambda i: (0, i, 0))],
        out_specs=pl.BlockSpec((tm, cols), lambda i: (i, 0)), compiler_params=_cp(("arbitrary",)))(g)


def _mm_nn(a, b3, out_dtype, name, tm=256, tn=1024):
    m_, k_ = a.shape
    j_, _, nj = b3.shape
    tm = _div(m_, tm, 16)
    tn = _div(nj, tn, LANE)
    per = nj // tn

    def body(a_ref, b_ref, o_ref):
        o_ref[...] = jnp.dot(a_ref[...], b_ref[0], preferred_element_type=F32).astype(o_ref.dtype)

    return pl.pallas_call(
        body, name=name, grid=(j_ * per, m_ // tm), out_shape=jax.ShapeDtypeStruct((m_, j_ * nj), out_dtype),
        in_specs=[pl.BlockSpec((tm, k_), lambda n, m: (m, 0)),
                  pl.BlockSpec((1, k_, tn), lambda n, m: (n // per, 0, n % per))],
        out_specs=pl.BlockSpec((tm, tn), lambda n, m: (m, n)),
        compiler_params=_cp(("arbitrary", "arbitrary")))(a, b3)


def _mm_nt(a, b3, out_dtype, name, tm=256, tn=1024, tk=2048):
    m_, k_ = a.shape
    j_, n_, kj = b3.shape
    tm = _div(m_, tm, 16)
    tn = _div(n_, tn, LANE)
    tk = _div(kj, tk, LANE)
    perk = kj // tk
    nk = k_ // tk

    def body(a_ref, b_ref, o_ref, acc_ref):
        k = pl.program_id(2)

        @pl.when(k == 0)
        def _():
            acc_ref[...] = jnp.zeros_like(acc_ref)

        acc_ref[...] += lax.dot_general(a_ref[...], b_ref[0], (((1,), (1,)), ((), ())), preferred_element_type=F32)

        @pl.when(k == nk - 1)
        def _():
            o_ref[...] = acc_ref[...].astype(o_ref.dtype)

    return pl.pallas_call(
        body, name=name, grid=(n_ // tn, m_ // tm, nk), out_shape=jax.ShapeDtypeStruct((m_, n_), out_dtype),
        in_specs=[pl.BlockSpec((tm, tk), lambda n, m, k: (m, k)),
                  pl.BlockSpec((1, tn, tk), lambda n, m, k: (k // perk, n, k % perk))],
        out_specs=pl.BlockSpec((tm, tn), lambda n, m, k: (m, n)),
        scratch_shapes=[pltpu.VMEM((tm, tn), F32)],
        compiler_params=_cp(("arbitrary", "arbitrary", "arbitrary")))(a, b3)


def _mm_tn(a, b, out_dtype, name, oj=1, tm=1024, tn=512):
    k_, m_ = a.shape
    _, n_ = b.shape
    nj = n_ // oj
    tm = _div(m_, tm, LANE)
    tn = _div(nj, tn, LANE)
    per = nj // tn

    def body(a_ref, b_ref, o_ref):
        o_ref[0] = lax.dot_general(a_ref[...], b_ref[...], (((0,), (0,)), ((), ())),
                                   preferred_element_type=F32).astype(o_ref.dtype)

    return pl.pallas_call(
        body, name=name, grid=(m_ // tm, n_ // tn), out_shape=jax.ShapeDtypeStruct((oj, m_, nj), out_dtype),
        in_specs=[pl.BlockSpec((k_, tm), lambda m, n: (0, m)), pl.BlockSpec((k_, tn), lambda m, n: (0, n))],
        out_specs=pl.BlockSpec((1, tm, tn), lambda m, n: (n // per, m, n % per)),
        compiler_params=_cp(("arbitrary", "arbitrary")))(a, b)


def _mod_fwd(c16, w, bias):
    _, d, cols = w.shape
    tn = _div(cols, 512, LANE)

    def body(c_ref, w_ref, b_ref, o_ref):
        sc = _silu(c_ref[...])
        o_ref[0] = jnp.dot(sc, w_ref[0], preferred_element_type=F32) + b_ref[0]

    return pl.pallas_call(
        body, name="mod_fwd", grid=(2, cols // tn), out_shape=jax.ShapeDtypeStruct((2, 16, cols), F32),
        in_specs=[pl.BlockSpec((16, d), lambda l, n: (0, 0)), pl.BlockSpec((1, d, tn), lambda l, n: (l, 0, n)),
                  pl.BlockSpec((1, 1, tn), lambda l, n: (l, 0, n))],
        out_specs=pl.BlockSpec((1, 16, tn), lambda l, n: (l, 0, n)),
        compiler_params=_cp(("arbitrary", "arbitrary")))(c16, w, bias)


def _adam(w, g, m, v):
    m2 = ADAM_B1 * m + (1.0 - ADAM_B1) * g
    v2 = ADAM_B2 * v + (1.0 - ADAM_B2) * (g * g)
    m_hat = m2 / (1.0 - ADAM_B1 ** ADAM_STEP)
    v_hat = v2 / (1.0 - ADAM_B2 ** ADAM_STEP)
    delta = -ADAM_LR * (m_hat / (jnp.sqrt(v_hat) + ADAM_EPS) + ADAM_WD * w)
    return delta, m2, v2


def _mod_bwd(c16, dm, w, m, v):
    _, d, cols = w.shape
    tm = _div(d, 256, 8)

    def body(c_ref, dm_ref, w_ref, m_ref, v_ref, g_ref, dl_ref, nm_ref, nv_ref, gc_ref):
        sc = _silu(c_ref[...])
        g = lax.dot_general(sc, dm_ref[0], (((0,), (0,)), ((), ())), preferred_element_type=F32,
                            precision=lax.Precision.HIGHEST)
        wv = w_ref[0]
        dl, m2, v2 = _adam(wv, g, m_ref[0], v_ref[0])
        g_ref[0] = g
        dl_ref[0] = dl
        nm_ref[0] = m2
        nv_ref[0] = v2
        gc_ref[0] = jnp.sum(wv * dm_ref[0, 8:9, :], axis=1, keepdims=True)

    big = pl.BlockSpec((1, tm, cols), lambda l, i: (l, i, 0))
    shp = jax.ShapeDtypeStruct(w.shape, F32)
    return pl.pallas_call(
        body, name="mod_bwd", grid=(2, d // tm),
        out_shape=(shp, shp, shp, shp, jax.ShapeDtypeStruct((2, d, 1), F32)),
        in_specs=[pl.BlockSpec((16, tm), lambda l, i: (0, i)), pl.BlockSpec((1, 16, cols), lambda l, i: (l, 0, 0)),
                  big, big, big],
        out_specs=(big, big, big, big, pl.BlockSpec((1, tm, 1), lambda l, i: (l, i, 0))),
        compiler_params=_cp(("arbitrary", "arbitrary")))(c16, dm, w, m, v)


def _row_spec(tm, d, off=0):
    return pl.BlockSpec((tm, d), lambda i: (i + off, 0))


def _vec_spec(d):
    return pl.BlockSpec((1, d), lambda i: (0, 0))


def _norm_fwd(x, nw, shift, scale, name):
    n, d = x.shape
    tm = _div(n, ROW_TILE, 16)

    def body(x_ref, nw_ref, sh_ref, sc_ref, hn_ref, r_ref):
        xv = x_ref[...]
        r = lax.rsqrt(jnp.mean(xv * xv, axis=1, keepdims=True) + EPS)
        hn_ref[...] = (xv * r * nw_ref[...] * (1.0 + sc_ref[...]) + sh_ref[...]).astype(BF16)
        r_ref[...] = r

    return pl.pallas_call(
        body, name=name, grid=(n // tm,),
        out_shape=(jax.ShapeDtypeStruct((n, d), BF16), jax.ShapeDtypeStruct((n, 1), F32)),
        in_specs=[_row_spec(tm, d), _vec_spec(d), _vec_spec(d), _vec_spec(d)],
        out_specs=(_row_spec(tm, d), _row_spec(tm, 1)), compiler_params=_cp(("arbitrary",)))(x, nw, shift, scale)


def _resid_norm_fwd(x, out0, gate, nw, shift, scale):
    n, d = x.shape
    tm = _div(n, ROW_TILE, 16)

    def body(x_ref, o_ref, g_ref, nw_ref, sh_ref, sc_ref, x1_ref, hn_ref, r_ref):
        xv = x_ref[...] + g_ref[...] * o_ref[...]
        r = lax.rsqrt(jnp.mean(xv * xv, axis=1, keepdims=True) + EPS)
        x1_ref[...] = xv
        hn_ref[...] = (xv * r * nw_ref[...] * (1.0 + sc_ref[...]) + sh_ref[...]).astype(BF16)
        r_ref[...] = r

    return pl.pallas_call(
        body, name="resid_norm1", grid=(n // tm,),
        out_shape=(jax.ShapeDtypeStruct((n, d), F32), jax.ShapeDtypeStruct((n, d), BF16),
                   jax.ShapeDtypeStruct((n, 1), F32)),
        in_specs=[_row_spec(tm, d), _row_spec(tm, d), _vec_spec(d), _vec_spec(d), _vec_spec(d), _vec_spec(d)],
        out_specs=(_row_spec(tm, d), _row_spec(tm, d), _row_spec(tm, 1)),
        compiler_params=_cp(("arbitrary",)))(x, out0, gate, nw, shift, scale)


def _loss_head(x1, out1, gate1, fnw, tgt):
    n, d = x1.shape
    tm = _div(n, ROW_TILE, 16)
    nsteps = n // tm

    def body(x_ref, o_ref, g_ref, w_ref, t_ref, dx_ref, do_ref, loss_ref, dw_ref, dg_ref, lacc_ref):
        i = pl.program_id(0)

        @pl.when(i == 0)
        def _():
            lacc_ref[...] = jnp.zeros_like(lacc_ref)
            dw_ref[...] = jnp.zeros_like(dw_ref)
            dg_ref[...] = jnp.zeros_like(dg_ref)

        ov = o_ref[...]
        gv = g_ref[...]
        wv = w_ref[...]
        xv = x_ref[...] + gv * ov
        r = lax.rsqrt(jnp.mean(xv * xv, axis=1, keepdims=True) + EPS)
        xh = xv * r
        diff = xh * wv - t_ref[...]
        lacc_ref[...] += jnp.sum(diff * diff, axis=0, keepdims=True)
        dy = diff * (1.0 / d)
        dw_ref[...] += jnp.sum(dy * xh, axis=0, keepdims=True)
        dxh = dy * wv
        dx = r * (dxh - xh * jnp.mean(xh * dxh, axis=1, keepdims=True))
        dx_ref[...] = dx
        dg_ref[...] += jnp.sum(dx * ov, axis=0, keepdims=True)
        do_ref[...] = (gv * dx).astype(BF16)

        @pl.when(i == nsteps - 1)
        def _():
            loss_ref[...] = (0.5 / d) * jnp.sum(lacc_ref[...], axis=1, keepdims=True)

    vec = jax.ShapeDtypeStruct((1, d), F32)
    return pl.pallas_call(
        body, name="loss_head", grid=(nsteps,),
        out_shape=(jax.ShapeDtypeStruct((n, d), F32), jax.ShapeDtypeStruct((n, d), BF16),
                   jax.ShapeDtypeStruct((1, 1), F32), vec, vec),
        in_specs=[_row_spec(tm, d), _row_spec(tm, d), _vec_spec(d), _vec_spec(d), _row_spec(tm, d)],
        out_specs=(_row_spec(tm, d), _row_spec(tm, d), pl.BlockSpec((1, 1), lambda i: (0, 0)), _vec_spec(d),
                   _vec_spec(d)),
        scratch_shapes=[pltpu.VMEM((1, d), F32)], compiler_params=_cp(("arbitrary",)))(x1, out1, gate1, fnw, tgt)


def _norm_bwd(x, rinv, dhns, nw, scale, name, row_off=0, dx_in=None, resid=None):
    n, d = x.shape
    tm = _div(n, ROW_TILE // 2, 16)
    nd = len(dhns)
    has_in = dx_in is not None
    has_res = resid is not None

    def body(*refs):
        it = iter(refs)
        x_ref, r_ref = next(it), next(it)
        dh_refs = [next(it) for _ in range(nd)]
        nw_ref, sc_ref = next(it), next(it)
        dxin_ref = next(it) if has_in else None
        o_ref, g_ref = (next(it), next(it)) if has_res else (None, None)
        dx_ref = next(it)
        do_ref = next(it) if has_res else None
        dsh_ref, dsc_ref, dnw_ref = next(it), next(it), next(it)
        dg_ref = next(it) if has_res else None
        i = pl.program_id(0)

        @pl.when(i == 0)
        def _():
            dsh_ref[...] = jnp.zeros_like(dsh_ref)
            dsc_ref[...] = jnp.zeros_like(dsc_ref)
            dnw_ref[...] = jnp.zeros_like(dnw_ref)
            if has_res:
                dg_ref[...] = jnp.zeros_like(dg_ref)

        dh = dh_refs[0][...]
        for rr in dh_refs[1:]:
            dh = dh + rr[...]
        r = r_ref[...]
        xh = x_ref[...] * r
        nwv = nw_ref[...]
        mod = 1.0 + sc_ref[...]
        dsh_ref[...] += jnp.sum(dh, axis=0, keepdims=True)
        dhx = dh * xh
        dsc_ref[...] += jnp.sum(dhx * nwv, axis=0, keepdims=True)
        dnw_ref[...] += jnp.sum(dhx * mod, axis=0, keepdims=True)
        dxh = dh * (nwv * mod)
        dx = r * (dxh - xh * jnp.mean(xh * dxh, axis=1, keepdims=True))
        if has_in:
            dx = dx + dxin_ref[...]
        dx_ref[...] = dx
        if has_res:
            dg_ref[...] += jnp.sum(dx * o_ref[...], axis=0, keepdims=True)
            do_ref[...] = (g_ref[...] * dx).astype(BF16)

    ins = [x, rinv] + list(dhns) + [nw, scale]
    in_specs = [_row_spec(tm, d), _row_spec(tm, 1)] + [_row_spec(tm, d, row_off)] * nd + [_vec_spec(d), _vec_spec(d)]
    if has_in:
        ins.append(dx_in)
        in_specs.append(_row_spec(tm, d))
    if has_res:
        ins += list(resid)
        in_specs += [_row_spec(tm, d), _vec_spec(d)]
    vec = jax.ShapeDtypeStruct((1, d), F32)
    out_shape = [jax.ShapeDtypeStruct((n, d), F32)]
    out_specs = [_row_spec(tm, d)]
    if has_res:
        out_shape.append(jax.ShapeDtypeStruct((n, d), BF16))
        out_specs.append(_row_spec(tm, d))
    out_shape += [vec, vec, vec]
    out_specs += [_vec_spec(d)] * 3
    if has_res:
        out_shape.append(vec)
        out_specs.append(_vec_spec(d))
    return pl.pallas_call(
        body, name=name, grid=(n // tm,), out_shape=tuple(out_shape), in_specs=in_specs, out_specs=tuple(out_specs),
        compiler_params=_cp(("arbitrary",)))(*ins)


def _seg_masks(rows, width, seg_bounds, offs):
    t = lax.broadcasted_iota(jnp.int32, (rows, width), 0)
    s0 = jnp.zeros_like(t)
    s1 = jnp.full_like(t, rows)
    for (a, b) in seg_bounds:
        inside = (t >= a) & (t < b)
        s0 = jnp.where(inside, a, s0)
        s1 = jnp.where(inside, b, s1)
    return {off: ((t + off >= s0) & (t + off < s1)) for off in offs}


def _shift_rows(v, off, valid):
    rows = v.shape[0]
    return jnp.where(valid, pltpu.roll(v, (-off) % rows, axis=0), 0.0)


def _conv_fwd_val(xv, wv, taps, masks):
    acc = None
    for tap, off in taps:
        xs = xv if off == 0 else _shift_rows(xv, off, masks[off])
        term = wv[tap:tap + 1, :] * xs
        acc = term if acc is None else acc + term
    return acc


def _conv_bwd_val(xv, dacc, wv, taps, masks):
    dx = None
    dws = []
    for tap, off in taps:
        xs = xv if off == 0 else _shift_rows(xv, off, masks[off])
        dws.append(jnp.sum(dacc * xs, axis=0, keepdims=True))
        ds = dacc if off == 0 else _shift_rows(dacc, -off, masks[-off])
        term = wv[tap:tap + 1, :] * ds
        dx = term if dx is None else dx + term
    return dx, dws


def _taps(k):
    left = k // 2
    return [(j, j - left) for j in range(k)]


def _seq_bounds():
    return [(0, CTX_LEN), (CTX_LEN, CTX_LEN + SEQ)]


def _dn_pre_fwd(proj, convw):
    dd = _dims()
    r = dd["r"]
    nstrip = 3 * DN_HEADS
    taps = _taps(DN_CONV)
    offs = [o for _, o in taps if o != 0]

    def body(x_ref, w_ref, o_ref):
        kind = pl.program_id(0) // DN_HEADS
        masks = _seg_masks(r, DN_HEAD_DIM, _seq_bounds(), offs)
        y = _silu(_conv_fwd_val(x_ref[...], w_ref[...], taps, masks))
        inv = lax.rsqrt(jnp.sum(y * y, axis=1, keepdims=True) + EPS)
        scale = jnp.where(kind == 0, inv * (DN_HEAD_DIM ** -0.5), jnp.where(kind == 1, inv, 1.0))
        o_ref[...] = y * scale

    return pl.pallas_call(
        body, name="dn_pre_fwd", grid=(nstrip,), out_shape=jax.ShapeDtypeStruct((r, 3 * dd["dnw"]), F32),
        in_specs=[pl.BlockSpec((r, DN_HEAD_DIM), lambda j: (0, j)), pl.BlockSpec((8, DN_HEAD_DIM), lambda j: (0, j))],
        out_specs=pl.BlockSpec((r, DN_HEAD_DIM), lambda j: (0, j)), compiler_params=_cp(("arbitrary",)))(proj, convw)


def _dn_pre_bwd(proj, convw, dqkv, kind):
    dd = _dims()
    r = dd["r"]
    taps = _taps(DN_CONV)
    offs = sorted({o for _, o in taps if o != 0} | {-o for _, o in taps if o != 0})
    cscale = DN_HEAD_DIM ** -0.5 if kind == 0 else 1.0

    def body(x_ref, w_ref, d_ref, dx_ref, dw_ref):
        masks = _seg_masks(r, DN_HEAD_DIM, _seq_bounds(), offs)
        xv = x_ref[...]
        wv = w_ref[...]
        acc = _conv_fwd_val(xv, wv, taps, masks)
        dout = d_ref[0] + d_ref[1]
        if kind == 2:
            dy = dout
        else:
            y = _silu(acc)
            inv = lax.rsqrt(jnp.sum(y * y, axis=1, keepdims=True) + EPS)
            dy = cscale * inv * (dout - y * (inv * inv) * jnp.sum(dout * y, axis=1, keepdims=True))
        dacc = dy * _dsilu(acc)
        dx, dws = _conv_bwd_val(xv, dacc, wv, taps, masks)
        dx_ref[...] = dx.astype(BF16)
        dw_ref[...] = jnp.concatenate(dws + [jnp.zeros((8 - len(dws), DN_HEAD_DIM), F32)], axis=0)

    base = kind * DN_HEADS
    return pl.pallas_call(
        body, name=f"dn_pre_bwd{kind}", grid=(DN_HEADS,),
        out_shape=(jax.ShapeDtypeStruct((r, dd["dnw"]), BF16), jax.ShapeDtypeStruct((8, dd["dnw"]), F32)),
        in_specs=[pl.BlockSpec((r, DN_HEAD_DIM), lambda j: (0, base + j)),
                  pl.BlockSpec((8, DN_HEAD_DIM), lambda j: (0, base + j)),
                  pl.BlockSpec((2, r, DN_HEAD_DIM), lambda j: (0, 0, j))],
        out_specs=(pl.BlockSpec((r, DN_HEAD_DIM), lambda j: (0, j)), pl.BlockSpec((8, DN_HEAD_DIM), lambda j: (0, j))),
        compiler_params=_cp(("arbitrary",)))(proj, convw, dqkv)


def _lru_conv_fwd(xl, convw):
    r, lw = xl.shape
    taps = _taps(LRU_CONV)
    offs = [o for _, o in taps if o != 0]

    def body(x_ref, w_ref, o_ref):
        masks = _seg_masks(r, LANE, _seq_bounds(), offs)
        wv = w_ref[...]
        o_ref[...] = _conv_fwd_val(x_ref[...], wv, taps, masks) + wv[4:5, :]

    return pl.pallas_call(
        body, name="lru_conv_fwd", grid=(lw // LANE,), out_shape=jax.ShapeDtypeStruct((r, lw), F32),
        in_specs=[pl.BlockSpec((r, LANE), lambda j: (0, j)), pl.BlockSpec((8, LANE), lambda j: (0, j))],
        out_specs=pl.BlockSpec((r, LANE), lambda j: (0, j)), compiler_params=_cp(("arbitrary",)))(xl, convw)


def _lru_conv_bwd(xl, convw, dxc):
    r, lw = xl.shape
    taps = _taps(LRU_CONV)
    offs = sorted({o for _, o in taps if o != 0} | {-o for _, o in taps if o != 0})

    def body(x_ref, w_ref, d_ref, dx_ref, dw_ref):
        masks = _seg_masks(r, LANE, _seq_bounds(), offs)
        dacc = d_ref[0] + d_ref[1]
        dx, dws = _conv_bwd_val(x_ref[...], dacc, w_ref[...], taps, masks)
        dx_ref[...] = dx
        db = jnp.sum(dacc, axis=0, keepdims=True)
        dw_ref[...] = jnp.concatenate(dws + [db, jnp.zeros((3, LANE), F32)], axis=0)

    return pl.pallas_call(
        body, name="lru_conv_bwd", grid=(lw // LANE,),
        out_shape=(jax.ShapeDtypeStruct((r, lw), F32), jax.ShapeDtypeStruct((8, lw), F32)),
        in_specs=[pl.BlockSpec((r, LANE), lambda j: (0, j)), pl.BlockSpec((8, LANE), lambda j: (0, j)),
                  pl.BlockSpec((2, r, LANE), lambda j: (0, 0, j))],
        out_specs=(pl.BlockSpec((r, LANE), lambda j: (0, j)), pl.BlockSpec((8, LANE), lambda j: (0, j))),
        compiler_params=_cp(("arbitrary",)))(xl, convw, dxc)


def _tri(rev_lane, n, m):
    r = lax.broadcasted_iota(jnp.int32, (n, n), 0)
    c = lax.broadcasted_iota(jnp.int32, (n, n), 1)
    return (r >= c).astype(F32), (r <= c).astype(F32)


def _hdot(a, b):
    return jnp.dot(a, b, preferred_element_type=F32, precision=lax.Precision.HIGHEST)


def _gate_lanes():
    lane = lax.broadcasted_iota(jnp.int32, (DN_CHUNK, LANE), 1)
    group = lane // 32
    rev = (lane % 32) >= DN_HEADS if DN_HEADS * 2 <= 32 else None
    return group, rev


def _dn_gates_fwd(pba, par):
    r = pba.shape[0]
    nch = r // DN_CHUNK
    hh = 2 * DN_HEADS

    def body(x_ref, p_ref, o_ref):
        lo, up = _tri(None, DN_CHUNK, DN_CHUNK)
        ones = jnp.ones((DN_CHUNK, DN_CHUNK), F32)
        lane = lax.broadcasted_iota(jnp.int32, (DN_CHUNK, LANE), 1)
        group = lane // 32
        rev = (lane % 32) >= DN_HEADS
        dtb = p_ref[0:1, :]
        nea = p_ref[1:2, :]

        def step(i, carry):
            rows = pl.ds(pl.multiple_of(i * DN_CHUNK, DN_CHUNK), DN_CHUNK)
            xv = x_ref[rows, :]
            beta = jax.nn.sigmoid(xv)
            g = jnp.where(group == 1, nea * _softplus(xv + dtb), 0.0)
            gc = jnp.where(rev, _hdot(up, g), _hdot(lo, g))
            gt = _hdot(ones, g)
            out = jnp.where(group == 0, beta, jnp.where(group == 1, g, 0.0))
            out = out + jnp.where(group == 2, pltpu.roll(gc, 32, axis=1), 0.0)
            out = out + jnp.where(group == 3, pltpu.roll(gt, 64, axis=1), 0.0)
            o_ref[rows, :] = out
            return carry

        lax.fori_loop(0, nch, step, 0)

    return pl.pallas_call(body, name="dn_gates_fwd", out_shape=jax.ShapeDtypeStruct((r, LANE), F32),
                          compiler_params=_cp(None))(pba, par)


def _dn_gates_bwd(pba, par, dsc):
    r = pba.shape[0]
    nch = r // DN_CHUNK

    def body(x_ref, p_ref, d_ref, dx_ref, dp_ref, acc_ref):
        lo, up = _tri(None, DN_CHUNK, DN_CHUNK)
        ones = jnp.ones((DN_CHUNK, DN_CHUNK), F32)
        lane = lax.broadcasted_iota(jnp.int32, (DN_CHUNK, LANE), 1)
        group = lane // 32
        rev = (lane % 32) >= DN_HEADS
        dtb = p_ref[0:1, :]
        nea = p_ref[1:2, :]
        acc_ref[...] = jnp.zeros_like(acc_ref)

        def step(i, carry):
            rows = pl.ds(pl.multiple_of(i * DN_CHUNK, DN_CHUNK), DN_CHUNK)
            xv = x_ref[rows, :]
            dv = d_ref[rows, :]
            beta = jax.nn.sigmoid(xv)
            z = xv + dtb
            sp = _softplus(z)
            dgc = jnp.where(group == 1, pltpu.roll(dv, LANE - 32, axis=1), 0.0)
            dgt = jnp.where(group == 1, pltpu.roll(dv, LANE - 64, axis=1), 0.0)
            dg = jnp.where(rev, _hdot(lo, dgc), _hdot(up, dgc)) + _hdot(ones, dgt)
            dg = dg + jnp.where(group == 1, dv, 0.0)
            dal = dg * nea * jax.nn.sigmoid(z)
            dpb = dv * beta * (1.0 - beta)
            dx_ref[rows, :] = jnp.where(group == 0, dpb, jnp.where(group == 1, dal, 0.0)).astype(BF16)
            acc_ref[0:1, :] += jnp.sum(jnp.where(group == 1, dal, 0.0), axis=0, keepdims=True)
            acc_ref[1:2, :] += jnp.sum(jnp.where(group == 1, dg * nea * sp, 0.0), axis=0, keepdims=True)
            return carry

        lax.fori_loop(0, nch, step, 0)
        dp_ref[...] = acc_ref[...]

    return pl.pallas_call(
        body, name="dn_gates_bwd",
        out_shape=(jax.ShapeDtypeStruct((r, LANE), BF16), jax.ShapeDtypeStruct((8, LANE), F32)),
        scratch_shapes=[pltpu.VMEM((8, LANE), F32)], compiler_params=_cp(None))(pba, par, dsc)


_B_NN = (((2,), (1,)), ((0,), (0,)))
_B_NT = (((2,), (2,)), ((0,), (0,)))
_B_TN = (((1,), (1,)), ((0,), (0,)))


def _bdot(a, b, dims, exact=False):
    return lax.dot_general(a, b, dims, preferred_element_type=F32,
                           precision=lax.Precision.HIGH if exact else None)


@jax.custom_vjp
def _inv_unit_tri(l):
    n = l.shape[-1]
    ri = lax.broadcasted_iota(jnp.int32, (n, n), 0)
    ci = lax.broadcasted_iota(jnp.int32, (n, n), 1)
    x = jnp.broadcast_to((ri == ci).astype(F32), l.shape)
    inner = 1
    for size in (4, 16, n):
        so, si = size.bit_length() - 1, inner.bit_length() - 1
        same_outer = (ri >> so) == (ci >> so)
        same_inner = (ri >> si) == (ci >> si)
        part = jnp.where(same_outer, jnp.where(same_inner, 0.0, l), 0.0)
        m = part if inner == 1 else _bdot(x, part, _B_NN, exact=True)
        y = x
        for _ in range(size // inner - 1):
            y = x - _bdot(m, y, _B_NN, exact=True)
        x = y
        inner = size
    return x


def _inv_fwd(l):
    a = _inv_unit_tri(l)
    return a, a


def _inv_bwd(a, da):
    t = _bdot(a, da, _B_TN, exact=True)
    return (-_bdot(t, a, _B_NT, exact=True),)


_inv_unit_tri.defvjp(_inv_fwd, _inv_bwd)


def _delta_chunk(q4, k4, v4, sc, s4, rev):
    hps = HEADS_PER_STEP
    c = DN_CHUNK
    hd = DN_HEAD_DIM
    ri = lax.broadcasted_iota(jnp.int32, (c, c), 0)
    ci = lax.broadcasted_iota(jnp.int32, (c, c), 1)
    ahead = (ri - ci) * jnp.where(rev, -1, 1)
    incl = ahead >= 0
    strict = ahead > 0
    sct = sc.T

    def heads(x):
        return jnp.concatenate([x[None, :, j * hd:(j + 1) * hd] for j in range(hps)], axis=0)

    def col(off):
        return jnp.concatenate([sc[None, :, off + j:off + j + 1] for j in range(hps)], axis=0)

    def row(off):
        return jnp.concatenate([sct[None, off + j:off + j + 1, :] for j in range(hps)], axis=0)

    q, k, v = heads(q4), heads(k4), heads(v4)
    beta_c, gc_c, gt_c = col(0), col(hps), col(2 * hps)
    beta_r, gc_r = row(0), row(hps)
    dmat = jnp.where(incl, jnp.exp(jnp.where(incl, gc_c - gc_r, 0.0)), 0.0)
    kk = _bdot(k, k, _B_NT, exact=True)
    qk = _bdot(q, k, _B_NT)
    a = _inv_unit_tri(jnp.where(strict, beta_c * kk * dmat, 0.0))
    w = _bdot(a * (beta_r * jnp.exp(gc_r)), k, _B_NN, exact=True)
    u = _bdot(a * beta_r, v, _B_NN, exact=True)
    u2 = u - _bdot(w, s4, _B_NN)
    o = jnp.exp(gc_c) * _bdot(q, s4, _B_NN) + _bdot(qk * dmat, u2, _B_NN)
    ke = k * jnp.exp(gt_c - gc_c)
    s_new = jnp.exp(gt_c[:, 0:1, :]) * s4 + _bdot(ke, u2, _B_TN)
    return jnp.concatenate([o[j] for j in range(hps)], axis=1), s_new


def _chunk_index(d, n, nctx, ntot):
    rev_idx = jnp.where(n < nctx, nctx - 1 - n, ntot - 1 - (n - nctx))
    return jnp.where(d == 0, n, rev_idx)


def _delta_fwd(qkv, scg):
    dd = _dims()
    r, dnw = dd["r"], dd["dnw"]
    hps = HEADS_PER_STEP
    nhg = DN_HEADS // hps
    bw = hps * DN_HEAD_DIM
    nctx, ntot = CTX_LEN // DN_CHUNK, r // DN_CHUNK

    def body(q_ref, k_ref, v_ref, sc_ref, o_ref, ss_ref, s_ref):
        d, n = pl.program_id(0), pl.program_id(2)

        @pl.when(n == 0)
        def _():
            s_ref[...] = jnp.zeros_like(s_ref)

        s4 = s_ref[...]
        ss_ref[0, 0] = s4
        o4, s4n = _delta_chunk(q_ref[...], k_ref[...], v_ref[...], sc_ref[0], s4, d == 1)
        o_ref[0] = o4
        s_ref[...] = s4n

    def rows(d, g, n):
        return _chunk_index(d, n, nctx, ntot)

    return pl.pallas_call(
        body, name="delta_fwd", grid=(2, nhg, ntot),
        out_shape=(jax.ShapeDtypeStruct((2, r, dnw), F32),
                   jax.ShapeDtypeStruct((2, ntot, DN_HEADS, DN_HEAD_DIM, DN_HEAD_DIM), F32)),
        in_specs=[pl.BlockSpec((DN_CHUNK, bw), lambda d, g, n: (rows(d, g, n), g)),
                  pl.BlockSpec((DN_CHUNK, bw), lambda d, g, n: (rows(d, g, n), nhg + g)),
                  pl.BlockSpec((DN_CHUNK, bw), lambda d, g, n: (rows(d, g, n), 2 * nhg + g)),
                  pl.BlockSpec((1, DN_CHUNK, LANE), lambda d, g, n: (d * nhg + g, rows(d, g, n), 0))],
        out_specs=(pl.BlockSpec((1, DN_CHUNK, bw), lambda d, g, n: (d, rows(d, g, n), g)),
                   pl.BlockSpec((1, 1, hps, DN_HEAD_DIM, DN_HEAD_DIM), lambda d, g, n: (d, n, g, 0, 0))),
        scratch_shapes=[pltpu.VMEM((hps, DN_HEAD_DIM, DN_HEAD_DIM), F32)],
        compiler_params=_cp(("arbitrary", "arbitrary", "arbitrary")))(qkv, qkv, qkv, scg)


def _delta_bwd(qkv, scg, states, do):
    dd = _dims()
    r, dnw = dd["r"], dd["dnw"]
    hps = HEADS_PER_STEP
    nhg = DN_HEADS // hps
    bw = hps * DN_HEAD_DIM
    nctx, ntot = CTX_LEN // DN_CHUNK, r // DN_CHUNK

    def body(q_ref, k_ref, v_ref, sc_ref, ss_ref, do_ref, dq_ref, dk_ref, dv_ref, dsc_ref, ds_ref):
        d, n2 = pl.program_id(0), pl.program_id(2)

        @pl.when(n2 == 0)
        def _():
            ds_ref[...] = jnp.zeros_like(ds_ref)

        fn = functools.partial(_delta_chunk, rev=(d == 1))
        _, vjp = jax.vjp(fn, q_ref[...], k_ref[...], v_ref[...], sc_ref[0], ss_ref[0, 0])
        dq, dk, dv, dsc, ds = vjp((do_ref[...], ds_ref[...]))
        dq_ref[0] = dq
        dk_ref[0] = dk
        dv_ref[0] = dv
        dsc_ref[0] = dsc
        ds_ref[...] = ds

    def rows(d, g, n2):
        return _chunk_index(d, ntot - 1 - n2, nctx, ntot)

    blk = lambda col: pl.BlockSpec((DN_CHUNK, bw), lambda d, g, n2: (rows(d, g, n2), col(g)))
    oblk = pl.BlockSpec((1, DN_CHUNK, bw), lambda d, g, n2: (d, rows(d, g, n2), g))
    scblk = pl.BlockSpec((1, DN_CHUNK, LANE), lambda d, g, n2: (d * nhg + g, rows(d, g, n2), 0))
    big = jax.ShapeDtypeStruct((2, r, dnw), F32)
    return pl.pallas_call(
        body, name="delta_bwd", grid=(2, nhg, ntot),
        out_shape=(big, big, big, jax.ShapeDtypeStruct(scg.shape, F32)),
        in_specs=[blk(lambda g: g), blk(lambda g: nhg + g), blk(lambda g: 2 * nhg + g), scblk,
                  pl.BlockSpec((1, 1, hps, DN_HEAD_DIM, DN_HEAD_DIM), lambda d, g, n2: (d, ntot - 1 - n2, g, 0, 0)),
                  blk(lambda g: g)],
        out_specs=(oblk, oblk, oblk, scblk),
        scratch_shapes=[pltpu.VMEM((hps, DN_HEAD_DIM, DN_HEAD_DIM), F32)],
        compiler_params=_cp(("arbitrary", "arbitrary", "arbitrary")))(qkv, qkv, qkv, scg, states, do)


def _lru_gate_vals(xn, wr, wi, br, bi, lam):
    r = jax.nn.sigmoid(jnp.dot(xn, wr, preferred_element_type=F32) + br)
    i = jax.nn.sigmoid(jnp.dot(xn, wi, preferred_element_type=F32) + bi)
    sp = _softplus(-lam)
    la = -LRU_C * r * sp
    a = jnp.exp(la)
    mm = jnp.sqrt(_neg_expm1(2.0 * la))
    return r, i, sp, la, a, mm


def _lru_gates_fwd(xc, w_r, w_i, vecs):
    r, lw = xc.shape
    bd = lw // LRU_BLOCKS
    tm = _div(r, ROW_TILE, 8)

    def body(x_ref, wr_ref, wi_ref, v_ref, a_ref, b_ref):
        for n in range(LRU_BLOCKS):
            sl = slice(n * bd, (n + 1) * bd)
            xn = x_ref[:, sl]
            _, i, _, _, a, mm = _lru_gate_vals(xn, wr_ref[0, n], wi_ref[0, n], v_ref[0, 0:1, sl], v_ref[0, 1:2, sl],
                                               v_ref[0, 2:3, sl])
            a_ref[0, :, sl] = a
            b_ref[0, :, sl] = mm * (i * xn)

    wspec = pl.BlockSpec((1, LRU_BLOCKS, bd, bd), lambda d, i: (d, 0, 0, 0))
    out = jax.ShapeDtypeStruct((2, r, lw), F32)
    ospec = pl.BlockSpec((1, tm, lw), lambda d, i: (d, i, 0))
    return pl.pallas_call(
        body, name="lru_gates_fwd", grid=(2, r // tm), out_shape=(out, out),
        in_specs=[pl.BlockSpec((tm, lw), lambda d, i: (i, 0)), wspec, wspec,
                  pl.BlockSpec((1, 8, lw), lambda d, i: (d, 0, 0))],
        out_specs=(ospec, ospec), compiler_params=_cp(("arbitrary", "arbitrary")))(xc, w_r, w_i, vecs)


def _lru_gates_bwd(xc, w_r, w_i, vecs, da, db):
    r, lw = xc.shape
    bd = lw // LRU_BLOCKS
    tm = _div(r, ROW_TILE, 8)

    def body(x_ref, wr_ref, wi_ref, v_ref, da_ref, db_ref, dx_ref, dwr_ref, dwi_ref, dv_ref):
        @pl.when(pl.program_id(1) == 0)
        def _():
            dwr_ref[...] = jnp.zeros_like(dwr_ref)
            dwi_ref[...] = jnp.zeros_like(dwi_ref)
            dv_ref[...] = jnp.zeros_like(dv_ref)

        for n in range(LRU_BLOCKS):
            sl = slice(n * bd, (n + 1) * bd)
            xn = x_ref[:, sl]
            wr, wi = wr_ref[0, n], wi_ref[0, n]
            lam = v_ref[0, 2:3, sl]
            rr, ii, sp, la, a, mm = _lru_gate_vals(xn, wr, wi, v_ref[0, 0:1, sl], v_ref[0, 1:2, sl], lam)
            dav, dbv = da_ref[0, :, sl], db_ref[0, :, sl]
            dmm = dbv * (ii * xn)
            dla = dav * a - dmm * (a * a) / jnp.maximum(mm, 1e-30)
            di = dbv * mm * xn
            dxn = dbv * mm * ii
            dr = dla * (-LRU_C * sp)
            dsp = jnp.sum(dla * (-LRU_C * rr), axis=0, keepdims=True)
            dpr = dr * rr * (1.0 - rr)
            dpi = di * ii * (1.0 - ii)
            dxn = dxn + lax.dot_general(dpr, wr, (((1,), (1,)), ((), ())), preferred_element_type=F32)
            dxn = dxn + lax.dot_general(dpi, wi, (((1,), (1,)), ((), ())), preferred_element_type=F32)
            dx_ref[0, :, sl] = dxn
            dwr_ref[0, n] += lax.dot_general(xn, dpr, (((0,), (0,)), ((), ())), preferred_element_type=F32)
            dwi_ref[0, n] += lax.dot_general(xn, dpi, (((0,), (0,)), ((), ())), preferred_element_type=F32)
            dv_ref[0, 0:1, sl] += jnp.sum(dpr, axis=0, keepdims=True)
            dv_ref[0, 1:2, sl] += jnp.sum(dpi, axis=0, keepdims=True)
            dv_ref[0, 2:3, sl] += dsp * (-jax.nn.sigmoid(-lam))

    wspec = pl.BlockSpec((1, LRU_BLOCKS, bd, bd), lambda d, i: (d, 0, 0, 0))
    vspec = pl.BlockSpec((1, 8, lw), lambda d, i: (d, 0, 0))
    big = pl.BlockSpec((1, tm, lw), lambda d, i: (d, i, 0))
    return pl.pallas_call(
        body, name="lru_gates_bwd", grid=(2, r // tm),
        out_shape=(jax.ShapeDtypeStruct((2, r, lw), F32), jax.ShapeDtypeStruct(w_r.shape, F32),
                   jax.ShapeDtypeStruct(w_i.shape, F32), jax.ShapeDtypeStruct((2, 8, lw), F32)),
        in_specs=[pl.BlockSpec((tm, lw), lambda d, i: (i, 0)), wspec, wspec, vspec, big, big],
        out_specs=(big, wspec, wspec, vspec), compiler_params=_cp(("arbitrary", "arbitrary")))(xc, w_r, w_i, vecs, da, db)


def _scan_tiles(rev):
    nctx, nlat = CTX_LEN // 8, SEQ // 8
    return nctx, nlat


def _tile_of(step, rev, nctx, ntot):
    if not rev:
        return step
    return jnp.where(step < nctx, nctx - 1 - step, ntot - 1 - (step - nctx))


def _lru_scan_fwd(a, b, d):
    _, r, lw = a.shape
    cw = _div(lw, 256, LANE)
    nctx, ntot = CTX_LEN // 8, r // 8
    rev = d == 1
    order = list(range(7, -1, -1)) if rev else list(range(8))

    def body(a_ref, b_ref, h_ref):
        def step(s, carry):
            t0 = pl.multiple_of(_tile_of(s, rev, nctx, ntot) * 8, 8)
            at = a_ref[0, pl.ds(t0, 8), :]
            bt = b_ref[0, pl.ds(t0, 8), :]
            rows = [None] * 8
            h = carry
            for i in order:
                h = at[i:i + 1, :] * h + bt[i:i + 1, :]
                rows[i] = h
            h_ref[pl.ds(t0, 8), :] = jnp.concatenate(rows, axis=0)
            return h

        lax.fori_loop(0, ntot, step, jnp.zeros((1, cw), F32))

    return pl.pallas_call(
        body, name=f"lru_scan_fwd{d}", grid=(lw // cw,), out_shape=jax.ShapeDtypeStruct((r, lw), F32),
        in_specs=[pl.BlockSpec((1, r, cw), lambda j: (d, 0, j)), pl.BlockSpec((1, r, cw), lambda j: (d, 0, j))],
        out_specs=pl.BlockSpec((r, cw), lambda j: (0, j)), compiler_params=_cp(("arbitrary",)))(a, b)


def _lru_scan_bwd(a, h, dh, d):
    _, r, lw = a.shape
    cw = _div(lw, 256, LANE)
    nctx, ntot = CTX_LEN // 8, r // 8
    rev = d == 1
    order = list(range(8)) if rev else list(range(7, -1, -1))

    def body(a_ref, h_ref, dh_ref, da_ref, db_ref):
        def step(s2, carry):
            s = ntot - 1 - s2
            t0 = pl.multiple_of(_tile_of(s, rev, nctx, ntot) * 8, 8)
            sp = jnp.maximum(s - 1, 0)
            tp = pl.multiple_of(_tile_of(sp, rev, nctx, ntot) * 8, 8)
            at = a_ref[0, pl.ds(t0, 8), :]
            ht = h_ref[pl.ds(t0, 8), :]
            hp = h_ref[pl.ds(tp, 8), :]
            dht = dh_ref[pl.ds(t0, 8), :]
            last = hp[0:1, :] if rev else hp[7:8, :]
            hprev_tile = jnp.where(s > 0, last, 0.0)
            das, dbs = [None] * 8, [None] * 8
            c = carry
            for i in order:
                lam = dht[i:i + 1, :] + c
                before = i + 1 if rev else i - 1
                hb = ht[before:before + 1, :] if 0 <= before < 8 else hprev_tile
                das[i] = lam * hb
                dbs[i] = lam
                c = at[i:i + 1, :] * lam
            da_ref[pl.ds(t0, 8), :] = jnp.concatenate(das, axis=0)
            db_ref[pl.ds(t0, 8), :] = jnp.concatenate(dbs, axis=0)
            return c

        lax.fori_loop(0, ntot, step, jnp.zeros((1, cw), F32))

    strip = pl.BlockSpec((r, cw), lambda j: (0, j))
    out = jax.ShapeDtypeStruct((r, lw), F32)
    return pl.pallas_call(
        body, name=f"lru_scan_bwd{d}", grid=(lw // cw,), out_shape=(out, out),
        in_specs=[pl.BlockSpec((1, r, cw), lambda j: (d, 0, j)), strip, strip],
        out_specs=(strip, strip), compiler_params=_cp(("arbitrary",)))(a, h, dh)


def _mix_post_fwd(o, hsum, proj, dnw_vec):
    dd = _dims()
    dnw, lw, nmain = dd["dnw"], dd["lw"], dd["nmain"]
    tm = _div(SEQ, ROW_TILE, 16)
    coff = CTX_LEN // tm
    gcol = (3 * dnw + lw) // dnw
    lcol = (3 * dnw + lw + dnw) // lw
    hd = DN_HEAD_DIM

    def body(o_ref, h_ref, g_ref, lg_ref, w_ref, y_ref):
        wv = w_ref[...]
        for hh in range(DN_HEADS):
            sl = slice(hh * hd, (hh + 1) * hd)
            os_ = o_ref[0, :, sl] + o_ref[1, :, sl]
            rr = lax.rsqrt(jnp.mean(os_ * os_, axis=1, keepdims=True) + EPS)
            y_ref[:, sl] = (os_ * rr * wv * _silu(g_ref[:, sl])).astype(BF16)
        y_ref[:, dnw:] = (h_ref[...] * _silu(lg_ref[...])).astype(BF16)

    return pl.pallas_call(
        body, name="mix_post_fwd", grid=(SEQ // tm,), out_shape=jax.ShapeDtypeStruct((SEQ, dnw + lw), BF16),
        in_specs=[pl.BlockSpec((2, tm, dnw), lambda i: (0, i + coff, 0)), pl.BlockSpec((tm, lw), lambda i: (i, 0)),
                  pl.BlockSpec((tm, dnw), lambda i: (i + coff, gcol)), pl.BlockSpec((tm, lw), lambda i: (i + coff, lcol)),
                  pl.BlockSpec((1, hd), lambda i: (0, 0))],
        out_specs=pl.BlockSpec((tm, dnw + lw), lambda i: (i, 0)), compiler_params=_cp(("arbitrary",)))(
            o, hsum, proj, proj, dnw_vec)


def _mix_post_bwd(o, hsum, proj, dnw_vec, dy):
    dd = _dims()
    dnw, lw, r = dd["dnw"], dd["lw"], dd["r"]
    tm = _div(SEQ, ROW_TILE, 16)
    coff = CTX_LEN // tm
    gcol = (3 * dnw + lw) // dnw
    lcol = (3 * dnw + lw + dnw) // lw
    hd = DN_HEAD_DIM

    def body(o_ref, h_ref, g_ref, lg_ref, w_ref, dy_ref, do_ref, dh_ref, dg_ref, dw_ref):
        i = pl.program_id(0)

        @pl.when(i == 0)
        def _():
            dw_ref[...] = jnp.zeros_like(dw_ref)

        @pl.when(i < coff)
        def _():
            do_ref[...] = jnp.zeros_like(do_ref)
            dg_ref[...] = jnp.zeros_like(dg_ref)

        @pl.when(i >= coff)
        def _():
            wv = w_ref[...]
            dwacc = jnp.zeros((1, hd), F32)
            for hh in range(DN_HEADS):
                sl = slice(hh * hd, (hh + 1) * hd)
                os_ = o_ref[0, :, sl] + o_ref[1, :, sl]
                rr = lax.rsqrt(jnp.mean(os_ * os_, axis=1, keepdims=True) + EPS)
                nn = os_ * rr
                gv = g_ref[:, sl]
                dyv = dy_ref[:, sl]
                don = dyv * _silu(gv)
                dg_ref[:, sl] = (dyv * nn * wv * _dsilu(gv)).astype(BF16)
                dwacc = dwacc + jnp.sum(don * nn, axis=0, keepdims=True)
                dn_ = don * wv
                do_ref[:, sl] = rr * (dn_ - nn * jnp.mean(nn * dn_, axis=1, keepdims=True))
            dw_ref[...] += dwacc
            lgv = lg_ref[...]
            dyl = dy_ref[:, dnw:]
            dh_ref[...] = dyl * _silu(lgv)
            dg_ref[:, dnw:] = (dyl * h_ref[...] * _dsilu(lgv)).astype(BF16)

    lat = lambda i: jnp.maximum(i - coff, 0)
    return pl.pallas_call(
        body, name="mix_post_bwd", grid=(r // tm,),
        out_shape=(jax.ShapeDtypeStruct((r, dnw), F32), jax.ShapeDtypeStruct((SEQ, lw), F32),
                   jax.ShapeDtypeStruct((r, dnw + lw), BF16), jax.ShapeDtypeStruct((1, hd), F32)),
        in_specs=[pl.BlockSpec((2, tm, dnw), lambda i: (0, i, 0)), pl.BlockSpec((tm, lw), lambda i: (lat(i), 0)),
                  pl.BlockSpec((tm, dnw), lambda i: (i, gcol)), pl.BlockSpec((tm, lw), lambda i: (i, lcol)),
                  pl.BlockSpec((1, hd), lambda i: (0, 0)), pl.BlockSpec((tm, dnw + lw), lambda i: (lat(i), 0))],
        out_specs=(pl.BlockSpec((tm, dnw), lambda i: (i, 0)), pl.BlockSpec((tm, lw), lambda i: (lat(i), 0)),
                   pl.BlockSpec((tm, dnw + lw), lambda i: (i, 0)), pl.BlockSpec((1, hd), lambda i: (0, 0))),
        compiler_params=_cp(("arbitrary",)))(o, hsum, proj, proj, dnw_vec, dy)


def _sc_masks(tm, cw):
    t = lax.broadcasted_iota(jnp.int32, (tm, cw), 0) % GRID_W
    return {-1: t != 0, 1: t != GRID_W - 1}


def _sc_mix_fwd(p, convw):
    s, w4 = p.shape
    w = w4 // 4
    tm = _div(s, 512, GRID_W if GRID_W % 16 == 0 else 16 * GRID_W)
    cw = _div(w, 512, LANE)
    nc = w // cw
    taps = _taps(SC_CONV)

    def body(b_ref, c_ref, x_ref, g_ref, w_ref, y_ref):
        masks = _sc_masks(tm, cw)
        u = c_ref[...] * x_ref[...]
        z = _conv_fwd_val(u, w_ref[...], taps, masks)
        y_ref[...] = (b_ref[...] * z * _silu(g_ref[...])).astype(BF16)

    blk = lambda k: pl.BlockSpec((tm, cw), lambda c, i: (i, k * nc + c))
    return pl.pallas_call(
        body, name="sc_mix_fwd", grid=(nc, s // tm), out_shape=jax.ShapeDtypeStruct((s, w), BF16),
        in_specs=[blk(0), blk(1), blk(2), blk(3), pl.BlockSpec((8, cw), lambda c, i: (0, c))],
        out_specs=pl.BlockSpec((tm, cw), lambda c, i: (i, c)),
        compiler_params=_cp(("arbitrary", "arbitrary")))(p, p, p, p, convw)


def _sc_mix_bwd(p, convw, dy):
    s, w4 = p.shape
    w = w4 // 4
    tm = _div(s, 512, GRID_W if GRID_W % 16 == 0 else 16 * GRID_W)
    cw = _div(w, 512, LANE)
    nc = w // cw
    taps = _taps(SC_CONV)

    def body(b_ref, c_ref, x_ref, g_ref, w_ref, dy_ref, db_ref, dc_ref, dx_ref, dg_ref, dw_ref):
        @pl.when(pl.program_id(1) == 0)
        def _():
            dw_ref[...] = jnp.zeros_like(dw_ref)

        masks = _sc_masks(tm, cw)
        wv = w_ref[...]
        bv, cv, xv, gv, dyv = b_ref[...], c_ref[...], x_ref[...], g_ref[...], dy_ref[...]
        u = cv * xv
        z = _conv_fwd_val(u, wv, taps, masks)
        sg = _silu(gv)
        db_ref[...] = (dyv * z * sg).astype(BF16)
        dg_ref[...] = (dyv * bv * z * _dsilu(gv)).astype(BF16)
        dz = dyv * bv * sg
        du, dws = _conv_bwd_val(u, dz, wv, taps, masks)
        dc_ref[...] = (du * xv).astype(BF16)
        dx_ref[...] = (du * cv).astype(BF16)
        dw_ref[...] += jnp.concatenate(dws + [jnp.zeros((8 - len(dws), cw), F32)], axis=0)

    blk = lambda k: pl.BlockSpec((tm, cw), lambda c, i: (i, k * nc + c))
    oblk = pl.BlockSpec((tm, cw), lambda c, i: (i, c))
    wblk = pl.BlockSpec((8, cw), lambda c, i: (0, c))
    out = jax.ShapeDtypeStruct((s, w), BF16)
    return pl.pallas_call(
        body, name="sc_mix_bwd", grid=(nc, s // tm),
        out_shape=(out, out, out, out, jax.ShapeDtypeStruct((8, w), F32)),
        in_specs=[blk(0), blk(1), blk(2), blk(3), wblk, oblk],
        out_specs=(oblk, oblk, oblk, oblk, wblk),
        compiler_params=_cp(("arbitrary", "arbitrary")))(p, p, p, p, convw, dy)


def _adamw_slots(w, slots, m, v, name):
    rows, cols = w.shape
    nslot = slots.shape[0]
    tm = _div(rows, max(16, (64 * 4096) // cols), 16)

    def body(w_ref, s_ref, m_ref, v_ref, g_ref, dl_ref, nm_ref, nv_ref):
        g = s_ref[0].astype(F32)
        for j in range(1, nslot):
            g = g + s_ref[j].astype(F32)
        dl, m2, v2 = _adam(w_ref[...], g, m_ref[...], v_ref[...])
        g_ref[...] = g
        dl_ref[...] = dl
        nm_ref[...] = m2
        nv_ref[...] = v2

    blk = pl.BlockSpec((tm, cols), lambda i: (i, 0))
    out = jax.ShapeDtypeStruct((rows, cols), F32)
    return pl.pallas_call(
        body, name=name, grid=(rows // tm,), out_shape=(out, out, out, out),
        in_specs=[blk, pl.BlockSpec((nslot, tm, cols), lambda i: (0, i, 0)), blk, blk],
        out_specs=(blk, blk, blk, blk), compiler_params=_cp(("arbitrary",)))(w, slots, m, v)


def _adamw_flat(w, g, m, v, name):
    rows, cols = w.shape
    tm = _div(rows, 1024, 8)

    def body(w_ref, g_ref, m_ref, v_ref, dl_ref, nm_ref, nv_ref):
        dl, m2, v2 = _adam(w_ref[...], g_ref[...], m_ref[...], v_ref[...])
        dl_ref[...] = dl
        nm_ref[...] = m2
        nv_ref[...] = v2

    blk = pl.BlockSpec((tm, cols), lambda i: (i, 0))
    out = jax.ShapeDtypeStruct((rows, cols), F32)
    return pl.pallas_call(
        body, name=name, grid=(rows // tm,), out_shape=(out, out, out), in_specs=[blk] * 4, out_specs=(blk, blk, blk),
        compiler_params=_cp(("arbitrary",)))(w, g, m, v)


def _pack(arrs):
    flat = [a.reshape(-1).astype(F32) for a in arrs]
    sizes = [f.shape[0] for f in flat]
    total = sum(sizes)
    padded = -(-total // 1024) * 1024
    if padded > total:
        flat.append(jnp.zeros((padded - total,), F32))
    offs, o = [], 0
    for s_ in sizes:
        offs.append(o)
        o += s_
    return jnp.concatenate(flat).reshape(padded // LANE, LANE), offs


def _unpack(flat2d, offs, shapes):
    flat = flat2d.reshape(-1)
    out = []
    for o, shp in zip(offs, shapes):
        n = 1
        for s_ in shp:
            n *= s_
        out.append(flat[o:o + n].reshape(shp))
    return out


def _pad_rows(a, rows):
    return jnp.concatenate([a, jnp.zeros((rows - a.shape[0],) + a.shape[1:], a.dtype)], axis=0)


def _to_col_major(t):
    n, ch = t.shape
    return t.reshape(n // GRID_W, GRID_W, ch).swapaxes(0, 1).reshape(n, ch)


def _to_raster(t):
    n, ch = t.shape
    return t.reshape(GRID_W, n // GRID_W, ch).swapaxes(0, 1).reshape(n, ch)


def _scg_layout(sc):
    r = sc.shape[0]
    hps = HEADS_PER_STEP
    nhg = DN_HEADS // hps
    t = sc.reshape(r, 4, 32)[:, :, :2 * DN_HEADS].reshape(r, 4, 2, nhg, hps)
    t = jnp.stack([t[:, 0], t[:, 2], t[:, 3]], axis=1)
    t = t.transpose(2, 3, 0, 1, 4).reshape(2 * nhg, r, 3 * hps)
    return jnp.concatenate([t, jnp.zeros((2 * nhg, r, LANE - 3 * hps), F32)], axis=2)


def _scg_layout_bwd(dscg):
    r = dscg.shape[1]
    hps = HEADS_PER_STEP
    nhg = DN_HEADS // hps
    t = dscg[:, :, :3 * hps].reshape(2, nhg, r, 3, hps).transpose(2, 3, 0, 1, 4).reshape(r, 3, 2 * DN_HEADS)
    t = jnp.concatenate([t, jnp.zeros((r, 3, 32 - 2 * DN_HEADS), F32)], axis=2) if 2 * DN_HEADS < 32 else t
    z = jnp.zeros((r, 1, 32), F32)
    return jnp.concatenate([t[:, 0:1], z, t[:, 1:2], t[:, 2:3]], axis=1).reshape(r, LANE)


def _gather_cols(full8, width):
    return full8.transpose(1, 0, 2).reshape(full8.shape[1], NDEV * width)


def kernel(x, c, ctx, c_ctx, mod_w, mod_b, norm_w, ab_w_in, ab_qkv_conv, ab_a_log, ab_dt_bias, ab_dn_norm, ab_lru_conv_w, ab_lru_conv_b, ab_lru_w_r, ab_lru_b_r, ab_lru_w_i, ab_lru_b_i, ab_lru_lambda, ab_w_out, sc_w_in, sc_conv, sc_w_out, final_norm_w, loss_target, m_c_ctx, m_mod_w, m_mod_b, m_norm_w, m_ab_w_in, m_ab_qkv_conv, m_ab_a_log, m_ab_dt_bias, m_ab_dn_norm, m_ab_lru_conv_w, m_ab_lru_conv_b, m_ab_lru_w_r, m_ab_lru_b_r, m_ab_lru_w_i, m_ab_lru_b_i, m_ab_lru_lambda, m_ab_w_out, m_sc_w_in, m_sc_conv, m_sc_w_out, m_final_norm_w, v_c_ctx, v_mod_w, v_mod_b, v_norm_w, v_ab_w_in, v_ab_qkv_conv, v_ab_a_log, v_ab_dt_bias, v_ab_dn_norm, v_ab_lru_conv_w, v_ab_lru_conv_b, v_ab_lru_w_r, v_ab_lru_b_r, v_ab_lru_w_i, v_ab_lru_b_i, v_ab_lru_lambda, v_ab_w_out, v_sc_w_in, v_sc_conv, v_sc_w_out, v_final_norm_w):
    dd = _dims()
    d, dnw, lw, r = dd["d"], dd["dnw"], dd["lw"], dd["r"]
    off_lru, off_beta, ab_state, ab_in, nmain = dd["off_lru"], dd["off_beta"], dd["ab_state"], dd["ab_in"], dd["nmain"]
    hh2 = 2 * DN_HEADS
    me = _me()
    x = x[0]
    ctx = ctx[0]
    tgt = loss_target[0]

    small_sharded = [ab_qkv_conv[0], ab_lru_conv_w[0], ab_lru_b_r[0], ab_lru_b_i[0], ab_lru_lambda[0], sc_conv[0]]
    pk, offs0 = _pack([c] + small_sharded)
    (g0,) = _exchange([pk], "gather_small_in", scatter=False)
    parts = [_unpack(g0[j], offs0, [c.shape] + [a.shape for a in small_sharded]) for j in range(NDEV)]
    c_all = jnp.concatenate([p[0] for p in parts], axis=0)
    qkv_conv, lru_conv_w, lru_b_r, lru_b_i, lru_lam, sc_conv_f = [
        jnp.concatenate([p[i] for p in parts], axis=-1) for i in range(1, 7)]

    mcols = mod_w.shape[2]
    c16 = jnp.concatenate([c_all, c_ctx[None, :], jnp.zeros((7, d), F32)], axis=0)
    bias = lax.dynamic_slice_in_dim(mod_b, me * mcols, mcols, axis=1)[:, None, :]
    mod_loc = _mod_fwd(c16, mod_w, bias)
    (mod_g,) = _exchange([mod_loc], "gather_mod", scatter=False)
    mod_all = mod_g.transpose(1, 2, 0, 3).reshape(2, 16, NDEV * mcols)
    mine = lax.dynamic_slice_in_dim(mod_all, me, 1, axis=1)[:, 0]
    shift = [mine[l, None, 0:d] for l in range(2)]
    scale = [mine[l, None, d:2 * d] for l in range(2)]
    gate = [mine[l, None, 2 * d:3 * d] for l in range(2)]
    shift_c, scale_c = mod_all[0, 8, None, 0:d], mod_all[0, 8, None, d:2 * d]
    nw0, nw1 = norm_w[0][None, :], norm_w[1][None, :]

    wsrc = [ab_w_in[0].astype(BF16), ab_w_out[0].astype(BF16), sc_w_in[0].astype(BF16), sc_w_out[0].astype(BF16)]
    g_abin, g_about, g_scin, g_scout = _gather_two_level(wsrc, "gather_weights")
    w_full = _gather_cols(g_abin, ab_in // NDEV)
    w_main = jnp.concatenate([w_full[:, :off_beta], w_full[:, ab_state:]], axis=1)
    w_ba = jnp.concatenate([w_full[:, off_beta:ab_state], jnp.zeros((d, LANE - 2 * hh2), BF16)], axis=1)
    w_about = g_about.reshape(dnw + lw, d)
    w_scout = g_scout.reshape(d, d)

    hn_c, rinv_c = _norm_fwd(ctx, nw0, shift_c, scale_c, "norm0_ctx")
    hn_x, rinv_x = _norm_fwd(x, nw0, shift[0], scale[0], "norm0_x")
    hn0 = jnp.concatenate([hn_c, hn_x], axis=0)
    proj = _mm_nn(hn0, w_main[None], F32, "proj_main")
    pba = _mm_nn(hn0, w_ba[None], F32, "proj_ba")

    convw8 = _pad_rows(qkv_conv, 8)
    qkv = _dn_pre_fwd(proj, convw8)
    zpad = jnp.zeros((32 - hh2,), F32)
    par = jnp.stack([jnp.concatenate([jnp.zeros((32,), F32), ab_dt_bias[0].reshape(-1), zpad, jnp.zeros((64,), F32)]),
                     jnp.concatenate([jnp.zeros((32,), F32), -jnp.exp(ab_a_log[0].reshape(-1)), zpad,
                                      jnp.zeros((64,), F32)])] + [jnp.zeros((LANE,), F32)] * 6)
    if hh2 < 32:
        pba_l = jnp.concatenate([pba[:, :hh2], jnp.zeros((r, 32 - hh2), F32), pba[:, hh2:2 * hh2],
                                 jnp.zeros((r, LANE - 32 - hh2), F32)], axis=1)
    else:
        pba_l = pba
    sc = _dn_gates_fwd(pba_l, par)
    scg = _scg_layout(sc)
    o_dn, states = _delta_fwd(qkv, scg)

    xl = proj[:, off_lru:off_beta]
    xl_cat = jnp.concatenate([xl[:CTX_LEN], _to_col_major(xl[CTX_LEN:])], axis=0)
    lconv8 = jnp.concatenate([lru_conv_w, ab_lru_conv_b[0][None, :], jnp.zeros((3, lw), F32)], axis=0)
    xc = _lru_conv_fwd(xl_cat, lconv8)
    lvecs = jnp.stack([lru_b_r, lru_b_i, lru_lam] + [jnp.zeros_like(lru_lam)] * 5, axis=1)
    w_r, w_i = ab_lru_w_r[0], ab_lru_w_i[0]
    la_a, la_b = _lru_gates_fwd(xc, w_r, w_i, lvecs)
    h_dir = [_lru_scan_fwd(la_a, la_b, 0), _lru_scan_fwd(la_a, la_b, 1)]
    hsum = _to_raster(h_dir[0][CTX_LEN:] + h_dir[1][CTX_LEN:])

    dnn = ab_dn_norm[0][None, :]
    y0 = _mix_post_fwd(o_dn, hsum, proj, dnn)
    out0 = _mm_nn(y0, w_about[None], F32, "out0")

    x1, hn1, rinv1 = _resid_norm_fwd(x, out0, gate[0], nw1, shift[1], scale[1])
    p1 = _mm_nn(hn1, g_scin, F32, "proj_sc")
    scw8 = _pad_rows(sc_conv_f, 8)
    y1 = _sc_mix_fwd(p1, scw8)
    out1 = _mm_nn(y1, w_scout[None], F32, "out1")

    fnw = final_norm_w[None, :]
    dx2, dout1, loss11, dfnw, dgate1 = _loss_head(x1, out1, gate[1], fnw, tgt)
    dy1 = _mm_nt(dout1, w_scout[None], F32, "dy1")
    gw_scout = _mm_tn(y1, dout1, BF16, "gw_scout")[0].reshape(NDEV, d // NDEV, d)
    dpb, dpc, dpx, dpg, dscw8 = _sc_mix_bwd(p1, scw8, dy1)
    dp1 = jnp.concatenate([dpb, dpc, dpx, dpg], axis=1)
    dhn1 = _mm_nt(dp1, g_scin, F32, "dhn1", tk=2048)
    gw_scin = _mm_tn(hn1, dp1, BF16, "gw_scin", oj=NDEV)
    dx1, dout0, dshift1, dscale1, dnw1, dgate0 = _norm_bwd(
        x1, rinv1, [dhn1], nw1, scale[1], "norm1_bwd", dx_in=dx2, resid=(out0, gate[0]))

    dy0 = _mm_nt(dout0, w_about[None], F32, "dy0")
    gw_about = _mm_tn(y0, dout0, BF16, "gw_about")[0].reshape(NDEV, (dnw + lw) // NDEV, d)
    do, dh_r, dgates, ddnn = _mix_post_bwd(o_dn, hsum, proj, dnn, dy0)

    dq, dk, dv, dscg = _delta_bwd(qkv, scg, states, do)
    dsc = _scg_layout_bwd(dscg)
    dpba, dpar = _dn_gates_bwd(pba_l, par, dsc)
    dpq, dcw_q = _dn_pre_bwd(proj, convw8, dq, 0)
    dpk, dcw_k = _dn_pre_bwd(proj, convw8, dk, 1)
    dpv, dcw_v = _dn_pre_bwd(proj, convw8, dv, 2)

    dh_cat = jnp.concatenate([jnp.zeros((CTX_LEN, lw), F32), _to_col_major(dh_r)], axis=0)
    da0, db0 = _lru_scan_bwd(la_a, h_dir[0], dh_cat, 0)
    da1, db1 = _lru_scan_bwd(la_a, h_dir[1], dh_cat, 1)
    dxc, dw_r, dw_i, dlvecs = _lru_gates_bwd(xc, w_r, w_i, lvecs, jnp.stack([da0, da1]), jnp.stack([db0, db1]))
    dxl_cat, dlconv8 = _lru_conv_bwd(xl_cat, lconv8, dxc)
    dxl = jnp.concatenate([dxl_cat[:CTX_LEN], _to_raster(dxl_cat[CTX_LEN:])], axis=0).astype(BF16)

    dproj = jnp.concatenate([dpq, dpk, dpv, dxl, dgates], axis=1)
    if hh2 < 32:
        dpba_w = jnp.concatenate([dpba[:, :hh2], dpba[:, 32:32 + hh2], jnp.zeros((r, LANE - 2 * hh2), BF16)], axis=1)
    else:
        dpba_w = dpba
    dhn0_a = _mm_nt(dproj, w_main[None], F32, "dhn0_main")
    dhn0_b = _mm_nt(dpba_w, w_ba[None], F32, "dhn0_ba")
    gw_main = _mm_tn(hn0, dproj, BF16, "gw_main")[0]
    gw_ba = _mm_tn(hn0, dpba_w, BF16, "gw_ba")[0]
    gw_full = jnp.concatenate([gw_main[:, :off_beta], gw_ba[:, :2 * hh2], gw_main[:, off_beta:]], axis=1)
    gw_abin = gw_full.reshape(d, NDEV, ab_in // NDEV).transpose(1, 0, 2)

    nt = _div(SEQ, ROW_TILE // 2, 16)
    grad_x, dshift0, dscale0, dnw0_x = _norm_bwd(x, rinv_x, [dhn0_a, dhn0_b], nw0, scale[0], "norm0_bwd_x",
                                                 row_off=CTX_LEN // nt, dx_in=dx1)
    _, dshift_c, dscale_c, dnw0_c = _norm_bwd(ctx, rinv_c, [dhn0_a, dhn0_b], nw0, scale_c, "norm0_bwd_ctx")

    gws = [gw_abin, gw_about, gw_scin, gw_scout]
    sib = _sibling_swap(gws, "swap_grads")
    parts = [_chip_partial(o_, s_, f"chip_partial{i}") for i, (o_, s_) in enumerate(zip(gws, sib))]
    s_abin, s_about, s_scin, s_scout = _chip_exchange(parts, "scatter_grads")
    big = {}
    big["ab_w_in"] = _adamw_slots(ab_w_in[0], s_abin, m_ab_w_in[0], v_ab_w_in[0], "adamw_ab_w_in")
    big["ab_w_out"] = _adamw_slots(ab_w_out[0], s_about, m_ab_w_out[0], v_ab_w_out[0], "adamw_ab_w_out")
    big["sc_w_in"] = _adamw_slots(sc_w_in[0], s_scin, m_sc_w_in[0], v_sc_w_in[0], "adamw_sc_w_in")
    big["sc_w_out"] = _adamw_slots(sc_w_out[0], s_scout, m_sc_w_out[0], v_sc_w_out[0], "adamw_sc_w_out")

    dmod_own = jnp.stack([jnp.concatenate([dshift0, dscale0, dgate0], axis=1)[0],
                          jnp.concatenate([dshift1, dscale1, dgate1], axis=1)[0]])
    dmod_c = jnp.concatenate([dshift_c, dscale_c], axis=1)[0]
    g_qkv_conv = jnp.concatenate([dcw_q[:DN_CONV], dcw_k[:DN_CONV], dcw_v[:DN_CONV]], axis=1)
    small = [
        ("loss", loss11.reshape(1)),
        ("norm_w", jnp.concatenate([dnw0_x + dnw0_c, dnw1], axis=0)),
        ("dmod_own", dmod_own),
        ("dmod_c", dmod_c),
        ("qkv_conv", g_qkv_conv),
        ("a_log", dpar[1, 32:32 + hh2].reshape(2, DN_HEADS)),
        ("dt_bias", dpar[0, 32:32 + hh2].reshape(2, DN_HEADS)),
        ("dn_norm", ddnn[0]),
        ("lru_conv_w", dlconv8[:LRU_CONV]),
        ("lru_conv_b", dlconv8[LRU_CONV]),
        ("lru_w_r", dw_r),
        ("lru_w_i", dw_i),
        ("lru_b_r", dlvecs[:, 0]),
        ("lru_b_i", dlvecs[:, 1]),
        ("lru_lambda", dlvecs[:, 2]),
        ("sc_conv", dscw8[:SC_CONV]),
        ("final_norm_w", dfnw[0]),
    ]
    names = [n for n, _ in small]
    shapes = [a.shape for _, a in small]
    spk, soffs = _pack([a for _, a in small])
    (sg,) = _exchange([spk], "gather_small_grads", scatter=False)
    tot = dict(zip(names, _unpack(_sum_slots(sg, "sum_small_grads"), soffs, shapes)))
    i_own = names.index("dmod_own")
    dmod_all = jnp.stack([_unpack(sg[j], soffs[i_own:i_own + 1], shapes[i_own:i_own + 1])[0] for j in range(NDEV)])

    dmc_full = jnp.concatenate([tot["dmod_c"], jnp.zeros((d,), F32)])
    rows9 = jnp.concatenate([dmod_all, jnp.stack([dmc_full, jnp.zeros((3 * d,), F32)])[None],
                             jnp.zeros((7, 2, 3 * d), F32)], axis=0)
    dm = lax.dynamic_slice_in_dim(rows9, me * mcols, mcols, axis=2).transpose(1, 0, 2)
    g_modw, dl_modw, nm_modw, nv_modw, gcc = _mod_bwd(c16, dm, mod_w, m_mod_w, v_mod_w)
    gcc_pk, _ = _pack([gcc[0, :, 0]])
    (gcc_g,) = _exchange([gcc_pk], "gather_c_ctx", scatter=False)
    gcc_sum = _sum_slots(gcc_g, "sum_c_ctx").reshape(-1)[:d]
    g_c_ctx_pre = gcc_sum

    grad_mod_b = tot["dmod_own"] + jnp.stack([dmc_full, jnp.zeros((3 * d,), F32)])

    def shard(a):
        wd = a.shape[-1] // NDEV
        return lax.dynamic_slice_in_dim(a, me * wd, wd, axis=a.ndim - 1)

    small_w = [
        ("c_ctx", c_ctx, None, m_c_ctx, v_c_ctx),
        ("mod_b", mod_b, grad_mod_b, m_mod_b, v_mod_b),
        ("norm_w", norm_w, tot["norm_w"], m_norm_w, v_norm_w),
        ("ab_qkv_conv", ab_qkv_conv, shard(tot["qkv_conv"])[None], m_ab_qkv_conv, v_ab_qkv_conv),
        ("ab_a_log", ab_a_log, tot["a_log"][None], m_ab_a_log, v_ab_a_log),
        ("ab_dt_bias", ab_dt_bias, tot["dt_bias"][None], m_ab_dt_bias, v_ab_dt_bias),
        ("ab_dn_norm", ab_dn_norm, tot["dn_norm"][None], m_ab_dn_norm, v_ab_dn_norm),
        ("ab_lru_conv_w", ab_lru_conv_w, shard(tot["lru_conv_w"])[None], m_ab_lru_conv_w, v_ab_lru_conv_w),
        ("ab_lru_conv_b", ab_lru_conv_b, tot["lru_conv_b"][None], m_ab_lru_conv_b, v_ab_lru_conv_b),
        ("ab_lru_w_r", ab_lru_w_r, tot["lru_w_r"][None], m_ab_lru_w_r, v_ab_lru_w_r),
        ("ab_lru_b_r", ab_lru_b_r, shard(tot["lru_b_r"])[None], m_ab_lru_b_r, v_ab_lru_b_r),
        ("ab_lru_w_i", ab_lru_w_i, tot["lru_w_i"][None], m_ab_lru_w_i, v_ab_lru_w_i),
        ("ab_lru_b_i", ab_lru_b_i, shard(tot["lru_b_i"])[None], m_ab_lru_b_i, v_ab_lru_b_i),
        ("ab_lru_lambda", ab_lru_lambda, shard(tot["lru_lambda"])[None], m_ab_lru_lambda, v_ab_lru_lambda),
        ("sc_conv", sc_conv, shard(tot["sc_conv"])[None], m_sc_conv, v_sc_conv),
        ("final_norm_w", final_norm_w, tot["final_norm_w"], m_final_norm_w, v_final_norm_w),
    ]
    sg_cc = jax.nn.sigmoid(c_ctx)
    g_c_ctx = g_c_ctx_pre * (sg_cc * (1.0 + c_ctx * (1.0 - sg_cc)))
    small_w[0] = ("c_ctx", c_ctx, g_c_ctx, m_c_ctx, v_c_ctx)
    wshapes = [w.shape for _, w, _, _, _ in small_w]
    wpk, woffs = _pack([w for _, w, _, _, _ in small_w])
    gpk, _ = _pack([g for _, _, g, _, _ in small_w])
    mpk, _ = _pack([m for _, _, _, m, _ in small_w])
    vpk, _ = _pack([v for _, _, _, _, v in small_w])
    dl_pk, nm_pk, nv_pk = _adamw_flat(wpk, gpk, mpk, vpk, "adamw_small")
    sm_g = {n: g for n, _, g, _, _ in small_w}
    sm_dl = dict(zip([n for n, *_ in small_w], _unpack(dl_pk, woffs, wshapes)))
    sm_nm = dict(zip([n for n, *_ in small_w], _unpack(nm_pk, woffs, wshapes)))
    sm_nv = dict(zip([n for n, *_ in small_w], _unpack(nv_pk, woffs, wshapes)))

    grads, deltas, new_m, new_v = {}, {}, {}, {}
    for n in sm_g:
        grads[n], deltas[n], new_m[n], new_v[n] = sm_g[n], sm_dl[n], sm_nm[n], sm_nv[n]
    grads["mod_w"], deltas["mod_w"], new_m["mod_w"], new_v["mod_w"] = g_modw, dl_modw, nm_modw, nv_modw
    for n in ("ab_w_in", "ab_w_out", "sc_w_in", "sc_w_out"):
        g, dl, m2, v2 = big[n]
        grads[n], deltas[n], new_m[n], new_v[n] = g[None], dl[None], m2[None], v2[None]

    order = ["c_ctx", "mod_w", "mod_b", "norm_w", "ab_w_in", "ab_qkv_conv", "ab_a_log", "ab_dt_bias", "ab_dn_norm",
             "ab_lru_conv_w", "ab_lru_conv_b", "ab_lru_w_r", "ab_lru_b_r", "ab_lru_w_i", "ab_lru_b_i", "ab_lru_lambda",
             "ab_w_out", "sc_w_in", "sc_conv", "sc_w_out", "final_norm_w"]
    loss = tot["loss"][0]
    return (loss, grad_x[None], *[grads[n] for n in order], *[deltas[n] for n in order],
            *[new_m[n] for n in order], *[new_v[n] for n in order])
```

```python
import functools

import jax
import jax.numpy as jnp
from jax import lax
from jax.experimental import pallas as pl
from jax.experimental.pallas import tpu as pltpu

F32 = jnp.float32
BF16 = jnp.bfloat16

NDEV = 8
D_MODEL = 4096
SEQ = 4096
CTX_LEN = 256
GRID_W = 64
EPS = 1e-6
DN_HEADS = 16
DN_HEAD_DIM = 128
DN_CONV = 4
DN_CHUNK = 64
LRU_WIDTH = 2048
LRU_BLOCKS = 16
LRU_CONV = 4
LRU_C = 8.0
SC_CONV = 3
ADAM_LR = 0.001
ADAM_B1 = 0.9
ADAM_B2 = 0.999
ADAM_EPS = 1e-08
ADAM_WD = 0.01
ADAM_STEP = 10

ROW_TILE = 256
HEADS_PER_STEP = 16
LANE = 128
VMEM_LIMIT = 56 * 1024 * 1024


def _dims():
    d = D_MODEL
    dnw = DN_HEADS * DN_HEAD_DIM
    lw = LRU_WIDTH
    off_lru = 3 * dnw
    off_beta = off_lru + lw
    ab_state = off_beta + 4 * DN_HEADS
    ab_in = ab_state + dnw + lw
    return dict(d=d, dnw=dnw, lw=lw, off_lru=off_lru, off_beta=off_beta, ab_state=ab_state, ab_in=ab_in,
                nmain=3 * dnw + lw + dnw + lw, r=CTX_LEN + SEQ, lbd=lw // LRU_BLOCKS)


def _div(n, pref, mult):
    best = None
    for t in range(mult, min(n, pref) + 1, mult):
        if n % t == 0:
            best = t
    return best if best is not None else n


def _cp(sem, vmem=VMEM_LIMIT):
    return pltpu.CompilerParams(dimension_semantics=sem, vmem_limit_bytes=vmem)


def _silu(z):
    return z * jax.nn.sigmoid(z)


def _dsilu(z):
    s = jax.nn.sigmoid(z)
    return s * (1.0 + z * (1.0 - s))


def _softplus(z):
    return jnp.maximum(z, 0.0) + jnp.log1p(jnp.exp(-jnp.abs(z)))


def _neg_expm1(z):
    series = -z * (1.0 + z * (0.5 + z * (1.0 / 6.0 + z * (1.0 / 24.0))))
    return jnp.where(z > -0.05, series, 1.0 - jnp.exp(z))


def _me():
    return 4 * lax.axis_index("x") + 2 * lax.axis_index("y") + lax.axis_index("c")


def _exchange(srcs, name, scatter):
    n = len(srcs)

    def body(*refs):
        src_refs, out_refs = refs[:n], refs[n:2 * n]
        send_sems, recv_sems, local_sems = refs[2 * n:]
        x, y, c = lax.axis_index("x"), lax.axis_index("y"), lax.axis_index("c")
        me = 4 * x + 2 * y + c
        copies = []
        for i in range(n):
            mine = src_refs[i].at[me] if scatter else src_refs[i]
            lc = pltpu.make_async_copy(mine, out_refs[i].at[me], local_sems.at[i])
            lc.start()
            copies.append(lc)
            for k in range(1, NDEV):
                px = 1 - x if (k >> 2) & 1 else x
                py = 1 - y if (k >> 1) & 1 else y
                pc = 1 - c if k & 1 else c
                peer = 4 * px + 2 * py + pc
                src = src_refs[i].at[peer] if scatter else src_refs[i]
                cp = pltpu.make_async_remote_copy(
                    src_ref=src, dst_ref=out_refs[i].at[me], send_sem=send_sems.at[i, k - 1],
                    recv_sem=recv_sems.at[i, k - 1], device_id=(px, py, pc), device_id_type=pl.DeviceIdType.MESH)
                cp.start()
                copies.append(cp)
        for cp in copies:
            cp.wait()

    any_spec = pl.BlockSpec(memory_space=pl.ANY)
    out_shape = tuple(jax.ShapeDtypeStruct(s.shape if scatter else (NDEV,) + s.shape, s.dtype) for s in srcs)
    outs = pl.pallas_call(
        body, name=name, out_shape=out_shape, in_specs=[any_spec] * n, out_specs=tuple([any_spec] * n),
        scratch_shapes=[pltpu.SemaphoreType.DMA((n, NDEV - 1)), pltpu.SemaphoreType.DMA((n, NDEV - 1)),
                        pltpu.SemaphoreType.DMA((n,))],
    )(*srcs)
    return list(outs)


def _flip(v, bit):
    return 1 - v if bit else v


def _gather_ops(src_refs, out_refs, send_sems, recv_sems, local_sems):
    n = len(src_refs)

    def setup():
        x, y, c = lax.axis_index("x"), lax.axis_index("y"), lax.axis_index("c")
        chips = [(_flip(x, k >> 1), _flip(y, k & 1)) for k in (1, 2, 3)]
        return x, y, c, chips

    def copy(i, sem, slot, to, src=None):
        return pltpu.make_async_remote_copy(
            src_ref=out_refs[i].at[slot] if src is None else src, dst_ref=out_refs[i].at[slot],
            send_sem=send_sems.at[i, sem], recv_sem=recv_sems.at[i, sem], device_id=to,
            device_id_type=pl.DeviceIdType.MESH)

    def own(i, me):
        return pltpu.make_async_copy(src_refs[i], out_refs[i].at[me], local_sems.at[i])

    def first(i, x, y, c, chips):
        me = 4 * x + 2 * y + c
        return [copy(i, 0, me, (x, y, 1 - c), src=src_refs[i])] + [
            copy(i, 1 + j, me, (px, py, c), src=src_refs[i]) for j, (px, py) in enumerate(chips)]

    def start():
        x, y, c, chips = setup()
        for i in range(n):
            own(i, 4 * x + 2 * y + c).start()
            for cp in first(i, x, y, c, chips):
                cp.start()

    def finish():
        x, y, c, chips = setup()
        here = (x, y, c)
        passed = []
        for i in range(n):
            for j, (px, py) in enumerate(chips):
                theirs = 4 * px + 2 * py + c
                copy(i, 1 + j, theirs, here).wait_recv()
                fwd = copy(i, 4 + j, theirs, (x, y, 1 - c))
                fwd.start()
                passed.append(fwd)
        for i in range(n):
            copy(i, 0, 4 * x + 2 * y + (1 - c), here).wait_recv()
            for j, (px, py) in enumerate(chips):
                copy(i, 4 + j, 4 * px + 2 * py + (1 - c), here).wait_recv()
        for i in range(n):
            for cp in first(i, x, y, c, chips):
                cp.wait_send()
        for cp in passed:
            cp.wait_send()
        for i in range(n):
            own(i, 4 * x + 2 * y + c).wait()

    return start, finish


def _chipx_ops(src_refs, out_refs, send_sems, recv_sems, local_sems):
    n = len(src_refs)

    def copies():
        x, y, c = lax.axis_index("x"), lax.axis_index("y"), lax.axis_index("c")
        mine = 2 * x + y
        res = []
        for i in range(n):
            res.append(pltpu.make_async_copy(src_refs[i].at[mine], out_refs[i].at[mine], local_sems.at[i]))
            for k in (1, 2, 3):
                px, py = _flip(x, k >> 1), _flip(y, k & 1)
                res.append(pltpu.make_async_remote_copy(
                    src_ref=src_refs[i].at[2 * px + py], dst_ref=out_refs[i].at[mine], send_sem=send_sems.at[i, k - 1],
                    recv_sem=recv_sems.at[i, k - 1], device_id=(px, py, c), device_id_type=pl.DeviceIdType.MESH))
        return res

    def start():
        for cp in copies():
            cp.start()

    def finish():
        for cp in copies():
            cp.wait()

    return start, finish


def _rider(kind, arrays):
    n = len(arrays)
    any_spec = pl.BlockSpec(memory_space=pl.ANY)
    if kind == "gather":
        out_shapes = [jax.ShapeDtypeStruct((NDEV,) + a.shape, a.dtype) for a in arrays]
        nsem, make = NDEV - 1, _gather_ops
    else:
        out_shapes = [jax.ShapeDtypeStruct(a.shape, a.dtype) for a in arrays]
        nsem, make = NDEV // 2 - 1, _chipx_ops
    scratch = [pltpu.SemaphoreType.DMA((n, nsem)), pltpu.SemaphoreType.DMA((n, nsem)), pltpu.SemaphoreType.DMA((n,))]
    return list(arrays), [any_spec] * n, out_shapes, [any_spec] * n, scratch, make


def _exchange_call(kind, arrays, name):
    n = len(arrays)
    ins, in_specs, out_shapes, out_specs, scratch, make = _rider(kind, arrays)

    def body(*refs):
        start, finish = make(refs[:n], refs[n:2 * n], *refs[2 * n:])
        start()
        finish()

    outs = pl.pallas_call(body, name=name, out_shape=tuple(out_shapes), in_specs=in_specs, out_specs=tuple(out_specs),
                          scratch_shapes=scratch)(*ins)
    return list(outs)


def _gather_two_level(srcs, name):
    return _exchange_call("gather", srcs, name)


def _call_with_rider(body, rider, *, name, grid, out_shape, in_specs, out_specs, scratch_shapes, compiler_params,
                     operands):
    if rider is None:
        outs = pl.pallas_call(body, name=name, grid=grid, out_shape=tuple(out_shape), in_specs=list(in_specs),
                              out_specs=tuple(out_specs), scratch_shapes=list(scratch_shapes),
                              compiler_params=compiler_params)(*operands)
        return list(outs), []
    r_ins, r_in_specs, r_out_shapes, r_out_specs, r_scratch, make = _rider(*rider)
    ni, no, ns, nr = len(in_specs), len(out_shape), len(scratch_shapes), len(r_ins)

    def riding(*refs):
        ins, rins = refs[:ni], refs[ni:ni + nr]
        outs, routs = refs[ni + nr:ni + nr + no], refs[ni + nr + no:ni + nr + no + nr]
        scr, rscr = refs[ni + 2 * nr + no:ni + 2 * nr + no + ns], refs[ni + 2 * nr + no + ns:]
        start, finish = make(rins, routs, *rscr)
        ids = [pl.program_id(a) for a in range(len(grid))]
        first, last = ids[0] == 0, ids[0] == grid[0] - 1
        for a in range(1, len(grid)):
            first = jnp.logical_and(first, ids[a] == 0)
            last = jnp.logical_and(last, ids[a] == grid[a] - 1)

        @pl.when(first)
        def _():
            start()

        body(*ins, *outs, *scr)

        @pl.when(last)
        def _():
            finish()

    outs = pl.pallas_call(
        riding, name=name, grid=grid, out_shape=tuple(out_shape) + tuple(r_out_shapes),
        in_specs=list(in_specs) + r_in_specs, out_specs=tuple(out_specs) + tuple(r_out_specs),
        scratch_shapes=list(scratch_shapes) + r_scratch, compiler_params=compiler_params)(*operands, *r_ins)
    return list(outs[:no]), list(outs[no:])


def _sibling_swap(srcs, name):
    n = len(srcs)
    nchip = NDEV // 2

    def body(*refs):
        src_refs, out_refs = refs[:n], refs[n:2 * n]
        send_sems, recv_sems = refs[2 * n:]
        x, y, c = lax.axis_index("x"), lax.axis_index("y"), lax.axis_index("c")
        copies = []
        for i in range(n):
            for q in range(nchip):
                cp = pltpu.make_async_remote_copy(
                    src_ref=src_refs[i].at[2 * q + (1 - c)], dst_ref=out_refs[i].at[q], send_sem=send_sems.at[i, q],
                    recv_sem=recv_sems.at[i, q], device_id=(x, y, 1 - c), device_id_type=pl.DeviceIdType.MESH)
                cp.start()
                copies.append(cp)
        for cp in copies:
            cp.wait()

    any_spec = pl.BlockSpec(memory_space=pl.ANY)
    out_shape = tuple(jax.ShapeDtypeStruct((nchip,) + s.shape[1:], s.dtype) for s in srcs)
    outs = pl.pallas_call(
        body, name=name, out_shape=out_shape, in_specs=[any_spec] * n, out_specs=tuple([any_spec] * n),
        scratch_shapes=[pltpu.SemaphoreType.DMA((n, nchip)), pltpu.SemaphoreType.DMA((n, nchip))],
    )(*srcs)
    return list(outs)


def _chip_partial(own, sib, name):
    _, rows, cols = own.shape
    tm = _div(rows, max(16, (128 * 4096) // cols), 16)

    def body(c_ref, o_ref, s_ref, p_ref):
        p_ref[0] = (o_ref[0].astype(F32) + s_ref[0].astype(F32)).astype(BF16)

    core = lax.axis_index("c").astype(jnp.int32).reshape(1)
    grid_spec = pltpu.PrefetchScalarGridSpec(
        num_scalar_prefetch=1, grid=(NDEV // 2, rows // tm),
        in_specs=[pl.BlockSpec((1, tm, cols), lambda q, i, c_ref: (2 * q + c_ref[0], i, 0)),
                  pl.BlockSpec((1, tm, cols), lambda q, i, c_ref: (q, i, 0))],
        out_specs=pl.BlockSpec((1, tm, cols), lambda q, i, c_ref: (q, i, 0)))
    return pl.pallas_call(
        body, name=name, grid_spec=grid_spec, out_shape=jax.ShapeDtypeStruct(sib.shape, BF16),
        compiler_params=_cp(("arbitrary", "arbitrary")))(core, own, sib)


def _chip_exchange(parts, name):
    return _exchange_call("chipx", parts, name)


def _sum_slots(g, name):
    _, rows, cols = g.shape
    tm = _div(rows, 1024, 8)

    def body(g_ref, o_ref):
        acc = g_ref[0]
        for j in range(1, NDEV):
            acc = acc + g_ref[j]
        o_ref[...] = acc

    return pl.pallas_call(
        body, name=name, grid=(rows // tm,), out_shape=jax.ShapeDtypeStruct((rows, cols), F32),
        in_specs=[pl.BlockSpec((NDEV, tm, cols), lambda i: (0, i, 0))],
        out_specs=pl.BlockSpec((tm, cols), lambda i: (i, 0)), compiler_params=_cp(("arbitrary",)))(g)


def _mm_nn(a, b3, out_dtype, name, tm=256, tn=1024):
    m_, k_ = a.shape
    j_, _, nj = b3.shape
    tm = _div(m_, tm, 16)
    tn = _div(nj, tn, LANE)
    per = nj // tn

    def body(a_ref, b_ref, o_ref):
        o_ref[...] = jnp.dot(a_ref[...], b_ref[0], preferred_element_type=F32).astype(o_ref.dtype)

    return pl.pallas_call(
        body, name=name, grid=(j_ * per, m_ // tm), out_shape=jax.ShapeDtypeStruct((m_, j_ * nj), out_dtype),
        in_specs=[pl.BlockSpec((tm, k_), lambda n, m: (m, 0)),
                  pl.BlockSpec((1, k_, tn), lambda n, m: (n // per, 0, n % per))],
        out_specs=pl.BlockSpec((tm, tn), lambda n, m: (m, n)),
        compiler_params=_cp(("arbitrary", "arbitrary")))(a, b3)


def _mm_nt(a, b3, out_dtype, name, tm=256, tn=1024, tk=2048, rider=None):
    m_, k_ = a.shape
    j_, n_, kj = b3.shape
    tm = _div(m_, tm, 16)
    tn = _div(n_, tn, LANE)
    tk = _div(kj, tk, LANE)
    perk = kj // tk
    nk = k_ // tk

    def body(a_ref, b_ref, o_ref, acc_ref):
        k = pl.program_id(2)

        @pl.when(k == 0)
        def _():
            acc_ref[...] = jnp.zeros_like(acc_ref)

        acc_ref[...] += lax.dot_general(a_ref[...], b_ref[0], (((1,), (1,)), ((), ())), preferred_element_type=F32)

        @pl.when(k == nk - 1)
        def _():
            o_ref[...] = acc_ref[...].astype(o_ref.dtype)

    outs, rode = _call_with_rider(
        body, rider, name=name, grid=(n_ // tn, m_ // tm, nk),
        out_shape=(jax.ShapeDtypeStruct((m_, n_), out_dtype),),
        in_specs=[pl.BlockSpec((tm, tk), lambda n, m, k: (m, k)),
                  pl.BlockSpec((1, tn, tk), lambda n, m, k: (k // perk, n, k % perk))],
        out_specs=(pl.BlockSpec((tm, tn), lambda n, m, k: (m, n)),),
        scratch_shapes=[pltpu.VMEM((tm, tn), F32)],
        compiler_params=_cp(("arbitrary", "arbitrary", "arbitrary")), operands=(a, b3))
    return outs[0] if rider is None else (outs[0], rode)


def _mm_tn(a, b, out_dtype, name, oj=1, tm=1024, tn=512):
    k_, m_ = a.shape
    _, n_ = b.shape
    nj = n_ // oj
    tm = _div(m_, tm, LANE)
    tn = _div(nj, tn, LANE)
    per = nj // tn

    def body(a_ref, b_ref, o_ref):
        o_ref[0] = lax.dot_general(a_ref[...], b_ref[...], (((0,), (0,)), ((), ())),
                                   preferred_element_type=F32).astype(o_ref.dtype)

    return pl.pallas_call(
        body, name=name, grid=(m_ // tm, n_ // tn), out_shape=jax.ShapeDtypeStruct((oj, m_, nj), out_dtype),
        in_specs=[pl.BlockSpec((k_, tm), lambda m, n: (0, m)), pl.BlockSpec((k_, tn), lambda m, n: (0, n))],
        out_specs=pl.BlockSpec((1, tm, tn), lambda m, n: (n // per, m, n % per)),
        compiler_params=_cp(("arbitrary", "arbitrary")))(a, b)


def _mod_fwd(c16, w, bias):
    _, d, cols = w.shape
    tn = _div(cols, 512, LANE)

    def body(c_ref, w_ref, b_ref, o_ref):
        sc = _silu(c_ref[...])
        o_ref[0] = jnp.dot(sc, w_ref[0], preferred_element_type=F32) + b_ref[0]

    return pl.pallas_call(
        body, name="mod_fwd", grid=(2, cols // tn), out_shape=jax.ShapeDtypeStruct((2, 16, cols), F32),
        in_specs=[pl.BlockSpec((16, d), lambda l, n: (0, 0)), pl.BlockSpec((1, d, tn), lambda l, n: (l, 0, n)),
                  pl.BlockSpec((1, 1, tn), lambda l, n: (l, 0, n))],
        out_specs=pl.BlockSpec((1, 16, tn), lambda l, n: (l, 0, n)),
        compiler_params=_cp(("arbitrary", "arbitrary")))(c16, w, bias)


def _adam(w, g, m, v):
    m2 = ADAM_B1 * m + (1.0 - ADAM_B1) * g
    v2 = ADAM_B2 * v + (1.0 - ADAM_B2) * (g * g)
    m_hat = m2 / (1.0 - ADAM_B1 ** ADAM_STEP)
    v_hat = v2 / (1.0 - ADAM_B2 ** ADAM_STEP)
    delta = -ADAM_LR * (m_hat / (jnp.sqrt(v_hat) + ADAM_EPS) + ADAM_WD * w)
    return delta, m2, v2


def _mod_bwd(c16, dm, w, m, v):
    _, d, cols = w.shape
    tm = _div(d, 256, 8)

    def body(c_ref, dm_ref, w_ref, m_ref, v_ref, g_ref, dl_ref, nm_ref, nv_ref, gc_ref):
        sc = _silu(c_ref[...])
        g = lax.dot_general(sc, dm_ref[0], (((0,), (0,)), ((), ())), preferred_element_type=F32,
                            precision=lax.Precision.HIGHEST)
        wv = w_ref[0]
        dl, m2, v2 = _adam(wv, g, m_ref[0], v_ref[0])
        g_ref[0] = g
        dl_ref[0] = dl
        nm_ref[0] = m2
        nv_ref[0] = v2
        gc_ref[0] = jnp.sum(wv * dm_ref[0, 8:9, :], axis=1, keepdims=True)

    big = pl.BlockSpec((1, tm, cols), lambda l, i: (l, i, 0))
    shp = jax.ShapeDtypeStruct(w.shape, F32)
    return pl.pallas_call(
        body, name="mod_bwd", grid=(2, d // tm),
        out_shape=(shp, shp, shp, shp, jax.ShapeDtypeStruct((2, d, 1), F32)),
        in_specs=[pl.BlockSpec((16, tm), lambda l, i: (0, i)), pl.BlockSpec((1, 16, cols), lambda l, i: (l, 0, 0)),
                  big, big, big],
        out_specs=(big, big, big, big, pl.BlockSpec((1, tm, 1), lambda l, i: (l, i, 0))),
        compiler_params=_cp(("arbitrary", "arbitrary")))(c16, dm, w, m, v)


def _row_spec(tm, d, off=0):
    return pl.BlockSpec((tm, d), lambda i: (i + off, 0))


def _vec_spec(d):
    return pl.BlockSpec((1, d), lambda i: (0, 0))


def _norm_fwd(x, nw, shift, scale, name):
    n, d = x.shape
    tm = _div(n, ROW_TILE, 16)

    def body(x_ref, nw_ref, sh_ref, sc_ref, hn_ref, r_ref):
        xv = x_ref[...]
        r = lax.rsqrt(jnp.mean(xv * xv, axis=1, keepdims=True) + EPS)
        hn_ref[...] = (xv * r * nw_ref[...] * (1.0 + sc_ref[...]) + sh_ref[...]).astype(BF16)
        r_ref[...] = r

    return pl.pallas_call(
        body, name=name, grid=(n // tm,),
        out_shape=(jax.ShapeDtypeStruct((n, d), BF16), jax.ShapeDtypeStruct((n, 1), F32)),
        in_specs=[_row_spec(tm, d), _vec_spec(d), _vec_spec(d), _vec_spec(d)],
        out_specs=(_row_spec(tm, d), _row_spec(tm, 1)), compiler_params=_cp(("arbitrary",)))(x, nw, shift, scale)


def _resid_norm_fwd(x, out0, gate, nw, shift, scale):
    n, d = x.shape
    tm = _div(n, ROW_TILE, 16)

    def body(x_ref, o_ref, g_ref, nw_ref, sh_ref, sc_ref, x1_ref, hn_ref, r_ref):
        xv = x_ref[...] + g_ref[...] * o_ref[...]
        r = lax.rsqrt(jnp.mean(xv * xv, axis=1, keepdims=True) + EPS)
        x1_ref[...] = xv
        hn_ref[...] = (xv * r * nw_ref[...] * (1.0 + sc_ref[...]) + sh_ref[...]).astype(BF16)
        r_ref[...] = r

    return pl.pallas_call(
        body, name="resid_norm1", grid=(n // tm,),
        out_shape=(jax.ShapeDtypeStruct((n, d), F32), jax.ShapeDtypeStruct((n, d), BF16),
                   jax.ShapeDtypeStruct((n, 1), F32)),
        in_specs=[_row_spec(tm, d), _row_spec(tm, d), _vec_spec(d), _vec_spec(d), _vec_spec(d), _vec_spec(d)],
        out_specs=(_row_spec(tm, d), _row_spec(tm, d), _row_spec(tm, 1)),
        compiler_params=_cp(("arbitrary",)))(x, out0, gate, nw, shift, scale)


def _loss_head(x1, out1, gate1, fnw, tgt):
    n, d = x1.shape
    tm = _div(n, ROW_TILE, 16)
    nsteps = n // tm

    def body(x_ref, o_ref, g_ref, w_ref, t_ref, dx_ref, do_ref, loss_ref, dw_ref, dg_ref, lacc_ref):
        i = pl.program_id(0)

        @pl.when(i == 0)
        def _():
            lacc_ref[...] = jnp.zeros_like(lacc_ref)
            dw_ref[...] = jnp.zeros_like(dw_ref)
            dg_ref[...] = jnp.zeros_like(dg_ref)

        ov = o_ref[...]
        gv = g_ref[...]
        wv = w_ref[...]
        xv = x_ref[...] + gv * ov
        r = lax.rsqrt(jnp.mean(xv * xv, axis=1, keepdims=True) + EPS)
        xh = xv * r
        diff = xh * wv - t_ref[...]
        lacc_ref[...] += jnp.sum(diff * diff, axis=0, keepdims=True)
        dy = diff * (1.0 / d)
        dw_ref[...] += jnp.sum(dy * xh, axis=0, keepdims=True)
        dxh = dy * wv
        dx = r * (dxh - xh * jnp.mean(xh * dxh, axis=1, keepdims=True))
        dx_ref[...] = dx
        dg_ref[...] += jnp.sum(dx * ov, axis=0, keepdims=True)
        do_ref[...] = (gv * dx).astype(BF16)

        @pl.when(i == nsteps - 1)
        def _():
            loss_ref[...] = (0.5 / d) * jnp.sum(lacc_ref[...], axis=1, keepdims=True)

    vec = jax.ShapeDtypeStruct((1, d), F32)
    return pl.pallas_call(
        body, name="loss_head", grid=(nsteps,),
        out_shape=(jax.ShapeDtypeStruct((n, d), F32), jax.ShapeDtypeStruct((n, d), BF16),
                   jax.ShapeDtypeStruct((1, 1), F32), vec, vec),
        in_specs=[_row_spec(tm, d), _row_spec(tm, d), _vec_spec(d), _vec_spec(d), _row_spec(tm, d)],
        out_specs=(_row_spec(tm, d), _row_spec(tm, d), pl.BlockSpec((1, 1), lambda i: (0, 0)), _vec_spec(d),
                   _vec_spec(d)),
        scratch_shapes=[pltpu.VMEM((1, d), F32)], compiler_params=_cp(("arbitrary",)))(x1, out1, gate1, fnw, tgt)


def _norm_bwd(x, rinv, dhns, nw, scale, name, row_off=0, dx_in=None, resid=None):
    n, d = x.shape
    tm = _div(n, ROW_TILE // 2, 16)
    nd = len(dhns)
    has_in = dx_in is not None
    has_res = resid is not None

    def body(*refs):
        it = iter(refs)
        x_ref, r_ref = next(it), next(it)
        dh_refs = [next(it) for _ in range(nd)]
        nw_ref, sc_ref = next(it), next(it)
        dxin_ref = next(it) if has_in else None
        o_ref, g_ref = (next(it), next(it)) if has_res else (None, None)
        dx_ref = next(it)
        do_ref = next(it) if has_res else None
        dsh_ref, dsc_ref, dnw_ref = next(it), next(it), next(it)
        dg_ref = next(it) if has_res else None
        i = pl.program_id(0)

        @pl.when(i == 0)
        def _():
            dsh_ref[...] = jnp.zeros_like(dsh_ref)
            dsc_ref[...] = jnp.zeros_like(dsc_ref)
            dnw_ref[...] = jnp.zeros_like(dnw_ref)
            if has_res:
                dg_ref[...] = jnp.zeros_like(dg_ref)

        dh = dh_refs[0][...]
        for rr in dh_refs[1:]:
            dh = dh + rr[...]
        r = r_ref[...]
        xh = x_ref[...] * r
        nwv = nw_ref[...]
        mod = 1.0 + sc_ref[...]
        dsh_ref[...] += jnp.sum(dh, axis=0, keepdims=True)
        dhx = dh * xh
        dsc_ref[...] += jnp.sum(dhx * nwv, axis=0, keepdims=True)
        dnw_ref[...] += jnp.sum(dhx * mod, axis=0, keepdims=True)
        dxh = dh * (nwv * mod)
        dx = r * (dxh - xh * jnp.mean(xh * dxh, axis=1, keepdims=True))
        if has_in:
            dx = dx + dxin_ref[...]
        dx_ref[...] = dx
        if has_res:
            dg_ref[...] += jnp.sum(dx * o_ref[...], axis=0, keepdims=True)
            do_ref[...] = (g_ref[...] * dx).astype(BF16)

    ins = [x, rinv] + list(dhns) + [nw, scale]
    in_specs = [_row_spec(tm, d), _row_spec(tm, 1)] + [_row_spec(tm, d, row_off)] * nd + [_vec_spec(d), _vec_spec(d)]
    if has_in:
        ins.append(dx_in)
        in_specs.append(_row_spec(tm, d))
    if has_res:
        ins += list(resid)
        in_specs += [_row_spec(tm, d), _vec_spec(d)]
    vec = jax.ShapeDtypeStruct((1, d), F32)
    out_shape = [jax.ShapeDtypeStruct((n, d), F32)]
    out_specs = [_row_spec(tm, d)]
    if has_res:
        out_shape.append(jax.ShapeDtypeStruct((n, d), BF16))
        out_specs.append(_row_spec(tm, d))
    out_shape += [vec, vec, vec]
    out_specs += [_vec_spec(d)] * 3
    if has_res:
        out_shape.append(vec)
        out_specs.append(_vec_spec(d))
    return pl.pallas_call(
        body, name=name, grid=(n // tm,), out_shape=tuple(out_shape), in_specs=in_specs, out_specs=tuple(out_specs),
        compiler_params=_cp(("arbitrary",)))(*ins)


def _seg_masks(rows, width, seg_bounds, offs):
    t = lax.broadcasted_iota(jnp.int32, (rows, width), 0)
    s0 = jnp.zeros_like(t)
    s1 = jnp.full_like(t, rows)
    for (a, b) in seg_bounds:
        inside = (t >= a) & (t < b)
        s0 = jnp.where(inside, a, s0)
        s1 = jnp.where(inside, b, s1)
    return {off: ((t + off >= s0) & (t + off < s1)) for off in offs}


def _shift_rows(v, off, valid):
    rows = v.shape[0]
    return jnp.where(valid, pltpu.roll(v, (-off) % rows, axis=0), 0.0)


def _conv_fwd_val(xv, wv, taps, masks):
    acc = None
    for tap, off in taps:
        xs = xv if off == 0 else _shift_rows(xv, off, masks[off])
        term = wv[tap:tap + 1, :] * xs
        acc = term if acc is None else acc + term
    return acc


def _conv_bwd_val(xv, dacc, wv, taps, masks):
    dx = None
    dws = []
    for tap, off in taps:
        xs = xv if off == 0 else _shift_rows(xv, off, masks[off])
        dws.append(jnp.sum(dacc * xs, axis=0, keepdims=True))
        ds = dacc if off == 0 else _shift_rows(dacc, -off, masks[-off])
        term = wv[tap:tap + 1, :] * ds
        dx = term if dx is None else dx + term
    return dx, dws


def _taps(k):
    left = k // 2
    return [(j, j - left) for j in range(k)]


def _seq_bounds():
    return [(0, CTX_LEN), (CTX_LEN, CTX_LEN + SEQ)]


def _dn_pre_fwd(proj, convw):
    dd = _dims()
    r = dd["r"]
    nstrip = 3 * DN_HEADS
    taps = _taps(DN_CONV)
    offs = [o for _, o in taps if o != 0]

    def body(x_ref, w_ref, o_ref):
        kind = pl.program_id(0) // DN_HEADS
        masks = _seg_masks(r, DN_HEAD_DIM, _seq_bounds(), offs)
        y = _silu(_conv_fwd_val(x_ref[...], w_ref[...], taps, masks))
        inv = lax.rsqrt(jnp.sum(y * y, axis=1, keepdims=True) + EPS)
        scale = jnp.where(kind == 0, inv * (DN_HEAD_DIM ** -0.5), jnp.where(kind == 1, inv, 1.0))
        o_ref[...] = y * scale

    return pl.pallas_call(
        body, name="dn_pre_fwd", grid=(nstrip,), out_shape=jax.ShapeDtypeStruct((r, 3 * dd["dnw"]), F32),
        in_specs=[pl.BlockSpec((r, DN_HEAD_DIM), lambda j: (0, j)), pl.BlockSpec((8, DN_HEAD_DIM), lambda j: (0, j))],
        out_specs=pl.BlockSpec((r, DN_HEAD_DIM), lambda j: (0, j)), compiler_params=_cp(("arbitrary",)))(proj, convw)


def _dn_pre_bwd(proj, convw, dqkv, kind):
    dd = _dims()
    r = dd["r"]
    taps = _taps(DN_CONV)
    offs = sorted({o for _, o in taps if o != 0} | {-o for _, o in taps if o != 0})
    cscale = DN_HEAD_DIM ** -0.5 if kind == 0 else 1.0

    def body(x_ref, w_ref, d_ref, dx_ref, dw_ref):
        masks = _seg_masks(r, DN_HEAD_DIM, _seq_bounds(), offs)
        xv = x_ref[...]
        wv = w_ref[...]
        acc = _conv_fwd_val(xv, wv, taps, masks)
        dout = d_ref[0] + d_ref[1]
        if kind == 2:
            dy = dout
        else:
            y = _silu(acc)
            inv = lax.rsqrt(jnp.sum(y * y, axis=1, keepdims=True) + EPS)
            dy = cscale * inv * (dout - y * (inv * inv) * jnp.sum(dout * y, axis=1, keepdims=True))
        dacc = dy * _dsilu(acc)
        dx, dws = _conv_bwd_val(xv, dacc, wv, taps, masks)
        dx_ref[...] = dx.astype(BF16)
        dw_ref[...] = jnp.concatenate(dws + [jnp.zeros((8 - len(dws), DN_HEAD_DIM), F32)], axis=0)

    base = kind * DN_HEADS
    return pl.pallas_call(
        body, name=f"dn_pre_bwd{kind}", grid=(DN_HEADS,),
        out_shape=(jax.ShapeDtypeStruct((r, dd["dnw"]), BF16), jax.ShapeDtypeStruct((8, dd["dnw"]), F32)),
        in_specs=[pl.BlockSpec((r, DN_HEAD_DIM), lambda j: (0, base + j)),
                  pl.BlockSpec((8, DN_HEAD_DIM), lambda j: (0, base + j)),
                  pl.BlockSpec((2, r, DN_HEAD_DIM), lambda j: (0, 0, j))],
        out_specs=(pl.BlockSpec((r, DN_HEAD_DIM), lambda j: (0, j)), pl.BlockSpec((8, DN_HEAD_DIM), lambda j: (0, j))),
        compiler_params=_cp(("arbitrary",)))(proj, convw, dqkv)


def _lru_conv_fwd(xl, convw):
    r, lw = xl.shape
    taps = _taps(LRU_CONV)
    offs = [o for _, o in taps if o != 0]

    def body(x_ref, w_ref, o_ref):
        masks = _seg_masks(r, LANE, _seq_bounds(), offs)
        wv = w_ref[...]
        o_ref[...] = _conv_fwd_val(x_ref[...], wv, taps, masks) + wv[4:5, :]

    return pl.pallas_call(
        body, name="lru_conv_fwd", grid=(lw // LANE,), out_shape=jax.ShapeDtypeStruct((r, lw), F32),
        in_specs=[pl.BlockSpec((r, LANE), lambda j: (0, j)), pl.BlockSpec((8, LANE), lambda j: (0, j))],
        out_specs=pl.BlockSpec((r, LANE), lambda j: (0, j)), compiler_params=_cp(("arbitrary",)))(xl, convw)


def _lru_conv_bwd(xl, convw, dxc):
    r, lw = xl.shape
    taps = _taps(LRU_CONV)
    offs = sorted({o for _, o in taps if o != 0} | {-o for _, o in taps if o != 0})

    def body(x_ref, w_ref, d_ref, dx_ref, dw_ref):
        masks = _seg_masks(r, LANE, _seq_bounds(), offs)
        dacc = d_ref[0] + d_ref[1]
        dx, dws = _conv_bwd_val(x_ref[...], dacc, w_ref[...], taps, masks)
        dx_ref[...] = dx
        db = jnp.sum(dacc, axis=0, keepdims=True)
        dw_ref[...] = jnp.concatenate(dws + [db, jnp.zeros((3, LANE), F32)], axis=0)

    return pl.pallas_call(
        body, name="lru_conv_bwd", grid=(lw // LANE,),
        out_shape=(jax.ShapeDtypeStruct((r, lw), F32), jax.ShapeDtypeStruct((8, lw), F32)),
        in_specs=[pl.BlockSpec((r, LANE), lambda j: (0, j)), pl.BlockSpec((8, LANE), lambda j: (0, j)),
                  pl.BlockSpec((2, r, LANE), lambda j: (0, 0, j))],
        out_specs=(pl.BlockSpec((r, LANE), lambda j: (0, j)), pl.BlockSpec((8, LANE), lambda j: (0, j))),
        compiler_params=_cp(("arbitrary",)))(xl, convw, dxc)


def _tri(rev_lane, n, m):
    r = lax.broadcasted_iota(jnp.int32, (n, n), 0)
    c = lax.broadcasted_iota(jnp.int32, (n, n), 1)
    return (r >= c).astype(F32), (r <= c).astype(F32)


def _hdot(a, b):
    return jnp.dot(a, b, preferred_element_type=F32, precision=lax.Precision.HIGHEST)


def _gate_lanes():
    lane = lax.broadcasted_iota(jnp.int32, (DN_CHUNK, LANE), 1)
    group = lane // 32
    rev = (lane % 32) >= DN_HEADS if DN_HEADS * 2 <= 32 else None
    return group, rev


def _dn_gates_fwd(pba, par):
    r = pba.shape[0]
    nch = r // DN_CHUNK
    hh = 2 * DN_HEADS

    def body(x_ref, p_ref, o_ref):
        lo, up = _tri(None, DN_CHUNK, DN_CHUNK)
        ones = jnp.ones((DN_CHUNK, DN_CHUNK), F32)
        lane = lax.broadcasted_iota(jnp.int32, (DN_CHUNK, LANE), 1)
        group = lane // 32
        rev = (lane % 32) >= DN_HEADS
        dtb = p_ref[0:1, :]
        nea = p_ref[1:2, :]

        def step(i, carry):
            rows = pl.ds(pl.multiple_of(i * DN_CHUNK, DN_CHUNK), DN_CHUNK)
            xv = x_ref[rows, :]
            beta = jax.nn.sigmoid(xv)
            g = jnp.where(group == 1, nea * _softplus(xv + dtb), 0.0)
            gc = jnp.where(rev, _hdot(up, g), _hdot(lo, g))
            gt = _hdot(ones, g)
            out = jnp.where(group == 0, beta, jnp.where(group == 1, g, 0.0))
            out = out + jnp.where(group == 2, pltpu.roll(gc, 32, axis=1), 0.0)
            out = out + jnp.where(group == 3, pltpu.roll(gt, 64, axis=1), 0.0)
            o_ref[rows, :] = out
            return carry

        lax.fori_loop(0, nch, step, 0)

    return pl.pallas_call(body, name="dn_gates_fwd", out_shape=jax.ShapeDtypeStruct((r, LANE), F32),
                          compiler_params=_cp(None))(pba, par)


def _dn_gates_bwd(pba, par, dsc):
    r = pba.shape[0]
    nch = r // DN_CHUNK

    def body(x_ref, p_ref, d_ref, dx_ref, dp_ref, acc_ref):
        lo, up = _tri(None, DN_CHUNK, DN_CHUNK)
        ones = jnp.ones((DN_CHUNK, DN_CHUNK), F32)
        lane = lax.broadcasted_iota(jnp.int32, (DN_CHUNK, LANE), 1)
        group = lane // 32
        rev = (lane % 32) >= DN_HEADS
        dtb = p_ref[0:1, :]
        nea = p_ref[1:2, :]
        acc_ref[...] = jnp.zeros_like(acc_ref)

        def step(i, carry):
            rows = pl.ds(pl.multiple_of(i * DN_CHUNK, DN_CHUNK), DN_CHUNK)
            xv = x_ref[rows, :]
            dv = d_ref[rows, :]
            beta = jax.nn.sigmoid(xv)
            z = xv + dtb
            sp = _softplus(z)
            dgc = jnp.where(group == 1, pltpu.roll(dv, LANE - 32, axis=1), 0.0)
            dgt = jnp.where(group == 1, pltpu.roll(dv, LANE - 64, axis=1), 0.0)
            dg = jnp.where(rev, _hdot(lo, dgc), _hdot(up, dgc)) + _hdot(ones, dgt)
            dg = dg + jnp.where(group == 1, dv, 0.0)
            dal = dg * nea * jax.nn.sigmoid(z)
            dpb = dv * beta * (1.0 - beta)
            dx_ref[rows, :] = jnp.where(group == 0, dpb, jnp.where(group == 1, dal, 0.0)).astype(BF16)
            acc_ref[0:1, :] += jnp.sum(jnp.where(group == 1, dal, 0.0), axis=0, keepdims=True)
            acc_ref[1:2, :] += jnp.sum(jnp.where(group == 1, dg * nea * sp, 0.0), axis=0, keepdims=True)
            return carry

        lax.fori_loop(0, nch, step, 0)
        dp_ref[...] = acc_ref[...]

    return pl.pallas_call(
        body, name="dn_gates_bwd",
        out_shape=(jax.ShapeDtypeStruct((r, LANE), BF16), jax.ShapeDtypeStruct((8, LANE), F32)),
        scratch_shapes=[pltpu.VMEM((8, LANE), F32)], compiler_params=_cp(None))(pba, par, dsc)


_B_NN = (((2,), (1,)), ((0,), (0,)))
_B_NT = (((2,), (2,)), ((0,), (0,)))
_B_TN = (((1,), (1,)), ((0,), (0,)))


def _bdot(a, b, dims, exact=False):
    return lax.dot_general(a, b, dims, preferred_element_type=F32,
                           precision=lax.Precision.HIGH if exact else None)


@jax.custom_vjp
def _inv_unit_tri(l):
    n = l.shape[-1]
    ri = lax.broadcasted_iota(jnp.int32, (n, n), 0)
    ci = lax.broadcasted_iota(jnp.int32, (n, n), 1)
    x = jnp.broadcast_to((ri == ci).astype(F32), l.shape)
    inner = 1
    for size in (4, 16, n):
        so, si = size.bit_length() - 1, inner.bit_length() - 1
        same_outer = (ri >> so) == (ci >> so)
        same_inner = (ri >> si) == (ci >> si)
        part = jnp.where(same_outer, jnp.where(same_inner, 0.0, l), 0.0)
        m = part if inner == 1 else _bdot(x, part, _B_NN, exact=True)
        y = x
        for _ in range(size // inner - 1):
            y = x - _bdot(m, y, _B_NN, exact=True)
        x = y
        inner = size
    return x


def _inv_fwd(l):
    a = _inv_unit_tri(l)
    return a, a


def _inv_bwd(a, da):
    t = _bdot(a, da, _B_TN, exact=True)
    return (-_bdot(t, a, _B_NT, exact=True),)


_inv_unit_tri.defvjp(_inv_fwd, _inv_bwd)


def _delta_chunk(q4, k4, v4, sc, s4, rev):
    hps = HEADS_PER_STEP
    c = DN_CHUNK
    hd = DN_HEAD_DIM
    ri = lax.broadcasted_iota(jnp.int32, (c, c), 0)
    ci = lax.broadcasted_iota(jnp.int32, (c, c), 1)
    ahead = (ri - ci) * jnp.where(rev, -1, 1)
    incl = ahead >= 0
    strict = ahead > 0
    sct = sc.T

    def heads(x):
        return jnp.concatenate([x[None, :, j * hd:(j + 1) * hd] for j in range(hps)], axis=0)

    def col(off):
        return jnp.concatenate([sc[None, :, off + j:off + j + 1] for j in range(hps)], axis=0)

    def row(off):
        return jnp.concatenate([sct[None, off + j:off + j + 1, :] for j in range(hps)], axis=0)

    q, k, v = heads(q4), heads(k4), heads(v4)
    beta_c, gc_c, gt_c = col(0), col(hps), col(2 * hps)
    beta_r, gc_r = row(0), row(hps)
    dmat = jnp.where(incl, jnp.exp(jnp.where(incl, gc_c - gc_r, 0.0)), 0.0)
    kk = _bdot(k, k, _B_NT, exact=True)
    qk = _bdot(q, k, _B_NT)
    a = _inv_unit_tri(jnp.where(strict, beta_c * kk * dmat, 0.0))
    w = _bdot(a * (beta_r * jnp.exp(gc_r)), k, _B_NN, exact=True)
    u = _bdot(a * beta_r, v, _B_NN, exact=True)
    u2 = u - _bdot(w, s4, _B_NN)
    o = jnp.exp(gc_c) * _bdot(q, s4, _B_NN) + _bdot(qk * dmat, u2, _B_NN)
    ke = k * jnp.exp(gt_c - gc_c)
    s_new = jnp.exp(gt_c[:, 0:1, :]) * s4 + _bdot(ke, u2, _B_TN)
    return jnp.concatenate([o[j] for j in range(hps)], axis=1), s_new


def _chunk_index(d, n, nctx, ntot):
    rev_idx = jnp.where(n < nctx, nctx - 1 - n, ntot - 1 - (n - nctx))
    return jnp.where(d == 0, n, rev_idx)


def _delta_fwd(qkv, scg, rider=None):
    dd = _dims()
    r, dnw = dd["r"], dd["dnw"]
    hps = HEADS_PER_STEP
    nhg = DN_HEADS // hps
    bw = hps * DN_HEAD_DIM
    nctx, ntot = CTX_LEN // DN_CHUNK, r // DN_CHUNK

    def body(q_ref, k_ref, v_ref, sc_ref, o_ref, ss_ref, s_ref):
        d, n = pl.program_id(0), pl.program_id(2)

        @pl.when(n == 0)
        def _():
            s_ref[...] = jnp.zeros_like(s_ref)

        s4 = s_ref[...]
        ss_ref[0, 0] = s4
        o4, s4n = _delta_chunk(q_ref[...], k_ref[...], v_ref[...], sc_ref[0], s4, d == 1)
        o_ref[0] = o4
        s_ref[...] = s4n

    def rows(d, g, n):
        return _chunk_index(d, n, nctx, ntot)

    return _call_with_rider(
        body, rider, name="delta_fwd", grid=(2, nhg, ntot),
        out_shape=(jax.ShapeDtypeStruct((2, r, dnw), F32),
                   jax.ShapeDtypeStruct((2, ntot, DN_HEADS, DN_HEAD_DIM, DN_HEAD_DIM), F32)),
        in_specs=[pl.BlockSpec((DN_CHUNK, bw), lambda d, g, n: (rows(d, g, n), g)),
                  pl.BlockSpec((DN_CHUNK, bw), lambda d, g, n: (rows(d, g, n), nhg + g)),
                  pl.BlockSpec((DN_CHUNK, bw), lambda d, g, n: (rows(d, g, n), 2 * nhg + g)),
                  pl.BlockSpec((1, DN_CHUNK, LANE), lambda d, g, n: (d * nhg + g, rows(d, g, n), 0))],
        out_specs=(pl.BlockSpec((1, DN_CHUNK, bw), lambda d, g, n: (d, rows(d, g, n), g)),
                   pl.BlockSpec((1, 1, hps, DN_HEAD_DIM, DN_HEAD_DIM), lambda d, g, n: (d, n, g, 0, 0))),
        scratch_shapes=[pltpu.VMEM((hps, DN_HEAD_DIM, DN_HEAD_DIM), F32)],
        compiler_params=_cp(("arbitrary", "arbitrary", "arbitrary")), operands=(qkv, qkv, qkv, scg))


def _delta_bwd(qkv, scg, states, do, rider=None):
    dd = _dims()
    r, dnw = dd["r"], dd["dnw"]
    hps = HEADS_PER_STEP
    nhg = DN_HEADS // hps
    bw = hps * DN_HEAD_DIM
    nctx, ntot = CTX_LEN // DN_CHUNK, r // DN_CHUNK

    def body(q_ref, k_ref, v_ref, sc_ref, ss_ref, do_ref, dq_ref, dk_ref, dv_ref, dsc_ref, ds_ref):
        d, n2 = pl.program_id(0), pl.program_id(2)

        @pl.when(n2 == 0)
        def _():
            ds_ref[...] = jnp.zeros_like(ds_ref)

        fn = functools.partial(_delta_chunk, rev=(d == 1))
        _, vjp = jax.vjp(fn, q_ref[...], k_ref[...], v_ref[...], sc_ref[0], ss_ref[0, 0])
        dq, dk, dv, dsc, ds = vjp((do_ref[...], ds_ref[...]))
        dq_ref[0] = dq
        dk_ref[0] = dk
        dv_ref[0] = dv
        dsc_ref[0] = dsc
        ds_ref[...] = ds

    def rows(d, g, n2):
        return _chunk_index(d, ntot - 1 - n2, nctx, ntot)

    blk = lambda col: pl.BlockSpec((DN_CHUNK, bw), lambda d, g, n2: (rows(d, g, n2), col(g)))
    oblk = pl.BlockSpec((1, DN_CHUNK, bw), lambda d, g, n2: (d, rows(d, g, n2), g))
    scblk = pl.BlockSpec((1, DN_CHUNK, LANE), lambda d, g, n2: (d * nhg + g, rows(d, g, n2), 0))
    big = jax.ShapeDtypeStruct((2, r, dnw), F32)
    return _call_with_rider(
        body, rider, name="delta_bwd", grid=(2, nhg, ntot),
        out_shape=(big, big, big, jax.ShapeDtypeStruct(scg.shape, F32)),
        in_specs=[blk(lambda g: g), blk(lambda g: nhg + g), blk(lambda g: 2 * nhg + g), scblk,
                  pl.BlockSpec((1, 1, hps, DN_HEAD_DIM, DN_HEAD_DIM), lambda d, g, n2: (d, ntot - 1 - n2, g, 0, 0)),
                  blk(lambda g: g)],
        out_specs=(oblk, oblk, oblk, scblk),
        scratch_shapes=[pltpu.VMEM((hps, DN_HEAD_DIM, DN_HEAD_DIM), F32)],
        compiler_params=_cp(("arbitrary", "arbitrary", "arbitrary")), operands=(qkv, qkv, qkv, scg, states, do))


def _lru_gate_vals(xn, wr, wi, br, bi, lam):
    r = jax.nn.sigmoid(jnp.dot(xn, wr, preferred_element_type=F32) + br)
    i = jax.nn.sigmoid(jnp.dot(xn, wi, preferred_element_type=F32) + bi)
    sp = _softplus(-lam)
    la = -LRU_C * r * sp
    a = jnp.exp(la)
    mm = jnp.sqrt(_neg_expm1(2.0 * la))
    return r, i, sp, la, a, mm


def _lru_gates_fwd(xc, w_r, w_i, vecs):
    r, lw = xc.shape
    bd = lw // LRU_BLOCKS
    tm = _div(r, ROW_TILE, 8)

    def body(x_ref, wr_ref, wi_ref, v_ref, a_ref, b_ref):
        for n in range(LRU_BLOCKS):
            sl = slice(n * bd, (n + 1) * bd)
            xn = x_ref[:, sl]
            _, i, _, _, a, mm = _lru_gate_vals(xn, wr_ref[0, n], wi_ref[0, n], v_ref[0, 0:1, sl], v_ref[0, 1:2, sl],
                                               v_ref[0, 2:3, sl])
            a_ref[0, :, sl] = a
            b_ref[0, :, sl] = mm * (i * xn)

    wspec = pl.BlockSpec((1, LRU_BLOCKS, bd, bd), lambda d, i: (d, 0, 0, 0))
    out = jax.ShapeDtypeStruct((2, r, lw), F32)
    ospec = pl.BlockSpec((1, tm, lw), lambda d, i: (d, i, 0))
    return pl.pallas_call(
        body, name="lru_gates_fwd", grid=(2, r // tm), out_shape=(out, out),
        in_specs=[pl.BlockSpec((tm, lw), lambda d, i: (i, 0)), wspec, wspec,
                  pl.BlockSpec((1, 8, lw), lambda d, i: (d, 0, 0))],
        out_specs=(ospec, ospec), compiler_params=_cp(("arbitrary", "arbitrary")))(xc, w_r, w_i, vecs)


def _lru_gates_bwd(xc, w_r, w_i, vecs, da, db):
    r, lw = xc.shape
    bd = lw // LRU_BLOCKS
    tm = _div(r, ROW_TILE, 8)

    def body(x_ref, wr_ref, wi_ref, v_ref, da_ref, db_ref, dx_ref, dwr_ref, dwi_ref, dv_ref):
        @pl.when(pl.program_id(1) == 0)
        def _():
            dwr_ref[...] = jnp.zeros_like(dwr_ref)
            dwi_ref[...] = jnp.zeros_like(dwi_ref)
            dv_ref[...] = jnp.zeros_like(dv_ref)

        for n in range(LRU_BLOCKS):
            sl = slice(n * bd, (n + 1) * bd)
            xn = x_ref[:, sl]
            wr, wi = wr_ref[0, n], wi_ref[0, n]
            lam = v_ref[0, 2:3, sl]
            rr, ii, sp, la, a, mm = _lru_gate_vals(xn, wr, wi, v_ref[0, 0:1, sl], v_ref[0, 1:2, sl], lam)
            dav, dbv = da_ref[0, :, sl], db_ref[0, :, sl]
            dmm = dbv * (ii * xn)
            dla = dav * a - dmm * (a * a) / jnp.maximum(mm, 1e-30)
            di = dbv * mm * xn
            dxn = dbv * mm * ii
            dr = dla * (-LRU_C * sp)
            dsp = jnp.sum(dla * (-LRU_C * rr), axis=0, keepdims=True)
            dpr = dr * rr * (1.0 - rr)
            dpi = di * ii * (1.0 - ii)
            dxn = dxn + lax.dot_general(dpr, wr, (((1,), (1,)), ((), ())), preferred_element_type=F32)
            dxn = dxn + lax.dot_general(dpi, wi, (((1,), (1,)), ((), ())), preferred_element_type=F32)
            dx_ref[0, :, sl] = dxn
            dwr_ref[0, n] += lax.dot_general(xn, dpr, (((0,), (0,)), ((), ())), preferred_element_type=F32)
            dwi_ref[0, n] += lax.dot_general(xn, dpi, (((0,), (0,)), ((), ())), preferred_element_type=F32)
            dv_ref[0, 0:1, sl] += jnp.sum(dpr, axis=0, keepdims=True)
            dv_ref[0, 1:2, sl] += jnp.sum(dpi, axis=0, keepdims=True)
            dv_ref[0, 2:3, sl] += dsp * (-jax.nn.sigmoid(-lam))

    wspec = pl.BlockSpec((1, LRU_BLOCKS, bd, bd), lambda d, i: (d, 0, 0, 0))
    vspec = pl.BlockSpec((1, 8, lw), lambda d, i: (d, 0, 0))
    big = pl.BlockSpec((1, tm, lw), lambda d, i: (d, i, 0))
    return pl.pallas_call(
        body, name="lru_gates_bwd", grid=(2, r // tm),
        out_shape=(jax.ShapeDtypeStruct((2, r, lw), F32), jax.ShapeDtypeStruct(w_r.shape, F32),
                   jax.ShapeDtypeStruct(w_i.shape, F32), jax.ShapeDtypeStruct((2, 8, lw), F32)),
        in_specs=[pl.BlockSpec((tm, lw), lambda d, i: (i, 0)), wspec, wspec, vspec, big, big],
        out_specs=(big, wspec, wspec, vspec), compiler_params=_cp(("arbitrary", "arbitrary")))(xc, w_r, w_i, vecs, da, db)


def _scan_tiles(rev):
    nctx, nlat = CTX_LEN // 8, SEQ // 8
    return nctx, nlat


def _tile_of(step, rev, nctx, ntot):
    if not rev:
        return step
    return jnp.where(step < nctx, nctx - 1 - step, ntot - 1 - (step - nctx))


def _lru_scan_fwd(a, b, d):
    _, r, lw = a.shape
    cw = _div(lw, 256, LANE)
    nctx, ntot = CTX_LEN // 8, r // 8
    rev = d == 1
    order = list(range(7, -1, -1)) if rev else list(range(8))

    def body(a_ref, b_ref, h_ref):
        def step(s, carry):
            t0 = pl.multiple_of(_tile_of(s, rev, nctx, ntot) * 8, 8)
            at = a_ref[0, pl.ds(t0, 8), :]
            bt = b_ref[0, pl.ds(t0, 8), :]
            rows = [None] * 8
            h = carry
            for i in order:
                h = at[i:i + 1, :] * h + bt[i:i + 1, :]
                rows[i] = h
            h_ref[pl.ds(t0, 8), :] = jnp.concatenate(rows, axis=0)
            return h

        lax.fori_loop(0, ntot, step, jnp.zeros((1, cw), F32))

    return pl.pallas_call(
        body, name=f"lru_scan_fwd{d}", grid=(lw // cw,), out_shape=jax.ShapeDtypeStruct((r, lw), F32),
        in_specs=[pl.BlockSpec((1, r, cw), lambda j: (d, 0, j)), pl.BlockSpec((1, r, cw), lambda j: (d, 0, j))],
        out_specs=pl.BlockSpec((r, cw), lambda j: (0, j)), compiler_params=_cp(("arbitrary",)))(a, b)


def _lru_scan_bwd(a, h, dh, d):
    _, r, lw = a.shape
    cw = _div(lw, 256, LANE)
    nctx, ntot = CTX_LEN // 8, r // 8
    rev = d == 1
    order = list(range(8)) if rev else list(range(7, -1, -1))

    def body(a_ref, h_ref, dh_ref, da_ref, db_ref):
        def step(s2, carry):
            s = ntot - 1 - s2
            t0 = pl.multiple_of(_tile_of(s, rev, nctx, ntot) * 8, 8)
            sp = jnp.maximum(s - 1, 0)
            tp = pl.multiple_of(_tile_of(sp, rev, nctx, ntot) * 8, 8)
            at = a_ref[0, pl.ds(t0, 8), :]
            ht = h_ref[pl.ds(t0, 8), :]
            hp = h_ref[pl.ds(tp, 8), :]
            dht = dh_ref[pl.ds(t0, 8), :]
            last = hp[0:1, :] if rev else hp[7:8, :]
            hprev_tile = jnp.where(s > 0, last, 0.0)
            das, dbs = [None] * 8, [None] * 8
            c = carry
            for i in order:
                lam = dht[i:i + 1, :] + c
                before = i + 1 if rev else i - 1
                hb = ht[before:before + 1, :] if 0 <= before < 8 else hprev_tile
                das[i] = lam * hb
                dbs[i] = lam
                c = at[i:i + 1, :] * lam
            da_ref[pl.ds(t0, 8), :] = jnp.concatenate(das, axis=0)
            db_ref[pl.ds(t0, 8), :] = jnp.concatenate(dbs, axis=0)
            return c

        lax.fori_loop(0, ntot, step, jnp.zeros((1, cw), F32))

    strip = pl.BlockSpec((r, cw), lambda j: (0, j))
    out = jax.ShapeDtypeStruct((r, lw), F32)
    return pl.pallas_call(
        body, name=f"lru_scan_bwd{d}", grid=(lw // cw,), out_shape=(out, out),
        in_specs=[pl.BlockSpec((1, r, cw), lambda j: (d, 0, j)), strip, strip],
        out_specs=(strip, strip), compiler_params=_cp(("arbitrary",)))(a, h, dh)


def _mix_post_fwd(o, hsum, proj, dnw_vec):
    dd = _dims()
    dnw, lw, nmain = dd["dnw"], dd["lw"], dd["nmain"]
    tm = _div(SEQ, ROW_TILE, 16)
    coff = CTX_LEN // tm
    gcol = (3 * dnw + lw) // dnw
    lcol = (3 * dnw + lw + dnw) // lw
    hd = DN_HEAD_DIM

    def body(o_ref, h_ref, g_ref, lg_ref, w_ref, y_ref):
        wv = w_ref[...]
        for hh in range(DN_HEADS):
            sl = slice(hh * hd, (hh + 1) * hd)
            os_ = o_ref[0, :, sl] + o_ref[1, :, sl]
            rr = lax.rsqrt(jnp.mean(os_ * os_, axis=1, keepdims=True) + EPS)
            y_ref[:, sl] = (os_ * rr * wv * _silu(g_ref[:, sl])).astype(BF16)
        y_ref[:, dnw:] = (h_ref[...] * _silu(lg_ref[...])).astype(BF16)

    return pl.pallas_call(
        body, name="mix_post_fwd", grid=(SEQ // tm,), out_shape=jax.ShapeDtypeStruct((SEQ, dnw + lw), BF16),
        in_specs=[pl.BlockSpec((2, tm, dnw), lambda i: (0, i + coff, 0)), pl.BlockSpec((tm, lw), lambda i: (i, 0)),
                  pl.BlockSpec((tm, dnw), lambda i: (i + coff, gcol)), pl.BlockSpec((tm, lw), lambda i: (i + coff, lcol)),
                  pl.BlockSpec((1, hd), lambda i: (0, 0))],
        out_specs=pl.BlockSpec((tm, dnw + lw), lambda i: (i, 0)), compiler_params=_cp(("arbitrary",)))(
            o, hsum, proj, proj, dnw_vec)


def _mix_post_bwd(o, hsum, proj, dnw_vec, dy):
    dd = _dims()
    dnw, lw, r = dd["dnw"], dd["lw"], dd["r"]
    tm = _div(SEQ, ROW_TILE, 16)
    coff = CTX_LEN // tm
    gcol = (3 * dnw + lw) // dnw
    lcol = (3 * dnw + lw + dnw) // lw
    hd = DN_HEAD_DIM

    def body(o_ref, h_ref, g_ref, lg_ref, w_ref, dy_ref, do_ref, dh_ref, dg_ref, dw_ref):
        i = pl.program_id(0)

        @pl.when(i == 0)
        def _():
            dw_ref[...] = jnp.zeros_like(dw_ref)

        @pl.when(i < coff)
        def _():
            do_ref[...] = jnp.zeros_like(do_ref)
            dg_ref[...] = jnp.zeros_like(dg_ref)

        @pl.when(i >= coff)
        def _():
            wv = w_ref[...]
            dwacc = jnp.zeros((1, hd), F32)
            for hh in range(DN_HEADS):
                sl = slice(hh * hd, (hh + 1) * hd)
                os_ = o_ref[0, :, sl] + o_ref[1, :, sl]
                rr = lax.rsqrt(jnp.mean(os_ * os_, axis=1, keepdims=True) + EPS)
                nn = os_ * rr
                gv = g_ref[:, sl]
                dyv = dy_ref[:, sl]
                don = dyv * _silu(gv)
                dg_ref[:, sl] = (dyv * nn * wv * _dsilu(gv)).astype(BF16)
                dwacc = dwacc + jnp.sum(don * nn, axis=0, keepdims=True)
                dn_ = don * wv
                do_ref[:, sl] = rr * (dn_ - nn * jnp.mean(nn * dn_, axis=1, keepdims=True))
            dw_ref[...] += dwacc
            lgv = lg_ref[...]
            dyl = dy_ref[:, dnw:]
            dh_ref[...] = dyl * _silu(lgv)
            dg_ref[:, dnw:] = (dyl * h_ref[...] * _dsilu(lgv)).astype(BF16)

    lat = lambda i: jnp.maximum(i - coff, 0)
    return pl.pallas_call(
        body, name="mix_post_bwd", grid=(r // tm,),
        out_shape=(jax.ShapeDtypeStruct((r, dnw), F32), jax.ShapeDtypeStruct((SEQ, lw), F32),
                   jax.ShapeDtypeStruct((r, dnw + lw), BF16), jax.ShapeDtypeStruct((1, hd), F32)),
        in_specs=[pl.BlockSpec((2, tm, dnw), lambda i: (0, i, 0)), pl.BlockSpec((tm, lw), lambda i: (lat(i), 0)),
                  pl.BlockSpec((tm, dnw), lambda i: (i, gcol)), pl.BlockSpec((tm, lw), lambda i: (i, lcol)),
                  pl.BlockSpec((1, hd), lambda i: (0, 0)), pl.BlockSpec((tm, dnw + lw), lambda i: (lat(i), 0))],
        out_specs=(pl.BlockSpec((tm, dnw), lambda i: (i, 0)), pl.BlockSpec((tm, lw), lambda i: (lat(i), 0)),
                   pl.BlockSpec((tm, dnw + lw), lambda i: (i, 0)), pl.BlockSpec((1, hd), lambda i: (0, 0))),
        compiler_params=_cp(("arbitrary",)))(o, hsum, proj, proj, dnw_vec, dy)


def _sc_masks(tm, cw):
    t = lax.broadcasted_iota(jnp.int32, (tm, cw), 0) % GRID_W
    return {-1: t != 0, 1: t != GRID_W - 1}


def _sc_mix_fwd(p, convw):
    s, w4 = p.shape
    w = w4 // 4
    tm = _div(s, 512, GRID_W if GRID_W % 16 == 0 else 16 * GRID_W)
    cw = _div(w, 512, LANE)
    nc = w // cw
    taps = _taps(SC_CONV)

    def body(b_ref, c_ref, x_ref, g_ref, w_ref, y_ref):
        masks = _sc_masks(tm, cw)
        u = c_ref[...] * x_ref[...]
        z = _conv_fwd_val(u, w_ref[...], taps, masks)
        y_ref[...] = (b_ref[...] * z * _silu(g_ref[...])).astype(BF16)

    blk = lambda k: pl.BlockSpec((tm, cw), lambda c, i: (i, k * nc + c))
    return pl.pallas_call(
        body, name="sc_mix_fwd", grid=(nc, s // tm), out_shape=jax.ShapeDtypeStruct((s, w), BF16),
        in_specs=[blk(0), blk(1), blk(2), blk(3), pl.BlockSpec((8, cw), lambda c, i: (0, c))],
        out_specs=pl.BlockSpec((tm, cw), lambda c, i: (i, c)),
        compiler_params=_cp(("arbitrary", "arbitrary")))(p, p, p, p, convw)


def _sc_mix_bwd(p, convw, dy):
    s, w4 = p.shape
    w = w4 // 4
    tm = _div(s, 512, GRID_W if GRID_W % 16 == 0 else 16 * GRID_W)
    cw = _div(w, 512, LANE)
    nc = w // cw
    taps = _taps(SC_CONV)

    def body(b_ref, c_ref, x_ref, g_ref, w_ref, dy_ref, db_ref, dc_ref, dx_ref, dg_ref, dw_ref):
        @pl.when(pl.program_id(1) == 0)
        def _():
            dw_ref[...] = jnp.zeros_like(dw_ref)

        masks = _sc_masks(tm, cw)
        wv = w_ref[...]
        bv, cv, xv, gv, dyv = b_ref[...], c_ref[...], x_ref[...], g_ref[...], dy_ref[...]
        u = cv * xv
        z = _conv_fwd_val(u, wv, taps, masks)
        sg = _silu(gv)
        db_ref[...] = (dyv * z * sg).astype(BF16)
        dg_ref[...] = (dyv * bv * z * _dsilu(gv)).astype(BF16)
        dz = dyv * bv * sg
        du, dws = _conv_bwd_val(u, dz, wv, taps, masks)
        dc_ref[...] = (du * xv).astype(BF16)
        dx_ref[...] = (du * cv).astype(BF16)
        dw_ref[...] += jnp.concatenate(dws + [jnp.zeros((8 - len(dws), cw), F32)], axis=0)

    blk = lambda k: pl.BlockSpec((tm, cw), lambda c, i: (i, k * nc + c))
    oblk = pl.BlockSpec((tm, cw), lambda c, i: (i, c))
    wblk = pl.BlockSpec((8, cw), lambda c, i: (0, c))
    out = jax.ShapeDtypeStruct((s, w), BF16)
    return pl.pallas_call(
        body, name="sc_mix_bwd", grid=(nc, s // tm),
        out_shape=(out, out, out, out, jax.ShapeDtypeStruct((8, w), F32)),
        in_specs=[blk(0), blk(1), blk(2), blk(3), wblk, oblk],
        out_specs=(oblk, oblk, oblk, oblk, wblk),
        compiler_params=_cp(("arbitrary", "arbitrary")))(p, p, p, p, convw, dy)


def _adamw_slots(w, slots, m, v, name):
    rows, cols = w.shape
    nslot = slots.shape[0]
    tm = _div(rows, max(16, (64 * 4096) // cols), 16)

    def body(w_ref, s_ref, m_ref, v_ref, g_ref, dl_ref, nm_ref, nv_ref):
        g = s_ref[0].astype(F32)
        for j in range(1, nslot):
            g = g + s_ref[j].astype(F32)
        dl, m2, v2 = _adam(w_ref[...], g, m_ref[...], v_ref[...])
        g_ref[...] = g
        dl_ref[...] = dl
        nm_ref[...] = m2
        nv_ref[...] = v2

    blk = pl.BlockSpec((tm, cols), lambda i: (i, 0))
    out = jax.ShapeDtypeStruct((rows, cols), F32)
    return pl.pallas_call(
        body, name=name, grid=(rows // tm,), out_shape=(out, out, out, out),
        in_specs=[blk, pl.BlockSpec((nslot, tm, cols), lambda i: (0, i, 0)), blk, blk],
        out_specs=(blk, blk, blk, blk), compiler_params=_cp(("arbitrary",)))(w, slots, m, v)


def _adamw_flat(w, g, m, v, name):
    rows, cols = w.shape
    tm = _div(rows, 1024, 8)

    def body(w_ref, g_ref, m_ref, v_ref, dl_ref, nm_ref, nv_ref):
        dl, m2, v2 = _adam(w_ref[...], g_ref[...], m_ref[...], v_ref[...])
        dl_ref[...] = dl
        nm_ref[...] = m2
        nv_ref[...] = v2

    blk = pl.BlockSpec((tm, cols), lambda i: (i, 0))
    out = jax.ShapeDtypeStruct((rows, cols), F32)
    return pl.pallas_call(
        body, name=name, grid=(rows // tm,), out_shape=(out, out, out), in_specs=[blk] * 4, out_specs=(blk, blk, blk),
        compiler_params=_cp(("arbitrary",)))(w, g, m, v)


def _pack(arrs):
    flat = [a.reshape(-1).astype(F32) for a in arrs]
    sizes = [f.shape[0] for f in flat]
    total = sum(sizes)
    padded = -(-total // 1024) * 1024
    if padded > total:
        flat.append(jnp.zeros((padded - total,), F32))
    offs, o = [], 0
    for s_ in sizes:
        offs.append(o)
        o += s_
    return jnp.concatenate(flat).reshape(padded // LANE, LANE), offs


def _unpack(flat2d, offs, shapes):
    flat = flat2d.reshape(-1)
    out = []
    for o, shp in zip(offs, shapes):
        n = 1
        for s_ in shp:
            n *= s_
        out.append(flat[o:o + n].reshape(shp))
    return out


def _pad_rows(a, rows):
    return jnp.concatenate([a, jnp.zeros((rows - a.shape[0],) + a.shape[1:], a.dtype)], axis=0)


def _to_col_major(t):
    n, ch = t.shape
    return t.reshape(n // GRID_W, GRID_W, ch).swapaxes(0, 1).reshape(n, ch)


def _to_raster(t):
    n, ch = t.shape
    return t.reshape(GRID_W, n // GRID_W, ch).swapaxes(0, 1).reshape(n, ch)


def _scg_layout(sc):
    r = sc.shape[0]
    hps = HEADS_PER_STEP
    nhg = DN_HEADS // hps
    t = sc.reshape(r, 4, 32)[:, :, :2 * DN_HEADS].reshape(r, 4, 2, nhg, hps)
    t = jnp.stack([t[:, 0], t[:, 2], t[:, 3]], axis=1)
    t = t.transpose(2, 3, 0, 1, 4).reshape(2 * nhg, r, 3 * hps)
    return jnp.concatenate([t, jnp.zeros((2 * nhg, r, LANE - 3 * hps), F32)], axis=2)


def _scg_layout_bwd(dscg):
    r = dscg.shape[1]
    hps = HEADS_PER_STEP
    nhg = DN_HEADS // hps
    t = dscg[:, :, :3 * hps].reshape(2, nhg, r, 3, hps).transpose(2, 3, 0, 1, 4).reshape(r, 3, 2 * DN_HEADS)
    t = jnp.concatenate([t, jnp.zeros((r, 3, 32 - 2 * DN_HEADS), F32)], axis=2) if 2 * DN_HEADS < 32 else t
    z = jnp.zeros((r, 1, 32), F32)
    return jnp.concatenate([t[:, 0:1], z, t[:, 1:2], t[:, 2:3]], axis=1).reshape(r, LANE)


def _gather_cols(full8, width):
    return full8.transpose(1, 0, 2).reshape(full8.shape[1], NDEV * width)


def kernel(x, c, ctx, c_ctx, mod_w, mod_b, norm_w, ab_w_in, ab_qkv_conv, ab_a_log, ab_dt_bias, ab_dn_norm, ab_lru_conv_w, ab_lru_conv_b, ab_lru_w_r, ab_lru_b_r, ab_lru_w_i, ab_lru_b_i, ab_lru_lambda, ab_w_out, sc_w_in, sc_conv, sc_w_out, final_norm_w, loss_target, m_c_ctx, m_mod_w, m_mod_b, m_norm_w, m_ab_w_in, m_ab_qkv_conv, m_ab_a_log, m_ab_dt_bias, m_ab_dn_norm, m_ab_lru_conv_w, m_ab_lru_conv_b, m_ab_lru_w_r, m_ab_lru_b_r, m_ab_lru_w_i, m_ab_lru_b_i, m_ab_lru_lambda, m_ab_w_out, m_sc_w_in, m_sc_conv, m_sc_w_out, m_final_norm_w, v_c_ctx, v_mod_w, v_mod_b, v_norm_w, v_ab_w_in, v_ab_qkv_conv, v_ab_a_log, v_ab_dt_bias, v_ab_dn_norm, v_ab_lru_conv_w, v_ab_lru_conv_b, v_ab_lru_w_r, v_ab_lru_b_r, v_ab_lru_w_i, v_ab_lru_b_i, v_ab_lru_lambda, v_ab_w_out, v_sc_w_in, v_sc_conv, v_sc_w_out, v_final_norm_w):
    dd = _dims()
    d, dnw, lw, r = dd["d"], dd["dnw"], dd["lw"], dd["r"]
    off_lru, off_beta, ab_state, ab_in, nmain = dd["off_lru"], dd["off_beta"], dd["ab_state"], dd["ab_in"], dd["nmain"]
    hh2 = 2 * DN_HEADS
    me = _me()
    x = x[0]
    ctx = ctx[0]
    tgt = loss_target[0]

    small_sharded = [ab_qkv_conv[0], ab_lru_conv_w[0], ab_lru_b_r[0], ab_lru_b_i[0], ab_lru_lambda[0], sc_conv[0]]
    pk, offs0 = _pack([c] + small_sharded)
    (g0,) = _exchange([pk], "gather_small_in", scatter=False)
    parts = [_unpack(g0[j], offs0, [c.shape] + [a.shape for a in small_sharded]) for j in range(NDEV)]
    c_all = jnp.concatenate([p[0] for p in parts], axis=0)
    qkv_conv, lru_conv_w, lru_b_r, lru_b_i, lru_lam, sc_conv_f = [
        jnp.concatenate([p[i] for p in parts], axis=-1) for i in range(1, 7)]

    mcols = mod_w.shape[2]
    c16 = jnp.concatenate([c_all, c_ctx[None, :], jnp.zeros((7, d), F32)], axis=0)
    bias = lax.dynamic_slice_in_dim(mod_b, me * mcols, mcols, axis=1)[:, None, :]
    mod_loc = _mod_fwd(c16, mod_w, bias)
    (mod_g,) = _exchange([mod_loc], "gather_mod", scatter=False)
    mod_all = mod_g.transpose(1, 2, 0, 3).reshape(2, 16, NDEV * mcols)
    mine = lax.dynamic_slice_in_dim(mod_all, me, 1, axis=1)[:, 0]
    shift = [mine[l, None, 0:d] for l in range(2)]
    scale = [mine[l, None, d:2 * d] for l in range(2)]
    gate = [mine[l, None, 2 * d:3 * d] for l in range(2)]
    shift_c, scale_c = mod_all[0, 8, None, 0:d], mod_all[0, 8, None, d:2 * d]
    nw0, nw1 = norm_w[0][None, :], norm_w[1][None, :]

    g_abin, g_about = _gather_two_level([ab_w_in[0].astype(BF16), ab_w_out[0].astype(BF16)], "gather_weights")
    sc_src = [sc_w_in[0].astype(BF16), sc_w_out[0].astype(BF16)]
    w_full = _gather_cols(g_abin, ab_in // NDEV)
    w_main = jnp.concatenate([w_full[:, :off_beta], w_full[:, ab_state:]], axis=1)
    w_ba = jnp.concatenate([w_full[:, off_beta:ab_state], jnp.zeros((d, LANE - 2 * hh2), BF16)], axis=1)
    w_about = g_about.reshape(dnw + lw, d)

    hn_c, rinv_c = _norm_fwd(ctx, nw0, shift_c, scale_c, "norm0_ctx")
    hn_x, rinv_x = _norm_fwd(x, nw0, shift[0], scale[0], "norm0_x")
    hn0 = jnp.concatenate([hn_c, hn_x], axis=0)
    proj = _mm_nn(hn0, w_main[None], F32, "proj_main")
    pba = _mm_nn(hn0, w_ba[None], F32, "proj_ba")

    convw8 = _pad_rows(qkv_conv, 8)
    qkv = _dn_pre_fwd(proj, convw8)
    zpad = jnp.zeros((32 - hh2,), F32)
    par = jnp.stack([jnp.concatenate([jnp.zeros((32,), F32), ab_dt_bias[0].reshape(-1), zpad, jnp.zeros((64,), F32)]),
                     jnp.concatenate([jnp.zeros((32,), F32), -jnp.exp(ab_a_log[0].reshape(-1)), zpad,
                                      jnp.zeros((64,), F32)])] + [jnp.zeros((LANE,), F32)] * 6)
    if hh2 < 32:
        pba_l = jnp.concatenate([pba[:, :hh2], jnp.zeros((r, 32 - hh2), F32), pba[:, hh2:2 * hh2],
                                 jnp.zeros((r, LANE - 32 - hh2), F32)], axis=1)
    else:
        pba_l = pba
    sc = _dn_gates_fwd(pba_l, par)
    scg = _scg_layout(sc)
    (o_dn, states), (g_scin, g_scout) = _delta_fwd(qkv, scg, rider=("gather", sc_src))
    w_scout = g_scout.reshape(d, d)

    xl = proj[:, off_lru:off_beta]
    xl_cat = jnp.concatenate([xl[:CTX_LEN], _to_col_major(xl[CTX_LEN:])], axis=0)
    lconv8 = jnp.concatenate([lru_conv_w, ab_lru_conv_b[0][None, :], jnp.zeros((3, lw), F32)], axis=0)
    xc = _lru_conv_fwd(xl_cat, lconv8)
    lvecs = jnp.stack([lru_b_r, lru_b_i, lru_lam] + [jnp.zeros_like(lru_lam)] * 5, axis=1)
    w_r, w_i = ab_lru_w_r[0], ab_lru_w_i[0]
    la_a, la_b = _lru_gates_fwd(xc, w_r, w_i, lvecs)
    h_dir = [_lru_scan_fwd(la_a, la_b, 0), _lru_scan_fwd(la_a, la_b, 1)]
    hsum = _to_raster(h_dir[0][CTX_LEN:] + h_dir[1][CTX_LEN:])

    dnn = ab_dn_norm[0][None, :]
    y0 = _mix_post_fwd(o_dn, hsum, proj, dnn)
    out0 = _mm_nn(y0, w_about[None], F32, "out0")

    x1, hn1, rinv1 = _resid_norm_fwd(x, out0, gate[0], nw1, shift[1], scale[1])
    p1 = _mm_nn(hn1, g_scin, F32, "proj_sc")
    scw8 = _pad_rows(sc_conv_f, 8)
    y1 = _sc_mix_fwd(p1, scw8)
    out1 = _mm_nn(y1, w_scout[None], F32, "out1")

    fnw = final_norm_w[None, :]
    dx2, dout1, loss11, dfnw, dgate1 = _loss_head(x1, out1, gate[1], fnw, tgt)
    dy1 = _mm_nt(dout1, w_scout[None], F32, "dy1")
    gw_scout = _mm_tn(y1, dout1, BF16, "gw_scout")[0].reshape(NDEV, d // NDEV, d)
    dpb, dpc, dpx, dpg, dscw8 = _sc_mix_bwd(p1, scw8, dy1)
    dp1 = jnp.concatenate([dpb, dpc, dpx, dpg], axis=1)
    dhn1 = _mm_nt(dp1, g_scin, F32, "dhn1", tk=2048)
    gw_scin = _mm_tn(hn1, dp1, BF16, "gw_scin", oj=NDEV)
    dx1, dout0, dshift1, dscale1, dnw1, dgate0 = _norm_bwd(
        x1, rinv1, [dhn1], nw1, scale[1], "norm1_bwd", dx_in=dx2, resid=(out0, gate[0]))

    dy0 = _mm_nt(dout0, w_about[None], F32, "dy0")
    gw_about = _mm_tn(y0, dout0, BF16, "gw_about")[0].reshape(NDEV, (dnw + lw) // NDEV, d)
    do, dh_r, dgates, ddnn = _mix_post_bwd(o_dn, hsum, proj, dnn, dy0)

    early = [gw_about, gw_scin, gw_scout]
    sib_e = _sibling_swap(early, "swap_grads_early")
    parts_e = [_chip_partial(o_, s_, f"chip_partial_early{i}") for i, (o_, s_) in enumerate(zip(early, sib_e))]
    (dq, dk, dv, dscg), (s_about, s_scin, s_scout) = _delta_bwd(qkv, scg, states, do, rider=("chipx", parts_e))
    dsc = _scg_layout_bwd(dscg)
    dpba, dpar = _dn_gates_bwd(pba_l, par, dsc)
    dpq, dcw_q = _dn_pre_bwd(proj, convw8, dq, 0)
    dpk, dcw_k = _dn_pre_bwd(proj, convw8, dk, 1)
    dpv, dcw_v = _dn_pre_bwd(proj, convw8, dv, 2)

    dh_cat = jnp.concatenate([jnp.zeros((CTX_LEN, lw), F32), _to_col_major(dh_r)], axis=0)
    da0, db0 = _lru_scan_bwd(la_a, h_dir[0], dh_cat, 0)
    da1, db1 = _lru_scan_bwd(la_a, h_dir[1], dh_cat, 1)
    dxc, dw_r, dw_i, dlvecs = _lru_gates_bwd(xc, w_r, w_i, lvecs, jnp.stack([da0, da1]), jnp.stack([db0, db1]))
    dxl_cat, dlconv8 = _lru_conv_bwd(xl_cat, lconv8, dxc)
    dxl = jnp.concatenate([dxl_cat[:CTX_LEN], _to_raster(dxl_cat[CTX_LEN:])], axis=0).astype(BF16)

    dproj = jnp.concatenate([dpq, dpk, dpv, dxl, dgates], axis=1)
    if hh2 < 32:
        dpba_w = jnp.concatenate([dpba[:, :hh2], dpba[:, 32:32 + hh2], jnp.zeros((r, LANE - 2 * hh2), BF16)], axis=1)
    else:
        dpba_w = dpba
    gw_main = _mm_tn(hn0, dproj, BF16, "gw_main")[0]
    gw_ba = _mm_tn(hn0, dpba_w, BF16, "gw_ba")[0]
    gw_full = jnp.concatenate([gw_main[:, :off_beta], gw_ba[:, :2 * hh2], gw_main[:, off_beta:]], axis=1)
    gw_abin = gw_full.reshape(d, NDEV, ab_in // NDEV).transpose(1, 0, 2)
    (sib_l,) = _sibling_swap([gw_abin], "swap_grads_late")
    part_l = _chip_partial(gw_abin, sib_l, "chip_partial_late")
    dhn0_a, (s_abin,) = _mm_nt(dproj, w_main[None], F32, "dhn0_main", rider=("chipx", [part_l]))
    dhn0_b = _mm_nt(dpba_w, w_ba[None], F32, "dhn0_ba")

    nt = _div(SEQ, ROW_TILE // 2, 16)
    grad_x, dshift0, dscale0, dnw0_x = _norm_bwd(x, rinv_x, [dhn0_a, dhn0_b], nw0, scale[0], "norm0_bwd_x",
                                                 row_off=CTX_LEN // nt, dx_in=dx1)
    _, dshift_c, dscale_c, dnw0_c = _norm_bwd(ctx, rinv_c, [dhn0_a, dhn0_b], nw0, scale_c, "norm0_bwd_ctx")

    big = {}
    big["ab_w_in"] = _adamw_slots(ab_w_in[0], s_abin, m_ab_w_in[0], v_ab_w_in[0], "adamw_ab_w_in")
    big["ab_w_out"] = _adamw_slots(ab_w_out[0], s_about, m_ab_w_out[0], v_ab_w_out[0], "adamw_ab_w_out")
    big["sc_w_in"] = _adamw_slots(sc_w_in[0], s_scin, m_sc_w_in[0], v_sc_w_in[0], "adamw_sc_w_in")
    big["sc_w_out"] = _adamw_slots(sc_w_out[0], s_scout, m_sc_w_out[0], v_sc_w_out[0], "adamw_sc_w_out")

    dmod_own = jnp.stack([jnp.concatenate([dshift0, dscale0, dgate0], axis=1)[0],
                          jnp.concatenate([dshift1, dscale1, dgate1], axis=1)[0]])
    dmod_c = jnp.concatenate([dshift_c, dscale_c], axis=1)[0]
    g_qkv_conv = jnp.concatenate([dcw_q[:DN_CONV], dcw_k[:DN_CONV], dcw_v[:DN_CONV]], axis=1)
    small = [
        ("loss", loss11.reshape(1)),
        ("norm_w", jnp.concatenate([dnw0_x + dnw0_c, dnw1], axis=0)),
        ("dmod_own", dmod_own),
        ("dmod_c", dmod_c),
        ("qkv_conv", g_qkv_conv),
        ("a_log", dpar[1, 32:32 + hh2].reshape(2, DN_HEADS)),
        ("dt_bias", dpar[0, 32:32 + hh2].reshape(2, DN_HEADS)),
        ("dn_norm", ddnn[0]),
        ("lru_conv_w", dlconv8[:LRU_CONV]),
        ("lru_conv_b", dlconv8[LRU_CONV]),
        ("lru_w_r", dw_r),
        ("lru_w_i", dw_i),
        ("lru_b_r", dlvecs[:, 0]),
        ("lru_b_i", dlvecs[:, 1]),
        ("lru_lambda", dlvecs[:, 2]),
        ("sc_conv", dscw8[:SC_CONV]),
        ("final_norm_w", dfnw[0]),
    ]
    names = [n for n, _ in small]
    shapes = [a.shape for _, a in small]
    spk, soffs = _pack([a for _, a in small])
    (sg,) = _gather_two_level([spk], "gather_small_grads")
    tot = dict(zip(names, _unpack(_sum_slots(sg, "sum_small_grads"), soffs, shapes)))
    i_own = names.index("dmod_own")
    dmod_all = jnp.stack([_unpack(sg[j], soffs[i_own:i_own + 1], shapes[i_own:i_own + 1])[0] for j in range(NDEV)])

    dmc_full = jnp.concatenate([tot["dmod_c"], jnp.zeros((d,), F32)])
    rows9 = jnp.concatenate([dmod_all, jnp.stack([dmc_full, jnp.zeros((3 * d,), F32)])[None],
                             jnp.zeros((7, 2, 3 * d), F32)], axis=0)
    dm = lax.dynamic_slice_in_dim(rows9, me * mcols, mcols, axis=2).transpose(1, 0, 2)
    g_modw, dl_modw, nm_modw, nv_modw, gcc = _mod_bwd(c16, dm, mod_w, m_mod_w, v_mod_w)
    gcc_pk, _ = _pack([gcc[0, :, 0]])
    (gcc_g,) = _exchange([gcc_pk], "gather_c_ctx", scatter=False)
    gcc_sum = _sum_slots(gcc_g, "sum_c_ctx").reshape(-1)[:d]
    g_c_ctx_pre = gcc_sum

    grad_mod_b = tot["dmod_own"] + jnp.stack([dmc_full, jnp.zeros((3 * d,), F32)])

    def shard(a):
        wd = a.shape[-1] // NDEV
        return lax.dynamic_slice_in_dim(a, me * wd, wd, axis=a.ndim - 1)

    small_w = [
        ("c_ctx", c_ctx, None, m_c_ctx, v_c_ctx),
        ("mod_b", mod_b, grad_mod_b, m_mod_b, v_mod_b),
        ("norm_w", norm_w, tot["norm_w"], m_norm_w, v_norm_w),
        ("ab_qkv_conv", ab_qkv_conv, shard(tot["qkv_conv"])[None], m_ab_qkv_conv, v_ab_qkv_conv),
        ("ab_a_log", ab_a_log, tot["a_log"][None], m_ab_a_log, v_ab_a_log),
        ("ab_dt_bias", ab_dt_bias, tot["dt_bias"][None], m_ab_dt_bias, v_ab_dt_bias),
        ("ab_dn_norm", ab_dn_norm, tot["dn_norm"][None], m_ab_dn_norm, v_ab_dn_norm),
        ("ab_lru_conv_w", ab_lru_conv_w, shard(tot["lru_conv_w"])[None], m_ab_lru_conv_w, v_ab_lru_conv_w),
        ("ab_lru_conv_b", ab_lru_conv_b, tot["lru_conv_b"][None], m_ab_lru_conv_b, v_ab_lru_conv_b),
        ("ab_lru_w_r", ab_lru_w_r, tot["lru_w_r"][None], m_ab_lru_w_r, v_ab_lru_w_r),
        ("ab_lru_b_r", ab_lru_b_r, shard(tot["lru_b_r"])[None], m_ab_lru_b_r, v_ab_lru_b_r),
        ("ab_lru_w_i", ab_lru_w_i, tot["lru_w_i"][None], m_ab_lru_w_i, v_ab_lru_w_i),
        ("ab_lru_b_i", ab_lru_b_i, shard(tot["lru_b_i"])[None], m_ab_lru_b_i, v_ab_lru_b_i),
        ("ab_lru_lambda", ab_lru_lambda, shard(tot["lru_lambda"])[None], m_ab_lru_lambda, v_ab_lru_lambda),
        ("sc_conv", sc_conv, shard(tot["sc_conv"])[None], m_sc_conv, v_sc_conv),
        ("final_norm_w", final_norm_w, tot["final_norm_w"], m_final_norm_w, v_final_norm_w),
    ]
    sg_cc = jax.nn.sigmoid(c_ctx)
    g_c_ctx = g_c_ctx_pre * (sg_cc * (1.0 + c_ctx * (1.0 - sg_cc)))
    small_w[0] = ("c_ctx", c_ctx, g_c_ctx, m_c_ctx, v_c_ctx)
    wshapes = [w.shape for _, w, _, _, _ in small_w]
    wpk, woffs = _pack([w for _, w, _, _, _ in small_w])
    gpk, _ = _pack([g for _, _, g, _, _ in small_w])
    mpk, _ = _pack([m for _, _, _, m, _ in small_w])
    vpk, _ = _pack([v for _, _, _, _, v in small_w])
    dl_pk, nm_pk, nv_pk = _adamw_flat(wpk, gpk, mpk, vpk, "adamw_small")
    sm_g = {n: g for n, _, g, _, _ in small_w}
    sm_dl = dict(zip([n for n, *_ in small_w], _unpack(dl_pk, woffs, wshapes)))
    sm_nm = dict(zip([n for n, *_ in small_w], _unpack(nm_pk, woffs, wshapes)))
    sm_nv = dict(zip([n for n, *_ in small_w], _unpack(nv_pk, woffs, wshapes)))

    grads, deltas, new_m, new_v = {}, {}, {}, {}
    for n in sm_g:
        grads[n], deltas[n], new_m[n], new_v[n] = sm_g[n], sm_dl[n], sm_nm[n], sm_nv[n]
    grads["mod_w"], deltas["mod_w"], new_m["mod_w"], new_v["mod_w"] = g_modw, dl_modw, nm_modw, nv_modw
    for n in ("ab_w_in", "ab_w_out", "sc_w_in", "sc_w_out"):
        g, dl, m2, v2 = big[n]
        grads[n], deltas[n], new_m[n], new_v[n] = g[None], dl[None], m2[None], v2[None]

    order = ["c_ctx", "mod_w", "mod_b", "norm_w", "ab_w_in", "ab_qkv_conv", "ab_a_log", "ab_dt_bias", "ab_dn_norm",
             "ab_lru_conv_w", "ab_lru_conv_b", "ab_lru_w_r", "ab_lru_b_r", "ab_lru_w_i", "ab_lru_b_i", "ab_lru_lambda",
             "ab_w_out", "sc_w_in", "sc_conv", "sc_w_out", "final_norm_w"]
    loss = tot["loss"][0]
    return (loss, grad_x[None], *[grads[n] for n in order], *[deltas[n] for n in order],
            *[new_m[n] for n in order], *[new_v[n] for n in order])
```

```python
import functools

import jax
import jax.numpy as jnp
from jax import lax
from jax.experimental import pallas as pl
from jax.experimental.pallas import tpu as pltpu

F32 = jnp.float32
BF16 = jnp.bfloat16

NDEV = 8
D_MODEL = 4096
SEQ = 4096
CTX_LEN = 256
GRID_W = 64
EPS = 1e-6
DN_HEADS = 16
DN_HEAD_DIM = 128
DN_CONV = 4
DN_CHUNK = 64
LRU_WIDTH = 2048
LRU_BLOCKS = 16
LRU_CONV = 4
LRU_C = 8.0
SC_CONV = 3
ADAM_LR = 0.001
ADAM_B1 = 0.9
ADAM_B2 = 0.999
ADAM_EPS = 1e-08
ADAM_WD = 0.01
ADAM_STEP = 10

ROW_TILE = 256
HEADS_PER_STEP = 16
LANE = 128
VMEM_LIMIT = 56 * 1024 * 1024


def _dims():
    d = D_MODEL
    dnw = DN_HEADS * DN_HEAD_DIM
    lw = LRU_WIDTH
    off_lru = 3 * dnw
    off_beta = off_lru + lw
    ab_state = off_beta + 4 * DN_HEADS
    ab_in = ab_state + dnw + lw
    return dict(d=d, dnw=dnw, lw=lw, off_lru=off_lru, off_beta=off_beta, ab_state=ab_state, ab_in=ab_in,
                nmain=3 * dnw + lw + dnw + lw, r=CTX_LEN + SEQ, lbd=lw // LRU_BLOCKS)


def _div(n, pref, mult):
    best = None
    for t in range(mult, min(n, pref) + 1, mult):
        if n % t == 0:
            best = t
    return best if best is not None else n


def _cp(sem, vmem=VMEM_LIMIT):
    return pltpu.CompilerParams(dimension_semantics=sem, vmem_limit_bytes=vmem)


def _silu(z):
    return z * jax.nn.sigmoid(z)


def _dsilu(z):
    s = jax.nn.sigmoid(z)
    return s * (1.0 + z * (1.0 - s))


def _softplus(z):
    return jnp.maximum(z, 0.0) + jnp.log1p(jnp.exp(-jnp.abs(z)))


def _neg_expm1(z):
    series = -z * (1.0 + z * (0.5 + z * (1.0 / 6.0 + z * (1.0 / 24.0))))
    return jnp.where(z > -0.05, series, 1.0 - jnp.exp(z))


def _me():
    return 4 * lax.axis_index("x") + 2 * lax.axis_index("y") + lax.axis_index("c")


def _exchange(srcs, name, scatter):
    n = len(srcs)

    def body(*refs):
        src_refs, out_refs = refs[:n], refs[n:2 * n]
        send_sems, recv_sems, local_sems = refs[2 * n:]
        x, y, c = lax.axis_index("x"), lax.axis_index("y"), lax.axis_index("c")
        me = 4 * x + 2 * y + c
        copies = []
        for i in range(n):
            mine = src_refs[i].at[me] if scatter else src_refs[i]
            lc = pltpu.make_async_copy(mine, out_refs[i].at[me], local_sems.at[i])
            lc.start()
            copies.append(lc)
            for k in range(1, NDEV):
                px = 1 - x if (k >> 2) & 1 else x
                py = 1 - y if (k >> 1) & 1 else y
                pc = 1 - c if k & 1 else c
                peer = 4 * px + 2 * py + pc
                src = src_refs[i].at[peer] if scatter else src_refs[i]
                cp = pltpu.make_async_remote_copy(
                    src_ref=src, dst_ref=out_refs[i].at[me], send_sem=send_sems.at[i, k - 1],
                    recv_sem=recv_sems.at[i, k - 1], device_id=(px, py, pc), device_id_type=pl.DeviceIdType.MESH)
                cp.start()
                copies.append(cp)
        for cp in copies:
            cp.wait()

    any_spec = pl.BlockSpec(memory_space=pl.ANY)
    out_shape = tuple(jax.ShapeDtypeStruct(s.shape if scatter else (NDEV,) + s.shape, s.dtype) for s in srcs)
    outs = pl.pallas_call(
        body, name=name, out_shape=out_shape, in_specs=[any_spec] * n, out_specs=tuple([any_spec] * n),
        scratch_shapes=[pltpu.SemaphoreType.DMA((n, NDEV - 1)), pltpu.SemaphoreType.DMA((n, NDEV - 1)),
                        pltpu.SemaphoreType.DMA((n,))],
    )(*srcs)
    return list(outs)


def _flip(v, bit):
    return 1 - v if bit else v


def _gather_ops(src_refs, out_refs, send_sems, recv_sems, local_sems):
    n = len(src_refs)

    def setup():
        x, y, c = lax.axis_index("x"), lax.axis_index("y"), lax.axis_index("c")
        chips = [(_flip(x, k >> 1), _flip(y, k & 1)) for k in (1, 2, 3)]
        return x, y, c, chips

    def copy(i, sem, slot, to, src=None):
        return pltpu.make_async_remote_copy(
            src_ref=out_refs[i].at[slot] if src is None else src, dst_ref=out_refs[i].at[slot],
            send_sem=send_sems.at[i, sem], recv_sem=recv_sems.at[i, sem], device_id=to,
            device_id_type=pl.DeviceIdType.MESH)

    def own(i, me):
        return pltpu.make_async_copy(src_refs[i], out_refs[i].at[me], local_sems.at[i])

    def first(i, x, y, c, chips):
        me = 4 * x + 2 * y + c
        return [copy(i, 0, me, (x, y, 1 - c), src=src_refs[i])] + [
            copy(i, 1 + j, me, (px, py, c), src=src_refs[i]) for j, (px, py) in enumerate(chips)]

    def start():
        x, y, c, chips = setup()
        for i in range(n):
            own(i, 4 * x + 2 * y + c).start()
            for cp in first(i, x, y, c, chips):
                cp.start()

    def finish():
        x, y, c, chips = setup()
        here = (x, y, c)
        passed = []
        for i in range(n):
            for j, (px, py) in enumerate(chips):
                theirs = 4 * px + 2 * py + c
                copy(i, 1 + j, theirs, here).wait_recv()
                fwd = copy(i, 4 + j, theirs, (x, y, 1 - c))
                fwd.start()
                passed.append(fwd)
        for i in range(n):
            copy(i, 0, 4 * x + 2 * y + (1 - c), here).wait_recv()
            for j, (px, py) in enumerate(chips):
                copy(i, 4 + j, 4 * px + 2 * py + (1 - c), here).wait_recv()
        for i in range(n):
            for cp in first(i, x, y, c, chips):
                cp.wait_send()
        for cp in passed:
            cp.wait_send()
        for i in range(n):
            own(i, 4 * x + 2 * y + c).wait()

    return start, finish


def _chipx_ops(src_refs, out_refs, send_sems, recv_sems, local_sems):
    n = len(src_refs)

    def copies():
        x, y, c = lax.axis_index("x"), lax.axis_index("y"), lax.axis_index("c")
        mine = 2 * x + y
        res = []
        for i in range(n):
            res.append(pltpu.make_async_copy(src_refs[i].at[mine], out_refs[i].at[mine], local_sems.at[i]))
            for k in (1, 2, 3):
                px, py = _flip(x, k >> 1), _flip(y, k & 1)
                res.append(pltpu.make_async_remote_copy(
                    src_ref=src_refs[i].at[2 * px + py], dst_ref=out_refs[i].at[mine], send_sem=send_sems.at[i, k - 1],
                    recv_sem=recv_sems.at[i, k - 1], device_id=(px, py, c), device_id_type=pl.DeviceIdType.MESH))
        return res

    def start():
        for cp in copies():
            cp.start()

    def finish():
        for cp in copies():
            cp.wait()

    return start, finish


def _rider(kind, arrays):
    n = len(arrays)
    any_spec = pl.BlockSpec(memory_space=pl.ANY)
    if kind == "gather":
        out_shapes = [jax.ShapeDtypeStruct((NDEV,) + a.shape, a.dtype) for a in arrays]
        nsem, make = NDEV - 1, _gather_ops
    else:
        out_shapes = [jax.ShapeDtypeStruct(a.shape, a.dtype) for a in arrays]
        nsem, make = NDEV // 2 - 1, _chipx_ops
    scratch = [pltpu.SemaphoreType.DMA((n, nsem)), pltpu.SemaphoreType.DMA((n, nsem)), pltpu.SemaphoreType.DMA((n,))]
    return list(arrays), [any_spec] * n, out_shapes, [any_spec] * n, scratch, make


def _exchange_call(kind, arrays, name):
    n = len(arrays)
    ins, in_specs, out_shapes, out_specs, scratch, make = _rider(kind, arrays)

    def body(*refs):
        start, finish = make(refs[:n], refs[n:2 * n], *refs[2 * n:])
        start()
        finish()

    outs = pl.pallas_call(body, name=name, out_shape=tuple(out_shapes), in_specs=in_specs, out_specs=tuple(out_specs),
                          scratch_shapes=scratch)(*ins)
    return list(outs)


def _gather_two_level(srcs, name):
    return _exchange_call("gather", srcs, name)


def _call_with_rider(body, rider, *, name, grid, out_shape, in_specs, out_specs, scratch_shapes, compiler_params,
                     operands):
    if rider is None:
        outs = pl.pallas_call(body, name=name, grid=grid, out_shape=tuple(out_shape), in_specs=list(in_specs),
                              out_specs=tuple(out_specs), scratch_shapes=list(scratch_shapes),
                              compiler_params=compiler_params)(*operands)
        return list(outs), []
    r_ins, r_in_specs, r_out_shapes, r_out_specs, r_scratch, make = _rider(*rider)
    ni, no, ns, nr = len(in_specs), len(out_shape), len(scratch_shapes), len(r_ins)

    def riding(*refs):
        ins, rins = refs[:ni], refs[ni:ni + nr]
        outs, routs = refs[ni + nr:ni + nr + no], refs[ni + nr + no:ni + nr + no + nr]
        scr, rscr = refs[ni + 2 * nr + no:ni + 2 * nr + no + ns], refs[ni + 2 * nr + no + ns:]
        start, finish = make(rins, routs, *rscr)
        ids = [pl.program_id(a) for a in range(len(grid))]
        first, last = ids[0] == 0, ids[0] == grid[0] - 1
        for a in range(1, len(grid)):
            first = jnp.logical_and(first, ids[a] == 0)
            last = jnp.logical_and(last, ids[a] == grid[a] - 1)

        @pl.when(first)
        def _():
            start()

        body(*ins, *outs, *scr)

        @pl.when(last)
        def _():
            finish()

    outs = pl.pallas_call(
        riding, name=name, grid=grid, out_shape=tuple(out_shape) + tuple(r_out_shapes),
        in_specs=list(in_specs) + r_in_specs, out_specs=tuple(out_specs) + tuple(r_out_specs),
        scratch_shapes=list(scratch_shapes) + r_scratch, compiler_params=compiler_params)(*operands, *r_ins)
    return list(outs[:no]), list(outs[no:])


def _sibling_swap(srcs, name):
    n = len(srcs)
    nchip = NDEV // 2

    def body(*refs):
        src_refs, out_refs = refs[:n], refs[n:2 * n]
        send_sems, recv_sems = refs[2 * n:]
        x, y, c = lax.axis_index("x"), lax.axis_index("y"), lax.axis_index("c")
        copies = []
        for i in range(n):
            for q in range(nchip):
                cp = pltpu.make_async_remote_copy(
                    src_ref=src_refs[i].at[2 * q + (1 - c)], dst_ref=out_refs[i].at[q], send_sem=send_sems.at[i, q],
                    recv_sem=recv_sems.at[i, q], device_id=(x, y, 1 - c), device_id_type=pl.DeviceIdType.MESH)
                cp.start()
                copies.append(cp)
        for cp in copies:
            cp.wait()

    any_spec = pl.BlockSpec(memory_space=pl.ANY)
    out_shape = tuple(jax.ShapeDtypeStruct((nchip,) + s.shape[1:], s.dtype) for s in srcs)
    outs = pl.pallas_call(
        body, name=name, out_shape=out_shape, in_specs=[any_spec] * n, out_specs=tuple([any_spec] * n),
        scratch_shapes=[pltpu.SemaphoreType.DMA((n, nchip)), pltpu.SemaphoreType.DMA((n, nchip))],
    )(*srcs)
    return list(outs)


def _chip_partial(own, sib, name):
    _, rows, cols = own.shape
    tm = _div(rows, max(16, (128 * 4096) // cols), 16)

    def body(c_ref, o_ref, s_ref, p_ref):
        p_ref[0] = (o_ref[0].astype(F32) + s_ref[0].astype(F32)).astype(BF16)

    core = lax.axis_index("c").astype(jnp.int32).reshape(1)
    grid_spec = pltpu.PrefetchScalarGridSpec(
        num_scalar_prefetch=1, grid=(NDEV // 2, rows // tm),
        in_specs=[pl.BlockSpec((1, tm, cols), lambda q, i, c_ref: (2 * q + c_ref[0], i, 0)),
                  pl.BlockSpec((1, tm, cols), lambda q, i, c_ref: (q, i, 0))],
        out_specs=pl.BlockSpec((1, tm, cols), lambda q, i, c_ref: (q, i, 0)))
    return pl.pallas_call(
        body, name=name, grid_spec=grid_spec, out_shape=jax.ShapeDtypeStruct(sib.shape, BF16),
        compiler_params=_cp(("arbitrary", "arbitrary")))(core, own, sib)


def _chip_exchange(parts, name):
    return _exchange_call("chipx", parts, name)


def _sum_slots(g, name):
    _, rows, cols = g.shape
    tm = _div(rows, 1024, 8)

    def body(g_ref, o_ref):
        acc = g_ref[0]
        for j in range(1, NDEV):
            acc = acc + g_ref[j]
        o_ref[...] = acc

    return pl.pallas_call(
        body, name=name, grid=(rows // tm,), out_shape=jax.ShapeDtypeStruct((rows, cols), F32),
        in_specs=[pl.BlockSpec((NDEV, tm, cols), lambda i: (0, i, 0))],
        out_specs=pl.BlockSpec((tm, cols), lambda i: (i, 0)), compiler_params=_cp(("arbitrary",)))(g)


def _mm_nn(a, b3, out_dtype, name, tm=1152, tn=1024, rider=None):
    m_, k_ = a.shape
    j_, _, nj = b3.shape
    tm = _div(m_, tm, 16)
    tn = _div(nj, tn, LANE)
    per = nj // tn

    def body(a_ref, b_ref, o_ref):
        o_ref[...] = jnp.dot(a_ref[...], b_ref[0], preferred_element_type=F32).astype(o_ref.dtype)

    outs, rode = _call_with_rider(
        body, rider, name=name, grid=(j_ * per, m_ // tm),
        out_shape=(jax.ShapeDtypeStruct((m_, j_ * nj), out_dtype),),
        in_specs=[pl.BlockSpec((tm, k_), lambda n, m: (m, 0)),
                  pl.BlockSpec((1, k_, tn), lambda n, m: (n // per, 0, n % per))],
        out_specs=(pl.BlockSpec((tm, tn), lambda n, m: (m, n)),), scratch_shapes=[],
        compiler_params=_cp(("arbitrary", "arbitrary")), operands=(a, b3))
    return outs[0] if rider is None else (outs[0], rode)


def _mm_nt(a, b3, out_dtype, name, tm=1152, tn=1024, tk=2048, rider=None):
    m_, k_ = a.shape
    j_, n_, kj = b3.shape
    tm = _div(m_, tm, 16)
    tn = _div(n_, tn, LANE)
    tk = _div(kj, tk, LANE)
    perk = kj // tk
    nk = k_ // tk

    def body(a_ref, b_ref, o_ref, acc_ref):
        k = pl.program_id(2)

        @pl.when(k == 0)
        def _():
            acc_ref[...] = jnp.zeros_like(acc_ref)

        acc_ref[...] += lax.dot_general(a_ref[...], b_ref[0], (((1,), (1,)), ((), ())), preferred_element_type=F32)

        @pl.when(k == nk - 1)
        def _():
            o_ref[...] = acc_ref[...].astype(o_ref.dtype)

    outs, rode = _call_with_rider(
        body, rider, name=name, grid=(n_ // tn, m_ // tm, nk),
        out_shape=(jax.ShapeDtypeStruct((m_, n_), out_dtype),),
        in_specs=[pl.BlockSpec((tm, tk), lambda n, m, k: (m, k)),
                  pl.BlockSpec((1, tn, tk), lambda n, m, k: (k // perk, n, k % perk))],
        out_specs=(pl.BlockSpec((tm, tn), lambda n, m, k: (m, n)),),
        scratch_shapes=[pltpu.VMEM((tm, tn), F32)],
        compiler_params=_cp(("arbitrary", "arbitrary", "arbitrary")), operands=(a, b3))
    return outs[0] if rider is None else (outs[0], rode)


def _mm_tn(a, b, out_dtype, name, oj=1, tm=1024, tn=512):
    k_, m_ = a.shape
    _, n_ = b.shape
    nj = n_ // oj
    tm = _div(m_, tm, LANE)
    tn = _div(nj, tn, LANE)
    per = nj // tn

    def body(a_ref, b_ref, o_ref):
        o_ref[0] = lax.dot_general(a_ref[...], b_ref[...], (((0,), (0,)), ((), ())),
                                   preferred_element_type=F32).astype(o_ref.dtype)

    return pl.pallas_call(
        body, name=name, grid=(m_ // tm, n_ // tn), out_shape=jax.ShapeDtypeStruct((oj, m_, nj), out_dtype),
        in_specs=[pl.BlockSpec((k_, tm), lambda m, n: (0, m)), pl.BlockSpec((k_, tn), lambda m, n: (0, n))],
        out_specs=pl.BlockSpec((1, tm, tn), lambda m, n: (n // per, m, n % per)),
        compiler_params=_cp(("arbitrary", "arbitrary")))(a, b)


def _mod_fwd(c16, w, bias):
    _, d, cols = w.shape
    tn = _div(cols, 512, LANE)

    def body(c_ref, w_ref, b_ref, o_ref):
        sc = _silu(c_ref[...])
        o_ref[0] = jnp.dot(sc, w_ref[0], preferred_element_type=F32) + b_ref[0]

    return pl.pallas_call(
        body, name="mod_fwd", grid=(2, cols // tn), out_shape=jax.ShapeDtypeStruct((2, 16, cols), F32),
        in_specs=[pl.BlockSpec((16, d), lambda l, n: (0, 0)), pl.BlockSpec((1, d, tn), lambda l, n: (l, 0, n)),
                  pl.BlockSpec((1, 1, tn), lambda l, n: (l, 0, n))],
        out_specs=pl.BlockSpec((1, 16, tn), lambda l, n: (l, 0, n)),
        compiler_params=_cp(("arbitrary", "arbitrary")))(c16, w, bias)


def _adam(w, g, m, v):
    m2 = ADAM_B1 * m + (1.0 - ADAM_B1) * g
    v2 = ADAM_B2 * v + (1.0 - ADAM_B2) * (g * g)
    m_hat = m2 / (1.0 - ADAM_B1 ** ADAM_STEP)
    v_hat = v2 / (1.0 - ADAM_B2 ** ADAM_STEP)
    delta = -ADAM_LR * (m_hat / (jnp.sqrt(v_hat) + ADAM_EPS) + ADAM_WD * w)
    return delta, m2, v2


def _mod_bwd(c16, dm, w, m, v):
    _, d, cols = w.shape
    tm = _div(d, 256, 8)

    def body(c_ref, dm_ref, w_ref, m_ref, v_ref, g_ref, dl_ref, nm_ref, nv_ref, gc_ref):
        sc = _silu(c_ref[...])
        g = lax.dot_general(sc, dm_ref[0], (((0,), (0,)), ((), ())), preferred_element_type=F32,
                            precision=lax.Precision.HIGHEST)
        wv = w_ref[0]
        dl, m2, v2 = _adam(wv, g, m_ref[0], v_ref[0])
        g_ref[0] = g
        dl_ref[0] = dl
        nm_ref[0] = m2
        nv_ref[0] = v2
        gc_ref[0] = jnp.sum(wv * dm_ref[0, 8:9, :], axis=1, keepdims=True)

    big = pl.BlockSpec((1, tm, cols), lambda l, i: (l, i, 0))
    shp = jax.ShapeDtypeStruct(w.shape, F32)
    return pl.pallas_call(
        body, name="mod_bwd", grid=(2, d // tm),
        out_shape=(shp, shp, shp, shp, jax.ShapeDtypeStruct((2, d, 1), F32)),
        in_specs=[pl.BlockSpec((16, tm), lambda l, i: (0, i)), pl.BlockSpec((1, 16, cols), lambda l, i: (l, 0, 0)),
                  big, big, big],
        out_specs=(big, big, big, big, pl.BlockSpec((1, tm, 1), lambda l, i: (l, i, 0))),
        compiler_params=_cp(("arbitrary", "arbitrary")))(c16, dm, w, m, v)


def _row_spec(tm, d, off=0):
    return pl.BlockSpec((tm, d), lambda i: (i + off, 0))


def _vec_spec(d):
    return pl.BlockSpec((1, d), lambda i: (0, 0))


def _norm_fwd(x, nw, shift, scale, name):
    n, d = x.shape
    tm = _div(n, ROW_TILE, 16)

    def body(x_ref, nw_ref, sh_ref, sc_ref, hn_ref, r_ref):
        xv = x_ref[...]
        r = lax.rsqrt(jnp.mean(xv * xv, axis=1, keepdims=True) + EPS)
        hn_ref[...] = (xv * r * nw_ref[...] * (1.0 + sc_ref[...]) + sh_ref[...]).astype(BF16)
        r_ref[...] = r

    return pl.pallas_call(
        body, name=name, grid=(n // tm,),
        out_shape=(jax.ShapeDtypeStruct((n, d), BF16), jax.ShapeDtypeStruct((n, 1), F32)),
        in_specs=[_row_spec(tm, d), _vec_spec(d), _vec_spec(d), _vec_spec(d)],
        out_specs=(_row_spec(tm, d), _row_spec(tm, 1)), compiler_params=_cp(("arbitrary",)))(x, nw, shift, scale)


def _resid_norm_fwd(x, out0, gate, nw, shift, scale):
    n, d = x.shape
    tm = _div(n, ROW_TILE, 16)

    def body(x_ref, o_ref, g_ref, nw_ref, sh_ref, sc_ref, x1_ref, hn_ref, r_ref):
        xv = x_ref[...] + g_ref[...] * o_ref[...]
        r = lax.rsqrt(jnp.mean(xv * xv, axis=1, keepdims=True) + EPS)
        x1_ref[...] = xv
        hn_ref[...] = (xv * r * nw_ref[...] * (1.0 + sc_ref[...]) + sh_ref[...]).astype(BF16)
        r_ref[...] = r

    return pl.pallas_call(
        body, name="resid_norm1", grid=(n // tm,),
        out_shape=(jax.ShapeDtypeStruct((n, d), F32), jax.ShapeDtypeStruct((n, d), BF16),
                   jax.ShapeDtypeStruct((n, 1), F32)),
        in_specs=[_row_spec(tm, d), _row_spec(tm, d), _vec_spec(d), _vec_spec(d), _vec_spec(d), _vec_spec(d)],
        out_specs=(_row_spec(tm, d), _row_spec(tm, d), _row_spec(tm, 1)),
        compiler_params=_cp(("arbitrary",)))(x, out0, gate, nw, shift, scale)


def _loss_head(x1, out1, gate1, fnw, tgt):
    n, d = x1.shape
    tm = _div(n, ROW_TILE, 16)
    nsteps = n // tm

    def body(x_ref, o_ref, g_ref, w_ref, t_ref, dx_ref, do_ref, loss_ref, dw_ref, dg_ref, lacc_ref):
        i = pl.program_id(0)

        @pl.when(i == 0)
        def _():
            lacc_ref[...] = jnp.zeros_like(lacc_ref)
            dw_ref[...] = jnp.zeros_like(dw_ref)
            dg_ref[...] = jnp.zeros_like(dg_ref)

        ov = o_ref[...]
        gv = g_ref[...]
        wv = w_ref[...]
        xv = x_ref[...] + gv * ov
        r = lax.rsqrt(jnp.mean(xv * xv, axis=1, keepdims=True) + EPS)
        xh = xv * r
        diff = xh * wv - t_ref[...]
        lacc_ref[...] += jnp.sum(diff * diff, axis=0, keepdims=True)
        dy = diff * (1.0 / d)
        dw_ref[...] += jnp.sum(dy * xh, axis=0, keepdims=True)
        dxh = dy * wv
        dx = r * (dxh - xh * jnp.mean(xh * dxh, axis=1, keepdims=True))
        dx_ref[...] = dx
        dg_ref[...] += jnp.sum(dx * ov, axis=0, keepdims=True)
        do_ref[...] = (gv * dx).astype(BF16)

        @pl.when(i == nsteps - 1)
        def _():
            loss_ref[...] = (0.5 / d) * jnp.sum(lacc_ref[...], axis=1, keepdims=True)

    vec = jax.ShapeDtypeStruct((1, d), F32)
    return pl.pallas_call(
        body, name="loss_head", grid=(nsteps,),
        out_shape=(jax.ShapeDtypeStruct((n, d), F32), jax.ShapeDtypeStruct((n, d), BF16),
                   jax.ShapeDtypeStruct((1, 1), F32), vec, vec),
        in_specs=[_row_spec(tm, d), _row_spec(tm, d), _vec_spec(d), _vec_spec(d), _row_spec(tm, d)],
        out_specs=(_row_spec(tm, d), _row_spec(tm, d), pl.BlockSpec((1, 1), lambda i: (0, 0)), _vec_spec(d),
                   _vec_spec(d)),
        scratch_shapes=[pltpu.VMEM((1, d), F32)], compiler_params=_cp(("arbitrary",)))(x1, out1, gate1, fnw, tgt)


def _norm_bwd(x, rinv, dhns, nw, scale, name, row_off=0, dx_in=None, resid=None):
    n, d = x.shape
    tm = _div(n, ROW_TILE // 2, 16)
    nd = len(dhns)
    has_in = dx_in is not None
    has_res = resid is not None

    def body(*refs):
        it = iter(refs)
        x_ref, r_ref = next(it), next(it)
        dh_refs = [next(it) for _ in range(nd)]
        nw_ref, sc_ref = next(it), next(it)
        dxin_ref = next(it) if has_in else None
        o_ref, g_ref = (next(it), next(it)) if has_res else (None, None)
        dx_ref = next(it)
        do_ref = next(it) if has_res else None
        dsh_ref, dsc_ref, dnw_ref = next(it), next(it), next(it)
        dg_ref = next(it) if has_res else None
        i = pl.program_id(0)

        @pl.when(i == 0)
        def _():
            dsh_ref[...] = jnp.zeros_like(dsh_ref)
            dsc_ref[...] = jnp.zeros_like(dsc_ref)
            dnw_ref[...] = jnp.zeros_like(dnw_ref)
            if has_res:
                dg_ref[...] = jnp.zeros_like(dg_ref)

        dh = dh_refs[0][...]
        for rr in dh_refs[1:]:
            dh = dh + rr[...]
        r = r_ref[...]
        xh = x_ref[...] * r
        nwv = nw_ref[...]
        mod = 1.0 + sc_ref[...]
        dsh_ref[...] += jnp.sum(dh, axis=0, keepdims=True)
        dhx = dh * xh
        dsc_ref[...] += jnp.sum(dhx * nwv, axis=0, keepdims=True)
        dnw_ref[...] += jnp.sum(dhx * mod, axis=0, keepdims=True)
        dxh = dh * (nwv * mod)
        dx = r * (dxh - xh * jnp.mean(xh * dxh, axis=1, keepdims=True))
        if has_in:
            dx = dx + dxin_ref[...]
        dx_ref[...] = dx
        if has_res:
            dg_ref[...] += jnp.sum(dx * o_ref[...], axis=0, keepdims=True)
            do_ref[...] = (g_ref[...] * dx).astype(BF16)

    ins = [x, rinv] + list(dhns) + [nw, scale]
    in_specs = [_row_spec(tm, d), _row_spec(tm, 1)] + [_row_spec(tm, d, row_off)] * nd + [_vec_spec(d), _vec_spec(d)]
    if has_in:
        ins.append(dx_in)
        in_specs.append(_row_spec(tm, d))
    if has_res:
        ins += list(resid)
        in_specs += [_row_spec(tm, d), _vec_spec(d)]
    vec = jax.ShapeDtypeStruct((1, d), F32)
    out_shape = [jax.ShapeDtypeStruct((n, d), F32)]
    out_specs = [_row_spec(tm, d)]
    if has_res:
        out_shape.append(jax.ShapeDtypeStruct((n, d), BF16))
        out_specs.append(_row_spec(tm, d))
    out_shape += [vec, vec, vec]
    out_specs += [_vec_spec(d)] * 3
    if has_res:
        out_shape.append(vec)
        out_specs.append(_vec_spec(d))
    return pl.pallas_call(
        body, name=name, grid=(n // tm,), out_shape=tuple(out_shape), in_specs=in_specs, out_specs=tuple(out_specs),
        compiler_params=_cp(("arbitrary",)))(*ins)


def _seg_masks(rows, width, seg_bounds, offs):
    t = lax.broadcasted_iota(jnp.int32, (rows, width), 0)
    s0 = jnp.zeros_like(t)
    s1 = jnp.full_like(t, rows)
    for (a, b) in seg_bounds:
        inside = (t >= a) & (t < b)
        s0 = jnp.where(inside, a, s0)
        s1 = jnp.where(inside, b, s1)
    return {off: ((t + off >= s0) & (t + off < s1)) for off in offs}


def _shift_rows(v, off, valid):
    rows = v.shape[0]
    return jnp.where(valid, pltpu.roll(v, (-off) % rows, axis=0), 0.0)


def _conv_fwd_val(xv, wv, taps, masks):
    acc = None
    for tap, off in taps:
        xs = xv if off == 0 else _shift_rows(xv, off, masks[off])
        term = wv[tap:tap + 1, :] * xs
        acc = term if acc is None else acc + term
    return acc


def _conv_bwd_val(xv, dacc, wv, taps, masks):
    dx = None
    dws = []
    for tap, off in taps:
        xs = xv if off == 0 else _shift_rows(xv, off, masks[off])
        dws.append(jnp.sum(dacc * xs, axis=0, keepdims=True))
        ds = dacc if off == 0 else _shift_rows(dacc, -off, masks[-off])
        term = wv[tap:tap + 1, :] * ds
        dx = term if dx is None else dx + term
    return dx, dws


def _taps(k):
    left = k // 2
    return [(j, j - left) for j in range(k)]


def _seq_bounds():
    return [(0, CTX_LEN), (CTX_LEN, CTX_LEN + SEQ)]


def _dn_pre_fwd(proj, convw):
    dd = _dims()
    r = dd["r"]
    nstrip = 3 * DN_HEADS
    taps = _taps(DN_CONV)
    offs = [o for _, o in taps if o != 0]

    def body(x_ref, w_ref, o_ref):
        kind = pl.program_id(0) // DN_HEADS
        masks = _seg_masks(r, DN_HEAD_DIM, _seq_bounds(), offs)
        y = _silu(_conv_fwd_val(x_ref[...], w_ref[...], taps, masks))
        inv = lax.rsqrt(jnp.sum(y * y, axis=1, keepdims=True) + EPS)
        scale = jnp.where(kind == 0, inv * (DN_HEAD_DIM ** -0.5), jnp.where(kind == 1, inv, 1.0))
        o_ref[...] = y * scale

    return pl.pallas_call(
        body, name="dn_pre_fwd", grid=(nstrip,), out_shape=jax.ShapeDtypeStruct((r, 3 * dd["dnw"]), F32),
        in_specs=[pl.BlockSpec((r, DN_HEAD_DIM), lambda j: (0, j)), pl.BlockSpec((8, DN_HEAD_DIM), lambda j: (0, j))],
        out_specs=pl.BlockSpec((r, DN_HEAD_DIM), lambda j: (0, j)), compiler_params=_cp(("arbitrary",)))(proj, convw)


def _dn_pre_bwd(proj, convw, dqkv, kind):
    dd = _dims()
    r = dd["r"]
    taps = _taps(DN_CONV)
    offs = sorted({o for _, o in taps if o != 0} | {-o for _, o in taps if o != 0})
    cscale = DN_HEAD_DIM ** -0.5 if kind == 0 else 1.0

    def body(x_ref, w_ref, d_ref, dx_ref, dw_ref):
        masks = _seg_masks(r, DN_HEAD_DIM, _seq_bounds(), offs)
        xv = x_ref[...]
        wv = w_ref[...]
        acc = _conv_fwd_val(xv, wv, taps, masks)
        dout = d_ref[0] + d_ref[1]
        if kind == 2:
            dy = dout
        else:
            y = _silu(acc)
            inv = lax.rsqrt(jnp.sum(y * y, axis=1, keepdims=True) + EPS)
            dy = cscale * inv * (dout - y * (inv * inv) * jnp.sum(dout * y, axis=1, keepdims=True))
        dacc = dy * _dsilu(acc)
        dx, dws = _conv_bwd_val(xv, dacc, wv, taps, masks)
        dx_ref[...] = dx.astype(BF16)
        dw_ref[...] = jnp.concatenate(dws + [jnp.zeros((8 - len(dws), DN_HEAD_DIM), F32)], axis=0)

    base = kind * DN_HEADS
    return pl.pallas_call(
        body, name=f"dn_pre_bwd{kind}", grid=(DN_HEADS,),
        out_shape=(jax.ShapeDtypeStruct((r, dd["dnw"]), BF16), jax.ShapeDtypeStruct((8, dd["dnw"]), F32)),
        in_specs=[pl.BlockSpec((r, DN_HEAD_DIM), lambda j: (0, base + j)),
                  pl.BlockSpec((8, DN_HEAD_DIM), lambda j: (0, base + j)),
                  pl.BlockSpec((2, r, DN_HEAD_DIM), lambda j: (0, 0, j))],
        out_specs=(pl.BlockSpec((r, DN_HEAD_DIM), lambda j: (0, j)), pl.BlockSpec((8, DN_HEAD_DIM), lambda j: (0, j))),
        compiler_params=_cp(("arbitrary",)))(proj, convw, dqkv)


def _lru_conv_fwd(xl, convw):
    r, lw = xl.shape
    taps = _taps(LRU_CONV)
    offs = [o for _, o in taps if o != 0]

    def body(x_ref, w_ref, o_ref):
        masks = _seg_masks(r, LANE, _seq_bounds(), offs)
        wv = w_ref[...]
        o_ref[...] = _conv_fwd_val(x_ref[...], wv, taps, masks) + wv[4:5, :]

    return pl.pallas_call(
        body, name="lru_conv_fwd", grid=(lw // LANE,), out_shape=jax.ShapeDtypeStruct((r, lw), F32),
        in_specs=[pl.BlockSpec((r, LANE), lambda j: (0, j)), pl.BlockSpec((8, LANE), lambda j: (0, j))],
        out_specs=pl.BlockSpec((r, LANE), lambda j: (0, j)), compiler_params=_cp(("arbitrary",)))(xl, convw)


def _lru_conv_bwd(xl, convw, dxc):
    r, lw = xl.shape
    taps = _taps(LRU_CONV)
    offs = sorted({o for _, o in taps if o != 0} | {-o for _, o in taps if o != 0})

    def body(x_ref, w_ref, d_ref, dx_ref, dw_ref):
        masks = _seg_masks(r, LANE, _seq_bounds(), offs)
        dacc = d_ref[0] + d_ref[1]
        dx, dws = _conv_bwd_val(x_ref[...], dacc, w_ref[...], taps, masks)
        dx_ref[...] = dx
        db = jnp.sum(dacc, axis=0, keepdims=True)
        dw_ref[...] = jnp.concatenate(dws + [db, jnp.zeros((3, LANE), F32)], axis=0)

    return pl.pallas_call(
        body, name="lru_conv_bwd", grid=(lw // LANE,),
        out_shape=(jax.ShapeDtypeStruct((r, lw), F32), jax.ShapeDtypeStruct((8, lw), F32)),
        in_specs=[pl.BlockSpec((r, LANE), lambda j: (0, j)), pl.BlockSpec((8, LANE), lambda j: (0, j)),
                  pl.BlockSpec((2, r, LANE), lambda j: (0, 0, j))],
        out_specs=(pl.BlockSpec((r, LANE), lambda j: (0, j)), pl.BlockSpec((8, LANE), lambda j: (0, j))),
        compiler_params=_cp(("arbitrary",)))(xl, convw, dxc)


def _tri(rev_lane, n, m):
    r = lax.broadcasted_iota(jnp.int32, (n, n), 0)
    c = lax.broadcasted_iota(jnp.int32, (n, n), 1)
    return (r >= c).astype(F32), (r <= c).astype(F32)


def _hdot(a, b):
    return jnp.dot(a, b, preferred_element_type=F32, precision=lax.Precision.HIGHEST)


def _gate_lanes():
    lane = lax.broadcasted_iota(jnp.int32, (DN_CHUNK, LANE), 1)
    group = lane // 32
    rev = (lane % 32) >= DN_HEADS if DN_HEADS * 2 <= 32 else None
    return group, rev


def _dn_gates_fwd(pba, par):
    r = pba.shape[0]
    nch = r // DN_CHUNK
    hh = 2 * DN_HEADS

    def body(x_ref, p_ref, o_ref):
        lo, up = _tri(None, DN_CHUNK, DN_CHUNK)
        ones = jnp.ones((DN_CHUNK, DN_CHUNK), F32)
        lane = lax.broadcasted_iota(jnp.int32, (DN_CHUNK, LANE), 1)
        group = lane // 32
        rev = (lane % 32) >= DN_HEADS
        dtb = p_ref[0:1, :]
        nea = p_ref[1:2, :]

        def step(i, carry):
            rows = pl.ds(pl.multiple_of(i * DN_CHUNK, DN_CHUNK), DN_CHUNK)
            xv = x_ref[rows, :]
            beta = jax.nn.sigmoid(xv)
            g = jnp.where(group == 1, nea * _softplus(xv + dtb), 0.0)
            gc = jnp.where(rev, _hdot(up, g), _hdot(lo, g))
            gt = _hdot(ones, g)
            out = jnp.where(group == 0, beta, jnp.where(group == 1, g, 0.0))
            out = out + jnp.where(group == 2, pltpu.roll(gc, 32, axis=1), 0.0)
            out = out + jnp.where(group == 3, pltpu.roll(gt, 64, axis=1), 0.0)
            o_ref[rows, :] = out
            return carry

        lax.fori_loop(0, nch, step, 0)

    return pl.pallas_call(body, name="dn_gates_fwd", out_shape=jax.ShapeDtypeStruct((r, LANE), F32),
                          compiler_params=_cp(None))(pba, par)


def _dn_gates_bwd(pba, par, dsc):
    r = pba.shape[0]
    nch = r // DN_CHUNK

    def body(x_ref, p_ref, d_ref, dx_ref, dp_ref, acc_ref):
        lo, up = _tri(None, DN_CHUNK, DN_CHUNK)
        ones = jnp.ones((DN_CHUNK, DN_CHUNK), F32)
        lane = lax.broadcasted_iota(jnp.int32, (DN_CHUNK, LANE), 1)
        group = lane // 32
        rev = (lane % 32) >= DN_HEADS
        dtb = p_ref[0:1, :]
        nea = p_ref[1:2, :]
        acc_ref[...] = jnp.zeros_like(acc_ref)

        def step(i, carry):
            rows = pl.ds(pl.multiple_of(i * DN_CHUNK, DN_CHUNK), DN_CHUNK)
            xv = x_ref[rows, :]
            dv = d_ref[rows, :]
            beta = jax.nn.sigmoid(xv)
            z = xv + dtb
            sp = _softplus(z)
            dgc = jnp.where(group == 1, pltpu.roll(dv, LANE - 32, axis=1), 0.0)
            dgt = jnp.where(group == 1, pltpu.roll(dv, LANE - 64, axis=1), 0.0)
            dg = jnp.where(rev, _hdot(lo, dgc), _hdot(up, dgc)) + _hdot(ones, dgt)
            dg = dg + jnp.where(group == 1, dv, 0.0)
            dal = dg * nea * jax.nn.sigmoid(z)
            dpb = dv * beta * (1.0 - beta)
            dx_ref[rows, :] = jnp.where(group == 0, dpb, jnp.where(group == 1, dal, 0.0)).astype(BF16)
            acc_ref[0:1, :] += jnp.sum(jnp.where(group == 1, dal, 0.0), axis=0, keepdims=True)
            acc_ref[1:2, :] += jnp.sum(jnp.where(group == 1, dg * nea * sp, 0.0), axis=0, keepdims=True)
            return carry

        lax.fori_loop(0, nch, step, 0)
        dp_ref[...] = acc_ref[...]

    return pl.pallas_call(
        body, name="dn_gates_bwd",
        out_shape=(jax.ShapeDtypeStruct((r, LANE), BF16), jax.ShapeDtypeStruct((8, LANE), F32)),
        scratch_shapes=[pltpu.VMEM((8, LANE), F32)], compiler_params=_cp(None))(pba, par, dsc)


_B_NN = (((2,), (1,)), ((0,), (0,)))
_B_NT = (((2,), (2,)), ((0,), (0,)))
_B_TN = (((1,), (1,)), ((0,), (0,)))


def _bdot(a, b, dims, exact=False):
    return lax.dot_general(a, b, dims, preferred_element_type=F32,
                           precision=lax.Precision.HIGH if exact else None)


@jax.custom_vjp
def _inv_unit_tri(l):
    n = l.shape[-1]
    ri = lax.broadcasted_iota(jnp.int32, (n, n), 0)
    ci = lax.broadcasted_iota(jnp.int32, (n, n), 1)
    x = jnp.broadcast_to((ri == ci).astype(F32), l.shape)
    inner = 1
    for size in (4, 16, n):
        so, si = size.bit_length() - 1, inner.bit_length() - 1
        same_outer = (ri >> so) == (ci >> so)
        same_inner = (ri >> si) == (ci >> si)
        part = jnp.where(same_outer, jnp.where(same_inner, 0.0, l), 0.0)
        m = part if inner == 1 else _bdot(x, part, _B_NN, exact=True)
        y = x
        for _ in range(size // inner - 1):
            y = x - _bdot(m, y, _B_NN, exact=True)
        x = y
        inner = size
    return x


def _inv_fwd(l):
    a = _inv_unit_tri(l)
    return a, a


def _inv_bwd(a, da):
    t = _bdot(a, da, _B_TN, exact=True)
    return (-_bdot(t, a, _B_NT, exact=True),)


_inv_unit_tri.defvjp(_inv_fwd, _inv_bwd)


def _delta_chunk(q4, k4, v4, sc, s4, rev):
    hps = HEADS_PER_STEP
    c = DN_CHUNK
    hd = DN_HEAD_DIM
    ri = lax.broadcasted_iota(jnp.int32, (c, c), 0)
    ci = lax.broadcasted_iota(jnp.int32, (c, c), 1)
    ahead = (ri - ci) * jnp.where(rev, -1, 1)
    incl = ahead >= 0
    strict = ahead > 0
    sct = sc.T

    def heads(x):
        return jnp.concatenate([x[None, :, j * hd:(j + 1) * hd] for j in range(hps)], axis=0)

    def col(off):
        return jnp.concatenate([sc[None, :, off + j:off + j + 1] for j in range(hps)], axis=0)

    def row(off):
        return jnp.concatenate([sct[None, off + j:off + j + 1, :] for j in range(hps)], axis=0)

    q, k, v = heads(q4), heads(k4), heads(v4)
    beta_c, gc_c, gt_c = col(0), col(hps), col(2 * hps)
    beta_r, gc_r = row(0), row(hps)
    dmat = jnp.where(incl, jnp.exp(jnp.where(incl, gc_c - gc_r, 0.0)), 0.0)
    kk = _bdot(k, k, _B_NT, exact=True)
    qk = _bdot(q, k, _B_NT)
    a = _inv_unit_tri(jnp.where(strict, beta_c * kk * dmat, 0.0))
    w = _bdot(a * (beta_r * jnp.exp(gc_r)), k, _B_NN, exact=True)
    u = _bdot(a * beta_r, v, _B_NN, exact=True)
    u2 = u - _bdot(w, s4, _B_NN)
    o = jnp.exp(gc_c) * _bdot(q, s4, _B_NN) + _bdot(qk * dmat, u2, _B_NN)
    ke = k * jnp.exp(gt_c - gc_c)
    s_new = jnp.exp(gt_c[:, 0:1, :]) * s4 + _bdot(ke, u2, _B_TN)
    return jnp.concatenate([o[j] for j in range(hps)], axis=1), s_new


def _chunk_index(d, n, nctx, ntot):
    rev_idx = jnp.where(n < nctx, nctx - 1 - n, ntot - 1 - (n - nctx))
    return jnp.where(d == 0, n, rev_idx)


def _delta_fwd(qkv, scg, rider=None):
    dd = _dims()
    r, dnw = dd["r"], dd["dnw"]
    hps = HEADS_PER_STEP
    nhg = DN_HEADS // hps
    bw = hps * DN_HEAD_DIM
    nctx, ntot = CTX_LEN // DN_CHUNK, r // DN_CHUNK

    def body(q_ref, k_ref, v_ref, sc_ref, o_ref, ss_ref, s_ref):
        d, n = pl.program_id(0), pl.program_id(2)

        @pl.when(n == 0)
        def _():
            s_ref[...] = jnp.zeros_like(s_ref)

        s4 = s_ref[...]
        ss_ref[0, 0] = s4
        o4, s4n = _delta_chunk(q_ref[...], k_ref[...], v_ref[...], sc_ref[0], s4, d == 1)
        o_ref[0] = o4
        s_ref[...] = s4n

    def rows(d, g, n):
        return _chunk_index(d, n, nctx, ntot)

    return _call_with_rider(
        body, rider, name="delta_fwd", grid=(2, nhg, ntot),
        out_shape=(jax.ShapeDtypeStruct((2, r, dnw), F32),
                   jax.ShapeDtypeStruct((2, ntot, DN_HEADS, DN_HEAD_DIM, DN_HEAD_DIM), F32)),
        in_specs=[pl.BlockSpec((DN_CHUNK, bw), lambda d, g, n: (rows(d, g, n), g)),
                  pl.BlockSpec((DN_CHUNK, bw), lambda d, g, n: (rows(d, g, n), nhg + g)),
                  pl.BlockSpec((DN_CHUNK, bw), lambda d, g, n: (rows(d, g, n), 2 * nhg + g)),
                  pl.BlockSpec((1, DN_CHUNK, LANE), lambda d, g, n: (d * nhg + g, rows(d, g, n), 0))],
        out_specs=(pl.BlockSpec((1, DN_CHUNK, bw), lambda d, g, n: (d, rows(d, g, n), g)),
                   pl.BlockSpec((1, 1, hps, DN_HEAD_DIM, DN_HEAD_DIM), lambda d, g, n: (d, n, g, 0, 0))),
        scratch_shapes=[pltpu.VMEM((hps, DN_HEAD_DIM, DN_HEAD_DIM), F32)],
        compiler_params=_cp(("arbitrary", "arbitrary", "arbitrary")), operands=(qkv, qkv, qkv, scg))


def _delta_bwd(qkv, scg, states, do, rider=None):
    dd = _dims()
    r, dnw = dd["r"], dd["dnw"]
    hps = HEADS_PER_STEP
    nhg = DN_HEADS // hps
    bw = hps * DN_HEAD_DIM
    nctx, ntot = CTX_LEN // DN_CHUNK, r // DN_CHUNK

    def body(q_ref, k_ref, v_ref, sc_ref, ss_ref, do_ref, dq_ref, dk_ref, dv_ref, dsc_ref, ds_ref):
        d, n2 = pl.program_id(0), pl.program_id(2)

        @pl.when(n2 == 0)
        def _():
            ds_ref[...] = jnp.zeros_like(ds_ref)

        fn = functools.partial(_delta_chunk, rev=(d == 1))
        _, vjp = jax.vjp(fn, q_ref[...], k_ref[...], v_ref[...], sc_ref[0], ss_ref[0, 0])
        dq, dk, dv, dsc, ds = vjp((do_ref[...], ds_ref[...]))
        dq_ref[0] = dq
        dk_ref[0] = dk
        dv_ref[0] = dv
        dsc_ref[0] = dsc
        ds_ref[...] = ds

    def rows(d, g, n2):
        return _chunk_index(d, ntot - 1 - n2, nctx, ntot)

    blk = lambda col: pl.BlockSpec((DN_CHUNK, bw), lambda d, g, n2: (rows(d, g, n2), col(g)))
    oblk = pl.BlockSpec((1, DN_CHUNK, bw), lambda d, g, n2: (d, rows(d, g, n2), g))
    scblk = pl.BlockSpec((1, DN_CHUNK, LANE), lambda d, g, n2: (d * nhg + g, rows(d, g, n2), 0))
    big = jax.ShapeDtypeStruct((2, r, dnw), F32)
    return _call_with_rider(
        body, rider, name="delta_bwd", grid=(2, nhg, ntot),
        out_shape=(big, big, big, jax.ShapeDtypeStruct(scg.shape, F32)),
        in_specs=[blk(lambda g: g), blk(lambda g: nhg + g), blk(lambda g: 2 * nhg + g), scblk,
                  pl.BlockSpec((1, 1, hps, DN_HEAD_DIM, DN_HEAD_DIM), lambda d, g, n2: (d, ntot - 1 - n2, g, 0, 0)),
                  blk(lambda g: g)],
        out_specs=(oblk, oblk, oblk, scblk),
        scratch_shapes=[pltpu.VMEM((hps, DN_HEAD_DIM, DN_HEAD_DIM), F32)],
        compiler_params=_cp(("arbitrary", "arbitrary", "arbitrary")), operands=(qkv, qkv, qkv, scg, states, do))


def _lru_gate_vals(xn, wr, wi, br, bi, lam):
    r = jax.nn.sigmoid(jnp.dot(xn, wr, preferred_element_type=F32) + br)
    i = jax.nn.sigmoid(jnp.dot(xn, wi, preferred_element_type=F32) + bi)
    sp = _softplus(-lam)
    la = -LRU_C * r * sp
    a = jnp.exp(la)
    mm = jnp.sqrt(_neg_expm1(2.0 * la))
    return r, i, sp, la, a, mm


def _lru_gates_fwd(xc, w_r, w_i, vecs):
    r, lw = xc.shape
    bd = lw // LRU_BLOCKS
    tm = _div(r, ROW_TILE, 8)

    def body(x_ref, wr_ref, wi_ref, v_ref, a_ref, b_ref):
        for n in range(LRU_BLOCKS):
            sl = slice(n * bd, (n + 1) * bd)
            xn = x_ref[:, sl]
            _, i, _, _, a, mm = _lru_gate_vals(xn, wr_ref[0, n], wi_ref[0, n], v_ref[0, 0:1, sl], v_ref[0, 1:2, sl],
                                               v_ref[0, 2:3, sl])
            a_ref[0, :, sl] = a
            b_ref[0, :, sl] = mm * (i * xn)

    wspec = pl.BlockSpec((1, LRU_BLOCKS, bd, bd), lambda d, i: (d, 0, 0, 0))
    out = jax.ShapeDtypeStruct((2, r, lw), F32)
    ospec = pl.BlockSpec((1, tm, lw), lambda d, i: (d, i, 0))
    return pl.pallas_call(
        body, name="lru_gates_fwd", grid=(2, r // tm), out_shape=(out, out),
        in_specs=[pl.BlockSpec((tm, lw), lambda d, i: (i, 0)), wspec, wspec,
                  pl.BlockSpec((1, 8, lw), lambda d, i: (d, 0, 0))],
        out_specs=(ospec, ospec), compiler_params=_cp(("arbitrary", "arbitrary")))(xc, w_r, w_i, vecs)


def _lru_gates_bwd(xc, w_r, w_i, vecs, da, db):
    r, lw = xc.shape
    bd = lw // LRU_BLOCKS
    tm = _div(r, ROW_TILE, 8)

    def body(x_ref, wr_ref, wi_ref, v_ref, da_ref, db_ref, dx_ref, dwr_ref, dwi_ref, dv_ref):
        @pl.when(pl.program_id(1) == 0)
        def _():
            dwr_ref[...] = jnp.zeros_like(dwr_ref)
            dwi_ref[...] = jnp.zeros_like(dwi_ref)
            dv_ref[...] = jnp.zeros_like(dv_ref)

        for n in range(LRU_BLOCKS):
            sl = slice(n * bd, (n + 1) * bd)
            xn = x_ref[:, sl]
            wr, wi = wr_ref[0, n], wi_ref[0, n]
            lam = v_ref[0, 2:3, sl]
            rr, ii, sp, la, a, mm = _lru_gate_vals(xn, wr, wi, v_ref[0, 0:1, sl], v_ref[0, 1:2, sl], lam)
            dav, dbv = da_ref[0, :, sl], db_ref[0, :, sl]
            dmm = dbv * (ii * xn)
            dla = dav * a - dmm * (a * a) / jnp.maximum(mm, 1e-30)
            di = dbv * mm * xn
            dxn = dbv * mm * ii
            dr = dla * (-LRU_C * sp)
            dsp = jnp.sum(dla * (-LRU_C * rr), axis=0, keepdims=True)
            dpr = dr * rr * (1.0 - rr)
            dpi = di * ii * (1.0 - ii)
            dxn = dxn + lax.dot_general(dpr, wr, (((1,), (1,)), ((), ())), preferred_element_type=F32)
            dxn = dxn + lax.dot_general(dpi, wi, (((1,), (1,)), ((), ())), preferred_element_type=F32)
            dx_ref[0, :, sl] = dxn
            dwr_ref[0, n] += lax.dot_general(xn, dpr, (((0,), (0,)), ((), ())), preferred_element_type=F32)
            dwi_ref[0, n] += lax.dot_general(xn, dpi, (((0,), (0,)), ((), ())), preferred_element_type=F32)
            dv_ref[0, 0:1, sl] += jnp.sum(dpr, axis=0, keepdims=True)
            dv_ref[0, 1:2, sl] += jnp.sum(dpi, axis=0, keepdims=True)
            dv_ref[0, 2:3, sl] += dsp * (-jax.nn.sigmoid(-lam))

    wspec = pl.BlockSpec((1, LRU_BLOCKS, bd, bd), lambda d, i: (d, 0, 0, 0))
    vspec = pl.BlockSpec((1, 8, lw), lambda d, i: (d, 0, 0))
    big = pl.BlockSpec((1, tm, lw), lambda d, i: (d, i, 0))
    return pl.pallas_call(
        body, name="lru_gates_bwd", grid=(2, r // tm),
        out_shape=(jax.ShapeDtypeStruct((2, r, lw), F32), jax.ShapeDtypeStruct(w_r.shape, F32),
                   jax.ShapeDtypeStruct(w_i.shape, F32), jax.ShapeDtypeStruct((2, 8, lw), F32)),
        in_specs=[pl.BlockSpec((tm, lw), lambda d, i: (i, 0)), wspec, wspec, vspec, big, big],
        out_specs=(big, wspec, wspec, vspec), compiler_params=_cp(("arbitrary", "arbitrary")))(xc, w_r, w_i, vecs, da, db)


def _scan_tiles(rev):
    nctx, nlat = CTX_LEN // 8, SEQ // 8
    return nctx, nlat


def _tile_of(step, rev, nctx, ntot):
    if not rev:
        return step
    return jnp.where(step < nctx, nctx - 1 - step, ntot - 1 - (step - nctx))


def _lru_scan_fwd(a, b, d):
    _, r, lw = a.shape
    cw = _div(lw, 256, LANE)
    nctx, ntot = CTX_LEN // 8, r // 8
    rev = d == 1
    order = list(range(7, -1, -1)) if rev else list(range(8))

    def body(a_ref, b_ref, h_ref):
        def step(s, carry):
            t0 = pl.multiple_of(_tile_of(s, rev, nctx, ntot) * 8, 8)
            at = a_ref[0, pl.ds(t0, 8), :]
            bt = b_ref[0, pl.ds(t0, 8), :]
            rows = [None] * 8
            h = carry
            for i in order:
                h = at[i:i + 1, :] * h + bt[i:i + 1, :]
                rows[i] = h
            h_ref[pl.ds(t0, 8), :] = jnp.concatenate(rows, axis=0)
            return h

        lax.fori_loop(0, ntot, step, jnp.zeros((1, cw), F32))

    return pl.pallas_call(
        body, name=f"lru_scan_fwd{d}", grid=(lw // cw,), out_shape=jax.ShapeDtypeStruct((r, lw), F32),
        in_specs=[pl.BlockSpec((1, r, cw), lambda j: (d, 0, j)), pl.BlockSpec((1, r, cw), lambda j: (d, 0, j))],
        out_specs=pl.BlockSpec((r, cw), lambda j: (0, j)), compiler_params=_cp(("arbitrary",)))(a, b)


def _lru_scan_bwd(a, h, dh, d):
    _, r, lw = a.shape
    cw = _div(lw, 256, LANE)
    nctx, ntot = CTX_LEN // 8, r // 8
    rev = d == 1
    order = list(range(8)) if rev else list(range(7, -1, -1))

    def body(a_ref, h_ref, dh_ref, da_ref, db_ref):
        def step(s2, carry):
            s = ntot - 1 - s2
            t0 = pl.multiple_of(_tile_of(s, rev, nctx, ntot) * 8, 8)
            sp = jnp.maximum(s - 1, 0)
            tp = pl.multiple_of(_tile_of(sp, rev, nctx, ntot) * 8, 8)
            at = a_ref[0, pl.ds(t0, 8), :]
            ht = h_ref[pl.ds(t0, 8), :]
            hp = h_ref[pl.ds(tp, 8), :]
            dht = dh_ref[pl.ds(t0, 8), :]
            last = hp[0:1, :] if rev else hp[7:8, :]
            hprev_tile = jnp.where(s > 0, last, 0.0)
            das, dbs = [None] * 8, [None] * 8
            c = carry
            for i in order:
                lam = dht[i:i + 1, :] + c
                before = i + 1 if rev else i - 1
                hb = ht[before:before + 1, :] if 0 <= before < 8 else hprev_tile
                das[i] = lam * hb
                dbs[i] = lam
                c = at[i:i + 1, :] * lam
            da_ref[pl.ds(t0, 8), :] = jnp.concatenate(das, axis=0)
            db_ref[pl.ds(t0, 8), :] = jnp.concatenate(dbs, axis=0)
            return c

        lax.fori_loop(0, ntot, step, jnp.zeros((1, cw), F32))

    strip = pl.BlockSpec((r, cw), lambda j: (0, j))
    out = jax.ShapeDtypeStruct((r, lw), F32)
    return pl.pallas_call(
        body, name=f"lru_scan_bwd{d}", grid=(lw // cw,), out_shape=(out, out),
        in_specs=[pl.BlockSpec((1, r, cw), lambda j: (d, 0, j)), strip, strip],
        out_specs=(strip, strip), compiler_params=_cp(("arbitrary",)))(a, h, dh)


def _mix_post_fwd(o, hsum, proj, dnw_vec):
    dd = _dims()
    dnw, lw, nmain = dd["dnw"], dd["lw"], dd["nmain"]
    tm = _div(SEQ, ROW_TILE, 16)
    coff = CTX_LEN // tm
    gcol = (3 * dnw + lw) // dnw
    lcol = (3 * dnw + lw + dnw) // lw
    hd = DN_HEAD_DIM

    def body(o_ref, h_ref, g_ref, lg_ref, w_ref, y_ref):
        wv = w_ref[...]
        for hh in range(DN_HEADS):
            sl = slice(hh * hd, (hh + 1) * hd)
            os_ = o_ref[0, :, sl] + o_ref[1, :, sl]
            rr = lax.rsqrt(jnp.mean(os_ * os_, axis=1, keepdims=True) + EPS)
            y_ref[:, sl] = (os_ * rr * wv * _silu(g_ref[:, sl])).astype(BF16)
        y_ref[:, dnw:] = (h_ref[...] * _silu(lg_ref[...])).astype(BF16)

    return pl.pallas_call(
        body, name="mix_post_fwd", grid=(SEQ // tm,), out_shape=jax.ShapeDtypeStruct((SEQ, dnw + lw), BF16),
        in_specs=[pl.BlockSpec((2, tm, dnw), lambda i: (0, i + coff, 0)), pl.BlockSpec((tm, lw), lambda i: (i, 0)),
                  pl.BlockSpec((tm, dnw), lambda i: (i + coff, gcol)), pl.BlockSpec((tm, lw), lambda i: (i + coff, lcol)),
                  pl.BlockSpec((1, hd), lambda i: (0, 0))],
        out_specs=pl.BlockSpec((tm, dnw + lw), lambda i: (i, 0)), compiler_params=_cp(("arbitrary",)))(
            o, hsum, proj, proj, dnw_vec)


def _mix_post_bwd(o, hsum, proj, dnw_vec, dy):
    dd = _dims()
    dnw, lw, r = dd["dnw"], dd["lw"], dd["r"]
    tm = _div(SEQ, ROW_TILE, 16)
    coff = CTX_LEN // tm
    gcol = (3 * dnw + lw) // dnw
    lcol = (3 * dnw + lw + dnw) // lw
    hd = DN_HEAD_DIM

    def body(o_ref, h_ref, g_ref, lg_ref, w_ref, dy_ref, do_ref, dh_ref, dg_ref, dw_ref):
        i = pl.program_id(0)

        @pl.when(i == 0)
        def _():
            dw_ref[...] = jnp.zeros_like(dw_ref)

        @pl.when(i < coff)
        def _():
            do_ref[...] = jnp.zeros_like(do_ref)
            dg_ref[...] = jnp.zeros_like(dg_ref)

        @pl.when(i >= coff)
        def _():
            wv = w_ref[...]
            dwacc = jnp.zeros((1, hd), F32)
            for hh in range(DN_HEADS):
                sl = slice(hh * hd, (hh + 1) * hd)
                os_ = o_ref[0, :, sl] + o_ref[1, :, sl]
                rr = lax.rsqrt(jnp.mean(os_ * os_, axis=1, keepdims=True) + EPS)
                nn = os_ * rr
                gv = g_ref[:, sl]
                dyv = dy_ref[:, sl]
                don = dyv * _silu(gv)
                dg_ref[:, sl] = (dyv * nn * wv * _dsilu(gv)).astype(BF16)
                dwacc = dwacc + jnp.sum(don * nn, axis=0, keepdims=True)
                dn_ = don * wv
                do_ref[:, sl] = rr * (dn_ - nn * jnp.mean(nn * dn_, axis=1, keepdims=True))
            dw_ref[...] += dwacc
            lgv = lg_ref[...]
            dyl = dy_ref[:, dnw:]
            dh_ref[...] = dyl * _silu(lgv)
            dg_ref[:, dnw:] = (dyl * h_ref[...] * _dsilu(lgv)).astype(BF16)

    lat = lambda i: jnp.maximum(i - coff, 0)
    return pl.pallas_call(
        body, name="mix_post_bwd", grid=(r // tm,),
        out_shape=(jax.ShapeDtypeStruct((r, dnw), F32), jax.ShapeDtypeStruct((SEQ, lw), F32),
                   jax.ShapeDtypeStruct((r, dnw + lw), BF16), jax.ShapeDtypeStruct((1, hd), F32)),
        in_specs=[pl.BlockSpec((2, tm, dnw), lambda i: (0, i, 0)), pl.BlockSpec((tm, lw), lambda i: (lat(i), 0)),
                  pl.BlockSpec((tm, dnw), lambda i: (i, gcol)), pl.BlockSpec((tm, lw), lambda i: (i, lcol)),
                  pl.BlockSpec((1, hd), lambda i: (0, 0)), pl.BlockSpec((tm, dnw + lw), lambda i: (lat(i), 0))],
        out_specs=(pl.BlockSpec((tm, dnw), lambda i: (i, 0)), pl.BlockSpec((tm, lw), lambda i: (lat(i), 0)),
                   pl.BlockSpec((tm, dnw + lw), lambda i: (i, 0)), pl.BlockSpec((1, hd), lambda i: (0, 0))),
        compiler_params=_cp(("arbitrary",)))(o, hsum, proj, proj, dnw_vec, dy)


def _sc_masks(tm, cw):
    t = lax.broadcasted_iota(jnp.int32, (tm, cw), 0) % GRID_W
    return {-1: t != 0, 1: t != GRID_W - 1}


def _sc_mix_fwd(p, convw):
    s, w4 = p.shape
    w = w4 // 4
    tm = _div(s, 512, GRID_W if GRID_W % 16 == 0 else 16 * GRID_W)
    cw = _div(w, 512, LANE)
    nc = w // cw
    taps = _taps(SC_CONV)

    def body(b_ref, c_ref, x_ref, g_ref, w_ref, y_ref):
        masks = _sc_masks(tm, cw)
        u = c_ref[...] * x_ref[...]
        z = _conv_fwd_val(u, w_ref[...], taps, masks)
        y_ref[...] = (b_ref[...] * z * _silu(g_ref[...])).astype(BF16)

    blk = lambda k: pl.BlockSpec((tm, cw), lambda c, i: (i, k * nc + c))
    return pl.pallas_call(
        body, name="sc_mix_fwd", grid=(nc, s // tm), out_shape=jax.ShapeDtypeStruct((s, w), BF16),
        in_specs=[blk(0), blk(1), blk(2), blk(3), pl.BlockSpec((8, cw), lambda c, i: (0, c))],
        out_specs=pl.BlockSpec((tm, cw), lambda c, i: (i, c)),
        compiler_params=_cp(("arbitrary", "arbitrary")))(p, p, p, p, convw)


def _sc_mix_bwd(p, convw, dy):
    s, w4 = p.shape
    w = w4 // 4
    tm = _div(s, 512, GRID_W if GRID_W % 16 == 0 else 16 * GRID_W)
    cw = _div(w, 512, LANE)
    nc = w // cw
    taps = _taps(SC_CONV)

    def body(b_ref, c_ref, x_ref, g_ref, w_ref, dy_ref, db_ref, dc_ref, dx_ref, dg_ref, dw_ref):
        @pl.when(pl.program_id(1) == 0)
        def _():
            dw_ref[...] = jnp.zeros_like(dw_ref)

        masks = _sc_masks(tm, cw)
        wv = w_ref[...]
        bv, cv, xv, gv, dyv = b_ref[...], c_ref[...], x_ref[...], g_ref[...], dy_ref[...]
        u = cv * xv
        z = _conv_fwd_val(u, wv, taps, masks)
        sg = _silu(gv)
        db_ref[...] = (dyv * z * sg).astype(BF16)
        dg_ref[...] = (dyv * bv * z * _dsilu(gv)).astype(BF16)
        dz = dyv * bv * sg
        du, dws = _conv_bwd_val(u, dz, wv, taps, masks)
        dc_ref[...] = (du * xv).astype(BF16)
        dx_ref[...] = (du * cv).astype(BF16)
        dw_ref[...] += jnp.concatenate(dws + [jnp.zeros((8 - len(dws), cw), F32)], axis=0)

    blk = lambda k: pl.BlockSpec((tm, cw), lambda c, i: (i, k * nc + c))
    oblk = pl.BlockSpec((tm, cw), lambda c, i: (i, c))
    wblk = pl.BlockSpec((8, cw), lambda c, i: (0, c))
    out = jax.ShapeDtypeStruct((s, w), BF16)
    return pl.pallas_call(
        body, name="sc_mix_bwd", grid=(nc, s // tm),
        out_shape=(out, out, out, out, jax.ShapeDtypeStruct((8, w), F32)),
        in_specs=[blk(0), blk(1), blk(2), blk(3), wblk, oblk],
        out_specs=(oblk, oblk, oblk, oblk, wblk),
        compiler_params=_cp(("arbitrary", "arbitrary")))(p, p, p, p, convw, dy)


def _adamw_slots(w, slots, m, v, name):
    rows, cols = w.shape
    nslot = slots.shape[0]
    tm = _div(rows, max(16, (64 * 4096) // cols), 16)

    def body(w_ref, s_ref, m_ref, v_ref, g_ref, dl_ref, nm_ref, nv_ref):
        g = s_ref[0].astype(F32)
        for j in range(1, nslot):
            g = g + s_ref[j].astype(F32)
        dl, m2, v2 = _adam(w_ref[...], g, m_ref[...], v_ref[...])
        g_ref[...] = g
        dl_ref[...] = dl
        nm_ref[...] = m2
        nv_ref[...] = v2

    blk = pl.BlockSpec((tm, cols), lambda i: (i, 0))
    out = jax.ShapeDtypeStruct((rows, cols), F32)
    return pl.pallas_call(
        body, name=name, grid=(rows // tm,), out_shape=(out, out, out, out),
        in_specs=[blk, pl.BlockSpec((nslot, tm, cols), lambda i: (0, i, 0)), blk, blk],
        out_specs=(blk, blk, blk, blk), compiler_params=_cp(("arbitrary",)))(w, slots, m, v)


def _adamw_flat(w, g, m, v, name):
    rows, cols = w.shape
    tm = _div(rows, 1024, 8)

    def body(w_ref, g_ref, m_ref, v_ref, dl_ref, nm_ref, nv_ref):
        dl, m2, v2 = _adam(w_ref[...], g_ref[...], m_ref[...], v_ref[...])
        dl_ref[...] = dl
        nm_ref[...] = m2
        nv_ref[...] = v2

    blk = pl.BlockSpec((tm, cols), lambda i: (i, 0))
    out = jax.ShapeDtypeStruct((rows, cols), F32)
    return pl.pallas_call(
        body, name=name, grid=(rows // tm,), out_shape=(out, out, out), in_specs=[blk] * 4, out_specs=(blk, blk, blk),
        compiler_params=_cp(("arbitrary",)))(w, g, m, v)


def _pack(arrs):
    flat = [a.reshape(-1).astype(F32) for a in arrs]
    sizes = [f.shape[0] for f in flat]
    total = sum(sizes)
    padded = -(-total // 1024) * 1024
    if padded > total:
        flat.append(jnp.zeros((padded - total,), F32))
    offs, o = [], 0
    for s_ in sizes:
        offs.append(o)
        o += s_
    return jnp.concatenate(flat).reshape(padded // LANE, LANE), offs


def _unpack(flat2d, offs, shapes):
    flat = flat2d.reshape(-1)
    out = []
    for o, shp in zip(offs, shapes):
        n = 1
        for s_ in shp:
            n *= s_
        out.append(flat[o:o + n].reshape(shp))
    return out


def _pad_rows(a, rows):
    return jnp.concatenate([a, jnp.zeros((rows - a.shape[0],) + a.shape[1:], a.dtype)], axis=0)


def _to_col_major(t):
    n, ch = t.shape
    return t.reshape(n // GRID_W, GRID_W, ch).swapaxes(0, 1).reshape(n, ch)


def _to_raster(t):
    n, ch = t.shape
    return t.reshape(GRID_W, n // GRID_W, ch).swapaxes(0, 1).reshape(n, ch)


def _scg_layout(sc):
    r = sc.shape[0]
    hps = HEADS_PER_STEP
    nhg = DN_HEADS // hps
    t = sc.reshape(r, 4, 32)[:, :, :2 * DN_HEADS].reshape(r, 4, 2, nhg, hps)
    t = jnp.stack([t[:, 0], t[:, 2], t[:, 3]], axis=1)
    t = t.transpose(2, 3, 0, 1, 4).reshape(2 * nhg, r, 3 * hps)
    return jnp.concatenate([t, jnp.zeros((2 * nhg, r, LANE - 3 * hps), F32)], axis=2)


def _scg_layout_bwd(dscg):
    r = dscg.shape[1]
    hps = HEADS_PER_STEP
    nhg = DN_HEADS // hps
    t = dscg[:, :, :3 * hps].reshape(2, nhg, r, 3, hps).transpose(2, 3, 0, 1, 4).reshape(r, 3, 2 * DN_HEADS)
    t = jnp.concatenate([t, jnp.zeros((r, 3, 32 - 2 * DN_HEADS), F32)], axis=2) if 2 * DN_HEADS < 32 else t
    z = jnp.zeros((r, 1, 32), F32)
    return jnp.concatenate([t[:, 0:1], z, t[:, 1:2], t[:, 2:3]], axis=1).reshape(r, LANE)


def _gather_cols(full8, width):
    return full8.transpose(1, 0, 2).reshape(full8.shape[1], NDEV * width)


def kernel(x, c, ctx, c_ctx, mod_w, mod_b, norm_w, ab_w_in, ab_qkv_conv, ab_a_log, ab_dt_bias, ab_dn_norm, ab_lru_conv_w, ab_lru_conv_b, ab_lru_w_r, ab_lru_b_r, ab_lru_w_i, ab_lru_b_i, ab_lru_lambda, ab_w_out, sc_w_in, sc_conv, sc_w_out, final_norm_w, loss_target, m_c_ctx, m_mod_w, m_mod_b, m_norm_w, m_ab_w_in, m_ab_qkv_conv, m_ab_a_log, m_ab_dt_bias, m_ab_dn_norm, m_ab_lru_conv_w, m_ab_lru_conv_b, m_ab_lru_w_r, m_ab_lru_b_r, m_ab_lru_w_i, m_ab_lru_b_i, m_ab_lru_lambda, m_ab_w_out, m_sc_w_in, m_sc_conv, m_sc_w_out, m_final_norm_w, v_c_ctx, v_mod_w, v_mod_b, v_norm_w, v_ab_w_in, v_ab_qkv_conv, v_ab_a_log, v_ab_dt_bias, v_ab_dn_norm, v_ab_lru_conv_w, v_ab_lru_conv_b, v_ab_lru_w_r, v_ab_lru_b_r, v_ab_lru_w_i, v_ab_lru_b_i, v_ab_lru_lambda, v_ab_w_out, v_sc_w_in, v_sc_conv, v_sc_w_out, v_final_norm_w):
    dd = _dims()
    d, dnw, lw, r = dd["d"], dd["dnw"], dd["lw"], dd["r"]
    off_lru, off_beta, ab_state, ab_in, nmain = dd["off_lru"], dd["off_beta"], dd["ab_state"], dd["ab_in"], dd["nmain"]
    hh2 = 2 * DN_HEADS
    me = _me()
    x = x[0]
    ctx = ctx[0]
    tgt = loss_target[0]

    small_sharded = [ab_qkv_conv[0], ab_lru_conv_w[0], ab_lru_b_r[0], ab_lru_b_i[0], ab_lru_lambda[0], sc_conv[0]]
    pk, offs0 = _pack([c] + small_sharded)
    (g0,) = _exchange([pk], "gather_small_in", scatter=False)
    parts = [_unpack(g0[j], offs0, [c.shape] + [a.shape for a in small_sharded]) for j in range(NDEV)]
    c_all = jnp.concatenate([p[0] for p in parts], axis=0)
    qkv_conv, lru_conv_w, lru_b_r, lru_b_i, lru_lam, sc_conv_f = [
        jnp.concatenate([p[i] for p in parts], axis=-1) for i in range(1, 7)]

    mcols = mod_w.shape[2]
    c16 = jnp.concatenate([c_all, c_ctx[None, :], jnp.zeros((7, d), F32)], axis=0)
    bias = lax.dynamic_slice_in_dim(mod_b, me * mcols, mcols, axis=1)[:, None, :]
    mod_loc = _mod_fwd(c16, mod_w, bias)
    (mod_g,) = _exchange([mod_loc], "gather_mod", scatter=False)
    mod_all = mod_g.transpose(1, 2, 0, 3).reshape(2, 16, NDEV * mcols)
    mine = lax.dynamic_slice_in_dim(mod_all, me, 1, axis=1)[:, 0]
    shift = [mine[l, None, 0:d] for l in range(2)]
    scale = [mine[l, None, d:2 * d] for l in range(2)]
    gate = [mine[l, None, 2 * d:3 * d] for l in range(2)]
    shift_c, scale_c = mod_all[0, 8, None, 0:d], mod_all[0, 8, None, d:2 * d]
    nw0, nw1 = norm_w[0][None, :], norm_w[1][None, :]

    (g_abin,) = _gather_two_level([ab_w_in[0].astype(BF16)], "gather_weights")
    w_full = _gather_cols(g_abin, ab_in // NDEV)
    w_main = jnp.concatenate([w_full[:, :off_beta], w_full[:, ab_state:]], axis=1)
    w_ba = jnp.concatenate([w_full[:, off_beta:ab_state], jnp.zeros((d, LANE - 2 * hh2), BF16)], axis=1)

    hn_c, rinv_c = _norm_fwd(ctx, nw0, shift_c, scale_c, "norm0_ctx")
    hn_x, rinv_x = _norm_fwd(x, nw0, shift[0], scale[0], "norm0_x")
    hn0 = jnp.concatenate([hn_c, hn_x], axis=0)
    proj, (g_about,) = _mm_nn(hn0, w_main[None], F32, "proj_main", rider=("gather", [ab_w_out[0].astype(BF16)]))
    w_about = g_about.reshape(dnw + lw, d)
    pba = _mm_nn(hn0, w_ba[None], F32, "proj_ba")

    convw8 = _pad_rows(qkv_conv, 8)
    qkv = _dn_pre_fwd(proj, convw8)
    zpad = jnp.zeros((32 - hh2,), F32)
    par = jnp.stack([jnp.concatenate([jnp.zeros((32,), F32), ab_dt_bias[0].reshape(-1), zpad, jnp.zeros((64,), F32)]),
                     jnp.concatenate([jnp.zeros((32,), F32), -jnp.exp(ab_a_log[0].reshape(-1)), zpad,
                                      jnp.zeros((64,), F32)])] + [jnp.zeros((LANE,), F32)] * 6)
    if hh2 < 32:
        pba_l = jnp.concatenate([pba[:, :hh2], jnp.zeros((r, 32 - hh2), F32), pba[:, hh2:2 * hh2],
                                 jnp.zeros((r, LANE - 32 - hh2), F32)], axis=1)
    else:
        pba_l = pba
    sc = _dn_gates_fwd(pba_l, par)
    scg = _scg_layout(sc)
    (o_dn, states), (g_scin,) = _delta_fwd(qkv, scg, rider=("gather", [sc_w_in[0].astype(BF16)]))

    xl = proj[:, off_lru:off_beta]
    xl_cat = jnp.concatenate([xl[:CTX_LEN], _to_col_major(xl[CTX_LEN:])], axis=0)
    lconv8 = jnp.concatenate([lru_conv_w, ab_lru_conv_b[0][None, :], jnp.zeros((3, lw), F32)], axis=0)
    xc = _lru_conv_fwd(xl_cat, lconv8)
    lvecs = jnp.stack([lru_b_r, lru_b_i, lru_lam] + [jnp.zeros_like(lru_lam)] * 5, axis=1)
    w_r, w_i = ab_lru_w_r[0], ab_lru_w_i[0]
    la_a, la_b = _lru_gates_fwd(xc, w_r, w_i, lvecs)
    h_dir = [_lru_scan_fwd(la_a, la_b, 0), _lru_scan_fwd(la_a, la_b, 1)]
    hsum = _to_raster(h_dir[0][CTX_LEN:] + h_dir[1][CTX_LEN:])

    dnn = ab_dn_norm[0][None, :]
    y0 = _mix_post_fwd(o_dn, hsum, proj, dnn)
    out0 = _mm_nn(y0, w_about[None], F32, "out0")

    x1, hn1, rinv1 = _resid_norm_fwd(x, out0, gate[0], nw1, shift[1], scale[1])
    p1, (g_scout,) = _mm_nn(hn1, g_scin, F32, "proj_sc", rider=("gather", [sc_w_out[0].astype(BF16)]))
    w_scout = g_scout.reshape(d, d)
    scw8 = _pad_rows(sc_conv_f, 8)
    y1 = _sc_mix_fwd(p1, scw8)
    out1 = _mm_nn(y1, w_scout[None], F32, "out1")

    fnw = final_norm_w[None, :]
    dx2, dout1, loss11, dfnw, dgate1 = _loss_head(x1, out1, gate[1], fnw, tgt)
    dy1 = _mm_nt(dout1, w_scout[None], F32, "dy1")
    gw_scout = _mm_tn(y1, dout1, BF16, "gw_scout")[0].reshape(NDEV, d // NDEV, d)
    dpb, dpc, dpx, dpg, dscw8 = _sc_mix_bwd(p1, scw8, dy1)
    dp1 = jnp.concatenate([dpb, dpc, dpx, dpg], axis=1)
    dhn1 = _mm_nt(dp1, g_scin, F32, "dhn1", tk=2048)
    gw_scin = _mm_tn(hn1, dp1, BF16, "gw_scin", oj=NDEV)
    dx1, dout0, dshift1, dscale1, dnw1, dgate0 = _norm_bwd(
        x1, rinv1, [dhn1], nw1, scale[1], "norm1_bwd", dx_in=dx2, resid=(out0, gate[0]))

    dy0 = _mm_nt(dout0, w_about[None], F32, "dy0")
    gw_about = _mm_tn(y0, dout0, BF16, "gw_about")[0].reshape(NDEV, (dnw + lw) // NDEV, d)
    do, dh_r, dgates, ddnn = _mix_post_bwd(o_dn, hsum, proj, dnn, dy0)

    early = [gw_about, gw_scin, gw_scout]
    sib_e = _sibling_swap(early, "swap_grads_early")
    parts_e = [_chip_partial(o_, s_, f"chip_partial_early{i}") for i, (o_, s_) in enumerate(zip(early, sib_e))]
    (dq, dk, dv, dscg), (s_about, s_scin, s_scout) = _delta_bwd(qkv, scg, states, do, rider=("chipx", parts_e))
    dsc = _scg_layout_bwd(dscg)
    dpba, dpar = _dn_gates_bwd(pba_l, par, dsc)
    dpq, dcw_q = _dn_pre_bwd(proj, convw8, dq, 0)
    dpk, dcw_k = _dn_pre_bwd(proj, convw8, dk, 1)
    dpv, dcw_v = _dn_pre_bwd(proj, convw8, dv, 2)

    dh_cat = jnp.concatenate([jnp.zeros((CTX_LEN, lw), F32), _to_col_major(dh_r)], axis=0)
    da0, db0 = _lru_scan_bwd(la_a, h_dir[0], dh_cat, 0)
    da1, db1 = _lru_scan_bwd(la_a, h_dir[1], dh_cat, 1)
    dxc, dw_r, dw_i, dlvecs = _lru_gates_bwd(xc, w_r, w_i, lvecs, jnp.stack([da0, da1]), jnp.stack([db0, db1]))
    dxl_cat, dlconv8 = _lru_conv_bwd(xl_cat, lconv8, dxc)
    dxl = jnp.concatenate([dxl_cat[:CTX_LEN], _to_raster(dxl_cat[CTX_LEN:])], axis=0).astype(BF16)

    dproj = jnp.concatenate([dpq, dpk, dpv, dxl, dgates], axis=1)
    if hh2 < 32:
        dpba_w = jnp.concatenate([dpba[:, :hh2], dpba[:, 32:32 + hh2], jnp.zeros((r, LANE - 2 * hh2), BF16)], axis=1)
    else:
        dpba_w = dpba
    gw_main = _mm_tn(hn0, dproj, BF16, "gw_main")[0]
    gw_ba = _mm_tn(hn0, dpba_w, BF16, "gw_ba")[0]
    gw_full = jnp.concatenate([gw_main[:, :off_beta], gw_ba[:, :2 * hh2], gw_main[:, off_beta:]], axis=1)
    gw_abin = gw_full.reshape(d, NDEV, ab_in // NDEV).transpose(1, 0, 2)
    (sib_l,) = _sibling_swap([gw_abin], "swap_grads_late")
    part_l = _chip_partial(gw_abin, sib_l, "chip_partial_late")
    dhn0_a, (s_abin,) = _mm_nt(dproj, w_main[None], F32, "dhn0_main", rider=("chipx", [part_l]))
    dhn0_b = _mm_nt(dpba_w, w_ba[None], F32, "dhn0_ba")

    nt = _div(SEQ, ROW_TILE // 2, 16)
    grad_x, dshift0, dscale0, dnw0_x = _norm_bwd(x, rinv_x, [dhn0_a, dhn0_b], nw0, scale[0], "norm0_bwd_x",
                                                 row_off=CTX_LEN // nt, dx_in=dx1)
    _, dshift_c, dscale_c, dnw0_c = _norm_bwd(ctx, rinv_c, [dhn0_a, dhn0_b], nw0, scale_c, "norm0_bwd_ctx")

    big = {}
    big["ab_w_in"] = _adamw_slots(ab_w_in[0], s_abin, m_ab_w_in[0], v_ab_w_in[0], "adamw_ab_w_in")
    big["ab_w_out"] = _adamw_slots(ab_w_out[0], s_about, m_ab_w_out[0], v_ab_w_out[0], "adamw_ab_w_out")
    big["sc_w_in"] = _adamw_slots(sc_w_in[0], s_scin, m_sc_w_in[0], v_sc_w_in[0], "adamw_sc_w_in")
    big["sc_w_out"] = _adamw_slots(sc_w_out[0], s_scout, m_sc_w_out[0], v_sc_w_out[0], "adamw_sc_w_out")

    dmod_own = jnp.stack([jnp.concatenate([dshift0, dscale0, dgate0], axis=1)[0],
                          jnp.concatenate([dshift1, dscale1, dgate1], axis=1)[0]])
    dmod_c = jnp.concatenate([dshift_c, dscale_c], axis=1)[0]
    g_qkv_conv = jnp.concatenate([dcw_q[:DN_CONV], dcw_k[:DN_CONV], dcw_v[:DN_CONV]], axis=1)
    small = [
        ("loss", loss11.reshape(1)),
        ("norm_w", jnp.concatenate([dnw0_x + dnw0_c, dnw1], axis=0)),
        ("dmod_own", dmod_own),
        ("dmod_c", dmod_c),
        ("qkv_conv", g_qkv_conv),
        ("a_log", dpar[1, 32:32 + hh2].reshape(2, DN_HEADS)),
        ("dt_bias", dpar[0, 32:32 + hh2].reshape(2, DN_HEADS)),
        ("dn_norm", ddnn[0]),
        ("lru_conv_w", dlconv8[:LRU_CONV]),
        ("lru_conv_b", dlconv8[LRU_CONV]),
        ("lru_w_r", dw_r),
        ("lru_w_i", dw_i),
        ("lru_b_r", dlvecs[:, 0]),
        ("lru_b_i", dlvecs[:, 1]),
        ("lru_lambda", dlvecs[:, 2]),
        ("sc_conv", dscw8[:SC_CONV]),
        ("final_norm_w", dfnw[0]),
    ]
    names = [n for n, _ in small]
    shapes = [a.shape for _, a in small]
    spk, soffs = _pack([a for _, a in small])
    (sg,) = _gather_two_level([spk], "gather_small_grads")
    tot = dict(zip(names, _unpack(_sum_slots(sg, "sum_small_grads"), soffs, shapes)))
    i_own = names.index("dmod_own")
    dmod_all = jnp.stack([_unpack(sg[j], soffs[i_own:i_own + 1], shapes[i_own:i_own + 1])[0] for j in range(NDEV)])

    dmc_full = jnp.concatenate([tot["dmod_c"], jnp.zeros((d,), F32)])
    rows9 = jnp.concatenate([dmod_all, jnp.stack([dmc_full, jnp.zeros((3 * d,), F32)])[None],
                             jnp.zeros((7, 2, 3 * d), F32)], axis=0)
    dm = lax.dynamic_slice_in_dim(rows9, me * mcols, mcols, axis=2).transpose(1, 0, 2)
    g_modw, dl_modw, nm_modw, nv_modw, gcc = _mod_bwd(c16, dm, mod_w, m_mod_w, v_mod_w)
    gcc_pk, _ = _pack([gcc[0, :, 0]])
    (gcc_g,) = _exchange([gcc_pk], "gather_c_ctx", scatter=False)
    gcc_sum = _sum_slots(gcc_g, "sum_c_ctx").reshape(-1)[:d]
    g_c_ctx_pre = gcc_sum

    grad_mod_b = tot["dmod_own"] + jnp.stack([dmc_full, jnp.zeros((3 * d,), F32)])

    def shard(a):
        wd = a.shape[-1] // NDEV
        return lax.dynamic_slice_in_dim(a, me * wd, wd, axis=a.ndim - 1)

    small_w = [
        ("c_ctx", c_ctx, None, m_c_ctx, v_c_ctx),
        ("mod_b", mod_b, grad_mod_b, m_mod_b, v_mod_b),
        ("norm_w", norm_w, tot["norm_w"], m_norm_w, v_norm_w),
        ("ab_qkv_conv", ab_qkv_conv, shard(tot["qkv_conv"])[None], m_ab_qkv_conv, v_ab_qkv_conv),
        ("ab_a_log", ab_a_log, tot["a_log"][None], m_ab_a_log, v_ab_a_log),
        ("ab_dt_bias", ab_dt_bias, tot["dt_bias"][None], m_ab_dt_bias, v_ab_dt_bias),
        ("ab_dn_norm", ab_dn_norm, tot["dn_norm"][None], m_ab_dn_norm, v_ab_dn_norm),
        ("ab_lru_conv_w", ab_lru_conv_w, shard(tot["lru_conv_w"])[None], m_ab_lru_conv_w, v_ab_lru_conv_w),
        ("ab_lru_conv_b", ab_lru_conv_b, tot["lru_conv_b"][None], m_ab_lru_conv_b, v_ab_lru_conv_b),
        ("ab_lru_w_r", ab_lru_w_r, tot["lru_w_r"][None], m_ab_lru_w_r, v_ab_lru_w_r),
        ("ab_lru_b_r", ab_lru_b_r, shard(tot["lru_b_r"])[None], m_ab_lru_b_r, v_ab_lru_b_r),
        ("ab_lru_w_i", ab_lru_w_i, tot["lru_w_i"][None], m_ab_lru_w_i, v_ab_lru_w_i),
        ("ab_lru_b_i", ab_lru_b_i, shard(tot["lru_b_i"])[None], m_ab_lru_b_i, v_ab_lru_b_i),
        ("ab_lru_lambda", ab_lru_lambda, shard(tot["lru_lambda"])[None], m_ab_lru_lambda, v_ab_lru_lambda),
        ("sc_conv", sc_conv, shard(tot["sc_conv"])[None], m_sc_conv, v_sc_conv),
        ("final_norm_w", final_norm_w, tot["final_norm_w"], m_final_norm_w, v_final_norm_w),
    ]
    sg_cc = jax.nn.sigmoid(c_ctx)
    g_c_ctx = g_c_ctx_pre * (sg_cc * (1.0 + c_ctx * (1.0 - sg_cc)))
    small_w[0] = ("c_ctx", c_ctx, g_c_ctx, m_c_ctx, v_c_ctx)
    wshapes = [w.shape for _, w, _, _, _ in small_w]
    wpk, woffs = _pack([w for _, w, _, _, _ in small_w])
    gpk, _ = _pack([g for _, _, g, _, _ in small_w])
    mpk, _ = _pack([m for _, _, _, m, _ in small_w])
    vpk, _ = _pack([v for _, _, _, _, v in small_w])
    dl_pk, nm_pk, nv_pk = _adamw_flat(wpk, gpk, mpk, vpk, "adamw_small")
    sm_g = {n: g for n, _, g, _, _ in small_w}
    sm_dl = dict(zip([n for n, *_ in small_w], _unpack(dl_pk, woffs, wshapes)))
    sm_nm = dict(zip([n for n, *_ in small_w], _unpack(nm_pk, woffs, wshapes)))
    sm_nv = dict(zip([n for n, *_ in small_w], _unpack(nv_pk, woffs, wshapes)))

    grads, deltas, new_m, new_v = {}, {}, {}, {}
    for n in sm_g:
        grads[n], deltas[n], new_m[n], new_v[n] = sm_g[n], sm_dl[n], sm_nm[n], sm_nv[n]
    grads["mod_w"], deltas["mod_w"], new_m["mod_w"], new_v["mod_w"] = g_modw, dl_modw, nm_modw, nv_modw
    for n in ("ab_w_in", "ab_w_out", "sc_w_in", "sc_w_out"):
        g, dl, m2, v2 = big[n]
        grads[n], deltas[n], new_m[n], new_v[n] = g[None], dl[None], m2[None], v2[None]

    order = ["c_ctx", "mod_w", "mod_b", "norm_w", "ab_w_in", "ab_qkv_conv", "ab_a_log", "ab_dt_bias", "ab_dn_norm",
             "ab_lru_conv_w", "ab_lru_conv_b", "ab_lru_w_r", "ab_lru_b_r", "ab_lru_w_i", "ab_lru_b_i", "ab_lru_lambda",
             "ab_w_out", "sc_w_in", "sc_conv", "sc_w_out", "final_norm_w"]
    loss = tot["loss"][0]
    return (loss, grad_x[None], *[grads[n] for n in order], *[deltas[n] for n in order],
            *[new_m[n] for n in order], *[new_v[n] for n in order])
```

```python
import functools

import jax
import jax.numpy as jnp
from jax import lax
from jax.experimental import pallas as pl
from jax.experimental.pallas import tpu as pltpu

F32 = jnp.float32
BF16 = jnp.bfloat16

NDEV = 8
D_MODEL = 4096
SEQ = 4096
CTX_LEN = 256
GRID_W = 64
EPS = 1e-6
DN_HEADS = 16
DN_HEAD_DIM = 128
DN_CONV = 4
DN_CHUNK = 64
LRU_WIDTH = 2048
LRU_BLOCKS = 16
LRU_CONV = 4
LRU_C = 8.0
SC_CONV = 3
ADAM_LR = 0.001
ADAM_B1 = 0.9
ADAM_B2 = 0.999
ADAM_EPS = 1e-08
ADAM_WD = 0.01
ADAM_STEP = 10

ROW_TILE = 256
HEADS_PER_STEP = 16
LANE = 128
VMEM_LIMIT = 56 * 1024 * 1024


def _dims():
    d = D_MODEL
    dnw = DN_HEADS * DN_HEAD_DIM
    lw = LRU_WIDTH
    off_lru = 3 * dnw
    off_beta = off_lru + lw
    ab_state = off_beta + 4 * DN_HEADS
    ab_in = ab_state + dnw + lw
    return dict(d=d, dnw=dnw, lw=lw, off_lru=off_lru, off_beta=off_beta, ab_state=ab_state, ab_in=ab_in,
                nmain=3 * dnw + lw + dnw + lw, r=CTX_LEN + SEQ, lbd=lw // LRU_BLOCKS)


def _div(n, pref, mult):
    best = None
    for t in range(mult, min(n, pref) + 1, mult):
        if n % t == 0:
            best = t
    return best if best is not None else n


def _cp(sem, vmem=VMEM_LIMIT):
    return pltpu.CompilerParams(dimension_semantics=sem, vmem_limit_bytes=vmem)


def _silu(z):
    return z * jax.nn.sigmoid(z)


def _dsilu(z):
    s = jax.nn.sigmoid(z)
    return s * (1.0 + z * (1.0 - s))


def _softplus(z):
    return jnp.maximum(z, 0.0) + jnp.log1p(jnp.exp(-jnp.abs(z)))


def _neg_expm1(z):
    series = -z * (1.0 + z * (0.5 + z * (1.0 / 6.0 + z * (1.0 / 24.0))))
    return jnp.where(z > -0.05, series, 1.0 - jnp.exp(z))


def _me():
    return 4 * lax.axis_index("x") + 2 * lax.axis_index("y") + lax.axis_index("c")


def _exchange(srcs, name, scatter):
    n = len(srcs)

    def body(*refs):
        src_refs, out_refs = refs[:n], refs[n:2 * n]
        send_sems, recv_sems, local_sems = refs[2 * n:]
        x, y, c = lax.axis_index("x"), lax.axis_index("y"), lax.axis_index("c")
        me = 4 * x + 2 * y + c
        copies = []
        for i in range(n):
            mine = src_refs[i].at[me] if scatter else src_refs[i]
            lc = pltpu.make_async_copy(mine, out_refs[i].at[me], local_sems.at[i])
            lc.start()
            copies.append(lc)
            for k in range(1, NDEV):
                px = 1 - x if (k >> 2) & 1 else x
                py = 1 - y if (k >> 1) & 1 else y
                pc = 1 - c if k & 1 else c
                peer = 4 * px + 2 * py + pc
                src = src_refs[i].at[peer] if scatter else src_refs[i]
                cp = pltpu.make_async_remote_copy(
                    src_ref=src, dst_ref=out_refs[i].at[me], send_sem=send_sems.at[i, k - 1],
                    recv_sem=recv_sems.at[i, k - 1], device_id=(px, py, pc), device_id_type=pl.DeviceIdType.MESH)
                cp.start()
                copies.append(cp)
        for cp in copies:
            cp.wait()

    any_spec = pl.BlockSpec(memory_space=pl.ANY)
    out_shape = tuple(jax.ShapeDtypeStruct(s.shape if scatter else (NDEV,) + s.shape, s.dtype) for s in srcs)
    outs = pl.pallas_call(
        body, name=name, out_shape=out_shape, in_specs=[any_spec] * n, out_specs=tuple([any_spec] * n),
        scratch_shapes=[pltpu.SemaphoreType.DMA((n, NDEV - 1)), pltpu.SemaphoreType.DMA((n, NDEV - 1)),
                        pltpu.SemaphoreType.DMA((n,))],
    )(*srcs)
    return list(outs)


def _flip(v, bit):
    return 1 - v if bit else v


def _gather_ops(src_refs, out_refs, send_sems, recv_sems, local_sems):
    n = len(src_refs)

    def setup():
        x, y, c = lax.axis_index("x"), lax.axis_index("y"), lax.axis_index("c")
        chips = [(_flip(x, k >> 1), _flip(y, k & 1)) for k in (1, 2, 3)]
        return x, y, c, chips

    def copy(i, sem, slot, to, src=None):
        return pltpu.make_async_remote_copy(
            src_ref=out_refs[i].at[slot] if src is None else src, dst_ref=out_refs[i].at[slot],
            send_sem=send_sems.at[i, sem], recv_sem=recv_sems.at[i, sem], device_id=to,
            device_id_type=pl.DeviceIdType.MESH)

    def own(i, me):
        return pltpu.make_async_copy(src_refs[i], out_refs[i].at[me], local_sems.at[i])

    def first(i, x, y, c, chips):
        me = 4 * x + 2 * y + c
        return [copy(i, 0, me, (x, y, 1 - c), src=src_refs[i])] + [
            copy(i, 1 + j, me, (px, py, c), src=src_refs[i]) for j, (px, py) in enumerate(chips)]

    def start():
        x, y, c, chips = setup()
        for i in range(n):
            own(i, 4 * x + 2 * y + c).start()
            for cp in first(i, x, y, c, chips):
                cp.start()

    def finish():
        x, y, c, chips = setup()
        here = (x, y, c)
        passed = []
        for i in range(n):
            for j, (px, py) in enumerate(chips):
                theirs = 4 * px + 2 * py + c
                copy(i, 1 + j, theirs, here).wait_recv()
                fwd = copy(i, 4 + j, theirs, (x, y, 1 - c))
                fwd.start()
                passed.append(fwd)
        for i in range(n):
            copy(i, 0, 4 * x + 2 * y + (1 - c), here).wait_recv()
            for j, (px, py) in enumerate(chips):
                copy(i, 4 + j, 4 * px + 2 * py + (1 - c), here).wait_recv()
        for i in range(n):
            for cp in first(i, x, y, c, chips):
                cp.wait_send()
        for cp in passed:
            cp.wait_send()
        for i in range(n):
            own(i, 4 * x + 2 * y + c).wait()

    return start, finish


def _chipx_ops(src_refs, out_refs, send_sems, recv_sems, local_sems):
    n = len(src_refs)

    def copies():
        x, y, c = lax.axis_index("x"), lax.axis_index("y"), lax.axis_index("c")
        mine = 2 * x + y
        res = []
        for i in range(n):
            res.append(pltpu.make_async_copy(src_refs[i].at[mine], out_refs[i].at[mine], local_sems.at[i]))
            for k in (1, 2, 3):
                px, py = _flip(x, k >> 1), _flip(y, k & 1)
                res.append(pltpu.make_async_remote_copy(
                    src_ref=src_refs[i].at[2 * px + py], dst_ref=out_refs[i].at[mine], send_sem=send_sems.at[i, k - 1],
                    recv_sem=recv_sems.at[i, k - 1], device_id=(px, py, c), device_id_type=pl.DeviceIdType.MESH))
        return res

    def start():
        for cp in copies():
            cp.start()

    def finish():
        for cp in copies():
            cp.wait()

    return start, finish


def _rider(kind, arrays):
    n = len(arrays)
    any_spec = pl.BlockSpec(memory_space=pl.ANY)
    if kind == "gather":
        out_shapes = [jax.ShapeDtypeStruct((NDEV,) + a.shape, a.dtype) for a in arrays]
        nsem, make = NDEV - 1, _gather_ops
    else:
        out_shapes = [jax.ShapeDtypeStruct(a.shape, a.dtype) for a in arrays]
        nsem, make = NDEV // 2 - 1, _chipx_ops
    scratch = [pltpu.SemaphoreType.DMA((n, nsem)), pltpu.SemaphoreType.DMA((n, nsem)), pltpu.SemaphoreType.DMA((n,))]
    return list(arrays), [any_spec] * n, out_shapes, [any_spec] * n, scratch, make


def _exchange_call(kind, arrays, name):
    n = len(arrays)
    ins, in_specs, out_shapes, out_specs, scratch, make = _rider(kind, arrays)

    def body(*refs):
        start, finish = make(refs[:n], refs[n:2 * n], *refs[2 * n:])
        start()
        finish()

    outs = pl.pallas_call(body, name=name, out_shape=tuple(out_shapes), in_specs=in_specs, out_specs=tuple(out_specs),
                          scratch_shapes=scratch)(*ins)
    return list(outs)


def _gather_two_level(srcs, name):
    return _exchange_call("gather", srcs, name)


def _call_with_rider(body, rider, *, name, grid, out_shape, in_specs, out_specs, scratch_shapes, compiler_params,
                     operands):
    if rider is None:
        outs = pl.pallas_call(body, name=name, grid=grid, out_shape=tuple(out_shape), in_specs=list(in_specs),
                              out_specs=tuple(out_specs), scratch_shapes=list(scratch_shapes),
                              compiler_params=compiler_params)(*operands)
        return list(outs), []
    r_ins, r_in_specs, r_out_shapes, r_out_specs, r_scratch, make = _rider(*rider)
    ni, no, ns, nr = len(in_specs), len(out_shape), len(scratch_shapes), len(r_ins)

    def riding(*refs):
        ins, rins = refs[:ni], refs[ni:ni + nr]
        outs, routs = refs[ni + nr:ni + nr + no], refs[ni + nr + no:ni + nr + no + nr]
        scr, rscr = refs[ni + 2 * nr + no:ni + 2 * nr + no + ns], refs[ni + 2 * nr + no + ns:]
        start, finish = make(rins, routs, *rscr)
        ids = [pl.program_id(a) for a in range(len(grid))]
        first, last = ids[0] == 0, ids[0] == grid[0] - 1
        for a in range(1, len(grid)):
            first = jnp.logical_and(first, ids[a] == 0)
            last = jnp.logical_and(last, ids[a] == grid[a] - 1)

        @pl.when(first)
        def _():
            start()

        body(*ins, *outs, *scr)

        @pl.when(last)
        def _():
            finish()

    outs = pl.pallas_call(
        riding, name=name, grid=grid, out_shape=tuple(out_shape) + tuple(r_out_shapes),
        in_specs=list(in_specs) + r_in_specs, out_specs=tuple(out_specs) + tuple(r_out_specs),
        scratch_shapes=list(scratch_shapes) + r_scratch, compiler_params=compiler_params)(*operands, *r_ins)
    return list(outs[:no]), list(outs[no:])


def _sibling_swap(srcs, name):
    n = len(srcs)
    nchip = NDEV // 2

    def body(*refs):
        src_refs, out_refs = refs[:n], refs[n:2 * n]
        send_sems, recv_sems = refs[2 * n:]
        x, y, c = lax.axis_index("x"), lax.axis_index("y"), lax.axis_index("c")
        copies = []
        for i in range(n):
            for q in range(nchip):
                cp = pltpu.make_async_remote_copy(
                    src_ref=src_refs[i].at[2 * q + (1 - c)], dst_ref=out_refs[i].at[q], send_sem=send_sems.at[i, q],
                    recv_sem=recv_sems.at[i, q], device_id=(x, y, 1 - c), device_id_type=pl.DeviceIdType.MESH)
                cp.start()
                copies.append(cp)
        for cp in copies:
            cp.wait()

    any_spec = pl.BlockSpec(memory_space=pl.ANY)
    out_shape = tuple(jax.ShapeDtypeStruct((nchip,) + s.shape[1:], s.dtype) for s in srcs)
    outs = pl.pallas_call(
        body, name=name, out_shape=out_shape, in_specs=[any_spec] * n, out_specs=tuple([any_spec] * n),
        scratch_shapes=[pltpu.SemaphoreType.DMA((n, nchip)), pltpu.SemaphoreType.DMA((n, nchip))],
    )(*srcs)
    return list(outs)


def _chip_partial(own, sib, name):
    _, rows, cols = own.shape
    tm = _div(rows, max(16, (128 * 4096) // cols), 16)

    def body(c_ref, o_ref, s_ref, p_ref):
        p_ref[0] = (o_ref[0].astype(F32) + s_ref[0].astype(F32)).astype(BF16)

    core = lax.axis_index("c").astype(jnp.int32).reshape(1)
    grid_spec = pltpu.PrefetchScalarGridSpec(
        num_scalar_prefetch=1, grid=(NDEV // 2, rows // tm),
        in_specs=[pl.BlockSpec((1, tm, cols), lambda q, i, c_ref: (2 * q + c_ref[0], i, 0)),
                  pl.BlockSpec((1, tm, cols), lambda q, i, c_ref: (q, i, 0))],
        out_specs=pl.BlockSpec((1, tm, cols), lambda q, i, c_ref: (q, i, 0)))
    return pl.pallas_call(
        body, name=name, grid_spec=grid_spec, out_shape=jax.ShapeDtypeStruct(sib.shape, BF16),
        compiler_params=_cp(("arbitrary", "arbitrary")))(core, own, sib)


def _chip_exchange(parts, name):
    return _exchange_call("chipx", parts, name)


def _sum_slots(g, name):
    _, rows, cols = g.shape
    tm = _div(rows, 1024, 8)

    def body(g_ref, o_ref):
        acc = g_ref[0]
        for j in range(1, NDEV):
            acc = acc + g_ref[j]
        o_ref[...] = acc

    return pl.pallas_call(
        body, name=name, grid=(rows // tm,), out_shape=jax.ShapeDtypeStruct((rows, cols), F32),
        in_specs=[pl.BlockSpec((NDEV, tm, cols), lambda i: (0, i, 0))],
        out_specs=pl.BlockSpec((tm, cols), lambda i: (i, 0)), compiler_params=_cp(("arbitrary",)))(g)


def _mm_nn(a, b3, out_dtype, name, tm=1152, tn=1024, rider=None):
    m_, k_ = a.shape
    j_, _, nj = b3.shape
    tm = _div(m_, tm, 16)
    tn = _div(nj, tn, LANE)
    per = nj // tn

    def body(a_ref, b_ref, o_ref):
        o_ref[...] = jnp.dot(a_ref[...], b_ref[0], preferred_element_type=F32).astype(o_ref.dtype)

    outs, rode = _call_with_rider(
        body, rider, name=name, grid=(j_ * per, m_ // tm),
        out_shape=(jax.ShapeDtypeStruct((m_, j_ * nj), out_dtype),),
        in_specs=[pl.BlockSpec((tm, k_), lambda n, m: (m, 0)),
                  pl.BlockSpec((1, k_, tn), lambda n, m: (n // per, 0, n % per))],
        out_specs=(pl.BlockSpec((tm, tn), lambda n, m: (m, n)),), scratch_shapes=[],
        compiler_params=_cp(("arbitrary", "arbitrary")), operands=(a, b3))
    return outs[0] if rider is None else (outs[0], rode)


def _col_groups(arrays, tile):
    starts, cnts, pos = [], [], 0
    for arr in arrays:
        cnt = arr.shape[1] // tile
        starts.append(pos)
        cnts.append(cnt)
        pos += cnt
    return starts, cnts, pos


def _mm_nt(a, b3, out_dtype, name, tm=1152, tn=1024, tk=2048, rider=None):
    a_list = list(a) if isinstance(a, (list, tuple)) else [a]
    na = len(a_list)
    m_ = a_list[0].shape[0]
    j_, n_, kj = b3.shape
    tm = _div(m_, tm, 16)
    tn = _div(n_, tn, LANE)
    width = kj
    for arr in a_list:
        width = _gcd(width, arr.shape[1])
    tk = _div(width, tk if na == 1 else tk // 2, LANE)
    perk = kj // tk
    starts, cnts, nk = _col_groups(a_list, tk)

    def body(*refs):
        a_refs, b_ref, o_ref, acc_ref = refs[:na], refs[na], refs[na + 1], refs[na + 2]
        k = pl.program_id(2)

        @pl.when(k == 0)
        def _():
            acc_ref[...] = jnp.zeros_like(acc_ref)

        for g in range(na):
            def add(g=g):
                acc_ref[...] += lax.dot_general(a_refs[g][...], b_ref[0], (((1,), (1,)), ((), ())),
                                                preferred_element_type=F32)
            if na == 1:
                add()
            else:
                pl.when(jnp.logical_and(k >= starts[g], k < starts[g] + cnts[g]))(add)

        @pl.when(k == nk - 1)
        def _():
            o_ref[...] = acc_ref[...].astype(o_ref.dtype)

    def a_spec(g):
        return pl.BlockSpec((tm, tk), lambda n, m, k: (m, jnp.clip(k - starts[g], 0, cnts[g] - 1)))

    outs, rode = _call_with_rider(
        body, rider, name=name, grid=(n_ // tn, m_ // tm, nk),
        out_shape=(jax.ShapeDtypeStruct((m_, n_), out_dtype),),
        in_specs=[a_spec(g) for g in range(na)] + [
            pl.BlockSpec((1, tn, tk), lambda n, m, k: (k // perk, n, k % perk))],
        out_specs=(pl.BlockSpec((tm, tn), lambda n, m, k: (m, n)),),
        scratch_shapes=[pltpu.VMEM((tm, tn), F32)],
        compiler_params=_cp(("arbitrary", "arbitrary", "arbitrary")), operands=(*a_list, b3))
    return outs[0] if rider is None else (outs[0], rode)


def _gcd(x, y):
    while y:
        x, y = y, x % y
    return x


def _mm_tn(a, b, out_dtype, name, oj=1, tm=1024, tn=512):
    b_list = list(b) if isinstance(b, (list, tuple)) else [b]
    nb = len(b_list)
    k_, m_ = a.shape
    n_ = sum(arr.shape[1] for arr in b_list)
    nj = n_ // oj
    width = nj
    for arr in b_list:
        width = _gcd(width, arr.shape[1])
    tm = _div(m_, tm if nb == 1 else tm // 2, LANE)
    tn = _div(width, tn, LANE)
    per = nj // tn
    starts, cnts, nn = _col_groups(b_list, tn)

    def body(*refs):
        a_ref, b_refs, o_ref = refs[0], refs[1:1 + nb], refs[1 + nb]
        n = pl.program_id(1)
        for g in range(nb):
            def put(g=g):
                o_ref[0] = lax.dot_general(a_ref[...], b_refs[g][...], (((0,), (0,)), ((), ())),
                                           preferred_element_type=F32).astype(o_ref.dtype)
            if nb == 1:
                put()
            else:
                pl.when(jnp.logical_and(n >= starts[g], n < starts[g] + cnts[g]))(put)

    def b_spec(g):
        return pl.BlockSpec((k_, tn), lambda m, n: (0, jnp.clip(n - starts[g], 0, cnts[g] - 1)))

    return pl.pallas_call(
        body, name=name, grid=(m_ // tm, nn), out_shape=jax.ShapeDtypeStruct((oj, m_, nj), out_dtype),
        in_specs=[pl.BlockSpec((k_, tm), lambda m, n: (0, m))] + [b_spec(g) for g in range(nb)],
        out_specs=pl.BlockSpec((1, tm, tn), lambda m, n: (n // per, m, n % per)),
        compiler_params=_cp(("arbitrary", "arbitrary")))(a, *b_list)


def _mod_fwd(c16, w, bias):
    _, d, cols = w.shape
    tn = _div(cols, 512, LANE)

    def body(c_ref, w_ref, b_ref, o_ref):
        sc = _silu(c_ref[...])
        o_ref[0] = jnp.dot(sc, w_ref[0], preferred_element_type=F32) + b_ref[0]

    return pl.pallas_call(
        body, name="mod_fwd", grid=(2, cols // tn), out_shape=jax.ShapeDtypeStruct((2, 16, cols), F32),
        in_specs=[pl.BlockSpec((16, d), lambda l, n: (0, 0)), pl.BlockSpec((1, d, tn), lambda l, n: (l, 0, n)),
                  pl.BlockSpec((1, 1, tn), lambda l, n: (l, 0, n))],
        out_specs=pl.BlockSpec((1, 16, tn), lambda l, n: (l, 0, n)),
        compiler_params=_cp(("arbitrary", "arbitrary")))(c16, w, bias)


def _adam(w, g, m, v):
    m2 = ADAM_B1 * m + (1.0 - ADAM_B1) * g
    v2 = ADAM_B2 * v + (1.0 - ADAM_B2) * (g * g)
    m_hat = m2 / (1.0 - ADAM_B1 ** ADAM_STEP)
    v_hat = v2 / (1.0 - ADAM_B2 ** ADAM_STEP)
    delta = -ADAM_LR * (m_hat / (jnp.sqrt(v_hat) + ADAM_EPS) + ADAM_WD * w)
    return delta, m2, v2


def _mod_bwd(c16, dm, w, m, v):
    _, d, cols = w.shape
    tm = _div(d, 256, 8)

    def body(c_ref, dm_ref, w_ref, m_ref, v_ref, g_ref, dl_ref, nm_ref, nv_ref, gc_ref):
        sc = _silu(c_ref[...])
        g = lax.dot_general(sc, dm_ref[0], (((0,), (0,)), ((), ())), preferred_element_type=F32,
                            precision=lax.Precision.HIGHEST)
        wv = w_ref[0]
        dl, m2, v2 = _adam(wv, g, m_ref[0], v_ref[0])
        g_ref[0] = g
        dl_ref[0] = dl
        nm_ref[0] = m2
        nv_ref[0] = v2
        gc_ref[0] = jnp.sum(wv * dm_ref[0, 8:9, :], axis=1, keepdims=True)

    big = pl.BlockSpec((1, tm, cols), lambda l, i: (l, i, 0))
    shp = jax.ShapeDtypeStruct(w.shape, F32)
    return pl.pallas_call(
        body, name="mod_bwd", grid=(2, d // tm),
        out_shape=(shp, shp, shp, shp, jax.ShapeDtypeStruct((2, d, 1), F32)),
        in_specs=[pl.BlockSpec((16, tm), lambda l, i: (0, i)), pl.BlockSpec((1, 16, cols), lambda l, i: (l, 0, 0)),
                  big, big, big],
        out_specs=(big, big, big, big, pl.BlockSpec((1, tm, 1), lambda l, i: (l, i, 0))),
        compiler_params=_cp(("arbitrary", "arbitrary")))(c16, dm, w, m, v)


def _row_spec(tm, d, off=0):
    return pl.BlockSpec((tm, d), lambda i: (i + off, 0))


def _vec_spec(d):
    return pl.BlockSpec((1, d), lambda i: (0, 0))


def _norm_fwd(x, nw, shift, scale, name):
    n, d = x.shape
    tm = _div(n, ROW_TILE, 16)

    def body(x_ref, nw_ref, sh_ref, sc_ref, hn_ref, r_ref):
        xv = x_ref[...]
        r = lax.rsqrt(jnp.mean(xv * xv, axis=1, keepdims=True) + EPS)
        hn_ref[...] = (xv * r * nw_ref[...] * (1.0 + sc_ref[...]) + sh_ref[...]).astype(BF16)
        r_ref[...] = r

    return pl.pallas_call(
        body, name=name, grid=(n // tm,),
        out_shape=(jax.ShapeDtypeStruct((n, d), BF16), jax.ShapeDtypeStruct((n, 1), F32)),
        in_specs=[_row_spec(tm, d), _vec_spec(d), _vec_spec(d), _vec_spec(d)],
        out_specs=(_row_spec(tm, d), _row_spec(tm, 1)), compiler_params=_cp(("arbitrary",)))(x, nw, shift, scale)


def _resid_norm_fwd(x, out0, gate, nw, shift, scale):
    n, d = x.shape
    tm = _div(n, ROW_TILE, 16)

    def body(x_ref, o_ref, g_ref, nw_ref, sh_ref, sc_ref, x1_ref, hn_ref, r_ref):
        xv = x_ref[...] + g_ref[...] * o_ref[...]
        r = lax.rsqrt(jnp.mean(xv * xv, axis=1, keepdims=True) + EPS)
        x1_ref[...] = xv
        hn_ref[...] = (xv * r * nw_ref[...] * (1.0 + sc_ref[...]) + sh_ref[...]).astype(BF16)
        r_ref[...] = r

    return pl.pallas_call(
        body, name="resid_norm1", grid=(n // tm,),
        out_shape=(jax.ShapeDtypeStruct((n, d), F32), jax.ShapeDtypeStruct((n, d), BF16),
                   jax.ShapeDtypeStruct((n, 1), F32)),
        in_specs=[_row_spec(tm, d), _row_spec(tm, d), _vec_spec(d), _vec_spec(d), _vec_spec(d), _vec_spec(d)],
        out_specs=(_row_spec(tm, d), _row_spec(tm, d), _row_spec(tm, 1)),
        compiler_params=_cp(("arbitrary",)))(x, out0, gate, nw, shift, scale)


def _loss_head(x1, out1, gate1, fnw, tgt):
    n, d = x1.shape
    tm = _div(n, ROW_TILE, 16)
    nsteps = n // tm

    def body(x_ref, o_ref, g_ref, w_ref, t_ref, dx_ref, do_ref, loss_ref, dw_ref, dg_ref, lacc_ref):
        i = pl.program_id(0)

        @pl.when(i == 0)
        def _():
            lacc_ref[...] = jnp.zeros_like(lacc_ref)
            dw_ref[...] = jnp.zeros_like(dw_ref)
            dg_ref[...] = jnp.zeros_like(dg_ref)

        ov = o_ref[...]
        gv = g_ref[...]
        wv = w_ref[...]
        xv = x_ref[...] + gv * ov
        r = lax.rsqrt(jnp.mean(xv * xv, axis=1, keepdims=True) + EPS)
        xh = xv * r
        diff = xh * wv - t_ref[...]
        lacc_ref[...] += jnp.sum(diff * diff, axis=0, keepdims=True)
        dy = diff * (1.0 / d)
        dw_ref[...] += jnp.sum(dy * xh, axis=0, keepdims=True)
        dxh = dy * wv
        dx = r * (dxh - xh * jnp.mean(xh * dxh, axis=1, keepdims=True))
        dx_ref[...] = dx
        dg_ref[...] += jnp.sum(dx * ov, axis=0, keepdims=True)
        do_ref[...] = (gv * dx).astype(BF16)

        @pl.when(i == nsteps - 1)
        def _():
            loss_ref[...] = (0.5 / d) * jnp.sum(lacc_ref[...], axis=1, keepdims=True)

    vec = jax.ShapeDtypeStruct((1, d), F32)
    return pl.pallas_call(
        body, name="loss_head", grid=(nsteps,),
        out_shape=(jax.ShapeDtypeStruct((n, d), F32), jax.ShapeDtypeStruct((n, d), BF16),
                   jax.ShapeDtypeStruct((1, 1), F32), vec, vec),
        in_specs=[_row_spec(tm, d), _row_spec(tm, d), _vec_spec(d), _vec_spec(d), _row_spec(tm, d)],
        out_specs=(_row_spec(tm, d), _row_spec(tm, d), pl.BlockSpec((1, 1), lambda i: (0, 0)), _vec_spec(d),
                   _vec_spec(d)),
        scratch_shapes=[pltpu.VMEM((1, d), F32)], compiler_params=_cp(("arbitrary",)))(x1, out1, gate1, fnw, tgt)


def _norm_bwd(x, rinv, dhns, nw, scale, name, row_off=0, dx_in=None, resid=None):
    n, d = x.shape
    tm = _div(n, ROW_TILE // 2, 16)
    nd = len(dhns)
    has_in = dx_in is not None
    has_res = resid is not None

    def body(*refs):
        it = iter(refs)
        x_ref, r_ref = next(it), next(it)
        dh_refs = [next(it) for _ in range(nd)]
        nw_ref, sc_ref = next(it), next(it)
        dxin_ref = next(it) if has_in else None
        o_ref, g_ref = (next(it), next(it)) if has_res else (None, None)
        dx_ref = next(it)
        do_ref = next(it) if has_res else None
        dsh_ref, dsc_ref, dnw_ref = next(it), next(it), next(it)
        dg_ref = next(it) if has_res else None
        i = pl.program_id(0)

        @pl.when(i == 0)
        def _():
            dsh_ref[...] = jnp.zeros_like(dsh_ref)
            dsc_ref[...] = jnp.zeros_like(dsc_ref)
            dnw_ref[...] = jnp.zeros_like(dnw_ref)
            if has_res:
                dg_ref[...] = jnp.zeros_like(dg_ref)

        dh = dh_refs[0][...]
        for rr in dh_refs[1:]:
            dh = dh + rr[...]
        r = r_ref[...]
        xh = x_ref[...] * r
        nwv = nw_ref[...]
        mod = 1.0 + sc_ref[...]
        dsh_ref[...] += jnp.sum(dh, axis=0, keepdims=True)
        dhx = dh * xh
        dsc_ref[...] += jnp.sum(dhx * nwv, axis=0, keepdims=True)
        dnw_ref[...] += jnp.sum(dhx * mod, axis=0, keepdims=True)
        dxh = dh * (nwv * mod)
        dx = r * (dxh - xh * jnp.mean(xh * dxh, axis=1, keepdims=True))
        if has_in:
            dx = dx + dxin_ref[...]
        dx_ref[...] = dx
        if has_res:
            dg_ref[...] += jnp.sum(dx * o_ref[...], axis=0, keepdims=True)
            do_ref[...] = (g_ref[...] * dx).astype(BF16)

    ins = [x, rinv] + list(dhns) + [nw, scale]
    in_specs = [_row_spec(tm, d), _row_spec(tm, 1)] + [_row_spec(tm, d, row_off)] * nd + [_vec_spec(d), _vec_spec(d)]
    if has_in:
        ins.append(dx_in)
        in_specs.append(_row_spec(tm, d))
    if has_res:
        ins += list(resid)
        in_specs += [_row_spec(tm, d), _vec_spec(d)]
    vec = jax.ShapeDtypeStruct((1, d), F32)
    out_shape = [jax.ShapeDtypeStruct((n, d), F32)]
    out_specs = [_row_spec(tm, d)]
    if has_res:
        out_shape.append(jax.ShapeDtypeStruct((n, d), BF16))
        out_specs.append(_row_spec(tm, d))
    out_shape += [vec, vec, vec]
    out_specs += [_vec_spec(d)] * 3
    if has_res:
        out_shape.append(vec)
        out_specs.append(_vec_spec(d))
    return pl.pallas_call(
        body, name=name, grid=(n // tm,), out_shape=tuple(out_shape), in_specs=in_specs, out_specs=tuple(out_specs),
        compiler_params=_cp(("arbitrary",)))(*ins)


def _seg_masks(rows, width, seg_bounds, offs):
    t = lax.broadcasted_iota(jnp.int32, (rows, width), 0)
    s0 = jnp.zeros_like(t)
    s1 = jnp.full_like(t, rows)
    for (a, b) in seg_bounds:
        inside = (t >= a) & (t < b)
        s0 = jnp.where(inside, a, s0)
        s1 = jnp.where(inside, b, s1)
    return {off: ((t + off >= s0) & (t + off < s1)) for off in offs}


def _shift_rows(v, off, valid):
    rows = v.shape[0]
    return jnp.where(valid, pltpu.roll(v, (-off) % rows, axis=0), 0.0)


def _conv_fwd_val(xv, wv, taps, masks):
    acc = None
    for tap, off in taps:
        xs = xv if off == 0 else _shift_rows(xv, off, masks[off])
        term = wv[tap:tap + 1, :] * xs
        acc = term if acc is None else acc + term
    return acc


def _conv_bwd_val(xv, dacc, wv, taps, masks):
    dx = None
    dws = []
    for tap, off in taps:
        xs = xv if off == 0 else _shift_rows(xv, off, masks[off])
        dws.append(jnp.sum(dacc * xs, axis=0, keepdims=True))
        ds = dacc if off == 0 else _shift_rows(dacc, -off, masks[-off])
        term = wv[tap:tap + 1, :] * ds
        dx = term if dx is None else dx + term
    return dx, dws


def _taps(k):
    left = k // 2
    return [(j, j - left) for j in range(k)]


def _seq_bounds():
    return [(0, CTX_LEN), (CTX_LEN, CTX_LEN + SEQ)]


def _dn_pre_fwd(proj, convw):
    dd = _dims()
    r = dd["r"]
    nstrip = 3 * DN_HEADS
    taps = _taps(DN_CONV)
    offs = [o for _, o in taps if o != 0]

    def body(x_ref, w_ref, o_ref):
        kind = pl.program_id(0) // DN_HEADS
        masks = _seg_masks(r, DN_HEAD_DIM, _seq_bounds(), offs)
        y = _silu(_conv_fwd_val(x_ref[...], w_ref[...], taps, masks))
        inv = lax.rsqrt(jnp.sum(y * y, axis=1, keepdims=True) + EPS)
        scale = jnp.where(kind == 0, inv * (DN_HEAD_DIM ** -0.5), jnp.where(kind == 1, inv, 1.0))
        o_ref[...] = y * scale

    return pl.pallas_call(
        body, name="dn_pre_fwd", grid=(nstrip,), out_shape=jax.ShapeDtypeStruct((r, 3 * dd["dnw"]), F32),
        in_specs=[pl.BlockSpec((r, DN_HEAD_DIM), lambda j: (0, j)), pl.BlockSpec((8, DN_HEAD_DIM), lambda j: (0, j))],
        out_specs=pl.BlockSpec((r, DN_HEAD_DIM), lambda j: (0, j)), compiler_params=_cp(("arbitrary",)))(proj, convw)


def _dn_pre_bwd(proj, convw, dqkv, kind):
    dd = _dims()
    r = dd["r"]
    taps = _taps(DN_CONV)
    offs = sorted({o for _, o in taps if o != 0} | {-o for _, o in taps if o != 0})
    cscale = DN_HEAD_DIM ** -0.5 if kind == 0 else 1.0

    def body(x_ref, w_ref, d_ref, dx_ref, dw_ref):
        masks = _seg_masks(r, DN_HEAD_DIM, _seq_bounds(), offs)
        xv = x_ref[...]
        wv = w_ref[...]
        acc = _conv_fwd_val(xv, wv, taps, masks)
        dout = d_ref[0] + d_ref[1]
        if kind == 2:
            dy = dout
        else:
            y = _silu(acc)
            inv = lax.rsqrt(jnp.sum(y * y, axis=1, keepdims=True) + EPS)
            dy = cscale * inv * (dout - y * (inv * inv) * jnp.sum(dout * y, axis=1, keepdims=True))
        dacc = dy * _dsilu(acc)
        dx, dws = _conv_bwd_val(xv, dacc, wv, taps, masks)
        dx_ref[...] = dx.astype(BF16)
        dw_ref[...] = jnp.concatenate(dws + [jnp.zeros((8 - len(dws), DN_HEAD_DIM), F32)], axis=0)

    base = kind * DN_HEADS
    return pl.pallas_call(
        body, name=f"dn_pre_bwd{kind}", grid=(DN_HEADS,),
        out_shape=(jax.ShapeDtypeStruct((r, dd["dnw"]), BF16), jax.ShapeDtypeStruct((8, dd["dnw"]), F32)),
        in_specs=[pl.BlockSpec((r, DN_HEAD_DIM), lambda j: (0, base + j)),
                  pl.BlockSpec((8, DN_HEAD_DIM), lambda j: (0, base + j)),
                  pl.BlockSpec((2, r, DN_HEAD_DIM), lambda j: (0, 0, j))],
        out_specs=(pl.BlockSpec((r, DN_HEAD_DIM), lambda j: (0, j)), pl.BlockSpec((8, DN_HEAD_DIM), lambda j: (0, j))),
        compiler_params=_cp(("arbitrary",)))(proj, convw, dqkv)


def _lru_conv_fwd(xl, convw):
    r, lw = xl.shape
    taps = _taps(LRU_CONV)
    offs = [o for _, o in taps if o != 0]

    def body(x_ref, w_ref, o_ref):
        masks = _seg_masks(r, LANE, _seq_bounds(), offs)
        wv = w_ref[...]
        o_ref[...] = _conv_fwd_val(x_ref[...], wv, taps, masks) + wv[4:5, :]

    return pl.pallas_call(
        body, name="lru_conv_fwd", grid=(lw // LANE,), out_shape=jax.ShapeDtypeStruct((r, lw), F32),
        in_specs=[pl.BlockSpec((r, LANE), lambda j: (0, j)), pl.BlockSpec((8, LANE), lambda j: (0, j))],
        out_specs=pl.BlockSpec((r, LANE), lambda j: (0, j)), compiler_params=_cp(("arbitrary",)))(xl, convw)


def _lru_conv_bwd(xl, convw, dxc):
    r, lw = xl.shape
    taps = _taps(LRU_CONV)
    offs = sorted({o for _, o in taps if o != 0} | {-o for _, o in taps if o != 0})

    def body(x_ref, w_ref, d_ref, dx_ref, dw_ref):
        masks = _seg_masks(r, LANE, _seq_bounds(), offs)
        dacc = d_ref[0] + d_ref[1]
        dx, dws = _conv_bwd_val(x_ref[...], dacc, w_ref[...], taps, masks)
        dx_ref[...] = dx
        db = jnp.sum(dacc, axis=0, keepdims=True)
        dw_ref[...] = jnp.concatenate(dws + [db, jnp.zeros((3, LANE), F32)], axis=0)

    return pl.pallas_call(
        body, name="lru_conv_bwd", grid=(lw // LANE,),
        out_shape=(jax.ShapeDtypeStruct((r, lw), F32), jax.ShapeDtypeStruct((8, lw), F32)),
        in_specs=[pl.BlockSpec((r, LANE), lambda j: (0, j)), pl.BlockSpec((8, LANE), lambda j: (0, j)),
                  pl.BlockSpec((2, r, LANE), lambda j: (0, 0, j))],
        out_specs=(pl.BlockSpec((r, LANE), lambda j: (0, j)), pl.BlockSpec((8, LANE), lambda j: (0, j))),
        compiler_params=_cp(("arbitrary",)))(xl, convw, dxc)


def _tri(rev_lane, n, m):
    r = lax.broadcasted_iota(jnp.int32, (n, n), 0)
    c = lax.broadcasted_iota(jnp.int32, (n, n), 1)
    return (r >= c).astype(F32), (r <= c).astype(F32)


def _hdot(a, b):
    return jnp.dot(a, b, preferred_element_type=F32, precision=lax.Precision.HIGHEST)


def _gate_lanes():
    lane = lax.broadcasted_iota(jnp.int32, (DN_CHUNK, LANE), 1)
    group = lane // 32
    rev = (lane % 32) >= DN_HEADS if DN_HEADS * 2 <= 32 else None
    return group, rev


def _dn_gates_fwd(pba, par):
    r = pba.shape[0]
    nch = r // DN_CHUNK
    hh = 2 * DN_HEADS

    def body(x_ref, p_ref, o_ref):
        lo, up = _tri(None, DN_CHUNK, DN_CHUNK)
        ones = jnp.ones((DN_CHUNK, DN_CHUNK), F32)
        lane = lax.broadcasted_iota(jnp.int32, (DN_CHUNK, LANE), 1)
        group = lane // 32
        rev = (lane % 32) >= DN_HEADS
        dtb = p_ref[0:1, :]
        nea = p_ref[1:2, :]

        def step(i, carry):
            rows = pl.ds(pl.multiple_of(i * DN_CHUNK, DN_CHUNK), DN_CHUNK)
            xv = x_ref[rows, :]
            beta = jax.nn.sigmoid(xv)
            g = jnp.where(group == 1, nea * _softplus(xv + dtb), 0.0)
            gc = jnp.where(rev, _hdot(up, g), _hdot(lo, g))
            gt = _hdot(ones, g)
            out = jnp.where(group == 0, beta, jnp.where(group == 1, g, 0.0))
            out = out + jnp.where(group == 2, pltpu.roll(gc, 32, axis=1), 0.0)
            out = out + jnp.where(group == 3, pltpu.roll(gt, 64, axis=1), 0.0)
            o_ref[rows, :] = out
            return carry

        lax.fori_loop(0, nch, step, 0)

    return pl.pallas_call(body, name="dn_gates_fwd", out_shape=jax.ShapeDtypeStruct((r, LANE), F32),
                          compiler_params=_cp(None))(pba, par)


def _dn_gates_bwd(pba, par, dsc):
    r = pba.shape[0]
    nch = r // DN_CHUNK

    def body(x_ref, p_ref, d_ref, dx_ref, dp_ref, acc_ref):
        lo, up = _tri(None, DN_CHUNK, DN_CHUNK)
        ones = jnp.ones((DN_CHUNK, DN_CHUNK), F32)
        lane = lax.broadcasted_iota(jnp.int32, (DN_CHUNK, LANE), 1)
        group = lane // 32
        rev = (lane % 32) >= DN_HEADS
        dtb = p_ref[0:1, :]
        nea = p_ref[1:2, :]
        acc_ref[...] = jnp.zeros_like(acc_ref)

        def step(i, carry):
            rows = pl.ds(pl.multiple_of(i * DN_CHUNK, DN_CHUNK), DN_CHUNK)
            xv = x_ref[rows, :]
            dv = d_ref[rows, :]
            beta = jax.nn.sigmoid(xv)
            z = xv + dtb
            sp = _softplus(z)
            dgc = jnp.where(group == 1, pltpu.roll(dv, LANE - 32, axis=1), 0.0)
            dgt = jnp.where(group == 1, pltpu.roll(dv, LANE - 64, axis=1), 0.0)
            dg = jnp.where(rev, _hdot(lo, dgc), _hdot(up, dgc)) + _hdot(ones, dgt)
            dg = dg + jnp.where(group == 1, dv, 0.0)
            dal = dg * nea * jax.nn.sigmoid(z)
            dpb = dv * beta * (1.0 - beta)
            dx_ref[rows, :] = jnp.where(group == 0, dpb, jnp.where(group == 1, dal, 0.0)).astype(BF16)
            acc_ref[0:1, :] += jnp.sum(jnp.where(group == 1, dal, 0.0), axis=0, keepdims=True)
            acc_ref[1:2, :] += jnp.sum(jnp.where(group == 1, dg * nea * sp, 0.0), axis=0, keepdims=True)
            return carry

        lax.fori_loop(0, nch, step, 0)
        dp_ref[...] = acc_ref[...]

    return pl.pallas_call(
        body, name="dn_gates_bwd",
        out_shape=(jax.ShapeDtypeStruct((r, LANE), BF16), jax.ShapeDtypeStruct((8, LANE), F32)),
        scratch_shapes=[pltpu.VMEM((8, LANE), F32)], compiler_params=_cp(None))(pba, par, dsc)


_B_NN = (((2,), (1,)), ((0,), (0,)))
_B_NT = (((2,), (2,)), ((0,), (0,)))
_B_TN = (((1,), (1,)), ((0,), (0,)))


def _bdot(a, b, dims, exact=False):
    return lax.dot_general(a, b, dims, preferred_element_type=F32,
                           precision=lax.Precision.HIGH if exact else None)


@jax.custom_vjp
def _inv_unit_tri(l):
    n = l.shape[-1]
    ri = lax.broadcasted_iota(jnp.int32, (n, n), 0)
    ci = lax.broadcasted_iota(jnp.int32, (n, n), 1)
    x = jnp.broadcast_to((ri == ci).astype(F32), l.shape)
    inner = 1
    for size in (4, 16, n):
        so, si = size.bit_length() - 1, inner.bit_length() - 1
        same_outer = (ri >> so) == (ci >> so)
        same_inner = (ri >> si) == (ci >> si)
        part = jnp.where(same_outer, jnp.where(same_inner, 0.0, l), 0.0)
        m = part if inner == 1 else _bdot(x, part, _B_NN, exact=True)
        y = x
        for _ in range(size // inner - 1):
            y = x - _bdot(m, y, _B_NN, exact=True)
        x = y
        inner = size
    return x


def _inv_fwd(l):
    a = _inv_unit_tri(l)
    return a, a


def _inv_bwd(a, da):
    t = _bdot(a, da, _B_TN, exact=True)
    return (-_bdot(t, a, _B_NT, exact=True),)


_inv_unit_tri.defvjp(_inv_fwd, _inv_bwd)


def _delta_chunk(q4, k4, v4, sc, s4, rev):
    hps = HEADS_PER_STEP
    c = DN_CHUNK
    hd = DN_HEAD_DIM
    ri = lax.broadcasted_iota(jnp.int32, (c, c), 0)
    ci = lax.broadcasted_iota(jnp.int32, (c, c), 1)
    ahead = (ri - ci) * jnp.where(rev, -1, 1)
    incl = ahead >= 0
    strict = ahead > 0
    sct = sc.T

    def heads(x):
        return jnp.concatenate([x[None, :, j * hd:(j + 1) * hd] for j in range(hps)], axis=0)

    def col(off):
        return jnp.concatenate([sc[None, :, off + j:off + j + 1] for j in range(hps)], axis=0)

    def row(off):
        return jnp.concatenate([sct[None, off + j:off + j + 1, :] for j in range(hps)], axis=0)

    q, k, v = heads(q4), heads(k4), heads(v4)
    beta_c, gc_c, gt_c = col(0), col(hps), col(2 * hps)
    beta_r, gc_r = row(0), row(hps)
    dmat = jnp.where(incl, jnp.exp(jnp.where(incl, gc_c - gc_r, 0.0)), 0.0)
    kk = _bdot(k, k, _B_NT, exact=True)
    qk = _bdot(q, k, _B_NT)
    a = _inv_unit_tri(jnp.where(strict, beta_c * kk * dmat, 0.0))
    w = _bdot(a * (beta_r * jnp.exp(gc_r)), k, _B_NN, exact=True)
    u = _bdot(a * beta_r, v, _B_NN, exact=True)
    u2 = u - _bdot(w, s4, _B_NN)
    o = jnp.exp(gc_c) * _bdot(q, s4, _B_NN) + _bdot(qk * dmat, u2, _B_NN)
    ke = k * jnp.exp(gt_c - gc_c)
    s_new = jnp.exp(gt_c[:, 0:1, :]) * s4 + _bdot(ke, u2, _B_TN)
    return jnp.concatenate([o[j] for j in range(hps)], axis=1), s_new


def _chunk_index(d, n, nctx, ntot):
    rev_idx = jnp.where(n < nctx, nctx - 1 - n, ntot - 1 - (n - nctx))
    return jnp.where(d == 0, n, rev_idx)


def _delta_fwd(qkv, scg, rider=None):
    dd = _dims()
    r, dnw = dd["r"], dd["dnw"]
    hps = HEADS_PER_STEP
    nhg = DN_HEADS // hps
    bw = hps * DN_HEAD_DIM
    nctx, ntot = CTX_LEN // DN_CHUNK, r // DN_CHUNK

    def body(q_ref, k_ref, v_ref, sc_ref, o_ref, ss_ref, s_ref):
        d, n = pl.program_id(0), pl.program_id(2)

        @pl.when(n == 0)
        def _():
            s_ref[...] = jnp.zeros_like(s_ref)

        s4 = s_ref[...]
        ss_ref[0, 0] = s4
        o4, s4n = _delta_chunk(q_ref[...], k_ref[...], v_ref[...], sc_ref[0], s4, d == 1)
        o_ref[0] = o4
        s_ref[...] = s4n

    def rows(d, g, n):
        return _chunk_index(d, n, nctx, ntot)

    return _call_with_rider(
        body, rider, name="delta_fwd", grid=(2, nhg, ntot),
        out_shape=(jax.ShapeDtypeStruct((2, r, dnw), F32),
                   jax.ShapeDtypeStruct((2, ntot, DN_HEADS, DN_HEAD_DIM, DN_HEAD_DIM), F32)),
        in_specs=[pl.BlockSpec((DN_CHUNK, bw), lambda d, g, n: (rows(d, g, n), g)),
                  pl.BlockSpec((DN_CHUNK, bw), lambda d, g, n: (rows(d, g, n), nhg + g)),
                  pl.BlockSpec((DN_CHUNK, bw), lambda d, g, n: (rows(d, g, n), 2 * nhg + g)),
                  pl.BlockSpec((1, DN_CHUNK, LANE), lambda d, g, n: (d * nhg + g, rows(d, g, n), 0))],
        out_specs=(pl.BlockSpec((1, DN_CHUNK, bw), lambda d, g, n: (d, rows(d, g, n), g)),
                   pl.BlockSpec((1, 1, hps, DN_HEAD_DIM, DN_HEAD_DIM), lambda d, g, n: (d, n, g, 0, 0))),
        scratch_shapes=[pltpu.VMEM((hps, DN_HEAD_DIM, DN_HEAD_DIM), F32)],
        compiler_params=_cp(("arbitrary", "arbitrary", "arbitrary")), operands=(qkv, qkv, qkv, scg))


def _delta_bwd(qkv, scg, states, do, rider=None):
    dd = _dims()
    r, dnw = dd["r"], dd["dnw"]
    hps = HEADS_PER_STEP
    nhg = DN_HEADS // hps
    bw = hps * DN_HEAD_DIM
    nctx, ntot = CTX_LEN // DN_CHUNK, r // DN_CHUNK

    def body(q_ref, k_ref, v_ref, sc_ref, ss_ref, do_ref, dq_ref, dk_ref, dv_ref, dsc_ref, ds_ref):
        d, n2 = pl.program_id(0), pl.program_id(2)

        @pl.when(n2 == 0)
        def _():
            ds_ref[...] = jnp.zeros_like(ds_ref)

        fn = functools.partial(_delta_chunk, rev=(d == 1))
        _, vjp = jax.vjp(fn, q_ref[...], k_ref[...], v_ref[...], sc_ref[0], ss_ref[0, 0])
        dq, dk, dv, dsc, ds = vjp((do_ref[...], ds_ref[...]))
        dq_ref[0] = dq
        dk_ref[0] = dk
        dv_ref[0] = dv
        dsc_ref[0] = dsc
        ds_ref[...] = ds

    def rows(d, g, n2):
        return _chunk_index(d, ntot - 1 - n2, nctx, ntot)

    blk = lambda col: pl.BlockSpec((DN_CHUNK, bw), lambda d, g, n2: (rows(d, g, n2), col(g)))
    oblk = pl.BlockSpec((1, DN_CHUNK, bw), lambda d, g, n2: (d, rows(d, g, n2), g))
    scblk = pl.BlockSpec((1, DN_CHUNK, LANE), lambda d, g, n2: (d * nhg + g, rows(d, g, n2), 0))
    big = jax.ShapeDtypeStruct((2, r, dnw), F32)
    return _call_with_rider(
        body, rider, name="delta_bwd", grid=(2, nhg, ntot),
        out_shape=(big, big, big, jax.ShapeDtypeStruct(scg.shape, F32)),
        in_specs=[blk(lambda g: g), blk(lambda g: nhg + g), blk(lambda g: 2 * nhg + g), scblk,
                  pl.BlockSpec((1, 1, hps, DN_HEAD_DIM, DN_HEAD_DIM), lambda d, g, n2: (d, ntot - 1 - n2, g, 0, 0)),
                  blk(lambda g: g)],
        out_specs=(oblk, oblk, oblk, scblk),
        scratch_shapes=[pltpu.VMEM((hps, DN_HEAD_DIM, DN_HEAD_DIM), F32)],
        compiler_params=_cp(("arbitrary", "arbitrary", "arbitrary")), operands=(qkv, qkv, qkv, scg, states, do))


def _lru_gate_vals(xn, wr, wi, br, bi, lam):
    r = jax.nn.sigmoid(jnp.dot(xn, wr, preferred_element_type=F32) + br)
    i = jax.nn.sigmoid(jnp.dot(xn, wi, preferred_element_type=F32) + bi)
    sp = _softplus(-lam)
    la = -LRU_C * r * sp
    a = jnp.exp(la)
    mm = jnp.sqrt(_neg_expm1(2.0 * la))
    return r, i, sp, la, a, mm


def _lru_gates_fwd(xc, w_r, w_i, vecs):
    r, lw = xc.shape
    bd = lw // LRU_BLOCKS
    tm = _div(r, ROW_TILE, 8)

    def body(x_ref, wr_ref, wi_ref, v_ref, a_ref, b_ref):
        for n in range(LRU_BLOCKS):
            sl = slice(n * bd, (n + 1) * bd)
            xn = x_ref[:, sl]
            _, i, _, _, a, mm = _lru_gate_vals(xn, wr_ref[0, n], wi_ref[0, n], v_ref[0, 0:1, sl], v_ref[0, 1:2, sl],
                                               v_ref[0, 2:3, sl])
            a_ref[0, :, sl] = a
            b_ref[0, :, sl] = mm * (i * xn)

    wspec = pl.BlockSpec((1, LRU_BLOCKS, bd, bd), lambda d, i: (d, 0, 0, 0))
    out = jax.ShapeDtypeStruct((2, r, lw), F32)
    ospec = pl.BlockSpec((1, tm, lw), lambda d, i: (d, i, 0))
    return pl.pallas_call(
        body, name="lru_gates_fwd", grid=(2, r // tm), out_shape=(out, out),
        in_specs=[pl.BlockSpec((tm, lw), lambda d, i: (i, 0)), wspec, wspec,
                  pl.BlockSpec((1, 8, lw), lambda d, i: (d, 0, 0))],
        out_specs=(ospec, ospec), compiler_params=_cp(("arbitrary", "arbitrary")))(xc, w_r, w_i, vecs)


def _lru_gates_bwd(xc, w_r, w_i, vecs, da, db):
    r, lw = xc.shape
    bd = lw // LRU_BLOCKS
    tm = _div(r, ROW_TILE, 8)

    def body(x_ref, wr_ref, wi_ref, v_ref, da_ref, db_ref, dx_ref, dwr_ref, dwi_ref, dv_ref):
        @pl.when(pl.program_id(1) == 0)
        def _():
            dwr_ref[...] = jnp.zeros_like(dwr_ref)
            dwi_ref[...] = jnp.zeros_like(dwi_ref)
            dv_ref[...] = jnp.zeros_like(dv_ref)

        for n in range(LRU_BLOCKS):
            sl = slice(n * bd, (n + 1) * bd)
            xn = x_ref[:, sl]
            wr, wi = wr_ref[0, n], wi_ref[0, n]
            lam = v_ref[0, 2:3, sl]
            rr, ii, sp, la, a, mm = _lru_gate_vals(xn, wr, wi, v_ref[0, 0:1, sl], v_ref[0, 1:2, sl], lam)
            dav, dbv = da_ref[0, :, sl], db_ref[0, :, sl]
            dmm = dbv * (ii * xn)
            dla = dav * a - dmm * (a * a) / jnp.maximum(mm, 1e-30)
            di = dbv * mm * xn
            dxn = dbv * mm * ii
            dr = dla * (-LRU_C * sp)
            dsp = jnp.sum(dla * (-LRU_C * rr), axis=0, keepdims=True)
            dpr = dr * rr * (1.0 - rr)
            dpi = di * ii * (1.0 - ii)
            dxn = dxn + lax.dot_general(dpr, wr, (((1,), (1,)), ((), ())), preferred_element_type=F32)
            dxn = dxn + lax.dot_general(dpi, wi, (((1,), (1,)), ((), ())), preferred_element_type=F32)
            dx_ref[0, :, sl] = dxn
            dwr_ref[0, n] += lax.dot_general(xn, dpr, (((0,), (0,)), ((), ())), preferred_element_type=F32)
            dwi_ref[0, n] += lax.dot_general(xn, dpi, (((0,), (0,)), ((), ())), preferred_element_type=F32)
            dv_ref[0, 0:1, sl] += jnp.sum(dpr, axis=0, keepdims=True)
            dv_ref[0, 1:2, sl] += jnp.sum(dpi, axis=0, keepdims=True)
            dv_ref[0, 2:3, sl] += dsp * (-jax.nn.sigmoid(-lam))

    wspec = pl.BlockSpec((1, LRU_BLOCKS, bd, bd), lambda d, i: (d, 0, 0, 0))
    vspec = pl.BlockSpec((1, 8, lw), lambda d, i: (d, 0, 0))
    big = pl.BlockSpec((1, tm, lw), lambda d, i: (d, i, 0))
    return pl.pallas_call(
        body, name="lru_gates_bwd", grid=(2, r // tm),
        out_shape=(jax.ShapeDtypeStruct((2, r, lw), F32), jax.ShapeDtypeStruct(w_r.shape, F32),
                   jax.ShapeDtypeStruct(w_i.shape, F32), jax.ShapeDtypeStruct((2, 8, lw), F32)),
        in_specs=[pl.BlockSpec((tm, lw), lambda d, i: (i, 0)), wspec, wspec, vspec, big, big],
        out_specs=(big, wspec, wspec, vspec), compiler_params=_cp(("arbitrary", "arbitrary")))(xc, w_r, w_i, vecs, da, db)


def _scan_block(d, i, nctx, ntot):
    return _chunk_index(d, i, nctx, ntot)


def _lru_scan_fwd(a, b):
    _, r, lw = a.shape
    rb = _div(CTX_LEN, ROW_TILE, 8)
    nctx, ntot, ntile = CTX_LEN // rb, r // rb, rb // 8

    def body(a_ref, b_ref, h_ref, hp_ref, c_ref):
        d, i = pl.program_id(0), pl.program_id(1)

        @pl.when(i == 0)
        def _():
            c_ref[...] = jnp.zeros_like(c_ref)

        def sweep(rev):
            order = list(range(7, -1, -1)) if rev else list(range(8))

            def step(s, carry):
                t0 = pl.multiple_of((ntile - 1 - s if rev else s) * 8, 8)
                at = a_ref[0, pl.ds(t0, 8), :]
                bt = b_ref[0, pl.ds(t0, 8), :]
                rows, prevs = [None] * 8, [None] * 8
                h = carry
                for j in order:
                    prevs[j] = h
                    h = at[j:j + 1, :] * h + bt[j:j + 1, :]
                    rows[j] = h
                h_ref[0, pl.ds(t0, 8), :] = jnp.concatenate(rows, axis=0)
                hp_ref[0, pl.ds(t0, 8), :] = jnp.concatenate(prevs, axis=0)
                return h

            c_ref[...] = lax.fori_loop(0, ntile, step, c_ref[...])

        @pl.when(d == 0)
        def _():
            sweep(False)

        @pl.when(d == 1)
        def _():
            sweep(True)

    blk = pl.BlockSpec((1, rb, lw), lambda d, i: (d, _scan_block(d, i, nctx, ntot), 0))
    out = jax.ShapeDtypeStruct((2, r, lw), F32)
    return pl.pallas_call(
        body, name="lru_scan_fwd", grid=(2, ntot), out_shape=(out, out), in_specs=[blk, blk], out_specs=(blk, blk),
        scratch_shapes=[pltpu.VMEM((1, lw), F32)], compiler_params=_cp(("arbitrary", "arbitrary")))(a, b)


def _lru_scan_bwd(a, hprev, dh):
    _, r, lw = a.shape
    rb = _div(CTX_LEN, ROW_TILE, 8)
    nctx, ntot, ntile = CTX_LEN // rb, r // rb, rb // 8

    def body(a_ref, hp_ref, dh_ref, da_ref, db_ref, c_ref):
        d, i = pl.program_id(0), pl.program_id(1)

        @pl.when(i == 0)
        def _():
            c_ref[...] = jnp.zeros_like(c_ref)

        def sweep(rev):
            order = list(range(8)) if rev else list(range(7, -1, -1))

            def step(s, carry):
                t0 = pl.multiple_of((s if rev else ntile - 1 - s) * 8, 8)
                at = a_ref[0, pl.ds(t0, 8), :]
                hpt = hp_ref[0, pl.ds(t0, 8), :]
                dht = dh_ref[pl.ds(t0, 8), :]
                das, dbs = [None] * 8, [None] * 8
                c = carry
                for j in order:
                    lam = dht[j:j + 1, :] + c
                    das[j] = lam * hpt[j:j + 1, :]
                    dbs[j] = lam
                    c = at[j:j + 1, :] * lam
                da_ref[0, pl.ds(t0, 8), :] = jnp.concatenate(das, axis=0)
                db_ref[0, pl.ds(t0, 8), :] = jnp.concatenate(dbs, axis=0)
                return c

            c_ref[...] = lax.fori_loop(0, ntile, step, c_ref[...])

        @pl.when(d == 0)
        def _():
            sweep(False)

        @pl.when(d == 1)
        def _():
            sweep(True)

    def rows(d, i):
        return _scan_block(d, ntot - 1 - i, nctx, ntot)

    blk = pl.BlockSpec((1, rb, lw), lambda d, i: (d, rows(d, i), 0))
    out = jax.ShapeDtypeStruct((2, r, lw), F32)
    return pl.pallas_call(
        body, name="lru_scan_bwd", grid=(2, ntot), out_shape=(out, out),
        in_specs=[blk, blk, pl.BlockSpec((rb, lw), lambda d, i: (rows(d, i), 0))], out_specs=(blk, blk),
        scratch_shapes=[pltpu.VMEM((1, lw), F32)], compiler_params=_cp(("arbitrary", "arbitrary")))(a, hprev, dh)


def _mix_post_fwd(o, hsum, proj, dnw_vec):
    dd = _dims()
    dnw, lw, nmain = dd["dnw"], dd["lw"], dd["nmain"]
    tm = _div(SEQ, ROW_TILE, 16)
    coff = CTX_LEN // tm
    gcol = (3 * dnw + lw) // dnw
    lcol = (3 * dnw + lw + dnw) // lw
    hd = DN_HEAD_DIM

    def body(o_ref, h_ref, g_ref, lg_ref, w_ref, y_ref):
        wv = w_ref[...]
        for hh in range(DN_HEADS):
            sl = slice(hh * hd, (hh + 1) * hd)
            os_ = o_ref[0, :, sl] + o_ref[1, :, sl]
            rr = lax.rsqrt(jnp.mean(os_ * os_, axis=1, keepdims=True) + EPS)
            y_ref[:, sl] = (os_ * rr * wv * _silu(g_ref[:, sl])).astype(BF16)
        y_ref[:, dnw:] = (h_ref[...] * _silu(lg_ref[...])).astype(BF16)

    return pl.pallas_call(
        body, name="mix_post_fwd", grid=(SEQ // tm,), out_shape=jax.ShapeDtypeStruct((SEQ, dnw + lw), BF16),
        in_specs=[pl.BlockSpec((2, tm, dnw), lambda i: (0, i + coff, 0)), pl.BlockSpec((tm, lw), lambda i: (i, 0)),
                  pl.BlockSpec((tm, dnw), lambda i: (i + coff, gcol)), pl.BlockSpec((tm, lw), lambda i: (i + coff, lcol)),
                  pl.BlockSpec((1, hd), lambda i: (0, 0))],
        out_specs=pl.BlockSpec((tm, dnw + lw), lambda i: (i, 0)), compiler_params=_cp(("arbitrary",)))(
            o, hsum, proj, proj, dnw_vec)


def _mix_post_bwd(o, hsum, proj, dnw_vec, dy):
    dd = _dims()
    dnw, lw, r = dd["dnw"], dd["lw"], dd["r"]
    tm = _div(SEQ, ROW_TILE, 16)
    coff = CTX_LEN // tm
    gcol = (3 * dnw + lw) // dnw
    lcol = (3 * dnw + lw + dnw) // lw
    hd = DN_HEAD_DIM

    def body(o_ref, h_ref, g_ref, lg_ref, w_ref, dy_ref, do_ref, dh_ref, dg_ref, dw_ref):
        i = pl.program_id(0)

        @pl.when(i == 0)
        def _():
            dw_ref[...] = jnp.zeros_like(dw_ref)

        @pl.when(i < coff)
        def _():
            do_ref[...] = jnp.zeros_like(do_ref)
            dg_ref[...] = jnp.zeros_like(dg_ref)

        @pl.when(i >= coff)
        def _():
            wv = w_ref[...]
            dwacc = jnp.zeros((1, hd), F32)
            for hh in range(DN_HEADS):
                sl = slice(hh * hd, (hh + 1) * hd)
                os_ = o_ref[0, :, sl] + o_ref[1, :, sl]
                rr = lax.rsqrt(jnp.mean(os_ * os_, axis=1, keepdims=True) + EPS)
                nn = os_ * rr
                gv = g_ref[:, sl]
                dyv = dy_ref[:, sl]
                don = dyv * _silu(gv)
                dg_ref[:, sl] = (dyv * nn * wv * _dsilu(gv)).astype(BF16)
                dwacc = dwacc + jnp.sum(don * nn, axis=0, keepdims=True)
                dn_ = don * wv
                do_ref[:, sl] = rr * (dn_ - nn * jnp.mean(nn * dn_, axis=1, keepdims=True))
            dw_ref[...] += dwacc
            lgv = lg_ref[...]
            dyl = dy_ref[:, dnw:]
            dh_ref[...] = dyl * _silu(lgv)
            dg_ref[:, dnw:] = (dyl * h_ref[...] * _dsilu(lgv)).astype(BF16)

    lat = lambda i: jnp.maximum(i - coff, 0)
    return pl.pallas_call(
        body, name="mix_post_bwd", grid=(r // tm,),
        out_shape=(jax.ShapeDtypeStruct((r, dnw), F32), jax.ShapeDtypeStruct((SEQ, lw), F32),
                   jax.ShapeDtypeStruct((r, dnw + lw), BF16), jax.ShapeDtypeStruct((1, hd), F32)),
        in_specs=[pl.BlockSpec((2, tm, dnw), lambda i: (0, i, 0)), pl.BlockSpec((tm, lw), lambda i: (lat(i), 0)),
                  pl.BlockSpec((tm, dnw), lambda i: (i, gcol)), pl.BlockSpec((tm, lw), lambda i: (i, lcol)),
                  pl.BlockSpec((1, hd), lambda i: (0, 0)), pl.BlockSpec((tm, dnw + lw), lambda i: (lat(i), 0))],
        out_specs=(pl.BlockSpec((tm, dnw), lambda i: (i, 0)), pl.BlockSpec((tm, lw), lambda i: (lat(i), 0)),
                   pl.BlockSpec((tm, dnw + lw), lambda i: (i, 0)), pl.BlockSpec((1, hd), lambda i: (0, 0))),
        compiler_params=_cp(("arbitrary",)))(o, hsum, proj, proj, dnw_vec, dy)


def _sc_masks(tm, cw):
    t = lax.broadcasted_iota(jnp.int32, (tm, cw), 0) % GRID_W
    return {-1: t != 0, 1: t != GRID_W - 1}


def _sc_mix_fwd(p, convw):
    s, w4 = p.shape
    w = w4 // 4
    tm = _div(s, 512, GRID_W if GRID_W % 16 == 0 else 16 * GRID_W)
    cw = _div(w, 512, LANE)
    nc = w // cw
    taps = _taps(SC_CONV)

    def body(b_ref, c_ref, x_ref, g_ref, w_ref, y_ref):
        masks = _sc_masks(tm, cw)
        u = c_ref[...] * x_ref[...]
        z = _conv_fwd_val(u, w_ref[...], taps, masks)
        y_ref[...] = (b_ref[...] * z * _silu(g_ref[...])).astype(BF16)

    blk = lambda k: pl.BlockSpec((tm, cw), lambda c, i: (i, k * nc + c))
    return pl.pallas_call(
        body, name="sc_mix_fwd", grid=(nc, s // tm), out_shape=jax.ShapeDtypeStruct((s, w), BF16),
        in_specs=[blk(0), blk(1), blk(2), blk(3), pl.BlockSpec((8, cw), lambda c, i: (0, c))],
        out_specs=pl.BlockSpec((tm, cw), lambda c, i: (i, c)),
        compiler_params=_cp(("arbitrary", "arbitrary")))(p, p, p, p, convw)


def _sc_mix_bwd(p, convw, dy):
    s, w4 = p.shape
    w = w4 // 4
    tm = _div(s, 512, GRID_W if GRID_W % 16 == 0 else 16 * GRID_W)
    cw = _div(w, 512, LANE)
    nc = w // cw
    taps = _taps(SC_CONV)

    def body(b_ref, c_ref, x_ref, g_ref, w_ref, dy_ref, db_ref, dc_ref, dx_ref, dg_ref, dw_ref):
        @pl.when(pl.program_id(1) == 0)
        def _():
            dw_ref[...] = jnp.zeros_like(dw_ref)

        masks = _sc_masks(tm, cw)
        wv = w_ref[...]
        bv, cv, xv, gv, dyv = b_ref[...], c_ref[...], x_ref[...], g_ref[...], dy_ref[...]
        u = cv * xv
        z = _conv_fwd_val(u, wv, taps, masks)
        sg = _silu(gv)
        db_ref[...] = (dyv * z * sg).astype(BF16)
        dg_ref[...] = (dyv * bv * z * _dsilu(gv)).astype(BF16)
        dz = dyv * bv * sg
        du, dws = _conv_bwd_val(u, dz, wv, taps, masks)
        dc_ref[...] = (du * xv).astype(BF16)
        dx_ref[...] = (du * cv).astype(BF16)
        dw_ref[...] += jnp.concatenate(dws + [jnp.zeros((8 - len(dws), cw), F32)], axis=0)

    blk = lambda k: pl.BlockSpec((tm, cw), lambda c, i: (i, k * nc + c))
    oblk = pl.BlockSpec((tm, cw), lambda c, i: (i, c))
    wblk = pl.BlockSpec((8, cw), lambda c, i: (0, c))
    out = jax.ShapeDtypeStruct((s, w), BF16)
    return pl.pallas_call(
        body, name="sc_mix_bwd", grid=(nc, s // tm),
        out_shape=(out, out, out, out, jax.ShapeDtypeStruct((8, w), F32)),
        in_specs=[blk(0), blk(1), blk(2), blk(3), wblk, oblk],
        out_specs=(oblk, oblk, oblk, oblk, wblk),
        compiler_params=_cp(("arbitrary", "arbitrary")))(p, p, p, p, convw, dy)


def _adamw_slots(w, slots, m, v, name):
    rows, cols = w.shape
    nslot = slots.shape[0]
    tm = _div(rows, max(16, (64 * 4096) // cols), 16)

    def body(w_ref, s_ref, m_ref, v_ref, g_ref, dl_ref, nm_ref, nv_ref):
        g = s_ref[0].astype(F32)
        for j in range(1, nslot):
            g = g + s_ref[j].astype(F32)
        dl, m2, v2 = _adam(w_ref[...], g, m_ref[...], v_ref[...])
        g_ref[...] = g
        dl_ref[...] = dl
        nm_ref[...] = m2
        nv_ref[...] = v2

    blk = pl.BlockSpec((tm, cols), lambda i: (i, 0))
    out = jax.ShapeDtypeStruct((rows, cols), F32)
    return pl.pallas_call(
        body, name=name, grid=(rows // tm,), out_shape=(out, out, out, out),
        in_specs=[blk, pl.BlockSpec((nslot, tm, cols), lambda i: (0, i, 0)), blk, blk],
        out_specs=(blk, blk, blk, blk), compiler_params=_cp(("arbitrary",)))(w, slots, m, v)


def _adamw_flat(w, g, m, v, name):
    rows, cols = w.shape
    tm = _div(rows, 1024, 8)

    def body(w_ref, g_ref, m_ref, v_ref, dl_ref, nm_ref, nv_ref):
        dl, m2, v2 = _adam(w_ref[...], g_ref[...], m_ref[...], v_ref[...])
        dl_ref[...] = dl
        nm_ref[...] = m2
        nv_ref[...] = v2

    blk = pl.BlockSpec((tm, cols), lambda i: (i, 0))
    out = jax.ShapeDtypeStruct((rows, cols), F32)
    return pl.pallas_call(
        body, name=name, grid=(rows // tm,), out_shape=(out, out, out), in_specs=[blk] * 4, out_specs=(blk, blk, blk),
        compiler_params=_cp(("arbitrary",)))(w, g, m, v)


def _pack(arrs):
    flat = [a.reshape(-1).astype(F32) for a in arrs]
    sizes = [f.shape[0] for f in flat]
    total = sum(sizes)
    padded = -(-total // 1024) * 1024
    if padded > total:
        flat.append(jnp.zeros((padded - total,), F32))
    offs, o = [], 0
    for s_ in sizes:
        offs.append(o)
        o += s_
    return jnp.concatenate(flat).reshape(padded // LANE, LANE), offs


def _unpack(flat2d, offs, shapes):
    flat = flat2d.reshape(-1)
    out = []
    for o, shp in zip(offs, shapes):
        n = 1
        for s_ in shp:
            n *= s_
        out.append(flat[o:o + n].reshape(shp))
    return out


def _pad_rows(a, rows):
    return jnp.concatenate([a, jnp.zeros((rows - a.shape[0],) + a.shape[1:], a.dtype)], axis=0)


def _to_col_major(t):
    n, ch = t.shape
    return t.reshape(n // GRID_W, GRID_W, ch).swapaxes(0, 1).reshape(n, ch)


def _to_raster(t):
    n, ch = t.shape
    return t.reshape(GRID_W, n // GRID_W, ch).swapaxes(0, 1).reshape(n, ch)


def _scg_layout(sc):
    r = sc.shape[0]
    hps = HEADS_PER_STEP
    nhg = DN_HEADS // hps
    t = sc.reshape(r, 4, 32)[:, :, :2 * DN_HEADS].reshape(r, 4, 2, nhg, hps)
    t = jnp.stack([t[:, 0], t[:, 2], t[:, 3]], axis=1)
    t = t.transpose(2, 3, 0, 1, 4).reshape(2 * nhg, r, 3 * hps)
    return jnp.concatenate([t, jnp.zeros((2 * nhg, r, LANE - 3 * hps), F32)], axis=2)


def _scg_layout_bwd(dscg):
    r = dscg.shape[1]
    hps = HEADS_PER_STEP
    nhg = DN_HEADS // hps
    t = dscg[:, :, :3 * hps].reshape(2, nhg, r, 3, hps).transpose(2, 3, 0, 1, 4).reshape(r, 3, 2 * DN_HEADS)
    t = jnp.concatenate([t, jnp.zeros((r, 3, 32 - 2 * DN_HEADS), F32)], axis=2) if 2 * DN_HEADS < 32 else t
    z = jnp.zeros((r, 1, 32), F32)
    return jnp.concatenate([t[:, 0:1], z, t[:, 1:2], t[:, 2:3]], axis=1).reshape(r, LANE)


def _gather_cols(full8, width):
    return full8.transpose(1, 0, 2).reshape(full8.shape[1], NDEV * width)


def kernel(x, c, ctx, c_ctx, mod_w, mod_b, norm_w, ab_w_in, ab_qkv_conv, ab_a_log, ab_dt_bias, ab_dn_norm, ab_lru_conv_w, ab_lru_conv_b, ab_lru_w_r, ab_lru_b_r, ab_lru_w_i, ab_lru_b_i, ab_lru_lambda, ab_w_out, sc_w_in, sc_conv, sc_w_out, final_norm_w, loss_target, m_c_ctx, m_mod_w, m_mod_b, m_norm_w, m_ab_w_in, m_ab_qkv_conv, m_ab_a_log, m_ab_dt_bias, m_ab_dn_norm, m_ab_lru_conv_w, m_ab_lru_conv_b, m_ab_lru_w_r, m_ab_lru_b_r, m_ab_lru_w_i, m_ab_lru_b_i, m_ab_lru_lambda, m_ab_w_out, m_sc_w_in, m_sc_conv, m_sc_w_out, m_final_norm_w, v_c_ctx, v_mod_w, v_mod_b, v_norm_w, v_ab_w_in, v_ab_qkv_conv, v_ab_a_log, v_ab_dt_bias, v_ab_dn_norm, v_ab_lru_conv_w, v_ab_lru_conv_b, v_ab_lru_w_r, v_ab_lru_b_r, v_ab_lru_w_i, v_ab_lru_b_i, v_ab_lru_lambda, v_ab_w_out, v_sc_w_in, v_sc_conv, v_sc_w_out, v_final_norm_w):
    dd = _dims()
    d, dnw, lw, r = dd["d"], dd["dnw"], dd["lw"], dd["r"]
    off_lru, off_beta, ab_state, ab_in, nmain = dd["off_lru"], dd["off_beta"], dd["ab_state"], dd["ab_in"], dd["nmain"]
    hh2 = 2 * DN_HEADS
    me = _me()
    x = x[0]
    ctx = ctx[0]
    tgt = loss_target[0]

    small_sharded = [ab_qkv_conv[0], ab_lru_conv_w[0], ab_lru_b_r[0], ab_lru_b_i[0], ab_lru_lambda[0], sc_conv[0]]
    pk, offs0 = _pack([c] + small_sharded)
    (g0,) = _exchange([pk], "gather_small_in", scatter=False)
    parts = [_unpack(g0[j], offs0, [c.shape] + [a.shape for a in small_sharded]) for j in range(NDEV)]
    c_all = jnp.concatenate([p[0] for p in parts], axis=0)
    qkv_conv, lru_conv_w, lru_b_r, lru_b_i, lru_lam, sc_conv_f = [
        jnp.concatenate([p[i] for p in parts], axis=-1) for i in range(1, 7)]

    mcols = mod_w.shape[2]
    c16 = jnp.concatenate([c_all, c_ctx[None, :], jnp.zeros((7, d), F32)], axis=0)
    bias = lax.dynamic_slice_in_dim(mod_b, me * mcols, mcols, axis=1)[:, None, :]
    mod_loc = _mod_fwd(c16, mod_w, bias)
    (mod_g,) = _exchange([mod_loc], "gather_mod", scatter=False)
    mod_all = mod_g.transpose(1, 2, 0, 3).reshape(2, 16, NDEV * mcols)
    mine = lax.dynamic_slice_in_dim(mod_all, me, 1, axis=1)[:, 0]
    shift = [mine[l, None, 0:d] for l in range(2)]
    scale = [mine[l, None, d:2 * d] for l in range(2)]
    gate = [mine[l, None, 2 * d:3 * d] for l in range(2)]
    shift_c, scale_c = mod_all[0, 8, None, 0:d], mod_all[0, 8, None, d:2 * d]
    nw0, nw1 = norm_w[0][None, :], norm_w[1][None, :]

    (g_abin,) = _gather_two_level([ab_w_in[0].astype(BF16)], "gather_weights")
    w_full = _gather_cols(g_abin, ab_in // NDEV)
    w_main = jnp.concatenate([w_full[:, :off_beta], w_full[:, ab_state:]], axis=1)
    w_ba = jnp.concatenate([w_full[:, off_beta:ab_state], jnp.zeros((d, LANE - 2 * hh2), BF16)], axis=1)

    hn_c, rinv_c = _norm_fwd(ctx, nw0, shift_c, scale_c, "norm0_ctx")
    hn_x, rinv_x = _norm_fwd(x, nw0, shift[0], scale[0], "norm0_x")
    hn0 = jnp.concatenate([hn_c, hn_x], axis=0)
    proj, (g_about,) = _mm_nn(hn0, w_main[None], F32, "proj_main", rider=("gather", [ab_w_out[0].astype(BF16)]))
    w_about = g_about.reshape(dnw + lw, d)
    pba = _mm_nn(hn0, w_ba[None], F32, "proj_ba")

    convw8 = _pad_rows(qkv_conv, 8)
    qkv = _dn_pre_fwd(proj, convw8)
    zpad = jnp.zeros((32 - hh2,), F32)
    par = jnp.stack([jnp.concatenate([jnp.zeros((32,), F32), ab_dt_bias[0].reshape(-1), zpad, jnp.zeros((64,), F32)]),
                     jnp.concatenate([jnp.zeros((32,), F32), -jnp.exp(ab_a_log[0].reshape(-1)), zpad,
                                      jnp.zeros((64,), F32)])] + [jnp.zeros((LANE,), F32)] * 6)
    if hh2 < 32:
        pba_l = jnp.concatenate([pba[:, :hh2], jnp.zeros((r, 32 - hh2), F32), pba[:, hh2:2 * hh2],
                                 jnp.zeros((r, LANE - 32 - hh2), F32)], axis=1)
    else:
        pba_l = pba
    sc = _dn_gates_fwd(pba_l, par)
    scg = _scg_layout(sc)
    (o_dn, states), (g_scin,) = _delta_fwd(qkv, scg, rider=("gather", [sc_w_in[0].astype(BF16)]))

    xl = proj[:, off_lru:off_beta]
    xl_cat = jnp.concatenate([xl[:CTX_LEN], _to_col_major(xl[CTX_LEN:])], axis=0)
    lconv8 = jnp.concatenate([lru_conv_w, ab_lru_conv_b[0][None, :], jnp.zeros((3, lw), F32)], axis=0)
    xc = _lru_conv_fwd(xl_cat, lconv8)
    lvecs = jnp.stack([lru_b_r, lru_b_i, lru_lam] + [jnp.zeros_like(lru_lam)] * 5, axis=1)
    w_r, w_i = ab_lru_w_r[0], ab_lru_w_i[0]
    la_a, la_b = _lru_gates_fwd(xc, w_r, w_i, lvecs)
    h_dir, h_prev = _lru_scan_fwd(la_a, la_b)
    hsum = _to_raster(h_dir[0, CTX_LEN:] + h_dir[1, CTX_LEN:])

    dnn = ab_dn_norm[0][None, :]
    y0 = _mix_post_fwd(o_dn, hsum, proj, dnn)
    out0 = _mm_nn(y0, w_about[None], F32, "out0")

    x1, hn1, rinv1 = _resid_norm_fwd(x, out0, gate[0], nw1, shift[1], scale[1])
    p1, (g_scout,) = _mm_nn(hn1, g_scin, F32, "proj_sc", rider=("gather", [sc_w_out[0].astype(BF16)]))
    w_scout = g_scout.reshape(d, d)
    scw8 = _pad_rows(sc_conv_f, 8)
    y1 = _sc_mix_fwd(p1, scw8)
    out1 = _mm_nn(y1, w_scout[None], F32, "out1")

    fnw = final_norm_w[None, :]
    dx2, dout1, loss11, dfnw, dgate1 = _loss_head(x1, out1, gate[1], fnw, tgt)
    dy1 = _mm_nt(dout1, w_scout[None], F32, "dy1")
    gw_scout = _mm_tn(y1, dout1, BF16, "gw_scout")[0].reshape(NDEV, d // NDEV, d)
    dpb, dpc, dpx, dpg, dscw8 = _sc_mix_bwd(p1, scw8, dy1)
    dp1 = [dpb, dpc, dpx, dpg]
    dhn1 = _mm_nt(dp1, g_scin, F32, "dhn1", tk=2048)
    gw_scin = _mm_tn(hn1, dp1, BF16, "gw_scin", oj=NDEV)
    dx1, dout0, dshift1, dscale1, dnw1, dgate0 = _norm_bwd(
        x1, rinv1, [dhn1], nw1, scale[1], "norm1_bwd", dx_in=dx2, resid=(out0, gate[0]))

    dy0 = _mm_nt(dout0, w_about[None], F32, "dy0")
    gw_about = _mm_tn(y0, dout0, BF16, "gw_about")[0].reshape(NDEV, (dnw + lw) // NDEV, d)
    do, dh_r, dgates, ddnn = _mix_post_bwd(o_dn, hsum, proj, dnn, dy0)

    early = [gw_about, gw_scin, gw_scout]
    sib_e = _sibling_swap(early, "swap_grads_early")
    parts_e = [_chip_partial(o_, s_, f"chip_partial_early{i}") for i, (o_, s_) in enumerate(zip(early, sib_e))]
    (dq, dk, dv, dscg), (s_about, s_scin, s_scout) = _delta_bwd(qkv, scg, states, do, rider=("chipx", parts_e))
    dsc = _scg_layout_bwd(dscg)
    dpba, dpar = _dn_gates_bwd(pba_l, par, dsc)
    dpq, dcw_q = _dn_pre_bwd(proj, convw8, dq, 0)
    dpk, dcw_k = _dn_pre_bwd(proj, convw8, dk, 1)
    dpv, dcw_v = _dn_pre_bwd(proj, convw8, dv, 2)

    dh_cat = jnp.concatenate([jnp.zeros((CTX_LEN, lw), F32), _to_col_major(dh_r)], axis=0)
    d_a, d_b = _lru_scan_bwd(la_a, h_prev, dh_cat)
    dxc, dw_r, dw_i, dlvecs = _lru_gates_bwd(xc, w_r, w_i, lvecs, d_a, d_b)
    dxl_cat, dlconv8 = _lru_conv_bwd(xl_cat, lconv8, dxc)
    dxl = jnp.concatenate([dxl_cat[:CTX_LEN], _to_raster(dxl_cat[CTX_LEN:])], axis=0).astype(BF16)

    dproj = [dpq, dpk, dpv, dxl, dgates]
    if hh2 < 32:
        dpba_w = jnp.concatenate([dpba[:, :hh2], dpba[:, 32:32 + hh2], jnp.zeros((r, LANE - 2 * hh2), BF16)], axis=1)
    else:
        dpba_w = dpba
    gw_qkv = _mm_tn(hn0, dproj[:3], BF16, "gw_main_qkv")[0]
    gw_rest = _mm_tn(hn0, dproj[3:], BF16, "gw_main_rest")[0]
    gw_ba = _mm_tn(hn0, dpba_w, BF16, "gw_ba")[0]
    gw_full = jnp.concatenate([gw_qkv, gw_rest[:, :lw], gw_ba[:, :2 * hh2], gw_rest[:, lw:]], axis=1)
    gw_abin = gw_full.reshape(d, NDEV, ab_in // NDEV).transpose(1, 0, 2)
    (sib_l,) = _sibling_swap([gw_abin], "swap_grads_late")
    part_l = _chip_partial(gw_abin, sib_l, "chip_partial_late")
    dhn0_a, (s_abin,) = _mm_nt(dproj, w_main[None], F32, "dhn0_main", rider=("chipx", [part_l]))
    dhn0_b = _mm_nt(dpba_w, w_ba[None], F32, "dhn0_ba")

    nt = _div(SEQ, ROW_TILE // 2, 16)
    grad_x, dshift0, dscale0, dnw0_x = _norm_bwd(x, rinv_x, [dhn0_a, dhn0_b], nw0, scale[0], "norm0_bwd_x",
                                                 row_off=CTX_LEN // nt, dx_in=dx1)
    _, dshift_c, dscale_c, dnw0_c = _norm_bwd(ctx, rinv_c, [dhn0_a, dhn0_b], nw0, scale_c, "norm0_bwd_ctx")

    big = {}
    big["ab_w_in"] = _adamw_slots(ab_w_in[0], s_abin, m_ab_w_in[0], v_ab_w_in[0], "adamw_ab_w_in")
    big["ab_w_out"] = _adamw_slots(ab_w_out[0], s_about, m_ab_w_out[0], v_ab_w_out[0], "adamw_ab_w_out")
    big["sc_w_in"] = _adamw_slots(sc_w_in[0], s_scin, m_sc_w_in[0], v_sc_w_in[0], "adamw_sc_w_in")
    big["sc_w_out"] = _adamw_slots(sc_w_out[0], s_scout, m_sc_w_out[0], v_sc_w_out[0], "adamw_sc_w_out")

    dmod_own = jnp.stack([jnp.concatenate([dshift0, dscale0, dgate0], axis=1)[0],
                          jnp.concatenate([dshift1, dscale1, dgate1], axis=1)[0]])
    dmod_c = jnp.concatenate([dshift_c, dscale_c], axis=1)[0]
    g_qkv_conv = jnp.concatenate([dcw_q[:DN_CONV], dcw_k[:DN_CONV], dcw_v[:DN_CONV]], axis=1)
    small = [
        ("loss", loss11.reshape(1)),
        ("norm_w", jnp.concatenate([dnw0_x + dnw0_c, dnw1], axis=0)),
        ("dmod_own", dmod_own),
        ("dmod_c", dmod_c),
        ("qkv_conv", g_qkv_conv),
        ("a_log", dpar[1, 32:32 + hh2].reshape(2, DN_HEADS)),
        ("dt_bias", dpar[0, 32:32 + hh2].reshape(2, DN_HEADS)),
        ("dn_norm", ddnn[0]),
        ("lru_conv_w", dlconv8[:LRU_CONV]),
        ("lru_conv_b", dlconv8[LRU_CONV]),
        ("lru_w_r", dw_r),
        ("lru_w_i", dw_i),
        ("lru_b_r", dlvecs[:, 0]),
        ("lru_b_i", dlvecs[:, 1]),
        ("lru_lambda", dlvecs[:, 2]),
        ("sc_conv", dscw8[:SC_CONV]),
        ("final_norm_w", dfnw[0]),
    ]
    names = [n for n, _ in small]
    shapes = [a.shape for _, a in small]
    spk, soffs = _pack([a for _, a in small])
    (sg,) = _gather_two_level([spk], "gather_small_grads")
    tot = dict(zip(names, _unpack(_sum_slots(sg, "sum_small_grads"), soffs, shapes)))
    i_own = names.index("dmod_own")
    dmod_all = jnp.stack([_unpack(sg[j], soffs[i_own:i_own + 1], shapes[i_own:i_own + 1])[0] for j in range(NDEV)])

    dmc_full = jnp.concatenate([tot["dmod_c"], jnp.zeros((d,), F32)])
    rows9 = jnp.concatenate([dmod_all, jnp.stack([dmc_full, jnp.zeros((3 * d,), F32)])[None],
                             jnp.zeros((7, 2, 3 * d), F32)], axis=0)
    dm = lax.dynamic_slice_in_dim(rows9, me * mcols, mcols, axis=2).transpose(1, 0, 2)
    g_modw, dl_modw, nm_modw, nv_modw, gcc = _mod_bwd(c16, dm, mod_w, m_mod_w, v_mod_w)
    gcc_pk, _ = _pack([gcc[0, :, 0]])
    (gcc_g,) = _exchange([gcc_pk], "gather_c_ctx", scatter=False)
    gcc_sum = _sum_slots(gcc_g, "sum_c_ctx").reshape(-1)[:d]
    g_c_ctx_pre = gcc_sum

    grad_mod_b = tot["dmod_own"] + jnp.stack([dmc_full, jnp.zeros((3 * d,), F32)])

    def shard(a):
        wd = a.shape[-1] // NDEV
        return lax.dynamic_slice_in_dim(a, me * wd, wd, axis=a.ndim - 1)

    small_w = [
        ("c_ctx", c_ctx, None, m_c_ctx, v_c_ctx),
        ("mod_b", mod_b, grad_mod_b, m_mod_b, v_mod_b),
        ("norm_w", norm_w, tot["norm_w"], m_norm_w, v_norm_w),
        ("ab_qkv_conv", ab_qkv_conv, shard(tot["qkv_conv"])[None], m_ab_qkv_conv, v_ab_qkv_conv),
        ("ab_a_log", ab_a_log, tot["a_log"][None], m_ab_a_log, v_ab_a_log),
        ("ab_dt_bias", ab_dt_bias, tot["dt_bias"][None], m_ab_dt_bias, v_ab_dt_bias),
        ("ab_dn_norm", ab_dn_norm, tot["dn_norm"][None], m_ab_dn_norm, v_ab_dn_norm),
        ("ab_lru_conv_w", ab_lru_conv_w, shard(tot["lru_conv_w"])[None], m_ab_lru_conv_w, v_ab_lru_conv_w),
        ("ab_lru_conv_b", ab_lru_conv_b, tot["lru_conv_b"][None], m_ab_lru_conv_b, v_ab_lru_conv_b),
        ("ab_lru_w_r", ab_lru_w_r, tot["lru_w_r"][None], m_ab_lru_w_r, v_ab_lru_w_r),
        ("ab_lru_b_r", ab_lru_b_r, shard(tot["lru_b_r"])[None], m_ab_lru_b_r, v_ab_lru_b_r),
        ("ab_lru_w_i", ab_lru_w_i, tot["lru_w_i"][None], m_ab_lru_w_i, v_ab_lru_w_i),
        ("ab_lru_b_i", ab_lru_b_i, shard(tot["lru_b_i"])[None], m_ab_lru_b_i, v_ab_lru_b_i),
        ("ab_lru_lambda", ab_lru_lambda, shard(tot["lru_lambda"])[None], m_ab_lru_lambda, v_ab_lru_lambda),
        ("sc_conv", sc_conv, shard(tot["sc_conv"])[None], m_sc_conv, v_sc_conv),
        ("final_norm_w", final_norm_w, tot["final_norm_w"], m_final_norm_w, v_final_norm_w),
    ]
    sg_cc = jax.nn.sigmoid(c_ctx)
    g_c_ctx = g_c_ctx_pre * (sg_cc * (1.0 + c_ctx * (1.0 - sg_cc)))
    small_w[0] = ("c_ctx", c_ctx, g_c_ctx, m_c_ctx, v_c_ctx)
    wshapes = [w.shape for _, w, _, _, _ in small_w]
    wpk, woffs = _pack([w for _, w, _, _, _ in small_w])
    gpk, _ = _pack([g for _, _, g, _, _ in small_w])
    mpk, _ = _pack([m for _, _, _, m, _ in small_w])
    vpk, _ = _pack([v for _, _, _, _, v in small_w])
    dl_pk, nm_pk, nv_pk = _adamw_flat(wpk, gpk, mpk, vpk, "adamw_small")
    sm_g = {n: g for n, _, g, _, _ in small_w}
    sm_dl = dict(zip([n for n, *_ in small_w], _unpack(dl_pk, woffs, wshapes)))
    sm_nm = dict(zip([n for n, *_ in small_w], _unpack(nm_pk, woffs, wshapes)))
    sm_nv = dict(zip([n for n, *_ in small_w], _unpack(nv_pk, woffs, wshapes)))

    grads, deltas, new_m, new_v = {}, {}, {}, {}
    for n in sm_g:
        grads[n], deltas[n], new_m[n], new_v[n] = sm_g[n], sm_dl[n], sm_nm[n], sm_nv[n]
    grads["mod_w"], deltas["mod_w"], new_m["mod_w"], new_v["mod_w"] = g_modw, dl_modw, nm_modw, nv_modw
    for n in ("ab_w_in", "ab_w_out", "sc_w_in", "sc_w_out"):
        g, dl, m2, v2 = big[n]
        grads[n], deltas[n], new_m[n], new_v[n] = g[None], dl[None], m2[None], v2[None]

    order = ["c_ctx", "mod_w", "mod_b", "norm_w", "ab_w_in", "ab_qkv_conv", "ab_a_log", "ab_dt_bias", "ab_dn_norm",
             "ab_lru_conv_w", "ab_lru_conv_b", "ab_lru_w_r", "ab_lru_b_r", "ab_lru_w_i", "ab_lru_b_i", "ab_lru_lambda",
             "ab_w_out", "sc_w_in", "sc_conv", "sc_w_out", "final_norm_w"]
    loss = tot["loss"][0]
    return (loss, grad_x[None], *[grads[n] for n in order], *[deltas[n] for n in order],
            *[new_m[n] for n in order], *[new_v[n] for n in order])
```

```python
import functools

import jax
import jax.numpy as jnp
from jax import lax
from jax.experimental import pallas as pl
from jax.experimental.pallas import tpu as pltpu

F32 = jnp.float32
BF16 = jnp.bfloat16

NDEV = 8
D_MODEL = 4096
SEQ = 4096
CTX_LEN = 256
GRID_W = 64
EPS = 1e-6
DN_HEADS = 16
DN_HEAD_DIM = 128
DN_CONV = 4
DN_CHUNK = 64
LRU_WIDTH = 2048
LRU_BLOCKS = 16
LRU_CONV = 4
LRU_C = 8.0
SC_CONV = 3
ADAM_LR = 0.001
ADAM_B1 = 0.9
ADAM_B2 = 0.999
ADAM_EPS = 1e-08
ADAM_WD = 0.01
ADAM_STEP = 10

ROW_TILE = 256
HEADS_PER_STEP = 16
LANE = 128
VMEM_LIMIT = 56 * 1024 * 1024


def _dims():
    d = D_MODEL
    dnw = DN_HEADS * DN_HEAD_DIM
    lw = LRU_WIDTH
    off_lru = 3 * dnw
    off_beta = off_lru + lw
    ab_state = off_beta + 4 * DN_HEADS
    ab_in = ab_state + dnw + lw
    return dict(d=d, dnw=dnw, lw=lw, off_lru=off_lru, off_beta=off_beta, ab_state=ab_state, ab_in=ab_in,
                nmain=3 * dnw + lw + dnw + lw, r=CTX_LEN + SEQ, lbd=lw // LRU_BLOCKS)


def _div(n, pref, mult):
    best = None
    for t in range(mult, min(n, pref) + 1, mult):
        if n % t == 0:
            best = t
    return best if best is not None else n


def _cp(sem, vmem=VMEM_LIMIT):
    return pltpu.CompilerParams(dimension_semantics=sem, vmem_limit_bytes=vmem)


def _silu(z):
    return z * jax.nn.sigmoid(z)


def _dsilu(z):
    s = jax.nn.sigmoid(z)
    return s * (1.0 + z * (1.0 - s))


def _softplus(z):
    return jnp.maximum(z, 0.0) + jnp.log1p(jnp.exp(-jnp.abs(z)))


def _neg_expm1(z):
    series = -z * (1.0 + z * (0.5 + z * (1.0 / 6.0 + z * (1.0 / 24.0))))
    return jnp.where(z > -0.05, series, 1.0 - jnp.exp(z))


def _me():
    return 4 * lax.axis_index("x") + 2 * lax.axis_index("y") + lax.axis_index("c")


def _exchange(srcs, name, scatter):
    n = len(srcs)

    def body(*refs):
        src_refs, out_refs = refs[:n], refs[n:2 * n]
        send_sems, recv_sems, local_sems = refs[2 * n:]
        x, y, c = lax.axis_index("x"), lax.axis_index("y"), lax.axis_index("c")
        me = 4 * x + 2 * y + c
        copies = []
        for i in range(n):
            mine = src_refs[i].at[me] if scatter else src_refs[i]
            lc = pltpu.make_async_copy(mine, out_refs[i].at[me], local_sems.at[i])
            lc.start()
            copies.append(lc)
            for k in range(1, NDEV):
                px = 1 - x if (k >> 2) & 1 else x
                py = 1 - y if (k >> 1) & 1 else y
                pc = 1 - c if k & 1 else c
                peer = 4 * px + 2 * py + pc
                src = src_refs[i].at[peer] if scatter else src_refs[i]
                cp = pltpu.make_async_remote_copy(
                    src_ref=src, dst_ref=out_refs[i].at[me], send_sem=send_sems.at[i, k - 1],
                    recv_sem=recv_sems.at[i, k - 1], device_id=(px, py, pc), device_id_type=pl.DeviceIdType.MESH)
                cp.start()
                copies.append(cp)
        for cp in copies:
            cp.wait()

    any_spec = pl.BlockSpec(memory_space=pl.ANY)
    out_shape = tuple(jax.ShapeDtypeStruct(s.shape if scatter else (NDEV,) + s.shape, s.dtype) for s in srcs)
    outs = pl.pallas_call(
        body, name=name, out_shape=out_shape, in_specs=[any_spec] * n, out_specs=tuple([any_spec] * n),
        scratch_shapes=[pltpu.SemaphoreType.DMA((n, NDEV - 1)), pltpu.SemaphoreType.DMA((n, NDEV - 1)),
                        pltpu.SemaphoreType.DMA((n,))],
    )(*srcs)
    return list(outs)


def _flip(v, bit):
    return 1 - v if bit else v


def _gather_ops(src_refs, out_refs, send_sems, recv_sems, local_sems):
    n = len(src_refs)

    def setup():
        x, y, c = lax.axis_index("x"), lax.axis_index("y"), lax.axis_index("c")
        chips = [(_flip(x, k >> 1), _flip(y, k & 1)) for k in (1, 2, 3)]
        return x, y, c, chips

    def copy(i, sem, slot, to, src=None):
        return pltpu.make_async_remote_copy(
            src_ref=out_refs[i].at[slot] if src is None else src, dst_ref=out_refs[i].at[slot],
            send_sem=send_sems.at[i, sem], recv_sem=recv_sems.at[i, sem], device_id=to,
            device_id_type=pl.DeviceIdType.MESH)

    def own(i, me):
        return pltpu.make_async_copy(src_refs[i], out_refs[i].at[me], local_sems.at[i])

    def first(i, x, y, c, chips):
        me = 4 * x + 2 * y + c
        return [copy(i, 0, me, (x, y, 1 - c), src=src_refs[i])] + [
            copy(i, 1 + j, me, (px, py, c), src=src_refs[i]) for j, (px, py) in enumerate(chips)]

    def start():
        x, y, c, chips = setup()
        for i in range(n):
            own(i, 4 * x + 2 * y + c).start()
            for cp in first(i, x, y, c, chips):
                cp.start()

    def finish():
        x, y, c, chips = setup()
        here = (x, y, c)
        passed = []
        for i in range(n):
            for j, (px, py) in enumerate(chips):
                theirs = 4 * px + 2 * py + c
                copy(i, 1 + j, theirs, here).wait_recv()
                fwd = copy(i, 4 + j, theirs, (x, y, 1 - c))
                fwd.start()
                passed.append(fwd)
        for i in range(n):
            copy(i, 0, 4 * x + 2 * y + (1 - c), here).wait_recv()
            for j, (px, py) in enumerate(chips):
                copy(i, 4 + j, 4 * px + 2 * py + (1 - c), here).wait_recv()
        for i in range(n):
            for cp in first(i, x, y, c, chips):
                cp.wait_send()
        for cp in passed:
            cp.wait_send()
        for i in range(n):
            own(i, 4 * x + 2 * y + c).wait()

    return start, finish


def _chipx_ops(src_refs, out_refs, send_sems, recv_sems, local_sems):
    n = len(src_refs)

    def copies():
        x, y, c = lax.axis_index("x"), lax.axis_index("y"), lax.axis_index("c")
        mine = 2 * x + y
        res = []
        for i in range(n):
            res.append(pltpu.make_async_copy(src_refs[i].at[mine], out_refs[i].at[mine], local_sems.at[i]))
            for k in (1, 2, 3):
                px, py = _flip(x, k >> 1), _flip(y, k & 1)
                res.append(pltpu.make_async_remote_copy(
                    src_ref=src_refs[i].at[2 * px + py], dst_ref=out_refs[i].at[mine], send_sem=send_sems.at[i, k - 1],
                    recv_sem=recv_sems.at[i, k - 1], device_id=(px, py, c), device_id_type=pl.DeviceIdType.MESH))
        return res

    def start():
        for cp in copies():
            cp.start()

    def finish():
        for cp in copies():
            cp.wait()

    return start, finish


def _swap_ops(src_refs, out_refs, send_sems, recv_sems):
    n = len(src_refs)

    def copies():
        x, y, c = lax.axis_index("x"), lax.axis_index("y"), lax.axis_index("c")
        return [pltpu.make_async_remote_copy(
            src_ref=src_refs[i].at[2 * q + (1 - c)], dst_ref=out_refs[i].at[q], send_sem=send_sems.at[i, q],
            recv_sem=recv_sems.at[i, q], device_id=(x, y, 1 - c), device_id_type=pl.DeviceIdType.MESH)
            for i in range(n) for q in range(NDEV // 2)]

    def start():
        for cp in copies():
            cp.start()

    def finish():
        for cp in copies():
            cp.wait()

    return start, finish


def _rider(kind, arrays):
    n = len(arrays)
    any_spec = pl.BlockSpec(memory_space=pl.ANY)
    dma = pltpu.SemaphoreType.DMA
    if kind == "gather":
        out_shapes = [jax.ShapeDtypeStruct((NDEV,) + a.shape, a.dtype) for a in arrays]
        scratch, make = [dma((n, NDEV - 1)), dma((n, NDEV - 1)), dma((n,))], _gather_ops
    elif kind == "swap":
        out_shapes = [jax.ShapeDtypeStruct((NDEV // 2,) + a.shape[1:], a.dtype) for a in arrays]
        scratch, make = [dma((n, NDEV // 2)), dma((n, NDEV // 2))], _swap_ops
    else:
        out_shapes = [jax.ShapeDtypeStruct(a.shape, a.dtype) for a in arrays]
        scratch, make = [dma((n, NDEV // 2 - 1)), dma((n, NDEV // 2 - 1)), dma((n,))], _chipx_ops
    return list(arrays), [any_spec] * n, out_shapes, [any_spec] * n, scratch, make


def _exchange_call(kind, arrays, name):
    n = len(arrays)
    ins, in_specs, out_shapes, out_specs, scratch, make = _rider(kind, arrays)

    def body(*refs):
        start, finish = make(refs[:n], refs[n:2 * n], *refs[2 * n:])
        start()
        finish()

    outs = pl.pallas_call(body, name=name, out_shape=tuple(out_shapes), in_specs=in_specs, out_specs=tuple(out_specs),
                          scratch_shapes=scratch)(*ins)
    return list(outs)


def _gather_two_level(srcs, name):
    return _exchange_call("gather", srcs, name)


def _call_with_rider(body, rider, *, name, grid, out_shape, in_specs, out_specs, scratch_shapes, compiler_params,
                     operands):
    if rider is None:
        outs = pl.pallas_call(body, name=name, grid=grid, out_shape=tuple(out_shape), in_specs=list(in_specs),
                              out_specs=tuple(out_specs), scratch_shapes=list(scratch_shapes),
                              compiler_params=compiler_params)(*operands)
        return list(outs), []
    r_ins, r_in_specs, r_out_shapes, r_out_specs, r_scratch, make = _rider(*rider)
    ni, no, ns, nr = len(in_specs), len(out_shape), len(scratch_shapes), len(r_ins)

    def riding(*refs):
        ins, rins = refs[:ni], refs[ni:ni + nr]
        outs, routs = refs[ni + nr:ni + nr + no], refs[ni + nr + no:ni + nr + no + nr]
        scr, rscr = refs[ni + 2 * nr + no:ni + 2 * nr + no + ns], refs[ni + 2 * nr + no + ns:]
        start, finish = make(rins, routs, *rscr)
        ids = [pl.program_id(a) for a in range(len(grid))]
        first, last = ids[0] == 0, ids[0] == grid[0] - 1
        for a in range(1, len(grid)):
            first = jnp.logical_and(first, ids[a] == 0)
            last = jnp.logical_and(last, ids[a] == grid[a] - 1)

        @pl.when(first)
        def _():
            start()

        body(*ins, *outs, *scr)

        @pl.when(last)
        def _():
            finish()

    outs = pl.pallas_call(
        riding, name=name, grid=grid, out_shape=tuple(out_shape) + tuple(r_out_shapes),
        in_specs=list(in_specs) + r_in_specs, out_specs=tuple(out_specs) + tuple(r_out_specs),
        scratch_shapes=list(scratch_shapes) + r_scratch, compiler_params=compiler_params)(*operands, *r_ins)
    return list(outs[:no]), list(outs[no:])


def _sibling_swap(srcs, name):
    return _exchange_call("swap", srcs, name)


def _chip_partial(own, sib, name):
    _, rows, cols = own.shape
    tm = _div(rows, max(16, (128 * 4096) // cols), 16)

    def body(c_ref, o_ref, s_ref, p_ref):
        p_ref[0] = (o_ref[0].astype(F32) + s_ref[0].astype(F32)).astype(BF16)

    core = lax.axis_index("c").astype(jnp.int32).reshape(1)
    grid_spec = pltpu.PrefetchScalarGridSpec(
        num_scalar_prefetch=1, grid=(NDEV // 2, rows // tm),
        in_specs=[pl.BlockSpec((1, tm, cols), lambda q, i, c_ref: (2 * q + c_ref[0], i, 0)),
                  pl.BlockSpec((1, tm, cols), lambda q, i, c_ref: (q, i, 0))],
        out_specs=pl.BlockSpec((1, tm, cols), lambda q, i, c_ref: (q, i, 0)))
    return pl.pallas_call(
        body, name=name, grid_spec=grid_spec, out_shape=jax.ShapeDtypeStruct(sib.shape, BF16),
        compiler_params=_cp(("arbitrary", "arbitrary")))(core, own, sib)


def _chip_exchange(parts, name):
    return _exchange_call("chipx", parts, name)


def _sum_slots(g, name):
    _, rows, cols = g.shape
    tm = _div(rows, 1024, 8)

    def body(g_ref, o_ref):
        acc = g_ref[0]
        for j in range(1, NDEV):
            acc = acc + g_ref[j]
        o_ref[...] = acc

    return pl.pallas_call(
        body, name=name, grid=(rows // tm,), out_shape=jax.ShapeDtypeStruct((rows, cols), F32),
        in_specs=[pl.BlockSpec((NDEV, tm, cols), lambda i: (0, i, 0))],
        out_specs=pl.BlockSpec((tm, cols), lambda i: (i, 0)), compiler_params=_cp(("arbitrary",)))(g)


def _mm_nn(a, b3, out_dtype, name, tm=1152, tn=1024, rider=None):
    m_, k_ = a.shape
    j_, _, nj = b3.shape
    tm = _div(m_, tm, 16)
    tn = _div(nj, tn, LANE)
    per = nj // tn

    def body(a_ref, b_ref, o_ref):
        o_ref[...] = jnp.dot(a_ref[...], b_ref[0], preferred_element_type=F32).astype(o_ref.dtype)

    outs, rode = _call_with_rider(
        body, rider, name=name, grid=(j_ * per, m_ // tm),
        out_shape=(jax.ShapeDtypeStruct((m_, j_ * nj), out_dtype),),
        in_specs=[pl.BlockSpec((tm, k_), lambda n, m: (m, 0)),
                  pl.BlockSpec((1, k_, tn), lambda n, m: (n // per, 0, n % per))],
        out_specs=(pl.BlockSpec((tm, tn), lambda n, m: (m, n)),), scratch_shapes=[],
        compiler_params=_cp(("arbitrary", "arbitrary")), operands=(a, b3))
    return outs[0] if rider is None else (outs[0], rode)


def _mm_nt(a, b3, out_dtype, name, tm=1152, tn=1024, tk=2048, rider=None):
    m_, k_ = a.shape
    j_, n_, kj = b3.shape
    tm = _div(m_, tm, 16)
    tn = _div(n_, tn, LANE)
    tk = _div(kj, tk, LANE)
    perk = kj // tk
    nk = k_ // tk

    def body(a_ref, b_ref, o_ref, acc_ref):
        k = pl.program_id(2)

        @pl.when(k == 0)
        def _():
            acc_ref[...] = jnp.zeros_like(acc_ref)

        acc_ref[...] += lax.dot_general(a_ref[...], b_ref[0], (((1,), (1,)), ((), ())), preferred_element_type=F32)

        @pl.when(k == nk - 1)
        def _():
            o_ref[...] = acc_ref[...].astype(o_ref.dtype)

    outs, rode = _call_with_rider(
        body, rider, name=name, grid=(n_ // tn, m_ // tm, nk),
        out_shape=(jax.ShapeDtypeStruct((m_, n_), out_dtype),),
        in_specs=[pl.BlockSpec((tm, tk), lambda n, m, k: (m, k)),
                  pl.BlockSpec((1, tn, tk), lambda n, m, k: (k // perk, n, k % perk))],
        out_specs=(pl.BlockSpec((tm, tn), lambda n, m, k: (m, n)),),
        scratch_shapes=[pltpu.VMEM((tm, tn), F32)],
        compiler_params=_cp(("arbitrary", "arbitrary", "arbitrary")), operands=(a, b3))
    return outs[0] if rider is None else (outs[0], rode)


def _mm_tn(a, b, out_dtype, name, oj=1, tm=1024, tn=512):
    k_, m_ = a.shape
    _, n_ = b.shape
    nj = n_ // oj
    tm = _div(m_, tm, LANE)
    tn = _div(nj, tn, LANE)
    per = nj // tn

    def body(a_ref, b_ref, o_ref):
        o_ref[0] = lax.dot_general(a_ref[...], b_ref[...], (((0,), (0,)), ((), ())),
                                   preferred_element_type=F32).astype(o_ref.dtype)

    return pl.pallas_call(
        body, name=name, grid=(m_ // tm, n_ // tn), out_shape=jax.ShapeDtypeStruct((oj, m_, nj), out_dtype),
        in_specs=[pl.BlockSpec((k_, tm), lambda m, n: (0, m)), pl.BlockSpec((k_, tn), lambda m, n: (0, n))],
        out_specs=pl.BlockSpec((1, tm, tn), lambda m, n: (n // per, m, n % per)),
        compiler_params=_cp(("arbitrary", "arbitrary")))(a, b)


def _mod_fwd(c16, w, bias):
    _, d, cols = w.shape
    tn = _div(cols, 512, LANE)

    def body(c_ref, w_ref, b_ref, o_ref):
        sc = _silu(c_ref[...])
        o_ref[0] = jnp.dot(sc, w_ref[0], preferred_element_type=F32) + b_ref[0]

    return pl.pallas_call(
        body, name="mod_fwd", grid=(2, cols // tn), out_shape=jax.ShapeDtypeStruct((2, 16, cols), F32),
        in_specs=[pl.BlockSpec((16, d), lambda l, n: (0, 0)), pl.BlockSpec((1, d, tn), lambda l, n: (l, 0, n)),
                  pl.BlockSpec((1, 1, tn), lambda l, n: (l, 0, n))],
        out_specs=pl.BlockSpec((1, 16, tn), lambda l, n: (l, 0, n)),
        compiler_params=_cp(("arbitrary", "arbitrary")))(c16, w, bias)


def _adam(w, g, m, v):
    m2 = ADAM_B1 * m + (1.0 - ADAM_B1) * g
    v2 = ADAM_B2 * v + (1.0 - ADAM_B2) * (g * g)
    m_hat = m2 / (1.0 - ADAM_B1 ** ADAM_STEP)
    v_hat = v2 / (1.0 - ADAM_B2 ** ADAM_STEP)
    delta = -ADAM_LR * (m_hat / (jnp.sqrt(v_hat) + ADAM_EPS) + ADAM_WD * w)
    return delta, m2, v2


def _mod_bwd(c16, dm, w, m, v):
    _, d, cols = w.shape
    tm = _div(d, 256, 8)

    def body(c_ref, dm_ref, w_ref, m_ref, v_ref, g_ref, dl_ref, nm_ref, nv_ref, gc_ref):
        sc = _silu(c_ref[...])
        g = lax.dot_general(sc, dm_ref[0], (((0,), (0,)), ((), ())), preferred_element_type=F32,
                            precision=lax.Precision.HIGHEST)
        wv = w_ref[0]
        dl, m2, v2 = _adam(wv, g, m_ref[0], v_ref[0])
        g_ref[0] = g
        dl_ref[0] = dl
        nm_ref[0] = m2
        nv_ref[0] = v2
        gc_ref[0] = jnp.sum(wv * dm_ref[0, 8:9, :], axis=1, keepdims=True)

    big = pl.BlockSpec((1, tm, cols), lambda l, i: (l, i, 0))
    shp = jax.ShapeDtypeStruct(w.shape, F32)
    return pl.pallas_call(
        body, name="mod_bwd", grid=(2, d // tm),
        out_shape=(shp, shp, shp, shp, jax.ShapeDtypeStruct((2, d, 1), F32)),
        in_specs=[pl.BlockSpec((16, tm), lambda l, i: (0, i)), pl.BlockSpec((1, 16, cols), lambda l, i: (l, 0, 0)),
                  big, big, big],
        out_specs=(big, big, big, big, pl.BlockSpec((1, tm, 1), lambda l, i: (l, i, 0))),
        compiler_params=_cp(("arbitrary", "arbitrary")))(c16, dm, w, m, v)


def _row_spec(tm, d, off=0):
    return pl.BlockSpec((tm, d), lambda i: (i + off, 0))


def _vec_spec(d):
    return pl.BlockSpec((1, d), lambda i: (0, 0))


def _norm_fwd(x, nw, shift, scale, name):
    n, d = x.shape
    tm = _div(n, ROW_TILE, 16)

    def body(x_ref, nw_ref, sh_ref, sc_ref, hn_ref, r_ref):
        xv = x_ref[...]
        r = lax.rsqrt(jnp.mean(xv * xv, axis=1, keepdims=True) + EPS)
        hn_ref[...] = (xv * r * nw_ref[...] * (1.0 + sc_ref[...]) + sh_ref[...]).astype(BF16)
        r_ref[...] = r

    return pl.pallas_call(
        body, name=name, grid=(n // tm,),
        out_shape=(jax.ShapeDtypeStruct((n, d), BF16), jax.ShapeDtypeStruct((n, 1), F32)),
        in_specs=[_row_spec(tm, d), _vec_spec(d), _vec_spec(d), _vec_spec(d)],
        out_specs=(_row_spec(tm, d), _row_spec(tm, 1)), compiler_params=_cp(("arbitrary",)))(x, nw, shift, scale)


def _resid_norm_fwd(x, out0, gate, nw, shift, scale):
    n, d = x.shape
    tm = _div(n, ROW_TILE, 16)

    def body(x_ref, o_ref, g_ref, nw_ref, sh_ref, sc_ref, x1_ref, hn_ref, r_ref):
        xv = x_ref[...] + g_ref[...] * o_ref[...]
        r = lax.rsqrt(jnp.mean(xv * xv, axis=1, keepdims=True) + EPS)
        x1_ref[...] = xv
        hn_ref[...] = (xv * r * nw_ref[...] * (1.0 + sc_ref[...]) + sh_ref[...]).astype(BF16)
        r_ref[...] = r

    return pl.pallas_call(
        body, name="resid_norm1", grid=(n // tm,),
        out_shape=(jax.ShapeDtypeStruct((n, d), F32), jax.ShapeDtypeStruct((n, d), BF16),
                   jax.ShapeDtypeStruct((n, 1), F32)),
        in_specs=[_row_spec(tm, d), _row_spec(tm, d), _vec_spec(d), _vec_spec(d), _vec_spec(d), _vec_spec(d)],
        out_specs=(_row_spec(tm, d), _row_spec(tm, d), _row_spec(tm, 1)),
        compiler_params=_cp(("arbitrary",)))(x, out0, gate, nw, shift, scale)


def _loss_head(x1, out1, gate1, fnw, tgt):
    n, d = x1.shape
    tm = _div(n, ROW_TILE, 16)
    nsteps = n // tm

    def body(x_ref, o_ref, g_ref, w_ref, t_ref, dx_ref, do_ref, loss_ref, dw_ref, dg_ref, lacc_ref):
        i = pl.program_id(0)

        @pl.when(i == 0)
        def _():
            lacc_ref[...] = jnp.zeros_like(lacc_ref)
            dw_ref[...] = jnp.zeros_like(dw_ref)
            dg_ref[...] = jnp.zeros_like(dg_ref)

        ov = o_ref[...]
        gv = g_ref[...]
        wv = w_ref[...]
        xv = x_ref[...] + gv * ov
        r = lax.rsqrt(jnp.mean(xv * xv, axis=1, keepdims=True) + EPS)
        xh = xv * r
        diff = xh * wv - t_ref[...]
        lacc_ref[...] += jnp.sum(diff * diff, axis=0, keepdims=True)
        dy = diff * (1.0 / d)
        dw_ref[...] += jnp.sum(dy * xh, axis=0, keepdims=True)
        dxh = dy * wv
        dx = r * (dxh - xh * jnp.mean(xh * dxh, axis=1, keepdims=True))
        dx_ref[...] = dx
        dg_ref[...] += jnp.sum(dx * ov, axis=0, keepdims=True)
        do_ref[...] = (gv * dx).astype(BF16)

        @pl.when(i == nsteps - 1)
        def _():
            loss_ref[...] = (0.5 / d) * jnp.sum(lacc_ref[...], axis=1, keepdims=True)

    vec = jax.ShapeDtypeStruct((1, d), F32)
    return pl.pallas_call(
        body, name="loss_head", grid=(nsteps,),
        out_shape=(jax.ShapeDtypeStruct((n, d), F32), jax.ShapeDtypeStruct((n, d), BF16),
                   jax.ShapeDtypeStruct((1, 1), F32), vec, vec),
        in_specs=[_row_spec(tm, d), _row_spec(tm, d), _vec_spec(d), _vec_spec(d), _row_spec(tm, d)],
        out_specs=(_row_spec(tm, d), _row_spec(tm, d), pl.BlockSpec((1, 1), lambda i: (0, 0)), _vec_spec(d),
                   _vec_spec(d)),
        scratch_shapes=[pltpu.VMEM((1, d), F32)], compiler_params=_cp(("arbitrary",)))(x1, out1, gate1, fnw, tgt)


def _norm_bwd(x, rinv, dhns, nw, scale, name, row_off=0, dx_in=None, resid=None):
    n, d = x.shape
    tm = _div(n, ROW_TILE // 2, 16)
    nd = len(dhns)
    has_in = dx_in is not None
    has_res = resid is not None

    def body(*refs):
        it = iter(refs)
        x_ref, r_ref = next(it), next(it)
        dh_refs = [next(it) for _ in range(nd)]
        nw_ref, sc_ref = next(it), next(it)
        dxin_ref = next(it) if has_in else None
        o_ref, g_ref = (next(it), next(it)) if has_res else (None, None)
        dx_ref = next(it)
        do_ref = next(it) if has_res else None
        dsh_ref, dsc_ref, dnw_ref = next(it), next(it), next(it)
        dg_ref = next(it) if has_res else None
        i = pl.program_id(0)

        @pl.when(i == 0)
        def _():
            dsh_ref[...] = jnp.zeros_like(dsh_ref)
            dsc_ref[...] = jnp.zeros_like(dsc_ref)
            dnw_ref[...] = jnp.zeros_like(dnw_ref)
            if has_res:
                dg_ref[...] = jnp.zeros_like(dg_ref)

        dh = dh_refs[0][...]
        for rr in dh_refs[1:]:
            dh = dh + rr[...]
        r = r_ref[...]
        xh = x_ref[...] * r
        nwv = nw_ref[...]
        mod = 1.0 + sc_ref[...]
        dsh_ref[...] += jnp.sum(dh, axis=0, keepdims=True)
        dhx = dh * xh
        dsc_ref[...] += jnp.sum(dhx * nwv, axis=0, keepdims=True)
        dnw_ref[...] += jnp.sum(dhx * mod, axis=0, keepdims=True)
        dxh = dh * (nwv * mod)
        dx = r * (dxh - xh * jnp.mean(xh * dxh, axis=1, keepdims=True))
        if has_in:
            dx = dx + dxin_ref[...]
        dx_ref[...] = dx
        if has_res:
            dg_ref[...] += jnp.sum(dx * o_ref[...], axis=0, keepdims=True)
            do_ref[...] = (g_ref[...] * dx).astype(BF16)

    ins = [x, rinv] + list(dhns) + [nw, scale]
    in_specs = [_row_spec(tm, d), _row_spec(tm, 1)] + [_row_spec(tm, d, row_off)] * nd + [_vec_spec(d), _vec_spec(d)]
    if has_in:
        ins.append(dx_in)
        in_specs.append(_row_spec(tm, d))
    if has_res:
        ins += list(resid)
        in_specs += [_row_spec(tm, d), _vec_spec(d)]
    vec = jax.ShapeDtypeStruct((1, d), F32)
    out_shape = [jax.ShapeDtypeStruct((n, d), F32)]
    out_specs = [_row_spec(tm, d)]
    if has_res:
        out_shape.append(jax.ShapeDtypeStruct((n, d), BF16))
        out_specs.append(_row_spec(tm, d))
    out_shape += [vec, vec, vec]
    out_specs += [_vec_spec(d)] * 3
    if has_res:
        out_shape.append(vec)
        out_specs.append(_vec_spec(d))
    return pl.pallas_call(
        body, name=name, grid=(n // tm,), out_shape=tuple(out_shape), in_specs=in_specs, out_specs=tuple(out_specs),
        compiler_params=_cp(("arbitrary",)))(*ins)


def _seg_masks(rows, width, seg_bounds, offs):
    t = lax.broadcasted_iota(jnp.int32, (rows, width), 0)
    s0 = jnp.zeros_like(t)
    s1 = jnp.full_like(t, rows)
    for (a, b) in seg_bounds:
        inside = (t >= a) & (t < b)
        s0 = jnp.where(inside, a, s0)
        s1 = jnp.where(inside, b, s1)
    return {off: ((t + off >= s0) & (t + off < s1)) for off in offs}


def _shift_rows(v, off, valid):
    rows = v.shape[0]
    return jnp.where(valid, pltpu.roll(v, (-off) % rows, axis=0), 0.0)


def _conv_fwd_val(xv, wv, taps, masks):
    acc = None
    for tap, off in taps:
        xs = xv if off == 0 else _shift_rows(xv, off, masks[off])
        term = wv[tap:tap + 1, :] * xs
        acc = term if acc is None else acc + term
    return acc


def _conv_bwd_val(xv, dacc, wv, taps, masks):
    dx = None
    dws = []
    for tap, off in taps:
        xs = xv if off == 0 else _shift_rows(xv, off, masks[off])
        dws.append(jnp.sum(dacc * xs, axis=0, keepdims=True))
        ds = dacc if off == 0 else _shift_rows(dacc, -off, masks[-off])
        term = wv[tap:tap + 1, :] * ds
        dx = term if dx is None else dx + term
    return dx, dws


def _taps(k):
    left = k // 2
    return [(j, j - left) for j in range(k)]


def _seq_bounds():
    return [(0, CTX_LEN), (CTX_LEN, CTX_LEN + SEQ)]


def _dn_pre_fwd(proj, convw):
    dd = _dims()
    r = dd["r"]
    nstrip = 3 * DN_HEADS
    taps = _taps(DN_CONV)
    offs = [o for _, o in taps if o != 0]

    def body(x_ref, w_ref, o_ref):
        kind = pl.program_id(0) // DN_HEADS
        masks = _seg_masks(r, DN_HEAD_DIM, _seq_bounds(), offs)
        y = _silu(_conv_fwd_val(x_ref[...], w_ref[...], taps, masks))
        inv = lax.rsqrt(jnp.sum(y * y, axis=1, keepdims=True) + EPS)
        scale = jnp.where(kind == 0, inv * (DN_HEAD_DIM ** -0.5), jnp.where(kind == 1, inv, 1.0))
        o_ref[...] = y * scale

    return pl.pallas_call(
        body, name="dn_pre_fwd", grid=(nstrip,), out_shape=jax.ShapeDtypeStruct((r, 3 * dd["dnw"]), F32),
        in_specs=[pl.BlockSpec((r, DN_HEAD_DIM), lambda j: (0, j)), pl.BlockSpec((8, DN_HEAD_DIM), lambda j: (0, j))],
        out_specs=pl.BlockSpec((r, DN_HEAD_DIM), lambda j: (0, j)), compiler_params=_cp(("arbitrary",)))(proj, convw)


def _dn_pre_bwd(proj, convw, dqkv, kind):
    dd = _dims()
    r = dd["r"]
    taps = _taps(DN_CONV)
    offs = sorted({o for _, o in taps if o != 0} | {-o for _, o in taps if o != 0})
    cscale = DN_HEAD_DIM ** -0.5 if kind == 0 else 1.0

    def body(x_ref, w_ref, d_ref, dx_ref, dw_ref):
        masks = _seg_masks(r, DN_HEAD_DIM, _seq_bounds(), offs)
        xv = x_ref[...]
        wv = w_ref[...]
        acc = _conv_fwd_val(xv, wv, taps, masks)
        dout = d_ref[0] + d_ref[1]
        if kind == 2:
            dy = dout
        else:
            y = _silu(acc)
            inv = lax.rsqrt(jnp.sum(y * y, axis=1, keepdims=True) + EPS)
            dy = cscale * inv * (dout - y * (inv * inv) * jnp.sum(dout * y, axis=1, keepdims=True))
        dacc = dy * _dsilu(acc)
        dx, dws = _conv_bwd_val(xv, dacc, wv, taps, masks)
        dx_ref[...] = dx.astype(BF16)
        dw_ref[...] = jnp.concatenate(dws + [jnp.zeros((8 - len(dws), DN_HEAD_DIM), F32)], axis=0)

    base = kind * DN_HEADS
    return pl.pallas_call(
        body, name=f"dn_pre_bwd{kind}", grid=(DN_HEADS,),
        out_shape=(jax.ShapeDtypeStruct((r, dd["dnw"]), BF16), jax.ShapeDtypeStruct((8, dd["dnw"]), F32)),
        in_specs=[pl.BlockSpec((r, DN_HEAD_DIM), lambda j: (0, base + j)),
                  pl.BlockSpec((8, DN_HEAD_DIM), lambda j: (0, base + j)),
                  pl.BlockSpec((2, r, DN_HEAD_DIM), lambda j: (0, 0, j))],
        out_specs=(pl.BlockSpec((r, DN_HEAD_DIM), lambda j: (0, j)), pl.BlockSpec((8, DN_HEAD_DIM), lambda j: (0, j))),
        compiler_params=_cp(("arbitrary",)))(proj, convw, dqkv)


def _lru_conv_fwd(xl, convw):
    r, lw = xl.shape
    taps = _taps(LRU_CONV)
    offs = [o for _, o in taps if o != 0]

    def body(x_ref, w_ref, o_ref):
        masks = _seg_masks(r, LANE, _seq_bounds(), offs)
        wv = w_ref[...]
        o_ref[...] = _conv_fwd_val(x_ref[...], wv, taps, masks) + wv[4:5, :]

    return pl.pallas_call(
        body, name="lru_conv_fwd", grid=(lw // LANE,), out_shape=jax.ShapeDtypeStruct((r, lw), F32),
        in_specs=[pl.BlockSpec((r, LANE), lambda j: (0, j)), pl.BlockSpec((8, LANE), lambda j: (0, j))],
        out_specs=pl.BlockSpec((r, LANE), lambda j: (0, j)), compiler_params=_cp(("arbitrary",)))(xl, convw)


def _lru_conv_bwd(xl, convw, dxc):
    r, lw = xl.shape
    taps = _taps(LRU_CONV)
    offs = sorted({o for _, o in taps if o != 0} | {-o for _, o in taps if o != 0})

    def body(x_ref, w_ref, d_ref, dx_ref, dw_ref):
        masks = _seg_masks(r, LANE, _seq_bounds(), offs)
        dacc = d_ref[0] + d_ref[1]
        dx, dws = _conv_bwd_val(x_ref[...], dacc, w_ref[...], taps, masks)
        dx_ref[...] = dx
        db = jnp.sum(dacc, axis=0, keepdims=True)
        dw_ref[...] = jnp.concatenate(dws + [db, jnp.zeros((3, LANE), F32)], axis=0)

    return pl.pallas_call(
        body, name="lru_conv_bwd", grid=(lw // LANE,),
        out_shape=(jax.ShapeDtypeStruct((r, lw), F32), jax.ShapeDtypeStruct((8, lw), F32)),
        in_specs=[pl.BlockSpec((r, LANE), lambda j: (0, j)), pl.BlockSpec((8, LANE), lambda j: (0, j)),
                  pl.BlockSpec((2, r, LANE), lambda j: (0, 0, j))],
        out_specs=(pl.BlockSpec((r, LANE), lambda j: (0, j)), pl.BlockSpec((8, LANE), lambda j: (0, j))),
        compiler_params=_cp(("arbitrary",)))(xl, convw, dxc)


def _tri(rev_lane, n, m):
    r = lax.broadcasted_iota(jnp.int32, (n, n), 0)
    c = lax.broadcasted_iota(jnp.int32, (n, n), 1)
    return (r >= c).astype(F32), (r <= c).astype(F32)


def _hdot(a, b):
    return jnp.dot(a, b, preferred_element_type=F32, precision=lax.Precision.HIGHEST)


def _gate_lanes():
    lane = lax.broadcasted_iota(jnp.int32, (DN_CHUNK, LANE), 1)
    group = lane // 32
    rev = (lane % 32) >= DN_HEADS if DN_HEADS * 2 <= 32 else None
    return group, rev


def _dn_gates_fwd(pba, par):
    r = pba.shape[0]
    nch = r // DN_CHUNK
    hh = 2 * DN_HEADS

    def body(x_ref, p_ref, o_ref):
        lo, up = _tri(None, DN_CHUNK, DN_CHUNK)
        ones = jnp.ones((DN_CHUNK, DN_CHUNK), F32)
        lane = lax.broadcasted_iota(jnp.int32, (DN_CHUNK, LANE), 1)
        group = lane // 32
        rev = (lane % 32) >= DN_HEADS
        dtb = p_ref[0:1, :]
        nea = p_ref[1:2, :]

        def step(i, carry):
            rows = pl.ds(pl.multiple_of(i * DN_CHUNK, DN_CHUNK), DN_CHUNK)
            xv = x_ref[rows, :]
            beta = jax.nn.sigmoid(xv)
            g = jnp.where(group == 1, nea * _softplus(xv + dtb), 0.0)
            gc = jnp.where(rev, _hdot(up, g), _hdot(lo, g))
            gt = _hdot(ones, g)
            out = jnp.where(group == 0, beta, jnp.where(group == 1, g, 0.0))
            out = out + jnp.where(group == 2, pltpu.roll(gc, 32, axis=1), 0.0)
            out = out + jnp.where(group == 3, pltpu.roll(gt, 64, axis=1), 0.0)
            o_ref[rows, :] = out
            return carry

        lax.fori_loop(0, nch, step, 0)

    return pl.pallas_call(body, name="dn_gates_fwd", out_shape=jax.ShapeDtypeStruct((r, LANE), F32),
                          compiler_params=_cp(None))(pba, par)


def _dn_gates_bwd(pba, par, dsc):
    r = pba.shape[0]
    nch = r // DN_CHUNK

    def body(x_ref, p_ref, d_ref, dx_ref, dp_ref, acc_ref):
        lo, up = _tri(None, DN_CHUNK, DN_CHUNK)
        ones = jnp.ones((DN_CHUNK, DN_CHUNK), F32)
        lane = lax.broadcasted_iota(jnp.int32, (DN_CHUNK, LANE), 1)
        group = lane // 32
        rev = (lane % 32) >= DN_HEADS
        dtb = p_ref[0:1, :]
        nea = p_ref[1:2, :]
        acc_ref[...] = jnp.zeros_like(acc_ref)

        def step(i, carry):
            rows = pl.ds(pl.multiple_of(i * DN_CHUNK, DN_CHUNK), DN_CHUNK)
            xv = x_ref[rows, :]
            dv = d_ref[rows, :]
            beta = jax.nn.sigmoid(xv)
            z = xv + dtb
            sp = _softplus(z)
            dgc = jnp.where(group == 1, pltpu.roll(dv, LANE - 32, axis=1), 0.0)
            dgt = jnp.where(group == 1, pltpu.roll(dv, LANE - 64, axis=1), 0.0)
            dg = jnp.where(rev, _hdot(lo, dgc), _hdot(up, dgc)) + _hdot(ones, dgt)
            dg = dg + jnp.where(group == 1, dv, 0.0)
            dal = dg * nea * jax.nn.sigmoid(z)
            dpb = dv * beta * (1.0 - beta)
            dx_ref[rows, :] = jnp.where(group == 0, dpb, jnp.where(group == 1, dal, 0.0)).astype(BF16)
            acc_ref[0:1, :] += jnp.sum(jnp.where(group == 1, dal, 0.0), axis=0, keepdims=True)
            acc_ref[1:2, :] += jnp.sum(jnp.where(group == 1, dg * nea * sp, 0.0), axis=0, keepdims=True)
            return carry

        lax.fori_loop(0, nch, step, 0)
        dp_ref[...] = acc_ref[...]

    return pl.pallas_call(
        body, name="dn_gates_bwd",
        out_shape=(jax.ShapeDtypeStruct((r, LANE), BF16), jax.ShapeDtypeStruct((8, LANE), F32)),
        scratch_shapes=[pltpu.VMEM((8, LANE), F32)], compiler_params=_cp(None))(pba, par, dsc)


_B_NN = (((2,), (1,)), ((0,), (0,)))
_B_NT = (((2,), (2,)), ((0,), (0,)))
_B_TN = (((1,), (1,)), ((0,), (0,)))


def _bdot(a, b, dims, exact=False):
    return lax.dot_general(a, b, dims, preferred_element_type=F32,
                           precision=lax.Precision.HIGH if exact else None)


@jax.custom_vjp
def _inv_unit_tri(l):
    n = l.shape[-1]
    ri = lax.broadcasted_iota(jnp.int32, (n, n), 0)
    ci = lax.broadcasted_iota(jnp.int32, (n, n), 1)
    x = jnp.broadcast_to((ri == ci).astype(F32), l.shape)
    inner = 1
    for size in (4, 16, n):
        so, si = size.bit_length() - 1, inner.bit_length() - 1
        same_outer = (ri >> so) == (ci >> so)
        same_inner = (ri >> si) == (ci >> si)
        part = jnp.where(same_outer, jnp.where(same_inner, 0.0, l), 0.0)
        m = part if inner == 1 else _bdot(x, part, _B_NN, exact=True)
        y = x
        for _ in range(size // inner - 1):
            y = x - _bdot(m, y, _B_NN, exact=True)
        x = y
        inner = size
    return x


def _inv_fwd(l):
    a = _inv_unit_tri(l)
    return a, a


def _inv_bwd(a, da):
    t = _bdot(a, da, _B_TN, exact=True)
    return (-_bdot(t, a, _B_NT, exact=True),)


_inv_unit_tri.defvjp(_inv_fwd, _inv_bwd)


def _delta_chunk(q4, k4, v4, sc, s4, rev):
    hps = HEADS_PER_STEP
    c = DN_CHUNK
    hd = DN_HEAD_DIM
    ri = lax.broadcasted_iota(jnp.int32, (c, c), 0)
    ci = lax.broadcasted_iota(jnp.int32, (c, c), 1)
    ahead = (ri - ci) * jnp.where(rev, -1, 1)
    incl = ahead >= 0
    strict = ahead > 0
    sct = sc.T

    def heads(x):
        return jnp.concatenate([x[None, :, j * hd:(j + 1) * hd] for j in range(hps)], axis=0)

    def col(off):
        return jnp.concatenate([sc[None, :, off + j:off + j + 1] for j in range(hps)], axis=0)

    def row(off):
        return jnp.concatenate([sct[None, off + j:off + j + 1, :] for j in range(hps)], axis=0)

    q, k, v = heads(q4), heads(k4), heads(v4)
    beta_c, gc_c, gt_c = col(0), col(hps), col(2 * hps)
    beta_r, gc_r = row(0), row(hps)
    dmat = jnp.where(incl, jnp.exp(jnp.where(incl, gc_c - gc_r, 0.0)), 0.0)
    kk = _bdot(k, k, _B_NT, exact=True)
    qk = _bdot(q, k, _B_NT)
    a = _inv_unit_tri(jnp.where(strict, beta_c * kk * dmat, 0.0))
    w = _bdot(a * (beta_r * jnp.exp(gc_r)), k, _B_NN, exact=True)
    u = _bdot(a * beta_r, v, _B_NN, exact=True)
    u2 = u - _bdot(w, s4, _B_NN)
    o = jnp.exp(gc_c) * _bdot(q, s4, _B_NN) + _bdot(qk * dmat, u2, _B_NN)
    ke = k * jnp.exp(gt_c - gc_c)
    s_new = jnp.exp(gt_c[:, 0:1, :]) * s4 + _bdot(ke, u2, _B_TN)
    return jnp.concatenate([o[j] for j in range(hps)], axis=1), s_new


def _chunk_index(d, n, nctx, ntot):
    rev_idx = jnp.where(n < nctx, nctx - 1 - n, ntot - 1 - (n - nctx))
    return jnp.where(d == 0, n, rev_idx)


def _delta_fwd(qkv, scg, rider=None):
    dd = _dims()
    r, dnw = dd["r"], dd["dnw"]
    hps = HEADS_PER_STEP
    nhg = DN_HEADS // hps
    bw = hps * DN_HEAD_DIM
    nctx, ntot = CTX_LEN // DN_CHUNK, r // DN_CHUNK

    def body(q_ref, k_ref, v_ref, sc_ref, o_ref, ss_ref, s_ref):
        d, n = pl.program_id(0), pl.program_id(2)

        @pl.when(n == 0)
        def _():
            s_ref[...] = jnp.zeros_like(s_ref)

        s4 = s_ref[...]
        ss_ref[0, 0] = s4
        o4, s4n = _delta_chunk(q_ref[...], k_ref[...], v_ref[...], sc_ref[0], s4, d == 1)
        o_ref[0] = o4
        s_ref[...] = s4n

    def rows(d, g, n):
        return _chunk_index(d, n, nctx, ntot)

    return _call_with_rider(
        body, rider, name="delta_fwd", grid=(2, nhg, ntot),
        out_shape=(jax.ShapeDtypeStruct((2, r, dnw), F32),
                   jax.ShapeDtypeStruct((2, ntot, DN_HEADS, DN_HEAD_DIM, DN_HEAD_DIM), F32)),
        in_specs=[pl.BlockSpec((DN_CHUNK, bw), lambda d, g, n: (rows(d, g, n), g)),
                  pl.BlockSpec((DN_CHUNK, bw), lambda d, g, n: (rows(d, g, n), nhg + g)),
                  pl.BlockSpec((DN_CHUNK, bw), lambda d, g, n: (rows(d, g, n), 2 * nhg + g)),
                  pl.BlockSpec((1, DN_CHUNK, LANE), lambda d, g, n: (d * nhg + g, rows(d, g, n), 0))],
        out_specs=(pl.BlockSpec((1, DN_CHUNK, bw), lambda d, g, n: (d, rows(d, g, n), g)),
                   pl.BlockSpec((1, 1, hps, DN_HEAD_DIM, DN_HEAD_DIM), lambda d, g, n: (d, n, g, 0, 0))),
        scratch_shapes=[pltpu.VMEM((hps, DN_HEAD_DIM, DN_HEAD_DIM), F32)],
        compiler_params=_cp(("arbitrary", "arbitrary", "arbitrary")), operands=(qkv, qkv, qkv, scg))


def _delta_bwd(qkv, scg, states, do, rider=None):
    dd = _dims()
    r, dnw = dd["r"], dd["dnw"]
    hps = HEADS_PER_STEP
    nhg = DN_HEADS // hps
    bw = hps * DN_HEAD_DIM
    nctx, ntot = CTX_LEN // DN_CHUNK, r // DN_CHUNK

    def body(q_ref, k_ref, v_ref, sc_ref, ss_ref, do_ref, dq_ref, dk_ref, dv_ref, dsc_ref, ds_ref):
        d, n2 = pl.program_id(0), pl.program_id(2)

        @pl.when(n2 == 0)
        def _():
            ds_ref[...] = jnp.zeros_like(ds_ref)

        fn = functools.partial(_delta_chunk, rev=(d == 1))
        _, vjp = jax.vjp(fn, q_ref[...], k_ref[...], v_ref[...], sc_ref[0], ss_ref[0, 0])
        dq, dk, dv, dsc, ds = vjp((do_ref[...], ds_ref[...]))
        dq_ref[0] = dq
        dk_ref[0] = dk
        dv_ref[0] = dv
        dsc_ref[0] = dsc
        ds_ref[...] = ds

    def rows(d, g, n2):
        return _chunk_index(d, ntot - 1 - n2, nctx, ntot)

    blk = lambda col: pl.BlockSpec((DN_CHUNK, bw), lambda d, g, n2: (rows(d, g, n2), col(g)))
    oblk = pl.BlockSpec((1, DN_CHUNK, bw), lambda d, g, n2: (d, rows(d, g, n2), g))
    scblk = pl.BlockSpec((1, DN_CHUNK, LANE), lambda d, g, n2: (d * nhg + g, rows(d, g, n2), 0))
    big = jax.ShapeDtypeStruct((2, r, dnw), F32)
    return _call_with_rider(
        body, rider, name="delta_bwd", grid=(2, nhg, ntot),
        out_shape=(big, big, big, jax.ShapeDtypeStruct(scg.shape, F32)),
        in_specs=[blk(lambda g: g), blk(lambda g: nhg + g), blk(lambda g: 2 * nhg + g), scblk,
                  pl.BlockSpec((1, 1, hps, DN_HEAD_DIM, DN_HEAD_DIM), lambda d, g, n2: (d, ntot - 1 - n2, g, 0, 0)),
                  blk(lambda g: g)],
        out_specs=(oblk, oblk, oblk, scblk),
        scratch_shapes=[pltpu.VMEM((hps, DN_HEAD_DIM, DN_HEAD_DIM), F32)],
        compiler_params=_cp(("arbitrary", "arbitrary", "arbitrary")), operands=(qkv, qkv, qkv, scg, states, do))


def _lru_gate_vals(xn, wr, wi, br, bi, lam):
    r = jax.nn.sigmoid(jnp.dot(xn, wr, preferred_element_type=F32) + br)
    i = jax.nn.sigmoid(jnp.dot(xn, wi, preferred_element_type=F32) + bi)
    sp = _softplus(-lam)
    la = -LRU_C * r * sp
    a = jnp.exp(la)
    mm = jnp.sqrt(_neg_expm1(2.0 * la))
    return r, i, sp, la, a, mm


def _lru_gates_fwd(xc, w_r, w_i, vecs):
    r, lw = xc.shape
    bd = lw // LRU_BLOCKS
    tm = _div(r, ROW_TILE, 8)

    def body(x_ref, wr_ref, wi_ref, v_ref, a_ref, b_ref):
        for n in range(LRU_BLOCKS):
            sl = slice(n * bd, (n + 1) * bd)
            xn = x_ref[:, sl]
            _, i, _, _, a, mm = _lru_gate_vals(xn, wr_ref[0, n], wi_ref[0, n], v_ref[0, 0:1, sl], v_ref[0, 1:2, sl],
                                               v_ref[0, 2:3, sl])
            a_ref[0, :, sl] = a
            b_ref[0, :, sl] = mm * (i * xn)

    wspec = pl.BlockSpec((1, LRU_BLOCKS, bd, bd), lambda d, i: (d, 0, 0, 0))
    out = jax.ShapeDtypeStruct((2, r, lw), F32)
    ospec = pl.BlockSpec((1, tm, lw), lambda d, i: (d, i, 0))
    return pl.pallas_call(
        body, name="lru_gates_fwd", grid=(2, r // tm), out_shape=(out, out),
        in_specs=[pl.BlockSpec((tm, lw), lambda d, i: (i, 0)), wspec, wspec,
                  pl.BlockSpec((1, 8, lw), lambda d, i: (d, 0, 0))],
        out_specs=(ospec, ospec), compiler_params=_cp(("arbitrary", "arbitrary")))(xc, w_r, w_i, vecs)


def _lru_gates_bwd(xc, w_r, w_i, vecs, da, db):
    r, lw = xc.shape
    bd = lw // LRU_BLOCKS
    tm = _div(r, ROW_TILE, 8)

    def body(x_ref, wr_ref, wi_ref, v_ref, da_ref, db_ref, dx_ref, dwr_ref, dwi_ref, dv_ref):
        @pl.when(pl.program_id(1) == 0)
        def _():
            dwr_ref[...] = jnp.zeros_like(dwr_ref)
            dwi_ref[...] = jnp.zeros_like(dwi_ref)
            dv_ref[...] = jnp.zeros_like(dv_ref)

        for n in range(LRU_BLOCKS):
            sl = slice(n * bd, (n + 1) * bd)
            xn = x_ref[:, sl]
            wr, wi = wr_ref[0, n], wi_ref[0, n]
            lam = v_ref[0, 2:3, sl]
            rr, ii, sp, la, a, mm = _lru_gate_vals(xn, wr, wi, v_ref[0, 0:1, sl], v_ref[0, 1:2, sl], lam)
            dav, dbv = da_ref[0, :, sl], db_ref[0, :, sl]
            dmm = dbv * (ii * xn)
            dla = dav * a - dmm * (a * a) / jnp.maximum(mm, 1e-30)
            di = dbv * mm * xn
            dxn = dbv * mm * ii
            dr = dla * (-LRU_C * sp)
            dsp = jnp.sum(dla * (-LRU_C * rr), axis=0, keepdims=True)
            dpr = dr * rr * (1.0 - rr)
            dpi = di * ii * (1.0 - ii)
            dxn = dxn + lax.dot_general(dpr, wr, (((1,), (1,)), ((), ())), preferred_element_type=F32)
            dxn = dxn + lax.dot_general(dpi, wi, (((1,), (1,)), ((), ())), preferred_element_type=F32)
            dx_ref[0, :, sl] = dxn
            dwr_ref[0, n] += lax.dot_general(xn, dpr, (((0,), (0,)), ((), ())), preferred_element_type=F32)
            dwi_ref[0, n] += lax.dot_general(xn, dpi, (((0,), (0,)), ((), ())), preferred_element_type=F32)
            dv_ref[0, 0:1, sl] += jnp.sum(dpr, axis=0, keepdims=True)
            dv_ref[0, 1:2, sl] += jnp.sum(dpi, axis=0, keepdims=True)
            dv_ref[0, 2:3, sl] += dsp * (-jax.nn.sigmoid(-lam))

    wspec = pl.BlockSpec((1, LRU_BLOCKS, bd, bd), lambda d, i: (d, 0, 0, 0))
    vspec = pl.BlockSpec((1, 8, lw), lambda d, i: (d, 0, 0))
    big = pl.BlockSpec((1, tm, lw), lambda d, i: (d, i, 0))
    return pl.pallas_call(
        body, name="lru_gates_bwd", grid=(2, r // tm),
        out_shape=(jax.ShapeDtypeStruct((2, r, lw), F32), jax.ShapeDtypeStruct(w_r.shape, F32),
                   jax.ShapeDtypeStruct(w_i.shape, F32), jax.ShapeDtypeStruct((2, 8, lw), F32)),
        in_specs=[pl.BlockSpec((tm, lw), lambda d, i: (i, 0)), wspec, wspec, vspec, big, big],
        out_specs=(big, wspec, wspec, vspec), compiler_params=_cp(("arbitrary", "arbitrary")))(xc, w_r, w_i, vecs, da, db)


def _scan_block(d, i, nctx, ntot):
    return _chunk_index(d, i, nctx, ntot)


def _lru_scan_fwd(a, b):
    _, r, lw = a.shape
    rb = _div(CTX_LEN, ROW_TILE, 8)
    nctx, ntot, ntile = CTX_LEN // rb, r // rb, rb // 8

    def body(a_ref, b_ref, h_ref, hp_ref, c_ref):
        d, i = pl.program_id(0), pl.program_id(1)

        @pl.when(i == 0)
        def _():
            c_ref[...] = jnp.zeros_like(c_ref)

        def sweep(rev):
            order = list(range(7, -1, -1)) if rev else list(range(8))

            def step(s, carry):
                t0 = pl.multiple_of((ntile - 1 - s if rev else s) * 8, 8)
                at = a_ref[0, pl.ds(t0, 8), :]
                bt = b_ref[0, pl.ds(t0, 8), :]
                rows, prevs = [None] * 8, [None] * 8
                h = carry
                for j in order:
                    prevs[j] = h
                    h = at[j:j + 1, :] * h + bt[j:j + 1, :]
                    rows[j] = h
                h_ref[0, pl.ds(t0, 8), :] = jnp.concatenate(rows, axis=0)
                hp_ref[0, pl.ds(t0, 8), :] = jnp.concatenate(prevs, axis=0)
                return h

            c_ref[...] = lax.fori_loop(0, ntile, step, c_ref[...])

        @pl.when(d == 0)
        def _():
            sweep(False)

        @pl.when(d == 1)
        def _():
            sweep(True)

    blk = pl.BlockSpec((1, rb, lw), lambda d, i: (d, _scan_block(d, i, nctx, ntot), 0))
    out = jax.ShapeDtypeStruct((2, r, lw), F32)
    return pl.pallas_call(
        body, name="lru_scan_fwd", grid=(2, ntot), out_shape=(out, out), in_specs=[blk, blk], out_specs=(blk, blk),
        scratch_shapes=[pltpu.VMEM((1, lw), F32)], compiler_params=_cp(("arbitrary", "arbitrary")))(a, b)


def _lru_scan_bwd(a, hprev, dh):
    _, r, lw = a.shape
    rb = _div(CTX_LEN, ROW_TILE, 8)
    nctx, ntot, ntile = CTX_LEN // rb, r // rb, rb // 8

    def body(a_ref, hp_ref, dh_ref, da_ref, db_ref, c_ref):
        d, i = pl.program_id(0), pl.program_id(1)

        @pl.when(i == 0)
        def _():
            c_ref[...] = jnp.zeros_like(c_ref)

        def sweep(rev):
            order = list(range(8)) if rev else list(range(7, -1, -1))

            def step(s, carry):
                t0 = pl.multiple_of((s if rev else ntile - 1 - s) * 8, 8)
                at = a_ref[0, pl.ds(t0, 8), :]
                hpt = hp_ref[0, pl.ds(t0, 8), :]
                dht = dh_ref[pl.ds(t0, 8), :]
                das, dbs = [None] * 8, [None] * 8
                c = carry
                for j in order:
                    lam = dht[j:j + 1, :] + c
                    das[j] = lam * hpt[j:j + 1, :]
                    dbs[j] = lam
                    c = at[j:j + 1, :] * lam
                da_ref[0, pl.ds(t0, 8), :] = jnp.concatenate(das, axis=0)
                db_ref[0, pl.ds(t0, 8), :] = jnp.concatenate(dbs, axis=0)
                return c

            c_ref[...] = lax.fori_loop(0, ntile, step, c_ref[...])

        @pl.when(d == 0)
        def _():
            sweep(False)

        @pl.when(d == 1)
        def _():
            sweep(True)

    def rows(d, i):
        return _scan_block(d, ntot - 1 - i, nctx, ntot)

    blk = pl.BlockSpec((1, rb, lw), lambda d, i: (d, rows(d, i), 0))
    out = jax.ShapeDtypeStruct((2, r, lw), F32)
    return pl.pallas_call(
        body, name="lru_scan_bwd", grid=(2, ntot), out_shape=(out, out),
        in_specs=[blk, blk, pl.BlockSpec((rb, lw), lambda d, i: (rows(d, i), 0))], out_specs=(blk, blk),
        scratch_shapes=[pltpu.VMEM((1, lw), F32)], compiler_params=_cp(("arbitrary", "arbitrary")))(a, hprev, dh)


def _mix_post_fwd(o, hsum, proj, dnw_vec):
    dd = _dims()
    dnw, lw, nmain = dd["dnw"], dd["lw"], dd["nmain"]
    tm = _div(SEQ, ROW_TILE, 16)
    coff = CTX_LEN // tm
    gcol = (3 * dnw + lw) // dnw
    lcol = (3 * dnw + lw + dnw) // lw
    hd = DN_HEAD_DIM

    def body(o_ref, h_ref, g_ref, lg_ref, w_ref, y_ref):
        wv = w_ref[...]
        for hh in range(DN_HEADS):
            sl = slice(hh * hd, (hh + 1) * hd)
            os_ = o_ref[0, :, sl] + o_ref[1, :, sl]
            rr = lax.rsqrt(jnp.mean(os_ * os_, axis=1, keepdims=True) + EPS)
            y_ref[:, sl] = (os_ * rr * wv * _silu(g_ref[:, sl])).astype(BF16)
        y_ref[:, dnw:] = (h_ref[...] * _silu(lg_ref[...])).astype(BF16)

    return pl.pallas_call(
        body, name="mix_post_fwd", grid=(SEQ // tm,), out_shape=jax.ShapeDtypeStruct((SEQ, dnw + lw), BF16),
        in_specs=[pl.BlockSpec((2, tm, dnw), lambda i: (0, i + coff, 0)), pl.BlockSpec((tm, lw), lambda i: (i, 0)),
                  pl.BlockSpec((tm, dnw), lambda i: (i + coff, gcol)), pl.BlockSpec((tm, lw), lambda i: (i + coff, lcol)),
                  pl.BlockSpec((1, hd), lambda i: (0, 0))],
        out_specs=pl.BlockSpec((tm, dnw + lw), lambda i: (i, 0)), compiler_params=_cp(("arbitrary",)))(
            o, hsum, proj, proj, dnw_vec)


def _mix_post_bwd(o, hsum, proj, dnw_vec, dy):
    dd = _dims()
    dnw, lw, r = dd["dnw"], dd["lw"], dd["r"]
    tm = _div(SEQ, ROW_TILE, 16)
    coff = CTX_LEN // tm
    gcol = (3 * dnw + lw) // dnw
    lcol = (3 * dnw + lw + dnw) // lw
    hd = DN_HEAD_DIM

    def body(o_ref, h_ref, g_ref, lg_ref, w_ref, dy_ref, do_ref, dh_ref, dg_ref, dw_ref):
        i = pl.program_id(0)

        @pl.when(i == 0)
        def _():
            dw_ref[...] = jnp.zeros_like(dw_ref)

        @pl.when(i < coff)
        def _():
            do_ref[...] = jnp.zeros_like(do_ref)
            dg_ref[...] = jnp.zeros_like(dg_ref)

        @pl.when(i >= coff)
        def _():
            wv = w_ref[...]
            dwacc = jnp.zeros((1, hd), F32)
            for hh in range(DN_HEADS):
                sl = slice(hh * hd, (hh + 1) * hd)
                os_ = o_ref[0, :, sl] + o_ref[1, :, sl]
                rr = lax.rsqrt(jnp.mean(os_ * os_, axis=1, keepdims=True) + EPS)
                nn = os_ * rr
                gv = g_ref[:, sl]
                dyv = dy_ref[:, sl]
                don = dyv * _silu(gv)
                dg_ref[:, sl] = (dyv * nn * wv * _dsilu(gv)).astype(BF16)
                dwacc = dwacc + jnp.sum(don * nn, axis=0, keepdims=True)
                dn_ = don * wv
                do_ref[:, sl] = rr * (dn_ - nn * jnp.mean(nn * dn_, axis=1, keepdims=True))
            dw_ref[...] += dwacc
            lgv = lg_ref[...]
            dyl = dy_ref[:, dnw:]
            dh_ref[...] = dyl * _silu(lgv)
            dg_ref[:, dnw:] = (dyl * h_ref[...] * _dsilu(lgv)).astype(BF16)

    lat = lambda i: jnp.maximum(i - coff, 0)
    return pl.pallas_call(
        body, name="mix_post_bwd", grid=(r // tm,),
        out_shape=(jax.ShapeDtypeStruct((r, dnw), F32), jax.ShapeDtypeStruct((SEQ, lw), F32),
                   jax.ShapeDtypeStruct((r, dnw + lw), BF16), jax.ShapeDtypeStruct((1, hd), F32)),
        in_specs=[pl.BlockSpec((2, tm, dnw), lambda i: (0, i, 0)), pl.BlockSpec((tm, lw), lambda i: (lat(i), 0)),
                  pl.BlockSpec((tm, dnw), lambda i: (i, gcol)), pl.BlockSpec((tm, lw), lambda i: (i, lcol)),
                  pl.BlockSpec((1, hd), lambda i: (0, 0)), pl.BlockSpec((tm, dnw + lw), lambda i: (lat(i), 0))],
        out_specs=(pl.BlockSpec((tm, dnw), lambda i: (i, 0)), pl.BlockSpec((tm, lw), lambda i: (lat(i), 0)),
                   pl.BlockSpec((tm, dnw + lw), lambda i: (i, 0)), pl.BlockSpec((1, hd), lambda i: (0, 0))),
        compiler_params=_cp(("arbitrary",)))(o, hsum, proj, proj, dnw_vec, dy)


def _sc_masks(tm, cw):
    t = lax.broadcasted_iota(jnp.int32, (tm, cw), 0) % GRID_W
    return {-1: t != 0, 1: t != GRID_W - 1}


def _sc_mix_fwd(p, convw):
    s, w4 = p.shape
    w = w4 // 4
    tm = _div(s, 512, GRID_W if GRID_W % 16 == 0 else 16 * GRID_W)
    cw = _div(w, 512, LANE)
    nc = w // cw
    taps = _taps(SC_CONV)

    def body(b_ref, c_ref, x_ref, g_ref, w_ref, y_ref):
        masks = _sc_masks(tm, cw)
        u = c_ref[...] * x_ref[...]
        z = _conv_fwd_val(u, w_ref[...], taps, masks)
        y_ref[...] = (b_ref[...] * z * _silu(g_ref[...])).astype(BF16)

    blk = lambda k: pl.BlockSpec((tm, cw), lambda c, i: (i, k * nc + c))
    return pl.pallas_call(
        body, name="sc_mix_fwd", grid=(nc, s // tm), out_shape=jax.ShapeDtypeStruct((s, w), BF16),
        in_specs=[blk(0), blk(1), blk(2), blk(3), pl.BlockSpec((8, cw), lambda c, i: (0, c))],
        out_specs=pl.BlockSpec((tm, cw), lambda c, i: (i, c)),
        compiler_params=_cp(("arbitrary", "arbitrary")))(p, p, p, p, convw)


def _sc_mix_bwd(p, convw, dy):
    s, w4 = p.shape
    w = w4 // 4
    tm = _div(s, 512, GRID_W if GRID_W % 16 == 0 else 16 * GRID_W)
    cw = _div(w, 512, LANE)
    nc = w // cw
    taps = _taps(SC_CONV)

    def body(b_ref, c_ref, x_ref, g_ref, w_ref, dy_ref, db_ref, dc_ref, dx_ref, dg_ref, dw_ref):
        @pl.when(pl.program_id(1) == 0)
        def _():
            dw_ref[...] = jnp.zeros_like(dw_ref)

        masks = _sc_masks(tm, cw)
        wv = w_ref[...]
        bv, cv, xv, gv, dyv = b_ref[...], c_ref[...], x_ref[...], g_ref[...], dy_ref[...]
        u = cv * xv
        z = _conv_fwd_val(u, wv, taps, masks)
        sg = _silu(gv)
        db_ref[...] = (dyv * z * sg).astype(BF16)
        dg_ref[...] = (dyv * bv * z * _dsilu(gv)).astype(BF16)
        dz = dyv * bv * sg
        du, dws = _conv_bwd_val(u, dz, wv, taps, masks)
        dc_ref[...] = (du * xv).astype(BF16)
        dx_ref[...] = (du * cv).astype(BF16)
        dw_ref[...] += jnp.concatenate(dws + [jnp.zeros((8 - len(dws), cw), F32)], axis=0)

    blk = lambda k: pl.BlockSpec((tm, cw), lambda c, i: (i, k * nc + c))
    oblk = pl.BlockSpec((tm, cw), lambda c, i: (i, c))
    wblk = pl.BlockSpec((8, cw), lambda c, i: (0, c))
    out = jax.ShapeDtypeStruct((s, w), BF16)
    return pl.pallas_call(
        body, name="sc_mix_bwd", grid=(nc, s // tm),
        out_shape=(out, out, out, out, jax.ShapeDtypeStruct((8, w), F32)),
        in_specs=[blk(0), blk(1), blk(2), blk(3), wblk, oblk],
        out_specs=(oblk, oblk, oblk, oblk, wblk),
        compiler_params=_cp(("arbitrary", "arbitrary")))(p, p, p, p, convw, dy)


def _adamw_slots(w, slots, m, v, name):
    rows, cols = w.shape
    nslot = slots.shape[0]
    tm = _div(rows, max(16, (64 * 4096) // cols), 16)

    def body(w_ref, s_ref, m_ref, v_ref, g_ref, dl_ref, nm_ref, nv_ref):
        g = s_ref[0].astype(F32)
        for j in range(1, nslot):
            g = g + s_ref[j].astype(F32)
        dl, m2, v2 = _adam(w_ref[...], g, m_ref[...], v_ref[...])
        g_ref[...] = g
        dl_ref[...] = dl
        nm_ref[...] = m2
        nv_ref[...] = v2

    blk = pl.BlockSpec((tm, cols), lambda i: (i, 0))
    out = jax.ShapeDtypeStruct((rows, cols), F32)
    return pl.pallas_call(
        body, name=name, grid=(rows // tm,), out_shape=(out, out, out, out),
        in_specs=[blk, pl.BlockSpec((nslot, tm, cols), lambda i: (0, i, 0)), blk, blk],
        out_specs=(blk, blk, blk, blk), compiler_params=_cp(("arbitrary",)))(w, slots, m, v)


def _adamw_flat(w, g, m, v, name):
    rows, cols = w.shape
    tm = _div(rows, 1024, 8)

    def body(w_ref, g_ref, m_ref, v_ref, dl_ref, nm_ref, nv_ref):
        dl, m2, v2 = _adam(w_ref[...], g_ref[...], m_ref[...], v_ref[...])
        dl_ref[...] = dl
        nm_ref[...] = m2
        nv_ref[...] = v2

    blk = pl.BlockSpec((tm, cols), lambda i: (i, 0))
    out = jax.ShapeDtypeStruct((rows, cols), F32)
    return pl.pallas_call(
        body, name=name, grid=(rows // tm,), out_shape=(out, out, out), in_specs=[blk] * 4, out_specs=(blk, blk, blk),
        compiler_params=_cp(("arbitrary",)))(w, g, m, v)


def _pack(arrs):
    flat = [a.reshape(-1).astype(F32) for a in arrs]
    sizes = [f.shape[0] for f in flat]
    total = sum(sizes)
    padded = -(-total // 1024) * 1024
    if padded > total:
        flat.append(jnp.zeros((padded - total,), F32))
    offs, o = [], 0
    for s_ in sizes:
        offs.append(o)
        o += s_
    return jnp.concatenate(flat).reshape(padded // LANE, LANE), offs


def _unpack(flat2d, offs, shapes):
    flat = flat2d.reshape(-1)
    out = []
    for o, shp in zip(offs, shapes):
        n = 1
        for s_ in shp:
            n *= s_
        out.append(flat[o:o + n].reshape(shp))
    return out


def _pad_rows(a, rows):
    return jnp.concatenate([a, jnp.zeros((rows - a.shape[0],) + a.shape[1:], a.dtype)], axis=0)


def _to_col_major(t):
    n, ch = t.shape
    return t.reshape(n // GRID_W, GRID_W, ch).swapaxes(0, 1).reshape(n, ch)


def _to_raster(t):
    n, ch = t.shape
    return t.reshape(GRID_W, n // GRID_W, ch).swapaxes(0, 1).reshape(n, ch)


def _scg_layout(sc):
    r = sc.shape[0]
    hps = HEADS_PER_STEP
    nhg = DN_HEADS // hps
    t = sc.reshape(r, 4, 32)[:, :, :2 * DN_HEADS].reshape(r, 4, 2, nhg, hps)
    t = jnp.stack([t[:, 0], t[:, 2], t[:, 3]], axis=1)
    t = t.transpose(2, 3, 0, 1, 4).reshape(2 * nhg, r, 3 * hps)
    return jnp.concatenate([t, jnp.zeros((2 * nhg, r, LANE - 3 * hps), F32)], axis=2)


def _scg_layout_bwd(dscg):
    r = dscg.shape[1]
    hps = HEADS_PER_STEP
    nhg = DN_HEADS // hps
    t = dscg[:, :, :3 * hps].reshape(2, nhg, r, 3, hps).transpose(2, 3, 0, 1, 4).reshape(r, 3, 2 * DN_HEADS)
    t = jnp.concatenate([t, jnp.zeros((r, 3, 32 - 2 * DN_HEADS), F32)], axis=2) if 2 * DN_HEADS < 32 else t
    z = jnp.zeros((r, 1, 32), F32)
    return jnp.concatenate([t[:, 0:1], z, t[:, 1:2], t[:, 2:3]], axis=1).reshape(r, LANE)


def _gather_cols(full8, width):
    return full8.transpose(1, 0, 2).reshape(full8.shape[1], NDEV * width)


def kernel(x, c, ctx, c_ctx, mod_w, mod_b, norm_w, ab_w_in, ab_qkv_conv, ab_a_log, ab_dt_bias, ab_dn_norm, ab_lru_conv_w, ab_lru_conv_b, ab_lru_w_r, ab_lru_b_r, ab_lru_w_i, ab_lru_b_i, ab_lru_lambda, ab_w_out, sc_w_in, sc_conv, sc_w_out, final_norm_w, loss_target, m_c_ctx, m_mod_w, m_mod_b, m_norm_w, m_ab_w_in, m_ab_qkv_conv, m_ab_a_log, m_ab_dt_bias, m_ab_dn_norm, m_ab_lru_conv_w, m_ab_lru_conv_b, m_ab_lru_w_r, m_ab_lru_b_r, m_ab_lru_w_i, m_ab_lru_b_i, m_ab_lru_lambda, m_ab_w_out, m_sc_w_in, m_sc_conv, m_sc_w_out, m_final_norm_w, v_c_ctx, v_mod_w, v_mod_b, v_norm_w, v_ab_w_in, v_ab_qkv_conv, v_ab_a_log, v_ab_dt_bias, v_ab_dn_norm, v_ab_lru_conv_w, v_ab_lru_conv_b, v_ab_lru_w_r, v_ab_lru_b_r, v_ab_lru_w_i, v_ab_lru_b_i, v_ab_lru_lambda, v_ab_w_out, v_sc_w_in, v_sc_conv, v_sc_w_out, v_final_norm_w):
    dd = _dims()
    d, dnw, lw, r = dd["d"], dd["dnw"], dd["lw"], dd["r"]
    off_lru, off_beta, ab_state, ab_in, nmain = dd["off_lru"], dd["off_beta"], dd["ab_state"], dd["ab_in"], dd["nmain"]
    hh2 = 2 * DN_HEADS
    me = _me()
    x = x[0]
    ctx = ctx[0]
    tgt = loss_target[0]

    small_sharded = [ab_qkv_conv[0], ab_lru_conv_w[0], ab_lru_b_r[0], ab_lru_b_i[0], ab_lru_lambda[0], sc_conv[0]]
    pk, offs0 = _pack([c] + small_sharded)
    (g0,) = _exchange([pk], "gather_small_in", scatter=False)
    parts = [_unpack(g0[j], offs0, [c.shape] + [a.shape for a in small_sharded]) for j in range(NDEV)]
    c_all = jnp.concatenate([p[0] for p in parts], axis=0)
    qkv_conv, lru_conv_w, lru_b_r, lru_b_i, lru_lam, sc_conv_f = [
        jnp.concatenate([p[i] for p in parts], axis=-1) for i in range(1, 7)]

    mcols = mod_w.shape[2]
    c16 = jnp.concatenate([c_all, c_ctx[None, :], jnp.zeros((7, d), F32)], axis=0)
    bias = lax.dynamic_slice_in_dim(mod_b, me * mcols, mcols, axis=1)[:, None, :]
    mod_loc = _mod_fwd(c16, mod_w, bias)
    (mod_g,) = _exchange([mod_loc], "gather_mod", scatter=False)
    mod_all = mod_g.transpose(1, 2, 0, 3).reshape(2, 16, NDEV * mcols)
    mine = lax.dynamic_slice_in_dim(mod_all, me, 1, axis=1)[:, 0]
    shift = [mine[l, None, 0:d] for l in range(2)]
    scale = [mine[l, None, d:2 * d] for l in range(2)]
    gate = [mine[l, None, 2 * d:3 * d] for l in range(2)]
    shift_c, scale_c = mod_all[0, 8, None, 0:d], mod_all[0, 8, None, d:2 * d]
    nw0, nw1 = norm_w[0][None, :], norm_w[1][None, :]

    (g_abin,) = _gather_two_level([ab_w_in[0].astype(BF16)], "gather_weights")
    w_full = _gather_cols(g_abin, ab_in // NDEV)
    w_main = jnp.concatenate([w_full[:, :off_beta], w_full[:, ab_state:]], axis=1)
    w_ba = jnp.concatenate([w_full[:, off_beta:ab_state], jnp.zeros((d, LANE - 2 * hh2), BF16)], axis=1)

    hn_c, rinv_c = _norm_fwd(ctx, nw0, shift_c, scale_c, "norm0_ctx")
    hn_x, rinv_x = _norm_fwd(x, nw0, shift[0], scale[0], "norm0_x")
    hn0 = jnp.concatenate([hn_c, hn_x], axis=0)
    proj, (g_about,) = _mm_nn(hn0, w_main[None], F32, "proj_main", rider=("gather", [ab_w_out[0].astype(BF16)]))
    w_about = g_about.reshape(dnw + lw, d)
    pba = _mm_nn(hn0, w_ba[None], F32, "proj_ba")

    convw8 = _pad_rows(qkv_conv, 8)
    qkv = _dn_pre_fwd(proj, convw8)
    zpad = jnp.zeros((32 - hh2,), F32)
    par = jnp.stack([jnp.concatenate([jnp.zeros((32,), F32), ab_dt_bias[0].reshape(-1), zpad, jnp.zeros((64,), F32)]),
                     jnp.concatenate([jnp.zeros((32,), F32), -jnp.exp(ab_a_log[0].reshape(-1)), zpad,
                                      jnp.zeros((64,), F32)])] + [jnp.zeros((LANE,), F32)] * 6)
    if hh2 < 32:
        pba_l = jnp.concatenate([pba[:, :hh2], jnp.zeros((r, 32 - hh2), F32), pba[:, hh2:2 * hh2],
                                 jnp.zeros((r, LANE - 32 - hh2), F32)], axis=1)
    else:
        pba_l = pba
    sc = _dn_gates_fwd(pba_l, par)
    scg = _scg_layout(sc)
    (o_dn, states), (g_scin,) = _delta_fwd(qkv, scg, rider=("gather", [sc_w_in[0].astype(BF16)]))

    xl = proj[:, off_lru:off_beta]
    xl_cat = jnp.concatenate([xl[:CTX_LEN], _to_col_major(xl[CTX_LEN:])], axis=0)
    lconv8 = jnp.concatenate([lru_conv_w, ab_lru_conv_b[0][None, :], jnp.zeros((3, lw), F32)], axis=0)
    xc = _lru_conv_fwd(xl_cat, lconv8)
    lvecs = jnp.stack([lru_b_r, lru_b_i, lru_lam] + [jnp.zeros_like(lru_lam)] * 5, axis=1)
    w_r, w_i = ab_lru_w_r[0], ab_lru_w_i[0]
    la_a, la_b = _lru_gates_fwd(xc, w_r, w_i, lvecs)
    h_dir, h_prev = _lru_scan_fwd(la_a, la_b)
    hsum = _to_raster(h_dir[0, CTX_LEN:] + h_dir[1, CTX_LEN:])

    dnn = ab_dn_norm[0][None, :]
    y0 = _mix_post_fwd(o_dn, hsum, proj, dnn)
    out0 = _mm_nn(y0, w_about[None], F32, "out0")

    x1, hn1, rinv1 = _resid_norm_fwd(x, out0, gate[0], nw1, shift[1], scale[1])
    p1, (g_scout,) = _mm_nn(hn1, g_scin, F32, "proj_sc", rider=("gather", [sc_w_out[0].astype(BF16)]))
    w_scout = g_scout.reshape(d, d)
    scw8 = _pad_rows(sc_conv_f, 8)
    y1 = _sc_mix_fwd(p1, scw8)
    out1 = _mm_nn(y1, w_scout[None], F32, "out1")

    fnw = final_norm_w[None, :]
    dx2, dout1, loss11, dfnw, dgate1 = _loss_head(x1, out1, gate[1], fnw, tgt)
    dy1 = _mm_nt(dout1, w_scout[None], F32, "dy1")
    gw_scout = _mm_tn(y1, dout1, BF16, "gw_scout")[0].reshape(NDEV, d // NDEV, d)
    dpb, dpc, dpx, dpg, dscw8 = _sc_mix_bwd(p1, scw8, dy1)
    dp1 = jnp.concatenate([dpb, dpc, dpx, dpg], axis=1)
    dhn1 = _mm_nt(dp1, g_scin, F32, "dhn1", tk=2048)
    gw_scin = _mm_tn(hn1, dp1, BF16, "gw_scin", oj=NDEV)
    dx1, dout0, dshift1, dscale1, dnw1, dgate0 = _norm_bwd(
        x1, rinv1, [dhn1], nw1, scale[1], "norm1_bwd", dx_in=dx2, resid=(out0, gate[0]))

    dy0, sib_sc = _mm_nt(dout0, w_about[None], F32, "dy0", rider=("swap", [gw_scin, gw_scout]))
    gw_about = _mm_tn(y0, dout0, BF16, "gw_about")[0].reshape(NDEV, (dnw + lw) // NDEV, d)
    do, dh_r, dgates, ddnn = _mix_post_bwd(o_dn, hsum, proj, dnn, dy0)

    early = [gw_about, gw_scin, gw_scout]
    sib_e = _sibling_swap([gw_about], "swap_grads_mid") + sib_sc
    parts_e = [_chip_partial(o_, s_, f"chip_partial_early{i}") for i, (o_, s_) in enumerate(zip(early, sib_e))]
    (dq, dk, dv, dscg), (s_about, s_scin, s_scout) = _delta_bwd(qkv, scg, states, do, rider=("chipx", parts_e))
    dsc = _scg_layout_bwd(dscg)
    dpba, dpar = _dn_gates_bwd(pba_l, par, dsc)
    dpq, dcw_q = _dn_pre_bwd(proj, convw8, dq, 0)
    dpk, dcw_k = _dn_pre_bwd(proj, convw8, dk, 1)
    dpv, dcw_v = _dn_pre_bwd(proj, convw8, dv, 2)

    dh_cat = jnp.concatenate([jnp.zeros((CTX_LEN, lw), F32), _to_col_major(dh_r)], axis=0)
    d_a, d_b = _lru_scan_bwd(la_a, h_prev, dh_cat)
    dxc, dw_r, dw_i, dlvecs = _lru_gates_bwd(xc, w_r, w_i, lvecs, d_a, d_b)
    dxl_cat, dlconv8 = _lru_conv_bwd(xl_cat, lconv8, dxc)
    dxl = jnp.concatenate([dxl_cat[:CTX_LEN], _to_raster(dxl_cat[CTX_LEN:])], axis=0).astype(BF16)

    dproj = jnp.concatenate([dpq, dpk, dpv, dxl, dgates], axis=1)
    if hh2 < 32:
        dpba_w = jnp.concatenate([dpba[:, :hh2], dpba[:, 32:32 + hh2], jnp.zeros((r, LANE - 2 * hh2), BF16)], axis=1)
    else:
        dpba_w = dpba
    gw_main = _mm_tn(hn0, dproj, BF16, "gw_main")[0]
    gw_ba = _mm_tn(hn0, dpba_w, BF16, "gw_ba")[0]
    gw_full = jnp.concatenate([gw_main[:, :off_beta], gw_ba[:, :2 * hh2], gw_main[:, off_beta:]], axis=1)
    gw_abin = gw_full.reshape(d, NDEV, ab_in // NDEV).transpose(1, 0, 2)
    (sib_l,) = _sibling_swap([gw_abin], "swap_grads_late")
    part_l = _chip_partial(gw_abin, sib_l, "chip_partial_late")
    dhn0_a, (s_abin,) = _mm_nt(dproj, w_main[None], F32, "dhn0_main", rider=("chipx", [part_l]))
    dhn0_b = _mm_nt(dpba_w, w_ba[None], F32, "dhn0_ba")

    nt = _div(SEQ, ROW_TILE // 2, 16)
    grad_x, dshift0, dscale0, dnw0_x = _norm_bwd(x, rinv_x, [dhn0_a, dhn0_b], nw0, scale[0], "norm0_bwd_x",
                                                 row_off=CTX_LEN // nt, dx_in=dx1)
    _, dshift_c, dscale_c, dnw0_c = _norm_bwd(ctx, rinv_c, [dhn0_a, dhn0_b], nw0, scale_c, "norm0_bwd_ctx")

    big = {}
    big["ab_w_in"] = _adamw_slots(ab_w_in[0], s_abin, m_ab_w_in[0], v_ab_w_in[0], "adamw_ab_w_in")
    big["ab_w_out"] = _adamw_slots(ab_w_out[0], s_about, m_ab_w_out[0], v_ab_w_out[0], "adamw_ab_w_out")
    big["sc_w_in"] = _adamw_slots(sc_w_in[0], s_scin, m_sc_w_in[0], v_sc_w_in[0], "adamw_sc_w_in")
    big["sc_w_out"] = _adamw_slots(sc_w_out[0], s_scout, m_sc_w_out[0], v_sc_w_out[0], "adamw_sc_w_out")

    dmod_own = jnp.stack([jnp.concatenate([dshift0, dscale0, dgate0], axis=1)[0],
                          jnp.concatenate([dshift1, dscale1, dgate1], axis=1)[0]])
    dmod_c = jnp.concatenate([dshift_c, dscale_c], axis=1)[0]
    g_qkv_conv = jnp.concatenate([dcw_q[:DN_CONV], dcw_k[:DN_CONV], dcw_v[:DN_CONV]], axis=1)
    small = [
        ("loss", loss11.reshape(1)),
        ("norm_w", jnp.concatenate([dnw0_x + dnw0_c, dnw1], axis=0)),
        ("dmod_own", dmod_own),
        ("dmod_c", dmod_c),
        ("qkv_conv", g_qkv_conv),
        ("a_log", dpar[1, 32:32 + hh2].reshape(2, DN_HEADS)),
        ("dt_bias", dpar[0, 32:32 + hh2].reshape(2, DN_HEADS)),
        ("dn_norm", ddnn[0]),
        ("lru_conv_w", dlconv8[:LRU_CONV]),
        ("lru_conv_b", dlconv8[LRU_CONV]),
        ("lru_w_r", dw_r),
        ("lru_w_i", dw_i),
        ("lru_b_r", dlvecs[:, 0]),
        ("lru_b_i", dlvecs[:, 1]),
        ("lru_lambda", dlvecs[:, 2]),
        ("sc_conv", dscw8[:SC_CONV]),
        ("final_norm_w", dfnw[0]),
    ]
    names = [n for n, _ in small]
    shapes = [a.shape for _, a in small]
    spk, soffs = _pack([a for _, a in small])
    (sg,) = _gather_two_level([spk], "gather_small_grads")
    tot = dict(zip(names, _unpack(_sum_slots(sg, "sum_small_grads"), soffs, shapes)))
    i_own = names.index("dmod_own")
    dmod_all = jnp.stack([_unpack(sg[j], soffs[i_own:i_own + 1], shapes[i_own:i_own + 1])[0] for j in range(NDEV)])

    dmc_full = jnp.concatenate([tot["dmod_c"], jnp.zeros((d,), F32)])
    rows9 = jnp.concatenate([dmod_all, jnp.stack([dmc_full, jnp.zeros((3 * d,), F32)])[None],
                             jnp.zeros((7, 2, 3 * d), F32)], axis=0)
    dm = lax.dynamic_slice_in_dim(rows9, me * mcols, mcols, axis=2).transpose(1, 0, 2)
    g_modw, dl_modw, nm_modw, nv_modw, gcc = _mod_bwd(c16, dm, mod_w, m_mod_w, v_mod_w)
    gcc_pk, _ = _pack([gcc[0, :, 0]])
    (gcc_g,) = _exchange([gcc_pk], "gather_c_ctx", scatter=False)
    gcc_sum = _sum_slots(gcc_g, "sum_c_ctx").reshape(-1)[:d]
    g_c_ctx_pre = gcc_sum

    grad_mod_b = tot["dmod_own"] + jnp.stack([dmc_full, jnp.zeros((3 * d,), F32)])

    def shard(a):
        wd = a.shape[-1] // NDEV
        return lax.dynamic_slice_in_dim(a, me * wd, wd, axis=a.ndim - 1)

    small_w = [
        ("c_ctx", c_ctx, None, m_c_ctx, v_c_ctx),
        ("mod_b", mod_b, grad_mod_b, m_mod_b, v_mod_b),
        ("norm_w", norm_w, tot["norm_w"], m_norm_w, v_norm_w),
        ("ab_qkv_conv", ab_qkv_conv, shard(tot["qkv_conv"])[None], m_ab_qkv_conv, v_ab_qkv_conv),
        ("ab_a_log", ab_a_log, tot["a_log"][None], m_ab_a_log, v_ab_a_log),
        ("ab_dt_bias", ab_dt_bias, tot["dt_bias"][None], m_ab_dt_bias, v_ab_dt_bias),
        ("ab_dn_norm", ab_dn_norm, tot["dn_norm"][None], m_ab_dn_norm, v_ab_dn_norm),
        ("ab_lru_conv_w", ab_lru_conv_w, shard(tot["lru_conv_w"])[None], m_ab_lru_conv_w, v_ab_lru_conv_w),
        ("ab_lru_conv_b", ab_lru_conv_b, tot["lru_conv_b"][None], m_ab_lru_conv_b, v_ab_lru_conv_b),
        ("ab_lru_w_r", ab_lru_w_r, tot["lru_w_r"][None], m_ab_lru_w_r, v_ab_lru_w_r),
        ("ab_lru_b_r", ab_lru_b_r, shard(tot["lru_b_r"])[None], m_ab_lru_b_r, v_ab_lru_b_r),
        ("ab_lru_w_i", ab_lru_w_i, tot["lru_w_i"][None], m_ab_lru_w_i, v_ab_lru_w_i),
        ("ab_lru_b_i", ab_lru_b_i, shard(tot["lru_b_i"])[None], m_ab_lru_b_i, v_ab_lru_b_i),
        ("ab_lru_lambda", ab_lru_lambda, shard(tot["lru_lambda"])[None], m_ab_lru_lambda, v_ab_lru_lambda),
        ("sc_conv", sc_conv, shard(tot["sc_conv"])[None], m_sc_conv, v_sc_conv),
        ("final_norm_w", final_norm_w, tot["final_norm_w"], m_final_norm_w, v_final_norm_w),
    ]
    sg_cc = jax.nn.sigmoid(c_ctx)
    g_c_ctx = g_c_ctx_pre * (sg_cc * (1.0 + c_ctx * (1.0 - sg_cc)))
    small_w[0] = ("c_ctx", c_ctx, g_c_ctx, m_c_ctx, v_c_ctx)
    wshapes = [w.shape for _, w, _, _, _ in small_w]
    wpk, woffs = _pack([w for _, w, _, _, _ in small_w])
    gpk, _ = _pack([g for _, _, g, _, _ in small_w])
    mpk, _ = _pack([m for _, _, _, m, _ in small_w])
    vpk, _ = _pack([v for _, _, _, _, v in small_w])
    dl_pk, nm_pk, nv_pk = _adamw_flat(wpk, gpk, mpk, vpk, "adamw_small")
    sm_g = {n: g for n, _, g, _, _ in small_w}
    sm_dl = dict(zip([n for n, *_ in small_w], _unpack(dl_pk, woffs, wshapes)))
    sm_nm = dict(zip([n for n, *_ in small_w], _unpack(nm_pk, woffs, wshapes)))
    sm_nv = dict(zip([n for n, *_ in small_w], _unpack(nv_pk, woffs, wshapes)))

    grads, deltas, new_m, new_v = {}, {}, {}, {}
    for n in sm_g:
        grads[n], deltas[n], new_m[n], new_v[n] = sm_g[n], sm_dl[n], sm_nm[n], sm_nv[n]
    grads["mod_w"], deltas["mod_w"], new_m["mod_w"], new_v["mod_w"] = g_modw, dl_modw, nm_modw, nv_modw
    for n in ("ab_w_in", "ab_w_out", "sc_w_in", "sc_w_out"):
        g, dl, m2, v2 = big[n]
        grads[n], deltas[n], new_m[n], new_v[n] = g[None], dl[None], m2[None], v2[None]

    order = ["c_ctx", "mod_w", "mod_b", "norm_w", "ab_w_in", "ab_qkv_conv", "ab_a_log", "ab_dt_bias", "ab_dn_norm",
             "ab_lru_conv_w", "ab_lru_conv_b", "ab_lru_w_r", "ab_lru_b_r", "ab_lru_w_i", "ab_lru_b_i", "ab_lru_lambda",
             "ab_w_out", "sc_w_in", "sc_conv", "sc_w_out", "final_norm_w"]
    loss = tot["loss"][0]
    return (loss, grad_x[None], *[grads[n] for n in order], *[deltas[n] for n in order],
            *[new_m[n] for n in order], *[new_v[n] for n in order])
```

```python
import functools

import jax
import jax.numpy as jnp
from jax import lax
from jax.experimental import pallas as pl
from jax.experimental.pallas import tpu as pltpu

F32 = jnp.float32
BF16 = jnp.bfloat16

NDEV = 8
D_MODEL = 4096
SEQ = 4096
CTX_LEN = 256
GRID_W = 64
EPS = 1e-6
DN_HEADS = 16
DN_HEAD_DIM = 128
DN_CONV = 4
DN_CHUNK = 64
LRU_WIDTH = 2048
LRU_BLOCKS = 16
LRU_CONV = 4
LRU_C = 8.0
SC_CONV = 3
ADAM_LR = 0.001
ADAM_B1 = 0.9
ADAM_B2 = 0.999
ADAM_EPS = 1e-08
ADAM_WD = 0.01
ADAM_STEP = 10

ROW_TILE = 256
HEADS_PER_STEP = 16
LANE = 128
VMEM_LIMIT = 56 * 1024 * 1024


def _dims():
    d = D_MODEL
    dnw = DN_HEADS * DN_HEAD_DIM
    lw = LRU_WIDTH
    off_lru = 3 * dnw
    off_beta = off_lru + lw
    ab_state = off_beta + 4 * DN_HEADS
    ab_in = ab_state + dnw + lw
    return dict(d=d, dnw=dnw, lw=lw, off_lru=off_lru, off_beta=off_beta, ab_state=ab_state, ab_in=ab_in,
                nmain=3 * dnw + lw + dnw + lw, r=CTX_LEN + SEQ, lbd=lw // LRU_BLOCKS)


def _div(n, pref, mult):
    best = None
    for t in range(mult, min(n, pref) + 1, mult):
        if n % t == 0:
            best = t
    return best if best is not None else n


def _cp(sem, vmem=VMEM_LIMIT):
    return pltpu.CompilerParams(dimension_semantics=sem, vmem_limit_bytes=vmem)


def _silu(z):
    return z * jax.nn.sigmoid(z)


def _dsilu(z):
    s = jax.nn.sigmoid(z)
    return s * (1.0 + z * (1.0 - s))


def _softplus(z):
    return jnp.maximum(z, 0.0) + jnp.log1p(jnp.exp(-jnp.abs(z)))


def _neg_expm1(z):
    series = -z * (1.0 + z * (0.5 + z * (1.0 / 6.0 + z * (1.0 / 24.0))))
    return jnp.where(z > -0.05, series, 1.0 - jnp.exp(z))


def _me():
    return 4 * lax.axis_index("x") + 2 * lax.axis_index("y") + lax.axis_index("c")


def _exchange(srcs, name, scatter):
    n = len(srcs)

    def body(*refs):
        src_refs, out_refs = refs[:n], refs[n:2 * n]
        send_sems, recv_sems, local_sems = refs[2 * n:]
        x, y, c = lax.axis_index("x"), lax.axis_index("y"), lax.axis_index("c")
        me = 4 * x + 2 * y + c
        copies = []
        for i in range(n):
            mine = src_refs[i].at[me] if scatter else src_refs[i]
            lc = pltpu.make_async_copy(mine, out_refs[i].at[me], local_sems.at[i])
            lc.start()
            copies.append(lc)
            for k in range(1, NDEV):
                px = 1 - x if (k >> 2) & 1 else x
                py = 1 - y if (k >> 1) & 1 else y
                pc = 1 - c if k & 1 else c
                peer = 4 * px + 2 * py + pc
                src = src_refs[i].at[peer] if scatter else src_refs[i]
                cp = pltpu.make_async_remote_copy(
                    src_ref=src, dst_ref=out_refs[i].at[me], send_sem=send_sems.at[i, k - 1],
                    recv_sem=recv_sems.at[i, k - 1], device_id=(px, py, pc), device_id_type=pl.DeviceIdType.MESH)
                cp.start()
                copies.append(cp)
        for cp in copies:
            cp.wait()

    any_spec = pl.BlockSpec(memory_space=pl.ANY)
    out_shape = tuple(jax.ShapeDtypeStruct(s.shape if scatter else (NDEV,) + s.shape, s.dtype) for s in srcs)
    outs = pl.pallas_call(
        body, name=name, out_shape=out_shape, in_specs=[any_spec] * n, out_specs=tuple([any_spec] * n),
        scratch_shapes=[pltpu.SemaphoreType.DMA((n, NDEV - 1)), pltpu.SemaphoreType.DMA((n, NDEV - 1)),
                        pltpu.SemaphoreType.DMA((n,))],
    )(*srcs)
    return list(outs)


def _flip(v, bit):
    return 1 - v if bit else v


def _gather_ops(src_refs, out_refs, send_sems, recv_sems, local_sems):
    n = len(src_refs)

    def setup():
        x, y, c = lax.axis_index("x"), lax.axis_index("y"), lax.axis_index("c")
        chips = [(_flip(x, k >> 1), _flip(y, k & 1)) for k in (1, 2, 3)]
        return x, y, c, chips

    def copy(i, sem, slot, to, src=None):
        return pltpu.make_async_remote_copy(
            src_ref=out_refs[i].at[slot] if src is None else src, dst_ref=out_refs[i].at[slot],
            send_sem=send_sems.at[i, sem], recv_sem=recv_sems.at[i, sem], device_id=to,
            device_id_type=pl.DeviceIdType.MESH)

    def own(i, me):
        return pltpu.make_async_copy(src_refs[i], out_refs[i].at[me], local_sems.at[i])

    def first(i, x, y, c, chips):
        me = 4 * x + 2 * y + c
        return [copy(i, 0, me, (x, y, 1 - c), src=src_refs[i])] + [
            copy(i, 1 + j, me, (px, py, c), src=src_refs[i]) for j, (px, py) in enumerate(chips)]

    def start():
        x, y, c, chips = setup()
        for i in range(n):
            own(i, 4 * x + 2 * y + c).start()
            for cp in first(i, x, y, c, chips):
                cp.start()

    def finish():
        x, y, c, chips = setup()
        here = (x, y, c)
        passed = []
        for i in range(n):
            for j, (px, py) in enumerate(chips):
                theirs = 4 * px + 2 * py + c
                copy(i, 1 + j, theirs, here).wait_recv()
                fwd = copy(i, 4 + j, theirs, (x, y, 1 - c))
                fwd.start()
                passed.append(fwd)
        for i in range(n):
            copy(i, 0, 4 * x + 2 * y + (1 - c), here).wait_recv()
            for j, (px, py) in enumerate(chips):
                copy(i, 4 + j, 4 * px + 2 * py + (1 - c), here).wait_recv()
        for i in range(n):
            for cp in first(i, x, y, c, chips):
                cp.wait_send()
        for cp in passed:
            cp.wait_send()
        for i in range(n):
            own(i, 4 * x + 2 * y + c).wait()

    return start, finish


def _chipx_ops(src_refs, out_refs, send_sems, recv_sems, local_sems):
    n = len(src_refs)

    def copies():
        x, y, c = lax.axis_index("x"), lax.axis_index("y"), lax.axis_index("c")
        mine = 2 * x + y
        res = []
        for i in range(n):
            res.append(pltpu.make_async_copy(src_refs[i].at[mine], out_refs[i].at[mine], local_sems.at[i]))
            for k in (1, 2, 3):
                px, py = _flip(x, k >> 1), _flip(y, k & 1)
                res.append(pltpu.make_async_remote_copy(
                    src_ref=src_refs[i].at[2 * px + py], dst_ref=out_refs[i].at[mine], send_sem=send_sems.at[i, k - 1],
                    recv_sem=recv_sems.at[i, k - 1], device_id=(px, py, c), device_id_type=pl.DeviceIdType.MESH))
        return res

    def start():
        for cp in copies():
            cp.start()

    def finish():
        for cp in copies():
            cp.wait()

    return start, finish


def _swap_ops(src_refs, out_refs, send_sems, recv_sems):
    n = len(src_refs)

    def copies():
        x, y, c = lax.axis_index("x"), lax.axis_index("y"), lax.axis_index("c")
        return [pltpu.make_async_remote_copy(
            src_ref=src_refs[i].at[2 * q + (1 - c)], dst_ref=out_refs[i].at[q], send_sem=send_sems.at[i, q],
            recv_sem=recv_sems.at[i, q], device_id=(x, y, 1 - c), device_id_type=pl.DeviceIdType.MESH)
            for i in range(n) for q in range(NDEV // 2)]

    def start():
        for cp in copies():
            cp.start()

    def finish():
        for cp in copies():
            cp.wait()

    return start, finish


def _rider(kind, arrays):
    n = len(arrays)
    any_spec = pl.BlockSpec(memory_space=pl.ANY)
    dma = pltpu.SemaphoreType.DMA
    if kind == "gather":
        out_shapes = [jax.ShapeDtypeStruct((NDEV,) + a.shape, a.dtype) for a in arrays]
        scratch, make = [dma((n, NDEV - 1)), dma((n, NDEV - 1)), dma((n,))], _gather_ops
    elif kind == "swap":
        out_shapes = [jax.ShapeDtypeStruct((NDEV // 2,) + a.shape[1:], a.dtype) for a in arrays]
        scratch, make = [dma((n, NDEV // 2)), dma((n, NDEV // 2))], _swap_ops
    else:
        out_shapes = [jax.ShapeDtypeStruct(a.shape, a.dtype) for a in arrays]
        scratch, make = [dma((n, NDEV // 2 - 1)), dma((n, NDEV // 2 - 1)), dma((n,))], _chipx_ops
    return list(arrays), [any_spec] * n, out_shapes, [any_spec] * n, scratch, make


def _exchange_call(kind, arrays, name):
    n = len(arrays)
    ins, in_specs, out_shapes, out_specs, scratch, make = _rider(kind, arrays)

    def body(*refs):
        start, finish = make(refs[:n], refs[n:2 * n], *refs[2 * n:])
        start()
        finish()

    outs = pl.pallas_call(body, name=name, out_shape=tuple(out_shapes), in_specs=in_specs, out_specs=tuple(out_specs),
                          scratch_shapes=scratch)(*ins)
    return list(outs)


def _gather_two_level(srcs, name):
    return _exchange_call("gather", srcs, name)


def _call_with_rider(body, rider, *, name, grid, out_shape, in_specs, out_specs, scratch_shapes, compiler_params,
                     operands):
    if rider is None:
        outs = pl.pallas_call(body, name=name, grid=grid, out_shape=tuple(out_shape), in_specs=list(in_specs),
                              out_specs=tuple(out_specs), scratch_shapes=list(scratch_shapes),
                              compiler_params=compiler_params)(*operands)
        return list(outs), []
    r_ins, r_in_specs, r_out_shapes, r_out_specs, r_scratch, make = _rider(*rider)
    ni, no, ns, nr = len(in_specs), len(out_shape), len(scratch_shapes), len(r_ins)

    def riding(*refs):
        ins, rins = refs[:ni], refs[ni:ni + nr]
        outs, routs = refs[ni + nr:ni + nr + no], refs[ni + nr + no:ni + nr + no + nr]
        scr, rscr = refs[ni + 2 * nr + no:ni + 2 * nr + no + ns], refs[ni + 2 * nr + no + ns:]
        start, finish = make(rins, routs, *rscr)
        ids = [pl.program_id(a) for a in range(len(grid))]
        first, last = ids[0] == 0, ids[0] == grid[0] - 1
        for a in range(1, len(grid)):
            first = jnp.logical_and(first, ids[a] == 0)
            last = jnp.logical_and(last, ids[a] == grid[a] - 1)

        @pl.when(first)
        def _():
            start()

        body(*ins, *outs, *scr)

        @pl.when(last)
        def _():
            finish()

    outs = pl.pallas_call(
        riding, name=name, grid=grid, out_shape=tuple(out_shape) + tuple(r_out_shapes),
        in_specs=list(in_specs) + r_in_specs, out_specs=tuple(out_specs) + tuple(r_out_specs),
        scratch_shapes=list(scratch_shapes) + r_scratch, compiler_params=compiler_params)(*operands, *r_ins)
    return list(outs[:no]), list(outs[no:])


def _sibling_swap(srcs, name):
    return _exchange_call("swap", srcs, name)


def _chip_partial(own, sib, name):
    _, rows, cols = own.shape
    tm = _div(rows, max(16, (128 * 4096) // cols), 16)

    def body(c_ref, o_ref, s_ref, p_ref):
        p_ref[0] = (o_ref[0].astype(F32) + s_ref[0].astype(F32)).astype(BF16)

    core = lax.axis_index("c").astype(jnp.int32).reshape(1)
    grid_spec = pltpu.PrefetchScalarGridSpec(
        num_scalar_prefetch=1, grid=(NDEV // 2, rows // tm),
        in_specs=[pl.BlockSpec((1, tm, cols), lambda q, i, c_ref: (2 * q + c_ref[0], i, 0)),
                  pl.BlockSpec((1, tm, cols), lambda q, i, c_ref: (q, i, 0))],
        out_specs=pl.BlockSpec((1, tm, cols), lambda q, i, c_ref: (q, i, 0)))
    return pl.pallas_call(
        body, name=name, grid_spec=grid_spec, out_shape=jax.ShapeDtypeStruct(sib.shape, BF16),
        compiler_params=_cp(("arbitrary", "arbitrary")))(core, own, sib)


def _chip_exchange(parts, name):
    return _exchange_call("chipx", parts, name)


def _sum_slots(g, name):
    _, rows, cols = g.shape
    tm = _div(rows, 1024, 8)

    def body(g_ref, o_ref):
        acc = g_ref[0]
        for j in range(1, NDEV):
            acc = acc + g_ref[j]
        o_ref[...] = acc

    return pl.pallas_call(
        body, name=name, grid=(rows // tm,), out_shape=jax.ShapeDtypeStruct((rows, cols), F32),
        in_specs=[pl.BlockSpec((NDEV, tm, cols), lambda i: (0, i, 0))],
        out_specs=pl.BlockSpec((tm, cols), lambda i: (i, 0)), compiler_params=_cp(("arbitrary",)))(g)


def _mm_nn(a, b3, out_dtype, name, tm=1152, tn=1024, rider=None):
    m_, k_ = a.shape
    j_, _, nj = b3.shape
    tm = _div(m_, tm, 16)
    tn = _div(nj, tn, LANE)
    per = nj // tn

    def body(a_ref, b_ref, o_ref):
        o_ref[...] = jnp.dot(a_ref[...], b_ref[0], preferred_element_type=F32).astype(o_ref.dtype)

    outs, rode = _call_with_rider(
        body, rider, name=name, grid=(j_ * per, m_ // tm),
        out_shape=(jax.ShapeDtypeStruct((m_, j_ * nj), out_dtype),),
        in_specs=[pl.BlockSpec((tm, k_), lambda n, m: (m, 0)),
                  pl.BlockSpec((1, k_, tn), lambda n, m: (n // per, 0, n % per))],
        out_specs=(pl.BlockSpec((tm, tn), lambda n, m: (m, n)),), scratch_shapes=[],
        compiler_params=_cp(("arbitrary", "arbitrary")), operands=(a, b3))
    return outs[0] if rider is None else (outs[0], rode)


def _mm_nt(a, b3, out_dtype, name, tm=1152, tn=1024, tk=2048, rider=None):
    m_, k_ = a.shape
    j_, n_, kj = b3.shape
    tm = _div(m_, tm, 16)
    tn = _div(n_, tn, LANE)
    tk = _div(kj, tk, LANE)
    perk = kj // tk
    nk = k_ // tk

    def body(a_ref, b_ref, o_ref, acc_ref):
        k = pl.program_id(2)

        @pl.when(k == 0)
        def _():
            acc_ref[...] = jnp.zeros_like(acc_ref)

        acc_ref[...] += lax.dot_general(a_ref[...], b_ref[0], (((1,), (1,)), ((), ())), preferred_element_type=F32)

        @pl.when(k == nk - 1)
        def _():
            o_ref[...] = acc_ref[...].astype(o_ref.dtype)

    outs, rode = _call_with_rider(
        body, rider, name=name, grid=(n_ // tn, m_ // tm, nk),
        out_shape=(jax.ShapeDtypeStruct((m_, n_), out_dtype),),
        in_specs=[pl.BlockSpec((tm, tk), lambda n, m, k: (m, k)),
                  pl.BlockSpec((1, tn, tk), lambda n, m, k: (k // perk, n, k % perk))],
        out_specs=(pl.BlockSpec((tm, tn), lambda n, m, k: (m, n)),),
        scratch_shapes=[pltpu.VMEM((tm, tn), F32)],
        compiler_params=_cp(("arbitrary", "arbitrary", "arbitrary")), operands=(a, b3))
    return outs[0] if rider is None else (outs[0], rode)


def _mm_tn(a, b, out_dtype, name, oj=1, tm=1024, tn=512):
    k_, m_ = a.shape
    _, n_ = b.shape
    nj = n_ // oj
    tm = _div(m_, tm, LANE)
    tn = _div(nj, tn, LANE)
    per = nj // tn

    def body(a_ref, b_ref, o_ref):
        o_ref[0] = lax.dot_general(a_ref[...], b_ref[...], (((0,), (0,)), ((), ())),
                                   preferred_element_type=F32).astype(o_ref.dtype)

    return pl.pallas_call(
        body, name=name, grid=(m_ // tm, n_ // tn), out_shape=jax.ShapeDtypeStruct((oj, m_, nj), out_dtype),
        in_specs=[pl.BlockSpec((k_, tm), lambda m, n: (0, m)), pl.BlockSpec((k_, tn), lambda m, n: (0, n))],
        out_specs=pl.BlockSpec((1, tm, tn), lambda m, n: (n // per, m, n % per)),
        compiler_params=_cp(("arbitrary", "arbitrary")))(a, b)


def _mod_fwd(c16, w, bias):
    _, d, cols = w.shape
    tn = _div(cols, 512, LANE)

    def body(c_ref, w_ref, b_ref, o_ref):
        sc = _silu(c_ref[...])
        o_ref[0] = jnp.dot(sc, w_ref[0], preferred_element_type=F32) + b_ref[0]

    return pl.pallas_call(
        body, name="mod_fwd", grid=(2, cols // tn), out_shape=jax.ShapeDtypeStruct((2, 16, cols), F32),
        in_specs=[pl.BlockSpec((16, d), lambda l, n: (0, 0)), pl.BlockSpec((1, d, tn), lambda l, n: (l, 0, n)),
                  pl.BlockSpec((1, 1, tn), lambda l, n: (l, 0, n))],
        out_specs=pl.BlockSpec((1, 16, tn), lambda l, n: (l, 0, n)),
        compiler_params=_cp(("arbitrary", "arbitrary")))(c16, w, bias)


def _adam(w, g, m, v):
    m2 = ADAM_B1 * m + (1.0 - ADAM_B1) * g
    v2 = ADAM_B2 * v + (1.0 - ADAM_B2) * (g * g)
    m_hat = m2 / (1.0 - ADAM_B1 ** ADAM_STEP)
    v_hat = v2 / (1.0 - ADAM_B2 ** ADAM_STEP)
    delta = -ADAM_LR * (m_hat / (jnp.sqrt(v_hat) + ADAM_EPS) + ADAM_WD * w)
    return delta, m2, v2


def _mod_bwd(c16, dm, w, m, v):
    _, d, cols = w.shape
    tm = _div(d, 256, 8)

    def body(c_ref, dm_ref, w_ref, m_ref, v_ref, g_ref, dl_ref, nm_ref, nv_ref, gc_ref):
        sc = _silu(c_ref[...])
        g = lax.dot_general(sc, dm_ref[0], (((0,), (0,)), ((), ())), preferred_element_type=F32,
                            precision=lax.Precision.HIGHEST)
        wv = w_ref[0]
        dl, m2, v2 = _adam(wv, g, m_ref[0], v_ref[0])
        g_ref[0] = g
        dl_ref[0] = dl
        nm_ref[0] = m2
        nv_ref[0] = v2
        gc_ref[0] = jnp.sum(wv * dm_ref[0, 8:9, :], axis=1, keepdims=True)

    big = pl.BlockSpec((1, tm, cols), lambda l, i: (l, i, 0))
    shp = jax.ShapeDtypeStruct(w.shape, F32)
    return pl.pallas_call(
        body, name="mod_bwd", grid=(2, d // tm),
        out_shape=(shp, shp, shp, shp, jax.ShapeDtypeStruct((2, d, 1), F32)),
        in_specs=[pl.BlockSpec((16, tm), lambda l, i: (0, i)), pl.BlockSpec((1, 16, cols), lambda l, i: (l, 0, 0)),
                  big, big, big],
        out_specs=(big, big, big, big, pl.BlockSpec((1, tm, 1), lambda l, i: (l, i, 0))),
        compiler_params=_cp(("arbitrary", "arbitrary")))(c16, dm, w, m, v)


def _row_spec(tm, d, off=0):
    return pl.BlockSpec((tm, d), lambda i: (i + off, 0))


def _vec_spec(d):
    return pl.BlockSpec((1, d), lambda i: (0, 0))


def _norm_fwd(x, nw, shift, scale, name):
    n, d = x.shape
    tm = _div(n, ROW_TILE, 16)

    def body(x_ref, nw_ref, sh_ref, sc_ref, hn_ref, r_ref):
        xv = x_ref[...]
        r = lax.rsqrt(jnp.mean(xv * xv, axis=1, keepdims=True) + EPS)
        hn_ref[...] = (xv * r * nw_ref[...] * (1.0 + sc_ref[...]) + sh_ref[...]).astype(BF16)
        r_ref[...] = r

    return pl.pallas_call(
        body, name=name, grid=(n // tm,),
        out_shape=(jax.ShapeDtypeStruct((n, d), BF16), jax.ShapeDtypeStruct((n, 1), F32)),
        in_specs=[_row_spec(tm, d), _vec_spec(d), _vec_spec(d), _vec_spec(d)],
        out_specs=(_row_spec(tm, d), _row_spec(tm, 1)), compiler_params=_cp(("arbitrary",)))(x, nw, shift, scale)


def _resid_norm_fwd(x, out0, gate, nw, shift, scale):
    n, d = x.shape
    tm = _div(n, ROW_TILE, 16)

    def body(x_ref, o_ref, g_ref, nw_ref, sh_ref, sc_ref, x1_ref, hn_ref, r_ref):
        xv = x_ref[...] + g_ref[...] * o_ref[...]
        r = lax.rsqrt(jnp.mean(xv * xv, axis=1, keepdims=True) + EPS)
        x1_ref[...] = xv
        hn_ref[...] = (xv * r * nw_ref[...] * (1.0 + sc_ref[...]) + sh_ref[...]).astype(BF16)
        r_ref[...] = r

    return pl.pallas_call(
        body, name="resid_norm1", grid=(n // tm,),
        out_shape=(jax.ShapeDtypeStruct((n, d), F32), jax.ShapeDtypeStruct((n, d), BF16),
                   jax.ShapeDtypeStruct((n, 1), F32)),
        in_specs=[_row_spec(tm, d), _row_spec(tm, d), _vec_spec(d), _vec_spec(d), _vec_spec(d), _vec_spec(d)],
        out_specs=(_row_spec(tm, d), _row_spec(tm, d), _row_spec(tm, 1)),
        compiler_params=_cp(("arbitrary",)))(x, out0, gate, nw, shift, scale)


def _loss_head(x1, out1, gate1, fnw, tgt):
    n, d = x1.shape
    tm = _div(n, ROW_TILE, 16)
    nsteps = n // tm

    def body(x_ref, o_ref, g_ref, w_ref, t_ref, dx_ref, do_ref, loss_ref, dw_ref, dg_ref, lacc_ref):
        i = pl.program_id(0)

        @pl.when(i == 0)
        def _():
            lacc_ref[...] = jnp.zeros_like(lacc_ref)
            dw_ref[...] = jnp.zeros_like(dw_ref)
            dg_ref[...] = jnp.zeros_like(dg_ref)

        ov = o_ref[...]
        gv = g_ref[...]
        wv = w_ref[...]
        xv = x_ref[...] + gv * ov
        r = lax.rsqrt(jnp.mean(xv * xv, axis=1, keepdims=True) + EPS)
        xh = xv * r
        diff = xh * wv - t_ref[...]
        lacc_ref[...] += jnp.sum(diff * diff, axis=0, keepdims=True)
        dy = diff * (1.0 / d)
        dw_ref[...] += jnp.sum(dy * xh, axis=0, keepdims=True)
        dxh = dy * wv
        dx = r * (dxh - xh * jnp.mean(xh * dxh, axis=1, keepdims=True))
        dx_ref[...] = dx
        dg_ref[...] += jnp.sum(dx * ov, axis=0, keepdims=True)
        do_ref[...] = (gv * dx).astype(BF16)

        @pl.when(i == nsteps - 1)
        def _():
            loss_ref[...] = (0.5 / d) * jnp.sum(lacc_ref[...], axis=1, keepdims=True)

    vec = jax.ShapeDtypeStruct((1, d), F32)
    return pl.pallas_call(
        body, name="loss_head", grid=(nsteps,),
        out_shape=(jax.ShapeDtypeStruct((n, d), F32), jax.ShapeDtypeStruct((n, d), BF16),
                   jax.ShapeDtypeStruct((1, 1), F32), vec, vec),
        in_specs=[_row_spec(tm, d), _row_spec(tm, d), _vec_spec(d), _vec_spec(d), _row_spec(tm, d)],
        out_specs=(_row_spec(tm, d), _row_spec(tm, d), pl.BlockSpec((1, 1), lambda i: (0, 0)), _vec_spec(d),
                   _vec_spec(d)),
        scratch_shapes=[pltpu.VMEM((1, d), F32)], compiler_params=_cp(("arbitrary",)))(x1, out1, gate1, fnw, tgt)


def _norm_bwd(x, rinv, dhns, nw, scale, name, row_off=0, dx_in=None, resid=None):
    n, d = x.shape
    tm = _div(n, ROW_TILE // 2, 16)
    nd = len(dhns)
    has_in = dx_in is not None
    has_res = resid is not None

    def body(*refs):
        it = iter(refs)
        x_ref, r_ref = next(it), next(it)
        dh_refs = [next(it) for _ in range(nd)]
        nw_ref, sc_ref = next(it), next(it)
        dxin_ref = next(it) if has_in else None
        o_ref, g_ref = (next(it), next(it)) if has_res else (None, None)
        dx_ref = next(it)
        do_ref = next(it) if has_res else None
        dsh_ref, dsc_ref, dnw_ref = next(it), next(it), next(it)
        dg_ref = next(it) if has_res else None
        i = pl.program_id(0)

        @pl.when(i == 0)
        def _():
            dsh_ref[...] = jnp.zeros_like(dsh_ref)
            dsc_ref[...] = jnp.zeros_like(dsc_ref)
            dnw_ref[...] = jnp.zeros_like(dnw_ref)
            if has_res:
                dg_ref[...] = jnp.zeros_like(dg_ref)

        dh = dh_refs[0][...]
        for rr in dh_refs[1:]:
            dh = dh + rr[...]
        r = r_ref[...]
        xh = x_ref[...] * r
        nwv = nw_ref[...]
        mod = 1.0 + sc_ref[...]
        dsh_ref[...] += jnp.sum(dh, axis=0, keepdims=True)
        dhx = dh * xh
        dsc_ref[...] += jnp.sum(dhx * nwv, axis=0, keepdims=True)
        dnw_ref[...] += jnp.sum(dhx * mod, axis=0, keepdims=True)
        dxh = dh * (nwv * mod)
        dx = r * (dxh - xh * jnp.mean(xh * dxh, axis=1, keepdims=True))
        if has_in:
            dx = dx + dxin_ref[...]
        dx_ref[...] = dx
        if has_res:
            dg_ref[...] += jnp.sum(dx * o_ref[...], axis=0, keepdims=True)
            do_ref[...] = (g_ref[...] * dx).astype(BF16)

    ins = [x, rinv] + list(dhns) + [nw, scale]
    in_specs = [_row_spec(tm, d), _row_spec(tm, 1)] + [_row_spec(tm, d, row_off)] * nd + [_vec_spec(d), _vec_spec(d)]
    if has_in:
        ins.append(dx_in)
        in_specs.append(_row_spec(tm, d))
    if has_res:
        ins += list(resid)
        in_specs += [_row_spec(tm, d), _vec_spec(d)]
    vec = jax.ShapeDtypeStruct((1, d), F32)
    out_shape = [jax.ShapeDtypeStruct((n, d), F32)]
    out_specs = [_row_spec(tm, d)]
    if has_res:
        out_shape.append(jax.ShapeDtypeStruct((n, d), BF16))
        out_specs.append(_row_spec(tm, d))
    out_shape += [vec, vec, vec]
    out_specs += [_vec_spec(d)] * 3
    if has_res:
        out_shape.append(vec)
        out_specs.append(_vec_spec(d))
    return pl.pallas_call(
        body, name=name, grid=(n // tm,), out_shape=tuple(out_shape), in_specs=in_specs, out_specs=tuple(out_specs),
        compiler_params=_cp(("arbitrary",)))(*ins)


def _seg_masks(rows, width, seg_bounds, offs):
    t = lax.broadcasted_iota(jnp.int32, (rows, width), 0)
    s0 = jnp.zeros_like(t)
    s1 = jnp.full_like(t, rows)
    for (a, b) in seg_bounds:
        inside = (t >= a) & (t < b)
        s0 = jnp.where(inside, a, s0)
        s1 = jnp.where(inside, b, s1)
    return {off: ((t + off >= s0) & (t + off < s1)) for off in offs}


def _shift_rows(v, off, valid):
    rows = v.shape[0]
    return jnp.where(valid, pltpu.roll(v, (-off) % rows, axis=0), 0.0)


def _conv_fwd_val(xv, wv, taps, masks):
    acc = None
    for tap, off in taps:
        xs = xv if off == 0 else _shift_rows(xv, off, masks[off])
        term = wv[tap:tap + 1, :] * xs
        acc = term if acc is None else acc + term
    return acc


def _conv_bwd_val(xv, dacc, wv, taps, masks):
    dx = None
    dws = []
    for tap, off in taps:
        xs = xv if off == 0 else _shift_rows(xv, off, masks[off])
        dws.append(jnp.sum(dacc * xs, axis=0, keepdims=True))
        ds = dacc if off == 0 else _shift_rows(dacc, -off, masks[-off])
        term = wv[tap:tap + 1, :] * ds
        dx = term if dx is None else dx + term
    return dx, dws


def _taps(k):
    left = k // 2
    return [(j, j - left) for j in range(k)]


def _seq_bounds():
    return [(0, CTX_LEN), (CTX_LEN, CTX_LEN + SEQ)]


def _dn_pre_fwd(proj, convw):
    dd = _dims()
    r = dd["r"]
    nstrip = 3 * DN_HEADS
    taps = _taps(DN_CONV)
    offs = [o for _, o in taps if o != 0]

    def body(x_ref, w_ref, o_ref):
        kind = pl.program_id(0) // DN_HEADS
        masks = _seg_masks(r, DN_HEAD_DIM, _seq_bounds(), offs)
        y = _silu(_conv_fwd_val(x_ref[...], w_ref[...], taps, masks))
        inv = lax.rsqrt(jnp.sum(y * y, axis=1, keepdims=True) + EPS)
        scale = jnp.where(kind == 0, inv * (DN_HEAD_DIM ** -0.5), jnp.where(kind == 1, inv, 1.0))
        o_ref[...] = y * scale

    return pl.pallas_call(
        body, name="dn_pre_fwd", grid=(nstrip,), out_shape=jax.ShapeDtypeStruct((r, 3 * dd["dnw"]), F32),
        in_specs=[pl.BlockSpec((r, DN_HEAD_DIM), lambda j: (0, j)), pl.BlockSpec((8, DN_HEAD_DIM), lambda j: (0, j))],
        out_specs=pl.BlockSpec((r, DN_HEAD_DIM), lambda j: (0, j)), compiler_params=_cp(("arbitrary",)))(proj, convw)


def _dn_pre_bwd(proj, convw, dqkv, kind):
    dd = _dims()
    r = dd["r"]
    taps = _taps(DN_CONV)
    offs = sorted({o for _, o in taps if o != 0} | {-o for _, o in taps if o != 0})
    cscale = DN_HEAD_DIM ** -0.5 if kind == 0 else 1.0

    def body(x_ref, w_ref, d_ref, dx_ref, dw_ref):
        masks = _seg_masks(r, DN_HEAD_DIM, _seq_bounds(), offs)
        xv = x_ref[...]
        wv = w_ref[...]
        acc = _conv_fwd_val(xv, wv, taps, masks)
        dout = d_ref[0] + d_ref[1]
        if kind == 2:
            dy = dout
        else:
            y = _silu(acc)
            inv = lax.rsqrt(jnp.sum(y * y, axis=1, keepdims=True) + EPS)
            dy = cscale * inv * (dout - y * (inv * inv) * jnp.sum(dout * y, axis=1, keepdims=True))
        dacc = dy * _dsilu(acc)
        dx, dws = _conv_bwd_val(xv, dacc, wv, taps, masks)
        dx_ref[...] = dx.astype(BF16)
        dw_ref[...] = jnp.concatenate(dws + [jnp.zeros((8 - len(dws), DN_HEAD_DIM), F32)], axis=0)

    base = kind * DN_HEADS
    return pl.pallas_call(
        body, name=f"dn_pre_bwd{kind}", grid=(DN_HEADS,),
        out_shape=(jax.ShapeDtypeStruct((r, dd["dnw"]), BF16), jax.ShapeDtypeStruct((8, dd["dnw"]), F32)),
        in_specs=[pl.BlockSpec((r, DN_HEAD_DIM), lambda j: (0, base + j)),
                  pl.BlockSpec((8, DN_HEAD_DIM), lambda j: (0, base + j)),
                  pl.BlockSpec((2, r, DN_HEAD_DIM), lambda j: (0, 0, j))],
        out_specs=(pl.BlockSpec((r, DN_HEAD_DIM), lambda j: (0, j)), pl.BlockSpec((8, DN_HEAD_DIM), lambda j: (0, j))),
        compiler_params=_cp(("arbitrary",)))(proj, convw, dqkv)


def _lru_conv_fwd(xl, convw):
    r, lw = xl.shape
    taps = _taps(LRU_CONV)
    offs = [o for _, o in taps if o != 0]

    def body(x_ref, w_ref, o_ref):
        masks = _seg_masks(r, LANE, _seq_bounds(), offs)
        wv = w_ref[...]
        o_ref[...] = _conv_fwd_val(x_ref[...], wv, taps, masks) + wv[4:5, :]

    return pl.pallas_call(
        body, name="lru_conv_fwd", grid=(lw // LANE,), out_shape=jax.ShapeDtypeStruct((r, lw), F32),
        in_specs=[pl.BlockSpec((r, LANE), lambda j: (0, j)), pl.BlockSpec((8, LANE), lambda j: (0, j))],
        out_specs=pl.BlockSpec((r, LANE), lambda j: (0, j)), compiler_params=_cp(("arbitrary",)))(xl, convw)


def _lru_conv_bwd(xl, convw, dxc):
    r, lw = xl.shape
    taps = _taps(LRU_CONV)
    offs = sorted({o for _, o in taps if o != 0} | {-o for _, o in taps if o != 0})

    def body(x_ref, w_ref, d_ref, dx_ref, dw_ref):
        masks = _seg_masks(r, LANE, _seq_bounds(), offs)
        dacc = d_ref[0] + d_ref[1]
        dx, dws = _conv_bwd_val(x_ref[...], dacc, w_ref[...], taps, masks)
        dx_ref[...] = dx
        db = jnp.sum(dacc, axis=0, keepdims=True)
        dw_ref[...] = jnp.concatenate(dws + [db, jnp.zeros((3, LANE), F32)], axis=0)

    return pl.pallas_call(
        body, name="lru_conv_bwd", grid=(lw // LANE,),
        out_shape=(jax.ShapeDtypeStruct((r, lw), F32), jax.ShapeDtypeStruct((8, lw), F32)),
        in_specs=[pl.BlockSpec((r, LANE), lambda j: (0, j)), pl.BlockSpec((8, LANE), lambda j: (0, j)),
                  pl.BlockSpec((2, r, LANE), lambda j: (0, 0, j))],
        out_specs=(pl.BlockSpec((r, LANE), lambda j: (0, j)), pl.BlockSpec((8, LANE), lambda j: (0, j))),
        compiler_params=_cp(("arbitrary",)))(xl, convw, dxc)


def _tri(rev_lane, n, m):
    r = lax.broadcasted_iota(jnp.int32, (n, n), 0)
    c = lax.broadcasted_iota(jnp.int32, (n, n), 1)
    return (r >= c).astype(F32), (r <= c).astype(F32)


def _hdot(a, b):
    return jnp.dot(a, b, preferred_element_type=F32, precision=lax.Precision.HIGHEST)


def _dn_gates_fwd(pba, par):
    r = pba.shape[0]
    nch = r // DN_CHUNK
    hh = 2 * DN_HEADS

    def body(x_ref, p_ref, o_ref):
        lo, up = _tri(None, DN_CHUNK, DN_CHUNK)
        ones = jnp.ones((DN_CHUNK, DN_CHUNK), F32)
        lane = lax.broadcasted_iota(jnp.int32, (DN_CHUNK, LANE), 1)
        group = lane // 32
        rev = (lane % 32) >= DN_HEADS
        dtb = p_ref[0:1, :]
        nea = p_ref[1:2, :]

        def step(i, carry):
            rows = pl.ds(pl.multiple_of(i * DN_CHUNK, DN_CHUNK), DN_CHUNK)
            xv = x_ref[rows, :]
            beta = jax.nn.sigmoid(xv)
            g = jnp.where(group == 1, nea * _softplus(xv + dtb), 0.0)
            gc = jnp.where(rev, _hdot(up, g), _hdot(lo, g))
            gt = _hdot(ones, g)
            out = jnp.where(group == 0, beta, jnp.where(group == 1, g, 0.0))
            out = out + jnp.where(group == 2, pltpu.roll(gc, 32, axis=1), 0.0)
            out = out + jnp.where(group == 3, pltpu.roll(gt, 64, axis=1), 0.0)
            o_ref[rows, :] = out
            return carry

        lax.fori_loop(0, nch, step, 0)

    return pl.pallas_call(body, name="dn_gates_fwd", out_shape=jax.ShapeDtypeStruct((r, LANE), F32),
                          compiler_params=_cp(None))(pba, par)


def _dn_gates_bwd(pba, par, dsc):
    r = pba.shape[0]
    nch = r // DN_CHUNK

    def body(x_ref, p_ref, d_ref, dx_ref, dp_ref, acc_ref):
        lo, up = _tri(None, DN_CHUNK, DN_CHUNK)
        ones = jnp.ones((DN_CHUNK, DN_CHUNK), F32)
        lane = lax.broadcasted_iota(jnp.int32, (DN_CHUNK, LANE), 1)
        group = lane // 32
        rev = (lane % 32) >= DN_HEADS
        dtb = p_ref[0:1, :]
        nea = p_ref[1:2, :]
        acc_ref[...] = jnp.zeros_like(acc_ref)

        def step(i, carry):
            rows = pl.ds(pl.multiple_of(i * DN_CHUNK, DN_CHUNK), DN_CHUNK)
            xv = x_ref[rows, :]
            dv = d_ref[rows, :]
            beta = jax.nn.sigmoid(xv)
            z = xv + dtb
            sp = _softplus(z)
            dgc = jnp.where(group == 1, pltpu.roll(dv, LANE - 32, axis=1), 0.0)
            dgt = jnp.where(group == 1, pltpu.roll(dv, LANE - 64, axis=1), 0.0)
            dg = jnp.where(rev, _hdot(lo, dgc), _hdot(up, dgc)) + _hdot(ones, dgt)
            dg = dg + jnp.where(group == 1, dv, 0.0)
            dal = dg * nea * jax.nn.sigmoid(z)
            dpb = dv * beta * (1.0 - beta)
            dx_ref[rows, :] = jnp.where(group == 0, dpb, jnp.where(group == 1, dal, 0.0)).astype(BF16)
            acc_ref[0:1, :] += jnp.sum(jnp.where(group == 1, dal, 0.0), axis=0, keepdims=True)
            acc_ref[1:2, :] += jnp.sum(jnp.where(group == 1, dg * nea * sp, 0.0), axis=0, keepdims=True)
            return carry

        lax.fori_loop(0, nch, step, 0)
        dp_ref[...] = acc_ref[...]

    return pl.pallas_call(
        body, name="dn_gates_bwd",
        out_shape=(jax.ShapeDtypeStruct((r, LANE), BF16), jax.ShapeDtypeStruct((8, LANE), F32)),
        scratch_shapes=[pltpu.VMEM((8, LANE), F32)], compiler_params=_cp(None))(pba, par, dsc)


_B_NN = (((2,), (1,)), ((0,), (0,)))
_B_NT = (((2,), (2,)), ((0,), (0,)))
_B_TN = (((1,), (1,)), ((0,), (0,)))


def _bdot(a, b, dims, exact=False):
    return lax.dot_general(a, b, dims, preferred_element_type=F32,
                           precision=lax.Precision.HIGH if exact else None)


@jax.custom_vjp
def _inv_unit_tri(l):
    n = l.shape[-1]
    ri = lax.broadcasted_iota(jnp.int32, (n, n), 0)
    ci = lax.broadcasted_iota(jnp.int32, (n, n), 1)
    x = jnp.broadcast_to((ri == ci).astype(F32), l.shape)
    inner = 1
    for size in (4, 16, n):
        so, si = size.bit_length() - 1, inner.bit_length() - 1
        same_outer = (ri >> so) == (ci >> so)
        same_inner = (ri >> si) == (ci >> si)
        part = jnp.where(same_outer, jnp.where(same_inner, 0.0, l), 0.0)
        m = part if inner == 1 else _bdot(x, part, _B_NN, exact=True)
        y = x
        for _ in range(size // inner - 1):
            y = x - _bdot(m, y, _B_NN, exact=True)
        x = y
        inner = size
    return x


def _inv_fwd(l):
    a = _inv_unit_tri(l)
    return a, a


def _inv_bwd(a, da):
    t = _bdot(a, da, _B_TN, exact=True)
    return (-_bdot(t, a, _B_NT, exact=True),)


_inv_unit_tri.defvjp(_inv_fwd, _inv_bwd)


def _delta_chunk(q4, k4, v4, sc, s4, rev):
    hps = HEADS_PER_STEP
    c = DN_CHUNK
    hd = DN_HEAD_DIM
    ri = lax.broadcasted_iota(jnp.int32, (c, c), 0)
    ci = lax.broadcasted_iota(jnp.int32, (c, c), 1)
    ahead = (ri - ci) * jnp.where(rev, -1, 1)
    incl = ahead >= 0
    strict = ahead > 0
    sct = sc.T

    def heads(x):
        return jnp.concatenate([x[None, :, j * hd:(j + 1) * hd] for j in range(hps)], axis=0)

    def col(off):
        return jnp.concatenate([sc[None, :, off + j:off + j + 1] for j in range(hps)], axis=0)

    def row(off):
        return jnp.concatenate([sct[None, off + j:off + j + 1, :] for j in range(hps)], axis=0)

    q, k, v = heads(q4), heads(k4), heads(v4)
    beta_c, gc_c, gt_c = col(0), col(hps), col(2 * hps)
    beta_r, gc_r = row(0), row(hps)
    dmat = jnp.where(incl, jnp.exp(jnp.where(incl, gc_c - gc_r, 0.0)), 0.0)
    kk = _bdot(k, k, _B_NT, exact=True)
    qk = _bdot(q, k, _B_NT)
    a = _inv_unit_tri(jnp.where(strict, beta_c * kk * dmat, 0.0))
    w = _bdot(a * (beta_r * jnp.exp(gc_r)), k, _B_NN, exact=True)
    u = _bdot(a * beta_r, v, _B_NN, exact=True)
    u2 = u - _bdot(w, s4, _B_NN)
    o = jnp.exp(gc_c) * _bdot(q, s4, _B_NN) + _bdot(qk * dmat, u2, _B_NN)
    ke = k * jnp.exp(gt_c - gc_c)
    s_new = jnp.exp(gt_c[:, 0:1, :]) * s4 + _bdot(ke, u2, _B_TN)
    return jnp.concatenate([o[j] for j in range(hps)], axis=1), s_new


def _chunk_index(d, n, nctx, ntot):
    rev_idx = jnp.where(n < nctx, nctx - 1 - n, ntot - 1 - (n - nctx))
    return jnp.where(d == 0, n, rev_idx)


def _delta_fwd(qkv, scg, rider=None):
    dd = _dims()
    r, dnw = dd["r"], dd["dnw"]
    hps = HEADS_PER_STEP
    nhg = DN_HEADS // hps
    bw = hps * DN_HEAD_DIM
    nctx, ntot = CTX_LEN // DN_CHUNK, r // DN_CHUNK

    def body(q_ref, k_ref, v_ref, sc_ref, o_ref, ss_ref, s_ref):
        d, n = pl.program_id(0), pl.program_id(2)

        @pl.when(n == 0)
        def _():
            s_ref[...] = jnp.zeros_like(s_ref)

        s4 = s_ref[...]
        ss_ref[0, 0] = s4
        o4, s4n = _delta_chunk(q_ref[...], k_ref[...], v_ref[...], sc_ref[0], s4, d == 1)
        o_ref[0] = o4
        s_ref[...] = s4n

    def rows(d, g, n):
        return _chunk_index(d, n, nctx, ntot)

    return _call_with_rider(
        body, rider, name="delta_fwd", grid=(2, nhg, ntot),
        out_shape=(jax.ShapeDtypeStruct((2, r, dnw), F32),
                   jax.ShapeDtypeStruct((2, ntot, DN_HEADS, DN_HEAD_DIM, DN_HEAD_DIM), F32)),
        in_specs=[pl.BlockSpec((DN_CHUNK, bw), lambda d, g, n: (rows(d, g, n), g)),
                  pl.BlockSpec((DN_CHUNK, bw), lambda d, g, n: (rows(d, g, n), nhg + g)),
                  pl.BlockSpec((DN_CHUNK, bw), lambda d, g, n: (rows(d, g, n), 2 * nhg + g)),
                  pl.BlockSpec((1, DN_CHUNK, LANE), lambda d, g, n: (d * nhg + g, rows(d, g, n), 0))],
        out_specs=(pl.BlockSpec((1, DN_CHUNK, bw), lambda d, g, n: (d, rows(d, g, n), g)),
                   pl.BlockSpec((1, 1, hps, DN_HEAD_DIM, DN_HEAD_DIM), lambda d, g, n: (d, n, g, 0, 0))),
        scratch_shapes=[pltpu.VMEM((hps, DN_HEAD_DIM, DN_HEAD_DIM), F32)],
        compiler_params=_cp(("arbitrary", "arbitrary", "arbitrary")), operands=(qkv, qkv, qkv, scg))


def _delta_bwd(qkv, scg, states, do, rider=None):
    dd = _dims()
    r, dnw = dd["r"], dd["dnw"]
    hps = HEADS_PER_STEP
    nhg = DN_HEADS // hps
    bw = hps * DN_HEAD_DIM
    nctx, ntot = CTX_LEN // DN_CHUNK, r // DN_CHUNK

    def body(q_ref, k_ref, v_ref, sc_ref, ss_ref, do_ref, dq_ref, dk_ref, dv_ref, dsc_ref, ds_ref):
        d, n2 = pl.program_id(0), pl.program_id(2)

        @pl.when(n2 == 0)
        def _():
            ds_ref[...] = jnp.zeros_like(ds_ref)

        fn = functools.partial(_delta_chunk, rev=(d == 1))
        _, vjp = jax.vjp(fn, q_ref[...], k_ref[...], v_ref[...], sc_ref[0], ss_ref[0, 0])
        dq, dk, dv, dsc, ds = vjp((do_ref[...], ds_ref[...]))
        dq_ref[0] = dq
        dk_ref[0] = dk
        dv_ref[0] = dv
        dsc_ref[0] = dsc
        ds_ref[...] = ds

    def rows(d, g, n2):
        return _chunk_index(d, ntot - 1 - n2, nctx, ntot)

    blk = lambda col: pl.BlockSpec((DN_CHUNK, bw), lambda d, g, n2: (rows(d, g, n2), col(g)))
    oblk = pl.BlockSpec((1, DN_CHUNK, bw), lambda d, g, n2: (d, rows(d, g, n2), g))
    scblk = pl.BlockSpec((1, DN_CHUNK, LANE), lambda d, g, n2: (d * nhg + g, rows(d, g, n2), 0))
    big = jax.ShapeDtypeStruct((2, r, dnw), F32)
    return _call_with_rider(
        body, rider, name="delta_bwd", grid=(2, nhg, ntot),
        out_shape=(big, big, big, jax.ShapeDtypeStruct(scg.shape, F32)),
        in_specs=[blk(lambda g: g), blk(lambda g: nhg + g), blk(lambda g: 2 * nhg + g), scblk,
                  pl.BlockSpec((1, 1, hps, DN_HEAD_DIM, DN_HEAD_DIM), lambda d, g, n2: (d, ntot - 1 - n2, g, 0, 0)),
                  blk(lambda g: g)],
        out_specs=(oblk, oblk, oblk, scblk),
        scratch_shapes=[pltpu.VMEM((hps, DN_HEAD_DIM, DN_HEAD_DIM), F32)],
        compiler_params=_cp(("arbitrary", "arbitrary", "arbitrary")), operands=(qkv, qkv, qkv, scg, states, do))


def _lru_gate_vals(xn, wr, wi, br, bi, lam):
    r = jax.nn.sigmoid(jnp.dot(xn, wr, preferred_element_type=F32) + br)
    i = jax.nn.sigmoid(jnp.dot(xn, wi, preferred_element_type=F32) + bi)
    sp = _softplus(-lam)
    la = -LRU_C * r * sp
    a = jnp.exp(la)
    mm = jnp.sqrt(_neg_expm1(2.0 * la))
    return r, i, sp, la, a, mm


def _lru_gates_fwd(xc, w_r, w_i, vecs):
    r, lw = xc.shape
    bd = lw // LRU_BLOCKS
    tm = _div(r, ROW_TILE, 8)

    def body(x_ref, wr_ref, wi_ref, v_ref, a_ref, b_ref):
        for n in range(LRU_BLOCKS):
            sl = slice(n * bd, (n + 1) * bd)
            xn = x_ref[:, sl]
            _, i, _, _, a, mm = _lru_gate_vals(xn, wr_ref[0, n], wi_ref[0, n], v_ref[0, 0:1, sl], v_ref[0, 1:2, sl],
                                               v_ref[0, 2:3, sl])
            a_ref[0, :, sl] = a
            b_ref[0, :, sl] = mm * (i * xn)

    wspec = pl.BlockSpec((1, LRU_BLOCKS, bd, bd), lambda d, i: (d, 0, 0, 0))
    out = jax.ShapeDtypeStruct((2, r, lw), F32)
    ospec = pl.BlockSpec((1, tm, lw), lambda d, i: (d, i, 0))
    return pl.pallas_call(
        body, name="lru_gates_fwd", grid=(2, r // tm), out_shape=(out, out),
        in_specs=[pl.BlockSpec((tm, lw), lambda d, i: (i, 0)), wspec, wspec,
                  pl.BlockSpec((1, 8, lw), lambda d, i: (d, 0, 0))],
        out_specs=(ospec, ospec), compiler_params=_cp(("arbitrary", "arbitrary")))(xc, w_r, w_i, vecs)


def _lru_gates_bwd(xc, w_r, w_i, vecs, da, db):
    r, lw = xc.shape
    bd = lw // LRU_BLOCKS
    tm = _div(r, ROW_TILE, 8)

    def body(x_ref, wr_ref, wi_ref, v_ref, da_ref, db_ref, dx_ref, dwr_ref, dwi_ref, dv_ref):
        @pl.when(pl.program_id(1) == 0)
        def _():
            dwr_ref[...] = jnp.zeros_like(dwr_ref)
            dwi_ref[...] = jnp.zeros_like(dwi_ref)
            dv_ref[...] = jnp.zeros_like(dv_ref)

        for n in range(LRU_BLOCKS):
            sl = slice(n * bd, (n + 1) * bd)
            xn = x_ref[:, sl]
            wr, wi = wr_ref[0, n], wi_ref[0, n]
            lam = v_ref[0, 2:3, sl]
            rr, ii, sp, la, a, mm = _lru_gate_vals(xn, wr, wi, v_ref[0, 0:1, sl], v_ref[0, 1:2, sl], lam)
            dav, dbv = da_ref[0, :, sl], db_ref[0, :, sl]
            dmm = dbv * (ii * xn)
            dla = dav * a - dmm * (a * a) / jnp.maximum(mm, 1e-30)
            di = dbv * mm * xn
            dxn = dbv * mm * ii
            dr = dla * (-LRU_C * sp)
            dsp = jnp.sum(dla * (-LRU_C * rr), axis=0, keepdims=True)
            dpr = dr * rr * (1.0 - rr)
            dpi = di * ii * (1.0 - ii)
            dxn = dxn + lax.dot_general(dpr, wr, (((1,), (1,)), ((), ())), preferred_element_type=F32)
            dxn = dxn + lax.dot_general(dpi, wi, (((1,), (1,)), ((), ())), preferred_element_type=F32)
            dx_ref[0, :, sl] = dxn
            dwr_ref[0, n] += lax.dot_general(xn, dpr, (((0,), (0,)), ((), ())), preferred_element_type=F32)
            dwi_ref[0, n] += lax.dot_general(xn, dpi, (((0,), (0,)), ((), ())), preferred_element_type=F32)
            dv_ref[0, 0:1, sl] += jnp.sum(dpr, axis=0, keepdims=True)
            dv_ref[0, 1:2, sl] += jnp.sum(dpi, axis=0, keepdims=True)
            dv_ref[0, 2:3, sl] += dsp * (-jax.nn.sigmoid(-lam))

    wspec = pl.BlockSpec((1, LRU_BLOCKS, bd, bd), lambda d, i: (d, 0, 0, 0))
    vspec = pl.BlockSpec((1, 8, lw), lambda d, i: (d, 0, 0))
    big = pl.BlockSpec((1, tm, lw), lambda d, i: (d, i, 0))
    return pl.pallas_call(
        body, name="lru_gates_bwd", grid=(2, r // tm),
        out_shape=(jax.ShapeDtypeStruct((2, r, lw), F32), jax.ShapeDtypeStruct(w_r.shape, F32),
                   jax.ShapeDtypeStruct(w_i.shape, F32), jax.ShapeDtypeStruct((2, 8, lw), F32)),
        in_specs=[pl.BlockSpec((tm, lw), lambda d, i: (i, 0)), wspec, wspec, vspec, big, big],
        out_specs=(big, wspec, wspec, vspec), compiler_params=_cp(("arbitrary", "arbitrary")))(xc, w_r, w_i, vecs, da, db)


def _scan_block(d, i, nctx, ntot):
    return _chunk_index(d, i, nctx, ntot)


def _lru_scan_fwd(a, b):
    _, r, lw = a.shape
    rb = _div(CTX_LEN, ROW_TILE, 8)
    nctx, ntot, ntile = CTX_LEN // rb, r // rb, rb // 8

    def body(a_ref, b_ref, h_ref, hp_ref, c_ref):
        d, i = pl.program_id(0), pl.program_id(1)

        @pl.when(i == 0)
        def _():
            c_ref[...] = jnp.zeros_like(c_ref)

        def sweep(rev):
            order = list(range(7, -1, -1)) if rev else list(range(8))

            def step(s, carry):
                t0 = pl.multiple_of((ntile - 1 - s if rev else s) * 8, 8)
                at = a_ref[0, pl.ds(t0, 8), :]
                bt = b_ref[0, pl.ds(t0, 8), :]
                rows, prevs = [None] * 8, [None] * 8
                h = carry
                for j in order:
                    prevs[j] = h
                    h = at[j:j + 1, :] * h + bt[j:j + 1, :]
                    rows[j] = h
                h_ref[0, pl.ds(t0, 8), :] = jnp.concatenate(rows, axis=0)
                hp_ref[0, pl.ds(t0, 8), :] = jnp.concatenate(prevs, axis=0)
                return h

            c_ref[...] = lax.fori_loop(0, ntile, step, c_ref[...])

        @pl.when(d == 0)
        def _():
            sweep(False)

        @pl.when(d == 1)
        def _():
            sweep(True)

    blk = pl.BlockSpec((1, rb, lw), lambda d, i: (d, _scan_block(d, i, nctx, ntot), 0))
    out = jax.ShapeDtypeStruct((2, r, lw), F32)
    return pl.pallas_call(
        body, name="lru_scan_fwd", grid=(2, ntot), out_shape=(out, out), in_specs=[blk, blk], out_specs=(blk, blk),
        scratch_shapes=[pltpu.VMEM((1, lw), F32)], compiler_params=_cp(("arbitrary", "arbitrary")))(a, b)


def _lru_scan_bwd(a, hprev, dh):
    _, r, lw = a.shape
    rb = _div(CTX_LEN, ROW_TILE, 8)
    nctx, ntot, ntile = CTX_LEN // rb, r // rb, rb // 8

    def body(a_ref, hp_ref, dh_ref, da_ref, db_ref, c_ref):
        d, i = pl.program_id(0), pl.program_id(1)

        @pl.when(i == 0)
        def _():
            c_ref[...] = jnp.zeros_like(c_ref)

        def sweep(rev):
            order = list(range(8)) if rev else list(range(7, -1, -1))

            def step(s, carry):
                t0 = pl.multiple_of((s if rev else ntile - 1 - s) * 8, 8)
                at = a_ref[0, pl.ds(t0, 8), :]
                hpt = hp_ref[0, pl.ds(t0, 8), :]
                dht = dh_ref[pl.ds(t0, 8), :]
                das, dbs = [None] * 8, [None] * 8
                c = carry
                for j in order:
                    lam = dht[j:j + 1, :] + c
                    das[j] = lam * hpt[j:j + 1, :]
                    dbs[j] = lam
                    c = at[j:j + 1, :] * lam
                da_ref[0, pl.ds(t0, 8), :] = jnp.concatenate(das, axis=0)
                db_ref[0, pl.ds(t0, 8), :] = jnp.concatenate(dbs, axis=0)
                return c

            c_ref[...] = lax.fori_loop(0, ntile, step, c_ref[...])

        @pl.when(d == 0)
        def _():
            sweep(False)

        @pl.when(d == 1)
        def _():
            sweep(True)

    def rows(d, i):
        return _scan_block(d, ntot - 1 - i, nctx, ntot)

    blk = pl.BlockSpec((1, rb, lw), lambda d, i: (d, rows(d, i), 0))
    out = jax.ShapeDtypeStruct((2, r, lw), F32)
    return pl.pallas_call(
        body, name="lru_scan_bwd", grid=(2, ntot), out_shape=(out, out),
        in_specs=[blk, blk, pl.BlockSpec((rb, lw), lambda d, i: (rows(d, i), 0))], out_specs=(blk, blk),
        scratch_shapes=[pltpu.VMEM((1, lw), F32)], compiler_params=_cp(("arbitrary", "arbitrary")))(a, hprev, dh)


def _mix_post_fwd(o, hsum, proj, dnw_vec):
    dd = _dims()
    dnw, lw, nmain = dd["dnw"], dd["lw"], dd["nmain"]
    tm = _div(SEQ, ROW_TILE, 16)
    coff = CTX_LEN // tm
    gcol = (3 * dnw + lw) // dnw
    lcol = (3 * dnw + lw + dnw) // lw
    hd = DN_HEAD_DIM

    def body(o_ref, h_ref, g_ref, lg_ref, w_ref, y_ref):
        wv = w_ref[...]
        for hh in range(DN_HEADS):
            sl = slice(hh * hd, (hh + 1) * hd)
            os_ = o_ref[0, :, sl] + o_ref[1, :, sl]
            rr = lax.rsqrt(jnp.mean(os_ * os_, axis=1, keepdims=True) + EPS)
            y_ref[:, sl] = (os_ * rr * wv * _silu(g_ref[:, sl])).astype(BF16)
        y_ref[:, dnw:] = (h_ref[...] * _silu(lg_ref[...])).astype(BF16)

    return pl.pallas_call(
        body, name="mix_post_fwd", grid=(SEQ // tm,), out_shape=jax.ShapeDtypeStruct((SEQ, dnw + lw), BF16),
        in_specs=[pl.BlockSpec((2, tm, dnw), lambda i: (0, i + coff, 0)), pl.BlockSpec((tm, lw), lambda i: (i, 0)),
                  pl.BlockSpec((tm, dnw), lambda i: (i + coff, gcol)), pl.BlockSpec((tm, lw), lambda i: (i + coff, lcol)),
                  pl.BlockSpec((1, hd), lambda i: (0, 0))],
        out_specs=pl.BlockSpec((tm, dnw + lw), lambda i: (i, 0)), compiler_params=_cp(("arbitrary",)))(
            o, hsum, proj, proj, dnw_vec)


def _mix_post_bwd(o, hsum, proj, dnw_vec, dy):
    dd = _dims()
    dnw, lw, r = dd["dnw"], dd["lw"], dd["r"]
    tm = _div(SEQ, ROW_TILE, 16)
    coff = CTX_LEN // tm
    gcol = (3 * dnw + lw) // dnw
    lcol = (3 * dnw + lw + dnw) // lw
    hd = DN_HEAD_DIM

    def body(o_ref, h_ref, g_ref, lg_ref, w_ref, dy_ref, do_ref, dh_ref, dg_ref, dw_ref):
        i = pl.program_id(0)

        @pl.when(i == 0)
        def _():
            dw_ref[...] = jnp.zeros_like(dw_ref)

        @pl.when(i < coff)
        def _():
            do_ref[...] = jnp.zeros_like(do_ref)
            dg_ref[...] = jnp.zeros_like(dg_ref)

        @pl.when(i >= coff)
        def _():
            wv = w_ref[...]
            dwacc = jnp.zeros((1, hd), F32)
            for hh in range(DN_HEADS):
                sl = slice(hh * hd, (hh + 1) * hd)
                os_ = o_ref[0, :, sl] + o_ref[1, :, sl]
                rr = lax.rsqrt(jnp.mean(os_ * os_, axis=1, keepdims=True) + EPS)
                nn = os_ * rr
                gv = g_ref[:, sl]
                dyv = dy_ref[:, sl]
                don = dyv * _silu(gv)
                dg_ref[:, sl] = (dyv * nn * wv * _dsilu(gv)).astype(BF16)
                dwacc = dwacc + jnp.sum(don * nn, axis=0, keepdims=True)
                dn_ = don * wv
                do_ref[:, sl] = rr * (dn_ - nn * jnp.mean(nn * dn_, axis=1, keepdims=True))
            dw_ref[...] += dwacc
            lgv = lg_ref[...]
            dyl = dy_ref[:, dnw:]
            dh_ref[...] = dyl * _silu(lgv)
            dg_ref[:, dnw:] = (dyl * h_ref[...] * _dsilu(lgv)).astype(BF16)

    lat = lambda i: jnp.maximum(i - coff, 0)
    return pl.pallas_call(
        body, name="mix_post_bwd", grid=(r // tm,),
        out_shape=(jax.ShapeDtypeStruct((r, dnw), F32), jax.ShapeDtypeStruct((SEQ, lw), F32),
                   jax.ShapeDtypeStruct((r, dnw + lw), BF16), jax.ShapeDtypeStruct((1, hd), F32)),
        in_specs=[pl.BlockSpec((2, tm, dnw), lambda i: (0, i, 0)), pl.BlockSpec((tm, lw), lambda i: (lat(i), 0)),
                  pl.BlockSpec((tm, dnw), lambda i: (i, gcol)), pl.BlockSpec((tm, lw), lambda i: (i, lcol)),
                  pl.BlockSpec((1, hd), lambda i: (0, 0)), pl.BlockSpec((tm, dnw + lw), lambda i: (lat(i), 0))],
        out_specs=(pl.BlockSpec((tm, dnw), lambda i: (i, 0)), pl.BlockSpec((tm, lw), lambda i: (lat(i), 0)),
                   pl.BlockSpec((tm, dnw + lw), lambda i: (i, 0)), pl.BlockSpec((1, hd), lambda i: (0, 0))),
        compiler_params=_cp(("arbitrary",)))(o, hsum, proj, proj, dnw_vec, dy)


def _sc_masks(tm, cw):
    t = lax.broadcasted_iota(jnp.int32, (tm, cw), 0) % GRID_W
    return {-1: t != 0, 1: t != GRID_W - 1}


def _sc_mix_fwd(p, convw):
    s, w4 = p.shape
    w = w4 // 4
    tm = _div(s, 512, GRID_W if GRID_W % 16 == 0 else 16 * GRID_W)
    cw = _div(w, 512, LANE)
    nc = w // cw
    taps = _taps(SC_CONV)

    def body(b_ref, c_ref, x_ref, g_ref, w_ref, y_ref):
        masks = _sc_masks(tm, cw)
        u = c_ref[...] * x_ref[...]
        z = _conv_fwd_val(u, w_ref[...], taps, masks)
        y_ref[...] = (b_ref[...] * z * _silu(g_ref[...])).astype(BF16)

    blk = lambda k: pl.BlockSpec((tm, cw), lambda c, i: (i, k * nc + c))
    return pl.pallas_call(
        body, name="sc_mix_fwd", grid=(nc, s // tm), out_shape=jax.ShapeDtypeStruct((s, w), BF16),
        in_specs=[blk(0), blk(1), blk(2), blk(3), pl.BlockSpec((8, cw), lambda c, i: (0, c))],
        out_specs=pl.BlockSpec((tm, cw), lambda c, i: (i, c)),
        compiler_params=_cp(("arbitrary", "arbitrary")))(p, p, p, p, convw)


def _sc_mix_bwd(p, convw, dy):
    s, w4 = p.shape
    w = w4 // 4
    tm = _div(s, 512, GRID_W if GRID_W % 16 == 0 else 16 * GRID_W)
    cw = _div(w, 512, LANE)
    nc = w // cw
    taps = _taps(SC_CONV)

    def body(b_ref, c_ref, x_ref, g_ref, w_ref, dy_ref, db_ref, dc_ref, dx_ref, dg_ref, dw_ref):
        @pl.when(pl.program_id(1) == 0)
        def _():
            dw_ref[...] = jnp.zeros_like(dw_ref)

        masks = _sc_masks(tm, cw)
        wv = w_ref[...]
        bv, cv, xv, gv, dyv = b_ref[...], c_ref[...], x_ref[...], g_ref[...], dy_ref[...]
        u = cv * xv
        z = _conv_fwd_val(u, wv, taps, masks)
        sg = _silu(gv)
        db_ref[...] = (dyv * z * sg).astype(BF16)
        dg_ref[...] = (dyv * bv * z * _dsilu(gv)).astype(BF16)
        dz = dyv * bv * sg
        du, dws = _conv_bwd_val(u, dz, wv, taps, masks)
        dc_ref[...] = (du * xv).astype(BF16)
        dx_ref[...] = (du * cv).astype(BF16)
        dw_ref[...] += jnp.concatenate(dws + [jnp.zeros((8 - len(dws), cw), F32)], axis=0)

    blk = lambda k: pl.BlockSpec((tm, cw), lambda c, i: (i, k * nc + c))
    oblk = pl.BlockSpec((tm, cw), lambda c, i: (i, c))
    wblk = pl.BlockSpec((8, cw), lambda c, i: (0, c))
    out = jax.ShapeDtypeStruct((s, w), BF16)
    return pl.pallas_call(
        body, name="sc_mix_bwd", grid=(nc, s // tm),
        out_shape=(out, out, out, out, jax.ShapeDtypeStruct((8, w), F32)),
        in_specs=[blk(0), blk(1), blk(2), blk(3), wblk, oblk],
        out_specs=(oblk, oblk, oblk, oblk, wblk),
        compiler_params=_cp(("arbitrary", "arbitrary")))(p, p, p, p, convw, dy)


def _adamw_slots(w, slots, m, v, name):
    rows, cols = w.shape
    nslot = slots.shape[0]
    tm = _div(rows, max(16, (64 * 4096) // cols), 16)

    def body(w_ref, s_ref, m_ref, v_ref, g_ref, dl_ref, nm_ref, nv_ref):
        g = s_ref[0].astype(F32)
        for j in range(1, nslot):
            g = g + s_ref[j].astype(F32)
        dl, m2, v2 = _adam(w_ref[...], g, m_ref[...], v_ref[...])
        g_ref[...] = g
        dl_ref[...] = dl
        nm_ref[...] = m2
        nv_ref[...] = v2

    blk = pl.BlockSpec((tm, cols), lambda i: (i, 0))
    out = jax.ShapeDtypeStruct((rows, cols), F32)
    return pl.pallas_call(
        body, name=name, grid=(rows // tm,), out_shape=(out, out, out, out),
        in_specs=[blk, pl.BlockSpec((nslot, tm, cols), lambda i: (0, i, 0)), blk, blk],
        out_specs=(blk, blk, blk, blk), compiler_params=_cp(("arbitrary",)))(w, slots, m, v)


def _adamw_flat(w, g, m, v, name):
    rows, cols = w.shape
    tm = _div(rows, 1024, 8)

    def body(w_ref, g_ref, m_ref, v_ref, dl_ref, nm_ref, nv_ref):
        dl, m2, v2 = _adam(w_ref[...], g_ref[...], m_ref[...], v_ref[...])
        dl_ref[...] = dl
        nm_ref[...] = m2
        nv_ref[...] = v2

    blk = pl.BlockSpec((tm, cols), lambda i: (i, 0))
    out = jax.ShapeDtypeStruct((rows, cols), F32)
    return pl.pallas_call(
        body, name=name, grid=(rows // tm,), out_shape=(out, out, out), in_specs=[blk] * 4, out_specs=(blk, blk, blk),
        compiler_params=_cp(("arbitrary",)))(w, g, m, v)


def _pack(arrs):
    flat = [a.reshape(-1).astype(F32) for a in arrs]
    sizes = [f.shape[0] for f in flat]
    total = sum(sizes)
    padded = -(-total // 1024) * 1024
    if padded > total:
        flat.append(jnp.zeros((padded - total,), F32))
    offs, o = [], 0
    for s_ in sizes:
        offs.append(o)
        o += s_
    return jnp.concatenate(flat).reshape(padded // LANE, LANE), offs


def _unpack(flat2d, offs, shapes):
    flat = flat2d.reshape(-1)
    out = []
    for o, shp in zip(offs, shapes):
        n = 1
        for s_ in shp:
            n *= s_
        out.append(flat[o:o + n].reshape(shp))
    return out


def _pad_rows(a, rows):
    return jnp.concatenate([a, jnp.zeros((rows - a.shape[0],) + a.shape[1:], a.dtype)], axis=0)


def _to_col_major(t):
    n, ch = t.shape
    return t.reshape(n // GRID_W, GRID_W, ch).swapaxes(0, 1).reshape(n, ch)


def _to_raster(t):
    n, ch = t.shape
    return t.reshape(GRID_W, n // GRID_W, ch).swapaxes(0, 1).reshape(n, ch)


def _scg_layout(sc):
    r = sc.shape[0]
    hps = HEADS_PER_STEP
    nhg = DN_HEADS // hps
    t = sc.reshape(r, 4, 32)[:, :, :2 * DN_HEADS].reshape(r, 4, 2, nhg, hps)
    t = jnp.stack([t[:, 0], t[:, 2], t[:, 3]], axis=1)
    t = t.transpose(2, 3, 0, 1, 4).reshape(2 * nhg, r, 3 * hps)
    return jnp.concatenate([t, jnp.zeros((2 * nhg, r, LANE - 3 * hps), F32)], axis=2)


def _scg_layout_bwd(dscg):
    r = dscg.shape[1]
    hps = HEADS_PER_STEP
    nhg = DN_HEADS // hps
    t = dscg[:, :, :3 * hps].reshape(2, nhg, r, 3, hps).transpose(2, 3, 0, 1, 4).reshape(r, 3, 2 * DN_HEADS)
    t = jnp.concatenate([t, jnp.zeros((r, 3, 32 - 2 * DN_HEADS), F32)], axis=2) if 2 * DN_HEADS < 32 else t
    z = jnp.zeros((r, 1, 32), F32)
    return jnp.concatenate([t[:, 0:1], z, t[:, 1:2], t[:, 2:3]], axis=1).reshape(r, LANE)


def kernel(x, c, ctx, c_ctx, mod_w, mod_b, norm_w, ab_w_in, ab_qkv_conv, ab_a_log, ab_dt_bias, ab_dn_norm, ab_lru_conv_w, ab_lru_conv_b, ab_lru_w_r, ab_lru_b_r, ab_lru_w_i, ab_lru_b_i, ab_lru_lambda, ab_w_out, sc_w_in, sc_conv, sc_w_out, final_norm_w, loss_target, m_c_ctx, m_mod_w, m_mod_b, m_norm_w, m_ab_w_in, m_ab_qkv_conv, m_ab_a_log, m_ab_dt_bias, m_ab_dn_norm, m_ab_lru_conv_w, m_ab_lru_conv_b, m_ab_lru_w_r, m_ab_lru_b_r, m_ab_lru_w_i, m_ab_lru_b_i, m_ab_lru_lambda, m_ab_w_out, m_sc_w_in, m_sc_conv, m_sc_w_out, m_final_norm_w, v_c_ctx, v_mod_w, v_mod_b, v_norm_w, v_ab_w_in, v_ab_qkv_conv, v_ab_a_log, v_ab_dt_bias, v_ab_dn_norm, v_ab_lru_conv_w, v_ab_lru_conv_b, v_ab_lru_w_r, v_ab_lru_b_r, v_ab_lru_w_i, v_ab_lru_b_i, v_ab_lru_lambda, v_ab_w_out, v_sc_w_in, v_sc_conv, v_sc_w_out, v_final_norm_w):
    dd = _dims()
    d, dnw, lw, r = dd["d"], dd["dnw"], dd["lw"], dd["r"]
    off_lru, off_beta, ab_state, ab_in, nmain = dd["off_lru"], dd["off_beta"], dd["ab_state"], dd["ab_in"], dd["nmain"]
    hh2 = 2 * DN_HEADS
    me = _me()
    x = x[0]
    ctx = ctx[0]
    tgt = loss_target[0]

    small_sharded = [ab_qkv_conv[0], ab_lru_conv_w[0], ab_lru_b_r[0], ab_lru_b_i[0], ab_lru_lambda[0], sc_conv[0]]
    pk, offs0 = _pack([c] + small_sharded)
    (g0,) = _exchange([pk], "gather_small_in", scatter=False)
    parts = [_unpack(g0[j], offs0, [c.shape] + [a.shape for a in small_sharded]) for j in range(NDEV)]
    c_all = jnp.concatenate([p[0] for p in parts], axis=0)
    qkv_conv, lru_conv_w, lru_b_r, lru_b_i, lru_lam, sc_conv_f = [
        jnp.concatenate([p[i] for p in parts], axis=-1) for i in range(1, 7)]

    mcols = mod_w.shape[2]
    c16 = jnp.concatenate([c_all, c_ctx[None, :], jnp.zeros((7, d), F32)], axis=0)
    bias = lax.dynamic_slice_in_dim(mod_b, me * mcols, mcols, axis=1)[:, None, :]
    mod_loc = _mod_fwd(c16, mod_w, bias)
    (mod_g,) = _exchange([mod_loc], "gather_mod", scatter=False)
    mod_all = mod_g.transpose(1, 2, 0, 3).reshape(2, 16, NDEV * mcols)
    mine = lax.dynamic_slice_in_dim(mod_all, me, 1, axis=1)[:, 0]
    shift = [mine[l, None, 0:d] for l in range(2)]
    scale = [mine[l, None, d:2 * d] for l in range(2)]
    gate = [mine[l, None, 2 * d:3 * d] for l in range(2)]
    shift_c, scale_c = mod_all[0, 8, None, 0:d], mod_all[0, 8, None, d:2 * d]
    nw0, nw1 = norm_w[0][None, :], norm_w[1][None, :]

    (g_abin,) = _gather_two_level([ab_w_in[0].astype(BF16)], "gather_weights")
    shard = ab_in // NDEV

    def shard_cols(lo, hi):
        return [g_abin[j][:, max(lo, j * shard) - j * shard:min(hi, (j + 1) * shard) - j * shard]
                for j in range(NDEV) if max(lo, j * shard) < min(hi, (j + 1) * shard)]

    w_main = jnp.concatenate(shard_cols(0, off_beta) + shard_cols(ab_state, ab_in), axis=1)
    w_ba = jnp.concatenate(shard_cols(off_beta, ab_state) + [jnp.zeros((d, LANE - 2 * hh2), BF16)], axis=1)

    hn_c, rinv_c = _norm_fwd(ctx, nw0, shift_c, scale_c, "norm0_ctx")
    hn_x, rinv_x = _norm_fwd(x, nw0, shift[0], scale[0], "norm0_x")
    hn0 = jnp.concatenate([hn_c, hn_x], axis=0)
    proj, (g_about,) = _mm_nn(hn0, w_main[None], F32, "proj_main", rider=("gather", [ab_w_out[0].astype(BF16)]))
    w_about = g_about.reshape(dnw + lw, d)
    pba = _mm_nn(hn0, w_ba[None], F32, "proj_ba")

    convw8 = _pad_rows(qkv_conv, 8)
    qkv = _dn_pre_fwd(proj, convw8)
    zpad = jnp.zeros((32 - hh2,), F32)
    par = jnp.stack([jnp.concatenate([jnp.zeros((32,), F32), ab_dt_bias[0].reshape(-1), zpad, jnp.zeros((64,), F32)]),
                     jnp.concatenate([jnp.zeros((32,), F32), -jnp.exp(ab_a_log[0].reshape(-1)), zpad,
                                      jnp.zeros((64,), F32)])] + [jnp.zeros((LANE,), F32)] * 6)
    if hh2 < 32:
        pba_l = jnp.concatenate([pba[:, :hh2], jnp.zeros((r, 32 - hh2), F32), pba[:, hh2:2 * hh2],
                                 jnp.zeros((r, LANE - 32 - hh2), F32)], axis=1)
    else:
        pba_l = pba
    sc = _dn_gates_fwd(pba_l, par)
    scg = _scg_layout(sc)
    (o_dn, states), (g_scin,) = _delta_fwd(qkv, scg, rider=("gather", [sc_w_in[0].astype(BF16)]))

    xl = proj[:, off_lru:off_beta]
    xl_cat = jnp.concatenate([xl[:CTX_LEN], _to_col_major(xl[CTX_LEN:])], axis=0)
    lconv8 = jnp.concatenate([lru_conv_w, ab_lru_conv_b[0][None, :], jnp.zeros((3, lw), F32)], axis=0)
    xc = _lru_conv_fwd(xl_cat, lconv8)
    lvecs = jnp.stack([lru_b_r, lru_b_i, lru_lam] + [jnp.zeros_like(lru_lam)] * 5, axis=1)
    w_r, w_i = ab_lru_w_r[0], ab_lru_w_i[0]
    la_a, la_b = _lru_gates_fwd(xc, w_r, w_i, lvecs)
    h_dir, h_prev = _lru_scan_fwd(la_a, la_b)
    hsum = _to_raster(h_dir[0, CTX_LEN:] + h_dir[1, CTX_LEN:])

    dnn = ab_dn_norm[0][None, :]
    y0 = _mix_post_fwd(o_dn, hsum, proj, dnn)
    out0 = _mm_nn(y0, w_about[None], F32, "out0")

    x1, hn1, rinv1 = _resid_norm_fwd(x, out0, gate[0], nw1, shift[1], scale[1])
    p1, (g_scout,) = _mm_nn(hn1, g_scin, F32, "proj_sc", rider=("gather", [sc_w_out[0].astype(BF16)]))
    w_scout = g_scout.reshape(d, d)
    scw8 = _pad_rows(sc_conv_f, 8)
    y1 = _sc_mix_fwd(p1, scw8)
    out1 = _mm_nn(y1, w_scout[None], F32, "out1")

    fnw = final_norm_w[None, :]
    dx2, dout1, loss11, dfnw, dgate1 = _loss_head(x1, out1, gate[1], fnw, tgt)
    dy1 = _mm_nt(dout1, w_scout[None], F32, "dy1")
    gw_scout = _mm_tn(y1, dout1, BF16, "gw_scout")[0].reshape(NDEV, d // NDEV, d)
    dpb, dpc, dpx, dpg, dscw8 = _sc_mix_bwd(p1, scw8, dy1)
    dp1 = jnp.concatenate([dpb, dpc, dpx, dpg], axis=1)
    dhn1 = _mm_nt(dp1, g_scin, F32, "dhn1", tk=2048)
    gw_scin = _mm_tn(hn1, dp1, BF16, "gw_scin", oj=NDEV)
    dx1, dout0, dshift1, dscale1, dnw1, dgate0 = _norm_bwd(
        x1, rinv1, [dhn1], nw1, scale[1], "norm1_bwd", dx_in=dx2, resid=(out0, gate[0]))

    dy0, sib_sc = _mm_nt(dout0, w_about[None], F32, "dy0", rider=("swap", [gw_scin, gw_scout]))
    gw_about = _mm_tn(y0, dout0, BF16, "gw_about")[0].reshape(NDEV, (dnw + lw) // NDEV, d)
    do, dh_r, dgates, ddnn = _mix_post_bwd(o_dn, hsum, proj, dnn, dy0)

    early = [gw_about, gw_scin, gw_scout]
    sib_e = _sibling_swap([gw_about], "swap_grads_mid") + sib_sc
    parts_e = [_chip_partial(o_, s_, f"chip_partial_early{i}") for i, (o_, s_) in enumerate(zip(early, sib_e))]
    (dq, dk, dv, dscg), (s_about, s_scin, s_scout) = _delta_bwd(qkv, scg, states, do, rider=("chipx", parts_e))
    dsc = _scg_layout_bwd(dscg)
    dpba, dpar = _dn_gates_bwd(pba_l, par, dsc)
    dpq, dcw_q = _dn_pre_bwd(proj, convw8, dq, 0)
    dpk, dcw_k = _dn_pre_bwd(proj, convw8, dk, 1)
    dpv, dcw_v = _dn_pre_bwd(proj, convw8, dv, 2)

    dh_cat = jnp.concatenate([jnp.zeros((CTX_LEN, lw), F32), _to_col_major(dh_r)], axis=0)
    d_a, d_b = _lru_scan_bwd(la_a, h_prev, dh_cat)
    dxc, dw_r, dw_i, dlvecs = _lru_gates_bwd(xc, w_r, w_i, lvecs, d_a, d_b)
    dxl_cat, dlconv8 = _lru_conv_bwd(xl_cat, lconv8, dxc)
    dxl = jnp.concatenate([dxl_cat[:CTX_LEN], _to_raster(dxl_cat[CTX_LEN:])], axis=0).astype(BF16)

    dproj = jnp.concatenate([dpq, dpk, dpv, dxl, dgates], axis=1)
    if hh2 < 32:
        dpba_w = jnp.concatenate([dpba[:, :hh2], dpba[:, 32:32 + hh2], jnp.zeros((r, LANE - 2 * hh2), BF16)], axis=1)
    else:
        dpba_w = dpba
    gw_main = _mm_tn(hn0, dproj, BF16, "gw_main")[0]
    gw_ba = _mm_tn(hn0, dpba_w, BF16, "gw_ba")[0]
    def grad_cols(lo, hi):
        pieces = []
        for a_, b_, src, off in ((0, off_beta, gw_main, 0), (off_beta, ab_state, gw_ba, -off_beta),
                                 (ab_state, ab_in, gw_main, off_beta - ab_state)):
            if max(lo, a_) < min(hi, b_):
                pieces.append(src[:, max(lo, a_) + off:min(hi, b_) + off])
        return pieces

    gw_abin = jnp.stack([jnp.concatenate(grad_cols(j * shard, (j + 1) * shard), axis=1) for j in range(NDEV)])
    (sib_l,) = _sibling_swap([gw_abin], "swap_grads_late")
    part_l = _chip_partial(gw_abin, sib_l, "chip_partial_late")
    dhn0_a, (s_abin,) = _mm_nt(dproj, w_main[None], F32, "dhn0_main", rider=("chipx", [part_l]))
    dhn0_b = _mm_nt(dpba_w, w_ba[None], F32, "dhn0_ba")

    nt = _div(SEQ, ROW_TILE // 2, 16)
    grad_x, dshift0, dscale0, dnw0_x = _norm_bwd(x, rinv_x, [dhn0_a, dhn0_b], nw0, scale[0], "norm0_bwd_x",
                                                 row_off=CTX_LEN // nt, dx_in=dx1)
    _, dshift_c, dscale_c, dnw0_c = _norm_bwd(ctx, rinv_c, [dhn0_a, dhn0_b], nw0, scale_c, "norm0_bwd_ctx")

    big = {}
    big["ab_w_in"] = _adamw_slots(ab_w_in[0], s_abin, m_ab_w_in[0], v_ab_w_in[0], "adamw_ab_w_in")
    big["ab_w_out"] = _adamw_slots(ab_w_out[0], s_about, m_ab_w_out[0], v_ab_w_out[0], "adamw_ab_w_out")
    big["sc_w_in"] = _adamw_slots(sc_w_in[0], s_scin, m_sc_w_in[0], v_sc_w_in[0], "adamw_sc_w_in")
    big["sc_w_out"] = _adamw_slots(sc_w_out[0], s_scout, m_sc_w_out[0], v_sc_w_out[0], "adamw_sc_w_out")

    dmod_own = jnp.stack([jnp.concatenate([dshift0, dscale0, dgate0], axis=1)[0],
                          jnp.concatenate([dshift1, dscale1, dgate1], axis=1)[0]])
    dmod_c = jnp.concatenate([dshift_c, dscale_c], axis=1)[0]
    g_qkv_conv = jnp.concatenate([dcw_q[:DN_CONV], dcw_k[:DN_CONV], dcw_v[:DN_CONV]], axis=1)
    small = [
        ("loss", loss11.reshape(1)),
        ("norm_w", jnp.concatenate([dnw0_x + dnw0_c, dnw1], axis=0)),
        ("dmod_own", dmod_own),
        ("dmod_c", dmod_c),
        ("qkv_conv", g_qkv_conv),
        ("a_log", dpar[1, 32:32 + hh2].reshape(2, DN_HEADS)),
        ("dt_bias", dpar[0, 32:32 + hh2].reshape(2, DN_HEADS)),
        ("dn_norm", ddnn[0]),
        ("lru_conv_w", dlconv8[:LRU_CONV]),
        ("lru_conv_b", dlconv8[LRU_CONV]),
        ("lru_w_r", dw_r),
        ("lru_w_i", dw_i),
        ("lru_b_r", dlvecs[:, 0]),
        ("lru_b_i", dlvecs[:, 1]),
        ("lru_lambda", dlvecs[:, 2]),
        ("sc_conv", dscw8[:SC_CONV]),
        ("final_norm_w", dfnw[0]),
    ]
    names = [n for n, _ in small]
    shapes = [a.shape for _, a in small]
    spk, soffs = _pack([a for _, a in small])
    (sg,) = _gather_two_level([spk], "gather_small_grads")
    tot = dict(zip(names, _unpack(_sum_slots(sg, "sum_small_grads"), soffs, shapes)))
    i_own = names.index("dmod_own")
    dmod_all = jnp.stack([_unpack(sg[j], soffs[i_own:i_own + 1], shapes[i_own:i_own + 1])[0] for j in range(NDEV)])

    dmc_full = jnp.concatenate([tot["dmod_c"], jnp.zeros((d,), F32)])
    rows9 = jnp.concatenate([dmod_all, jnp.stack([dmc_full, jnp.zeros((3 * d,), F32)])[None],
                             jnp.zeros((7, 2, 3 * d), F32)], axis=0)
    dm = lax.dynamic_slice_in_dim(rows9, me * mcols, mcols, axis=2).transpose(1, 0, 2)
    g_modw, dl_modw, nm_modw, nv_modw, gcc = _mod_bwd(c16, dm, mod_w, m_mod_w, v_mod_w)
    gcc_pk, _ = _pack([gcc[0, :, 0]])
    (gcc_g,) = _exchange([gcc_pk], "gather_c_ctx", scatter=False)
    gcc_sum = _sum_slots(gcc_g, "sum_c_ctx").reshape(-1)[:d]
    g_c_ctx_pre = gcc_sum

    grad_mod_b = tot["dmod_own"] + jnp.stack([dmc_full, jnp.zeros((3 * d,), F32)])

    def shard(a):
        wd = a.shape[-1] // NDEV
        return lax.dynamic_slice_in_dim(a, me * wd, wd, axis=a.ndim - 1)

    small_w = [
        ("c_ctx", c_ctx, None, m_c_ctx, v_c_ctx),
        ("mod_b", mod_b, grad_mod_b, m_mod_b, v_mod_b),
        ("norm_w", norm_w, tot["norm_w"], m_norm_w, v_norm_w),
        ("ab_qkv_conv", ab_qkv_conv, shard(tot["qkv_conv"])[None], m_ab_qkv_conv, v_ab_qkv_conv),
        ("ab_a_log", ab_a_log, tot["a_log"][None], m_ab_a_log, v_ab_a_log),
        ("ab_dt_bias", ab_dt_bias, tot["dt_bias"][None], m_ab_dt_bias, v_ab_dt_bias),
        ("ab_dn_norm", ab_dn_norm, tot["dn_norm"][None], m_ab_dn_norm, v_ab_dn_norm),
        ("ab_lru_conv_w", ab_lru_conv_w, shard(tot["lru_conv_w"])[None], m_ab_lru_conv_w, v_ab_lru_conv_w),
        ("ab_lru_conv_b", ab_lru_conv_b, tot["lru_conv_b"][None], m_ab_lru_conv_b, v_ab_lru_conv_b),
        ("ab_lru_w_r", ab_lru_w_r, tot["lru_w_r"][None], m_ab_lru_w_r, v_ab_lru_w_r),
        ("ab_lru_b_r", ab_lru_b_r, shard(tot["lru_b_r"])[None], m_ab_lru_b_r, v_ab_lru_b_r),
        ("ab_lru_w_i", ab_lru_w_i, tot["lru_w_i"][None], m_ab_lru_w_i, v_ab_lru_w_i),
        ("ab_lru_b_i", ab_lru_b_i, shard(tot["lru_b_i"])[None], m_ab_lru_b_i, v_ab_lru_b_i),
        ("ab_lru_lambda", ab_lru_lambda, shard(tot["lru_lambda"])[None], m_ab_lru_lambda, v_ab_lru_lambda),
        ("sc_conv", sc_conv, shard(tot["sc_conv"])[None], m_sc_conv, v_sc_conv),
        ("final_norm_w", final_norm_w, tot["final_norm_w"], m_final_norm_w, v_final_norm_w),
    ]
    sg_cc = jax.nn.sigmoid(c_ctx)
    g_c_ctx = g_c_ctx_pre * (sg_cc * (1.0 + c_ctx * (1.0 - sg_cc)))
    small_w[0] = ("c_ctx", c_ctx, g_c_ctx, m_c_ctx, v_c_ctx)
    wshapes = [w.shape for _, w, _, _, _ in small_w]
    wpk, woffs = _pack([w for _, w, _, _, _ in small_w])
    gpk, _ = _pack([g for _, _, g, _, _ in small_w])
    mpk, _ = _pack([m for _, _, _, m, _ in small_w])
    vpk, _ = _pack([v for _, _, _, _, v in small_w])
    dl_pk, nm_pk, nv_pk = _adamw_flat(wpk, gpk, mpk, vpk, "adamw_small")
    sm_g = {n: g for n, _, g, _, _ in small_w}
    sm_dl = dict(zip([n for n, *_ in small_w], _unpack(dl_pk, woffs, wshapes)))
    sm_nm = dict(zip([n for n, *_ in small_w], _unpack(nm_pk, woffs, wshapes)))
    sm_nv = dict(zip([n for n, *_ in small_w], _unpack(nv_pk, woffs, wshapes)))

    grads, deltas, new_m, new_v = {}, {}, {}, {}
    for n in sm_g:
        grads[n], deltas[n], new_m[n], new_v[n] = sm_g[n], sm_dl[n], sm_nm[n], sm_nv[n]
    grads["mod_w"], deltas["mod_w"], new_m["mod_w"], new_v["mod_w"] = g_modw, dl_modw, nm_modw, nv_modw
    for n in ("ab_w_in", "ab_w_out", "sc_w_in", "sc_w_out"):
        g, dl, m2, v2 = big[n]
        grads[n], deltas[n], new_m[n], new_v[n] = g[None], dl[None], m2[None], v2[None]

    order = ["c_ctx", "mod_w", "mod_b", "norm_w", "ab_w_in", "ab_qkv_conv", "ab_a_log", "ab_dt_bias", "ab_dn_norm",
             "ab_lru_conv_w", "ab_lru_conv_b", "ab_lru_w_r", "ab_lru_b_r", "ab_lru_w_i", "ab_lru_b_i", "ab_lru_lambda",
             "ab_w_out", "sc_w_in", "sc_conv", "sc_w_out", "final_norm_w"]
    loss = tot["loss"][0]
    return (loss, grad_x[None], *[grads[n] for n in order], *[deltas[n] for n in order],
            *[new_m[n] for n in order], *[new_v[n] for n in order])
```

```python
import functools

import jax
import jax.numpy as jnp
from jax import lax
from jax.experimental import pallas as pl
from jax.experimental.pallas import tpu as pltpu

F32 = jnp.float32
BF16 = jnp.bfloat16

NDEV = 8
D_MODEL = 4096
SEQ = 4096
CTX_LEN = 256
GRID_W = 64
EPS = 1e-6
DN_HEADS = 16
DN_HEAD_DIM = 128
DN_CONV = 4
DN_CHUNK = 64
LRU_WIDTH = 2048
LRU_BLOCKS = 16
LRU_CONV = 4
LRU_C = 8.0
SC_CONV = 3
ADAM_LR = 0.001
ADAM_B1 = 0.9
ADAM_B2 = 0.999
ADAM_EPS = 1e-08
ADAM_WD = 0.01
ADAM_STEP = 10

ROW_TILE = 256
HEADS_PER_STEP = 16
LANE = 128
VMEM_LIMIT = 56 * 1024 * 1024


def _dims():
    d = D_MODEL
    dnw = DN_HEADS * DN_HEAD_DIM
    lw = LRU_WIDTH
    off_lru = 3 * dnw
    off_beta = off_lru + lw
    ab_state = off_beta + 4 * DN_HEADS
    ab_in = ab_state + dnw + lw
    return dict(d=d, dnw=dnw, lw=lw, off_lru=off_lru, off_beta=off_beta, ab_state=ab_state, ab_in=ab_in,
                nmain=3 * dnw + lw + dnw + lw, r=CTX_LEN + SEQ, lbd=lw // LRU_BLOCKS)


def _div(n, pref, mult):
    best = None
    for t in range(mult, min(n, pref) + 1, mult):
        if n % t == 0:
            best = t
    return best if best is not None else n


def _cp(sem, vmem=VMEM_LIMIT):
    return pltpu.CompilerParams(dimension_semantics=sem, vmem_limit_bytes=vmem)


def _silu(z):
    return z * jax.nn.sigmoid(z)


def _dsilu(z):
    s = jax.nn.sigmoid(z)
    return s * (1.0 + z * (1.0 - s))


def _softplus(z):
    return jnp.maximum(z, 0.0) + jnp.log1p(jnp.exp(-jnp.abs(z)))


def _neg_expm1(z):
    series = -z * (1.0 + z * (0.5 + z * (1.0 / 6.0 + z * (1.0 / 24.0))))
    return jnp.where(z > -0.05, series, 1.0 - jnp.exp(z))


def _me():
    return 4 * lax.axis_index("x") + 2 * lax.axis_index("y") + lax.axis_index("c")


def _exchange(srcs, name, scatter):
    n = len(srcs)

    def body(*refs):
        src_refs, out_refs = refs[:n], refs[n:2 * n]
        send_sems, recv_sems, local_sems = refs[2 * n:]
        x, y, c = lax.axis_index("x"), lax.axis_index("y"), lax.axis_index("c")
        me = 4 * x + 2 * y + c
        copies = []
        for i in range(n):
            mine = src_refs[i].at[me] if scatter else src_refs[i]
            lc = pltpu.make_async_copy(mine, out_refs[i].at[me], local_sems.at[i])
            lc.start()
            copies.append(lc)
            for k in range(1, NDEV):
                px = 1 - x if (k >> 2) & 1 else x
                py = 1 - y if (k >> 1) & 1 else y
                pc = 1 - c if k & 1 else c
                peer = 4 * px + 2 * py + pc
                src = src_refs[i].at[peer] if scatter else src_refs[i]
                cp = pltpu.make_async_remote_copy(
                    src_ref=src, dst_ref=out_refs[i].at[me], send_sem=send_sems.at[i, k - 1],
                    recv_sem=recv_sems.at[i, k - 1], device_id=(px, py, pc), device_id_type=pl.DeviceIdType.MESH)
                cp.start()
                copies.append(cp)
        for cp in copies:
            cp.wait()

    any_spec = pl.BlockSpec(memory_space=pl.ANY)
    out_shape = tuple(jax.ShapeDtypeStruct(s.shape if scatter else (NDEV,) + s.shape, s.dtype) for s in srcs)
    outs = pl.pallas_call(
        body, name=name, out_shape=out_shape, in_specs=[any_spec] * n, out_specs=tuple([any_spec] * n),
        scratch_shapes=[pltpu.SemaphoreType.DMA((n, NDEV - 1)), pltpu.SemaphoreType.DMA((n, NDEV - 1)),
                        pltpu.SemaphoreType.DMA((n,))],
    )(*srcs)
    return list(outs)


def _flip(v, bit):
    return 1 - v if bit else v


def _gather_ops(src_refs, out_refs, send_sems, recv_sems, local_sems):
    n = len(src_refs)

    def setup():
        x, y, c = lax.axis_index("x"), lax.axis_index("y"), lax.axis_index("c")
        chips = [(_flip(x, k >> 1), _flip(y, k & 1)) for k in (1, 2, 3)]
        return x, y, c, chips

    def copy(i, sem, slot, to, src=None):
        return pltpu.make_async_remote_copy(
            src_ref=out_refs[i].at[slot] if src is None else src, dst_ref=out_refs[i].at[slot],
            send_sem=send_sems.at[i, sem], recv_sem=recv_sems.at[i, sem], device_id=to,
            device_id_type=pl.DeviceIdType.MESH)

    def own(i, me):
        return pltpu.make_async_copy(src_refs[i], out_refs[i].at[me], local_sems.at[i])

    def first(i, x, y, c, chips):
        me = 4 * x + 2 * y + c
        return [copy(i, 0, me, (x, y, 1 - c), src=src_refs[i])] + [
            copy(i, 1 + j, me, (px, py, c), src=src_refs[i]) for j, (px, py) in enumerate(chips)]

    def start():
        x, y, c, chips = setup()
        for i in range(n):
            own(i, 4 * x + 2 * y + c).start()
            for cp in first(i, x, y, c, chips):
                cp.start()

    def finish():
        x, y, c, chips = setup()
        here = (x, y, c)
        passed = []
        for i in range(n):
            for j, (px, py) in enumerate(chips):
                theirs = 4 * px + 2 * py + c
                copy(i, 1 + j, theirs, here).wait_recv()
                fwd = copy(i, 4 + j, theirs, (x, y, 1 - c))
                fwd.start()
                passed.append(fwd)
        for i in range(n):
            copy(i, 0, 4 * x + 2 * y + (1 - c), here).wait_recv()
            for j, (px, py) in enumerate(chips):
                copy(i, 4 + j, 4 * px + 2 * py + (1 - c), here).wait_recv()
        for i in range(n):
            for cp in first(i, x, y, c, chips):
                cp.wait_send()
        for cp in passed:
            cp.wait_send()
        for i in range(n):
            own(i, 4 * x + 2 * y + c).wait()

    return start, finish


def _chipx_ops(src_refs, out_refs, send_sems, recv_sems, local_sems):
    n = len(src_refs)

    def copies():
        x, y, c = lax.axis_index("x"), lax.axis_index("y"), lax.axis_index("c")
        mine = 2 * x + y
        res = []
        for i in range(n):
            res.append(pltpu.make_async_copy(src_refs[i].at[mine], out_refs[i].at[mine], local_sems.at[i]))
            for k in (1, 2, 3):
                px, py = _flip(x, k >> 1), _flip(y, k & 1)
                res.append(pltpu.make_async_remote_copy(
                    src_ref=src_refs[i].at[2 * px + py], dst_ref=out_refs[i].at[mine], send_sem=send_sems.at[i, k - 1],
                    recv_sem=recv_sems.at[i, k - 1], device_id=(px, py, c), device_id_type=pl.DeviceIdType.MESH))
        return res

    def start():
        for cp in copies():
            cp.start()

    def finish():
        for cp in copies():
            cp.wait()

    return start, finish


def _swap_ops(src_refs, out_refs, send_sems, recv_sems):
    n = len(src_refs)

    def copies():
        x, y, c = lax.axis_index("x"), lax.axis_index("y"), lax.axis_index("c")
        return [pltpu.make_async_remote_copy(
            src_ref=src_refs[i].at[2 * q + (1 - c)], dst_ref=out_refs[i].at[q], send_sem=send_sems.at[i, q],
            recv_sem=recv_sems.at[i, q], device_id=(x, y, 1 - c), device_id_type=pl.DeviceIdType.MESH)
            for i in range(n) for q in range(NDEV // 2)]

    def start():
        for cp in copies():
            cp.start()

    def finish():
        for cp in copies():
            cp.wait()

    return start, finish


def _rider(kind, arrays):
    n = len(arrays)
    any_spec = pl.BlockSpec(memory_space=pl.ANY)
    dma = pltpu.SemaphoreType.DMA
    if kind == "gather":
        out_shapes = [jax.ShapeDtypeStruct((NDEV,) + a.shape, a.dtype) for a in arrays]
        scratch, make = [dma((n, NDEV - 1)), dma((n, NDEV - 1)), dma((n,))], _gather_ops
    elif kind == "swap":
        out_shapes = [jax.ShapeDtypeStruct((NDEV // 2,) + a.shape[1:], a.dtype) for a in arrays]
        scratch, make = [dma((n, NDEV // 2)), dma((n, NDEV // 2))], _swap_ops
    else:
        out_shapes = [jax.ShapeDtypeStruct(a.shape, a.dtype) for a in arrays]
        scratch, make = [dma((n, NDEV // 2 - 1)), dma((n, NDEV // 2 - 1)), dma((n,))], _chipx_ops
    return list(arrays), [any_spec] * n, out_shapes, [any_spec] * n, scratch, make


def _exchange_call(kind, arrays, name):
    n = len(arrays)
    ins, in_specs, out_shapes, out_specs, scratch, make = _rider(kind, arrays)

    def body(*refs):
        start, finish = make(refs[:n], refs[n:2 * n], *refs[2 * n:])
        start()
        finish()

    outs = pl.pallas_call(body, name=name, out_shape=tuple(out_shapes), in_specs=in_specs, out_specs=tuple(out_specs),
                          scratch_shapes=scratch)(*ins)
    return list(outs)


def _gather_two_level(srcs, name):
    return _exchange_call("gather", srcs, name)


def _call_with_rider(body, rider, *, name, grid, out_shape, in_specs, out_specs, scratch_shapes, compiler_params,
                     operands):
    if rider is None:
        outs = pl.pallas_call(body, name=name, grid=grid, out_shape=tuple(out_shape), in_specs=list(in_specs),
                              out_specs=tuple(out_specs), scratch_shapes=list(scratch_shapes),
                              compiler_params=compiler_params)(*operands)
        return list(outs), []
    r_ins, r_in_specs, r_out_shapes, r_out_specs, r_scratch, make = _rider(*rider)
    ni, no, ns, nr = len(in_specs), len(out_shape), len(scratch_shapes), len(r_ins)

    def riding(*refs):
        ins, rins = refs[:ni], refs[ni:ni + nr]
        outs, routs = refs[ni + nr:ni + nr + no], refs[ni + nr + no:ni + nr + no + nr]
        scr, rscr = refs[ni + 2 * nr + no:ni + 2 * nr + no + ns], refs[ni + 2 * nr + no + ns:]
        start, finish = make(rins, routs, *rscr)
        ids = [pl.program_id(a) for a in range(len(grid))]
        first, last = ids[0] == 0, ids[0] == grid[0] - 1
        for a in range(1, len(grid)):
            first = jnp.logical_and(first, ids[a] == 0)
            last = jnp.logical_and(last, ids[a] == grid[a] - 1)

        @pl.when(first)
        def _():
            start()

        body(*ins, *outs, *scr)

        @pl.when(last)
        def _():
            finish()

    outs = pl.pallas_call(
        riding, name=name, grid=grid, out_shape=tuple(out_shape) + tuple(r_out_shapes),
        in_specs=list(in_specs) + r_in_specs, out_specs=tuple(out_specs) + tuple(r_out_specs),
        scratch_shapes=list(scratch_shapes) + r_scratch, compiler_params=compiler_params)(*operands, *r_ins)
    return list(outs[:no]), list(outs[no:])


def _sibling_swap(srcs, name):
    return _exchange_call("swap", srcs, name)


def _chip_partial(own, sib, name):
    _, rows, cols = own.shape
    tm = _div(rows, max(16, (128 * 4096) // cols), 16)

    def body(c_ref, o_ref, s_ref, p_ref):
        p_ref[0] = (o_ref[0].astype(F32) + s_ref[0].astype(F32)).astype(BF16)

    core = lax.axis_index("c").astype(jnp.int32).reshape(1)
    grid_spec = pltpu.PrefetchScalarGridSpec(
        num_scalar_prefetch=1, grid=(NDEV // 2, rows // tm),
        in_specs=[pl.BlockSpec((1, tm, cols), lambda q, i, c_ref: (2 * q + c_ref[0], i, 0)),
                  pl.BlockSpec((1, tm, cols), lambda q, i, c_ref: (q, i, 0))],
        out_specs=pl.BlockSpec((1, tm, cols), lambda q, i, c_ref: (q, i, 0)))
    return pl.pallas_call(
        body, name=name, grid_spec=grid_spec, out_shape=jax.ShapeDtypeStruct(sib.shape, BF16),
        compiler_params=_cp(("arbitrary", "arbitrary")))(core, own, sib)


def _chip_exchange(parts, name):
    return _exchange_call("chipx", parts, name)


def _sum_slots(g, name):
    _, rows, cols = g.shape
    tm = _div(rows, 1024, 8)

    def body(g_ref, o_ref):
        acc = g_ref[0]
        for j in range(1, NDEV):
            acc = acc + g_ref[j]
        o_ref[...] = acc

    return pl.pallas_call(
        body, name=name, grid=(rows // tm,), out_shape=jax.ShapeDtypeStruct((rows, cols), F32),
        in_specs=[pl.BlockSpec((NDEV, tm, cols), lambda i: (0, i, 0))],
        out_specs=pl.BlockSpec((tm, cols), lambda i: (i, 0)), compiler_params=_cp(("arbitrary",)))(g)


def _mm_nn(a, b3, out_dtype, name, tm=1152, tn=1024, rider=None):
    m_, k_ = a.shape
    j_, _, nj = b3.shape
    tm = _div(m_, tm, 16)
    tn = _div(nj, tn, LANE)
    per = nj // tn

    def body(a_ref, b_ref, o_ref):
        o_ref[...] = jnp.dot(a_ref[...], b_ref[0], preferred_element_type=F32).astype(o_ref.dtype)

    outs, rode = _call_with_rider(
        body, rider, name=name, grid=(j_ * per, m_ // tm),
        out_shape=(jax.ShapeDtypeStruct((m_, j_ * nj), out_dtype),),
        in_specs=[pl.BlockSpec((tm, k_), lambda n, m: (m, 0)),
                  pl.BlockSpec((1, k_, tn), lambda n, m: (n // per, 0, n % per))],
        out_specs=(pl.BlockSpec((tm, tn), lambda n, m: (m, n)),), scratch_shapes=[],
        compiler_params=_cp(("arbitrary", "arbitrary")), operands=(a, b3))
    return outs[0] if rider is None else (outs[0], rode)


def _mm_nt(a, b3, out_dtype, name, tm=1152, tn=1024, tk=2048, rider=None):
    m_, k_ = a.shape
    j_, n_, kj = b3.shape
    tm = _div(m_, tm, 16)
    tn = _div(n_, tn, LANE)
    tk = _div(kj, tk, LANE)
    perk = kj // tk
    nk = k_ // tk

    def body(a_ref, b_ref, o_ref, acc_ref):
        k = pl.program_id(2)

        @pl.when(k == 0)
        def _():
            acc_ref[...] = jnp.zeros_like(acc_ref)

        acc_ref[...] += lax.dot_general(a_ref[...], b_ref[0], (((1,), (1,)), ((), ())), preferred_element_type=F32)

        @pl.when(k == nk - 1)
        def _():
            o_ref[...] = acc_ref[...].astype(o_ref.dtype)

    outs, rode = _call_with_rider(
        body, rider, name=name, grid=(n_ // tn, m_ // tm, nk),
        out_shape=(jax.ShapeDtypeStruct((m_, n_), out_dtype),),
        in_specs=[pl.BlockSpec((tm, tk), lambda n, m, k: (m, k)),
                  pl.BlockSpec((1, tn, tk), lambda n, m, k: (k // perk, n, k % perk))],
        out_specs=(pl.BlockSpec((tm, tn), lambda n, m, k: (m, n)),),
        scratch_shapes=[pltpu.VMEM((tm, tn), F32)],
        compiler_params=_cp(("arbitrary", "arbitrary", "arbitrary")), operands=(a, b3))
    return outs[0] if rider is None else (outs[0], rode)


def _mm_tn(a, b, out_dtype, name, oj=1, tm=1024, tn=512):
    k_, m_ = a.shape
    _, n_ = b.shape
    nj = n_ // oj
    tm = _div(m_, tm, LANE)
    tn = _div(nj, tn, LANE)
    per = nj // tn

    def body(a_ref, b_ref, o_ref):
        o_ref[0] = lax.dot_general(a_ref[...], b_ref[...], (((0,), (0,)), ((), ())),
                                   preferred_element_type=F32).astype(o_ref.dtype)

    return pl.pallas_call(
        body, name=name, grid=(m_ // tm, n_ // tn), out_shape=jax.ShapeDtypeStruct((oj, m_, nj), out_dtype),
        in_specs=[pl.BlockSpec((k_, tm), lambda m, n: (0, m)), pl.BlockSpec((k_, tn), lambda m, n: (0, n))],
        out_specs=pl.BlockSpec((1, tm, tn), lambda m, n: (n // per, m, n % per)),
        compiler_params=_cp(("arbitrary", "arbitrary")))(a, b)


def _mod_fwd(c16, w, bias):
    _, d, cols = w.shape
    tn = _div(cols, 512, LANE)

    def body(c_ref, w_ref, b_ref, o_ref):
        sc = _silu(c_ref[...])
        o_ref[0] = jnp.dot(sc, w_ref[0], preferred_element_type=F32) + b_ref[0]

    return pl.pallas_call(
        body, name="mod_fwd", grid=(2, cols // tn), out_shape=jax.ShapeDtypeStruct((2, 16, cols), F32),
        in_specs=[pl.BlockSpec((16, d), lambda l, n: (0, 0)), pl.BlockSpec((1, d, tn), lambda l, n: (l, 0, n)),
                  pl.BlockSpec((1, 1, tn), lambda l, n: (l, 0, n))],
        out_specs=pl.BlockSpec((1, 16, tn), lambda l, n: (l, 0, n)),
        compiler_params=_cp(("arbitrary", "arbitrary")))(c16, w, bias)


def _adam(w, g, m, v):
    m2 = ADAM_B1 * m + (1.0 - ADAM_B1) * g
    v2 = ADAM_B2 * v + (1.0 - ADAM_B2) * (g * g)
    m_hat = m2 / (1.0 - ADAM_B1 ** ADAM_STEP)
    v_hat = v2 / (1.0 - ADAM_B2 ** ADAM_STEP)
    delta = -ADAM_LR * (m_hat / (jnp.sqrt(v_hat) + ADAM_EPS) + ADAM_WD * w)
    return delta, m2, v2


def _mod_bwd(c16, dm, w, m, v):
    _, d, cols = w.shape
    tm = _div(d, 256, 8)

    def body(c_ref, dm_ref, w_ref, m_ref, v_ref, g_ref, dl_ref, nm_ref, nv_ref, gc_ref):
        sc = _silu(c_ref[...])
        g = lax.dot_general(sc, dm_ref[0], (((0,), (0,)), ((), ())), preferred_element_type=F32,
                            precision=lax.Precision.HIGHEST)
        wv = w_ref[0]
        dl, m2, v2 = _adam(wv, g, m_ref[0], v_ref[0])
        g_ref[0] = g
        dl_ref[0] = dl
        nm_ref[0] = m2
        nv_ref[0] = v2
        gc_ref[0] = jnp.sum(wv * dm_ref[0, 8:9, :], axis=1, keepdims=True)

    big = pl.BlockSpec((1, tm, cols), lambda l, i: (l, i, 0))
    shp = jax.ShapeDtypeStruct(w.shape, F32)
    return pl.pallas_call(
        body, name="mod_bwd", grid=(2, d // tm),
        out_shape=(shp, shp, shp, shp, jax.ShapeDtypeStruct((2, d, 1), F32)),
        in_specs=[pl.BlockSpec((16, tm), lambda l, i: (0, i)), pl.BlockSpec((1, 16, cols), lambda l, i: (l, 0, 0)),
                  big, big, big],
        out_specs=(big, big, big, big, pl.BlockSpec((1, tm, 1), lambda l, i: (l, i, 0))),
        compiler_params=_cp(("arbitrary", "arbitrary")))(c16, dm, w, m, v)


def _row_spec(tm, d, off=0):
    return pl.BlockSpec((tm, d), lambda i: (i + off, 0))


def _vec_spec(d):
    return pl.BlockSpec((1, d), lambda i: (0, 0))


def _norm_fwd(x, nw, shift, scale, name):
    n, d = x.shape
    tm = _div(n, ROW_TILE, 16)

    def body(x_ref, nw_ref, sh_ref, sc_ref, hn_ref, r_ref):
        xv = x_ref[...]
        r = lax.rsqrt(jnp.mean(xv * xv, axis=1, keepdims=True) + EPS)
        hn_ref[...] = (xv * r * nw_ref[...] * (1.0 + sc_ref[...]) + sh_ref[...]).astype(BF16)
        r_ref[...] = r

    return pl.pallas_call(
        body, name=name, grid=(n // tm,),
        out_shape=(jax.ShapeDtypeStruct((n, d), BF16), jax.ShapeDtypeStruct((n, 1), F32)),
        in_specs=[_row_spec(tm, d), _vec_spec(d), _vec_spec(d), _vec_spec(d)],
        out_specs=(_row_spec(tm, d), _row_spec(tm, 1)), compiler_params=_cp(("arbitrary",)))(x, nw, shift, scale)


def _resid_norm_fwd(x, out0, gate, nw, shift, scale):
    n, d = x.shape
    tm = _div(n, ROW_TILE, 16)

    def body(x_ref, o_ref, g_ref, nw_ref, sh_ref, sc_ref, x1_ref, hn_ref, r_ref):
        xv = x_ref[...] + g_ref[...] * o_ref[...]
        r = lax.rsqrt(jnp.mean(xv * xv, axis=1, keepdims=True) + EPS)
        x1_ref[...] = xv
        hn_ref[...] = (xv * r * nw_ref[...] * (1.0 + sc_ref[...]) + sh_ref[...]).astype(BF16)
        r_ref[...] = r

    return pl.pallas_call(
        body, name="resid_norm1", grid=(n // tm,),
        out_shape=(jax.ShapeDtypeStruct((n, d), F32), jax.ShapeDtypeStruct((n, d), BF16),
                   jax.ShapeDtypeStruct((n, 1), F32)),
        in_specs=[_row_spec(tm, d), _row_spec(tm, d), _vec_spec(d), _vec_spec(d), _vec_spec(d), _vec_spec(d)],
        out_specs=(_row_spec(tm, d), _row_spec(tm, d), _row_spec(tm, 1)),
        compiler_params=_cp(("arbitrary",)))(x, out0, gate, nw, shift, scale)


def _loss_head(x1, out1, gate1, fnw, tgt):
    n, d = x1.shape
    tm = _div(n, ROW_TILE, 16)
    nsteps = n // tm

    def body(x_ref, o_ref, g_ref, w_ref, t_ref, dx_ref, do_ref, loss_ref, dw_ref, dg_ref, lacc_ref):
        i = pl.program_id(0)

        @pl.when(i == 0)
        def _():
            lacc_ref[...] = jnp.zeros_like(lacc_ref)
            dw_ref[...] = jnp.zeros_like(dw_ref)
            dg_ref[...] = jnp.zeros_like(dg_ref)

        ov = o_ref[...]
        gv = g_ref[...]
        wv = w_ref[...]
        xv = x_ref[...] + gv * ov
        r = lax.rsqrt(jnp.mean(xv * xv, axis=1, keepdims=True) + EPS)
        xh = xv * r
        diff = xh * wv - t_ref[...]
        lacc_ref[...] += jnp.sum(diff * diff, axis=0, keepdims=True)
        dy = diff * (1.0 / d)
        dw_ref[...] += jnp.sum(dy * xh, axis=0, keepdims=True)
        dxh = dy * wv
        dx = r * (dxh - xh * jnp.mean(xh * dxh, axis=1, keepdims=True))
        dx_ref[...] = dx
        dg_ref[...] += jnp.sum(dx * ov, axis=0, keepdims=True)
        do_ref[...] = (gv * dx).astype(BF16)

        @pl.when(i == nsteps - 1)
        def _():
            loss_ref[...] = (0.5 / d) * jnp.sum(lacc_ref[...], axis=1, keepdims=True)

    vec = jax.ShapeDtypeStruct((1, d), F32)
    return pl.pallas_call(
        body, name="loss_head", grid=(nsteps,),
        out_shape=(jax.ShapeDtypeStruct((n, d), F32), jax.ShapeDtypeStruct((n, d), BF16),
                   jax.ShapeDtypeStruct((1, 1), F32), vec, vec),
        in_specs=[_row_spec(tm, d), _row_spec(tm, d), _vec_spec(d), _vec_spec(d), _row_spec(tm, d)],
        out_specs=(_row_spec(tm, d), _row_spec(tm, d), pl.BlockSpec((1, 1), lambda i: (0, 0)), _vec_spec(d),
                   _vec_spec(d)),
        scratch_shapes=[pltpu.VMEM((1, d), F32)], compiler_params=_cp(("arbitrary",)))(x1, out1, gate1, fnw, tgt)


def _norm_bwd(x, rinv, dhns, nw, scale, name, row_off=0, dx_in=None, resid=None):
    n, d = x.shape
    tm = _div(n, ROW_TILE // 2, 16)
    nd = len(dhns)
    has_in = dx_in is not None
    has_res = resid is not None

    def body(*refs):
        it = iter(refs)
        x_ref, r_ref = next(it), next(it)
        dh_refs = [next(it) for _ in range(nd)]
        nw_ref, sc_ref = next(it), next(it)
        dxin_ref = next(it) if has_in else None
        o_ref, g_ref = (next(it), next(it)) if has_res else (None, None)
        dx_ref = next(it)
        do_ref = next(it) if has_res else None
        dsh_ref, dsc_ref, dnw_ref = next(it), next(it), next(it)
        dg_ref = next(it) if has_res else None
        i = pl.program_id(0)

        @pl.when(i == 0)
        def _():
            dsh_ref[...] = jnp.zeros_like(dsh_ref)
            dsc_ref[...] = jnp.zeros_like(dsc_ref)
            dnw_ref[...] = jnp.zeros_like(dnw_ref)
            if has_res:
                dg_ref[...] = jnp.zeros_like(dg_ref)

        dh = dh_refs[0][...]
        for rr in dh_refs[1:]:
            dh = dh + rr[...]
        r = r_ref[...]
        xh = x_ref[...] * r
        nwv = nw_ref[...]
        mod = 1.0 + sc_ref[...]
        dsh_ref[...] += jnp.sum(dh, axis=0, keepdims=True)
        dhx = dh * xh
        dsc_ref[...] += jnp.sum(dhx * nwv, axis=0, keepdims=True)
        dnw_ref[...] += jnp.sum(dhx * mod, axis=0, keepdims=True)
        dxh = dh * (nwv * mod)
        dx = r * (dxh - xh * jnp.mean(xh * dxh, axis=1, keepdims=True))
        if has_in:
            dx = dx + dxin_ref[...]
        dx_ref[...] = dx
        if has_res:
            dg_ref[...] += jnp.sum(dx * o_ref[...], axis=0, keepdims=True)
            do_ref[...] = (g_ref[...] * dx).astype(BF16)

    ins = [x, rinv] + list(dhns) + [nw, scale]
    in_specs = [_row_spec(tm, d), _row_spec(tm, 1)] + [_row_spec(tm, d, row_off)] * nd + [_vec_spec(d), _vec_spec(d)]
    if has_in:
        ins.append(dx_in)
        in_specs.append(_row_spec(tm, d))
    if has_res:
        ins += list(resid)
        in_specs += [_row_spec(tm, d), _vec_spec(d)]
    vec = jax.ShapeDtypeStruct((1, d), F32)
    out_shape = [jax.ShapeDtypeStruct((n, d), F32)]
    out_specs = [_row_spec(tm, d)]
    if has_res:
        out_shape.append(jax.ShapeDtypeStruct((n, d), BF16))
        out_specs.append(_row_spec(tm, d))
    out_shape += [vec, vec, vec]
    out_specs += [_vec_spec(d)] * 3
    if has_res:
        out_shape.append(vec)
        out_specs.append(_vec_spec(d))
    return pl.pallas_call(
        body, name=name, grid=(n // tm,), out_shape=tuple(out_shape), in_specs=in_specs, out_specs=tuple(out_specs),
        compiler_params=_cp(("arbitrary",)))(*ins)


def _seg_masks(rows, width, seg_bounds, offs):
    t = lax.broadcasted_iota(jnp.int32, (rows, width), 0)
    s0 = jnp.zeros_like(t)
    s1 = jnp.full_like(t, rows)
    for (a, b) in seg_bounds:
        inside = (t >= a) & (t < b)
        s0 = jnp.where(inside, a, s0)
        s1 = jnp.where(inside, b, s1)
    return {off: ((t + off >= s0) & (t + off < s1)) for off in offs}


def _shift_rows(v, off, valid):
    rows = v.shape[0]
    return jnp.where(valid, pltpu.roll(v, (-off) % rows, axis=0), 0.0)


def _conv_fwd_val(xv, wv, taps, masks):
    acc = None
    for tap, off in taps:
        xs = xv if off == 0 else _shift_rows(xv, off, masks[off])
        term = wv[tap:tap + 1, :] * xs
        acc = term if acc is None else acc + term
    return acc


def _conv_bwd_val(xv, dacc, wv, taps, masks):
    dx = None
    dws = []
    for tap, off in taps:
        xs = xv if off == 0 else _shift_rows(xv, off, masks[off])
        dws.append(jnp.sum(dacc * xs, axis=0, keepdims=True))
        ds = dacc if off == 0 else _shift_rows(dacc, -off, masks[-off])
        term = wv[tap:tap + 1, :] * ds
        dx = term if dx is None else dx + term
    return dx, dws


def _taps(k):
    left = k // 2
    return [(j, j - left) for j in range(k)]


def _seq_bounds():
    return [(0, CTX_LEN), (CTX_LEN, CTX_LEN + SEQ)]


def _dn_pre_fwd(proj, convw):
    dd = _dims()
    r = dd["r"]
    nstrip = 3 * DN_HEADS
    taps = _taps(DN_CONV)
    offs = [o for _, o in taps if o != 0]

    def body(x_ref, w_ref, o_ref):
        kind = pl.program_id(0) // DN_HEADS
        masks = _seg_masks(r, DN_HEAD_DIM, _seq_bounds(), offs)
        y = _silu(_conv_fwd_val(x_ref[...], w_ref[...], taps, masks))
        inv = lax.rsqrt(jnp.sum(y * y, axis=1, keepdims=True) + EPS)
        scale = jnp.where(kind == 0, inv * (DN_HEAD_DIM ** -0.5), jnp.where(kind == 1, inv, 1.0))
        o_ref[...] = y * scale

    return pl.pallas_call(
        body, name="dn_pre_fwd", grid=(nstrip,), out_shape=jax.ShapeDtypeStruct((r, 3 * dd["dnw"]), F32),
        in_specs=[pl.BlockSpec((r, DN_HEAD_DIM), lambda j: (0, j)), pl.BlockSpec((8, DN_HEAD_DIM), lambda j: (0, j))],
        out_specs=pl.BlockSpec((r, DN_HEAD_DIM), lambda j: (0, j)), compiler_params=_cp(("arbitrary",)))(proj, convw)


def _dn_pre_bwd(proj, convw, dqkv, kind):
    dd = _dims()
    r = dd["r"]
    taps = _taps(DN_CONV)
    offs = sorted({o for _, o in taps if o != 0} | {-o for _, o in taps if o != 0})
    cscale = DN_HEAD_DIM ** -0.5 if kind == 0 else 1.0

    def body(x_ref, w_ref, d_ref, dx_ref, dw_ref):
        masks = _seg_masks(r, DN_HEAD_DIM, _seq_bounds(), offs)
        xv = x_ref[...]
        wv = w_ref[...]
        acc = _conv_fwd_val(xv, wv, taps, masks)
        dout = d_ref[0] + d_ref[1]
        if kind == 2:
            dy = dout
        else:
            y = _silu(acc)
            inv = lax.rsqrt(jnp.sum(y * y, axis=1, keepdims=True) + EPS)
            dy = cscale * inv * (dout - y * (inv * inv) * jnp.sum(dout * y, axis=1, keepdims=True))
        dacc = dy * _dsilu(acc)
        dx, dws = _conv_bwd_val(xv, dacc, wv, taps, masks)
        dx_ref[...] = dx.astype(BF16)
        dw_ref[...] = jnp.concatenate(dws + [jnp.zeros((8 - len(dws), DN_HEAD_DIM), F32)], axis=0)

    base = kind * DN_HEADS
    return pl.pallas_call(
        body, name=f"dn_pre_bwd{kind}", grid=(DN_HEADS,),
        out_shape=(jax.ShapeDtypeStruct((r, dd["dnw"]), BF16), jax.ShapeDtypeStruct((8, dd["dnw"]), F32)),
        in_specs=[pl.BlockSpec((r, DN_HEAD_DIM), lambda j: (0, base + j)),
                  pl.BlockSpec((8, DN_HEAD_DIM), lambda j: (0, base + j)),
                  pl.BlockSpec((2, r, DN_HEAD_DIM), lambda j: (0, 0, j))],
        out_specs=(pl.BlockSpec((r, DN_HEAD_DIM), lambda j: (0, j)), pl.BlockSpec((8, DN_HEAD_DIM), lambda j: (0, j))),
        compiler_params=_cp(("arbitrary",)))(proj, convw, dqkv)


def _lru_conv_fwd(xl, convw):
    r, lw = xl.shape
    taps = _taps(LRU_CONV)
    offs = [o for _, o in taps if o != 0]

    def body(x_ref, w_ref, o_ref):
        masks = _seg_masks(r, LANE, _seq_bounds(), offs)
        wv = w_ref[...]
        o_ref[...] = _conv_fwd_val(x_ref[...], wv, taps, masks) + wv[4:5, :]

    return pl.pallas_call(
        body, name="lru_conv_fwd", grid=(lw // LANE,), out_shape=jax.ShapeDtypeStruct((r, lw), F32),
        in_specs=[pl.BlockSpec((r, LANE), lambda j: (0, j)), pl.BlockSpec((8, LANE), lambda j: (0, j))],
        out_specs=pl.BlockSpec((r, LANE), lambda j: (0, j)), compiler_params=_cp(("arbitrary",)))(xl, convw)


def _lru_conv_bwd(xl, convw, dxc):
    r, lw = xl.shape
    taps = _taps(LRU_CONV)
    offs = sorted({o for _, o in taps if o != 0} | {-o for _, o in taps if o != 0})

    def body(x_ref, w_ref, d_ref, dx_ref, dw_ref):
        masks = _seg_masks(r, LANE, _seq_bounds(), offs)
        dacc = d_ref[0] + d_ref[1]
        dx, dws = _conv_bwd_val(x_ref[...], dacc, w_ref[...], taps, masks)
        dx_ref[...] = dx
        db = jnp.sum(dacc, axis=0, keepdims=True)
        dw_ref[...] = jnp.concatenate(dws + [db, jnp.zeros((3, LANE), F32)], axis=0)

    return pl.pallas_call(
        body, name="lru_conv_bwd", grid=(lw // LANE,),
        out_shape=(jax.ShapeDtypeStruct((r, lw), F32), jax.ShapeDtypeStruct((8, lw), F32)),
        in_specs=[pl.BlockSpec((r, LANE), lambda j: (0, j)), pl.BlockSpec((8, LANE), lambda j: (0, j)),
                  pl.BlockSpec((2, r, LANE), lambda j: (0, 0, j))],
        out_specs=(pl.BlockSpec((r, LANE), lambda j: (0, j)), pl.BlockSpec((8, LANE), lambda j: (0, j))),
        compiler_params=_cp(("arbitrary",)))(xl, convw, dxc)


def _tri(rev_lane, n, m):
    r = lax.broadcasted_iota(jnp.int32, (n, n), 0)
    c = lax.broadcasted_iota(jnp.int32, (n, n), 1)
    return (r >= c).astype(F32), (r <= c).astype(F32)


def _hdot(a, b):
    return jnp.dot(a, b, preferred_element_type=F32, precision=lax.Precision.HIGHEST)


def _dn_gates_fwd(pba, par):
    r = pba.shape[0]
    nch = r // DN_CHUNK
    hh = 2 * DN_HEADS

    def body(x_ref, p_ref, o_ref):
        lo, up = _tri(None, DN_CHUNK, DN_CHUNK)
        ones = jnp.ones((DN_CHUNK, DN_CHUNK), F32)
        lane = lax.broadcasted_iota(jnp.int32, (DN_CHUNK, LANE), 1)
        group = lane // 32
        rev = (lane % 32) >= DN_HEADS
        dtb = p_ref[0:1, :]
        nea = p_ref[1:2, :]

        def step(i, carry):
            rows = pl.ds(pl.multiple_of(i * DN_CHUNK, DN_CHUNK), DN_CHUNK)
            xv = x_ref[rows, :]
            beta = jax.nn.sigmoid(xv)
            g = jnp.where(group == 1, nea * _softplus(xv + dtb), 0.0)
            gc = jnp.where(rev, _hdot(up, g), _hdot(lo, g))
            gt = _hdot(ones, g)
            out = jnp.where(group == 0, beta, jnp.where(group == 1, g, 0.0))
            out = out + jnp.where(group == 2, pltpu.roll(gc, 32, axis=1), 0.0)
            out = out + jnp.where(group == 3, pltpu.roll(gt, 64, axis=1), 0.0)
            o_ref[rows, :] = out
            return carry

        lax.fori_loop(0, nch, step, 0)

    return pl.pallas_call(body, name="dn_gates_fwd", out_shape=jax.ShapeDtypeStruct((r, LANE), F32),
                          compiler_params=_cp(None))(pba, par)


def _dn_gates_bwd(pba, par, dsc):
    r = pba.shape[0]
    nch = r // DN_CHUNK

    def body(x_ref, p_ref, d_ref, dx_ref, dp_ref, acc_ref):
        lo, up = _tri(None, DN_CHUNK, DN_CHUNK)
        ones = jnp.ones((DN_CHUNK, DN_CHUNK), F32)
        lane = lax.broadcasted_iota(jnp.int32, (DN_CHUNK, LANE), 1)
        group = lane // 32
        rev = (lane % 32) >= DN_HEADS
        dtb = p_ref[0:1, :]
        nea = p_ref[1:2, :]
        acc_ref[...] = jnp.zeros_like(acc_ref)

        def step(i, carry):
            rows = pl.ds(pl.multiple_of(i * DN_CHUNK, DN_CHUNK), DN_CHUNK)
            xv = x_ref[rows, :]
            dv = d_ref[rows, :]
            beta = jax.nn.sigmoid(xv)
            z = xv + dtb
            sp = _softplus(z)
            dgc = jnp.where(group == 1, pltpu.roll(dv, LANE - 32, axis=1), 0.0)
            dgt = jnp.where(group == 1, pltpu.roll(dv, LANE - 64, axis=1), 0.0)
            dg = jnp.where(rev, _hdot(lo, dgc), _hdot(up, dgc)) + _hdot(ones, dgt)
            dg = dg + jnp.where(group == 1, dv, 0.0)
            dal = dg * nea * jax.nn.sigmoid(z)
            dpb = dv * beta * (1.0 - beta)
            dx_ref[rows, :] = jnp.where(group == 0, dpb, jnp.where(group == 1, dal, 0.0)).astype(BF16)
            acc_ref[0:1, :] += jnp.sum(jnp.where(group == 1, dal, 0.0), axis=0, keepdims=True)
            acc_ref[1:2, :] += jnp.sum(jnp.where(group == 1, dg * nea * sp, 0.0), axis=0, keepdims=True)
            return carry

        lax.fori_loop(0, nch, step, 0)
        dp_ref[...] = acc_ref[...]

    return pl.pallas_call(
        body, name="dn_gates_bwd",
        out_shape=(jax.ShapeDtypeStruct((r, LANE), BF16), jax.ShapeDtypeStruct((8, LANE), F32)),
        scratch_shapes=[pltpu.VMEM((8, LANE), F32)], compiler_params=_cp(None))(pba, par, dsc)


_B_NN = (((2,), (1,)), ((0,), (0,)))
_B_NT = (((2,), (2,)), ((0,), (0,)))
_B_TN = (((1,), (1,)), ((0,), (0,)))


def _bdot(a, b, dims, exact=False):
    return lax.dot_general(a, b, dims, preferred_element_type=F32,
                           precision=lax.Precision.HIGH if exact else None)


@jax.custom_vjp
def _inv_unit_tri(l):
    n = l.shape[-1]
    ri = lax.broadcasted_iota(jnp.int32, (n, n), 0)
    ci = lax.broadcasted_iota(jnp.int32, (n, n), 1)
    x = jnp.broadcast_to((ri == ci).astype(F32), l.shape)
    inner = 1
    for size in (4, 16, n):
        so, si = size.bit_length() - 1, inner.bit_length() - 1
        same_outer = (ri >> so) == (ci >> so)
        same_inner = (ri >> si) == (ci >> si)
        part = jnp.where(same_outer, jnp.where(same_inner, 0.0, l), 0.0)
        m = part if inner == 1 else _bdot(x, part, _B_NN, exact=True)
        y = x
        for _ in range(size // inner - 1):
            y = x - _bdot(m, y, _B_NN, exact=True)
        x = y
        inner = size
    return x


def _inv_fwd(l):
    a = _inv_unit_tri(l)
    return a, a


def _inv_bwd(a, da):
    t = _bdot(a, da, _B_TN, exact=True)
    return (-_bdot(t, a, _B_NT, exact=True),)


_inv_unit_tri.defvjp(_inv_fwd, _inv_bwd)


def _delta_chunk(q4, k4, v4, sc, s4, rev):
    hps = HEADS_PER_STEP
    c = DN_CHUNK
    hd = DN_HEAD_DIM
    ri = lax.broadcasted_iota(jnp.int32, (c, c), 0)
    ci = lax.broadcasted_iota(jnp.int32, (c, c), 1)
    ahead = (ri - ci) * jnp.where(rev, -1, 1)
    incl = ahead >= 0
    strict = ahead > 0
    sct = sc.T

    def heads(x):
        return jnp.concatenate([x[None, :, j * hd:(j + 1) * hd] for j in range(hps)], axis=0)

    def col(off):
        return jnp.concatenate([sc[None, :, off + j:off + j + 1] for j in range(hps)], axis=0)

    def row(off):
        return jnp.concatenate([sct[None, off + j:off + j + 1, :] for j in range(hps)], axis=0)

    q, k, v = heads(q4), heads(k4), heads(v4)
    beta_c, gc_c, gt_c = col(0), col(hps), col(2 * hps)
    beta_r, gc_r = row(0), row(hps)
    dmat = jnp.where(incl, jnp.exp(jnp.where(incl, gc_c - gc_r, 0.0)), 0.0)
    kk = _bdot(k, k, _B_NT, exact=True)
    qk = _bdot(q, k, _B_NT)
    a = _inv_unit_tri(jnp.where(strict, beta_c * kk * dmat, 0.0))
    w = _bdot(a * (beta_r * jnp.exp(gc_r)), k, _B_NN, exact=True)
    u = _bdot(a * beta_r, v, _B_NN, exact=True)
    u2 = u - _bdot(w, s4, _B_NN)
    o = jnp.exp(gc_c) * _bdot(q, s4, _B_NN) + _bdot(qk * dmat, u2, _B_NN)
    ke = k * jnp.exp(gt_c - gc_c)
    s_new = jnp.exp(gt_c[:, 0:1, :]) * s4 + _bdot(ke, u2, _B_TN)
    return jnp.concatenate([o[j] for j in range(hps)], axis=1), s_new


def _chunk_index(d, n, nctx, ntot):
    rev_idx = jnp.where(n < nctx, nctx - 1 - n, ntot - 1 - (n - nctx))
    return jnp.where(d == 0, n, rev_idx)


def _delta_fwd(qkv, scg, rider=None):
    dd = _dims()
    r, dnw = dd["r"], dd["dnw"]
    hps = HEADS_PER_STEP
    nhg = DN_HEADS // hps
    bw = hps * DN_HEAD_DIM
    nctx, ntot = CTX_LEN // DN_CHUNK, r // DN_CHUNK

    def body(q_ref, k_ref, v_ref, sc_ref, o_ref, ss_ref, s_ref):
        d, n = pl.program_id(0), pl.program_id(2)

        @pl.when(n == 0)
        def _():
            s_ref[...] = jnp.zeros_like(s_ref)

        s4 = s_ref[...]
        ss_ref[0, 0] = s4
        o4, s4n = _delta_chunk(q_ref[...], k_ref[...], v_ref[...], sc_ref[0], s4, d == 1)
        o_ref[0] = o4
        s_ref[...] = s4n

    def rows(d, g, n):
        return _chunk_index(d, n, nctx, ntot)

    return _call_with_rider(
        body, rider, name="delta_fwd", grid=(2, nhg, ntot),
        out_shape=(jax.ShapeDtypeStruct((2, r, dnw), F32),
                   jax.ShapeDtypeStruct((2, ntot, DN_HEADS, DN_HEAD_DIM, DN_HEAD_DIM), F32)),
        in_specs=[pl.BlockSpec((DN_CHUNK, bw), lambda d, g, n: (rows(d, g, n), g)),
                  pl.BlockSpec((DN_CHUNK, bw), lambda d, g, n: (rows(d, g, n), nhg + g)),
                  pl.BlockSpec((DN_CHUNK, bw), lambda d, g, n: (rows(d, g, n), 2 * nhg + g)),
                  pl.BlockSpec((1, DN_CHUNK, LANE), lambda d, g, n: (d * nhg + g, rows(d, g, n), 0))],
        out_specs=(pl.BlockSpec((1, DN_CHUNK, bw), lambda d, g, n: (d, rows(d, g, n), g)),
                   pl.BlockSpec((1, 1, hps, DN_HEAD_DIM, DN_HEAD_DIM), lambda d, g, n: (d, n, g, 0, 0))),
        scratch_shapes=[pltpu.VMEM((hps, DN_HEAD_DIM, DN_HEAD_DIM), F32)],
        compiler_params=_cp(("arbitrary", "arbitrary", "arbitrary")), operands=(qkv, qkv, qkv, scg))


def _delta_bwd(qkv, scg, states, do, rider=None):
    dd = _dims()
    r, dnw = dd["r"], dd["dnw"]
    hps = HEADS_PER_STEP
    nhg = DN_HEADS // hps
    bw = hps * DN_HEAD_DIM
    nctx, ntot = CTX_LEN // DN_CHUNK, r // DN_CHUNK

    def body(q_ref, k_ref, v_ref, sc_ref, ss_ref, do_ref, dq_ref, dk_ref, dv_ref, dsc_ref, ds_ref):
        d, n2 = pl.program_id(0), pl.program_id(2)

        @pl.when(n2 == 0)
        def _():
            ds_ref[...] = jnp.zeros_like(ds_ref)

        fn = functools.partial(_delta_chunk, rev=(d == 1))
        _, vjp = jax.vjp(fn, q_ref[...], k_ref[...], v_ref[...], sc_ref[0], ss_ref[0, 0])
        dq, dk, dv, dsc, ds = vjp((do_ref[...], ds_ref[...]))
        dq_ref[0] = dq
        dk_ref[0] = dk
        dv_ref[0] = dv
        dsc_ref[0] = dsc
        ds_ref[...] = ds

    def rows(d, g, n2):
        return _chunk_index(d, ntot - 1 - n2, nctx, ntot)

    blk = lambda col: pl.BlockSpec((DN_CHUNK, bw), lambda d, g, n2: (rows(d, g, n2), col(g)))
    oblk = pl.BlockSpec((1, DN_CHUNK, bw), lambda d, g, n2: (d, rows(d, g, n2), g))
    scblk = pl.BlockSpec((1, DN_CHUNK, LANE), lambda d, g, n2: (d * nhg + g, rows(d, g, n2), 0))
    big = jax.ShapeDtypeStruct((2, r, dnw), F32)
    return _call_with_rider(
        body, rider, name="delta_bwd", grid=(2, nhg, ntot),
        out_shape=(big, big, big, jax.ShapeDtypeStruct(scg.shape, F32)),
        in_specs=[blk(lambda g: g), blk(lambda g: nhg + g), blk(lambda g: 2 * nhg + g), scblk,
                  pl.BlockSpec((1, 1, hps, DN_HEAD_DIM, DN_HEAD_DIM), lambda d, g, n2: (d, ntot - 1 - n2, g, 0, 0)),
                  blk(lambda g: g)],
        out_specs=(oblk, oblk, oblk, scblk),
        scratch_shapes=[pltpu.VMEM((hps, DN_HEAD_DIM, DN_HEAD_DIM), F32)],
        compiler_params=_cp(("arbitrary", "arbitrary", "arbitrary")), operands=(qkv, qkv, qkv, scg, states, do))


def _lru_gate_vals(xn, wr, wi, br, bi, lam):
    r = jax.nn.sigmoid(jnp.dot(xn, wr, preferred_element_type=F32) + br)
    i = jax.nn.sigmoid(jnp.dot(xn, wi, preferred_element_type=F32) + bi)
    sp = _softplus(-lam)
    la = -LRU_C * r * sp
    a = jnp.exp(la)
    mm = jnp.sqrt(_neg_expm1(2.0 * la))
    return r, i, sp, la, a, mm


def _lru_gates_fwd(xc, w_r, w_i, vecs):
    r, lw = xc.shape
    bd = lw // LRU_BLOCKS
    tm = _div(r, ROW_TILE, 8)

    def body(x_ref, wr_ref, wi_ref, v_ref, a_ref, b_ref):
        for n in range(LRU_BLOCKS):
            sl = slice(n * bd, (n + 1) * bd)
            xn = x_ref[:, sl]
            _, i, _, _, a, mm = _lru_gate_vals(xn, wr_ref[0, n], wi_ref[0, n], v_ref[0, 0:1, sl], v_ref[0, 1:2, sl],
                                               v_ref[0, 2:3, sl])
            a_ref[0, :, sl] = a
            b_ref[0, :, sl] = mm * (i * xn)

    wspec = pl.BlockSpec((1, LRU_BLOCKS, bd, bd), lambda d, i: (d, 0, 0, 0))
    out = jax.ShapeDtypeStruct((2, r, lw), F32)
    ospec = pl.BlockSpec((1, tm, lw), lambda d, i: (d, i, 0))
    return pl.pallas_call(
        body, name="lru_gates_fwd", grid=(2, r // tm), out_shape=(out, out),
        in_specs=[pl.BlockSpec((tm, lw), lambda d, i: (i, 0)), wspec, wspec,
                  pl.BlockSpec((1, 8, lw), lambda d, i: (d, 0, 0))],
        out_specs=(ospec, ospec), compiler_params=_cp(("arbitrary", "arbitrary")))(xc, w_r, w_i, vecs)


def _lru_gates_bwd(xc, w_r, w_i, vecs, da, db):
    r, lw = xc.shape
    bd = lw // LRU_BLOCKS
    tm = _div(r, ROW_TILE, 8)

    def body(x_ref, wr_ref, wi_ref, v_ref, da_ref, db_ref, dx_ref, dwr_ref, dwi_ref, dv_ref):
        @pl.when(pl.program_id(1) == 0)
        def _():
            dwr_ref[...] = jnp.zeros_like(dwr_ref)
            dwi_ref[...] = jnp.zeros_like(dwi_ref)
            dv_ref[...] = jnp.zeros_like(dv_ref)

        for n in range(LRU_BLOCKS):
            sl = slice(n * bd, (n + 1) * bd)
            xn = x_ref[:, sl]
            wr, wi = wr_ref[0, n], wi_ref[0, n]
            lam = v_ref[0, 2:3, sl]
            rr, ii, sp, la, a, mm = _lru_gate_vals(xn, wr, wi, v_ref[0, 0:1, sl], v_ref[0, 1:2, sl], lam)
            dav, dbv = da_ref[0, :, sl], db_ref[0, :, sl]
            dmm = dbv * (ii * xn)
            dla = dav * a - dmm * (a * a) / jnp.maximum(mm, 1e-30)
            di = dbv * mm * xn
            dxn = dbv * mm * ii
            dr = dla * (-LRU_C * sp)
            dsp = jnp.sum(dla * (-LRU_C * rr), axis=0, keepdims=True)
            dpr = dr * rr * (1.0 - rr)
            dpi = di * ii * (1.0 - ii)
            dxn = dxn + lax.dot_general(dpr, wr, (((1,), (1,)), ((), ())), preferred_element_type=F32)
            dxn = dxn + lax.dot_general(dpi, wi, (((1,), (1,)), ((), ())), preferred_element_type=F32)
            dx_ref[0, :, sl] = dxn
            dwr_ref[0, n] += lax.dot_general(xn, dpr, (((0,), (0,)), ((), ())), preferred_element_type=F32)
            dwi_ref[0, n] += lax.dot_general(xn, dpi, (((0,), (0,)), ((), ())), preferred_element_type=F32)
            dv_ref[0, 0:1, sl] += jnp.sum(dpr, axis=0, keepdims=True)
            dv_ref[0, 1:2, sl] += jnp.sum(dpi, axis=0, keepdims=True)
            dv_ref[0, 2:3, sl] += dsp * (-jax.nn.sigmoid(-lam))

    wspec = pl.BlockSpec((1, LRU_BLOCKS, bd, bd), lambda d, i: (d, 0, 0, 0))
    vspec = pl.BlockSpec((1, 8, lw), lambda d, i: (d, 0, 0))
    big = pl.BlockSpec((1, tm, lw), lambda d, i: (d, i, 0))
    return pl.pallas_call(
        body, name="lru_gates_bwd", grid=(2, r // tm),
        out_shape=(jax.ShapeDtypeStruct((2, r, lw), F32), jax.ShapeDtypeStruct(w_r.shape, F32),
                   jax.ShapeDtypeStruct(w_i.shape, F32), jax.ShapeDtypeStruct((2, 8, lw), F32)),
        in_specs=[pl.BlockSpec((tm, lw), lambda d, i: (i, 0)), wspec, wspec, vspec, big, big],
        out_specs=(big, wspec, wspec, vspec), compiler_params=_cp(("arbitrary", "arbitrary")))(xc, w_r, w_i, vecs, da, db)


def _scan_block(d, i, nctx, ntot):
    return _chunk_index(d, i, nctx, ntot)


def _lru_scan_fwd(a, b):
    _, r, lw = a.shape
    rb = _div(CTX_LEN, ROW_TILE, 8)
    nctx, ntot, ntile = CTX_LEN // rb, r // rb, rb // 8

    def body(a_ref, b_ref, h_ref, hp_ref, c_ref):
        d, i = pl.program_id(0), pl.program_id(1)

        @pl.when(i == 0)
        def _():
            c_ref[...] = jnp.zeros_like(c_ref)

        def sweep(rev):
            order = list(range(7, -1, -1)) if rev else list(range(8))

            def step(s, carry):
                t0 = pl.multiple_of((ntile - 1 - s if rev else s) * 8, 8)
                at = a_ref[0, pl.ds(t0, 8), :]
                bt = b_ref[0, pl.ds(t0, 8), :]
                rows, prevs = [None] * 8, [None] * 8
                h = carry
                for j in order:
                    prevs[j] = h
                    h = at[j:j + 1, :] * h + bt[j:j + 1, :]
                    rows[j] = h
                h_ref[0, pl.ds(t0, 8), :] = jnp.concatenate(rows, axis=0)
                hp_ref[0, pl.ds(t0, 8), :] = jnp.concatenate(prevs, axis=0)
                return h

            c_ref[...] = lax.fori_loop(0, ntile, step, c_ref[...])

        @pl.when(d == 0)
        def _():
            sweep(False)

        @pl.when(d == 1)
        def _():
            sweep(True)

    blk = pl.BlockSpec((1, rb, lw), lambda d, i: (d, _scan_block(d, i, nctx, ntot), 0))
    out = jax.ShapeDtypeStruct((2, r, lw), F32)
    return pl.pallas_call(
        body, name="lru_scan_fwd", grid=(2, ntot), out_shape=(out, out), in_specs=[blk, blk], out_specs=(blk, blk),
        scratch_shapes=[pltpu.VMEM((1, lw), F32)], compiler_params=_cp(("arbitrary", "arbitrary")))(a, b)


def _lru_scan_bwd(a, hprev, dh):
    _, r, lw = a.shape
    rb = _div(CTX_LEN, ROW_TILE, 8)
    nctx, ntot, ntile = CTX_LEN // rb, r // rb, rb // 8

    def body(a_ref, hp_ref, dh_ref, da_ref, db_ref, c_ref):
        d, i = pl.program_id(0), pl.program_id(1)

        @pl.when(i == 0)
        def _():
            c_ref[...] = jnp.zeros_like(c_ref)

        def sweep(rev):
            order = list(range(8)) if rev else list(range(7, -1, -1))

            def step(s, carry):
                t0 = pl.multiple_of((s if rev else ntile - 1 - s) * 8, 8)
                at = a_ref[0, pl.ds(t0, 8), :]
                hpt = hp_ref[0, pl.ds(t0, 8), :]
                dht = dh_ref[pl.ds(t0, 8), :]
                das, dbs = [None] * 8, [None] * 8
                c = carry
                for j in order:
                    lam = dht[j:j + 1, :] + c
                    das[j] = lam * hpt[j:j + 1, :]
                    dbs[j] = lam
                    c = at[j:j + 1, :] * lam
                da_ref[0, pl.ds(t0, 8), :] = jnp.concatenate(das, axis=0)
                db_ref[0, pl.ds(t0, 8), :] = jnp.concatenate(dbs, axis=0)
                return c

            c_ref[...] = lax.fori_loop(0, ntile, step, c_ref[...])

        @pl.when(d == 0)
        def _():
            sweep(False)

        @pl.when(d == 1)
        def _():
            sweep(True)

    def rows(d, i):
        return _scan_block(d, ntot - 1 - i, nctx, ntot)

    blk = pl.BlockSpec((1, rb, lw), lambda d, i: (d, rows(d, i), 0))
    out = jax.ShapeDtypeStruct((2, r, lw), F32)
    return pl.pallas_call(
        body, name="lru_scan_bwd", grid=(2, ntot), out_shape=(out, out),
        in_specs=[blk, blk, pl.BlockSpec((rb, lw), lambda d, i: (rows(d, i), 0))], out_specs=(blk, blk),
        scratch_shapes=[pltpu.VMEM((1, lw), F32)], compiler_params=_cp(("arbitrary", "arbitrary")))(a, hprev, dh)


def _mix_post_fwd(o, hsum, proj, dnw_vec):
    dd = _dims()
    dnw, lw, nmain = dd["dnw"], dd["lw"], dd["nmain"]
    tm = _div(SEQ, ROW_TILE, 16)
    coff = CTX_LEN // tm
    gcol = (3 * dnw + lw) // dnw
    lcol = (3 * dnw + lw + dnw) // lw
    hd = DN_HEAD_DIM

    def body(o_ref, h_ref, g_ref, lg_ref, w_ref, y_ref):
        wv = w_ref[...]
        for hh in range(DN_HEADS):
            sl = slice(hh * hd, (hh + 1) * hd)
            os_ = o_ref[0, :, sl] + o_ref[1, :, sl]
            rr = lax.rsqrt(jnp.mean(os_ * os_, axis=1, keepdims=True) + EPS)
            y_ref[:, sl] = (os_ * rr * wv * _silu(g_ref[:, sl])).astype(BF16)
        y_ref[:, dnw:] = (h_ref[...] * _silu(lg_ref[...])).astype(BF16)

    return pl.pallas_call(
        body, name="mix_post_fwd", grid=(SEQ // tm,), out_shape=jax.ShapeDtypeStruct((SEQ, dnw + lw), BF16),
        in_specs=[pl.BlockSpec((2, tm, dnw), lambda i: (0, i + coff, 0)), pl.BlockSpec((tm, lw), lambda i: (i, 0)),
                  pl.BlockSpec((tm, dnw), lambda i: (i + coff, gcol)), pl.BlockSpec((tm, lw), lambda i: (i + coff, lcol)),
                  pl.BlockSpec((1, hd), lambda i: (0, 0))],
        out_specs=pl.BlockSpec((tm, dnw + lw), lambda i: (i, 0)), compiler_params=_cp(("arbitrary",)))(
            o, hsum, proj, proj, dnw_vec)


def _mix_post_bwd(o, hsum, proj, dnw_vec, dy):
    dd = _dims()
    dnw, lw, r = dd["dnw"], dd["lw"], dd["r"]
    tm = _div(SEQ, ROW_TILE, 16)
    coff = CTX_LEN // tm
    gcol = (3 * dnw + lw) // dnw
    lcol = (3 * dnw + lw + dnw) // lw
    hd = DN_HEAD_DIM

    def body(o_ref, h_ref, g_ref, lg_ref, w_ref, dy_ref, do_ref, dh_ref, dg_ref, dw_ref):
        i = pl.program_id(0)

        @pl.when(i == 0)
        def _():
            dw_ref[...] = jnp.zeros_like(dw_ref)

        @pl.when(i < coff)
        def _():
            do_ref[...] = jnp.zeros_like(do_ref)
            dg_ref[...] = jnp.zeros_like(dg_ref)

        @pl.when(i >= coff)
        def _():
            wv = w_ref[...]
            dwacc = jnp.zeros((1, hd), F32)
            for hh in range(DN_HEADS):
                sl = slice(hh * hd, (hh + 1) * hd)
                os_ = o_ref[0, :, sl] + o_ref[1, :, sl]
                rr = lax.rsqrt(jnp.mean(os_ * os_, axis=1, keepdims=True) + EPS)
                nn = os_ * rr
                gv = g_ref[:, sl]
                dyv = dy_ref[:, sl]
                don = dyv * _silu(gv)
                dg_ref[:, sl] = (dyv * nn * wv * _dsilu(gv)).astype(BF16)
                dwacc = dwacc + jnp.sum(don * nn, axis=0, keepdims=True)
                dn_ = don * wv
                do_ref[:, sl] = rr * (dn_ - nn * jnp.mean(nn * dn_, axis=1, keepdims=True))
            dw_ref[...] += dwacc
            lgv = lg_ref[...]
            dyl = dy_ref[:, dnw:]
            dh_ref[...] = dyl * _silu(lgv)
            dg_ref[:, dnw:] = (dyl * h_ref[...] * _dsilu(lgv)).astype(BF16)

    lat = lambda i: jnp.maximum(i - coff, 0)
    return pl.pallas_call(
        body, name="mix_post_bwd", grid=(r // tm,),
        out_shape=(jax.ShapeDtypeStruct((r, dnw), F32), jax.ShapeDtypeStruct((SEQ, lw), F32),
                   jax.ShapeDtypeStruct((r, dnw + lw), BF16), jax.ShapeDtypeStruct((1, hd), F32)),
        in_specs=[pl.BlockSpec((2, tm, dnw), lambda i: (0, i, 0)), pl.BlockSpec((tm, lw), lambda i: (lat(i), 0)),
                  pl.BlockSpec((tm, dnw), lambda i: (i, gcol)), pl.BlockSpec((tm, lw), lambda i: (i, lcol)),
                  pl.BlockSpec((1, hd), lambda i: (0, 0)), pl.BlockSpec((tm, dnw + lw), lambda i: (lat(i), 0))],
        out_specs=(pl.BlockSpec((tm, dnw), lambda i: (i, 0)), pl.BlockSpec((tm, lw), lambda i: (lat(i), 0)),
                   pl.BlockSpec((tm, dnw + lw), lambda i: (i, 0)), pl.BlockSpec((1, hd), lambda i: (0, 0))),
        compiler_params=_cp(("arbitrary",)))(o, hsum, proj, proj, dnw_vec, dy)


def _sc_masks(tm, cw):
    t = lax.broadcasted_iota(jnp.int32, (tm, cw), 0) % GRID_W
    return {-1: t != 0, 1: t != GRID_W - 1}


def _sc_mix_fwd(p, convw):
    s, w4 = p.shape
    w = w4 // 4
    tm = _div(s, 512, GRID_W if GRID_W % 16 == 0 else 16 * GRID_W)
    cw = _div(w, 512, LANE)
    nc = w // cw
    taps = _taps(SC_CONV)

    def body(b_ref, c_ref, x_ref, g_ref, w_ref, y_ref):
        masks = _sc_masks(tm, cw)
        u = c_ref[...] * x_ref[...]
        z = _conv_fwd_val(u, w_ref[...], taps, masks)
        y_ref[...] = (b_ref[...] * z * _silu(g_ref[...])).astype(BF16)

    blk = lambda k: pl.BlockSpec((tm, cw), lambda c, i: (i, k * nc + c))
    return pl.pallas_call(
        body, name="sc_mix_fwd", grid=(nc, s // tm), out_shape=jax.ShapeDtypeStruct((s, w), BF16),
        in_specs=[blk(0), blk(1), blk(2), blk(3), pl.BlockSpec((8, cw), lambda c, i: (0, c))],
        out_specs=pl.BlockSpec((tm, cw), lambda c, i: (i, c)),
        compiler_params=_cp(("arbitrary", "arbitrary")))(p, p, p, p, convw)


def _sc_mix_bwd(p, convw, dy):
    s, w4 = p.shape
    w = w4 // 4
    tm = GRID_W if GRID_W % 16 == 0 else 16 * GRID_W
    cw = _div(w, 512, LANE)
    nc = w // cw
    taps = _taps(SC_CONV)

    def body(p_ref, w_ref, dy_ref, dp_ref, dw_ref):
        @pl.when(pl.program_id(0) == 0)
        def _():
            dw_ref[...] = jnp.zeros_like(dw_ref)

        masks = _sc_masks(tm, cw)
        for c in range(nc):
            cols = [slice(k * w + c * cw, k * w + (c + 1) * cw) for k in range(4)]
            here = slice(c * cw, (c + 1) * cw)
            wv = w_ref[:, here]
            bv, cv, xv, gv = p_ref[:, cols[0]], p_ref[:, cols[1]], p_ref[:, cols[2]], p_ref[:, cols[3]]
            dyv = dy_ref[:, here]
            u = cv * xv
            z = _conv_fwd_val(u, wv, taps, masks)
            sg = _silu(gv)
            dp_ref[:, cols[0]] = (dyv * z * sg).astype(BF16)
            dp_ref[:, cols[3]] = (dyv * bv * z * _dsilu(gv)).astype(BF16)
            du, dws = _conv_bwd_val(u, dyv * bv * sg, wv, taps, masks)
            dp_ref[:, cols[1]] = (du * xv).astype(BF16)
            dp_ref[:, cols[2]] = (du * cv).astype(BF16)
            dw_ref[:, here] += jnp.concatenate(dws + [jnp.zeros((8 - len(dws), cw), F32)], axis=0)

    return pl.pallas_call(
        body, name="sc_mix_bwd", grid=(s // tm,),
        out_shape=(jax.ShapeDtypeStruct((s, w4), BF16), jax.ShapeDtypeStruct((8, w), F32)),
        in_specs=[pl.BlockSpec((tm, w4), lambda i: (i, 0)), pl.BlockSpec((8, w), lambda i: (0, 0)),
                  pl.BlockSpec((tm, w), lambda i: (i, 0))],
        out_specs=(pl.BlockSpec((tm, w4), lambda i: (i, 0)), pl.BlockSpec((8, w), lambda i: (0, 0))),
        compiler_params=_cp(("arbitrary",)))(p, convw, dy)


def _adamw_slots(w, slots, m, v, name):
    rows, cols = w.shape
    nslot = slots.shape[0]
    tm = _div(rows, max(16, (64 * 4096) // cols), 16)

    def body(w_ref, s_ref, m_ref, v_ref, g_ref, dl_ref, nm_ref, nv_ref):
        g = s_ref[0].astype(F32)
        for j in range(1, nslot):
            g = g + s_ref[j].astype(F32)
        dl, m2, v2 = _adam(w_ref[...], g, m_ref[...], v_ref[...])
        g_ref[...] = g
        dl_ref[...] = dl
        nm_ref[...] = m2
        nv_ref[...] = v2

    blk = pl.BlockSpec((tm, cols), lambda i: (i, 0))
    out = jax.ShapeDtypeStruct((rows, cols), F32)
    return pl.pallas_call(
        body, name=name, grid=(rows // tm,), out_shape=(out, out, out, out),
        in_specs=[blk, pl.BlockSpec((nslot, tm, cols), lambda i: (0, i, 0)), blk, blk],
        out_specs=(blk, blk, blk, blk), compiler_params=_cp(("arbitrary",)))(w, slots, m, v)


def _adamw_flat(w, g, m, v, name):
    rows, cols = w.shape
    tm = _div(rows, 1024, 8)

    def body(w_ref, g_ref, m_ref, v_ref, dl_ref, nm_ref, nv_ref):
        dl, m2, v2 = _adam(w_ref[...], g_ref[...], m_ref[...], v_ref[...])
        dl_ref[...] = dl
        nm_ref[...] = m2
        nv_ref[...] = v2

    blk = pl.BlockSpec((tm, cols), lambda i: (i, 0))
    out = jax.ShapeDtypeStruct((rows, cols), F32)
    return pl.pallas_call(
        body, name=name, grid=(rows // tm,), out_shape=(out, out, out), in_specs=[blk] * 4, out_specs=(blk, blk, blk),
        compiler_params=_cp(("arbitrary",)))(w, g, m, v)


def _pack(arrs):
    flat = [a.reshape(-1).astype(F32) for a in arrs]
    sizes = [f.shape[0] for f in flat]
    total = sum(sizes)
    padded = -(-total // 1024) * 1024
    if padded > total:
        flat.append(jnp.zeros((padded - total,), F32))
    offs, o = [], 0
    for s_ in sizes:
        offs.append(o)
        o += s_
    return jnp.concatenate(flat).reshape(padded // LANE, LANE), offs


def _unpack(flat2d, offs, shapes):
    flat = flat2d.reshape(-1)
    out = []
    for o, shp in zip(offs, shapes):
        n = 1
        for s_ in shp:
            n *= s_
        out.append(flat[o:o + n].reshape(shp))
    return out


def _pad_rows(a, rows):
    return jnp.concatenate([a, jnp.zeros((rows - a.shape[0],) + a.shape[1:], a.dtype)], axis=0)


def _to_col_major(t):
    n, ch = t.shape
    return t.reshape(n // GRID_W, GRID_W, ch).swapaxes(0, 1).reshape(n, ch)


def _to_raster(t):
    n, ch = t.shape
    return t.reshape(GRID_W, n // GRID_W, ch).swapaxes(0, 1).reshape(n, ch)


def _scg_layout(sc):
    r = sc.shape[0]
    hps = HEADS_PER_STEP
    nhg = DN_HEADS // hps
    t = sc.reshape(r, 4, 32)[:, :, :2 * DN_HEADS].reshape(r, 4, 2, nhg, hps)
    t = jnp.stack([t[:, 0], t[:, 2], t[:, 3]], axis=1)
    t = t.transpose(2, 3, 0, 1, 4).reshape(2 * nhg, r, 3 * hps)
    return jnp.concatenate([t, jnp.zeros((2 * nhg, r, LANE - 3 * hps), F32)], axis=2)


def _scg_layout_bwd(dscg):
    r = dscg.shape[1]
    hps = HEADS_PER_STEP
    nhg = DN_HEADS // hps
    t = dscg[:, :, :3 * hps].reshape(2, nhg, r, 3, hps).transpose(2, 3, 0, 1, 4).reshape(r, 3, 2 * DN_HEADS)
    t = jnp.concatenate([t, jnp.zeros((r, 3, 32 - 2 * DN_HEADS), F32)], axis=2) if 2 * DN_HEADS < 32 else t
    z = jnp.zeros((r, 1, 32), F32)
    return jnp.concatenate([t[:, 0:1], z, t[:, 1:2], t[:, 2:3]], axis=1).reshape(r, LANE)


def kernel(x, c, ctx, c_ctx, mod_w, mod_b, norm_w, ab_w_in, ab_qkv_conv, ab_a_log, ab_dt_bias, ab_dn_norm, ab_lru_conv_w, ab_lru_conv_b, ab_lru_w_r, ab_lru_b_r, ab_lru_w_i, ab_lru_b_i, ab_lru_lambda, ab_w_out, sc_w_in, sc_conv, sc_w_out, final_norm_w, loss_target, m_c_ctx, m_mod_w, m_mod_b, m_norm_w, m_ab_w_in, m_ab_qkv_conv, m_ab_a_log, m_ab_dt_bias, m_ab_dn_norm, m_ab_lru_conv_w, m_ab_lru_conv_b, m_ab_lru_w_r, m_ab_lru_b_r, m_ab_lru_w_i, m_ab_lru_b_i, m_ab_lru_lambda, m_ab_w_out, m_sc_w_in, m_sc_conv, m_sc_w_out, m_final_norm_w, v_c_ctx, v_mod_w, v_mod_b, v_norm_w, v_ab_w_in, v_ab_qkv_conv, v_ab_a_log, v_ab_dt_bias, v_ab_dn_norm, v_ab_lru_conv_w, v_ab_lru_conv_b, v_ab_lru_w_r, v_ab_lru_b_r, v_ab_lru_w_i, v_ab_lru_b_i, v_ab_lru_lambda, v_ab_w_out, v_sc_w_in, v_sc_conv, v_sc_w_out, v_final_norm_w):
    dd = _dims()
    d, dnw, lw, r = dd["d"], dd["dnw"], dd["lw"], dd["r"]
    off_lru, off_beta, ab_state, ab_in, nmain = dd["off_lru"], dd["off_beta"], dd["ab_state"], dd["ab_in"], dd["nmain"]
    hh2 = 2 * DN_HEADS
    me = _me()
    x = x[0]
    ctx = ctx[0]
    tgt = loss_target[0]

    small_sharded = [ab_qkv_conv[0], ab_lru_conv_w[0], ab_lru_b_r[0], ab_lru_b_i[0], ab_lru_lambda[0], sc_conv[0]]
    pk, offs0 = _pack([c] + small_sharded)
    (g0,) = _exchange([pk], "gather_small_in", scatter=False)
    parts = [_unpack(g0[j], offs0, [c.shape] + [a.shape for a in small_sharded]) for j in range(NDEV)]
    c_all = jnp.concatenate([p[0] for p in parts], axis=0)
    qkv_conv, lru_conv_w, lru_b_r, lru_b_i, lru_lam, sc_conv_f = [
        jnp.concatenate([p[i] for p in parts], axis=-1) for i in range(1, 7)]

    mcols = mod_w.shape[2]
    c16 = jnp.concatenate([c_all, c_ctx[None, :], jnp.zeros((7, d), F32)], axis=0)
    bias = lax.dynamic_slice_in_dim(mod_b, me * mcols, mcols, axis=1)[:, None, :]
    mod_loc = _mod_fwd(c16, mod_w, bias)
    (mod_g,) = _exchange([mod_loc], "gather_mod", scatter=False)
    mod_all = mod_g.transpose(1, 2, 0, 3).reshape(2, 16, NDEV * mcols)
    mine = lax.dynamic_slice_in_dim(mod_all, me, 1, axis=1)[:, 0]
    shift = [mine[l, None, 0:d] for l in range(2)]
    scale = [mine[l, None, d:2 * d] for l in range(2)]
    gate = [mine[l, None, 2 * d:3 * d] for l in range(2)]
    shift_c, scale_c = mod_all[0, 8, None, 0:d], mod_all[0, 8, None, d:2 * d]
    nw0, nw1 = norm_w[0][None, :], norm_w[1][None, :]

    (g_abin,) = _gather_two_level([ab_w_in[0].astype(BF16)], "gather_weights")
    shard = ab_in // NDEV

    def shard_cols(lo, hi):
        return [g_abin[j][:, max(lo, j * shard) - j * shard:min(hi, (j + 1) * shard) - j * shard]
                for j in range(NDEV) if max(lo, j * shard) < min(hi, (j + 1) * shard)]

    w_main = jnp.concatenate(shard_cols(0, off_beta) + shard_cols(ab_state, ab_in), axis=1)
    w_ba = jnp.concatenate(shard_cols(off_beta, ab_state) + [jnp.zeros((d, LANE - 2 * hh2), BF16)], axis=1)

    hn_c, rinv_c = _norm_fwd(ctx, nw0, shift_c, scale_c, "norm0_ctx")
    hn_x, rinv_x = _norm_fwd(x, nw0, shift[0], scale[0], "norm0_x")
    hn0 = jnp.concatenate([hn_c, hn_x], axis=0)
    proj, (g_about,) = _mm_nn(hn0, w_main[None], F32, "proj_main", rider=("gather", [ab_w_out[0].astype(BF16)]))
    w_about = g_about.reshape(dnw + lw, d)
    pba = _mm_nn(hn0, w_ba[None], F32, "proj_ba")

    convw8 = _pad_rows(qkv_conv, 8)
    qkv = _dn_pre_fwd(proj, convw8)
    zpad = jnp.zeros((32 - hh2,), F32)
    par = jnp.stack([jnp.concatenate([jnp.zeros((32,), F32), ab_dt_bias[0].reshape(-1), zpad, jnp.zeros((64,), F32)]),
                     jnp.concatenate([jnp.zeros((32,), F32), -jnp.exp(ab_a_log[0].reshape(-1)), zpad,
                                      jnp.zeros((64,), F32)])] + [jnp.zeros((LANE,), F32)] * 6)
    if hh2 < 32:
        pba_l = jnp.concatenate([pba[:, :hh2], jnp.zeros((r, 32 - hh2), F32), pba[:, hh2:2 * hh2],
                                 jnp.zeros((r, LANE - 32 - hh2), F32)], axis=1)
    else:
        pba_l = pba
    sc = _dn_gates_fwd(pba_l, par)
    scg = _scg_layout(sc)
    (o_dn, states), (g_scin,) = _delta_fwd(qkv, scg, rider=("gather", [sc_w_in[0].astype(BF16)]))

    xl = proj[:, off_lru:off_beta]
    xl_cat = jnp.concatenate([xl[:CTX_LEN], _to_col_major(xl[CTX_LEN:])], axis=0)
    lconv8 = jnp.concatenate([lru_conv_w, ab_lru_conv_b[0][None, :], jnp.zeros((3, lw), F32)], axis=0)
    xc = _lru_conv_fwd(xl_cat, lconv8)
    lvecs = jnp.stack([lru_b_r, lru_b_i, lru_lam] + [jnp.zeros_like(lru_lam)] * 5, axis=1)
    w_r, w_i = ab_lru_w_r[0], ab_lru_w_i[0]
    la_a, la_b = _lru_gates_fwd(xc, w_r, w_i, lvecs)
    h_dir, h_prev = _lru_scan_fwd(la_a, la_b)
    hsum = _to_raster(h_dir[0, CTX_LEN:] + h_dir[1, CTX_LEN:])

    dnn = ab_dn_norm[0][None, :]
    y0 = _mix_post_fwd(o_dn, hsum, proj, dnn)
    out0 = _mm_nn(y0, w_about[None], F32, "out0")

    x1, hn1, rinv1 = _resid_norm_fwd(x, out0, gate[0], nw1, shift[1], scale[1])
    p1, (g_scout,) = _mm_nn(hn1, g_scin, F32, "proj_sc", rider=("gather", [sc_w_out[0].astype(BF16)]))
    w_scout = g_scout.reshape(d, d)
    scw8 = _pad_rows(sc_conv_f, 8)
    y1 = _sc_mix_fwd(p1, scw8)
    out1 = _mm_nn(y1, w_scout[None], F32, "out1")

    fnw = final_norm_w[None, :]
    dx2, dout1, loss11, dfnw, dgate1 = _loss_head(x1, out1, gate[1], fnw, tgt)
    dy1 = _mm_nt(dout1, w_scout[None], F32, "dy1")
    gw_scout = _mm_tn(y1, dout1, BF16, "gw_scout")[0].reshape(NDEV, d // NDEV, d)
    dp1, dscw8 = _sc_mix_bwd(p1, scw8, dy1)
    dhn1 = _mm_nt(dp1, g_scin, F32, "dhn1", tk=2048)
    gw_scin = _mm_tn(hn1, dp1, BF16, "gw_scin", oj=NDEV)
    dx1, dout0, dshift1, dscale1, dnw1, dgate0 = _norm_bwd(
        x1, rinv1, [dhn1], nw1, scale[1], "norm1_bwd", dx_in=dx2, resid=(out0, gate[0]))

    dy0, sib_sc = _mm_nt(dout0, w_about[None], F32, "dy0", rider=("swap", [gw_scin, gw_scout]))
    gw_about = _mm_tn(y0, dout0, BF16, "gw_about")[0].reshape(NDEV, (dnw + lw) // NDEV, d)
    do, dh_r, dgates, ddnn = _mix_post_bwd(o_dn, hsum, proj, dnn, dy0)

    early = [gw_about, gw_scin, gw_scout]
    sib_e = _sibling_swap([gw_about], "swap_grads_mid") + sib_sc
    parts_e = [_chip_partial(o_, s_, f"chip_partial_early{i}") for i, (o_, s_) in enumerate(zip(early, sib_e))]
    (dq, dk, dv, dscg), (s_about, s_scin, s_scout) = _delta_bwd(qkv, scg, states, do, rider=("chipx", parts_e))
    dsc = _scg_layout_bwd(dscg)
    dpba, dpar = _dn_gates_bwd(pba_l, par, dsc)
    dpq, dcw_q = _dn_pre_bwd(proj, convw8, dq, 0)
    dpk, dcw_k = _dn_pre_bwd(proj, convw8, dk, 1)
    dpv, dcw_v = _dn_pre_bwd(proj, convw8, dv, 2)

    dh_cat = jnp.concatenate([jnp.zeros((CTX_LEN, lw), F32), _to_col_major(dh_r)], axis=0)
    d_a, d_b = _lru_scan_bwd(la_a, h_prev, dh_cat)
    dxc, dw_r, dw_i, dlvecs = _lru_gates_bwd(xc, w_r, w_i, lvecs, d_a, d_b)
    dxl_cat, dlconv8 = _lru_conv_bwd(xl_cat, lconv8, dxc)
    dxl = jnp.concatenate([dxl_cat[:CTX_LEN], _to_raster(dxl_cat[CTX_LEN:])], axis=0).astype(BF16)

    dproj = jnp.concatenate([dpq, dpk, dpv, dxl, dgates], axis=1)
    if hh2 < 32:
        dpba_w = jnp.concatenate([dpba[:, :hh2], dpba[:, 32:32 + hh2], jnp.zeros((r, LANE - 2 * hh2), BF16)], axis=1)
    else:
        dpba_w = dpba
    gw_main = _mm_tn(hn0, dproj, BF16, "gw_main")[0]
    gw_ba = _mm_tn(hn0, dpba_w, BF16, "gw_ba")[0]
    def grad_cols(lo, hi):
        pieces = []
        for a_, b_, src, off in ((0, off_beta, gw_main, 0), (off_beta, ab_state, gw_ba, -off_beta),
                                 (ab_state, ab_in, gw_main, off_beta - ab_state)):
            if max(lo, a_) < min(hi, b_):
                pieces.append(src[:, max(lo, a_) + off:min(hi, b_) + off])
        return pieces

    gw_abin = jnp.stack([jnp.concatenate(grad_cols(j * shard, (j + 1) * shard), axis=1) for j in range(NDEV)])
    (sib_l,) = _sibling_swap([gw_abin], "swap_grads_late")
    part_l = _chip_partial(gw_abin, sib_l, "chip_partial_late")
    dhn0_a, (s_abin,) = _mm_nt(dproj, w_main[None], F32, "dhn0_main", rider=("chipx", [part_l]))
    dhn0_b = _mm_nt(dpba_w, w_ba[None], F32, "dhn0_ba")

    nt = _div(SEQ, ROW_TILE // 2, 16)
    grad_x, dshift0, dscale0, dnw0_x = _norm_bwd(x, rinv_x, [dhn0_a, dhn0_b], nw0, scale[0], "norm0_bwd_x",
                                                 row_off=CTX_LEN // nt, dx_in=dx1)
    _, dshift_c, dscale_c, dnw0_c = _norm_bwd(ctx, rinv_c, [dhn0_a, dhn0_b], nw0, scale_c, "norm0_bwd_ctx")

    big = {}
    big["ab_w_in"] = _adamw_slots(ab_w_in[0], s_abin, m_ab_w_in[0], v_ab_w_in[0], "adamw_ab_w_in")
    big["ab_w_out"] = _adamw_slots(ab_w_out[0], s_about, m_ab_w_out[0], v_ab_w_out[0], "adamw_ab_w_out")
    big["sc_w_in"] = _adamw_slots(sc_w_in[0], s_scin, m_sc_w_in[0], v_sc_w_in[0], "adamw_sc_w_in")
    big["sc_w_out"] = _adamw_slots(sc_w_out[0], s_scout, m_sc_w_out[0], v_sc_w_out[0], "adamw_sc_w_out")

    dmod_own = jnp.stack([jnp.concatenate([dshift0, dscale0, dgate0], axis=1)[0],
                          jnp.concatenate([dshift1, dscale1, dgate1], axis=1)[0]])
    dmod_c = jnp.concatenate([dshift_c, dscale_c], axis=1)[0]
    g_qkv_conv = jnp.concatenate([dcw_q[:DN_CONV], dcw_k[:DN_CONV], dcw_v[:DN_CONV]], axis=1)
    small = [
        ("loss", loss11.reshape(1)),
        ("norm_w", jnp.concatenate([dnw0_x + dnw0_c, dnw1], axis=0)),
        ("dmod_own", dmod_own),
        ("dmod_c", dmod_c),
        ("qkv_conv", g_qkv_conv),
        ("a_log", dpar[1, 32:32 + hh2].reshape(2, DN_HEADS)),
        ("dt_bias", dpar[0, 32:32 + hh2].reshape(2, DN_HEADS)),
        ("dn_norm", ddnn[0]),
        ("lru_conv_w", dlconv8[:LRU_CONV]),
        ("lru_conv_b", dlconv8[LRU_CONV]),
        ("lru_w_r", dw_r),
        ("lru_w_i", dw_i),
        ("lru_b_r", dlvecs[:, 0]),
        ("lru_b_i", dlvecs[:, 1]),
        ("lru_lambda", dlvecs[:, 2]),
        ("sc_conv", dscw8[:SC_CONV]),
        ("final_norm_w", dfnw[0]),
    ]
    names = [n for n, _ in small]
    shapes = [a.shape for _, a in small]
    spk, soffs = _pack([a for _, a in small])
    (sg,) = _gather_two_level([spk], "gather_small_grads")
    tot = dict(zip(names, _unpack(_sum_slots(sg, "sum_small_grads"), soffs, shapes)))
    i_own = names.index("dmod_own")
    dmod_all = jnp.stack([_unpack(sg[j], soffs[i_own:i_own + 1], shapes[i_own:i_own + 1])[0] for j in range(NDEV)])

    dmc_full = jnp.concatenate([tot["dmod_c"], jnp.zeros((d,), F32)])
    rows9 = jnp.concatenate([dmod_all, jnp.stack([dmc_full, jnp.zeros((3 * d,), F32)])[None],
                             jnp.zeros((7, 2, 3 * d), F32)], axis=0)
    dm = lax.dynamic_slice_in_dim(rows9, me * mcols, mcols, axis=2).transpose(1, 0, 2)
    g_modw, dl_modw, nm_modw, nv_modw, gcc = _mod_bwd(c16, dm, mod_w, m_mod_w, v_mod_w)
    gcc_pk, _ = _pack([gcc[0, :, 0]])
    (gcc_g,) = _exchange([gcc_pk], "gather_c_ctx", scatter=False)
    gcc_sum = _sum_slots(gcc_g, "sum_c_ctx").reshape(-1)[:d]
    g_c_ctx_pre = gcc_sum

    grad_mod_b = tot["dmod_own"] + jnp.stack([dmc_full, jnp.zeros((3 * d,), F32)])

    def shard(a):
        wd = a.shape[-1] // NDEV
        return lax.dynamic_slice_in_dim(a, me * wd, wd, axis=a.ndim - 1)

    small_w = [
        ("c_ctx", c_ctx, None, m_c_ctx, v_c_ctx),
        ("mod_b", mod_b, grad_mod_b, m_mod_b, v_mod_b),
        ("norm_w", norm_w, tot["norm_w"], m_norm_w, v_norm_w),
        ("ab_qkv_conv", ab_qkv_conv, shard(tot["qkv_conv"])[None], m_ab_qkv_conv, v_ab_qkv_conv),
        ("ab_a_log", ab_a_log, tot["a_log"][None], m_ab_a_log, v_ab_a_log),
        ("ab_dt_bias", ab_dt_bias, tot["dt_bias"][None], m_ab_dt_bias, v_ab_dt_bias),
        ("ab_dn_norm", ab_dn_norm, tot["dn_norm"][None], m_ab_dn_norm, v_ab_dn_norm),
        ("ab_lru_conv_w", ab_lru_conv_w, shard(tot["lru_conv_w"])[None], m_ab_lru_conv_w, v_ab_lru_conv_w),
        ("ab_lru_conv_b", ab_lru_conv_b, tot["lru_conv_b"][None], m_ab_lru_conv_b, v_ab_lru_conv_b),
        ("ab_lru_w_r", ab_lru_w_r, tot["lru_w_r"][None], m_ab_lru_w_r, v_ab_lru_w_r),
        ("ab_lru_b_r", ab_lru_b_r, shard(tot["lru_b_r"])[None], m_ab_lru_b_r, v_ab_lru_b_r),
        ("ab_lru_w_i", ab_lru_w_i, tot["lru_w_i"][None], m_ab_lru_w_i, v_ab_lru_w_i),
        ("ab_lru_b_i", ab_lru_b_i, shard(tot["lru_b_i"])[None], m_ab_lru_b_i, v_ab_lru_b_i),
        ("ab_lru_lambda", ab_lru_lambda, shard(tot["lru_lambda"])[None], m_ab_lru_lambda, v_ab_lru_lambda),
        ("sc_conv", sc_conv, shard(tot["sc_conv"])[None], m_sc_conv, v_sc_conv),
        ("final_norm_w", final_norm_w, tot["final_norm_w"], m_final_norm_w, v_final_norm_w),
    ]
    sg_cc = jax.nn.sigmoid(c_ctx)
    g_c_ctx = g_c_ctx_pre * (sg_cc * (1.0 + c_ctx * (1.0 - sg_cc)))
    small_w[0] = ("c_ctx", c_ctx, g_c_ctx, m_c_ctx, v_c_ctx)
    wshapes = [w.shape for _, w, _, _, _ in small_w]
    wpk, woffs = _pack([w for _, w, _, _, _ in small_w])
    gpk, _ = _pack([g for _, _, g, _, _ in small_w])
    mpk, _ = _pack([m for _, _, _, m, _ in small_w])
    vpk, _ = _pack([v for _, _, _, _, v in small_w])
    dl_pk, nm_pk, nv_pk = _adamw_flat(wpk, gpk, mpk, vpk, "adamw_small")
    sm_g = {n: g for n, _, g, _, _ in small_w}
    sm_dl = dict(zip([n for n, *_ in small_w], _unpack(dl_pk, woffs, wshapes)))
    sm_nm = dict(zip([n for n, *_ in small_w], _unpack(nm_pk, woffs, wshapes)))
    sm_nv = dict(zip([n for n, *_ in small_w], _unpack(nv_pk, woffs, wshapes)))

    grads, deltas, new_m, new_v = {}, {}, {}, {}
    for n in sm_g:
        grads[n], deltas[n], new_m[n], new_v[n] = sm_g[n], sm_dl[n], sm_nm[n], sm_nv[n]
    grads["mod_w"], deltas["mod_w"], new_m["mod_w"], new_v["mod_w"] = g_modw, dl_modw, nm_modw, nv_modw
    for n in ("ab_w_in", "ab_w_out", "sc_w_in", "sc_w_out"):
        g, dl, m2, v2 = big[n]
        grads[n], deltas[n], new_m[n], new_v[n] = g[None], dl[None], m2[None], v2[None]

    order = ["c_ctx", "mod_w", "mod_b", "norm_w", "ab_w_in", "ab_qkv_conv", "ab_a_log", "ab_dt_bias", "ab_dn_norm",
             "ab_lru_conv_w", "ab_lru_conv_b", "ab_lru_w_r", "ab_lru_b_r", "ab_lru_w_i", "ab_lru_b_i", "ab_lru_lambda",
             "ab_w_out", "sc_w_in", "sc_conv", "sc_w_out", "final_norm_w"]
    loss = tot["loss"][0]
    return (loss, grad_x[None], *[grads[n] for n in order], *[deltas[n] for n in order],
            *[new_m[n] for n in order], *[new_v[n] for n in order])
```

```python
import functools

import jax
import jax.numpy as jnp
from jax import lax
from jax.experimental import pallas as pl
from jax.experimental.pallas import tpu as pltpu

F32 = jnp.float32
BF16 = jnp.bfloat16

NDEV = 8
D_MODEL = 4096
SEQ = 4096
CTX_LEN = 256
GRID_W = 64
EPS = 1e-6
DN_HEADS = 16
DN_HEAD_DIM = 128
DN_CONV = 4
DN_CHUNK = 64
LRU_WIDTH = 2048
LRU_BLOCKS = 16
LRU_CONV = 4
LRU_C = 8.0
SC_CONV = 3
ADAM_LR = 0.001
ADAM_B1 = 0.9
ADAM_B2 = 0.999
ADAM_EPS = 1e-08
ADAM_WD = 0.01
ADAM_STEP = 10

ROW_TILE = 256
HEADS_PER_STEP = 16
LANE = 128
VMEM_LIMIT = 56 * 1024 * 1024


def _dims():
    d = D_MODEL
    dnw = DN_HEADS * DN_HEAD_DIM
    lw = LRU_WIDTH
    off_lru = 3 * dnw
    off_beta = off_lru + lw
    ab_state = off_beta + 4 * DN_HEADS
    ab_in = ab_state + dnw + lw
    return dict(d=d, dnw=dnw, lw=lw, off_lru=off_lru, off_beta=off_beta, ab_state=ab_state, ab_in=ab_in,
                nmain=3 * dnw + lw + dnw + lw, r=CTX_LEN + SEQ, lbd=lw // LRU_BLOCKS)


def _div(n, pref, mult):
    best = None
    for t in range(mult, min(n, pref) + 1, mult):
        if n % t == 0:
            best = t
    return best if best is not None else n


def _cp(sem, vmem=VMEM_LIMIT):
    return pltpu.CompilerParams(dimension_semantics=sem, vmem_limit_bytes=vmem)


def _silu(z):
    return z * jax.nn.sigmoid(z)


def _dsilu(z):
    s = jax.nn.sigmoid(z)
    return s * (1.0 + z * (1.0 - s))


def _softplus(z):
    return jnp.maximum(z, 0.0) + jnp.log1p(jnp.exp(-jnp.abs(z)))


def _neg_expm1(z):
    series = -z * (1.0 + z * (0.5 + z * (1.0 / 6.0 + z * (1.0 / 24.0))))
    return jnp.where(z > -0.05, series, 1.0 - jnp.exp(z))


def _me():
    return 4 * lax.axis_index("x") + 2 * lax.axis_index("y") + lax.axis_index("c")


def _exchange(srcs, name, scatter):
    n = len(srcs)

    def body(*refs):
        src_refs, out_refs = refs[:n], refs[n:2 * n]
        send_sems, recv_sems, local_sems = refs[2 * n:]
        x, y, c = lax.axis_index("x"), lax.axis_index("y"), lax.axis_index("c")
        me = 4 * x + 2 * y + c
        copies = []
        for i in range(n):
            mine = src_refs[i].at[me] if scatter else src_refs[i]
            lc = pltpu.make_async_copy(mine, out_refs[i].at[me], local_sems.at[i])
            lc.start()
            copies.append(lc)
            for k in range(1, NDEV):
                px = 1 - x if (k >> 2) & 1 else x
                py = 1 - y if (k >> 1) & 1 else y
                pc = 1 - c if k & 1 else c
                peer = 4 * px + 2 * py + pc
                src = src_refs[i].at[peer] if scatter else src_refs[i]
                cp = pltpu.make_async_remote_copy(
                    src_ref=src, dst_ref=out_refs[i].at[me], send_sem=send_sems.at[i, k - 1],
                    recv_sem=recv_sems.at[i, k - 1], device_id=(px, py, pc), device_id_type=pl.DeviceIdType.MESH)
                cp.start()
                copies.append(cp)
        for cp in copies:
            cp.wait()

    any_spec = pl.BlockSpec(memory_space=pl.ANY)
    out_shape = tuple(jax.ShapeDtypeStruct(s.shape if scatter else (NDEV,) + s.shape, s.dtype) for s in srcs)
    outs = pl.pallas_call(
        body, name=name, out_shape=out_shape, in_specs=[any_spec] * n, out_specs=tuple([any_spec] * n),
        scratch_shapes=[pltpu.SemaphoreType.DMA((n, NDEV - 1)), pltpu.SemaphoreType.DMA((n, NDEV - 1)),
                        pltpu.SemaphoreType.DMA((n,))],
    )(*srcs)
    return list(outs)


def _flip(v, bit):
    return 1 - v if bit else v


def _gather_ops(src_refs, out_refs, send_sems, recv_sems, local_sems):
    n = len(src_refs)

    def setup():
        x, y, c = lax.axis_index("x"), lax.axis_index("y"), lax.axis_index("c")
        chips = [(_flip(x, k >> 1), _flip(y, k & 1)) for k in (1, 2, 3)]
        return x, y, c, chips

    def copy(i, sem, slot, to, src=None):
        return pltpu.make_async_remote_copy(
            src_ref=out_refs[i].at[slot] if src is None else src, dst_ref=out_refs[i].at[slot],
            send_sem=send_sems.at[i, sem], recv_sem=recv_sems.at[i, sem], device_id=to,
            device_id_type=pl.DeviceIdType.MESH)

    def own(i, me):
        return pltpu.make_async_copy(src_refs[i], out_refs[i].at[me], local_sems.at[i])

    def first(i, x, y, c, chips):
        me = 4 * x + 2 * y + c
        return [copy(i, 0, me, (x, y, 1 - c), src=src_refs[i])] + [
            copy(i, 1 + j, me, (px, py, c), src=src_refs[i]) for j, (px, py) in enumerate(chips)]

    def start():
        x, y, c, chips = setup()
        for i in range(n):
            own(i, 4 * x + 2 * y + c).start()
            for cp in first(i, x, y, c, chips):
                cp.start()

    def finish():
        x, y, c, chips = setup()
        here = (x, y, c)
        passed = []
        for i in range(n):
            for j, (px, py) in enumerate(chips):
                theirs = 4 * px + 2 * py + c
                copy(i, 1 + j, theirs, here).wait_recv()
                fwd = copy(i, 4 + j, theirs, (x, y, 1 - c))
                fwd.start()
                passed.append(fwd)
        for i in range(n):
            copy(i, 0, 4 * x + 2 * y + (1 - c), here).wait_recv()
            for j, (px, py) in enumerate(chips):
                copy(i, 4 + j, 4 * px + 2 * py + (1 - c), here).wait_recv()
        for i in range(n):
            for cp in first(i, x, y, c, chips):
                cp.wait_send()
        for cp in passed:
            cp.wait_send()
        for i in range(n):
            own(i, 4 * x + 2 * y + c).wait()

    return start, finish


def _chipx_ops(src_refs, out_refs, send_sems, recv_sems, local_sems):
    n = len(src_refs)

    def copies():
        x, y, c = lax.axis_index("x"), lax.axis_index("y"), lax.axis_index("c")
        mine = 2 * x + y
        res = []
        for i in range(n):
            res.append(pltpu.make_async_copy(src_refs[i].at[mine], out_refs[i].at[mine], local_sems.at[i]))
            for k in (1, 2, 3):
                px, py = _flip(x, k >> 1), _flip(y, k & 1)
                res.append(pltpu.make_async_remote_copy(
                    src_ref=src_refs[i].at[2 * px + py], dst_ref=out_refs[i].at[mine], send_sem=send_sems.at[i, k - 1],
                    recv_sem=recv_sems.at[i, k - 1], device_id=(px, py, c), device_id_type=pl.DeviceIdType.MESH))
        return res

    def start():
        for cp in copies():
            cp.start()

    def finish():
        for cp in copies():
            cp.wait()

    return start, finish


def _swap_ops(src_refs, out_refs, send_sems, recv_sems):
    n = len(src_refs)

    def copies():
        x, y, c = lax.axis_index("x"), lax.axis_index("y"), lax.axis_index("c")
        return [pltpu.make_async_remote_copy(
            src_ref=src_refs[i].at[2 * q + (1 - c)], dst_ref=out_refs[i].at[q], send_sem=send_sems.at[i, q],
            recv_sem=recv_sems.at[i, q], device_id=(x, y, 1 - c), device_id_type=pl.DeviceIdType.MESH)
            for i in range(n) for q in range(NDEV // 2)]

    def start():
        for cp in copies():
            cp.start()

    def finish():
        for cp in copies():
            cp.wait()

    return start, finish


def _rider(kind, arrays):
    n = len(arrays)
    any_spec = pl.BlockSpec(memory_space=pl.ANY)
    dma = pltpu.SemaphoreType.DMA
    if kind == "gather":
        out_shapes = [jax.ShapeDtypeStruct((NDEV,) + a.shape, a.dtype) for a in arrays]
        scratch, make = [dma((n, NDEV - 1)), dma((n, NDEV - 1)), dma((n,))], _gather_ops
    elif kind == "swap":
        out_shapes = [jax.ShapeDtypeStruct((NDEV // 2,) + a.shape[1:], a.dtype) for a in arrays]
        scratch, make = [dma((n, NDEV // 2)), dma((n, NDEV // 2))], _swap_ops
    else:
        out_shapes = [jax.ShapeDtypeStruct(a.shape, a.dtype) for a in arrays]
        scratch, make = [dma((n, NDEV // 2 - 1)), dma((n, NDEV // 2 - 1)), dma((n,))], _chipx_ops
    return list(arrays), [any_spec] * n, out_shapes, [any_spec] * n, scratch, make


def _exchange_call(kind, arrays, name):
    n = len(arrays)
    ins, in_specs, out_shapes, out_specs, scratch, make = _rider(kind, arrays)

    def body(*refs):
        start, finish = make(refs[:n], refs[n:2 * n], *refs[2 * n:])
        start()
        finish()

    outs = pl.pallas_call(body, name=name, out_shape=tuple(out_shapes), in_specs=in_specs, out_specs=tuple(out_specs),
                          scratch_shapes=scratch)(*ins)
    return list(outs)


def _gather_two_level(srcs, name):
    return _exchange_call("gather", srcs, name)


def _call_with_rider(body, rider, *, name, grid, out_shape, in_specs, out_specs, scratch_shapes, compiler_params,
                     operands):
    if rider is None:
        outs = pl.pallas_call(body, name=name, grid=grid, out_shape=tuple(out_shape), in_specs=list(in_specs),
                              out_specs=tuple(out_specs), scratch_shapes=list(scratch_shapes),
                              compiler_params=compiler_params)(*operands)
        return list(outs), []
    r_ins, r_in_specs, r_out_shapes, r_out_specs, r_scratch, make = _rider(*rider)
    ni, no, ns, nr = len(in_specs), len(out_shape), len(scratch_shapes), len(r_ins)

    def riding(*refs):
        ins, rins = refs[:ni], refs[ni:ni + nr]
        outs, routs = refs[ni + nr:ni + nr + no], refs[ni + nr + no:ni + nr + no + nr]
        scr, rscr = refs[ni + 2 * nr + no:ni + 2 * nr + no + ns], refs[ni + 2 * nr + no + ns:]
        start, finish = make(rins, routs, *rscr)
        ids = [pl.program_id(a) for a in range(len(grid))]
        first, last = ids[0] == 0, ids[0] == grid[0] - 1
        for a in range(1, len(grid)):
            first = jnp.logical_and(first, ids[a] == 0)
            last = jnp.logical_and(last, ids[a] == grid[a] - 1)

        @pl.when(first)
        def _():
            start()

        body(*ins, *outs, *scr)

        @pl.when(last)
        def _():
            finish()

    outs = pl.pallas_call(
        riding, name=name, grid=grid, out_shape=tuple(out_shape) + tuple(r_out_shapes),
        in_specs=list(in_specs) + r_in_specs, out_specs=tuple(out_specs) + tuple(r_out_specs),
        scratch_shapes=list(scratch_shapes) + r_scratch, compiler_params=compiler_params)(*operands, *r_ins)
    return list(outs[:no]), list(outs[no:])


def _sibling_swap(srcs, name):
    return _exchange_call("swap", srcs, name)


def _chip_partial(own, sib, name):
    _, rows, cols = own.shape
    tm = _div(rows, max(16, (128 * 4096) // cols), 16)

    def body(c_ref, o_ref, s_ref, p_ref):
        p_ref[0] = (o_ref[0].astype(F32) + s_ref[0].astype(F32)).astype(BF16)

    core = lax.axis_index("c").astype(jnp.int32).reshape(1)
    grid_spec = pltpu.PrefetchScalarGridSpec(
        num_scalar_prefetch=1, grid=(NDEV // 2, rows // tm),
        in_specs=[pl.BlockSpec((1, tm, cols), lambda q, i, c_ref: (2 * q + c_ref[0], i, 0)),
                  pl.BlockSpec((1, tm, cols), lambda q, i, c_ref: (q, i, 0))],
        out_specs=pl.BlockSpec((1, tm, cols), lambda q, i, c_ref: (q, i, 0)))
    return pl.pallas_call(
        body, name=name, grid_spec=grid_spec, out_shape=jax.ShapeDtypeStruct(sib.shape, BF16),
        compiler_params=_cp(("arbitrary", "arbitrary")))(core, own, sib)


def _chip_exchange(parts, name):
    return _exchange_call("chipx", parts, name)


def _sum_slots(g, name):
    _, rows, cols = g.shape
    tm = _div(rows, 1024, 8)

    def body(g_ref, o_ref):
        acc = g_ref[0]
        for j in range(1, NDEV):
            acc = acc + g_ref[j]
        o_ref[...] = acc

    return pl.pallas_call(
        body, name=name, grid=(rows // tm,), out_shape=jax.ShapeDtypeStruct((rows, cols), F32),
        in_specs=[pl.BlockSpec((NDEV, tm, cols), lambda i: (0, i, 0))],
        out_specs=pl.BlockSpec((tm, cols), lambda i: (i, 0)), compiler_params=_cp(("arbitrary",)))(g)


def _mm_nn(a, b3, out_dtype, name, tm=1152, tn=1024, rider=None):
    m_, k_ = a.shape
    j_, _, nj = b3.shape
    tm = _div(m_, tm, 16)
    tn = _div(nj, tn, LANE)
    per = nj // tn

    def body(a_ref, b_ref, o_ref):
        o_ref[...] = jnp.dot(a_ref[...], b_ref[0], preferred_element_type=F32).astype(o_ref.dtype)

    outs, rode = _call_with_rider(
        body, rider, name=name, grid=(j_ * per, m_ // tm),
        out_shape=(jax.ShapeDtypeStruct((m_, j_ * nj), out_dtype),),
        in_specs=[pl.BlockSpec((tm, k_), lambda n, m: (m, 0)),
                  pl.BlockSpec((1, k_, tn), lambda n, m: (n // per, 0, n % per))],
        out_specs=(pl.BlockSpec((tm, tn), lambda n, m: (m, n)),), scratch_shapes=[],
        compiler_params=_cp(("arbitrary", "arbitrary")), operands=(a, b3))
    return outs[0] if rider is None else (outs[0], rode)


def _mm_nt(a, b3, out_dtype, name, tm=1152, tn=1024, tk=2048, rider=None):
    m_, k_ = a.shape
    j_, n_, kj = b3.shape
    tm = _div(m_, tm, 16)
    tn = _div(n_, tn, LANE)
    tk = _div(kj, tk, LANE)
    perk = kj // tk
    nk = k_ // tk

    def body(a_ref, b_ref, o_ref, acc_ref):
        k = pl.program_id(2)

        @pl.when(k == 0)
        def _():
            acc_ref[...] = jnp.zeros_like(acc_ref)

        acc_ref[...] += lax.dot_general(a_ref[...], b_ref[0], (((1,), (1,)), ((), ())), preferred_element_type=F32)

        @pl.when(k == nk - 1)
        def _():
            o_ref[...] = acc_ref[...].astype(o_ref.dtype)

    outs, rode = _call_with_rider(
        body, rider, name=name, grid=(n_ // tn, m_ // tm, nk),
        out_shape=(jax.ShapeDtypeStruct((m_, n_), out_dtype),),
        in_specs=[pl.BlockSpec((tm, tk), lambda n, m, k: (m, k)),
                  pl.BlockSpec((1, tn, tk), lambda n, m, k: (k // perk, n, k % perk))],
        out_specs=(pl.BlockSpec((tm, tn), lambda n, m, k: (m, n)),),
        scratch_shapes=[pltpu.VMEM((tm, tn), F32)],
        compiler_params=_cp(("arbitrary", "arbitrary", "arbitrary")), operands=(a, b3))
    return outs[0] if rider is None else (outs[0], rode)


def _mm_tn(a, b, out_dtype, name, oj=1, tm=1024, tn=512):
    k_, m_ = a.shape
    _, n_ = b.shape
    nj = n_ // oj
    tm = _div(m_, tm, LANE)
    tn = _div(nj, tn, LANE)
    per = nj // tn

    def body(a_ref, b_ref, o_ref):
        o_ref[0] = lax.dot_general(a_ref[...], b_ref[...], (((0,), (0,)), ((), ())),
                                   preferred_element_type=F32).astype(o_ref.dtype)

    return pl.pallas_call(
        body, name=name, grid=(m_ // tm, n_ // tn), out_shape=jax.ShapeDtypeStruct((oj, m_, nj), out_dtype),
        in_specs=[pl.BlockSpec((k_, tm), lambda m, n: (0, m)), pl.BlockSpec((k_, tn), lambda m, n: (0, n))],
        out_specs=pl.BlockSpec((1, tm, tn), lambda m, n: (n // per, m, n % per)),
        compiler_params=_cp(("arbitrary", "arbitrary")))(a, b)


def _mod_fwd(c16, w, bias):
    _, d, cols = w.shape
    tn = _div(cols, 512, LANE)

    def body(c_ref, w_ref, b_ref, o_ref):
        sc = _silu(c_ref[...])
        o_ref[0] = jnp.dot(sc, w_ref[0], preferred_element_type=F32) + b_ref[0]

    return pl.pallas_call(
        body, name="mod_fwd", grid=(2, cols // tn), out_shape=jax.ShapeDtypeStruct((2, 16, cols), F32),
        in_specs=[pl.BlockSpec((16, d), lambda l, n: (0, 0)), pl.BlockSpec((1, d, tn), lambda l, n: (l, 0, n)),
                  pl.BlockSpec((1, 1, tn), lambda l, n: (l, 0, n))],
        out_specs=pl.BlockSpec((1, 16, tn), lambda l, n: (l, 0, n)),
        compiler_params=_cp(("arbitrary", "arbitrary")))(c16, w, bias)


def _adam(w, g, m, v):
    m2 = ADAM_B1 * m + (1.0 - ADAM_B1) * g
    v2 = ADAM_B2 * v + (1.0 - ADAM_B2) * (g * g)
    m_hat = m2 / (1.0 - ADAM_B1 ** ADAM_STEP)
    v_hat = v2 / (1.0 - ADAM_B2 ** ADAM_STEP)
    delta = -ADAM_LR * (m_hat / (jnp.sqrt(v_hat) + ADAM_EPS) + ADAM_WD * w)
    return delta, m2, v2


def _mod_bwd(c16, dm, w, m, v):
    _, d, cols = w.shape
    tm = _div(d, 256, 8)

    def body(c_ref, dm_ref, w_ref, m_ref, v_ref, g_ref, dl_ref, nm_ref, nv_ref, gc_ref):
        sc = _silu(c_ref[...])
        g = lax.dot_general(sc, dm_ref[0], (((0,), (0,)), ((), ())), preferred_element_type=F32,
                            precision=lax.Precision.HIGHEST)
        wv = w_ref[0]
        dl, m2, v2 = _adam(wv, g, m_ref[0], v_ref[0])
        g_ref[0] = g
        dl_ref[0] = dl
        nm_ref[0] = m2
        nv_ref[0] = v2
        gc_ref[0] = jnp.sum(wv * dm_ref[0, 8:9, :], axis=1, keepdims=True)

    big = pl.BlockSpec((1, tm, cols), lambda l, i: (l, i, 0))
    shp = jax.ShapeDtypeStruct(w.shape, F32)
    return pl.pallas_call(
        body, name="mod_bwd", grid=(2, d // tm),
        out_shape=(shp, shp, shp, shp, jax.ShapeDtypeStruct((2, d, 1), F32)),
        in_specs=[pl.BlockSpec((16, tm), lambda l, i: (0, i)), pl.BlockSpec((1, 16, cols), lambda l, i: (l, 0, 0)),
                  big, big, big],
        out_specs=(big, big, big, big, pl.BlockSpec((1, tm, 1), lambda l, i: (l, i, 0))),
        compiler_params=_cp(("arbitrary", "arbitrary")))(c16, dm, w, m, v)


def _row_spec(tm, d, off=0):
    return pl.BlockSpec((tm, d), lambda i: (i + off, 0))


def _vec_spec(d):
    return pl.BlockSpec((1, d), lambda i: (0, 0))


def _norm_fwd(x, nw, shift, scale, name):
    n, d = x.shape
    tm = _div(n, ROW_TILE, 16)

    def body(x_ref, nw_ref, sh_ref, sc_ref, hn_ref, r_ref):
        xv = x_ref[...]
        r = lax.rsqrt(jnp.mean(xv * xv, axis=1, keepdims=True) + EPS)
        hn_ref[...] = (xv * r * nw_ref[...] * (1.0 + sc_ref[...]) + sh_ref[...]).astype(BF16)
        r_ref[...] = r

    return pl.pallas_call(
        body, name=name, grid=(n // tm,),
        out_shape=(jax.ShapeDtypeStruct((n, d), BF16), jax.ShapeDtypeStruct((n, 1), F32)),
        in_specs=[_row_spec(tm, d), _vec_spec(d), _vec_spec(d), _vec_spec(d)],
        out_specs=(_row_spec(tm, d), _row_spec(tm, 1)), compiler_params=_cp(("arbitrary",)))(x, nw, shift, scale)


def _resid_norm_fwd(x, out0, gate, nw, shift, scale):
    n, d = x.shape
    tm = _div(n, ROW_TILE, 16)

    def body(x_ref, o_ref, g_ref, nw_ref, sh_ref, sc_ref, x1_ref, hn_ref, r_ref):
        xv = x_ref[...] + g_ref[...] * o_ref[...]
        r = lax.rsqrt(jnp.mean(xv * xv, axis=1, keepdims=True) + EPS)
        x1_ref[...] = xv
        hn_ref[...] = (xv * r * nw_ref[...] * (1.0 + sc_ref[...]) + sh_ref[...]).astype(BF16)
        r_ref[...] = r

    return pl.pallas_call(
        body, name="resid_norm1", grid=(n // tm,),
        out_shape=(jax.ShapeDtypeStruct((n, d), F32), jax.ShapeDtypeStruct((n, d), BF16),
                   jax.ShapeDtypeStruct((n, 1), F32)),
        in_specs=[_row_spec(tm, d), _row_spec(tm, d), _vec_spec(d), _vec_spec(d), _vec_spec(d), _vec_spec(d)],
        out_specs=(_row_spec(tm, d), _row_spec(tm, d), _row_spec(tm, 1)),
        compiler_params=_cp(("arbitrary",)))(x, out0, gate, nw, shift, scale)


def _loss_head(x1, out1, gate1, fnw, tgt):
    n, d = x1.shape
    tm = _div(n, ROW_TILE, 16)
    nsteps = n // tm

    def body(x_ref, o_ref, g_ref, w_ref, t_ref, dx_ref, do_ref, loss_ref, dw_ref, dg_ref, lacc_ref):
        i = pl.program_id(0)

        @pl.when(i == 0)
        def _():
            lacc_ref[...] = jnp.zeros_like(lacc_ref)
            dw_ref[...] = jnp.zeros_like(dw_ref)
            dg_ref[...] = jnp.zeros_like(dg_ref)

        ov = o_ref[...]
        gv = g_ref[...]
        wv = w_ref[...]
        xv = x_ref[...] + gv * ov
        r = lax.rsqrt(jnp.mean(xv * xv, axis=1, keepdims=True) + EPS)
        xh = xv * r
        diff = xh * wv - t_ref[...]
        lacc_ref[...] += jnp.sum(diff * diff, axis=0, keepdims=True)
        dy = diff * (1.0 / d)
        dw_ref[...] += jnp.sum(dy * xh, axis=0, keepdims=True)
        dxh = dy * wv
        dx = r * (dxh - xh * jnp.mean(xh * dxh, axis=1, keepdims=True))
        dx_ref[...] = dx
        dg_ref[...] += jnp.sum(dx * ov, axis=0, keepdims=True)
        do_ref[...] = (gv * dx).astype(BF16)

        @pl.when(i == nsteps - 1)
        def _():
            loss_ref[...] = (0.5 / d) * jnp.sum(lacc_ref[...], axis=1, keepdims=True)

    vec = jax.ShapeDtypeStruct((1, d), F32)
    return pl.pallas_call(
        body, name="loss_head", grid=(nsteps,),
        out_shape=(jax.ShapeDtypeStruct((n, d), F32), jax.ShapeDtypeStruct((n, d), BF16),
                   jax.ShapeDtypeStruct((1, 1), F32), vec, vec),
        in_specs=[_row_spec(tm, d), _row_spec(tm, d), _vec_spec(d), _vec_spec(d), _row_spec(tm, d)],
        out_specs=(_row_spec(tm, d), _row_spec(tm, d), pl.BlockSpec((1, 1), lambda i: (0, 0)), _vec_spec(d),
                   _vec_spec(d)),
        scratch_shapes=[pltpu.VMEM((1, d), F32)], compiler_params=_cp(("arbitrary",)))(x1, out1, gate1, fnw, tgt)


def _norm_bwd(x, rinv, dhns, nw, scale, name, row_off=0, dx_in=None, resid=None):
    n, d = x.shape
    tm = _div(n, ROW_TILE // 2, 16)
    nd = len(dhns)
    has_in = dx_in is not None
    has_res = resid is not None

    def body(*refs):
        it = iter(refs)
        x_ref, r_ref = next(it), next(it)
        dh_refs = [next(it) for _ in range(nd)]
        nw_ref, sc_ref = next(it), next(it)
        dxin_ref = next(it) if has_in else None
        o_ref, g_ref = (next(it), next(it)) if has_res else (None, None)
        dx_ref = next(it)
        do_ref = next(it) if has_res else None
        dsh_ref, dsc_ref, dnw_ref = next(it), next(it), next(it)
        dg_ref = next(it) if has_res else None
        i = pl.program_id(0)

        @pl.when(i == 0)
        def _():
            dsh_ref[...] = jnp.zeros_like(dsh_ref)
            dsc_ref[...] = jnp.zeros_like(dsc_ref)
            dnw_ref[...] = jnp.zeros_like(dnw_ref)
            if has_res:
                dg_ref[...] = jnp.zeros_like(dg_ref)

        dh = dh_refs[0][...]
        for rr in dh_refs[1:]:
            dh = dh + rr[...]
        r = r_ref[...]
        xh = x_ref[...] * r
        nwv = nw_ref[...]
        mod = 1.0 + sc_ref[...]
        dsh_ref[...] += jnp.sum(dh, axis=0, keepdims=True)
        dhx = dh * xh
        dsc_ref[...] += jnp.sum(dhx * nwv, axis=0, keepdims=True)
        dnw_ref[...] += jnp.sum(dhx * mod, axis=0, keepdims=True)
        dxh = dh * (nwv * mod)
        dx = r * (dxh - xh * jnp.mean(xh * dxh, axis=1, keepdims=True))
        if has_in:
            dx = dx + dxin_ref[...]
        dx_ref[...] = dx
        if has_res:
            dg_ref[...] += jnp.sum(dx * o_ref[...], axis=0, keepdims=True)
            do_ref[...] = (g_ref[...] * dx).astype(BF16)

    ins = [x, rinv] + list(dhns) + [nw, scale]
    in_specs = [_row_spec(tm, d), _row_spec(tm, 1)] + [_row_spec(tm, d, row_off)] * nd + [_vec_spec(d), _vec_spec(d)]
    if has_in:
        ins.append(dx_in)
        in_specs.append(_row_spec(tm, d))
    if has_res:
        ins += list(resid)
        in_specs += [_row_spec(tm, d), _vec_spec(d)]
    vec = jax.ShapeDtypeStruct((1, d), F32)
    out_shape = [jax.ShapeDtypeStruct((n, d), F32)]
    out_specs = [_row_spec(tm, d)]
    if has_res:
        out_shape.append(jax.ShapeDtypeStruct((n, d), BF16))
        out_specs.append(_row_spec(tm, d))
    out_shape += [vec, vec, vec]
    out_specs += [_vec_spec(d)] * 3
    if has_res:
        out_shape.append(vec)
        out_specs.append(_vec_spec(d))
    return pl.pallas_call(
        body, name=name, grid=(n // tm,), out_shape=tuple(out_shape), in_specs=in_specs, out_specs=tuple(out_specs),
        compiler_params=_cp(("arbitrary",)))(*ins)


def _seg_masks(rows, width, seg_bounds, offs):
    t = lax.broadcasted_iota(jnp.int32, (rows, width), 0)
    s0 = jnp.zeros_like(t)
    s1 = jnp.full_like(t, rows)
    for (a, b) in seg_bounds:
        inside = (t >= a) & (t < b)
        s0 = jnp.where(inside, a, s0)
        s1 = jnp.where(inside, b, s1)
    return {off: ((t + off >= s0) & (t + off < s1)) for off in offs}


def _shift_rows(v, off, valid):
    rows = v.shape[0]
    return jnp.where(valid, pltpu.roll(v, (-off) % rows, axis=0), 0.0)


def _conv_fwd_val(xv, wv, taps, masks):
    acc = None
    for tap, off in taps:
        xs = xv if off == 0 else _shift_rows(xv, off, masks[off])
        term = wv[tap:tap + 1, :] * xs
        acc = term if acc is None else acc + term
    return acc


def _conv_bwd_val(xv, dacc, wv, taps, masks):
    dx = None
    dws = []
    for tap, off in taps:
        xs = xv if off == 0 else _shift_rows(xv, off, masks[off])
        dws.append(jnp.sum(dacc * xs, axis=0, keepdims=True))
        ds = dacc if off == 0 else _shift_rows(dacc, -off, masks[-off])
        term = wv[tap:tap + 1, :] * ds
        dx = term if dx is None else dx + term
    return dx, dws


def _taps(k):
    left = k // 2
    return [(j, j - left) for j in range(k)]


def _seq_bounds():
    return [(0, CTX_LEN), (CTX_LEN, CTX_LEN + SEQ)]


def _dn_pre_fwd(proj, convw):
    dd = _dims()
    r = dd["r"]
    nstrip = 3 * DN_HEADS
    taps = _taps(DN_CONV)
    offs = [o for _, o in taps if o != 0]

    def body(x_ref, w_ref, o_ref):
        kind = pl.program_id(0) // DN_HEADS
        masks = _seg_masks(r, DN_HEAD_DIM, _seq_bounds(), offs)
        y = _silu(_conv_fwd_val(x_ref[...], w_ref[...], taps, masks))
        inv = lax.rsqrt(jnp.sum(y * y, axis=1, keepdims=True) + EPS)
        scale = jnp.where(kind == 0, inv * (DN_HEAD_DIM ** -0.5), jnp.where(kind == 1, inv, 1.0))
        o_ref[...] = y * scale

    return pl.pallas_call(
        body, name="dn_pre_fwd", grid=(nstrip,), out_shape=jax.ShapeDtypeStruct((r, 3 * dd["dnw"]), F32),
        in_specs=[pl.BlockSpec((r, DN_HEAD_DIM), lambda j: (0, j)), pl.BlockSpec((8, DN_HEAD_DIM), lambda j: (0, j))],
        out_specs=pl.BlockSpec((r, DN_HEAD_DIM), lambda j: (0, j)), compiler_params=_cp(("arbitrary",)))(proj, convw)


def _dn_pre_bwd(proj, convw, dqkv, kind):
    dd = _dims()
    r = dd["r"]
    taps = _taps(DN_CONV)
    offs = sorted({o for _, o in taps if o != 0} | {-o for _, o in taps if o != 0})
    cscale = DN_HEAD_DIM ** -0.5 if kind == 0 else 1.0

    def body(x_ref, w_ref, d_ref, dx_ref, dw_ref):
        masks = _seg_masks(r, DN_HEAD_DIM, _seq_bounds(), offs)
        xv = x_ref[...]
        wv = w_ref[...]
        acc = _conv_fwd_val(xv, wv, taps, masks)
        dout = d_ref[0] + d_ref[1]
        if kind == 2:
            dy = dout
        else:
            y = _silu(acc)
            inv = lax.rsqrt(jnp.sum(y * y, axis=1, keepdims=True) + EPS)
            dy = cscale * inv * (dout - y * (inv * inv) * jnp.sum(dout * y, axis=1, keepdims=True))
        dacc = dy * _dsilu(acc)
        dx, dws = _conv_bwd_val(xv, dacc, wv, taps, masks)
        dx_ref[...] = dx.astype(BF16)
        dw_ref[...] = jnp.concatenate(dws + [jnp.zeros((8 - len(dws), DN_HEAD_DIM), F32)], axis=0)

    base = kind * DN_HEADS
    return pl.pallas_call(
        body, name=f"dn_pre_bwd{kind}", grid=(DN_HEADS,),
        out_shape=(jax.ShapeDtypeStruct((r, dd["dnw"]), BF16), jax.ShapeDtypeStruct((8, dd["dnw"]), F32)),
        in_specs=[pl.BlockSpec((r, DN_HEAD_DIM), lambda j: (0, base + j)),
                  pl.BlockSpec((8, DN_HEAD_DIM), lambda j: (0, base + j)),
                  pl.BlockSpec((2, r, DN_HEAD_DIM), lambda j: (0, 0, j))],
        out_specs=(pl.BlockSpec((r, DN_HEAD_DIM), lambda j: (0, j)), pl.BlockSpec((8, DN_HEAD_DIM), lambda j: (0, j))),
        compiler_params=_cp(("arbitrary",)))(proj, convw, dqkv)


def _lru_conv_fwd(xl, convw):
    r, lw = xl.shape
    taps = _taps(LRU_CONV)
    offs = [o for _, o in taps if o != 0]

    def body(x_ref, w_ref, o_ref):
        masks = _seg_masks(r, LANE, _seq_bounds(), offs)
        wv = w_ref[...]
        o_ref[...] = _conv_fwd_val(x_ref[...], wv, taps, masks) + wv[4:5, :]

    return pl.pallas_call(
        body, name="lru_conv_fwd", grid=(lw // LANE,), out_shape=jax.ShapeDtypeStruct((r, lw), F32),
        in_specs=[pl.BlockSpec((r, LANE), lambda j: (0, j)), pl.BlockSpec((8, LANE), lambda j: (0, j))],
        out_specs=pl.BlockSpec((r, LANE), lambda j: (0, j)), compiler_params=_cp(("arbitrary",)))(xl, convw)


def _lru_conv_bwd(xl, convw, dxc):
    r, lw = xl.shape
    taps = _taps(LRU_CONV)
    offs = sorted({o for _, o in taps if o != 0} | {-o for _, o in taps if o != 0})

    def body(x_ref, w_ref, d_ref, dx_ref, dw_ref):
        masks = _seg_masks(r, LANE, _seq_bounds(), offs)
        dacc = d_ref[0] + d_ref[1]
        dx, dws = _conv_bwd_val(x_ref[...], dacc, w_ref[...], taps, masks)
        dx_ref[...] = dx
        db = jnp.sum(dacc, axis=0, keepdims=True)
        dw_ref[...] = jnp.concatenate(dws + [db, jnp.zeros((3, LANE), F32)], axis=0)

    return pl.pallas_call(
        body, name="lru_conv_bwd", grid=(lw // LANE,),
        out_shape=(jax.ShapeDtypeStruct((r, lw), F32), jax.ShapeDtypeStruct((8, lw), F32)),
        in_specs=[pl.BlockSpec((r, LANE), lambda j: (0, j)), pl.BlockSpec((8, LANE), lambda j: (0, j)),
                  pl.BlockSpec((2, r, LANE), lambda j: (0, 0, j))],
        out_specs=(pl.BlockSpec((r, LANE), lambda j: (0, j)), pl.BlockSpec((8, LANE), lambda j: (0, j))),
        compiler_params=_cp(("arbitrary",)))(xl, convw, dxc)


def _tri(rev_lane, n, m):
    r = lax.broadcasted_iota(jnp.int32, (n, n), 0)
    c = lax.broadcasted_iota(jnp.int32, (n, n), 1)
    return (r >= c).astype(F32), (r <= c).astype(F32)


def _hdot(a, b):
    return jnp.dot(a, b, preferred_element_type=F32, precision=lax.Precision.HIGHEST)


def _dn_gates_fwd(pba, par):
    r = pba.shape[0]
    nch = r // DN_CHUNK
    hh = 2 * DN_HEADS

    def body(x_ref, p_ref, o_ref):
        lo, up = _tri(None, DN_CHUNK, DN_CHUNK)
        ones = jnp.ones((DN_CHUNK, DN_CHUNK), F32)
        lane = lax.broadcasted_iota(jnp.int32, (DN_CHUNK, LANE), 1)
        group = lane // 32
        rev = (lane % 32) >= DN_HEADS
        dtb = p_ref[0:1, :]
        nea = p_ref[1:2, :]

        def step(i, carry):
            rows = pl.ds(pl.multiple_of(i * DN_CHUNK, DN_CHUNK), DN_CHUNK)
            xv = x_ref[rows, :]
            beta = jax.nn.sigmoid(xv)
            g = jnp.where(group == 1, nea * _softplus(xv + dtb), 0.0)
            gc = jnp.where(rev, _hdot(up, g), _hdot(lo, g))
            gt = _hdot(ones, g)
            out = jnp.where(group == 0, beta, jnp.where(group == 1, g, 0.0))
            out = out + jnp.where(group == 2, pltpu.roll(gc, 32, axis=1), 0.0)
            out = out + jnp.where(group == 3, pltpu.roll(gt, 64, axis=1), 0.0)
            o_ref[rows, :] = out
            return carry

        lax.fori_loop(0, nch, step, 0)

    return pl.pallas_call(body, name="dn_gates_fwd", out_shape=jax.ShapeDtypeStruct((r, LANE), F32),
                          compiler_params=_cp(None))(pba, par)


def _dn_gates_bwd(pba, par, dsc):
    r = pba.shape[0]
    nch = r // DN_CHUNK

    def body(x_ref, p_ref, d_ref, dx_ref, dp_ref, acc_ref):
        lo, up = _tri(None, DN_CHUNK, DN_CHUNK)
        ones = jnp.ones((DN_CHUNK, DN_CHUNK), F32)
        lane = lax.broadcasted_iota(jnp.int32, (DN_CHUNK, LANE), 1)
        group = lane // 32
        rev = (lane % 32) >= DN_HEADS
        dtb = p_ref[0:1, :]
        nea = p_ref[1:2, :]
        acc_ref[...] = jnp.zeros_like(acc_ref)

        def step(i, carry):
            rows = pl.ds(pl.multiple_of(i * DN_CHUNK, DN_CHUNK), DN_CHUNK)
            xv = x_ref[rows, :]
            dv = d_ref[rows, :]
            beta = jax.nn.sigmoid(xv)
            z = xv + dtb
            sp = _softplus(z)
            dgc = jnp.where(group == 1, pltpu.roll(dv, LANE - 32, axis=1), 0.0)
            dgt = jnp.where(group == 1, pltpu.roll(dv, LANE - 64, axis=1), 0.0)
            dg = jnp.where(rev, _hdot(lo, dgc), _hdot(up, dgc)) + _hdot(ones, dgt)
            dg = dg + jnp.where(group == 1, dv, 0.0)
            dal = dg * nea * jax.nn.sigmoid(z)
            dpb = dv * beta * (1.0 - beta)
            dx_ref[rows, :] = jnp.where(group == 0, dpb, jnp.where(group == 1, dal, 0.0)).astype(BF16)
            acc_ref[0:1, :] += jnp.sum(jnp.where(group == 1, dal, 0.0), axis=0, keepdims=True)
            acc_ref[1:2, :] += jnp.sum(jnp.where(group == 1, dg * nea * sp, 0.0), axis=0, keepdims=True)
            return carry

        lax.fori_loop(0, nch, step, 0)
        dp_ref[...] = acc_ref[...]

    return pl.pallas_call(
        body, name="dn_gates_bwd",
        out_shape=(jax.ShapeDtypeStruct((r, LANE), BF16), jax.ShapeDtypeStruct((8, LANE), F32)),
        scratch_shapes=[pltpu.VMEM((8, LANE), F32)], compiler_params=_cp(None))(pba, par, dsc)


_B_NN = (((2,), (1,)), ((0,), (0,)))
_B_NT = (((2,), (2,)), ((0,), (0,)))
_B_TN = (((1,), (1,)), ((0,), (0,)))


def _bdot(a, b, dims, exact=False):
    return lax.dot_general(a, b, dims, preferred_element_type=F32,
                           precision=lax.Precision.HIGH if exact else None)


@jax.custom_vjp
def _inv_unit_tri(l):
    n = l.shape[-1]
    ri = lax.broadcasted_iota(jnp.int32, (n, n), 0)
    ci = lax.broadcasted_iota(jnp.int32, (n, n), 1)
    x = jnp.broadcast_to((ri == ci).astype(F32), l.shape)
    inner = 1
    for size in (4, 16, n):
        so, si = size.bit_length() - 1, inner.bit_length() - 1
        same_outer = (ri >> so) == (ci >> so)
        same_inner = (ri >> si) == (ci >> si)
        part = jnp.where(same_outer, jnp.where(same_inner, 0.0, l), 0.0)
        m = part if inner == 1 else _bdot(x, part, _B_NN, exact=True)
        y = x
        for _ in range(size // inner - 1):
            y = x - _bdot(m, y, _B_NN, exact=True)
        x = y
        inner = size
    return x


def _inv_fwd(l):
    a = _inv_unit_tri(l)
    return a, a


def _inv_bwd(a, da):
    t = _bdot(a, da, _B_TN, exact=True)
    return (-_bdot(t, a, _B_NT, exact=True),)


_inv_unit_tri.defvjp(_inv_fwd, _inv_bwd)


def _delta_chunk(q4, k4, v4, sc, s4, rev):
    hps = HEADS_PER_STEP
    c = DN_CHUNK
    hd = DN_HEAD_DIM
    ri = lax.broadcasted_iota(jnp.int32, (c, c), 0)
    ci = lax.broadcasted_iota(jnp.int32, (c, c), 1)
    ahead = (ri - ci) * jnp.where(rev, -1, 1)
    incl = ahead >= 0
    strict = ahead > 0
    sct = sc.T

    def heads(x):
        return jnp.concatenate([x[None, :, j * hd:(j + 1) * hd] for j in range(hps)], axis=0)

    def col(off):
        return jnp.concatenate([sc[None, :, off + j:off + j + 1] for j in range(hps)], axis=0)

    def row(off):
        return jnp.concatenate([sct[None, off + j:off + j + 1, :] for j in range(hps)], axis=0)

    q, k, v = heads(q4), heads(k4), heads(v4)
    beta_c, gc_c, gt_c = col(0), col(hps), col(2 * hps)
    beta_r, gc_r = row(0), row(hps)
    dmat = jnp.where(incl, jnp.exp(jnp.where(incl, gc_c - gc_r, 0.0)), 0.0)
    kk = _bdot(k, k, _B_NT)
    qk = _bdot(q, k, _B_NT)
    a = _inv_unit_tri(jnp.where(strict, beta_c * kk * dmat, 0.0))
    w = _bdot(a * (beta_r * jnp.exp(gc_r)), k, _B_NN)
    u = _bdot(a * beta_r, v, _B_NN)
    u2 = u - _bdot(w, s4, _B_NN)
    o = jnp.exp(gc_c) * _bdot(q, s4, _B_NN) + _bdot(qk * dmat, u2, _B_NN)
    ke = k * jnp.exp(gt_c - gc_c)
    s_new = jnp.exp(gt_c[:, 0:1, :]) * s4 + _bdot(ke, u2, _B_TN)
    return jnp.concatenate([o[j] for j in range(hps)], axis=1), s_new


def _chunk_index(d, n, nctx, ntot):
    rev_idx = jnp.where(n < nctx, nctx - 1 - n, ntot - 1 - (n - nctx))
    return jnp.where(d == 0, n, rev_idx)


def _delta_fwd(qkv, scg, rider=None):
    dd = _dims()
    r, dnw = dd["r"], dd["dnw"]
    hps = HEADS_PER_STEP
    nhg = DN_HEADS // hps
    bw = hps * DN_HEAD_DIM
    nctx, ntot = CTX_LEN // DN_CHUNK, r // DN_CHUNK

    def body(q_ref, k_ref, v_ref, sc_ref, o_ref, ss_ref, s_ref):
        d, n = pl.program_id(0), pl.program_id(2)

        @pl.when(n == 0)
        def _():
            s_ref[...] = jnp.zeros_like(s_ref)

        s4 = s_ref[...]
        ss_ref[0, 0] = s4
        o4, s4n = _delta_chunk(q_ref[...], k_ref[...], v_ref[...], sc_ref[0], s4, d == 1)
        o_ref[0] = o4
        s_ref[...] = s4n

    def rows(d, g, n):
        return _chunk_index(d, n, nctx, ntot)

    return _call_with_rider(
        body, rider, name="delta_fwd", grid=(2, nhg, ntot),
        out_shape=(jax.ShapeDtypeStruct((2, r, dnw), F32),
                   jax.ShapeDtypeStruct((2, ntot, DN_HEADS, DN_HEAD_DIM, DN_HEAD_DIM), F32)),
        in_specs=[pl.BlockSpec((DN_CHUNK, bw), lambda d, g, n: (rows(d, g, n), g)),
                  pl.BlockSpec((DN_CHUNK, bw), lambda d, g, n: (rows(d, g, n), nhg + g)),
                  pl.BlockSpec((DN_CHUNK, bw), lambda d, g, n: (rows(d, g, n), 2 * nhg + g)),
                  pl.BlockSpec((1, DN_CHUNK, LANE), lambda d, g, n: (d * nhg + g, rows(d, g, n), 0))],
        out_specs=(pl.BlockSpec((1, DN_CHUNK, bw), lambda d, g, n: (d, rows(d, g, n), g)),
                   pl.BlockSpec((1, 1, hps, DN_HEAD_DIM, DN_HEAD_DIM), lambda d, g, n: (d, n, g, 0, 0))),
        scratch_shapes=[pltpu.VMEM((hps, DN_HEAD_DIM, DN_HEAD_DIM), F32)],
        compiler_params=_cp(("arbitrary", "arbitrary", "arbitrary")), operands=(qkv, qkv, qkv, scg))


def _delta_bwd(qkv, scg, states, do, rider=None):
    dd = _dims()
    r, dnw = dd["r"], dd["dnw"]
    hps = HEADS_PER_STEP
    nhg = DN_HEADS // hps
    bw = hps * DN_HEAD_DIM
    nctx, ntot = CTX_LEN // DN_CHUNK, r // DN_CHUNK

    def body(q_ref, k_ref, v_ref, sc_ref, ss_ref, do_ref, dq_ref, dk_ref, dv_ref, dsc_ref, ds_ref):
        d, n2 = pl.program_id(0), pl.program_id(2)

        @pl.when(n2 == 0)
        def _():
            ds_ref[...] = jnp.zeros_like(ds_ref)

        fn = functools.partial(_delta_chunk, rev=(d == 1))
        _, vjp = jax.vjp(fn, q_ref[...], k_ref[...], v_ref[...], sc_ref[0], ss_ref[0, 0])
        dq, dk, dv, dsc, ds = vjp((do_ref[...], ds_ref[...]))
        dq_ref[0] = dq
        dk_ref[0] = dk
        dv_ref[0] = dv
        dsc_ref[0] = dsc
        ds_ref[...] = ds

    def rows(d, g, n2):
        return _chunk_index(d, ntot - 1 - n2, nctx, ntot)

    blk = lambda col: pl.BlockSpec((DN_CHUNK, bw), lambda d, g, n2: (rows(d, g, n2), col(g)))
    oblk = pl.BlockSpec((1, DN_CHUNK, bw), lambda d, g, n2: (d, rows(d, g, n2), g))
    scblk = pl.BlockSpec((1, DN_CHUNK, LANE), lambda d, g, n2: (d * nhg + g, rows(d, g, n2), 0))
    big = jax.ShapeDtypeStruct((2, r, dnw), F32)
    return _call_with_rider(
        body, rider, name="delta_bwd", grid=(2, nhg, ntot),
        out_shape=(big, big, big, jax.ShapeDtypeStruct(scg.shape, F32)),
        in_specs=[blk(lambda g: g), blk(lambda g: nhg + g), blk(lambda g: 2 * nhg + g), scblk,
                  pl.BlockSpec((1, 1, hps, DN_HEAD_DIM, DN_HEAD_DIM), lambda d, g, n2: (d, ntot - 1 - n2, g, 0, 0)),
                  blk(lambda g: g)],
        out_specs=(oblk, oblk, oblk, scblk),
        scratch_shapes=[pltpu.VMEM((hps, DN_HEAD_DIM, DN_HEAD_DIM), F32)],
        compiler_params=_cp(("arbitrary", "arbitrary", "arbitrary")), operands=(qkv, qkv, qkv, scg, states, do))


def _lru_gate_vals(xn, wr, wi, br, bi, lam):
    r = jax.nn.sigmoid(jnp.dot(xn, wr, preferred_element_type=F32) + br)
    i = jax.nn.sigmoid(jnp.dot(xn, wi, preferred_element_type=F32) + bi)
    sp = _softplus(-lam)
    la = -LRU_C * r * sp
    a = jnp.exp(la)
    mm = jnp.sqrt(_neg_expm1(2.0 * la))
    return r, i, sp, la, a, mm


def _lru_gates_fwd(xc, w_r, w_i, vecs):
    r, lw = xc.shape
    bd = lw // LRU_BLOCKS
    tm = _div(r, ROW_TILE, 8)

    def body(x_ref, wr_ref, wi_ref, v_ref, a_ref, b_ref):
        for n in range(LRU_BLOCKS):
            sl = slice(n * bd, (n + 1) * bd)
            xn = x_ref[:, sl]
            _, i, _, _, a, mm = _lru_gate_vals(xn, wr_ref[0, n], wi_ref[0, n], v_ref[0, 0:1, sl], v_ref[0, 1:2, sl],
                                               v_ref[0, 2:3, sl])
            a_ref[0, :, sl] = a
            b_ref[0, :, sl] = mm * (i * xn)

    wspec = pl.BlockSpec((1, LRU_BLOCKS, bd, bd), lambda d, i: (d, 0, 0, 0))
    out = jax.ShapeDtypeStruct((2, r, lw), F32)
    ospec = pl.BlockSpec((1, tm, lw), lambda d, i: (d, i, 0))
    return pl.pallas_call(
        body, name="lru_gates_fwd", grid=(2, r // tm), out_shape=(out, out),
        in_specs=[pl.BlockSpec((tm, lw), lambda d, i: (i, 0)), wspec, wspec,
                  pl.BlockSpec((1, 8, lw), lambda d, i: (d, 0, 0))],
        out_specs=(ospec, ospec), compiler_params=_cp(("arbitrary", "arbitrary")))(xc, w_r, w_i, vecs)


def _lru_gates_bwd(xc, w_r, w_i, vecs, da, db):
    r, lw = xc.shape
    bd = lw // LRU_BLOCKS
    tm = _div(r, ROW_TILE, 8)

    def body(x_ref, wr_ref, wi_ref, v_ref, da_ref, db_ref, dx_ref, dwr_ref, dwi_ref, dv_ref):
        @pl.when(pl.program_id(1) == 0)
        def _():
            dwr_ref[...] = jnp.zeros_like(dwr_ref)
            dwi_ref[...] = jnp.zeros_like(dwi_ref)
            dv_ref[...] = jnp.zeros_like(dv_ref)

        for n in range(LRU_BLOCKS):
            sl = slice(n * bd, (n + 1) * bd)
            xn = x_ref[:, sl]
            wr, wi = wr_ref[0, n], wi_ref[0, n]
            lam = v_ref[0, 2:3, sl]
            rr, ii, sp, la, a, mm = _lru_gate_vals(xn, wr, wi, v_ref[0, 0:1, sl], v_ref[0, 1:2, sl], lam)
            dav, dbv = da_ref[0, :, sl], db_ref[0, :, sl]
            dmm = dbv * (ii * xn)
            dla = dav * a - dmm * (a * a) / jnp.maximum(mm, 1e-30)
            di = dbv * mm * xn
            dxn = dbv * mm * ii
            dr = dla * (-LRU_C * sp)
            dsp = jnp.sum(dla * (-LRU_C * rr), axis=0, keepdims=True)
            dpr = dr * rr * (1.0 - rr)
            dpi = di * ii * (1.0 - ii)
            dxn = dxn + lax.dot_general(dpr, wr, (((1,), (1,)), ((), ())), preferred_element_type=F32)
            dxn = dxn + lax.dot_general(dpi, wi, (((1,), (1,)), ((), ())), preferred_element_type=F32)
            dx_ref[0, :, sl] = dxn
            dwr_ref[0, n] += lax.dot_general(xn, dpr, (((0,), (0,)), ((), ())), preferred_element_type=F32)
            dwi_ref[0, n] += lax.dot_general(xn, dpi, (((0,), (0,)), ((), ())), preferred_element_type=F32)
            dv_ref[0, 0:1, sl] += jnp.sum(dpr, axis=0, keepdims=True)
            dv_ref[0, 1:2, sl] += jnp.sum(dpi, axis=0, keepdims=True)
            dv_ref[0, 2:3, sl] += dsp * (-jax.nn.sigmoid(-lam))

    wspec = pl.BlockSpec((1, LRU_BLOCKS, bd, bd), lambda d, i: (d, 0, 0, 0))
    vspec = pl.BlockSpec((1, 8, lw), lambda d, i: (d, 0, 0))
    big = pl.BlockSpec((1, tm, lw), lambda d, i: (d, i, 0))
    return pl.pallas_call(
        body, name="lru_gates_bwd", grid=(2, r // tm),
        out_shape=(jax.ShapeDtypeStruct((2, r, lw), F32), jax.ShapeDtypeStruct(w_r.shape, F32),
                   jax.ShapeDtypeStruct(w_i.shape, F32), jax.ShapeDtypeStruct((2, 8, lw), F32)),
        in_specs=[pl.BlockSpec((tm, lw), lambda d, i: (i, 0)), wspec, wspec, vspec, big, big],
        out_specs=(big, wspec, wspec, vspec), compiler_params=_cp(("arbitrary", "arbitrary")))(xc, w_r, w_i, vecs, da, db)


def _scan_block(d, i, nctx, ntot):
    return _chunk_index(d, i, nctx, ntot)


def _lru_scan_fwd(a, b):
    _, r, lw = a.shape
    rb = _div(CTX_LEN, ROW_TILE, 8)
    nctx, ntot, ntile = CTX_LEN // rb, r // rb, rb // 8

    def body(a_ref, b_ref, h_ref, hp_ref, c_ref):
        d, i = pl.program_id(0), pl.program_id(1)

        @pl.when(i == 0)
        def _():
            c_ref[...] = jnp.zeros_like(c_ref)

        def sweep(rev):
            order = list(range(7, -1, -1)) if rev else list(range(8))

            def step(s, carry):
                t0 = pl.multiple_of((ntile - 1 - s if rev else s) * 8, 8)
                at = a_ref[0, pl.ds(t0, 8), :]
                bt = b_ref[0, pl.ds(t0, 8), :]
                rows, prevs = [None] * 8, [None] * 8
                h = carry
                for j in order:
                    prevs[j] = h
                    h = at[j:j + 1, :] * h + bt[j:j + 1, :]
                    rows[j] = h
                h_ref[0, pl.ds(t0, 8), :] = jnp.concatenate(rows, axis=0)
                hp_ref[0, pl.ds(t0, 8), :] = jnp.concatenate(prevs, axis=0)
                return h

            c_ref[...] = lax.fori_loop(0, ntile, step, c_ref[...])

        @pl.when(d == 0)
        def _():
            sweep(False)

        @pl.when(d == 1)
        def _():
            sweep(True)

    blk = pl.BlockSpec((1, rb, lw), lambda d, i: (d, _scan_block(d, i, nctx, ntot), 0))
    out = jax.ShapeDtypeStruct((2, r, lw), F32)
    return pl.pallas_call(
        body, name="lru_scan_fwd", grid=(2, ntot), out_shape=(out, out), in_specs=[blk, blk], out_specs=(blk, blk),
        scratch_shapes=[pltpu.VMEM((1, lw), F32)], compiler_params=_cp(("arbitrary", "arbitrary")))(a, b)


def _lru_scan_bwd(a, hprev, dh):
    _, r, lw = a.shape
    rb = _div(CTX_LEN, ROW_TILE, 8)
    nctx, ntot, ntile = CTX_LEN // rb, r // rb, rb // 8

    def body(a_ref, hp_ref, dh_ref, da_ref, db_ref, c_ref):
        d, i = pl.program_id(0), pl.program_id(1)

        @pl.when(i == 0)
        def _():
            c_ref[...] = jnp.zeros_like(c_ref)

        def sweep(rev):
            order = list(range(8)) if rev else list(range(7, -1, -1))

            def step(s, carry):
                t0 = pl.multiple_of((s if rev else ntile - 1 - s) * 8, 8)
                at = a_ref[0, pl.ds(t0, 8), :]
                hpt = hp_ref[0, pl.ds(t0, 8), :]
                dht = dh_ref[pl.ds(t0, 8), :]
                das, dbs = [None] * 8, [None] * 8
                c = carry
                for j in order:
                    lam = dht[j:j + 1, :] + c
                    das[j] = lam * hpt[j:j + 1, :]
                    dbs[j] = lam
                    c = at[j:j + 1, :] * lam
                da_ref[0, pl.ds(t0, 8), :] = jnp.concatenate(das, axis=0)
                db_ref[0, pl.ds(t0, 8), :] = jnp.concatenate(dbs, axis=0)
                return c

            c_ref[...] = lax.fori_loop(0, ntile, step, c_ref[...])

        @pl.when(d == 0)
        def _():
            sweep(False)

        @pl.when(d == 1)
        def _():
            sweep(True)

    def rows(d, i):
        return _scan_block(d, ntot - 1 - i, nctx, ntot)

    blk = pl.BlockSpec((1, rb, lw), lambda d, i: (d, rows(d, i), 0))
    out = jax.ShapeDtypeStruct((2, r, lw), F32)
    return pl.pallas_call(
        body, name="lru_scan_bwd", grid=(2, ntot), out_shape=(out, out),
        in_specs=[blk, blk, pl.BlockSpec((rb, lw), lambda d, i: (rows(d, i), 0))], out_specs=(blk, blk),
        scratch_shapes=[pltpu.VMEM((1, lw), F32)], compiler_params=_cp(("arbitrary", "arbitrary")))(a, hprev, dh)


def _mix_post_fwd(o, hsum, proj, dnw_vec):
    dd = _dims()
    dnw, lw, nmain = dd["dnw"], dd["lw"], dd["nmain"]
    tm = _div(SEQ, ROW_TILE, 16)
    coff = CTX_LEN // tm
    gcol = (3 * dnw + lw) // dnw
    lcol = (3 * dnw + lw + dnw) // lw
    hd = DN_HEAD_DIM

    def body(o_ref, h_ref, g_ref, lg_ref, w_ref, y_ref):
        wv = w_ref[...]
        for hh in range(DN_HEADS):
            sl = slice(hh * hd, (hh + 1) * hd)
            os_ = o_ref[0, :, sl] + o_ref[1, :, sl]
            rr = lax.rsqrt(jnp.mean(os_ * os_, axis=1, keepdims=True) + EPS)
            y_ref[:, sl] = (os_ * rr * wv * _silu(g_ref[:, sl])).astype(BF16)
        y_ref[:, dnw:] = (h_ref[...] * _silu(lg_ref[...])).astype(BF16)

    return pl.pallas_call(
        body, name="mix_post_fwd", grid=(SEQ // tm,), out_shape=jax.ShapeDtypeStruct((SEQ, dnw + lw), BF16),
        in_specs=[pl.BlockSpec((2, tm, dnw), lambda i: (0, i + coff, 0)), pl.BlockSpec((tm, lw), lambda i: (i, 0)),
                  pl.BlockSpec((tm, dnw), lambda i: (i + coff, gcol)), pl.BlockSpec((tm, lw), lambda i: (i + coff, lcol)),
                  pl.BlockSpec((1, hd), lambda i: (0, 0))],
        out_specs=pl.BlockSpec((tm, dnw + lw), lambda i: (i, 0)), compiler_params=_cp(("arbitrary",)))(
            o, hsum, proj, proj, dnw_vec)


def _mix_post_bwd(o, hsum, proj, dnw_vec, dy):
    dd = _dims()
    dnw, lw, r = dd["dnw"], dd["lw"], dd["r"]
    tm = _div(SEQ, ROW_TILE, 16)
    coff = CTX_LEN // tm
    gcol = (3 * dnw + lw) // dnw
    lcol = (3 * dnw + lw + dnw) // lw
    hd = DN_HEAD_DIM

    def body(o_ref, h_ref, g_ref, lg_ref, w_ref, dy_ref, do_ref, dh_ref, dg_ref, dw_ref):
        i = pl.program_id(0)

        @pl.when(i == 0)
        def _():
            dw_ref[...] = jnp.zeros_like(dw_ref)

        @pl.when(i < coff)
        def _():
            do_ref[...] = jnp.zeros_like(do_ref)
            dg_ref[...] = jnp.zeros_like(dg_ref)

        @pl.when(i >= coff)
        def _():
            wv = w_ref[...]
            dwacc = jnp.zeros((1, hd), F32)
            for hh in range(DN_HEADS):
                sl = slice(hh * hd, (hh + 1) * hd)
                os_ = o_ref[0, :, sl] + o_ref[1, :, sl]
                rr = lax.rsqrt(jnp.mean(os_ * os_, axis=1, keepdims=True) + EPS)
                nn = os_ * rr
                gv = g_ref[:, sl]
                dyv = dy_ref[:, sl]
                don = dyv * _silu(gv)
                dg_ref[:, sl] = (dyv * nn * wv * _dsilu(gv)).astype(BF16)
                dwacc = dwacc + jnp.sum(don * nn, axis=0, keepdims=True)
                dn_ = don * wv
                do_ref[:, sl] = rr * (dn_ - nn * jnp.mean(nn * dn_, axis=1, keepdims=True))
            dw_ref[...] += dwacc
            lgv = lg_ref[...]
            dyl = dy_ref[:, dnw:]
            dh_ref[...] = dyl * _silu(lgv)
            dg_ref[:, dnw:] = (dyl * h_ref[...] * _dsilu(lgv)).astype(BF16)

    lat = lambda i: jnp.maximum(i - coff, 0)
    return pl.pallas_call(
        body, name="mix_post_bwd", grid=(r // tm,),
        out_shape=(jax.ShapeDtypeStruct((r, dnw), F32), jax.ShapeDtypeStruct((SEQ, lw), F32),
                   jax.ShapeDtypeStruct((r, dnw + lw), BF16), jax.ShapeDtypeStruct((1, hd), F32)),
        in_specs=[pl.BlockSpec((2, tm, dnw), lambda i: (0, i, 0)), pl.BlockSpec((tm, lw), lambda i: (lat(i), 0)),
                  pl.BlockSpec((tm, dnw), lambda i: (i, gcol)), pl.BlockSpec((tm, lw), lambda i: (i, lcol)),
                  pl.BlockSpec((1, hd), lambda i: (0, 0)), pl.BlockSpec((tm, dnw + lw), lambda i: (lat(i), 0))],
        out_specs=(pl.BlockSpec((tm, dnw), lambda i: (i, 0)), pl.BlockSpec((tm, lw), lambda i: (lat(i), 0)),
                   pl.BlockSpec((tm, dnw + lw), lambda i: (i, 0)), pl.BlockSpec((1, hd), lambda i: (0, 0))),
        compiler_params=_cp(("arbitrary",)))(o, hsum, proj, proj, dnw_vec, dy)


def _sc_masks(tm, cw):
    t = lax.broadcasted_iota(jnp.int32, (tm, cw), 0) % GRID_W
    return {-1: t != 0, 1: t != GRID_W - 1}


def _sc_mix_fwd(p, convw):
    s, w4 = p.shape
    w = w4 // 4
    tm = _div(s, 512, GRID_W if GRID_W % 16 == 0 else 16 * GRID_W)
    cw = _div(w, 512, LANE)
    nc = w // cw
    taps = _taps(SC_CONV)

    def body(b_ref, c_ref, x_ref, g_ref, w_ref, y_ref):
        masks = _sc_masks(tm, cw)
        u = c_ref[...] * x_ref[...]
        z = _conv_fwd_val(u, w_ref[...], taps, masks)
        y_ref[...] = (b_ref[...] * z * _silu(g_ref[...])).astype(BF16)

    blk = lambda k: pl.BlockSpec((tm, cw), lambda c, i: (i, k * nc + c))
    return pl.pallas_call(
        body, name="sc_mix_fwd", grid=(nc, s // tm), out_shape=jax.ShapeDtypeStruct((s, w), BF16),
        in_specs=[blk(0), blk(1), blk(2), blk(3), pl.BlockSpec((8, cw), lambda c, i: (0, c))],
        out_specs=pl.BlockSpec((tm, cw), lambda c, i: (i, c)),
        compiler_params=_cp(("arbitrary", "arbitrary")))(p, p, p, p, convw)


def _sc_mix_bwd(p, convw, dy):
    s, w4 = p.shape
    w = w4 // 4
    tm = GRID_W if GRID_W % 16 == 0 else 16 * GRID_W
    cw = _div(w, 512, LANE)
    nc = w // cw
    taps = _taps(SC_CONV)

    def body(p_ref, w_ref, dy_ref, dp_ref, dw_ref):
        @pl.when(pl.program_id(0) == 0)
        def _():
            dw_ref[...] = jnp.zeros_like(dw_ref)

        masks = _sc_masks(tm, cw)
        for c in range(nc):
            cols = [slice(k * w + c * cw, k * w + (c + 1) * cw) for k in range(4)]
            here = slice(c * cw, (c + 1) * cw)
            wv = w_ref[:, here]
            bv, cv, xv, gv = p_ref[:, cols[0]], p_ref[:, cols[1]], p_ref[:, cols[2]], p_ref[:, cols[3]]
            dyv = dy_ref[:, here]
            u = cv * xv
            z = _conv_fwd_val(u, wv, taps, masks)
            sg = _silu(gv)
            dp_ref[:, cols[0]] = (dyv * z * sg).astype(BF16)
            dp_ref[:, cols[3]] = (dyv * bv * z * _dsilu(gv)).astype(BF16)
            du, dws = _conv_bwd_val(u, dyv * bv * sg, wv, taps, masks)
            dp_ref[:, cols[1]] = (du * xv).astype(BF16)
            dp_ref[:, cols[2]] = (du * cv).astype(BF16)
            dw_ref[:, here] += jnp.concatenate(dws + [jnp.zeros((8 - len(dws), cw), F32)], axis=0)

    return pl.pallas_call(
        body, name="sc_mix_bwd", grid=(s // tm,),
        out_shape=(jax.ShapeDtypeStruct((s, w4), BF16), jax.ShapeDtypeStruct((8, w), F32)),
        in_specs=[pl.BlockSpec((tm, w4), lambda i: (i, 0)), pl.BlockSpec((8, w), lambda i: (0, 0)),
                  pl.BlockSpec((tm, w), lambda i: (i, 0))],
        out_specs=(pl.BlockSpec((tm, w4), lambda i: (i, 0)), pl.BlockSpec((8, w), lambda i: (0, 0))),
        compiler_params=_cp(("arbitrary",)))(p, convw, dy)


def _adamw_slots(w, slots, m, v, name):
    rows, cols = w.shape
    nslot = slots.shape[0]
    tm = _div(rows, max(16, (64 * 4096) // cols), 16)

    def body(w_ref, s_ref, m_ref, v_ref, g_ref, dl_ref, nm_ref, nv_ref):
        g = s_ref[0].astype(F32)
        for j in range(1, nslot):
            g = g + s_ref[j].astype(F32)
        dl, m2, v2 = _adam(w_ref[...], g, m_ref[...], v_ref[...])
        g_ref[...] = g
        dl_ref[...] = dl
        nm_ref[...] = m2
        nv_ref[...] = v2

    blk = pl.BlockSpec((tm, cols), lambda i: (i, 0))
    out = jax.ShapeDtypeStruct((rows, cols), F32)
    return pl.pallas_call(
        body, name=name, grid=(rows // tm,), out_shape=(out, out, out, out),
        in_specs=[blk, pl.BlockSpec((nslot, tm, cols), lambda i: (0, i, 0)), blk, blk],
        out_specs=(blk, blk, blk, blk), compiler_params=_cp(("arbitrary",)))(w, slots, m, v)


def _adamw_flat(w, g, m, v, name):
    rows, cols = w.shape
    tm = _div(rows, 1024, 8)

    def body(w_ref, g_ref, m_ref, v_ref, dl_ref, nm_ref, nv_ref):
        dl, m2, v2 = _adam(w_ref[...], g_ref[...], m_ref[...], v_ref[...])
        dl_ref[...] = dl
        nm_ref[...] = m2
        nv_ref[...] = v2

    blk = pl.BlockSpec((tm, cols), lambda i: (i, 0))
    out = jax.ShapeDtypeStruct((rows, cols), F32)
    return pl.pallas_call(
        body, name=name, grid=(rows // tm,), out_shape=(out, out, out), in_specs=[blk] * 4, out_specs=(blk, blk, blk),
        compiler_params=_cp(("arbitrary",)))(w, g, m, v)


def _pack(arrs):
    flat = [a.reshape(-1).astype(F32) for a in arrs]
    sizes = [f.shape[0] for f in flat]
    total = sum(sizes)
    padded = -(-total // 1024) * 1024
    if padded > total:
        flat.append(jnp.zeros((padded - total,), F32))
    offs, o = [], 0
    for s_ in sizes:
        offs.append(o)
        o += s_
    return jnp.concatenate(flat).reshape(padded // LANE, LANE), offs


def _unpack(flat2d, offs, shapes):
    flat = flat2d.reshape(-1)
    out = []
    for o, shp in zip(offs, shapes):
        n = 1
        for s_ in shp:
            n *= s_
        out.append(flat[o:o + n].reshape(shp))
    return out


def _pad_rows(a, rows):
    return jnp.concatenate([a, jnp.zeros((rows - a.shape[0],) + a.shape[1:], a.dtype)], axis=0)


def _to_col_major(t):
    n, ch = t.shape
    return t.reshape(n // GRID_W, GRID_W, ch).swapaxes(0, 1).reshape(n, ch)


def _to_raster(t):
    n, ch = t.shape
    return t.reshape(GRID_W, n // GRID_W, ch).swapaxes(0, 1).reshape(n, ch)


def _scg_layout(sc):
    r = sc.shape[0]
    hps = HEADS_PER_STEP
    nhg = DN_HEADS // hps
    t = sc.reshape(r, 4, 32)[:, :, :2 * DN_HEADS].reshape(r, 4, 2, nhg, hps)
    t = jnp.stack([t[:, 0], t[:, 2], t[:, 3]], axis=1)
    t = t.transpose(2, 3, 0, 1, 4).reshape(2 * nhg, r, 3 * hps)
    return jnp.concatenate([t, jnp.zeros((2 * nhg, r, LANE - 3 * hps), F32)], axis=2)


def _scg_layout_bwd(dscg):
    r = dscg.shape[1]
    hps = HEADS_PER_STEP
    nhg = DN_HEADS // hps
    t = dscg[:, :, :3 * hps].reshape(2, nhg, r, 3, hps).transpose(2, 3, 0, 1, 4).reshape(r, 3, 2 * DN_HEADS)
    t = jnp.concatenate([t, jnp.zeros((r, 3, 32 - 2 * DN_HEADS), F32)], axis=2) if 2 * DN_HEADS < 32 else t
    z = jnp.zeros((r, 1, 32), F32)
    return jnp.concatenate([t[:, 0:1], z, t[:, 1:2], t[:, 2:3]], axis=1).reshape(r, LANE)


def kernel(x, c, ctx, c_ctx, mod_w, mod_b, norm_w, ab_w_in, ab_qkv_conv, ab_a_log, ab_dt_bias, ab_dn_norm, ab_lru_conv_w, ab_lru_conv_b, ab_lru_w_r, ab_lru_b_r, ab_lru_w_i, ab_lru_b_i, ab_lru_lambda, ab_w_out, sc_w_in, sc_conv, sc_w_out, final_norm_w, loss_target, m_c_ctx, m_mod_w, m_mod_b, m_norm_w, m_ab_w_in, m_ab_qkv_conv, m_ab_a_log, m_ab_dt_bias, m_ab_dn_norm, m_ab_lru_conv_w, m_ab_lru_conv_b, m_ab_lru_w_r, m_ab_lru_b_r, m_ab_lru_w_i, m_ab_lru_b_i, m_ab_lru_lambda, m_ab_w_out, m_sc_w_in, m_sc_conv, m_sc_w_out, m_final_norm_w, v_c_ctx, v_mod_w, v_mod_b, v_norm_w, v_ab_w_in, v_ab_qkv_conv, v_ab_a_log, v_ab_dt_bias, v_ab_dn_norm, v_ab_lru_conv_w, v_ab_lru_conv_b, v_ab_lru_w_r, v_ab_lru_b_r, v_ab_lru_w_i, v_ab_lru_b_i, v_ab_lru_lambda, v_ab_w_out, v_sc_w_in, v_sc_conv, v_sc_w_out, v_final_norm_w):
    dd = _dims()
    d, dnw, lw, r = dd["d"], dd["dnw"], dd["lw"], dd["r"]
    off_lru, off_beta, ab_state, ab_in, nmain = dd["off_lru"], dd["off_beta"], dd["ab_state"], dd["ab_in"], dd["nmain"]
    hh2 = 2 * DN_HEADS
    me = _me()
    x = x[0]
    ctx = ctx[0]
    tgt = loss_target[0]

    small_sharded = [ab_qkv_conv[0], ab_lru_conv_w[0], ab_lru_b_r[0], ab_lru_b_i[0], ab_lru_lambda[0], sc_conv[0]]
    pk, offs0 = _pack([c] + small_sharded)
    (g0,) = _exchange([pk], "gather_small_in", scatter=False)
    parts = [_unpack(g0[j], offs0, [c.shape] + [a.shape for a in small_sharded]) for j in range(NDEV)]
    c_all = jnp.concatenate([p[0] for p in parts], axis=0)
    qkv_conv, lru_conv_w, lru_b_r, lru_b_i, lru_lam, sc_conv_f = [
        jnp.concatenate([p[i] for p in parts], axis=-1) for i in range(1, 7)]

    mcols = mod_w.shape[2]
    c16 = jnp.concatenate([c_all, c_ctx[None, :], jnp.zeros((7, d), F32)], axis=0)
    bias = lax.dynamic_slice_in_dim(mod_b, me * mcols, mcols, axis=1)[:, None, :]
    mod_loc = _mod_fwd(c16, mod_w, bias)
    (mod_g,) = _exchange([mod_loc], "gather_mod", scatter=False)
    mod_all = mod_g.transpose(1, 2, 0, 3).reshape(2, 16, NDEV * mcols)
    mine = lax.dynamic_slice_in_dim(mod_all, me, 1, axis=1)[:, 0]
    shift = [mine[l, None, 0:d] for l in range(2)]
    scale = [mine[l, None, d:2 * d] for l in range(2)]
    gate = [mine[l, None, 2 * d:3 * d] for l in range(2)]
    shift_c, scale_c = mod_all[0, 8, None, 0:d], mod_all[0, 8, None, d:2 * d]
    nw0, nw1 = norm_w[0][None, :], norm_w[1][None, :]

    (g_abin,) = _gather_two_level([ab_w_in[0].astype(BF16)], "gather_weights")
    shard = ab_in // NDEV

    def shard_cols(lo, hi):
        return [g_abin[j][:, max(lo, j * shard) - j * shard:min(hi, (j + 1) * shard) - j * shard]
                for j in range(NDEV) if max(lo, j * shard) < min(hi, (j + 1) * shard)]

    w_main = jnp.concatenate(shard_cols(0, off_beta) + shard_cols(ab_state, ab_in), axis=1)
    w_ba = jnp.concatenate(shard_cols(off_beta, ab_state) + [jnp.zeros((d, LANE - 2 * hh2), BF16)], axis=1)

    hn_c, rinv_c = _norm_fwd(ctx, nw0, shift_c, scale_c, "norm0_ctx")
    hn_x, rinv_x = _norm_fwd(x, nw0, shift[0], scale[0], "norm0_x")
    hn0 = jnp.concatenate([hn_c, hn_x], axis=0)
    proj, (g_about,) = _mm_nn(hn0, w_main[None], F32, "proj_main", rider=("gather", [ab_w_out[0].astype(BF16)]))
    w_about = g_about.reshape(dnw + lw, d)
    pba = _mm_nn(hn0, w_ba[None], F32, "proj_ba")

    convw8 = _pad_rows(qkv_conv, 8)
    qkv = _dn_pre_fwd(proj, convw8)
    zpad = jnp.zeros((32 - hh2,), F32)
    par = jnp.stack([jnp.concatenate([jnp.zeros((32,), F32), ab_dt_bias[0].reshape(-1), zpad, jnp.zeros((64,), F32)]),
                     jnp.concatenate([jnp.zeros((32,), F32), -jnp.exp(ab_a_log[0].reshape(-1)), zpad,
                                      jnp.zeros((64,), F32)])] + [jnp.zeros((LANE,), F32)] * 6)
    if hh2 < 32:
        pba_l = jnp.concatenate([pba[:, :hh2], jnp.zeros((r, 32 - hh2), F32), pba[:, hh2:2 * hh2],
                                 jnp.zeros((r, LANE - 32 - hh2), F32)], axis=1)
    else:
        pba_l = pba
    sc = _dn_gates_fwd(pba_l, par)
    scg = _scg_layout(sc)
    (o_dn, states), (g_scin,) = _delta_fwd(qkv, scg, rider=("gather", [sc_w_in[0].astype(BF16)]))

    xl = proj[:, off_lru:off_beta]
    xl_cat = jnp.concatenate([xl[:CTX_LEN], _to_col_major(xl[CTX_LEN:])], axis=0)
    lconv8 = jnp.concatenate([lru_conv_w, ab_lru_conv_b[0][None, :], jnp.zeros((3, lw), F32)], axis=0)
    xc = _lru_conv_fwd(xl_cat, lconv8)
    lvecs = jnp.stack([lru_b_r, lru_b_i, lru_lam] + [jnp.zeros_like(lru_lam)] * 5, axis=1)
    w_r, w_i = ab_lru_w_r[0], ab_lru_w_i[0]
    la_a, la_b = _lru_gates_fwd(xc, w_r, w_i, lvecs)
    h_dir, h_prev = _lru_scan_fwd(la_a, la_b)
    hsum = _to_raster(h_dir[0, CTX_LEN:] + h_dir[1, CTX_LEN:])

    dnn = ab_dn_norm[0][None, :]
    y0 = _mix_post_fwd(o_dn, hsum, proj, dnn)
    out0 = _mm_nn(y0, w_about[None], F32, "out0")

    x1, hn1, rinv1 = _resid_norm_fwd(x, out0, gate[0], nw1, shift[1], scale[1])
    p1, (g_scout,) = _mm_nn(hn1, g_scin, F32, "proj_sc", rider=("gather", [sc_w_out[0].astype(BF16)]))
    w_scout = g_scout.reshape(d, d)
    scw8 = _pad_rows(sc_conv_f, 8)
    y1 = _sc_mix_fwd(p1, scw8)
    out1 = _mm_nn(y1, w_scout[None], F32, "out1")

    fnw = final_norm_w[None, :]
    dx2, dout1, loss11, dfnw, dgate1 = _loss_head(x1, out1, gate[1], fnw, tgt)
    dy1 = _mm_nt(dout1, w_scout[None], F32, "dy1")
    gw_scout = _mm_tn(y1, dout1, BF16, "gw_scout")[0].reshape(NDEV, d // NDEV, d)
    dp1, dscw8 = _sc_mix_bwd(p1, scw8, dy1)
    dhn1 = _mm_nt(dp1, g_scin, F32, "dhn1", tk=2048)
    gw_scin = _mm_tn(hn1, dp1, BF16, "gw_scin", oj=NDEV)
    dx1, dout0, dshift1, dscale1, dnw1, dgate0 = _norm_bwd(
        x1, rinv1, [dhn1], nw1, scale[1], "norm1_bwd", dx_in=dx2, resid=(out0, gate[0]))

    dy0, sib_sc = _mm_nt(dout0, w_about[None], F32, "dy0", rider=("swap", [gw_scin, gw_scout]))
    gw_about = _mm_tn(y0, dout0, BF16, "gw_about")[0].reshape(NDEV, (dnw + lw) // NDEV, d)
    do, dh_r, dgates, ddnn = _mix_post_bwd(o_dn, hsum, proj, dnn, dy0)

    early = [gw_about, gw_scin, gw_scout]
    sib_e = _sibling_swap([gw_about], "swap_grads_mid") + sib_sc
    parts_e = [_chip_partial(o_, s_, f"chip_partial_early{i}") for i, (o_, s_) in enumerate(zip(early, sib_e))]
    (dq, dk, dv, dscg), (s_about, s_scin, s_scout) = _delta_bwd(qkv, scg, states, do, rider=("chipx", parts_e))
    dsc = _scg_layout_bwd(dscg)
    dpba, dpar = _dn_gates_bwd(pba_l, par, dsc)
    dpq, dcw_q = _dn_pre_bwd(proj, convw8, dq, 0)
    dpk, dcw_k = _dn_pre_bwd(proj, convw8, dk, 1)
    dpv, dcw_v = _dn_pre_bwd(proj, convw8, dv, 2)

    dh_cat = jnp.concatenate([jnp.zeros((CTX_LEN, lw), F32), _to_col_major(dh_r)], axis=0)
    d_a, d_b = _lru_scan_bwd(la_a, h_prev, dh_cat)
    dxc, dw_r, dw_i, dlvecs = _lru_gates_bwd(xc, w_r, w_i, lvecs, d_a, d_b)
    dxl_cat, dlconv8 = _lru_conv_bwd(xl_cat, lconv8, dxc)
    dxl = jnp.concatenate([dxl_cat[:CTX_LEN], _to_raster(dxl_cat[CTX_LEN:])], axis=0).astype(BF16)

    dproj = jnp.concatenate([dpq, dpk, dpv, dxl, dgates], axis=1)
    if hh2 < 32:
        dpba_w = jnp.concatenate([dpba[:, :hh2], dpba[:, 32:32 + hh2], jnp.zeros((r, LANE - 2 * hh2), BF16)], axis=1)
    else:
        dpba_w = dpba
    gw_main = _mm_tn(hn0, dproj, BF16, "gw_main")[0]
    gw_ba = _mm_tn(hn0, dpba_w, BF16, "gw_ba")[0]
    def grad_cols(lo, hi):
        pieces = []
        for a_, b_, src, off in ((0, off_beta, gw_main, 0), (off_beta, ab_state, gw_ba, -off_beta),
                                 (ab_state, ab_in, gw_main, off_beta - ab_state)):
            if max(lo, a_) < min(hi, b_):
                pieces.append(src[:, max(lo, a_) + off:min(hi, b_) + off])
        return pieces

    gw_abin = jnp.stack([jnp.concatenate(grad_cols(j * shard, (j + 1) * shard), axis=1) for j in range(NDEV)])
    (sib_l,) = _sibling_swap([gw_abin], "swap_grads_late")
    part_l = _chip_partial(gw_abin, sib_l, "chip_partial_late")
    dhn0_a, (s_abin,) = _mm_nt(dproj, w_main[None], F32, "dhn0_main", rider=("chipx", [part_l]))
    dhn0_b = _mm_nt(dpba_w, w_ba[None], F32, "dhn0_ba")

    nt = _div(SEQ, ROW_TILE // 2, 16)
    grad_x, dshift0, dscale0, dnw0_x = _norm_bwd(x, rinv_x, [dhn0_a, dhn0_b], nw0, scale[0], "norm0_bwd_x",
                                                 row_off=CTX_LEN // nt, dx_in=dx1)
    _, dshift_c, dscale_c, dnw0_c = _norm_bwd(ctx, rinv_c, [dhn0_a, dhn0_b], nw0, scale_c, "norm0_bwd_ctx")

    big = {}
    big["ab_w_in"] = _adamw_slots(ab_w_in[0], s_abin, m_ab_w_in[0], v_ab_w_in[0], "adamw_ab_w_in")
    big["ab_w_out"] = _adamw_slots(ab_w_out[0], s_about, m_ab_w_out[0], v_ab_w_out[0], "adamw_ab_w_out")
    big["sc_w_in"] = _adamw_slots(sc_w_in[0], s_scin, m_sc_w_in[0], v_sc_w_in[0], "adamw_sc_w_in")
    big["sc_w_out"] = _adamw_slots(sc_w_out[0], s_scout, m_sc_w_out[0], v_sc_w_out[0], "adamw_sc_w_out")

    dmod_own = jnp.stack([jnp.concatenate([dshift0, dscale0, dgate0], axis=1)[0],
                          jnp.concatenate([dshift1, dscale1, dgate1], axis=1)[0]])
    dmod_c = jnp.concatenate([dshift_c, dscale_c], axis=1)[0]
    g_qkv_conv = jnp.concatenate([dcw_q[:DN_CONV], dcw_k[:DN_CONV], dcw_v[:DN_CONV]], axis=1)
    small = [
        ("loss", loss11.reshape(1)),
        ("norm_w", jnp.concatenate([dnw0_x + dnw0_c, dnw1], axis=0)),
        ("dmod_own", dmod_own),
        ("dmod_c", dmod_c),
        ("qkv_conv", g_qkv_conv),
        ("a_log", dpar[1, 32:32 + hh2].reshape(2, DN_HEADS)),
        ("dt_bias", dpar[0, 32:32 + hh2].reshape(2, DN_HEADS)),
        ("dn_norm", ddnn[0]),
        ("lru_conv_w", dlconv8[:LRU_CONV]),
        ("lru_conv_b", dlconv8[LRU_CONV]),
        ("lru_w_r", dw_r),
        ("lru_w_i", dw_i),
        ("lru_b_r", dlvecs[:, 0]),
        ("lru_b_i", dlvecs[:, 1]),
        ("lru_lambda", dlvecs[:, 2]),
        ("sc_conv", dscw8[:SC_CONV]),
        ("final_norm_w", dfnw[0]),
    ]
    names = [n for n, _ in small]
    shapes = [a.shape for _, a in small]
    spk, soffs = _pack([a for _, a in small])
    (sg,) = _gather_two_level([spk], "gather_small_grads")
    tot = dict(zip(names, _unpack(_sum_slots(sg, "sum_small_grads"), soffs, shapes)))
    i_own = names.index("dmod_own")
    dmod_all = jnp.stack([_unpack(sg[j], soffs[i_own:i_own + 1], shapes[i_own:i_own + 1])[0] for j in range(NDEV)])

    dmc_full = jnp.concatenate([tot["dmod_c"], jnp.zeros((d,), F32)])
    rows9 = jnp.concatenate([dmod_all, jnp.stack([dmc_full, jnp.zeros((3 * d,), F32)])[None],
                             jnp.zeros((7, 2, 3 * d), F32)], axis=0)
    dm = lax.dynamic_slice_in_dim(rows9, me * mcols, mcols, axis=2).transpose(1, 0, 2)
    g_modw, dl_modw, nm_modw, nv_modw, gcc = _mod_bwd(c16, dm, mod_w, m_mod_w, v_mod_w)
    gcc_pk, _ = _pack([gcc[0, :, 0]])
    (gcc_g,) = _exchange([gcc_pk], "gather_c_ctx", scatter=False)
    gcc_sum = _sum_slots(gcc_g, "sum_c_ctx").reshape(-1)[:d]
    g_c_ctx_pre = gcc_sum

    grad_mod_b = tot["dmod_own"] + jnp.stack([dmc_full, jnp.zeros((3 * d,), F32)])

    def shard(a):
        wd = a.shape[-1] // NDEV
        return lax.dynamic_slice_in_dim(a, me * wd, wd, axis=a.ndim - 1)

    small_w = [
        ("c_ctx", c_ctx, None, m_c_ctx, v_c_ctx),
        ("mod_b", mod_b, grad_mod_b, m_mod_b, v_mod_b),
        ("norm_w", norm_w, tot["norm_w"], m_norm_w, v_norm_w),
        ("ab_qkv_conv", ab_qkv_conv, shard(tot["qkv_conv"])[None], m_ab_qkv_conv, v_ab_qkv_conv),
        ("ab_a_log", ab_a_log, tot["a_log"][None], m_ab_a_log, v_ab_a_log),
        ("ab_dt_bias", ab_dt_bias, tot["dt_bias"][None], m_ab_dt_bias, v_ab_dt_bias),
        ("ab_dn_norm", ab_dn_norm, tot["dn_norm"][None], m_ab_dn_norm, v_ab_dn_norm),
        ("ab_lru_conv_w", ab_lru_conv_w, shard(tot["lru_conv_w"])[None], m_ab_lru_conv_w, v_ab_lru_conv_w),
        ("ab_lru_conv_b", ab_lru_conv_b, tot["lru_conv_b"][None], m_ab_lru_conv_b, v_ab_lru_conv_b),
        ("ab_lru_w_r", ab_lru_w_r, tot["lru_w_r"][None], m_ab_lru_w_r, v_ab_lru_w_r),
        ("ab_lru_b_r", ab_lru_b_r, shard(tot["lru_b_r"])[None], m_ab_lru_b_r, v_ab_lru_b_r),
        ("ab_lru_w_i", ab_lru_w_i, tot["lru_w_i"][None], m_ab_lru_w_i, v_ab_lru_w_i),
        ("ab_lru_b_i", ab_lru_b_i, shard(tot["lru_b_i"])[None], m_ab_lru_b_i, v_ab_lru_b_i),
        ("ab_lru_lambda", ab_lru_lambda, shard(tot["lru_lambda"])[None], m_ab_lru_lambda, v_ab_lru_lambda),
        ("sc_conv", sc_conv, shard(tot["sc_conv"])[None], m_sc_conv, v_sc_conv),
        ("final_norm_w", final_norm_w, tot["final_norm_w"], m_final_norm_w, v_final_norm_w),
    ]
    sg_cc = jax.nn.sigmoid(c_ctx)
    g_c_ctx = g_c_ctx_pre * (sg_cc * (1.0 + c_ctx * (1.0 - sg_cc)))
    small_w[0] = ("c_ctx", c_ctx, g_c_ctx, m_c_ctx, v_c_ctx)
    wshapes = [w.shape for _, w, _, _, _ in small_w]
    wpk, woffs = _pack([w for _, w, _, _, _ in small_w])
    gpk, _ = _pack([g for _, _, g, _, _ in small_w])
    mpk, _ = _pack([m for _, _, _, m, _ in small_w])
    vpk, _ = _pack([v for _, _, _, _, v in small_w])
    dl_pk, nm_pk, nv_pk = _adamw_flat(wpk, gpk, mpk, vpk, "adamw_small")
    sm_g = {n: g for n, _, g, _, _ in small_w}
    sm_dl = dict(zip([n for n, *_ in small_w], _unpack(dl_pk, woffs, wshapes)))
    sm_nm = dict(zip([n for n, *_ in small_w], _unpack(nm_pk, woffs, wshapes)))
    sm_nv = dict(zip([n for n, *_ in small_w], _unpack(nv_pk, woffs, wshapes)))

    grads, deltas, new_m, new_v = {}, {}, {}, {}
    for n in sm_g:
        grads[n], deltas[n], new_m[n], new_v[n] = sm_g[n], sm_dl[n], sm_nm[n], sm_nv[n]
    grads["mod_w"], deltas["mod_w"], new_m["mod_w"], new_v["mod_w"] = g_modw, dl_modw, nm_modw, nv_modw
    for n in ("ab_w_in", "ab_w_out", "sc_w_in", "sc_w_out"):
        g, dl, m2, v2 = big[n]
        grads[n], deltas[n], new_m[n], new_v[n] = g[None], dl[None], m2[None], v2[None]

    order = ["c_ctx", "mod_w", "mod_b", "norm_w", "ab_w_in", "ab_qkv_conv", "ab_a_log", "ab_dt_bias", "ab_dn_norm",
             "ab_lru_conv_w", "ab_lru_conv_b", "ab_lru_w_r", "ab_lru_b_r", "ab_lru_w_i", "ab_lru_b_i", "ab_lru_lambda",
             "ab_w_out", "sc_w_in", "sc_conv", "sc_w_out", "final_norm_w"]
    loss = tot["loss"][0]
    return (loss, grad_x[None], *[grads[n] for n in order], *[deltas[n] for n in order],
            *[new_m[n] for n in order], *[new_v[n] for n in order])
```
